```python
import math
import jax
import jax.numpy as jnp
from jax import lax
import numpy as np


D_MODEL = 2048
BATCH = 8
SEQ = 2048
DEPTH = 4

GDN_HEADS = 8
GDN_HEAD_DIM = 128
GDN_WIDTH = GDN_HEADS * GDN_HEAD_DIM
GDN_CHUNK = 64
QKV_CONV = 5
N_DIR = 2
SGU_GROUPS = 8
SGU_GROUP_DIM = 128
SGU_WIDTH = SGU_GROUPS * SGU_GROUP_DIM
SGU_BLOCK = 128
D_FF = 5632
FFN_CONV = 3
NORM_EPS = 1e-6
IN_SIZES = (3 * GDN_WIDTH, GDN_WIDTH, N_DIR * GDN_HEADS, N_DIR * GDN_HEADS, SGU_WIDTH, SGU_WIDTH, D_MODEL, D_MODEL)
N_IN = 3 * GDN_WIDTH + GDN_WIDTH + 2 * N_DIR * GDN_HEADS + 2 * SGU_WIDTH + 2 * D_MODEL

kernel_name = 'hybrid_gdn_sgu_convglu_encoder'


def rms_norm(x, g):
    xf = x.astype(jnp.float32)
    y = xf * lax.rsqrt(jnp.mean(xf * xf, axis=-1, keepdims=True) + NORM_EPS)
    return (y * g.astype(jnp.float32)).astype(x.dtype)


def l2_normalize(x):
    xf = x.astype(jnp.float32)
    return xf * lax.rsqrt(jnp.sum(xf * xf, axis=-1, keepdims=True) + NORM_EPS)


def depthwise_conv_centred(x, w):
    pad = w.shape[0] // 2
    return lax.conv_general_dilated(
        x, w[:, None, :].astype(x.dtype), window_strides=(1,), padding=[(pad, pad)],
        dimension_numbers=('NWC', 'WIO', 'NWC'), feature_group_count=x.shape[-1])


def split_columns(p):
    offsets = []
    acc = 0
    for s in IN_SIZES[:-1]:
        acc += s
        offsets.append(acc)
    return jnp.split(p, offsets, axis=-1)


def gated_delta_rule_chunked(q, k, v, g, beta):
    bsz, nh, seqlen, dk = k.shape
    dv = v.shape[-1]
    c = GDN_CHUNK
    nc = seqlen // c
    q = q * (dk ** -0.5)
    blk = lambda t: t.reshape(bsz, nh, nc, c, *t.shape[3:])
    q, k, v, g, beta = blk(q), blk(k), blk(v), blk(g), blk(beta)
    g = jnp.cumsum(g, axis=-1)
    lower = jnp.tril(jnp.ones((c, c), dtype=bool))
    decay = jnp.where(lower, jnp.exp(jnp.where(lower, g[..., :, None] - g[..., None, :], 0.0)), 0.0)
    k_beta = k * beta[..., None]
    kk = jnp.einsum('bhnid,bhnjd->bhnij', k_beta, k) * decay
    rhs = jnp.concatenate([v * beta[..., None], k_beta * jnp.exp(g)[..., None]], axis=-1)
    sol = lax.linalg.triangular_solve(kk, rhs, left_side=True, lower=True, unit_diagonal=True)
    u, w = sol[..., :dv], sol[..., dv:]
    qk = jnp.einsum('bhnid,bhnjd->bhnij', q, k) * decay
    q_dec = q * jnp.exp(g)[..., None]
    k_dec = k * jnp.exp(g[..., -1:] - g)[..., None]
    chunk_dec = jnp.exp(g[..., -1])

    def step(state, xs):
        q_i, qk_i, u_i, w_i, k_i, d_i = xs
        v_new = u_i - jnp.einsum('bhck,bhkv->bhcv', w_i, state)
        o_i = jnp.einsum('bhck,bhkv->bhcv', q_i, state) + jnp.einsum('bhcs,bhsv->bhcv', qk_i, v_new)
        state = state * d_i[..., None, None] + jnp.einsum('bhck,bhcv->bhkv', k_i, v_new)
        return state, o_i

    xs = tuple(jnp.moveaxis(t, 2, 0) for t in (q_dec, qk, u, w, k_dec, chunk_dec))
    state0 = jnp.zeros((bsz, nh, dk, dv), jnp.float32)
    _, o = lax.scan(step, state0, xs)
    return jnp.moveaxis(o, 0, 2).reshape(bsz, nh, seqlen, dv)


def bidirectional_gated_deltanet(qkv, z, a, b, conv_w, a_log, dt_bias, norm_g):
    bsz, seqlen, _ = qkv.shape
    f32 = jnp.float32
    qkv_c = jax.nn.silu(depthwise_conv_centred(qkv, conv_w))
    q, k, v = jnp.split(qkv_c, 3, axis=-1)
    heads = lambda t: t.reshape(bsz, seqlen, GDN_HEADS, GDN_HEAD_DIM).transpose(0, 2, 1, 3)
    q, k, v = l2_normalize(heads(q)), l2_normalize(heads(k)), heads(v).astype(f32)
    a = a.astype(f32).reshape(bsz, seqlen, N_DIR, GDN_HEADS).transpose(0, 2, 3, 1)
    b = b.astype(f32).reshape(bsz, seqlen, N_DIR, GDN_HEADS).transpose(0, 2, 3, 1)
    g = -jnp.exp(a_log.astype(f32))[None, :, :, None] * jax.nn.softplus(a + dt_bias.astype(f32)[None, :, :, None])
    beta = jax.nn.sigmoid(b)
    rev = lambda t: jnp.flip(t, axis=2)
    o2 = gated_delta_rule_chunked(
        jnp.concatenate([q, rev(q)], axis=1),
        jnp.concatenate([k, rev(k)], axis=1),
        jnp.concatenate([v, rev(v)], axis=1),
        jnp.concatenate([g[:, 0], rev(g[:, 1])], axis=1),
        jnp.concatenate([beta[:, 0], rev(beta[:, 1])], axis=1))
    o = o2[:, :GDN_HEADS] + rev(o2[:, GDN_HEADS:])
    o = o.transpose(0, 2, 1, 3)
    zg = jax.nn.silu(z.astype(f32)).reshape(bsz, seqlen, GDN_HEADS, GDN_HEAD_DIM)
    y = rms_norm(o, norm_g) * zg
    return y.reshape(bsz, seqlen, GDN_WIDTH).astype(qkv.dtype)


def chunked_spatial_gating(u, v, ln_g, ln_b, w_s, b_s):
    bsz, seqlen, _ = v.shape
    vf = v.astype(jnp.float32)
    mu = jnp.mean(vf, axis=-1, keepdims=True)
    var = jnp.mean(jnp.square(vf - mu), axis=-1, keepdims=True)
    vn = ((vf - mu) * lax.rsqrt(var + NORM_EPS) * ln_g.astype(jnp.float32) + ln_b.astype(jnp.float32)).astype(v.dtype)
    vn = vn.reshape(bsz, seqlen // SGU_BLOCK, SGU_BLOCK, SGU_GROUPS, SGU_GROUP_DIM)
    s = jnp.einsum('gts,bnsgc->bntgc', w_s, vn) + b_s.T[None, None, :, :, None]
    return u * s.reshape(bsz, seqlen, SGU_WIDTH)


def _fwd_setup_inputs(seed: int = 0) -> dict:
    key = jax.random.key(seed)
    ks = jax.random.split(key, 24)
    f32 = jnp.float32
    nrm = lambda k, shape, scale: jax.random.normal(k, shape, f32) * scale
    x = nrm(ks[0], (BATCH, SEQ, D_MODEL), 1.0)
    norm_mix_g = 1.0 + nrm(ks[1], (DEPTH, D_MODEL), 0.02)
    w_in = nrm(ks[2], (DEPTH, D_MODEL, N_IN), D_MODEL ** -0.5)
    qkv_conv_w = nrm(ks[3], (DEPTH, QKV_CONV, 3 * GDN_WIDTH), QKV_CONV ** -0.5)
    a_log = jnp.log(jax.random.uniform(ks[4], (DEPTH, N_DIR, GDN_HEADS), f32, minval=1.0, maxval=16.0))
    dt = jnp.exp(jax.random.uniform(ks[5], (DEPTH, N_DIR, GDN_HEADS), f32, minval=math.log(1e-3), maxval=math.log(1e-1)))
    dt_bias = dt + jnp.log(-jnp.expm1(-dt))
    gdn_norm_g = 1.0 + nrm(ks[6], (DEPTH, GDN_HEAD_DIM), 0.02)
    w_branch_a = nrm(ks[7], (DEPTH, GDN_WIDTH, D_MODEL), GDN_WIDTH ** -0.5)
    sgu_ln_g = 1.0 + nrm(ks[8], (DEPTH, SGU_WIDTH), 0.02)
    sgu_ln_b = nrm(ks[9], (DEPTH, SGU_WIDTH), 0.02)
    sgu_w = nrm(ks[10], (DEPTH, SGU_GROUPS, SGU_BLOCK, SGU_BLOCK), SGU_BLOCK ** -0.5)
    sgu_b = 1.0 + nrm(ks[11], (DEPTH, SGU_GROUPS, SGU_BLOCK), 0.02)
    w_branch_b = nrm(ks[12], (DEPTH, SGU_WIDTH, D_MODEL), SGU_WIDTH ** -0.5)
    w_out = nrm(ks[13], (DEPTH, D_MODEL, D_MODEL), D_MODEL ** -0.5)
    norm_ffn_g = 1.0 + nrm(ks[14], (DEPTH, D_MODEL), 0.02)
    w_up = nrm(ks[15], (DEPTH, D_MODEL, 2 * D_FF), D_MODEL ** -0.5)
    ffn_conv_w = nrm(ks[16], (DEPTH, FFN_CONV, 2 * D_FF), FFN_CONV ** -0.5)
    ffn_conv_b = nrm(ks[17], (DEPTH, 2 * D_FF), 0.02)
    w_down = nrm(ks[18], (DEPTH, D_FF, D_MODEL), D_FF ** -0.5)
    final_norm_g = 1.0 + nrm(ks[19], (D_MODEL,), 0.02)
    return {'x': x, 'norm_mix_g': norm_mix_g, 'w_in': w_in, 'qkv_conv_w': qkv_conv_w,
            'a_log': a_log, 'dt_bias': dt_bias, 'gdn_norm_g': gdn_norm_g, 'w_branch_a': w_branch_a,
            'sgu_ln_g': sgu_ln_g, 'sgu_ln_b': sgu_ln_b, 'sgu_w': sgu_w, 'sgu_b': sgu_b,
            'w_branch_b': w_branch_b, 'w_out': w_out, 'norm_ffn_g': norm_ffn_g, 'w_up': w_up,
            'ffn_conv_w': ffn_conv_w, 'ffn_conv_b': ffn_conv_b, 'w_down': w_down,
            'final_norm_g': final_norm_g}


def _fwd_reference(x, norm_mix_g, w_in, qkv_conv_w, a_log, dt_bias, gdn_norm_g, w_branch_a,
              sgu_ln_g, sgu_ln_b, sgu_w, sgu_b, w_branch_b, w_out, norm_ffn_g, w_up,
              ffn_conv_w, ffn_conv_b, w_down, final_norm_g):
    for l in range(DEPTH):
        h = rms_norm(x, norm_mix_g[l])
        qkv, z, a, b, u, v, gate_a, gate_b = split_columns(h @ w_in[l])
        y_a = bidirectional_gated_deltanet(qkv, z, a, b, qkv_conv_w[l], a_log[l], dt_bias[l], gdn_norm_g[l])
        y_b = chunked_spatial_gating(jax.nn.gelu(u), jax.nn.gelu(v), sgu_ln_g[l], sgu_ln_b[l], sgu_w[l], sgu_b[l])
        merged = jax.nn.sigmoid(gate_a) * (y_a @ w_branch_a[l]) + jax.nn.sigmoid(gate_b) * (y_b @ w_branch_b[l])
        x = x + merged @ w_out[l]
        h = rms_norm(x, norm_ffn_g[l])
        up = depthwise_conv_centred(h @ w_up[l], ffn_conv_w[l]) + ffn_conv_b[l]
        c_gate, c_val = jnp.split(up, 2, axis=-1)
        x = x + (jax.nn.silu(c_gate) * c_val) @ w_down[l]
    return rms_norm(x, final_norm_g)


import jax as _jax
import jax.numpy as _jnp

TWIN_FORMAT = 'train_step'
FWD_PARAMS = ['x', 'norm_mix_g', 'w_in', 'qkv_conv_w', 'a_log', 'dt_bias', 'gdn_norm_g', 'w_branch_a', 'sgu_ln_g', 'sgu_ln_b', 'sgu_w', 'sgu_b', 'w_branch_b', 'w_out', 'norm_ffn_g', 'w_up', 'ffn_conv_w', 'ffn_conv_b', 'w_down', 'final_norm_g']
TWIN_WEIGHTS = ['norm_mix_g', 'w_in', 'qkv_conv_w', 'a_log', 'dt_bias', 'gdn_norm_g', 'w_branch_a', 'sgu_ln_g', 'sgu_ln_b', 'sgu_w', 'sgu_b', 'w_branch_b', 'w_out', 'norm_ffn_g', 'w_up', 'ffn_conv_w', 'ffn_conv_b', 'w_down', 'final_norm_g']
TWIN_DIFF_INPUT = 'x'
TWIN_INPUTS = ['x', 'norm_mix_g', 'w_in', 'qkv_conv_w', 'a_log', 'dt_bias', 'gdn_norm_g', 'w_branch_a', 'sgu_ln_g', 'sgu_ln_b', 'sgu_w', 'sgu_b', 'w_branch_b', 'w_out', 'norm_ffn_g', 'w_up', 'ffn_conv_w', 'ffn_conv_b', 'w_down', 'final_norm_g', 'loss_target', 'm_norm_mix_g', 'm_w_in', 'm_qkv_conv_w', 'm_a_log', 'm_dt_bias', 'm_gdn_norm_g', 'm_w_branch_a', 'm_sgu_ln_g', 'm_sgu_ln_b', 'm_sgu_w', 'm_sgu_b', 'm_w_branch_b', 'm_w_out', 'm_norm_ffn_g', 'm_w_up', 'm_ffn_conv_w', 'm_ffn_conv_b', 'm_w_down', 'm_final_norm_g', 'v_norm_mix_g', 'v_w_in', 'v_qkv_conv_w', 'v_a_log', 'v_dt_bias', 'v_gdn_norm_g', 'v_w_branch_a', 'v_sgu_ln_g', 'v_sgu_ln_b', 'v_sgu_w', 'v_sgu_b', 'v_w_branch_b', 'v_w_out', 'v_norm_ffn_g', 'v_w_up', 'v_ffn_conv_w', 'v_ffn_conv_b', 'v_w_down', 'v_final_norm_g']
TWIN_OUTPUTS = ['loss', 'grad_x', 'grad_norm_mix_g', 'grad_w_in', 'grad_qkv_conv_w', 'grad_a_log', 'grad_dt_bias', 'grad_gdn_norm_g', 'grad_w_branch_a', 'grad_sgu_ln_g', 'grad_sgu_ln_b', 'grad_sgu_w', 'grad_sgu_b', 'grad_w_branch_b', 'grad_w_out', 'grad_norm_ffn_g', 'grad_w_up', 'grad_ffn_conv_w', 'grad_ffn_conv_b', 'grad_w_down', 'grad_final_norm_g', 'delta_norm_mix_g', 'delta_w_in', 'delta_qkv_conv_w', 'delta_a_log', 'delta_dt_bias', 'delta_gdn_norm_g', 'delta_w_branch_a', 'delta_sgu_ln_g', 'delta_sgu_ln_b', 'delta_sgu_w', 'delta_sgu_b', 'delta_w_branch_b', 'delta_w_out', 'delta_norm_ffn_g', 'delta_w_up', 'delta_ffn_conv_w', 'delta_ffn_conv_b', 'delta_w_down', 'delta_final_norm_g', 'new_m_norm_mix_g', 'new_m_w_in', 'new_m_qkv_conv_w', 'new_m_a_log', 'new_m_dt_bias', 'new_m_gdn_norm_g', 'new_m_w_branch_a', 'new_m_sgu_ln_g', 'new_m_sgu_ln_b', 'new_m_sgu_w', 'new_m_sgu_b', 'new_m_w_branch_b', 'new_m_w_out', 'new_m_norm_ffn_g', 'new_m_w_up', 'new_m_ffn_conv_w', 'new_m_ffn_conv_b', 'new_m_w_down', 'new_m_final_norm_g', 'new_v_norm_mix_g', 'new_v_w_in', 'new_v_qkv_conv_w', 'new_v_a_log', 'new_v_dt_bias', 'new_v_gdn_norm_g', 'new_v_w_branch_a', 'new_v_sgu_ln_g', 'new_v_sgu_ln_b', 'new_v_sgu_w', 'new_v_sgu_b', 'new_v_w_branch_b', 'new_v_w_out', 'new_v_norm_ffn_g', 'new_v_w_up', 'new_v_ffn_conv_w', 'new_v_ffn_conv_b', 'new_v_w_down', 'new_v_final_norm_g']
TWIN_LEAF_KINDS = {'loss': 'loss', 'grad_x': 'grad_x', 'grad_norm_mix_g': 'grad_w', 'grad_w_in': 'grad_w', 'grad_qkv_conv_w': 'grad_w', 'grad_a_log': 'grad_w', 'grad_dt_bias': 'grad_w', 'grad_gdn_norm_g': 'grad_w', 'grad_w_branch_a': 'grad_w', 'grad_sgu_ln_g': 'grad_w', 'grad_sgu_ln_b': 'grad_w', 'grad_sgu_w': 'grad_w', 'grad_sgu_b': 'grad_w', 'grad_w_branch_b': 'grad_w', 'grad_w_out': 'grad_w', 'grad_norm_ffn_g': 'grad_w', 'grad_w_up': 'grad_w', 'grad_ffn_conv_w': 'grad_w', 'grad_ffn_conv_b': 'grad_w', 'grad_w_down': 'grad_w', 'grad_final_norm_g': 'grad_w', 'delta_norm_mix_g': 'delta_w', 'delta_w_in': 'delta_w', 'delta_qkv_conv_w': 'delta_w', 'delta_a_log': 'delta_w', 'delta_dt_bias': 'delta_w', 'delta_gdn_norm_g': 'delta_w', 'delta_w_branch_a': 'delta_w', 'delta_sgu_ln_g': 'delta_w', 'delta_sgu_ln_b': 'delta_w', 'delta_sgu_w': 'delta_w', 'delta_sgu_b': 'delta_w', 'delta_w_branch_b': 'delta_w', 'delta_w_out': 'delta_w', 'delta_norm_ffn_g': 'delta_w', 'delta_w_up': 'delta_w', 'delta_ffn_conv_w': 'delta_w', 'delta_ffn_conv_b': 'delta_w', 'delta_w_down': 'delta_w', 'delta_final_norm_g': 'delta_w', 'new_m_norm_mix_g': 'new_m', 'new_m_w_in': 'new_m', 'new_m_qkv_conv_w': 'new_m', 'new_m_a_log': 'new_m', 'new_m_dt_bias': 'new_m', 'new_m_gdn_norm_g': 'new_m', 'new_m_w_branch_a': 'new_m', 'new_m_sgu_ln_g': 'new_m', 'new_m_sgu_ln_b': 'new_m', 'new_m_sgu_w': 'new_m', 'new_m_sgu_b': 'new_m', 'new_m_w_branch_b': 'new_m', 'new_m_w_out': 'new_m', 'new_m_norm_ffn_g': 'new_m', 'new_m_w_up': 'new_m', 'new_m_ffn_conv_w': 'new_m', 'new_m_ffn_conv_b': 'new_m', 'new_m_w_down': 'new_m', 'new_m_final_norm_g': 'new_m', 'new_v_norm_mix_g': 'new_v', 'new_v_w_in': 'new_v', 'new_v_qkv_conv_w': 'new_v', 'new_v_a_log': 'new_v', 'new_v_dt_bias': 'new_v', 'new_v_gdn_norm_g': 'new_v', 'new_v_w_branch_a': 'new_v', 'new_v_sgu_ln_g': 'new_v', 'new_v_sgu_ln_b': 'new_v', 'new_v_sgu_w': 'new_v', 'new_v_sgu_b': 'new_v', 'new_v_w_branch_b': 'new_v', 'new_v_w_out': 'new_v', 'new_v_norm_ffn_g': 'new_v', 'new_v_w_up': 'new_v', 'new_v_ffn_conv_w': 'new_v', 'new_v_ffn_conv_b': 'new_v', 'new_v_w_down': 'new_v', 'new_v_final_norm_g': 'new_v'}


def _forward(args):
    return _fwd_reference(*[args[k] for k in FWD_PARAMS])


def _output_shape():
    out = _jax.eval_shape(lambda: _forward(_fwd_setup_inputs(0)))
    return out.shape, out.dtype

N_MICROBATCH = 1
ADAM_LR = 0.001
ADAM_B1 = 0.9
ADAM_B2 = 0.999
ADAM_EPS = 1e-08
ADAM_WD = 0.01
ADAM_STEP = 10
PER_EXAMPLE_BATCH_AXIS = {'x': 0, 'loss_target': 0}
SHARED_INPUTS = []
_WEIGHT_DTYPES = {'norm_mix_g': _jnp.float32, 'w_in': _jnp.float32, 'qkv_conv_w': _jnp.float32, 'a_log': _jnp.float32, 'dt_bias': _jnp.float32, 'gdn_norm_g': _jnp.float32, 'w_branch_a': _jnp.float32, 'sgu_ln_g': _jnp.float32, 'sgu_ln_b': _jnp.float32, 'sgu_w': _jnp.float32, 'sgu_b': _jnp.float32, 'w_branch_b': _jnp.float32, 'w_out': _jnp.float32, 'norm_ffn_g': _jnp.float32, 'w_up': _jnp.float32, 'ffn_conv_w': _jnp.float32, 'ffn_conv_b': _jnp.float32, 'w_down': _jnp.float32, 'final_norm_g': _jnp.float32}
MOMENT_SCALE = {'norm_mix_g': 5.278164e-02, 'w_in': 2.334247e-02, 'qkv_conv_w': 1.999060e-02, 'a_log': 1.038223e-01, 'dt_bias': 1.006755e-01, 'gdn_norm_g': 7.655599e-02, 'w_branch_a': 1.950852e-02, 'sgu_ln_g': 3.071719e-02, 'sgu_ln_b': 3.052499e-02, 'sgu_w': 3.005457e-02, 'sgu_b': 2.989544e-02, 'w_branch_b': 3.043765e-02, 'w_out': 3.615561e-02, 'norm_ffn_g': 4.523775e-02, 'w_up': 1.877952e-02, 'ffn_conv_w': 1.876036e-02, 'ffn_conv_b': 1.852794e-02, 'w_down': 3.061606e-02, 'final_norm_g': 8.008185e+00}


def _to_microbatches(a, axis):
    t = _jnp.moveaxis(a, axis, 0)
    t = t.reshape((N_MICROBATCH, t.shape[0] // N_MICROBATCH) + t.shape[1:])
    return _jnp.moveaxis(t, 1, axis + 1)


def setup_inputs(seed: int = 0) -> dict:
    inp = _fwd_setup_inputs(seed)
    key = _jax.random.fold_in(_jax.random.key(seed), 7919)
    shape, _ = _output_shape()
    out = dict(inp)
    out["loss_target"] = _jax.random.normal(_jax.random.fold_in(key, 0), shape, _jnp.float32)
    for i, name in enumerate(TWIN_WEIGHTS):
        w = inp[name].astype(_jnp.float32)
        if MOMENT_SCALE is None:
            s = _jnp.sqrt(_jnp.mean(_jnp.square(w)) + 1e-30)
        else:
            s = MOMENT_SCALE[name]
        km, kv = _jax.random.split(_jax.random.fold_in(key, i + 1))
        out[name] = w
        out["m_" + name] = s * _jax.random.normal(km, w.shape, _jnp.float32)
        out["v_" + name] = (s * s) * _jax.random.uniform(kv, w.shape, _jnp.float32, 0.5, 1.5)
    if N_MICROBATCH > 1:
        for name, axis in PER_EXAMPLE_BATCH_AXIS.items():
            out[name] = _to_microbatches(out[name], axis)
    return {'x': out['x'], 'norm_mix_g': out['norm_mix_g'], 'w_in': out['w_in'], 'qkv_conv_w': out['qkv_conv_w'], 'a_log': out['a_log'], 'dt_bias': out['dt_bias'], 'gdn_norm_g': out['gdn_norm_g'], 'w_branch_a': out['w_branch_a'], 'sgu_ln_g': out['sgu_ln_g'], 'sgu_ln_b': out['sgu_ln_b'], 'sgu_w': out['sgu_w'], 'sgu_b': out['sgu_b'], 'w_branch_b': out['w_branch_b'], 'w_out': out['w_out'], 'norm_ffn_g': out['norm_ffn_g'], 'w_up': out['w_up'], 'ffn_conv_w': out['ffn_conv_w'], 'ffn_conv_b': out['ffn_conv_b'], 'w_down': out['w_down'], 'final_norm_g': out['final_norm_g'], 'loss_target': out['loss_target'], 'm_norm_mix_g': out['m_norm_mix_g'], 'm_w_in': out['m_w_in'], 'm_qkv_conv_w': out['m_qkv_conv_w'], 'm_a_log': out['m_a_log'], 'm_dt_bias': out['m_dt_bias'], 'm_gdn_norm_g': out['m_gdn_norm_g'], 'm_w_branch_a': out['m_w_branch_a'], 'm_sgu_ln_g': out['m_sgu_ln_g'], 'm_sgu_ln_b': out['m_sgu_ln_b'], 'm_sgu_w': out['m_sgu_w'], 'm_sgu_b': out['m_sgu_b'], 'm_w_branch_b': out['m_w_branch_b'], 'm_w_out': out['m_w_out'], 'm_norm_ffn_g': out['m_norm_ffn_g'], 'm_w_up': out['m_w_up'], 'm_ffn_conv_w': out['m_ffn_conv_w'], 'm_ffn_conv_b': out['m_ffn_conv_b'], 'm_w_down': out['m_w_down'], 'm_final_norm_g': out['m_final_norm_g'], 'v_norm_mix_g': out['v_norm_mix_g'], 'v_w_in': out['v_w_in'], 'v_qkv_conv_w': out['v_qkv_conv_w'], 'v_a_log': out['v_a_log'], 'v_dt_bias': out['v_dt_bias'], 'v_gdn_norm_g': out['v_gdn_norm_g'], 'v_w_branch_a': out['v_w_branch_a'], 'v_sgu_ln_g': out['v_sgu_ln_g'], 'v_sgu_ln_b': out['v_sgu_ln_b'], 'v_sgu_w': out['v_sgu_w'], 'v_sgu_b': out['v_sgu_b'], 'v_w_branch_b': out['v_w_branch_b'], 'v_w_out': out['v_w_out'], 'v_norm_ffn_g': out['v_norm_ffn_g'], 'v_w_up': out['v_w_up'], 'v_ffn_conv_w': out['v_ffn_conv_w'], 'v_ffn_conv_b': out['v_ffn_conv_b'], 'v_w_down': out['v_w_down'], 'v_final_norm_g': out['v_final_norm_g']}


def _loss(weights, diff, rest, loss_target):
    with _jax.named_scope("forward"):
        args = {**rest, TWIN_DIFF_INPUT: diff, **{k: w.astype(_WEIGHT_DTYPES[k]) for k, w in weights.items()}}
        y = _forward(args)
    with _jax.named_scope("loss_head"):
        err = _jnp.square(y.astype(_jnp.float32) - loss_target)
        return 0.5 * _jnp.sum(_jnp.mean(err, axis=-1)) if err.ndim else 0.5 * err


def _adamw(w, g, m, v):
    m = ADAM_B1 * m + (1.0 - ADAM_B1) * g
    v = ADAM_B2 * v + (1.0 - ADAM_B2) * _jnp.square(g)
    m_hat = m / (1.0 - ADAM_B1 ** ADAM_STEP)
    v_hat = v / (1.0 - ADAM_B2 ** ADAM_STEP)
    delta = -ADAM_LR * (m_hat / (_jnp.sqrt(v_hat) + ADAM_EPS) + ADAM_WD * w)
    return delta, m, v


def reference(x, norm_mix_g, w_in, qkv_conv_w, a_log, dt_bias, gdn_norm_g, w_branch_a, sgu_ln_g, sgu_ln_b, sgu_w, sgu_b, w_branch_b, w_out, norm_ffn_g, w_up, ffn_conv_w, ffn_conv_b, w_down, final_norm_g, loss_target, m_norm_mix_g, m_w_in, m_qkv_conv_w, m_a_log, m_dt_bias, m_gdn_norm_g, m_w_branch_a, m_sgu_ln_g, m_sgu_ln_b, m_sgu_w, m_sgu_b, m_w_branch_b, m_w_out, m_norm_ffn_g, m_w_up, m_ffn_conv_w, m_ffn_conv_b, m_w_down, m_final_norm_g, v_norm_mix_g, v_w_in, v_qkv_conv_w, v_a_log, v_dt_bias, v_gdn_norm_g, v_w_branch_a, v_sgu_ln_g, v_sgu_ln_b, v_sgu_w, v_sgu_b, v_w_branch_b, v_w_out, v_norm_ffn_g, v_w_up, v_ffn_conv_w, v_ffn_conv_b, v_w_down, v_final_norm_g):
    given = dict(x=x, norm_mix_g=norm_mix_g, w_in=w_in, qkv_conv_w=qkv_conv_w, a_log=a_log, dt_bias=dt_bias, gdn_norm_g=gdn_norm_g, w_branch_a=w_branch_a, sgu_ln_g=sgu_ln_g, sgu_ln_b=sgu_ln_b, sgu_w=sgu_w, sgu_b=sgu_b, w_branch_b=w_branch_b, w_out=w_out, norm_ffn_g=norm_ffn_g, w_up=w_up, ffn_conv_w=ffn_conv_w, ffn_conv_b=ffn_conv_b, w_down=w_down, final_norm_g=final_norm_g, loss_target=loss_target, m_norm_mix_g=m_norm_mix_g, m_w_in=m_w_in, m_qkv_conv_w=m_qkv_conv_w, m_a_log=m_a_log, m_dt_bias=m_dt_bias, m_gdn_norm_g=m_gdn_norm_g, m_w_branch_a=m_w_branch_a, m_sgu_ln_g=m_sgu_ln_g, m_sgu_ln_b=m_sgu_ln_b, m_sgu_w=m_sgu_w, m_sgu_b=m_sgu_b, m_w_branch_b=m_w_branch_b, m_w_out=m_w_out, m_norm_ffn_g=m_norm_ffn_g, m_w_up=m_w_up, m_ffn_conv_w=m_ffn_conv_w, m_ffn_conv_b=m_ffn_conv_b, m_w_down=m_w_down, m_final_norm_g=m_final_norm_g, v_norm_mix_g=v_norm_mix_g, v_w_in=v_w_in, v_qkv_conv_w=v_qkv_conv_w, v_a_log=v_a_log, v_dt_bias=v_dt_bias, v_gdn_norm_g=v_gdn_norm_g, v_w_branch_a=v_w_branch_a, v_sgu_ln_g=v_sgu_ln_g, v_sgu_ln_b=v_sgu_ln_b, v_sgu_w=v_sgu_w, v_sgu_b=v_sgu_b, v_w_branch_b=v_w_branch_b, v_w_out=v_w_out, v_norm_ffn_g=v_norm_ffn_g, v_w_up=v_w_up, v_ffn_conv_w=v_ffn_conv_w, v_ffn_conv_b=v_ffn_conv_b, v_w_down=v_w_down, v_final_norm_g=v_final_norm_g)
    weights = {n: given[n] for n in TWIN_WEIGHTS}
    shared = {n: given[n] for n in SHARED_INPUTS}
    per_example = {n: given[n] for n in ['x']}
    grad_fn = _jax.value_and_grad(_loss, argnums=(0, 1))

    def one_microbatch(ex, loss_target):
        ex = dict(ex)
        diff = ex.pop(TWIN_DIFF_INPUT)
        return grad_fn(weights, diff, {**shared, **ex}, loss_target)

    if N_MICROBATCH == 1:
        loss, (grad_w, grad_x) = one_microbatch(per_example, given["loss_target"])
    else:
        def body(carry, xs):
            loss_sum, grad_sum = carry
            l_k, (gw_k, gx_k) = one_microbatch(xs[0], xs[1])
            with _jax.named_scope("update"):
                return (loss_sum + l_k, _jax.tree.map(_jnp.add, grad_sum, gw_k)), gx_k

        init = (_jnp.zeros((), _jnp.float32), _jax.tree.map(_jnp.zeros_like, weights))
        (loss, grad_w), grad_x = _jax.lax.scan(body, init, (per_example, given["loss_target"]))
    with _jax.named_scope("update"):
        delta_w, new_m, new_v = {}, {}, {}
        for n in TWIN_WEIGHTS:
            delta_w[n], new_m[n], new_v[n] = _adamw(weights[n], grad_w[n], given["m_" + n], given["v_" + n])
    return (loss, grad_x, *[grad_w[n] for n in TWIN_WEIGHTS], *[delta_w[n] for n in TWIN_WEIGHTS],
            *[new_m[n] for n in TWIN_WEIGHTS], *[new_v[n] for n in TWIN_WEIGHTS])
```

```python
import functools
import math

import jax
import jax.numpy as jnp
from jax import lax
from jax.experimental import pallas as pl
from jax.experimental.pallas import tpu as pltpu

F32 = jnp.float32
BF16 = jnp.bfloat16

D_MODEL = 2048
SEQ = 2048
DEPTH = 4
GDN_HEADS = 8
HEAD_DIM = 128
GDN_CHUNK = 64
QKV_CONV = 5
SGU_GROUPS = 8
SGU_DIM = 128
SGU_BLOCK = 128
D_FF = 5632
FFN_CONV = 3
NORM_EPS = 1e-6
N_CHIPS = 4

ADAM_LR = 0.001
ADAM_B1 = 0.9
ADAM_B2 = 0.999
ADAM_EPS = 1e-08
ADAM_WD = 0.01
ADAM_STEP = 10

LANES = 128
VMEM_LIMIT = 56 * 1024 * 1024
AB_PAD = LANES


def _gw():
    return GDN_HEADS * HEAD_DIM


def _sw():
    return SGU_GROUPS * SGU_DIM


def _n_all():
    return 4 * _gw() + 2 * _sw() + 2 * D_MODEL + AB_PAD


def _pick(n, target, align=LANES):
    if n <= target:
        return n
    best = None
    t = align
    while t <= target:
        if n % t == 0:
            best = t
        t += align
    assert best is not None, (n, target, align)
    return best


def _params(sem=None):
    return pltpu.CompilerParams(dimension_semantics=sem, vmem_limit_bytes=VMEM_LIMIT)


def _dot(a, b, dims, hi):
    if hi:
        return lax.dot_general(a.astype(F32), b.astype(F32), (dims, ((), ())), precision=lax.Precision.HIGHEST,
                               preferred_element_type=F32)
    return lax.dot_general(a.astype(BF16), b.astype(BF16), (dims, ((), ())), preferred_element_type=F32)


def _make_mm(hi):
    @jax.custom_vjp
    def mm(a, b):
        return _dot(a, b, ((1,), (0,)), hi)

    @jax.custom_vjp
    def mm_nt(a, b):
        return _dot(a, b, ((1,), (1,)), hi)

    @jax.custom_vjp
    def mm_tn(a, b):
        return _dot(a, b, ((0,), (0,)), hi)

    mm.defvjp(lambda a, b: (mm(a, b), (a, b)), lambda r, g: (mm_nt(g, r[1]), mm_tn(r[0], g)))
    mm_nt.defvjp(lambda a, b: (mm_nt(a, b), (a, b)), lambda r, g: (mm(g, r[1]), mm_tn(g, r[0])))
    mm_tn.defvjp(lambda a, b: (mm_tn(a, b), (a, b)), lambda r, g: (mm_nt(r[1], g), mm(r[0], g)))
    return mm, mm_nt, mm_tn


_mm, _mm_nt, _mm_tn = _make_mm(False)
_mmh, _mmh_nt, _mmh_tn = _make_mm(True)


def _shift_rows_raw(x, s):
    if s == 0:
        return x
    n = x.shape[0]
    rolled = pltpu.roll(x, (-s) % n, 0)
    t = lax.broadcasted_iota(jnp.int32, x.shape, 0)
    ok = (t + s >= 0) & (t + s < n)
    return jnp.where(ok, rolled, 0.0)


@functools.partial(jax.custom_vjp, nondiff_argnums=(1,))
def _shift_rows(x, s):
    return _shift_rows_raw(x, s)


_shift_rows.defvjp(lambda x, s: (_shift_rows_raw(x, s), None), lambda s, _, g: (_shift_rows_raw(g, -s),))


def _sigmoid(x):
    return 1.0 / (1.0 + jnp.exp(-x))


def _silu(x):
    return x * _sigmoid(x)


def _gelu(x):
    return 0.5 * x * (1.0 + jnp.tanh(math.sqrt(2.0 / math.pi) * (x + 0.044715 * x * x * x)))


def _softplus(x):
    return jnp.maximum(x, 0.0) + jnp.log(1.0 + jnp.exp(-jnp.abs(x)))


def _rms(x, g):
    return x * lax.rsqrt(jnp.mean(x * x, axis=-1, keepdims=True) + NORM_EPS) * g


def _conv_rows(x, taps):
    pad = len(taps) // 2
    acc = None
    for j, w in enumerate(taps):
        term = _shift_rows(x, j - pad) * w
        acc = term if acc is None else acc + term
    return acc


@jax.custom_vjp
def _inv_unit(a):
    n = a.shape[0]
    eye = (lax.broadcasted_iota(jnp.int32, (n, n), 0) == lax.broadcasted_iota(jnp.int32, (n, n), 1)).astype(F32)
    p = eye - a
    ak = a
    for _ in range(int(math.log2(n)) - 1):
        ak = _mmh(ak, ak)
        p = p + _mmh(p, ak)
    return p


def _inv_unit_fwd(a):
    t = _inv_unit(a)
    return t, t


def _inv_unit_bwd(t, g):
    return (-_mmh_tn(t, _mmh_nt(g, t)),)


_inv_unit.defvjp(_inv_unit_fwd, _inv_unit_bwd)


def _tiled(fn, name, grid, ins, outs, sem=None):
    n_in = len(ins)

    def body(*refs):
        vals = [r[...] for r in refs[:n_in]]
        res = fn(*vals)
        if not isinstance(res, (tuple, list)):
            res = (res,)
        for o_ref, r, spec in zip(refs[n_in:], res, outs):
            acc = spec[4]
            if not acc:
                o_ref[...] = r.astype(o_ref.dtype)
            else:
                first = functools.reduce(jnp.logical_and, [pl.program_id(a) == 0 for a in acc])

                @pl.when(first)
                def _(o_ref=o_ref, r=r):
                    o_ref[...] = r.astype(o_ref.dtype)

                @pl.when(jnp.logical_not(first))
                def _(o_ref=o_ref, r=r):
                    o_ref[...] += r.astype(o_ref.dtype)

    return pl.pallas_call(
        body, name=name, grid=grid,
        in_specs=[pl.BlockSpec(b, m) for _, b, m in ins],
        out_specs=[pl.BlockSpec(s[2], s[3]) for s in outs],
        out_shape=[jax.ShapeDtypeStruct(s[0], s[1]) for s in outs],
        compiler_params=_params(sem),
    )(*[a for a, _, _ in ins])


def _matmul(a, b, mode, out_dtype, name, add=None, a_split=1, b_split=1, o_split=1):
    def dims2(x, split):
        return (x.shape[-2], x.shape[-1] * split)

    ar, ac = dims2(a, a_split)
    br, bc = dims2(b, b_split)
    if mode == 'nn':
        M, K, N = ar, ac, bc
        assert br == K
    elif mode == 'nt':
        M, K, N = ar, ac, br
        assert bc == K
    else:
        K, M, N = ar, ac, bc
        assert br == K
    tm = _pick(M // (a_split if mode == 'tn' else 1), 512)
    tn = _pick(N // max(o_split, b_split if mode != 'nt' else 1), 1408)
    tk = _pick(K // max(a_split if mode != 'tn' else 1, b_split if mode == 'nt' else 1), 2048)
    gm, gn, gk = M // tm, N // tn, K // tk

    def col_map(split, total_cols, tile):
        per = total_cols // split // tile

        def f(r, c):
            return (c // per, r, c % per) if split > 1 else (r, c)
        return f

    if mode == 'nn':
        a_idx = col_map(a_split, K, tk)
        b_idx = col_map(b_split, N, tn)
        a_spec = pl.BlockSpec(((None,) if a_split > 1 else ()) + (tm, tk), lambda m, n, k: a_idx(m, k))
        b_spec = pl.BlockSpec(((None,) if b_split > 1 else ()) + (tk, tn), lambda m, n, k: b_idx(k, n))
        dn = ((1,), (0,))
    elif mode == 'nt':
        a_idx = col_map(a_split, K, tk)
        b_idx = col_map(b_split, K, tk)
        a_spec = pl.BlockSpec(((None,) if a_split > 1 else ()) + (tm, tk), lambda m, n, k: a_idx(m, k))
        b_spec = pl.BlockSpec(((None,) if b_split > 1 else ()) + (tn, tk), lambda m, n, k: b_idx(n, k))
        dn = ((1,), (1,))
    else:
        a_idx = col_map(a_split, M, tm)
        b_idx = col_map(b_split, N, tn)
        a_spec = pl.BlockSpec(((None,) if a_split > 1 else ()) + (tk, tm), lambda m, n, k: a_idx(k, m))
        b_spec = pl.BlockSpec(((None,) if b_split > 1 else ()) + (tk, tn), lambda m, n, k: b_idx(k, n))
        dn = ((0,), (0,))
    o_idx = col_map(o_split, N, tn)
    o_block = ((None,) if o_split > 1 else ()) + (tm, tn)
    o_spec = pl.BlockSpec(o_block, lambda m, n, k: o_idx(m, n))
    o_shape = ((o_split, M, N // o_split) if o_split > 1 else (M, N))
    in_specs = [a_spec, b_spec]
    args = [a, b]
    if add is not None:
        in_specs.append(pl.BlockSpec((tm, tn), lambda m, n, k: (m, n)))
        args.append(add)

    def body(*refs):
        a_ref, b_ref = refs[0], refs[1]
        add_ref = refs[2] if add is not None else None
        o_ref = refs[3] if add is not None else refs[2]
        part = lax.dot_general(a_ref[...].astype(BF16), b_ref[...].astype(BF16), (dn, ((), ())),
                               preferred_element_type=F32)

        def finish(total):
            if add_ref is not None:
                total = total + add_ref[...]
            o_ref[...] = total.astype(o_ref.dtype)

        if gk == 1:
            finish(part)
        else:
            acc_ref = refs[-1]
            k = pl.program_id(2)

            @pl.when(k == 0)
            def _():
                acc_ref[...] = part

            @pl.when(jnp.logical_and(k > 0, k < gk - 1))
            def _():
                acc_ref[...] += part

            @pl.when(k == gk - 1)
            def _():
                finish(acc_ref[...] + part)

    return pl.pallas_call(
        body, name=name, grid=(gm, gn, gk), in_specs=in_specs, out_specs=o_spec,
        out_shape=jax.ShapeDtypeStruct(o_shape, out_dtype),
        scratch_shapes=([pltpu.VMEM((tm, tn), F32)] if gk > 1 else []),
        compiler_params=_params(("parallel", "parallel", "arbitrary")),
    )(*args)


def _row_tile():
    return _pick(SEQ, 256, 8)


def _rms_fwd(x, g, name):
    tm = _row_tile()
    (h,) = _tiled(lambda xv, gv: _rms(xv, gv), name, (SEQ // tm,),
                  [(x, (tm, D_MODEL), lambda i: (i, 0)), (g, (1, D_MODEL), lambda i: (0, 0))],
                  [((SEQ, D_MODEL), BF16, (tm, D_MODEL), lambda i: (i, 0), ())])
    return h


def _rms_bwd(x, g, dh, dres, name):
    tm = _row_tile()

    def fn(xv, gv, dhv, drv):
        _, vjp = jax.vjp(_rms, xv, gv)
        dx, dg = vjp(dhv)
        dx = dx + drv
        return dx, dx, dg

    row = lambda i: (i, 0)
    return _tiled(fn, name, (SEQ // tm,),
                  [(x, (tm, D_MODEL), row), (g, (1, D_MODEL), lambda i: (0, 0)),
                   (dh, (tm, D_MODEL), row), (dres, (tm, D_MODEL), row)],
                  [((SEQ, D_MODEL), F32, (tm, D_MODEL), row, ()),
                   ((SEQ, D_MODEL), BF16, (tm, D_MODEL), row, ()),
                   ((1, D_MODEL), F32, (1, D_MODEL), lambda i: (0, 0), (0,))])


def _loss_head(x, g, target):
    tm = _row_tile()

    def fn(xv, gv, tv):
        y, vjp = jax.vjp(_rms, xv, gv)
        err = y - tv
        part = 0.5 * jnp.sum(jnp.sum(err * err, axis=1, keepdims=True), axis=0, keepdims=True) / D_MODEL
        dx, dg = vjp(err / D_MODEL)
        return part, dx, dx, dg

    row = lambda i: (i, 0)
    one = lambda i: (0, 0)
    return _tiled(fn, "loss_head", (SEQ // tm,),
                  [(x, (tm, D_MODEL), row), (g, (1, D_MODEL), one), (target, (tm, D_MODEL), row)],
                  [((1, 1), F32, (1, 1), one, (0,)),
                   ((SEQ, D_MODEL), F32, (tm, D_MODEL), row, ()),
                   ((SEQ, D_MODEL), BF16, (tm, D_MODEL), row, ()),
                   ((1, D_MODEL), F32, (1, D_MODEL), one, (0,))])


def _p_col(width, index):
    return lambda tm: ((tm, width), lambda i: (i, index))


def _merge_fn(ga, gb, ta, tb):
    return _sigmoid(ga) * ta + _sigmoid(gb) * tb


def _merge_fwd(p, ta, tb):
    tm = _row_tile()
    d = D_MODEL
    ga_blk = 4 * _gw() + 2 * _sw()
    assert ga_blk % d == 0
    ia, ib = ga_blk // d, ga_blk // d + 1
    row = lambda i: (i, 0)
    (m,) = _tiled(_merge_fn, "merge_fwd", (SEQ // tm,),
                  [(p, (tm, d), lambda i: (i, ia)), (p, (tm, d), lambda i: (i, ib)),
                   (ta, (tm, d), row), (tb, (tm, d), row)],
                  [((SEQ, d), BF16, (tm, d), row, ())])
    return m


def _merge_bwd(p, ta, tb, dm):
    tm = _row_tile()
    d = D_MODEL
    ga_blk = 4 * _gw() + 2 * _sw()
    ia, ib = ga_blk // d, ga_blk // d + 1

    def fn(ga, gb, tav, tbv, dmv):
        _, vjp = jax.vjp(_merge_fn, ga, gb, tav, tbv)
        return vjp(dmv)

    row = lambda i: (i, 0)
    out = ((SEQ, d), BF16, (tm, d), row, ())
    return _tiled(fn, "merge_bwd", (SEQ // tm,),
                  [(p, (tm, d), lambda i: (i, ia)), (p, (tm, d), lambda i: (i, ib)),
                   (ta, (tm, d), row), (tb, (tm, d), row), (dm, (tm, d), row)],
                  [out, out, out, out])


def _gate_fn(ab, alog, dtb):
    lane = lax.broadcasted_iota(jnp.int32, ab.shape, 1)
    nd = 2 * GDN_HEADS
    g = -jnp.exp(alog) * _softplus(ab + dtb)
    beta = _sigmoid(ab)
    return jnp.where(lane < nd, g, jnp.where(lane < 2 * nd, beta, 0.0))


def _ab_index():
    off = 4 * _gw() + 2 * _sw() + 2 * D_MODEL
    assert off % AB_PAD == 0
    return off // AB_PAD


def _gate_fwd(p, alog, dtb):
    tm = _row_tile()
    iab = _ab_index()
    one = lambda i: (0, 0)
    (g,) = _tiled(_gate_fn, "gdn_gate_fwd", (SEQ // tm,),
                  [(p, (tm, AB_PAD), lambda i: (i, iab)), (alog, (1, AB_PAD), one), (dtb, (1, AB_PAD), one)],
                  [((SEQ, AB_PAD), F32, (tm, AB_PAD), lambda i: (i, 0), ())])
    return g


def _gate_bwd(p, alog, dtb, dg):
    tm = _row_tile()
    iab = _ab_index()
    one = lambda i: (0, 0)

    def fn(ab, al, db, dgv):
        _, vjp = jax.vjp(_gate_fn, ab, al, db)
        return vjp(dgv)

    return _tiled(fn, "gdn_gate_bwd", (SEQ // tm,),
                  [(p, (tm, AB_PAD), lambda i: (i, iab)), (alog, (1, AB_PAD), one), (dtb, (1, AB_PAD), one),
                   (dg, (tm, AB_PAD), lambda i: (i, 0))],
                  [((SEQ, AB_PAD), BF16, (tm, AB_PAD), lambda i: (i, 0), ()),
                   ((1, AB_PAD), F32, (1, AB_PAD), one, (0,)),
                   ((1, AB_PAD), F32, (1, AB_PAD), one, (0,))])


def _l2n(x):
    return x * lax.rsqrt(jnp.sum(x * x, axis=-1, keepdims=True) + NORM_EPS)


def _qkv_fn(xq, xk, xv, *taps):
    n = QKV_CONV
    q = _l2n(_silu(_conv_rows(xq, taps[0:n])))
    k = _l2n(_silu(_conv_rows(xk, taps[n:2 * n])))
    v = _silu(_conv_rows(xv, taps[2 * n:3 * n]))
    return q, k, v


def _qkv_specs(p, conv_w):
    hd, nh = HEAD_DIM, GDN_HEADS
    ins = [(p, (SEQ, hd), (lambda h, s=s: (0, s * nh + h))) for s in range(3)]
    ins += [(conv_w, (QKV_CONV, hd), (lambda h, s=s: (0, s * nh + h))) for s in range(3)]
    return ins


def _qkv_fwd(p, conv_w):
    hd, nh = HEAD_DIM, GDN_HEADS
    ins = _qkv_specs(p, conv_w)
    n_in = len(ins)
    out_spec = pl.BlockSpec((SEQ, hd), lambda h: (0, h))

    def body(*refs):
        xs = [r[...] for r in refs[:3]]
        taps = [refs[3 + s][j:j + 1, :] for s in range(3) for j in range(QKV_CONV)]
        q, k, v = _qkv_fn(*xs, *taps)
        refs[n_in][...] = q
        refs[n_in + 1][...] = k
        refs[n_in + 2][...] = v

    return pl.pallas_call(
        body, name="gdn_qkv_fwd", grid=(nh,),
        in_specs=[pl.BlockSpec(b, m) for _, b, m in ins], out_specs=[out_spec] * 3,
        out_shape=[jax.ShapeDtypeStruct((SEQ, nh * hd), F32)] * 3,
        compiler_params=_params(("parallel",)),
    )(*[a for a, _, _ in ins])


def _qkv_bwd(p, conv_w, dq, dk, dv):
    hd, nh = HEAD_DIM, GDN_HEADS
    ins = _qkv_specs(p, conv_w) + [(t, (SEQ, hd), lambda h: (0, h)) for t in (dq, dk, dv)]
    n_in = len(ins)

    def body(*refs):
        xs = [r[...] for r in refs[:3]]
        taps = [refs[3 + s][j:j + 1, :] for s in range(3) for j in range(QKV_CONV)]
        cts = tuple(r[...] for r in refs[6:9])
        _, vjp = jax.vjp(_qkv_fn, *xs, *taps)
        grads = vjp(cts)
        for s in range(3):
            refs[n_in + s][...] = grads[s].astype(BF16)
            for j in range(QKV_CONV):
                refs[n_in + 3 + s][j:j + 1, :] = jnp.sum(grads[3 + s * QKV_CONV + j], axis=0, keepdims=True)

    dx_spec = pl.BlockSpec((SEQ, hd), lambda h: (0, h))
    dw_spec = pl.BlockSpec((QKV_CONV, hd), lambda h: (0, h))
    outs = pl.pallas_call(
        body, name="gdn_qkv_bwd", grid=(nh,),
        in_specs=[pl.BlockSpec(b, m) for _, b, m in ins], out_specs=[dx_spec] * 3 + [dw_spec] * 3,
        out_shape=[jax.ShapeDtypeStruct((SEQ, nh * hd), BF16)] * 3
        + [jax.ShapeDtypeStruct((QKV_CONV, nh * hd), F32)] * 3,
        compiler_params=_params(("parallel",)),
    )(*[a for a, _, _ in ins])
    return outs[0:3], jnp.concatenate(outs[3:6], axis=1)


def _prep_fn(q, k, v, gblk, head):
    c = q.shape[0]
    scale = HEAD_DIM ** -0.5
    ii = lax.broadcasted_iota(jnp.int32, (c, c), 0)
    jj = lax.broadcasted_iota(jnp.int32, (c, c), 1)
    eye = (ii == jj).astype(F32)
    lane = lax.broadcasted_iota(jnp.int32, gblk.shape, 1)
    kk_t = _mm_nt(k, k)
    qk_t = _mm_nt(q, k)
    outs = []
    for d in range(2):
        g = jnp.sum(jnp.where(lane == head + d * GDN_HEADS, gblk, 0.0), axis=1, keepdims=True)
        beta = jnp.sum(jnp.where(lane == head + (2 + d) * GDN_HEADS, gblk, 0.0), axis=1, keepdims=True)
        incl = (jj <= ii) if d == 0 else (jj >= ii)
        strict = (jj < ii) if d == 0 else (jj > ii)
        incl_t = (ii <= jj) if d == 0 else (ii >= jj)
        g_row = jnp.sum(eye * g, axis=0, keepdims=True)
        gc = jnp.sum(jnp.where(incl, g_row, 0.0), axis=1, keepdims=True)
        gc_row = jnp.sum(jnp.where(incl_t, g, 0.0), axis=0, keepdims=True)
        decay = jnp.where(incl, jnp.exp(jnp.where(incl, gc - gc_row, 0.0)), 0.0)
        a = jnp.where(strict, kk_t * beta * decay, 0.0)
        t = _inv_unit(a)
        egc = jnp.exp(gc)
        u = _mmh(t, v * beta)
        w = _mmh(t, k * (beta * egc))
        qk = jnp.where(incl, qk_t * (scale * decay), 0.0)
        g_last = jnp.sum(g, axis=0, keepdims=True)
        q_dec = q * (scale * egc)
        k_dec = k * jnp.exp(g_last - gc)
        dec = jnp.exp(g_last)
        outs += [u, w, q_dec, k_dec, qk, dec]
    return tuple(outs)


def _n_chunks():
    return SEQ // GDN_CHUNK


def _gdn_prep_fwd(q, k, v, gates):
    c, hd, nh, nc = GDN_CHUNK, HEAD_DIM, GDN_HEADS, _n_chunks()
    gw = nh * hd
    blk = lambda n, h: (n, h)

    def body(q_ref, k_ref, v_ref, g_ref, u_ref, w_ref, qd_ref, kd_ref, qk_ref, dec_ref):
        h = pl.program_id(1)
        res = _prep_fn(q_ref[...], k_ref[...], v_ref[...], g_ref[...], h)
        for d in range(2):
            u, w, qd, kd, qk, dec = res[6 * d:6 * d + 6]
            u_ref[d] = u
            w_ref[d] = w
            qd_ref[d] = qd
            kd_ref[d] = kd
            qk_ref[d] = qk
            dec_ref[d, pl.ds(h, 1), :] = jnp.broadcast_to(dec, (1, LANES))

    wide = pl.BlockSpec((2, c, hd), lambda n, h: (0, n, h))
    return pl.pallas_call(
        body, name="gdn_prep_fwd", grid=(nc, nh),
        in_specs=[pl.BlockSpec((c, hd), blk)] * 3 + [pl.BlockSpec((c, LANES), lambda n, h: (n, 0))],
        out_specs=[wide] * 4 + [pl.BlockSpec((2, None, c, c), lambda n, h: (0, h, n, 0)),
                                pl.BlockSpec((2, None, nh, LANES), lambda n, h: (0, n, 0, 0))],
        out_shape=[jax.ShapeDtypeStruct((2, SEQ, gw), F32)] * 4
        + [jax.ShapeDtypeStruct((2, nh, SEQ, c), F32), jax.ShapeDtypeStruct((2, nc, nh, LANES), F32)],
        compiler_params=_params(("parallel", "arbitrary")),
    )(q, k, v, gates)


def _gdn_prep_bwd(q, k, v, gates, cts):
    c, hd, nh, nc = GDN_CHUNK, HEAD_DIM, GDN_HEADS, _n_chunks()
    gw = nh * hd
    blk = lambda n, h: (n, h)

    def body(*refs):
        q_ref, k_ref, v_ref, g_ref = refs[:4]
        ct_refs = refs[4:16]
        dq_ref, dk_ref, dv_ref, dg_ref = refs[16:20]
        h = pl.program_id(1)
        _, vjp = jax.vjp(lambda a, b, cc, g: _prep_fn(a, b, cc, g, h), q_ref[...], k_ref[...], v_ref[...], g_ref[...])
        vals = []
        for d in range(2):
            r = ct_refs[6 * d:6 * d + 6]
            vals += [r[0][...], r[1][...], r[2][...], r[3][...], r[4][...], r[5][pl.ds(h, 1), :][:, 0:1]]
        dq, dk, dv, dg = vjp(tuple(vals))
        dq_ref[...] = dq
        dk_ref[...] = dk
        dv_ref[...] = dv

        @pl.when(h == 0)
        def _():
            dg_ref[...] = dg

        @pl.when(h > 0)
        def _():
            dg_ref[...] += dg

    one_dir = [pl.BlockSpec((c, hd), blk)] * 4 + [pl.BlockSpec((None, c, c), lambda n, h: (h, n, 0)),
                                                  pl.BlockSpec((None, nh, LANES), lambda n, h: (n, 0, 0))]
    return pl.pallas_call(
        body, name="gdn_prep_bwd", grid=(nc, nh),
        in_specs=[pl.BlockSpec((c, hd), blk)] * 3 + [pl.BlockSpec((c, LANES), lambda n, h: (n, 0))] + one_dir * 2,
        out_specs=[pl.BlockSpec((c, hd), blk)] * 3 + [pl.BlockSpec((c, LANES), lambda n, h: (n, 0))],
        out_shape=[jax.ShapeDtypeStruct((SEQ, gw), F32)] * 3 + [jax.ShapeDtypeStruct((SEQ, LANES), F32)],
        compiler_params=_params(("parallel", "arbitrary")),
    )(q, k, v, gates, *cts)


def _scan_step(state, u, w, qd, kd, qk, dec):
    v_new = u - _mm(w, state)
    o = _mm(qd, state) + _mm(qk, v_new)
    new_state = state * dec + _mm_tn(kd, v_new)
    return new_state, o


def _scan_in_specs(chunk_of):
    c, hd, nh = GDN_CHUNK, HEAD_DIM, GDN_HEADS
    gw = nh * hd
    specs = []
    for d in range(2):
        f = chunk_of[d]
        specs += [pl.BlockSpec((None, c, gw), lambda n, d=d, f=f: (d, f(n), 0))] * 4
        specs += [pl.BlockSpec((None, nh, c, c), lambda n, d=d, f=f: (d, 0, f(n), 0)),
                  pl.BlockSpec((None, None, nh, LANES), lambda n, d=d, f=f: (d, f(n), 0, 0))]
    return specs


def _gdn_scan_fwd(u, w, qd, kd, qk, dec):
    c, hd, nh, nc = GDN_CHUNK, HEAD_DIM, GDN_HEADS, _n_chunks()
    gw = nh * hd
    chunk_of = (lambda n: n, lambda n: nc - 1 - n)

    def body(*refs):
        ins = refs[:12]
        o_refs = refs[12:14]
        st_refs = refs[14:16]
        s_ref = refs[16]

        @pl.when(pl.program_id(0) == 0)
        def _():
            s_ref[...] = jnp.zeros_like(s_ref)

        for d in range(2):
            u_r, w_r, qd_r, kd_r, qk_r, dec_r = ins[6 * d:6 * d + 6]
            for h in range(nh):
                cols = slice(h * hd, (h + 1) * hd)
                st = s_ref[d * nh + h]
                st_refs[d][h] = st
                new_state, o = _scan_step(st, u_r[:, cols], w_r[:, cols], qd_r[:, cols], kd_r[:, cols],
                                          qk_r[h], dec_r[h:h + 1, :])
                s_ref[d * nh + h] = new_state
                o_refs[d][:, cols] = o

    o_specs = [pl.BlockSpec((c, gw), lambda n, f=f: (f(n), 0)) for f in chunk_of]
    st_specs = [pl.BlockSpec((None, nh, hd, hd), lambda n, f=f: (f(n), 0, 0, 0)) for f in chunk_of]
    return pl.pallas_call(
        body, name="gdn_scan_fwd", grid=(nc,), in_specs=_scan_in_specs(chunk_of),
        out_specs=o_specs + st_specs,
        out_shape=[jax.ShapeDtypeStruct((SEQ, gw), F32)] * 2 + [jax.ShapeDtypeStruct((nc, nh, hd, hd), F32)] * 2,
        scratch_shapes=[pltpu.VMEM((2 * nh, hd, hd), F32)],
        compiler_params=_params(("arbitrary",)),
    )(*([u, w, qd, kd, qk, dec] * 2))


def _gdn_scan_bwd(u, w, qd, kd, qk, dec, st0, st1, do):
    c, hd, nh, nc = GDN_CHUNK, HEAD_DIM, GDN_HEADS, _n_chunks()
    gw = nh * hd
    chunk_of = (lambda n: nc - 1 - n, lambda n: n)

    def body(*refs):
        ins = refs[:12]
        st_in = refs[12:14]
        do_in = refs[14:16]
        outs = refs[16:28]
        ds_ref = refs[28]

        @pl.when(pl.program_id(0) == 0)
        def _():
            ds_ref[...] = jnp.zeros_like(ds_ref)

        for d in range(2):
            u_r, w_r, qd_r, kd_r, qk_r, dec_r = ins[6 * d:6 * d + 6]
            du_r, dw_r, dqd_r, dkd_r, dqk_r, ddec_r = outs[6 * d:6 * d + 6]
            for h in range(nh):
                cols = slice(h * hd, (h + 1) * hd)
                _, vjp = jax.vjp(_scan_step, st_in[d][h], u_r[:, cols], w_r[:, cols], qd_r[:, cols],
                                 kd_r[:, cols], qk_r[h], dec_r[h:h + 1, 0:1])
                dst, du, dw, dqd, dkd, dqk, ddec = vjp((ds_ref[d * nh + h], do_in[d][:, cols]))
                ds_ref[d * nh + h] = dst
                du_r[:, cols] = du
                dw_r[:, cols] = dw
                dqd_r[:, cols] = dqd
                dkd_r[:, cols] = dkd
                dqk_r[h] = dqk
                ddec_r[h:h + 1, :] = jnp.broadcast_to(ddec, (1, LANES))

    st_specs = [pl.BlockSpec((None, nh, hd, hd), lambda n, f=f: (f(n), 0, 0, 0)) for f in chunk_of]
    do_specs = [pl.BlockSpec((c, gw), lambda n, f=f: (f(n), 0)) for f in chunk_of]
    out_specs, out_shape = [], []
    for f in chunk_of:
        out_specs += [pl.BlockSpec((c, gw), lambda n, f=f: (f(n), 0))] * 4
        out_specs += [pl.BlockSpec((nh, c, c), lambda n, f=f: (0, f(n), 0)),
                      pl.BlockSpec((None, nh, LANES), lambda n, f=f: (f(n), 0, 0))]
        out_shape += [jax.ShapeDtypeStruct((SEQ, gw), F32)] * 4
        out_shape += [jax.ShapeDtypeStruct((nh, SEQ, c), F32), jax.ShapeDtypeStruct((nc, nh, LANES), F32)]
    return pl.pallas_call(
        body, name="gdn_scan_bwd", grid=(nc,), in_specs=_scan_in_specs(chunk_of) + st_specs + do_specs,
        out_specs=out_specs, out_shape=out_shape,
        scratch_shapes=[pltpu.VMEM((2 * nh, hd, hd), F32)],
        compiler_params=_params(("arbitrary",)),
    )(*([u, w, qd, kd, qk, dec] * 2), st0, st1, do, do)


def _post_fn(o0, o1, z, gn):
    return _rms(o0 + o1, gn) * _silu(z)


def _gdn_post_fwd(o0, o1, p, gn):
    tm, hd, nh = _row_tile(), HEAD_DIM, GDN_HEADS
    zoff = 3 * nh
    blk = lambda i, h: (i, h)
    (ya,) = _tiled(_post_fn, "gdn_post_fwd", (SEQ // tm, nh),
                   [(o0, (tm, hd), blk), (o1, (tm, hd), blk),
                    (p, (tm, hd), lambda i, h: (i, zoff + h)), (gn, (1, hd), lambda i, h: (0, 0))],
                   [((SEQ, nh * hd), BF16, (tm, hd), blk, ())])
    return ya


def _gdn_post_bwd(o0, o1, p, gn, dya):
    tm, hd, nh = _row_tile(), HEAD_DIM, GDN_HEADS
    zoff = 3 * nh
    blk = lambda i, h: (i, h)

    def fn(a, b, z, g, dy):
        _, vjp = jax.vjp(_post_fn, a, b, z, g)
        do, _, dz, dg = vjp(dy.astype(F32))
        return do, dz, dg

    return _tiled(fn, "gdn_post_bwd", (SEQ // tm, nh),
                  [(o0, (tm, hd), blk), (o1, (tm, hd), blk),
                   (p, (tm, hd), lambda i, h: (i, zoff + h)), (gn, (1, hd), lambda i, h: (0, 0)),
                   (dya, (tm, hd), blk)],
                  [((SEQ, nh * hd), F32, (tm, hd), blk, ()),
                   ((SEQ, nh * hd), BF16, (tm, hd), blk, ()),
                   ((1, hd), F32, (1, hd), lambda i, h: (0, 0), (0, 1))],
                  sem=("arbitrary", "arbitrary"))


def _sgu_ln(v, lg, lb):
    gv = _gelu(v)
    mu = jnp.mean(gv, axis=-1, keepdims=True)
    cen = gv - mu
    var = jnp.mean(cen * cen, axis=-1, keepdims=True)
    return cen * lax.rsqrt(var + NORM_EPS) * lg + lb


def _sgu_mix(vn, w_ref, bt_ref):
    parts = []
    for g in range(SGU_GROUPS):
        cols = slice(g * SGU_DIM, (g + 1) * SGU_DIM)
        parts.append(_mm(w_ref[g], vn[:, cols]) + bt_ref[:, g:g + 1])
    return jnp.concatenate(parts, axis=1)


def _sgu_specs(p):
    sw, sb = _sw(), SGU_BLOCK
    uoff = 4 * _gw()
    assert uoff % sw == 0
    iu = uoff // sw
    return [(p, (sb, sw), lambda n: (n, iu)), (p, (sb, sw), lambda n: (n, iu + 1))]


def _sgu_fwd(p, lg, lb, w_s, b_t):
    sw, sb = _sw(), SGU_BLOCK
    ins = _sgu_specs(p)

    def body(u_ref, v_ref, lg_ref, lb_ref, w_ref, bt_ref, y_ref):
        vn = _sgu_ln(v_ref[...], lg_ref[...], lb_ref[...])
        y_ref[...] = (_gelu(u_ref[...]) * _sgu_mix(vn, w_ref, bt_ref)).astype(BF16)

    one2 = lambda n: (0, 0)
    return pl.pallas_call(
        body, name="sgu_fwd", grid=(SEQ // sb,),
        in_specs=[pl.BlockSpec(b, m) for _, b, m in ins]
        + [pl.BlockSpec((1, sw), one2), pl.BlockSpec((1, sw), one2),
           pl.BlockSpec((SGU_GROUPS, sb, sb), lambda n: (0, 0, 0)), pl.BlockSpec((sb, LANES), one2)],
        out_specs=pl.BlockSpec((sb, sw), lambda n: (n, 0)),
        out_shape=jax.ShapeDtypeStruct((SEQ, sw), BF16),
        compiler_params=_params(("parallel",)),
    )(p, p, lg, lb, w_s, b_t)


def _sgu_bwd(p, lg, lb, w_s, b_t, dy):
    sw, sb, ng = _sw(), SGU_BLOCK, SGU_GROUPS
    ins = _sgu_specs(p)

    def body(u_ref, v_ref, lg_ref, lb_ref, w_ref, bt_ref, dy_ref, du_ref, dv_ref, dlg_ref, dlb_ref, dw_ref, dbt_ref):
        first = pl.program_id(0) == 0
        u, v = u_ref[...], v_ref[...]
        gu, gelu_vjp = jax.vjp(_gelu, u)
        vn, ln_vjp = jax.vjp(_sgu_ln, v, lg_ref[...], lb_ref[...])
        s = _sgu_mix(vn, w_ref, bt_ref)
        dyv = dy_ref[...].astype(F32)
        ds = dyv * gu
        (du,) = gelu_vjp(dyv * s)
        lane = lax.broadcasted_iota(jnp.int32, (sb, LANES), 1)
        dvn_parts = []
        dbt = jnp.zeros((sb, LANES), F32)
        for g in range(ng):
            cols = slice(g * SGU_DIM, (g + 1) * SGU_DIM)
            ds_g = ds[:, cols]
            dw_g = _mm_nt(ds_g, vn[:, cols])
            dvn_parts.append(_mm_tn(w_ref[g], ds_g))
            dbt = dbt + jnp.where(lane == g, jnp.sum(ds_g, axis=1, keepdims=True), 0.0)

            @pl.when(first)
            def _(g=g, dw_g=dw_g):
                dw_ref[g] = dw_g

            @pl.when(jnp.logical_not(first))
            def _(g=g, dw_g=dw_g):
                dw_ref[g] += dw_g

        dv, dlg, dlb = ln_vjp(jnp.concatenate(dvn_parts, axis=1))
        du_ref[...] = du.astype(BF16)
        dv_ref[...] = dv.astype(BF16)

        @pl.when(first)
        def _():
            dlg_ref[...] = dlg
            dlb_ref[...] = dlb
            dbt_ref[...] = dbt

        @pl.when(jnp.logical_not(first))
        def _():
            dlg_ref[...] += dlg
            dlb_ref[...] += dlb
            dbt_ref[...] += dbt

    one2 = lambda n: (0, 0)
    row = pl.BlockSpec((sb, sw), lambda n: (n, 0))
    return pl.pallas_call(
        body, name="sgu_bwd", grid=(SEQ // sb,),
        in_specs=[pl.BlockSpec(b, m) for _, b, m in ins]
        + [pl.BlockSpec((1, sw), one2), pl.BlockSpec((1, sw), one2),
           pl.BlockSpec((ng, sb, sb), lambda n: (0, 0, 0)), pl.BlockSpec((sb, LANES), one2), row],
        out_specs=[row, row, pl.BlockSpec((1, sw), one2), pl.BlockSpec((1, sw), one2),
                   pl.BlockSpec((ng, sb, sb), lambda n: (0, 0, 0)), pl.BlockSpec((sb, LANES), one2)],
        out_shape=[jax.ShapeDtypeStruct((SEQ, sw), BF16)] * 2 + [jax.ShapeDtypeStruct((1, sw), F32)] * 2
        + [jax.ShapeDtypeStruct((ng, sb, sb), F32), jax.ShapeDtypeStruct((sb, LANES), F32)],
        compiler_params=_params(("arbitrary",)),
    )(p, p, lg, lb, w_s, b_t, dy)


def _ffn_fn(xg, xv, bg, bv, *taps):
    n = FFN_CONV
    cg = _conv_rows(xg, taps[0:n]) + bg
    cv = _conv_rows(xv, taps[n:2 * n]) + bv
    return _silu(cg) * cv


def _ffn_tile():
    return _pick(D_FF, 256)


def _ffn_specs(up, conv_w, conv_b):
    tc = _ffn_tile()
    nt = D_FF // tc
    ins = [(up, (None, SEQ, tc), (lambda j, s=s: (s, 0, j))) for s in range(2)]
    ins += [(conv_b, (1, tc), (lambda j, s=s: (0, s * nt + j))) for s in range(2)]
    ins += [(conv_w, (FFN_CONV, tc), (lambda j, s=s: (0, s * nt + j))) for s in range(2)]
    return ins


def _ffn_act_fwd(up, conv_w, conv_b):
    tc = _ffn_tile()
    ins = _ffn_specs(up, conv_w, conv_b)

    def body(xg, xv, bg, bv, wg, wv, o_ref):
        taps = [w[j:j + 1, :] for w in (wg, wv) for j in range(FFN_CONV)]
        o_ref[...] = _ffn_fn(xg[...], xv[...], bg[...], bv[...], *taps).astype(BF16)

    return pl.pallas_call(
        body, name="ffn_act_fwd", grid=(D_FF // tc,),
        in_specs=[pl.BlockSpec(b, m) for _, b, m in ins],
        out_specs=pl.BlockSpec((SEQ, tc), lambda j: (0, j)),
        out_shape=jax.ShapeDtypeStruct((SEQ, D_FF), BF16),
        compiler_params=_params(("parallel",)),
    )(*[a for a, _, _ in ins])


def _ffn_act_bwd(up, conv_w, conv_b, dact):
    tc = _ffn_tile()
    nt = D_FF // tc
    ins = _ffn_specs(up, conv_w, conv_b) + [(dact, (SEQ, tc), lambda j: (0, j))]

    def body(xg, xv, bg, bv, wg, wv, dact_ref, dup_ref, dwg_ref, dwv_ref, dbg_ref, dbv_ref):
        taps = [w[j:j + 1, :] for w in (wg, wv) for j in range(FFN_CONV)]
        _, vjp = jax.vjp(_ffn_fn, xg[...], xv[...], bg[...], bv[...], *taps)
        grads = vjp(dact_ref[...].astype(F32))
        dup_ref[0] = grads[0].astype(BF16)
        dup_ref[1] = grads[1].astype(BF16)
        dbg_ref[...] = jnp.sum(grads[2], axis=0, keepdims=True)
        dbv_ref[...] = jnp.sum(grads[3], axis=0, keepdims=True)
        for j in range(FFN_CONV):
            dwg_ref[j:j + 1, :] = jnp.sum(grads[4 + j], axis=0, keepdims=True)
            dwv_ref[j:j + 1, :] = jnp.sum(grads[4 + FFN_CONV + j], axis=0, keepdims=True)

    col = lambda j: (0, j)
    outs = pl.pallas_call(
        body, name="ffn_act_bwd", grid=(nt,),
        in_specs=[pl.BlockSpec(b, m) for _, b, m in ins],
        out_specs=[pl.BlockSpec((2, SEQ, tc), lambda j: (0, 0, j)),
                   pl.BlockSpec((FFN_CONV, tc), col), pl.BlockSpec((FFN_CONV, tc), col),
                   pl.BlockSpec((1, tc), col), pl.BlockSpec((1, tc), col)],
        out_shape=[jax.ShapeDtypeStruct((2, SEQ, D_FF), BF16),
                   jax.ShapeDtypeStruct((FFN_CONV, D_FF), F32), jax.ShapeDtypeStruct((FFN_CONV, D_FF), F32),
                   jax.ShapeDtypeStruct((1, D_FF), F32), jax.ShapeDtypeStruct((1, D_FF), F32)],
        compiler_params=_params(("parallel",)),
    )(*[a for a, _, _ in ins])
    dup, dwg, dwv, dbg, dbv = outs
    return dup, jnp.concatenate([dwg, dwv], axis=1), jnp.concatenate([dbg, dbv], axis=1)


def _pad_lanes(v):
    return jnp.pad(v.reshape(1, -1), ((0, 0), (0, LANES - v.size)))


def _layer_fwd(x, lw):
    h = _rms_fwd(x, lw['norm_mix_g'], "rms_mix_fwd")
    p = _matmul(h, lw['w_all'], 'nn', F32, "mm_in")
    gates = _gate_fwd(p, lw['a_log'], lw['dt_bias'])
    q, k, v = _qkv_fwd(p, lw['qkv_conv_w'])
    u, w, qd, kd, qk, dec = _gdn_prep_fwd(q, k, v, gates)
    o0, o1, st0, st1 = _gdn_scan_fwd(u, w, qd, kd, qk, dec)
    ya = _gdn_post_fwd(o0, o1, p, lw['gdn_norm_g'])
    yb = _sgu_fwd(p, lw['sgu_ln_g'], lw['sgu_ln_b'], lw['sgu_w'], lw['sgu_bt'])
    ta = _matmul(ya, lw['w_branch_a'], 'nn', F32, "mm_branch_a", b_split=N_CHIPS)
    tb = _matmul(yb, lw['w_branch_b'], 'nn', F32, "mm_branch_b", b_split=N_CHIPS)
    m = _merge_fwd(p, ta, tb)
    x1 = _matmul(m, lw['w_out'], 'nn', F32, "mm_out", add=x)
    h2 = _rms_fwd(x1, lw['norm_ffn_g'], "rms_ffn_fwd")
    up = _matmul(h2, lw['w_up'], 'nn', F32, "mm_up", b_split=N_CHIPS, o_split=2)
    act = _ffn_act_fwd(up, lw['ffn_conv_w'], lw['ffn_conv_b'])
    x2 = _matmul(act, lw['w_down'], 'nn', F32, "mm_down", add=x1)
    saved = dict(x=x, h=h, p=p, gates=gates, q=q, k=k, v=v, u=u, w=w, qd=qd, kd=kd, qk=qk, dec=dec,
                 o0=o0, o1=o1, st0=st0, st1=st1, ya=ya, yb=yb, ta=ta, tb=tb, m=m, x1=x1, h2=h2, up=up, act=act)
    return x2, saved


def _layer_bwd(dx, dx_bf, lw, s):
    g = {}
    dact = _matmul(dx_bf, lw['w_down'], 'nt', BF16, "mm_down_dgrad")
    g['w_down'] = _matmul(s['act'], dx_bf, 'tn', F32, "mm_down_wgrad")
    dup, g['ffn_conv_w'], g['ffn_conv_b'] = _ffn_act_bwd(s['up'], lw['ffn_conv_w'], lw['ffn_conv_b'], dact)
    dh2 = _matmul(dup, lw['w_up'], 'nt', F32, "mm_up_dgrad", a_split=2, b_split=N_CHIPS)
    g['w_up'] = _matmul(s['h2'], dup, 'tn', F32, "mm_up_wgrad", b_split=2, o_split=N_CHIPS)
    dx1, dx1_bf, g['norm_ffn_g'] = _rms_bwd(s['x1'], lw['norm_ffn_g'], dh2, dx, "rms_ffn_bwd")
    dm = _matmul(dx1_bf, lw['w_out'], 'nt', F32, "mm_out_dgrad")
    g['w_out'] = _matmul(s['m'], dx1_bf, 'tn', F32, "mm_out_wgrad")
    d_ga, d_gb, d_ta, d_tb = _merge_bwd(s['p'], s['ta'], s['tb'], dm)
    dya = _matmul(d_ta, lw['w_branch_a'], 'nt', BF16, "mm_branch_a_dgrad", b_split=N_CHIPS)
    dyb = _matmul(d_tb, lw['w_branch_b'], 'nt', BF16, "mm_branch_b_dgrad", b_split=N_CHIPS)
    g['w_branch_a'] = _matmul(s['ya'], d_ta, 'tn', F32, "mm_branch_a_wgrad", o_split=N_CHIPS)
    g['w_branch_b'] = _matmul(s['yb'], d_tb, 'tn', F32, "mm_branch_b_wgrad", o_split=N_CHIPS)
    du_s, dv_s, g['sgu_ln_g'], g['sgu_ln_b'], g['sgu_w'], g['sgu_bt'] = _sgu_bwd(
        s['p'], lw['sgu_ln_g'], lw['sgu_ln_b'], lw['sgu_w'], lw['sgu_bt'], dyb)
    do, dz, g['gdn_norm_g'] = _gdn_post_bwd(s['o0'], s['o1'], s['p'], lw['gdn_norm_g'], dya)
    cts = _gdn_scan_bwd(s['u'], s['w'], s['qd'], s['kd'], s['qk'], s['dec'], s['st0'], s['st1'], do)
    dq, dk, dv, dgates = _gdn_prep_bwd(s['q'], s['k'], s['v'], s['gates'], cts)
    (dxq, dxk, dxv), g['qkv_conv_w'] = _qkv_bwd(s['p'], lw['qkv_conv_w'], dq, dk, dv)
    d_ab, g['a_log'], g['dt_bias'] = _gate_bwd(s['p'], lw['a_log'], lw['dt_bias'], dgates)
    dp = jnp.concatenate([dxq, dxk, dxv, dz, du_s, dv_s, d_ga, d_gb, d_ab], axis=1)
    dh = _matmul(dp, lw['w_all'], 'nt', F32, "mm_in_dgrad")
    g['w_all'] = _matmul(s['h'], dp, 'tn', F32, "mm_in_wgrad")
    dx0, dx0_bf, g['norm_mix_g'] = _rms_bwd(s['x'], lw['norm_mix_g'], dh, dx1, "rms_mix_bwd")
    return dx0, dx0_bf, g


def _w_all_from_w_in(w_in):
    n_ab = 4 * GDN_HEADS
    cut = 4 * _gw()
    pad = jnp.zeros((w_in.shape[0], AB_PAD - n_ab), w_in.dtype)
    return jnp.concatenate([w_in[:, :cut], w_in[:, cut + n_ab:], w_in[:, cut:cut + n_ab], pad], axis=1)


def _w_in_grad_from_all(g_all):
    n_ab = 4 * GDN_HEADS
    cut = 4 * _gw()
    n_main = cut + 2 * _sw() + 2 * D_MODEL
    return jnp.concatenate([g_all[:, :cut], g_all[:, n_main:n_main + n_ab], g_all[:, cut:n_main]], axis=1)


def _layer_weights(l, big, small):
    cat_cols = lambda t: jnp.concatenate([t[l, j] for j in range(N_CHIPS)], axis=-1)
    return dict(
        w_all=_w_all_from_w_in(cat_cols(big['w_in'])),
        w_branch_a=big['w_branch_a'][l], w_branch_b=big['w_branch_b'][l], w_up=big['w_up'][l],
        w_out=big['w_out'][l].reshape(D_MODEL, D_MODEL), w_down=big['w_down'][l].reshape(D_FF, D_MODEL),
        qkv_conv_w=cat_cols(small['qkv_conv_w']), ffn_conv_w=cat_cols(small['ffn_conv_w']),
        norm_mix_g=small['norm_mix_g'][l:l + 1], norm_ffn_g=small['norm_ffn_g'][l:l + 1],
        a_log=_pad_lanes(small['a_log'][l]), dt_bias=_pad_lanes(small['dt_bias'][l]),
        gdn_norm_g=small['gdn_norm_g'][l:l + 1],
        sgu_ln_g=small['sgu_ln_g'][l:l + 1], sgu_ln_b=small['sgu_ln_b'][l:l + 1], sgu_w=small['sgu_w'][l],
        sgu_bt=jnp.pad(small['sgu_b'][l].T, ((0, 0), (0, LANES - SGU_GROUPS))),
        ffn_conv_b=small['ffn_conv_b'][l:l + 1])


_SMALL_GRADS = ('norm_mix_g', 'a_log', 'dt_bias', 'gdn_norm_g', 'sgu_ln_g', 'sgu_ln_b', 'sgu_w', 'sgu_bt',
                'norm_ffn_g', 'ffn_conv_b', 'qkv_conv_w', 'ffn_conv_w')
_BIG = ('w_in', 'w_branch_a', 'w_branch_b', 'w_out', 'w_up', 'w_down')


def _big_grad_slices(g):
    w_in = _w_in_grad_from_all(g['w_all'])
    w_in = w_in.reshape(D_MODEL, N_CHIPS, -1).transpose(1, 0, 2)
    return dict(w_in=w_in, w_branch_a=g['w_branch_a'], w_branch_b=g['w_branch_b'], w_up=g['w_up'],
                w_out=g['w_out'].reshape(N_CHIPS, D_MODEL // N_CHIPS, D_MODEL),
                w_down=g['w_down'].reshape(N_CHIPS, D_FF // N_CHIPS, D_MODEL))


def _adamw(w, g, m, v, name):
    shape = w.shape
    cols = shape[-1]
    rows = w.size // cols
    tr = _pick(rows, max(8, (1 << 18) // cols // 8 * 8), 8) if rows % 8 == 0 else rows

    def fn(wv, gv, mv, vv):
        m2 = ADAM_B1 * mv + (1.0 - ADAM_B1) * gv
        v2 = ADAM_B2 * vv + (1.0 - ADAM_B2) * (gv * gv)
        m_hat = m2 / (1.0 - ADAM_B1 ** ADAM_STEP)
        v_hat = v2 / (1.0 - ADAM_B2 ** ADAM_STEP)
        delta = -ADAM_LR * (m_hat / (jnp.sqrt(v_hat) + ADAM_EPS) + ADAM_WD * wv)
        return delta, m2, v2

    row = lambda i: (i, 0)
    outs = _tiled(fn, name, (rows // tr,),
                  [(t.reshape(rows, cols), (tr, cols), row) for t in (w, g, m, v)],
                  [((rows, cols), F32, (tr, cols), row, ())] * 3, sem=("parallel",))
    return [o.reshape(shape) for o in outs]


MESH_IDS = pl.DeviceIdType.MESH
ANY = pl.BlockSpec(memory_space=pl.ANY)


def _place():
    x, y, c = lax.axis_index("x"), lax.axis_index("y"), lax.axis_index("c")
    chips = [(1 - x, y), (x, 1 - y), (1 - x, 1 - y)]
    return x, y, c, 2 * x + y, chips


def _all_gather_chips(xs):
    n = len(xs)
    half = DEPTH // 2

    def body(*refs):
        x_refs, o_refs = refs[:n], refs[n:2 * n]
        send_sems, recv_sems, local_sems = refs[2 * n:]
        x, y, c, me, chips = _place()
        sibling = (x, y, 1 - c)
        mine = pl.ds(c * half, half)
        theirs = pl.ds((1 - c) * half, half)

        def rcopy(i, k, src, dst, to):
            return pltpu.make_async_remote_copy(src_ref=src, dst_ref=dst, send_sem=send_sems.at[i * 6 + k],
                                                recv_sem=recv_sems.at[i * 6 + k], device_id=to,
                                                device_id_type=MESH_IDS)

        local = [pltpu.make_async_copy(x_refs[i], o_refs[i].at[:, me], local_sems.at[i]) for i in range(n)]
        for cp in local:
            cp.start()
        sends = []
        for j, chip in enumerate(chips):
            for i in range(n):
                sends.append(rcopy(i, j, x_refs[i].at[mine], o_refs[i].at[mine, me], (*chip, c)))
                sends[-1].start()
        for j, (cx, cy) in enumerate(chips):
            for i in range(n):
                landed = o_refs[i].at[mine, 2 * cx + cy]
                rcopy(i, j, landed, landed, sibling).wait_recv()
                sends.append(rcopy(i, 3 + j, landed, landed, sibling))
                sends[-1].start()
        for j, (cx, cy) in enumerate(chips):
            for i in range(n):
                passed = o_refs[i].at[theirs, 2 * cx + cy]
                rcopy(i, 3 + j, passed, passed, sibling).wait_recv()
        for cp in sends:
            cp.wait_send()
        for cp in local:
            cp.wait()

    return pl.pallas_call(
        body, name="all_gather_weights", in_specs=[ANY] * n, out_specs=[ANY] * n,
        out_shape=[jax.ShapeDtypeStruct((DEPTH, N_CHIPS) + t.shape[1:], t.dtype) for t in xs],
        scratch_shapes=[pltpu.SemaphoreType.DMA((6 * n,)), pltpu.SemaphoreType.DMA((6 * n,)),
                        pltpu.SemaphoreType.DMA((n,))],
        compiler_params=pltpu.CompilerParams(has_side_effects=True),
    )(*xs)


def _swap_halves(gs):
    n = len(gs)

    def body(*refs):
        g_refs, b_refs = refs[:n], refs[n:2 * n]
        send_sems, recv_sems = refs[2 * n:]
        x, y, c, _, _ = _place()
        copies = []
        for i in range(n):
            rh = gs[i].shape[1] // 2
            src = g_refs[i].at[:, pl.ds((1 - c) * rh, rh), :]
            copies.append(pltpu.make_async_remote_copy(src_ref=src, dst_ref=b_refs[i], send_sem=send_sems.at[i],
                                                       recv_sem=recv_sems.at[i], device_id=(x, y, 1 - c),
                                                       device_id_type=MESH_IDS))
            copies[-1].start()
        for cp in copies:
            cp.wait()

    return pl.pallas_call(
        body, name="grad_swap_halves", in_specs=[ANY] * n, out_specs=[ANY] * n,
        out_shape=[jax.ShapeDtypeStruct((t.shape[0], t.shape[1] // 2, t.shape[2]), t.dtype) for t in gs],
        scratch_shapes=[pltpu.SemaphoreType.DMA((n,)), pltpu.SemaphoreType.DMA((n,))],
        compiler_params=pltpu.CompilerParams(has_side_effects=True),
    )(*gs)


def _half_tile(rh, cols):
    return _pick(rh, max(16, (1 << 18) // cols // 16 * 16), 16)


def _chip_partial(g, b, pos, name):
    nchip, r, cols = g.shape
    rh = r // 2
    tr = _half_tile(rh, cols)
    nt = rh // tr

    def body(pos_ref, g_ref, b_ref, o_ref):
        o_ref[...] = (g_ref[...] + b_ref[...]).astype(BF16)

    return pl.pallas_call(
        body, name=name,
        grid_spec=pltpu.PrefetchScalarGridSpec(
            num_scalar_prefetch=1, grid=(nchip, nt),
            in_specs=[pl.BlockSpec((None, tr, cols), lambda j, i, pos: (j, pos[1] * nt + i, 0)),
                      pl.BlockSpec((None, tr, cols), lambda j, i, pos: (j, i, 0))],
            out_specs=pl.BlockSpec((None, tr, cols), lambda j, i, pos: (j, i, 0))),
        out_shape=jax.ShapeDtypeStruct((nchip, rh, cols), BF16),
        compiler_params=_params(("parallel", "parallel")),
    )(pos, g, b)


def _scatter_partials(ps):
    n = len(ps)

    def body(*refs):
        p_refs, r_refs = refs[:n], refs[n:2 * n]
        send_sems, recv_sems = refs[2 * n:]
        x, y, c, _, chips = _place()
        copies = []
        for j, (cx, cy) in enumerate(chips):
            for i in range(n):
                copies.append(pltpu.make_async_remote_copy(
                    src_ref=p_refs[i].at[2 * cx + cy], dst_ref=r_refs[i].at[j], send_sem=send_sems.at[3 * i + j],
                    recv_sem=recv_sems.at[3 * i + j], device_id=(cx, cy, c), device_id_type=MESH_IDS))
                copies[-1].start()
        for cp in copies:
            cp.wait()

    return pl.pallas_call(
        body, name="grad_scatter", in_specs=[ANY] * n, out_specs=[ANY] * n,
        out_shape=[jax.ShapeDtypeStruct((3,) + t.shape[1:], t.dtype) for t in ps],
        scratch_shapes=[pltpu.SemaphoreType.DMA((3 * n,)), pltpu.SemaphoreType.DMA((3 * n,))],
        compiler_params=pltpu.CompilerParams(has_side_effects=True),
    )(*ps)


def _reduce_own(g, b, rcv, pos, name):
    nchip, r, cols = g.shape
    rh = r // 2
    tr = _half_tile(rh, cols)
    nt = rh // tr

    def body(pos_ref, g_ref, b_ref, r_ref, o_ref):
        acc = g_ref[...] + b_ref[...]
        for j in range(3):
            acc = acc + r_ref[j].astype(F32)
        o_ref[...] = acc

    return pl.pallas_call(
        body, name=name,
        grid_spec=pltpu.PrefetchScalarGridSpec(
            num_scalar_prefetch=1, grid=(nt,),
            in_specs=[pl.BlockSpec((None, tr, cols), lambda i, pos: (pos[0], pos[1] * nt + i, 0)),
                      pl.BlockSpec((None, tr, cols), lambda i, pos: (pos[0], i, 0)),
                      pl.BlockSpec((3, tr, cols), lambda i, pos: (0, i, 0))],
            out_specs=pl.BlockSpec((tr, cols), lambda i, pos: (i, 0))),
        out_shape=jax.ShapeDtypeStruct((rh, cols), F32),
        compiler_params=_params(("parallel",)),
    )(pos, g, b, rcv)


def _share_halves(fs):
    n = len(fs)

    def body(*refs):
        f_refs, o_refs = refs[:n], refs[n:2 * n]
        send_sems, recv_sems, local_sems = refs[2 * n:]
        x, y, c, _, _ = _place()
        copies, local = [], []
        for i in range(n):
            rh = fs[i].shape[0]
            rows = o_refs[i].at[pl.ds(c * rh, rh), :]
            local.append(pltpu.make_async_copy(f_refs[i], rows, local_sems.at[i]))
            local[-1].start()
            copies.append(pltpu.make_async_remote_copy(src_ref=f_refs[i], dst_ref=rows, send_sem=send_sems.at[i],
                                                       recv_sem=recv_sems.at[i], device_id=(x, y, 1 - c),
                                                       device_id_type=MESH_IDS))
            copies[-1].start()
        for i in range(n):
            rh = fs[i].shape[0]
            theirs = o_refs[i].at[pl.ds((1 - c) * rh, rh), :]
            pltpu.make_async_remote_copy(src_ref=f_refs[i], dst_ref=theirs, send_sem=send_sems.at[i],
                                         recv_sem=recv_sems.at[i], device_id=(x, y, 1 - c),
                                         device_id_type=MESH_IDS).wait()
        for cp in local:
            cp.wait()

    return pl.pallas_call(
        body, name="grad_share_halves", in_specs=[ANY] * n, out_specs=[ANY] * n,
        out_shape=[jax.ShapeDtypeStruct((2 * t.shape[0], t.shape[1]), t.dtype) for t in fs],
        scratch_shapes=[pltpu.SemaphoreType.DMA((n,)), pltpu.SemaphoreType.DMA((n,)), pltpu.SemaphoreType.DMA((n,))],
        compiler_params=pltpu.CompilerParams(has_side_effects=True),
    )(*fs)


def _reduce_scatter_layer(slices, pos, l):
    names = list(slices)
    gs = [slices[k] for k in names]
    bs = _swap_halves(gs)
    ps = [_chip_partial(g, b, pos, "grad_chip_partial_" + k) for k, g, b in zip(names, gs, bs)]
    rs = _scatter_partials(ps)
    fs = [_reduce_own(g, b, r, pos, "grad_reduce_own_" + k) for k, g, b, r in zip(names, gs, bs, rs)]
    return dict(zip(names, _share_halves(fs)))


def _all_reduce_small(packed):
    rows = packed.shape[0]
    n_dev = 2 * N_CHIPS
    tr = _pick(rows, 512, 8)

    def body(x_ref, o_ref, buf, send_sems, recv_sems):
        x, y, c, _, _ = _place()
        me = 4 * x + 2 * y + c
        copies = []
        for px in range(2):
            for py in range(2):
                for pc in range(2):
                    peer = 4 * px + 2 * py + pc
                    cp = pltpu.make_async_remote_copy(src_ref=x_ref, dst_ref=buf.at[me], send_sem=send_sems.at[peer],
                                                      recv_sem=recv_sems.at[me], device_id=(px, py, pc),
                                                      device_id_type=MESH_IDS)

                    @pl.when(peer != me)
                    def _(cp=cp):
                        cp.start()
                    copies.append((peer, cp))
        buf[me] = x_ref[...]
        for peer, cp in copies:
            @pl.when(peer != me)
            def _(cp=cp, peer=peer):
                cp.wait_send()
                pltpu.make_async_remote_copy(src_ref=x_ref, dst_ref=buf.at[peer], send_sem=send_sems.at[peer],
                                             recv_sem=recv_sems.at[peer], device_id=(x, y, c),
                                             device_id_type=MESH_IDS).wait_recv()

        @pl.loop(0, rows // tr)
        def _(t):
            sl = pl.ds(pl.multiple_of(t * tr, 8), tr)
            acc = buf[0, sl, :]
            for s in range(1, n_dev):
                acc = acc + buf[s, sl, :]
            o_ref[sl, :] = acc

    vm = pl.BlockSpec(memory_space=pltpu.VMEM)
    return pl.pallas_call(
        body, name="all_reduce_small", in_specs=[vm], out_specs=vm,
        out_shape=jax.ShapeDtypeStruct(packed.shape, F32),
        scratch_shapes=[pltpu.VMEM((n_dev, rows, LANES), F32), pltpu.SemaphoreType.DMA((n_dev,)),
                        pltpu.SemaphoreType.DMA((n_dev,))],
        compiler_params=pltpu.CompilerParams(vmem_limit_bytes=VMEM_LIMIT, has_side_effects=True),
    )(packed)


_WEIGHTS = ('norm_mix_g', 'w_in', 'qkv_conv_w', 'a_log', 'dt_bias', 'gdn_norm_g', 'w_branch_a', 'sgu_ln_g',
            'sgu_ln_b', 'sgu_w', 'sgu_b', 'w_branch_b', 'w_out', 'norm_ffn_g', 'w_up', 'ffn_conv_w', 'ffn_conv_b',
            'w_down', 'final_norm_g')


def _local_step(x, target, big, small):
    lws, saves = [], []
    for l in range(DEPTH):
        lw = _layer_weights(l, big, small)
        x, s = _layer_fwd(x, lw)
        lws.append(lw)
        saves.append(s)
    loss, dx, dx_bf, d_final = _loss_head(x, small['final_norm_g'].reshape(1, -1), target)
    grads = [None] * DEPTH
    for l in reversed(range(DEPTH)):
        dx, dx_bf, grads[l] = _layer_bwd(dx, dx_bf, lws[l], saves[l])
    return loss, dx, grads, d_final


def _pack_small(grads, d_final):
    parts = [grads[l][k].reshape(-1) for l in range(DEPTH) for k in _SMALL_GRADS] + [d_final.reshape(-1)]
    flat = jnp.concatenate(parts)
    rows = -(-flat.size // (8 * LANES)) * 8
    return jnp.pad(flat, (0, rows * LANES - flat.size)).reshape(rows, LANES), [p.size for p in parts]


def _unpack_small(packed, grads, d_final):
    flat = packed.reshape(-1)
    out, off = [], 0
    for l in range(DEPTH):
        d = {}
        for k in _SMALL_GRADS:
            t = grads[l][k]
            d[k] = flat[off:off + t.size].reshape(t.shape)
            off += t.size
        out.append(d)
    return out, flat[off:off + d_final.size].reshape(d_final.shape)


def kernel(x, norm_mix_g, w_in, qkv_conv_w, a_log, dt_bias, gdn_norm_g, w_branch_a, sgu_ln_g, sgu_ln_b, sgu_w, sgu_b, w_branch_b, w_out, norm_ffn_g, w_up, ffn_conv_w, ffn_conv_b, w_down, final_norm_g, loss_target, m_norm_mix_g, m_w_in, m_qkv_conv_w, m_a_log, m_dt_bias, m_gdn_norm_g, m_w_branch_a, m_sgu_ln_g, m_sgu_ln_b, m_sgu_w, m_sgu_b, m_w_branch_b, m_w_out, m_norm_ffn_g, m_w_up, m_ffn_conv_w, m_ffn_conv_b, m_w_down, m_final_norm_g, v_norm_mix_g, v_w_in, v_qkv_conv_w, v_a_log, v_dt_bias, v_gdn_norm_g, v_w_branch_a, v_sgu_ln_g, v_sgu_ln_b, v_sgu_w, v_sgu_b, v_w_branch_b, v_w_out, v_norm_ffn_g, v_w_up, v_ffn_conv_w, v_ffn_conv_b, v_w_down, v_final_norm_g):
    w = dict(norm_mix_g=norm_mix_g, w_in=w_in, qkv_conv_w=qkv_conv_w, a_log=a_log, dt_bias=dt_bias,
             gdn_norm_g=gdn_norm_g, w_branch_a=w_branch_a, sgu_ln_g=sgu_ln_g, sgu_ln_b=sgu_ln_b, sgu_w=sgu_w,
             sgu_b=sgu_b, w_branch_b=w_branch_b, w_out=w_out, norm_ffn_g=norm_ffn_g, w_up=w_up,
             ffn_conv_w=ffn_conv_w, ffn_conv_b=ffn_conv_b, w_down=w_down, final_norm_g=final_norm_g)
    m = dict(norm_mix_g=m_norm_mix_g, w_in=m_w_in, qkv_conv_w=m_qkv_conv_w, a_log=m_a_log, dt_bias=m_dt_bias,
             gdn_norm_g=m_gdn_norm_g, w_branch_a=m_w_branch_a, sgu_ln_g=m_sgu_ln_g, sgu_ln_b=m_sgu_ln_b,
             sgu_w=m_sgu_w, sgu_b=m_sgu_b, w_branch_b=m_w_branch_b, w_out=m_w_out, norm_ffn_g=m_norm_ffn_g,
             w_up=m_w_up, ffn_conv_w=m_ffn_conv_w, ffn_conv_b=m_ffn_conv_b, w_down=m_w_down,
             final_norm_g=m_final_norm_g)
    v = dict(norm_mix_g=v_norm_mix_g, w_in=v_w_in, qkv_conv_w=v_qkv_conv_w, a_log=v_a_log, dt_bias=v_dt_bias,
             gdn_norm_g=v_gdn_norm_g, w_branch_a=v_w_branch_a, sgu_ln_g=v_sgu_ln_g, sgu_ln_b=v_sgu_ln_b,
             sgu_w=v_sgu_w, sgu_b=v_sgu_b, w_branch_b=v_w_branch_b, w_out=v_w_out, norm_ffn_g=v_norm_ffn_g,
             w_up=v_w_up, ffn_conv_w=v_ffn_conv_w, ffn_conv_b=v_ffn_conv_b, w_down=v_w_down,
             final_norm_g=v_final_norm_g)
    chip = 2 * lax.axis_index("x") + lax.axis_index("y")
    pos = jnp.stack([chip, lax.axis_index("c")]).astype(jnp.int32)

    sharded = [w[k].astype(BF16) for k in _BIG] + [w['qkv_conv_w'], w['ffn_conv_w']]
    gathered = _all_gather_chips(sharded)
    big = dict(zip(_BIG, gathered[:len(_BIG)]))
    small = dict(w, qkv_conv_w=gathered[-2], ffn_conv_w=gathered[-1])

    loss, grad_x, grads, d_final = _local_step(x[0], loss_target[0], big, small)
    loss = lax.psum(loss[0, 0], ("x", "y", "c"))

    big_g = {k: [None] * DEPTH for k in _BIG}
    for l in range(DEPTH):
        red = _reduce_scatter_layer(_big_grad_slices(grads[l]), pos, l)
        for k in _BIG:
            big_g[k][l] = red[k]
    packed, _ = _pack_small(grads, d_final)
    small_g, d_final = _unpack_small(_all_reduce_small(packed), grads, d_final)

    def stack(k):
        return jnp.stack([small_g[l][k] for l in range(DEPTH)])

    nd = 2 * GDN_HEADS
    g_out = {k: jnp.stack(big_g[k]) for k in _BIG}
    for k in ('norm_mix_g', 'gdn_norm_g', 'sgu_ln_g', 'sgu_ln_b', 'norm_ffn_g', 'ffn_conv_b'):
        g_out[k] = stack(k).reshape(w[k].shape)
    g_out['sgu_w'] = stack('sgu_w')
    g_out['a_log'] = stack('a_log')[:, 0, :nd].reshape(w['a_log'].shape)
    g_out['dt_bias'] = stack('dt_bias')[:, 0, :nd].reshape(w['dt_bias'].shape)
    g_out['sgu_b'] = jnp.swapaxes(stack('sgu_bt')[:, :, :SGU_GROUPS], 1, 2)
    for k in ('qkv_conv_w', 'ffn_conv_w'):
        full = stack(k)
        width = w[k].shape[-1]
        g_out[k] = lax.dynamic_slice_in_dim(full, chip * width, width, axis=2)
    g_out['final_norm_g'] = d_final.reshape(w['final_norm_g'].shape)

    deltas, new_m, new_v = {}, {}, {}
    for k in _WEIGHTS:
        deltas[k], new_m[k], new_v[k] = _adamw(w[k], g_out[k], m[k], v[k], "adamw_" + k)
    return (loss, grad_x[None], *[g_out[k] for k in _WEIGHTS], *[deltas[k] for k in _WEIGHTS],
            *[new_m[k] for k in _WEIGHTS], *[new_v[k] for k in _WEIGHTS])
```

```python
import functools
import math

import jax
import jax.numpy as jnp
from jax import lax
from jax.experimental import pallas as pl
from jax.experimental.pallas import tpu as pltpu

F32 = jnp.float32
BF16 = jnp.bfloat16

D_MODEL = 2048
SEQ = 2048
DEPTH = 4
GDN_HEADS = 8
HEAD_DIM = 128
GDN_CHUNK = 64
QKV_CONV = 5
SGU_GROUPS = 8
SGU_DIM = 128
SGU_BLOCK = 128
D_FF = 5632
FFN_CONV = 3
NORM_EPS = 1e-6
N_CHIPS = 4

ADAM_LR = 0.001
ADAM_B1 = 0.9
ADAM_B2 = 0.999
ADAM_EPS = 1e-08
ADAM_WD = 0.01
ADAM_STEP = 10

LANES = 128
VMEM_LIMIT = 56 * 1024 * 1024
AB_PAD = LANES


def _gw():
    return GDN_HEADS * HEAD_DIM


def _sw():
    return SGU_GROUPS * SGU_DIM


def _n_all():
    return 4 * _gw() + 2 * _sw() + 2 * D_MODEL + AB_PAD


def _pick(n, target, align=LANES):
    if n <= target:
        return n
    best = None
    t = align
    while t <= target:
        if n % t == 0:
            best = t
        t += align
    assert best is not None, (n, target, align)
    return best


def _params(sem=None):
    return pltpu.CompilerParams(dimension_semantics=sem, vmem_limit_bytes=VMEM_LIMIT)


def _dot(a, b, dims, hi):
    dn = (dims, ((), ()))
    if not hi:
        return lax.dot_general(a.astype(BF16), b.astype(BF16), dn, preferred_element_type=F32)
    a_hi, b_hi = a.astype(BF16), b.astype(BF16)
    a_lo = (a - a_hi.astype(F32)).astype(BF16)
    b_lo = (b - b_hi.astype(F32)).astype(BF16)
    d = lambda p, q: lax.dot_general(p, q, dn, preferred_element_type=F32)
    return d(a_hi, b_hi) + (d(a_hi, b_lo) + d(a_lo, b_hi))


def _make_mm(hi):
    @jax.custom_vjp
    def mm(a, b):
        return _dot(a, b, ((1,), (0,)), hi)

    @jax.custom_vjp
    def mm_nt(a, b):
        return _dot(a, b, ((1,), (1,)), hi)

    @jax.custom_vjp
    def mm_tn(a, b):
        return _dot(a, b, ((0,), (0,)), hi)

    mm.defvjp(lambda a, b: (mm(a, b), (a, b)), lambda r, g: (mm_nt(g, r[1]), mm_tn(r[0], g)))
    mm_nt.defvjp(lambda a, b: (mm_nt(a, b), (a, b)), lambda r, g: (mm(g, r[1]), mm_tn(g, r[0])))
    mm_tn.defvjp(lambda a, b: (mm_tn(a, b), (a, b)), lambda r, g: (mm_nt(r[1], g), mm(r[0], g)))
    return mm, mm_nt, mm_tn


_mm, _mm_nt, _mm_tn = _make_mm(False)
_mmh, _mmh_nt, _mmh_tn = _make_mm(True)


def _shift_rows_raw(x, s):
    if s == 0:
        return x
    n = x.shape[0]
    rolled = pltpu.roll(x, (-s) % n, 0)
    t = lax.broadcasted_iota(jnp.int32, x.shape, 0)
    ok = (t + s >= 0) & (t + s < n)
    return jnp.where(ok, rolled, 0.0)


@functools.partial(jax.custom_vjp, nondiff_argnums=(1,))
def _shift_rows(x, s):
    return _shift_rows_raw(x, s)


_shift_rows.defvjp(lambda x, s: (_shift_rows_raw(x, s), None), lambda s, _, g: (_shift_rows_raw(g, -s),))


def _sigmoid(x):
    return 1.0 / (1.0 + jnp.exp(-x))


def _silu(x):
    return x * _sigmoid(x)


def _gelu(x):
    return 0.5 * x * (1.0 + jnp.tanh(math.sqrt(2.0 / math.pi) * (x + 0.044715 * x * x * x)))


def _softplus(x):
    return jnp.maximum(x, 0.0) + jnp.log(1.0 + jnp.exp(-jnp.abs(x)))


def _rms(x, g):
    return x * lax.rsqrt(jnp.mean(x * x, axis=-1, keepdims=True) + NORM_EPS) * g


def _conv_rows(x, taps):
    pad = len(taps) // 2
    acc = None
    for j, w in enumerate(taps):
        term = _shift_rows(x, j - pad) * w
        acc = term if acc is None else acc + term
    return acc


@jax.custom_vjp
def _inv_unit(a):
    n = a.shape[0]
    eye = (lax.broadcasted_iota(jnp.int32, (n, n), 0) == lax.broadcasted_iota(jnp.int32, (n, n), 1)).astype(F32)
    p = eye - a
    ak = a
    for _ in range(int(math.log2(n)) - 1):
        ak = _mmh(ak, ak)
        p = p + _mmh(p, ak)
    return p


def _inv_unit_fwd(a):
    t = _inv_unit(a)
    return t, t


def _inv_unit_bwd(t, g):
    return (-_mmh_tn(t, _mmh_nt(g, t)),)


_inv_unit.defvjp(_inv_unit_fwd, _inv_unit_bwd)


def _tiled(fn, name, grid, ins, outs, sem=None):
    n_in = len(ins)

    def body(*refs):
        vals = [r[...] for r in refs[:n_in]]
        res = fn(*vals)
        if not isinstance(res, (tuple, list)):
            res = (res,)
        for o_ref, r, spec in zip(refs[n_in:], res, outs):
            acc = spec[4]
            if not acc:
                o_ref[...] = r.astype(o_ref.dtype)
            else:
                first = functools.reduce(jnp.logical_and, [pl.program_id(a) == 0 for a in acc])

                @pl.when(first)
                def _(o_ref=o_ref, r=r):
                    o_ref[...] = r.astype(o_ref.dtype)

                @pl.when(jnp.logical_not(first))
                def _(o_ref=o_ref, r=r):
                    o_ref[...] += r.astype(o_ref.dtype)

    return pl.pallas_call(
        body, name=name, grid=grid,
        in_specs=[pl.BlockSpec(b, m) for _, b, m in ins],
        out_specs=[pl.BlockSpec(s[2], s[3]) for s in outs],
        out_shape=[jax.ShapeDtypeStruct(s[0], s[1]) for s in outs],
        compiler_params=_params(sem),
    )(*[a for a, _, _ in ins])


def _matmul(a, b, mode, out_dtype, name, add=None, a_split=1, b_split=1, o_split=1):
    def dims2(x, split):
        return (x.shape[-2], x.shape[-1] * split)

    ar, ac = dims2(a, a_split)
    br, bc = dims2(b, b_split)
    if mode == 'nn':
        M, K, N = ar, ac, bc
        assert br == K
    elif mode == 'nt':
        M, K, N = ar, ac, br
        assert bc == K
    else:
        K, M, N = ar, ac, bc
        assert br == K
    tm = _pick(M // (a_split if mode == 'tn' else 1), 512)
    tn = _pick(N // max(o_split, b_split if mode != 'nt' else 1), 1408)
    tk = _pick(K // max(a_split if mode != 'tn' else 1, b_split if mode == 'nt' else 1), 2048)
    gm, gn, gk = M // tm, N // tn, K // tk

    def col_map(split, total_cols, tile):
        per = total_cols // split // tile

        def f(r, c):
            return (c // per, r, c % per) if split > 1 else (r, c)
        return f

    if mode == 'nn':
        a_idx = col_map(a_split, K, tk)
        b_idx = col_map(b_split, N, tn)
        a_spec = pl.BlockSpec(((None,) if a_split > 1 else ()) + (tm, tk), lambda m, n, k: a_idx(m, k))
        b_spec = pl.BlockSpec(((None,) if b_split > 1 else ()) + (tk, tn), lambda m, n, k: b_idx(k, n))
        dn = ((1,), (0,))
    elif mode == 'nt':
        a_idx = col_map(a_split, K, tk)
        b_idx = col_map(b_split, K, tk)
        a_spec = pl.BlockSpec(((None,) if a_split > 1 else ()) + (tm, tk), lambda m, n, k: a_idx(m, k))
        b_spec = pl.BlockSpec(((None,) if b_split > 1 else ()) + (tn, tk), lambda m, n, k: b_idx(n, k))
        dn = ((1,), (1,))
    else:
        a_idx = col_map(a_split, M, tm)
        b_idx = col_map(b_split, N, tn)
        a_spec = pl.BlockSpec(((None,) if a_split > 1 else ()) + (tk, tm), lambda m, n, k: a_idx(k, m))
        b_spec = pl.BlockSpec(((None,) if b_split > 1 else ()) + (tk, tn), lambda m, n, k: b_idx(k, n))
        dn = ((0,), (0,))
    o_idx = col_map(o_split, N, tn)
    o_block = ((None,) if o_split > 1 else ()) + (tm, tn)
    o_spec = pl.BlockSpec(o_block, lambda m, n, k: o_idx(m, n))
    o_shape = ((o_split, M, N // o_split) if o_split > 1 else (M, N))
    in_specs = [a_spec, b_spec]
    args = [a, b]
    if add is not None:
        in_specs.append(pl.BlockSpec((tm, tn), lambda m, n, k: (m, n)))
        args.append(add)

    def body(*refs):
        a_ref, b_ref = refs[0], refs[1]
        add_ref = refs[2] if add is not None else None
        o_ref = refs[3] if add is not None else refs[2]
        part = lax.dot_general(a_ref[...].astype(BF16), b_ref[...].astype(BF16), (dn, ((), ())),
                               preferred_element_type=F32)

        def finish(total):
            if add_ref is not None:
                total = total + add_ref[...]
            o_ref[...] = total.astype(o_ref.dtype)

        if gk == 1:
            finish(part)
        else:
            acc_ref = refs[-1]
            k = pl.program_id(2)

            @pl.when(k == 0)
            def _():
                acc_ref[...] = part

            @pl.when(jnp.logical_and(k > 0, k < gk - 1))
            def _():
                acc_ref[...] += part

            @pl.when(k == gk - 1)
            def _():
                finish(acc_ref[...] + part)

    return pl.pallas_call(
        body, name=name, grid=(gm, gn, gk), in_specs=in_specs, out_specs=o_spec,
        out_shape=jax.ShapeDtypeStruct(o_shape, out_dtype),
        scratch_shapes=([pltpu.VMEM((tm, tn), F32)] if gk > 1 else []),
        compiler_params=_params(("parallel", "parallel", "arbitrary")),
    )(*args)


def _row_tile():
    return _pick(SEQ, 256, 8)


def _rms_fwd(x, g, name):
    tm = _row_tile()
    (h,) = _tiled(lambda xv, gv: _rms(xv, gv), name, (SEQ // tm,),
                  [(x, (tm, D_MODEL), lambda i: (i, 0)), (g, (1, D_MODEL), lambda i: (0, 0))],
                  [((SEQ, D_MODEL), BF16, (tm, D_MODEL), lambda i: (i, 0), ())])
    return h


def _rms_bwd(x, g, dh, dres, name):
    tm = _row_tile()

    def fn(xv, gv, dhv, drv):
        _, vjp = jax.vjp(_rms, xv, gv)
        dx, dg = vjp(dhv)
        dx = dx + drv
        return dx, dx, dg

    row = lambda i: (i, 0)
    return _tiled(fn, name, (SEQ // tm,),
                  [(x, (tm, D_MODEL), row), (g, (1, D_MODEL), lambda i: (0, 0)),
                   (dh, (tm, D_MODEL), row), (dres, (tm, D_MODEL), row)],
                  [((SEQ, D_MODEL), F32, (tm, D_MODEL), row, ()),
                   ((SEQ, D_MODEL), BF16, (tm, D_MODEL), row, ()),
                   ((1, D_MODEL), F32, (1, D_MODEL), lambda i: (0, 0), (0,))])


def _loss_head(x, g, target):
    tm = _row_tile()

    def fn(xv, gv, tv):
        y, vjp = jax.vjp(_rms, xv, gv)
        err = y - tv
        part = 0.5 * jnp.sum(jnp.sum(err * err, axis=1, keepdims=True), axis=0, keepdims=True) / D_MODEL
        dx, dg = vjp(err / D_MODEL)
        return part, dx, dx, dg

    row = lambda i: (i, 0)
    one = lambda i: (0, 0)
    return _tiled(fn, "loss_head", (SEQ // tm,),
                  [(x, (tm, D_MODEL), row), (g, (1, D_MODEL), one), (target, (tm, D_MODEL), row)],
                  [((1, 1), F32, (1, 1), one, (0,)),
                   ((SEQ, D_MODEL), F32, (tm, D_MODEL), row, ()),
                   ((SEQ, D_MODEL), BF16, (tm, D_MODEL), row, ()),
                   ((1, D_MODEL), F32, (1, D_MODEL), one, (0,))])


def _p_col(width, index):
    return lambda tm: ((tm, width), lambda i: (i, index))


def _merge_fn(ga, gb, ta, tb):
    return _sigmoid(ga) * ta + _sigmoid(gb) * tb


def _merge_fwd(p, ta, tb):
    tm = _row_tile()
    d = D_MODEL
    ga_blk = 4 * _gw() + 2 * _sw()
    assert ga_blk % d == 0
    ia, ib = ga_blk // d, ga_blk // d + 1
    row = lambda i: (i, 0)
    (m,) = _tiled(_merge_fn, "merge_fwd", (SEQ // tm,),
                  [(p, (tm, d), lambda i: (i, ia)), (p, (tm, d), lambda i: (i, ib)),
                   (ta, (tm, d), row), (tb, (tm, d), row)],
                  [((SEQ, d), BF16, (tm, d), row, ())])
    return m


def _merge_bwd(p, ta, tb, dm):
    tm = _row_tile()
    d = D_MODEL
    ga_blk = 4 * _gw() + 2 * _sw()
    ia, ib = ga_blk // d, ga_blk // d + 1

    def fn(ga, gb, tav, tbv, dmv):
        _, vjp = jax.vjp(_merge_fn, ga, gb, tav, tbv)
        return vjp(dmv)

    row = lambda i: (i, 0)
    out = ((SEQ, d), BF16, (tm, d), row, ())
    return _tiled(fn, "merge_bwd", (SEQ // tm,),
                  [(p, (tm, d), lambda i: (i, ia)), (p, (tm, d), lambda i: (i, ib)),
                   (ta, (tm, d), row), (tb, (tm, d), row), (dm, (tm, d), row)],
                  [out, out, out, out])


def _gate_fn(ab, alog, dtb):
    lane = lax.broadcasted_iota(jnp.int32, ab.shape, 1)
    nd = 2 * GDN_HEADS
    g = -jnp.exp(alog) * _softplus(ab + dtb)
    beta = _sigmoid(ab)
    return jnp.where(lane < nd, g, jnp.where(lane < 2 * nd, beta, 0.0))


def _ab_index():
    off = 4 * _gw() + 2 * _sw() + 2 * D_MODEL
    assert off % AB_PAD == 0
    return off // AB_PAD


def _gate_fwd(p, alog, dtb):
    tm = _row_tile()
    iab = _ab_index()
    one = lambda i: (0, 0)
    (g,) = _tiled(_gate_fn, "gdn_gate_fwd", (SEQ // tm,),
                  [(p, (tm, AB_PAD), lambda i: (i, iab)), (alog, (1, AB_PAD), one), (dtb, (1, AB_PAD), one)],
                  [((SEQ, AB_PAD), F32, (tm, AB_PAD), lambda i: (i, 0), ())])
    return g


def _gate_bwd(p, alog, dtb, dg):
    tm = _row_tile()
    iab = _ab_index()
    one = lambda i: (0, 0)

    def fn(ab, al, db, dgv):
        _, vjp = jax.vjp(_gate_fn, ab, al, db)
        return vjp(dgv)

    return _tiled(fn, "gdn_gate_bwd", (SEQ // tm,),
                  [(p, (tm, AB_PAD), lambda i: (i, iab)), (alog, (1, AB_PAD), one), (dtb, (1, AB_PAD), one),
                   (dg, (tm, AB_PAD), lambda i: (i, 0))],
                  [((SEQ, AB_PAD), BF16, (tm, AB_PAD), lambda i: (i, 0), ()),
                   ((1, AB_PAD), F32, (1, AB_PAD), one, (0,)),
                   ((1, AB_PAD), F32, (1, AB_PAD), one, (0,))])


def _l2n(x):
    return x * lax.rsqrt(jnp.sum(x * x, axis=-1, keepdims=True) + NORM_EPS)


def _qkv_fn(xq, xk, xv, *taps):
    n = QKV_CONV
    q = _l2n(_silu(_conv_rows(xq, taps[0:n])))
    k = _l2n(_silu(_conv_rows(xk, taps[n:2 * n])))
    v = _silu(_conv_rows(xv, taps[2 * n:3 * n]))
    return q, k, v


def _qkv_specs(p, conv_w):
    hd, nh = HEAD_DIM, GDN_HEADS
    ins = [(p, (SEQ, hd), (lambda h, s=s: (0, s * nh + h))) for s in range(3)]
    ins += [(conv_w, (QKV_CONV, hd), (lambda h, s=s: (0, s * nh + h))) for s in range(3)]
    return ins


def _qkv_fwd(p, conv_w):
    hd, nh = HEAD_DIM, GDN_HEADS
    ins = _qkv_specs(p, conv_w)
    n_in = len(ins)
    out_spec = pl.BlockSpec((SEQ, hd), lambda h: (0, h))

    def body(*refs):
        xs = [r[...] for r in refs[:3]]
        taps = [refs[3 + s][j:j + 1, :] for s in range(3) for j in range(QKV_CONV)]
        q, k, v = _qkv_fn(*xs, *taps)
        refs[n_in][...] = q
        refs[n_in + 1][...] = k
        refs[n_in + 2][...] = v

    return pl.pallas_call(
        body, name="gdn_qkv_fwd", grid=(nh,),
        in_specs=[pl.BlockSpec(b, m) for _, b, m in ins], out_specs=[out_spec] * 3,
        out_shape=[jax.ShapeDtypeStruct((SEQ, nh * hd), F32)] * 3,
        compiler_params=_params(("parallel",)),
    )(*[a for a, _, _ in ins])


def _qkv_bwd(p, conv_w, dq, dk, dv):
    hd, nh = HEAD_DIM, GDN_HEADS
    ins = _qkv_specs(p, conv_w) + [(t, (SEQ, hd), lambda h: (0, h)) for t in (dq, dk, dv)]
    n_in = len(ins)

    def body(*refs):
        xs = [r[...] for r in refs[:3]]
        taps = [refs[3 + s][j:j + 1, :] for s in range(3) for j in range(QKV_CONV)]
        cts = tuple(r[...] for r in refs[6:9])
        _, vjp = jax.vjp(_qkv_fn, *xs, *taps)
        grads = vjp(cts)
        for s in range(3):
            refs[n_in + s][...] = grads[s].astype(BF16)
            for j in range(QKV_CONV):
                refs[n_in + 3 + s][j:j + 1, :] = jnp.sum(grads[3 + s * QKV_CONV + j], axis=0, keepdims=True)

    dx_spec = pl.BlockSpec((SEQ, hd), lambda h: (0, h))
    dw_spec = pl.BlockSpec((QKV_CONV, hd), lambda h: (0, h))
    outs = pl.pallas_call(
        body, name="gdn_qkv_bwd", grid=(nh,),
        in_specs=[pl.BlockSpec(b, m) for _, b, m in ins], out_specs=[dx_spec] * 3 + [dw_spec] * 3,
        out_shape=[jax.ShapeDtypeStruct((SEQ, nh * hd), BF16)] * 3
        + [jax.ShapeDtypeStruct((QKV_CONV, nh * hd), F32)] * 3,
        compiler_params=_params(("parallel",)),
    )(*[a for a, _, _ in ins])
    return outs[0:3], jnp.concatenate(outs[3:6], axis=1)


def _prep_fn(q, k, v, gblk, head):
    c = q.shape[0]
    scale = HEAD_DIM ** -0.5
    ii = lax.broadcasted_iota(jnp.int32, (c, c), 0)
    jj = lax.broadcasted_iota(jnp.int32, (c, c), 1)
    eye = (ii == jj).astype(F32)
    lane = lax.broadcasted_iota(jnp.int32, gblk.shape, 1)
    kk_t = _mm_nt(k, k)
    qk_t = _mm_nt(q, k)
    outs = []
    for d in range(2):
        g = jnp.sum(jnp.where(lane == head + d * GDN_HEADS, gblk, 0.0), axis=1, keepdims=True)
        beta = jnp.sum(jnp.where(lane == head + (2 + d) * GDN_HEADS, gblk, 0.0), axis=1, keepdims=True)
        incl = (jj <= ii) if d == 0 else (jj >= ii)
        strict = (jj < ii) if d == 0 else (jj > ii)
        incl_t = (ii <= jj) if d == 0 else (ii >= jj)
        g_row = jnp.sum(eye * g, axis=0, keepdims=True)
        gc = jnp.sum(jnp.where(incl, g_row, 0.0), axis=1, keepdims=True)
        gc_row = jnp.sum(jnp.where(incl_t, g, 0.0), axis=0, keepdims=True)
        decay = jnp.where(incl, jnp.exp(jnp.where(incl, gc - gc_row, 0.0)), 0.0)
        a = jnp.where(strict, kk_t * beta * decay, 0.0)
        t = _inv_unit(a)
        egc = jnp.exp(gc)
        u = _mmh(t, v * beta)
        w = _mmh(t, k * (beta * egc))
        qk = jnp.where(incl, qk_t * (scale * decay), 0.0)
        g_last = jnp.sum(g, axis=0, keepdims=True)
        q_dec = q * (scale * egc)
        k_dec = k * jnp.exp(g_last - gc)
        dec = jnp.exp(g_last)
        outs += [u, w, q_dec, k_dec, qk, dec]
    return tuple(outs)


def _n_chunks():
    return SEQ // GDN_CHUNK


def _gdn_prep_fwd(q, k, v, gates):
    c, hd, nh, nc = GDN_CHUNK, HEAD_DIM, GDN_HEADS, _n_chunks()
    gw = nh * hd
    row = lambda n: (n, 0)

    def body(q_ref, k_ref, v_ref, g_ref, u_ref, w_ref, qd_ref, kd_ref, qk_ref, dec_ref):
        gblk = g_ref[...]
        for h in range(nh):
            cols = slice(h * hd, (h + 1) * hd)
            res = _prep_fn(q_ref[:, cols], k_ref[:, cols], v_ref[:, cols], gblk, h)
            for d in range(2):
                u, w, qd, kd, qk, dec = res[6 * d:6 * d + 6]
                u_ref[d, :, cols] = u
                w_ref[d, :, cols] = w
                qd_ref[d, :, cols] = qd
                kd_ref[d, :, cols] = kd
                qk_ref[d, h] = qk
                dec_ref[d, h:h + 1, :] = jnp.broadcast_to(dec, (1, LANES))

    wide = pl.BlockSpec((2, c, gw), lambda n: (0, n, 0))
    return pl.pallas_call(
        body, name="gdn_prep_fwd", grid=(nc,),
        in_specs=[pl.BlockSpec((c, gw), row)] * 3 + [pl.BlockSpec((c, LANES), row)],
        out_specs=[wide] * 4 + [pl.BlockSpec((2, nh, c, c), lambda n: (0, 0, n, 0)),
                                pl.BlockSpec((2, None, nh, LANES), lambda n: (0, n, 0, 0))],
        out_shape=[jax.ShapeDtypeStruct((2, SEQ, gw), F32)] * 4
        + [jax.ShapeDtypeStruct((2, nh, SEQ, c), F32), jax.ShapeDtypeStruct((2, nc, nh, LANES), F32)],
        compiler_params=_params(("parallel",)),
    )(q, k, v, gates)


def _gdn_prep_bwd(q, k, v, gates, cts):
    c, hd, nh, nc = GDN_CHUNK, HEAD_DIM, GDN_HEADS, _n_chunks()
    gw = nh * hd
    row = lambda n: (n, 0)

    def body(*refs):
        q_ref, k_ref, v_ref, g_ref = refs[:4]
        ct_refs = refs[4:16]
        dq_ref, dk_ref, dv_ref, dg_ref = refs[16:20]
        gblk = g_ref[...]
        dg_total = None
        for h in range(nh):
            cols = slice(h * hd, (h + 1) * hd)
            _, vjp = jax.vjp(lambda a, b, cc, g, h=h: _prep_fn(a, b, cc, g, h),
                             q_ref[:, cols], k_ref[:, cols], v_ref[:, cols], gblk)
            vals = []
            for d in range(2):
                r = ct_refs[6 * d:6 * d + 6]
                vals += [r[0][:, cols], r[1][:, cols], r[2][:, cols], r[3][:, cols], r[4][h], r[5][h:h + 1, 0:1]]
            dq, dk, dv, dg = vjp(tuple(vals))
            dq_ref[:, cols] = dq
            dk_ref[:, cols] = dk
            dv_ref[:, cols] = dv
            dg_total = dg if dg_total is None else dg_total + dg
        dg_ref[...] = dg_total

    one_dir = [pl.BlockSpec((c, gw), row)] * 4 + [pl.BlockSpec((nh, c, c), lambda n: (0, n, 0)),
                                                  pl.BlockSpec((None, nh, LANES), lambda n: (n, 0, 0))]
    return pl.pallas_call(
        body, name="gdn_prep_bwd", grid=(nc,),
        in_specs=[pl.BlockSpec((c, gw), row)] * 3 + [pl.BlockSpec((c, LANES), row)] + one_dir * 2,
        out_specs=[pl.BlockSpec((c, gw), row)] * 3 + [pl.BlockSpec((c, LANES), row)],
        out_shape=[jax.ShapeDtypeStruct((SEQ, gw), F32)] * 3 + [jax.ShapeDtypeStruct((SEQ, LANES), F32)],
        compiler_params=_params(("parallel",)),
    )(q, k, v, gates, *cts)


def _scan_step(state, u, w, qd, kd, qk, dec):
    v_new = u - _mm(w, state)
    o = _mm(qd, state) + _mm(qk, v_new)
    new_state = state * dec + _mm_tn(kd, v_new)
    return new_state, o


def _scan_in_specs(chunk_of):
    c, hd, nh = GDN_CHUNK, HEAD_DIM, GDN_HEADS
    gw = nh * hd
    specs = []
    for d in range(2):
        f = chunk_of[d]
        specs += [pl.BlockSpec((None, c, gw), lambda n, d=d, f=f: (d, f(n), 0))] * 4
        specs += [pl.BlockSpec((None, nh, c, c), lambda n, d=d, f=f: (d, 0, f(n), 0)),
                  pl.BlockSpec((None, None, nh, LANES), lambda n, d=d, f=f: (d, f(n), 0, 0))]
    return specs


def _gdn_scan_fwd(u, w, qd, kd, qk, dec):
    c, hd, nh, nc = GDN_CHUNK, HEAD_DIM, GDN_HEADS, _n_chunks()
    gw = nh * hd
    chunk_of = (lambda n: n, lambda n: nc - 1 - n)

    def body(*refs):
        ins = refs[:12]
        o_refs = refs[12:14]
        st_refs = refs[14:16]
        s_ref = refs[16]

        @pl.when(pl.program_id(0) == 0)
        def _():
            s_ref[...] = jnp.zeros_like(s_ref)

        for d in range(2):
            u_r, w_r, qd_r, kd_r, qk_r, dec_r = ins[6 * d:6 * d + 6]
            for h in range(nh):
                cols = slice(h * hd, (h + 1) * hd)
                st = s_ref[d * nh + h]
                st_refs[d][h] = st
                new_state, o = _scan_step(st, u_r[:, cols], w_r[:, cols], qd_r[:, cols], kd_r[:, cols],
                                          qk_r[h], dec_r[h:h + 1, :])
                s_ref[d * nh + h] = new_state
                o_refs[d][:, cols] = o

    o_specs = [pl.BlockSpec((c, gw), lambda n, f=f: (f(n), 0)) for f in chunk_of]
    st_specs = [pl.BlockSpec((None, nh, hd, hd), lambda n, f=f: (f(n), 0, 0, 0)) for f in chunk_of]
    return pl.pallas_call(
        body, name="gdn_scan_fwd", grid=(nc,), in_specs=_scan_in_specs(chunk_of),
        out_specs=o_specs + st_specs,
        out_shape=[jax.ShapeDtypeStruct((SEQ, gw), F32)] * 2 + [jax.ShapeDtypeStruct((nc, nh, hd, hd), F32)] * 2,
        scratch_shapes=[pltpu.VMEM((2 * nh, hd, hd), F32)],
        compiler_params=_params(("arbitrary",)),
    )(*([u, w, qd, kd, qk, dec] * 2))


def _gdn_scan_bwd(u, w, qd, kd, qk, dec, st0, st1, do):
    c, hd, nh, nc = GDN_CHUNK, HEAD_DIM, GDN_HEADS, _n_chunks()
    gw = nh * hd
    chunk_of = (lambda n: nc - 1 - n, lambda n: n)

    def body(*refs):
        ins = refs[:12]
        st_in = refs[12:14]
        do_in = refs[14:16]
        outs = refs[16:28]
        ds_ref = refs[28]

        @pl.when(pl.program_id(0) == 0)
        def _():
            ds_ref[...] = jnp.zeros_like(ds_ref)

        for d in range(2):
            u_r, w_r, qd_r, kd_r, qk_r, dec_r = ins[6 * d:6 * d + 6]
            du_r, dw_r, dqd_r, dkd_r, dqk_r, ddec_r = outs[6 * d:6 * d + 6]
            for h in range(nh):
                cols = slice(h * hd, (h + 1) * hd)
                _, vjp = jax.vjp(_scan_step, st_in[d][h], u_r[:, cols], w_r[:, cols], qd_r[:, cols],
                                 kd_r[:, cols], qk_r[h], dec_r[h:h + 1, 0:1])
                dst, du, dw, dqd, dkd, dqk, ddec = vjp((ds_ref[d * nh + h], do_in[d][:, cols]))
                ds_ref[d * nh + h] = dst
                du_r[:, cols] = du
                dw_r[:, cols] = dw
                dqd_r[:, cols] = dqd
                dkd_r[:, cols] = dkd
                dqk_r[h] = dqk
                ddec_r[h:h + 1, :] = jnp.broadcast_to(ddec, (1, LANES))

    st_specs = [pl.BlockSpec((None, nh, hd, hd), lambda n, f=f: (f(n), 0, 0, 0)) for f in chunk_of]
    do_specs = [pl.BlockSpec((c, gw), lambda n, f=f: (f(n), 0)) for f in chunk_of]
    out_specs, out_shape = [], []
    for f in chunk_of:
        out_specs += [pl.BlockSpec((c, gw), lambda n, f=f: (f(n), 0))] * 4
        out_specs += [pl.BlockSpec((nh, c, c), lambda n, f=f: (0, f(n), 0)),
                      pl.BlockSpec((None, nh, LANES), lambda n, f=f: (f(n), 0, 0))]
        out_shape += [jax.ShapeDtypeStruct((SEQ, gw), F32)] * 4
        out_shape += [jax.ShapeDtypeStruct((nh, SEQ, c), F32), jax.ShapeDtypeStruct((nc, nh, LANES), F32)]
    return pl.pallas_call(
        body, name="gdn_scan_bwd", grid=(nc,), in_specs=_scan_in_specs(chunk_of) + st_specs + do_specs,
        out_specs=out_specs, out_shape=out_shape,
        scratch_shapes=[pltpu.VMEM((2 * nh, hd, hd), F32)],
        compiler_params=_params(("arbitrary",)),
    )(*([u, w, qd, kd, qk, dec] * 2), st0, st1, do, do)


def _post_fn(o0, o1, z, gn):
    return _rms(o0 + o1, gn) * _silu(z)


def _gdn_post_fwd(o0, o1, p, gn):
    tm, hd, nh = _row_tile(), HEAD_DIM, GDN_HEADS
    zoff = 3 * nh
    blk = lambda i, h: (i, h)
    (ya,) = _tiled(_post_fn, "gdn_post_fwd", (SEQ // tm, nh),
                   [(o0, (tm, hd), blk), (o1, (tm, hd), blk),
                    (p, (tm, hd), lambda i, h: (i, zoff + h)), (gn, (1, hd), lambda i, h: (0, 0))],
                   [((SEQ, nh * hd), BF16, (tm, hd), blk, ())])
    return ya


def _gdn_post_bwd(o0, o1, p, gn, dya):
    tm, hd, nh = _row_tile(), HEAD_DIM, GDN_HEADS
    zoff = 3 * nh
    blk = lambda i, h: (i, h)

    def fn(a, b, z, g, dy):
        _, vjp = jax.vjp(_post_fn, a, b, z, g)
        do, _, dz, dg = vjp(dy.astype(F32))
        return do, dz, dg

    return _tiled(fn, "gdn_post_bwd", (SEQ // tm, nh),
                  [(o0, (tm, hd), blk), (o1, (tm, hd), blk),
                   (p, (tm, hd), lambda i, h: (i, zoff + h)), (gn, (1, hd), lambda i, h: (0, 0)),
                   (dya, (tm, hd), blk)],
                  [((SEQ, nh * hd), F32, (tm, hd), blk, ()),
                   ((SEQ, nh * hd), BF16, (tm, hd), blk, ()),
                   ((1, hd), F32, (1, hd), lambda i, h: (0, 0), (0, 1))],
                  sem=("arbitrary", "arbitrary"))


def _sgu_ln(v, lg, lb):
    gv = _gelu(v)
    mu = jnp.mean(gv, axis=-1, keepdims=True)
    cen = gv - mu
    var = jnp.mean(cen * cen, axis=-1, keepdims=True)
    return cen * lax.rsqrt(var + NORM_EPS) * lg + lb


def _sgu_mix(vn, w_ref, bt_ref):
    parts = []
    for g in range(SGU_GROUPS):
        cols = slice(g * SGU_DIM, (g + 1) * SGU_DIM)
        parts.append(_mm(w_ref[g], vn[:, cols]) + bt_ref[:, g:g + 1])
    return jnp.concatenate(parts, axis=1)


def _sgu_specs(p):
    sw, sb = _sw(), SGU_BLOCK
    uoff = 4 * _gw()
    assert uoff % sw == 0
    iu = uoff // sw
    return [(p, (sb, sw), lambda n: (n, iu)), (p, (sb, sw), lambda n: (n, iu + 1))]


def _sgu_fwd(p, lg, lb, w_s, b_t):
    sw, sb = _sw(), SGU_BLOCK
    ins = _sgu_specs(p)

    def body(u_ref, v_ref, lg_ref, lb_ref, w_ref, bt_ref, y_ref):
        vn = _sgu_ln(v_ref[...], lg_ref[...], lb_ref[...])
        y_ref[...] = (_gelu(u_ref[...]) * _sgu_mix(vn, w_ref, bt_ref)).astype(BF16)

    one2 = lambda n: (0, 0)
    return pl.pallas_call(
        body, name="sgu_fwd", grid=(SEQ // sb,),
        in_specs=[pl.BlockSpec(b, m) for _, b, m in ins]
        + [pl.BlockSpec((1, sw), one2), pl.BlockSpec((1, sw), one2),
           pl.BlockSpec((SGU_GROUPS, sb, sb), lambda n: (0, 0, 0)), pl.BlockSpec((sb, LANES), one2)],
        out_specs=pl.BlockSpec((sb, sw), lambda n: (n, 0)),
        out_shape=jax.ShapeDtypeStruct((SEQ, sw), BF16),
        compiler_params=_params(("parallel",)),
    )(p, p, lg, lb, w_s, b_t)


def _sgu_bwd(p, lg, lb, w_s, b_t, dy):
    sw, sb, ng = _sw(), SGU_BLOCK, SGU_GROUPS
    ins = _sgu_specs(p)

    def body(u_ref, v_ref, lg_ref, lb_ref, w_ref, bt_ref, dy_ref, du_ref, dv_ref, dlg_ref, dlb_ref, dw_ref, dbt_ref):
        first = pl.program_id(0) == 0
        u, v = u_ref[...], v_ref[...]
        gu, gelu_vjp = jax.vjp(_gelu, u)
        vn, ln_vjp = jax.vjp(_sgu_ln, v, lg_ref[...], lb_ref[...])
        s = _sgu_mix(vn, w_ref, bt_ref)
        dyv = dy_ref[...].astype(F32)
        ds = dyv * gu
        (du,) = gelu_vjp(dyv * s)
        lane = lax.broadcasted_iota(jnp.int32, (sb, LANES), 1)
        dvn_parts = []
        dbt = jnp.zeros((sb, LANES), F32)
        for g in range(ng):
            cols = slice(g * SGU_DIM, (g + 1) * SGU_DIM)
            ds_g = ds[:, cols]
            dw_g = _mm_nt(ds_g, vn[:, cols])
            dvn_parts.append(_mm_tn(w_ref[g], ds_g))
            dbt = dbt + jnp.where(lane == g, jnp.sum(ds_g, axis=1, keepdims=True), 0.0)

            @pl.when(first)
            def _(g=g, dw_g=dw_g):
                dw_ref[g] = dw_g

            @pl.when(jnp.logical_not(first))
            def _(g=g, dw_g=dw_g):
                dw_ref[g] += dw_g

        dv, dlg, dlb = ln_vjp(jnp.concatenate(dvn_parts, axis=1))
        du_ref[...] = du.astype(BF16)
        dv_ref[...] = dv.astype(BF16)

        @pl.when(first)
        def _():
            dlg_ref[...] = dlg
            dlb_ref[...] = dlb
            dbt_ref[...] = dbt

        @pl.when(jnp.logical_not(first))
        def _():
            dlg_ref[...] += dlg
            dlb_ref[...] += dlb
            dbt_ref[...] += dbt

    one2 = lambda n: (0, 0)
    row = pl.BlockSpec((sb, sw), lambda n: (n, 0))
    return pl.pallas_call(
        body, name="sgu_bwd", grid=(SEQ // sb,),
        in_specs=[pl.BlockSpec(b, m) for _, b, m in ins]
        + [pl.BlockSpec((1, sw), one2), pl.BlockSpec((1, sw), one2),
           pl.BlockSpec((ng, sb, sb), lambda n: (0, 0, 0)), pl.BlockSpec((sb, LANES), one2), row],
        out_specs=[row, row, pl.BlockSpec((1, sw), one2), pl.BlockSpec((1, sw), one2),
                   pl.BlockSpec((ng, sb, sb), lambda n: (0, 0, 0)), pl.BlockSpec((sb, LANES), one2)],
        out_shape=[jax.ShapeDtypeStruct((SEQ, sw), BF16)] * 2 + [jax.ShapeDtypeStruct((1, sw), F32)] * 2
        + [jax.ShapeDtypeStruct((ng, sb, sb), F32), jax.ShapeDtypeStruct((sb, LANES), F32)],
        compiler_params=_params(("arbitrary",)),
    )(p, p, lg, lb, w_s, b_t, dy)


def _ffn_fn(xg, xv, bg, bv, *taps):
    n = FFN_CONV
    cg = _conv_rows(xg, taps[0:n]) + bg
    cv = _conv_rows(xv, taps[n:2 * n]) + bv
    return _silu(cg) * cv


def _ffn_tile():
    return _pick(D_FF, 256)


def _ffn_specs(up, conv_w, conv_b):
    tc = _ffn_tile()
    nt = D_FF // tc
    ins = [(up, (None, SEQ, tc), (lambda j, s=s: (s, 0, j))) for s in range(2)]
    ins += [(conv_b, (1, tc), (lambda j, s=s: (0, s * nt + j))) for s in range(2)]
    ins += [(conv_w, (FFN_CONV, tc), (lambda j, s=s: (0, s * nt + j))) for s in range(2)]
    return ins


def _ffn_act_fwd(up, conv_w, conv_b):
    tc = _ffn_tile()
    ins = _ffn_specs(up, conv_w, conv_b)

    def body(xg, xv, bg, bv, wg, wv, o_ref):
        taps = [w[j:j + 1, :] for w in (wg, wv) for j in range(FFN_CONV)]
        o_ref[...] = _ffn_fn(xg[...], xv[...], bg[...], bv[...], *taps).astype(BF16)

    return pl.pallas_call(
        body, name="ffn_act_fwd", grid=(D_FF // tc,),
        in_specs=[pl.BlockSpec(b, m) for _, b, m in ins],
        out_specs=pl.BlockSpec((SEQ, tc), lambda j: (0, j)),
        out_shape=jax.ShapeDtypeStruct((SEQ, D_FF), BF16),
        compiler_params=_params(("parallel",)),
    )(*[a for a, _, _ in ins])


def _ffn_act_bwd(up, conv_w, conv_b, dact):
    tc = _ffn_tile()
    nt = D_FF // tc
    ins = _ffn_specs(up, conv_w, conv_b) + [(dact, (SEQ, tc), lambda j: (0, j))]

    def body(xg, xv, bg, bv, wg, wv, dact_ref, dup_ref, dwg_ref, dwv_ref, dbg_ref, dbv_ref):
        taps = [w[j:j + 1, :] for w in (wg, wv) for j in range(FFN_CONV)]
        _, vjp = jax.vjp(_ffn_fn, xg[...], xv[...], bg[...], bv[...], *taps)
        grads = vjp(dact_ref[...].astype(F32))
        dup_ref[0] = grads[0].astype(BF16)
        dup_ref[1] = grads[1].astype(BF16)
        dbg_ref[...] = jnp.sum(grads[2], axis=0, keepdims=True)
        dbv_ref[...] = jnp.sum(grads[3], axis=0, keepdims=True)
        for j in range(FFN_CONV):
            dwg_ref[j:j + 1, :] = jnp.sum(grads[4 + j], axis=0, keepdims=True)
            dwv_ref[j:j + 1, :] = jnp.sum(grads[4 + FFN_CONV + j], axis=0, keepdims=True)

    col = lambda j: (0, j)
    outs = pl.pallas_call(
        body, name="ffn_act_bwd", grid=(nt,),
        in_specs=[pl.BlockSpec(b, m) for _, b, m in ins],
        out_specs=[pl.BlockSpec((2, SEQ, tc), lambda j: (0, 0, j)),
                   pl.BlockSpec((FFN_CONV, tc), col), pl.BlockSpec((FFN_CONV, tc), col),
                   pl.BlockSpec((1, tc), col), pl.BlockSpec((1, tc), col)],
        out_shape=[jax.ShapeDtypeStruct((2, SEQ, D_FF), BF16),
                   jax.ShapeDtypeStruct((FFN_CONV, D_FF), F32), jax.ShapeDtypeStruct((FFN_CONV, D_FF), F32),
                   jax.ShapeDtypeStruct((1, D_FF), F32), jax.ShapeDtypeStruct((1, D_FF), F32)],
        compiler_params=_params(("parallel",)),
    )(*[a for a, _, _ in ins])
    dup, dwg, dwv, dbg, dbv = outs
    return dup, jnp.concatenate([dwg, dwv], axis=1), jnp.concatenate([dbg, dbv], axis=1)


def _pad_lanes(v):
    return jnp.pad(v.reshape(1, -1), ((0, 0), (0, LANES - v.size)))


def _layer_fwd(x, lw):
    h = _rms_fwd(x, lw['norm_mix_g'], "rms_mix_fwd")
    p = _matmul(h, lw['w_all'], 'nn', F32, "mm_in")
    gates = _gate_fwd(p, lw['a_log'], lw['dt_bias'])
    q, k, v = _qkv_fwd(p, lw['qkv_conv_w'])
    u, w, qd, kd, qk, dec = _gdn_prep_fwd(q, k, v, gates)
    o0, o1, st0, st1 = _gdn_scan_fwd(u, w, qd, kd, qk, dec)
    ya = _gdn_post_fwd(o0, o1, p, lw['gdn_norm_g'])
    yb = _sgu_fwd(p, lw['sgu_ln_g'], lw['sgu_ln_b'], lw['sgu_w'], lw['sgu_bt'])
    ta = _matmul(ya, lw['w_branch_a'], 'nn', F32, "mm_branch_a", b_split=N_CHIPS)
    tb = _matmul(yb, lw['w_branch_b'], 'nn', F32, "mm_branch_b", b_split=N_CHIPS)
    m = _merge_fwd(p, ta, tb)
    x1 = _matmul(m, lw['w_out'], 'nn', F32, "mm_out", add=x)
    h2 = _rms_fwd(x1, lw['norm_ffn_g'], "rms_ffn_fwd")
    up = _matmul(h2, lw['w_up'], 'nn', F32, "mm_up", b_split=N_CHIPS, o_split=2)
    act = _ffn_act_fwd(up, lw['ffn_conv_w'], lw['ffn_conv_b'])
    x2 = _matmul(act, lw['w_down'], 'nn', F32, "mm_down", add=x1)
    saved = dict(x=x, h=h, p=p, gates=gates, q=q, k=k, v=v, u=u, w=w, qd=qd, kd=kd, qk=qk, dec=dec,
                 o0=o0, o1=o1, st0=st0, st1=st1, ya=ya, yb=yb, ta=ta, tb=tb, m=m, x1=x1, h2=h2, up=up, act=act)
    return x2, saved


def _layer_bwd(dx, dx_bf, lw, s):
    g = {}
    dact = _matmul(dx_bf, lw['w_down'], 'nt', BF16, "mm_down_dgrad")
    g['w_down'] = _matmul(s['act'], dx_bf, 'tn', F32, "mm_down_wgrad")
    dup, g['ffn_conv_w'], g['ffn_conv_b'] = _ffn_act_bwd(s['up'], lw['ffn_conv_w'], lw['ffn_conv_b'], dact)
    dh2 = _matmul(dup, lw['w_up'], 'nt', F32, "mm_up_dgrad", a_split=2, b_split=N_CHIPS)
    g['w_up'] = _matmul(s['h2'], dup, 'tn', F32, "mm_up_wgrad", b_split=2, o_split=N_CHIPS)
    dx1, dx1_bf, g['norm_ffn_g'] = _rms_bwd(s['x1'], lw['norm_ffn_g'], dh2, dx, "rms_ffn_bwd")
    dm = _matmul(dx1_bf, lw['w_out'], 'nt', F32, "mm_out_dgrad")
    g['w_out'] = _matmul(s['m'], dx1_bf, 'tn', F32, "mm_out_wgrad")
    d_ga, d_gb, d_ta, d_tb = _merge_bwd(s['p'], s['ta'], s['tb'], dm)
    dya = _matmul(d_ta, lw['w_branch_a'], 'nt', BF16, "mm_branch_a_dgrad", b_split=N_CHIPS)
    dyb = _matmul(d_tb, lw['w_branch_b'], 'nt', BF16, "mm_branch_b_dgrad", b_split=N_CHIPS)
    g['w_branch_a'] = _matmul(s['ya'], d_ta, 'tn', F32, "mm_branch_a_wgrad", o_split=N_CHIPS)
    g['w_branch_b'] = _matmul(s['yb'], d_tb, 'tn', F32, "mm_branch_b_wgrad", o_split=N_CHIPS)
    du_s, dv_s, g['sgu_ln_g'], g['sgu_ln_b'], g['sgu_w'], g['sgu_bt'] = _sgu_bwd(
        s['p'], lw['sgu_ln_g'], lw['sgu_ln_b'], lw['sgu_w'], lw['sgu_bt'], dyb)
    do, dz, g['gdn_norm_g'] = _gdn_post_bwd(s['o0'], s['o1'], s['p'], lw['gdn_norm_g'], dya)
    cts = _gdn_scan_bwd(s['u'], s['w'], s['qd'], s['kd'], s['qk'], s['dec'], s['st0'], s['st1'], do)
    dq, dk, dv, dgates = _gdn_prep_bwd(s['q'], s['k'], s['v'], s['gates'], cts)
    (dxq, dxk, dxv), g['qkv_conv_w'] = _qkv_bwd(s['p'], lw['qkv_conv_w'], dq, dk, dv)
    d_ab, g['a_log'], g['dt_bias'] = _gate_bwd(s['p'], lw['a_log'], lw['dt_bias'], dgates)
    dp = jnp.concatenate([dxq, dxk, dxv, dz, du_s, dv_s, d_ga, d_gb, d_ab], axis=1)
    dh = _matmul(dp, lw['w_all'], 'nt', F32, "mm_in_dgrad")
    g['w_all'] = _matmul(s['h'], dp, 'tn', F32, "mm_in_wgrad")
    dx0, dx0_bf, g['norm_mix_g'] = _rms_bwd(s['x'], lw['norm_mix_g'], dh, dx1, "rms_mix_bwd")
    return dx0, dx0_bf, g


def _w_all_from_w_in(w_in):
    n_ab = 4 * GDN_HEADS
    cut = 4 * _gw()
    pad = jnp.zeros((w_in.shape[0], AB_PAD - n_ab), w_in.dtype)
    return jnp.concatenate([w_in[:, :cut], w_in[:, cut + n_ab:], w_in[:, cut:cut + n_ab], pad], axis=1)


def _w_in_grad_from_all(g_all):
    n_ab = 4 * GDN_HEADS
    cut = 4 * _gw()
    n_main = cut + 2 * _sw() + 2 * D_MODEL
    return jnp.concatenate([g_all[:, :cut], g_all[:, n_main:n_main + n_ab], g_all[:, cut:n_main]], axis=1)


def _layer_weights(l, big, small):
    cat_cols = lambda t: jnp.concatenate([t[l, j] for j in range(N_CHIPS)], axis=-1)
    return dict(
        w_all=_w_all_from_w_in(cat_cols(big['w_in'])),
        w_branch_a=big['w_branch_a'][l], w_branch_b=big['w_branch_b'][l], w_up=big['w_up'][l],
        w_out=big['w_out'][l].reshape(D_MODEL, D_MODEL), w_down=big['w_down'][l].reshape(D_FF, D_MODEL),
        qkv_conv_w=cat_cols(small['qkv_conv_w']), ffn_conv_w=cat_cols(small['ffn_conv_w']),
        norm_mix_g=small['norm_mix_g'][l:l + 1], norm_ffn_g=small['norm_ffn_g'][l:l + 1],
        a_log=_pad_lanes(small['a_log'][l]), dt_bias=_pad_lanes(small['dt_bias'][l]),
        gdn_norm_g=small['gdn_norm_g'][l:l + 1],
        sgu_ln_g=small['sgu_ln_g'][l:l + 1], sgu_ln_b=small['sgu_ln_b'][l:l + 1], sgu_w=small['sgu_w'][l],
        sgu_bt=jnp.pad(small['sgu_b'][l].T, ((0, 0), (0, LANES - SGU_GROUPS))),
        ffn_conv_b=small['ffn_conv_b'][l:l + 1])


_SMALL_GRADS = ('norm_mix_g', 'a_log', 'dt_bias', 'gdn_norm_g', 'sgu_ln_g', 'sgu_ln_b', 'sgu_w', 'sgu_bt',
                'norm_ffn_g', 'ffn_conv_b', 'qkv_conv_w', 'ffn_conv_w')
_BIG = ('w_in', 'w_branch_a', 'w_branch_b', 'w_out', 'w_up', 'w_down')


def _big_grad_slices(g):
    w_in = _w_in_grad_from_all(g['w_all'])
    w_in = w_in.reshape(D_MODEL, N_CHIPS, -1).transpose(1, 0, 2)
    return dict(w_in=w_in, w_branch_a=g['w_branch_a'], w_branch_b=g['w_branch_b'], w_up=g['w_up'],
                w_out=g['w_out'].reshape(N_CHIPS, D_MODEL // N_CHIPS, D_MODEL),
                w_down=g['w_down'].reshape(N_CHIPS, D_FF // N_CHIPS, D_MODEL))


def _adamw(w, g, m, v, name):
    shape = w.shape
    cols = shape[-1]
    rows = w.size // cols
    tr = _pick(rows, max(8, (1 << 18) // cols // 8 * 8), 8) if rows % 8 == 0 else rows

    def fn(wv, gv, mv, vv):
        m2 = ADAM_B1 * mv + (1.0 - ADAM_B1) * gv
        v2 = ADAM_B2 * vv + (1.0 - ADAM_B2) * (gv * gv)
        m_hat = m2 / (1.0 - ADAM_B1 ** ADAM_STEP)
        v_hat = v2 / (1.0 - ADAM_B2 ** ADAM_STEP)
        delta = -ADAM_LR * (m_hat / (jnp.sqrt(v_hat) + ADAM_EPS) + ADAM_WD * wv)
        return delta, m2, v2

    row = lambda i: (i, 0)
    outs = _tiled(fn, name, (rows // tr,),
                  [(t.reshape(rows, cols), (tr, cols), row) for t in (w, g, m, v)],
                  [((rows, cols), F32, (tr, cols), row, ())] * 3, sem=("parallel",))
    return [o.reshape(shape) for o in outs]


MESH_IDS = pl.DeviceIdType.MESH
ANY = pl.BlockSpec(memory_space=pl.ANY)


def _place():
    x, y, c = lax.axis_index("x"), lax.axis_index("y"), lax.axis_index("c")
    chips = [(1 - x, y), (x, 1 - y), (1 - x, 1 - y)]
    return x, y, c, 2 * x + y, chips


def _chip_index():
    return 2 * lax.axis_index("x") + lax.axis_index("y")


def _place_shard(w, dtype, name):
    depth, r, cols = w.shape
    tr = _pick(r, max(16, (1 << 18) // cols // 16 * 16), 16) if r % 16 == 0 else r

    def body(w_ref, o_ref):
        o_ref[...] = w_ref[...].astype(dtype)

    return pl.pallas_call(
        body, name=name, grid=(depth, r // tr),
        in_specs=[pl.BlockSpec((None, tr, cols), lambda l, i: (l, i, 0))],
        out_specs=pl.BlockSpec((None, None, tr, cols), lambda l, i: (l, _chip_index(), i, 0)),
        out_shape=jax.ShapeDtypeStruct((depth, N_CHIPS, r, cols), dtype),
        compiler_params=_params(("parallel", "parallel")),
    )(w)


def _all_gather_chips(xs):
    n = len(xs)
    half = DEPTH // 2

    def body(*refs):
        o_refs = refs[n:2 * n]
        send_sems, recv_sems = refs[2 * n:]
        x, y, c, me, chips = _place()
        sibling = (x, y, 1 - c)
        mine = pl.ds(c * half, half)
        theirs = pl.ds((1 - c) * half, half)

        def rcopy(i, k, slab, to):
            return pltpu.make_async_remote_copy(src_ref=slab, dst_ref=slab, send_sem=send_sems.at[i * 6 + k],
                                                recv_sem=recv_sems.at[i * 6 + k], device_id=to,
                                                device_id_type=MESH_IDS)

        sends = []
        for j, chip in enumerate(chips):
            for i in range(n):
                sends.append(rcopy(i, j, o_refs[i].at[mine, me], (*chip, c)))
                sends[-1].start()
        for j, (cx, cy) in enumerate(chips):
            for i in range(n):
                landed = o_refs[i].at[mine, 2 * cx + cy]
                rcopy(i, j, landed, sibling).wait_recv()
                sends.append(rcopy(i, 3 + j, landed, sibling))
                sends[-1].start()
        for j, (cx, cy) in enumerate(chips):
            for i in range(n):
                rcopy(i, 3 + j, o_refs[i].at[theirs, 2 * cx + cy], sibling).wait_recv()
        for cp in sends:
            cp.wait_send()

    return pl.pallas_call(
        body, name="all_gather_weights", in_specs=[ANY] * n, out_specs=[ANY] * n,
        out_shape=[jax.ShapeDtypeStruct(t.shape, t.dtype) for t in xs],
        input_output_aliases={i: i for i in range(n)},
        scratch_shapes=[pltpu.SemaphoreType.DMA((6 * n,)), pltpu.SemaphoreType.DMA((6 * n,))],
        compiler_params=pltpu.CompilerParams(has_side_effects=True),
    )(*xs)


def _swap_halves(gs):
    n = len(gs)

    def body(*refs):
        g_refs, b_refs = refs[:n], refs[n:2 * n]
        send_sems, recv_sems = refs[2 * n:]
        x, y, c, _, _ = _place()
        copies = []
        for i in range(n):
            rh = gs[i].shape[1] // 2
            src = g_refs[i].at[:, pl.ds((1 - c) * rh, rh), :]
            copies.append(pltpu.make_async_remote_copy(src_ref=src, dst_ref=b_refs[i], send_sem=send_sems.at[i],
                                                       recv_sem=recv_sems.at[i], device_id=(x, y, 1 - c),
                                                       device_id_type=MESH_IDS))
            copies[-1].start()
        for cp in copies:
            cp.wait()

    return pl.pallas_call(
        body, name="grad_swap_halves", in_specs=[ANY] * n, out_specs=[ANY] * n,
        out_shape=[jax.ShapeDtypeStruct((t.shape[0], t.shape[1] // 2, t.shape[2]), t.dtype) for t in gs],
        scratch_shapes=[pltpu.SemaphoreType.DMA((n,)), pltpu.SemaphoreType.DMA((n,))],
        compiler_params=pltpu.CompilerParams(has_side_effects=True),
    )(*gs)


def _half_tile(rh, cols):
    return _pick(rh, max(16, (1 << 18) // cols // 16 * 16), 16)


def _chip_partial(g, b, name):
    nchip, r, cols = g.shape
    rh = r // 2
    tr = _half_tile(rh, cols)
    nt = rh // tr

    def body(g_ref, b_ref, o_ref):
        o_ref[...] = (g_ref[...] + b_ref[...]).astype(BF16)

    return pl.pallas_call(
        body, name=name, grid=(nchip, nt),
        in_specs=[pl.BlockSpec((None, tr, cols), lambda j, i: (j, lax.axis_index("c") * nt + i, 0)),
                  pl.BlockSpec((None, tr, cols), lambda j, i: (j, i, 0))],
        out_specs=pl.BlockSpec((None, tr, cols), lambda j, i: (j, i, 0)),
        out_shape=jax.ShapeDtypeStruct((nchip, rh, cols), BF16),
        compiler_params=_params(("parallel", "parallel")),
    )(g, b)


def _scatter_partials(ps):
    n = len(ps)

    def body(*refs):
        p_refs, r_refs = refs[:n], refs[n:2 * n]
        send_sems, recv_sems = refs[2 * n:]
        x, y, c, _, chips = _place()
        copies = []
        for j, (cx, cy) in enumerate(chips):
            for i in range(n):
                copies.append(pltpu.make_async_remote_copy(
                    src_ref=p_refs[i].at[2 * cx + cy], dst_ref=r_refs[i].at[j], send_sem=send_sems.at[3 * i + j],
                    recv_sem=recv_sems.at[3 * i + j], device_id=(cx, cy, c), device_id_type=MESH_IDS))
                copies[-1].start()
        for cp in copies:
            cp.wait()

    return pl.pallas_call(
        body, name="grad_scatter", in_specs=[ANY] * n, out_specs=[ANY] * n,
        out_shape=[jax.ShapeDtypeStruct((3,) + t.shape[1:], t.dtype) for t in ps],
        scratch_shapes=[pltpu.SemaphoreType.DMA((3 * n,)), pltpu.SemaphoreType.DMA((3 * n,))],
        compiler_params=pltpu.CompilerParams(has_side_effects=True),
    )(*ps)


def _reduce_own(g, b, rcv, acc, l, name):
    nchip, r, cols = g.shape
    rh = r // 2
    tr = _half_tile(rh, cols)
    nt = rh // tr

    def body(*refs):
        g_ref, b_ref, r_ref = refs[:3]
        o_ref = refs[-1]
        total = g_ref[...] + b_ref[...]
        for j in range(3):
            total = total + r_ref[j].astype(F32)
        o_ref[...] = total

    mine = lambda i: lax.axis_index("c") * nt + i
    in_specs = [pl.BlockSpec((None, tr, cols), lambda i: (_chip_index(), mine(i), 0)),
                pl.BlockSpec((None, tr, cols), lambda i: (_chip_index(), i, 0)),
                pl.BlockSpec((3, tr, cols), lambda i: (0, i, 0))]
    args = [g, b, rcv]
    aliases = {}
    if acc is not None:
        in_specs.append(ANY)
        args.append(acc)
        aliases = {3: 0}
    return pl.pallas_call(
        body, name=name, grid=(nt,), in_specs=in_specs,
        out_specs=pl.BlockSpec((None, tr, cols), lambda i: (l, mine(i), 0)),
        out_shape=jax.ShapeDtypeStruct((DEPTH, r, cols), F32), input_output_aliases=aliases,
        compiler_params=_params(("parallel",)),
    )(*args)


def _share_halves(fs, l):
    n = len(fs)

    def body(*refs):
        o_refs = refs[n:2 * n]
        send_sems, recv_sems = refs[2 * n:]
        x, y, c, _, _ = _place()

        def halves(i):
            rh = fs[i].shape[1] // 2
            return o_refs[i].at[l, pl.ds(c * rh, rh), :], o_refs[i].at[l, pl.ds((1 - c) * rh, rh), :]

        def copy(i, rows):
            return pltpu.make_async_remote_copy(src_ref=rows, dst_ref=rows, send_sem=send_sems.at[i],
                                                recv_sem=recv_sems.at[i], device_id=(x, y, 1 - c),
                                                device_id_type=MESH_IDS)

        for i in range(n):
            copy(i, halves(i)[0]).start()
        for i in range(n):
            mine, theirs = halves(i)
            copy(i, mine).wait_send()
            copy(i, theirs).wait_recv()

    return pl.pallas_call(
        body, name="grad_share_halves", in_specs=[ANY] * n, out_specs=[ANY] * n,
        out_shape=[jax.ShapeDtypeStruct(t.shape, t.dtype) for t in fs],
        input_output_aliases={i: i for i in range(n)},
        scratch_shapes=[pltpu.SemaphoreType.DMA((n,)), pltpu.SemaphoreType.DMA((n,))],
        compiler_params=pltpu.CompilerParams(has_side_effects=True),
    )(*fs)


def _reduce_scatter_layer(slices, stacked, l):
    names = list(slices)
    gs = [slices[k] for k in names]
    bs = _swap_halves(gs)
    ps = [_chip_partial(g, b, "grad_chip_partial_" + k) for k, g, b in zip(names, gs, bs)]
    rs = _scatter_partials(ps)
    fs = [_reduce_own(g, b, r, None if stacked is None else stacked[k], l, "grad_reduce_own_" + k)
          for k, g, b, r in zip(names, gs, bs, rs)]
    return dict(zip(names, _share_halves(fs, l)))


def _all_reduce_small(packed):
    rows = packed.shape[0]
    n_dev = 2 * N_CHIPS
    tr = _pick(rows, 512, 8)

    def body(x_ref, o_ref, buf, send_sems, recv_sems):
        x, y, c, _, _ = _place()
        me = 4 * x + 2 * y + c
        copies = []
        for px in range(2):
            for py in range(2):
                for pc in range(2):
                    peer = 4 * px + 2 * py + pc
                    cp = pltpu.make_async_remote_copy(src_ref=x_ref, dst_ref=buf.at[me], send_sem=send_sems.at[peer],
                                                      recv_sem=recv_sems.at[me], device_id=(px, py, pc),
                                                      device_id_type=MESH_IDS)

                    @pl.when(peer != me)
                    def _(cp=cp):
                        cp.start()
                    copies.append((peer, cp))
        buf[me] = x_ref[...]
        for peer, cp in copies:
            @pl.when(peer != me)
            def _(cp=cp, peer=peer):
                cp.wait_send()
                pltpu.make_async_remote_copy(src_ref=x_ref, dst_ref=buf.at[peer], send_sem=send_sems.at[peer],
                                             recv_sem=recv_sems.at[peer], device_id=(x, y, c),
                                             device_id_type=MESH_IDS).wait_recv()

        @pl.loop(0, rows // tr)
        def _(t):
            sl = pl.ds(pl.multiple_of(t * tr, 8), tr)
            acc = buf[0, sl, :]
            for s in range(1, n_dev):
                acc = acc + buf[s, sl, :]
            o_ref[sl, :] = acc

    vm = pl.BlockSpec(memory_space=pltpu.VMEM)
    return pl.pallas_call(
        body, name="all_reduce_small", in_specs=[vm], out_specs=vm,
        out_shape=jax.ShapeDtypeStruct(packed.shape, F32),
        scratch_shapes=[pltpu.VMEM((n_dev, rows, LANES), F32), pltpu.SemaphoreType.DMA((n_dev,)),
                        pltpu.SemaphoreType.DMA((n_dev,))],
        compiler_params=pltpu.CompilerParams(vmem_limit_bytes=VMEM_LIMIT, has_side_effects=True),
    )(packed)


_WEIGHTS = ('norm_mix_g', 'w_in', 'qkv_conv_w', 'a_log', 'dt_bias', 'gdn_norm_g', 'w_branch_a', 'sgu_ln_g',
            'sgu_ln_b', 'sgu_w', 'sgu_b', 'w_branch_b', 'w_out', 'norm_ffn_g', 'w_up', 'ffn_conv_w', 'ffn_conv_b',
            'w_down', 'final_norm_g')


def _local_step(x, target, big, small, after_layer=None):
    lws, saves = [], []
    for l in range(DEPTH):
        lw = _layer_weights(l, big, small)
        x, s = _layer_fwd(x, lw)
        lws.append(lw)
        saves.append(s)
    loss, dx, dx_bf, d_final = _loss_head(x, small['final_norm_g'].reshape(1, -1), target)
    grads = [None] * DEPTH
    for l in reversed(range(DEPTH)):
        dx, dx_bf, grads[l] = _layer_bwd(dx, dx_bf, lws[l], saves[l])
        if after_layer is not None:
            after_layer(l, grads[l])
    return loss, dx, grads, d_final


def _pack_small(grads, d_final):
    parts = [grads[l][k].reshape(-1) for l in range(DEPTH) for k in _SMALL_GRADS] + [d_final.reshape(-1)]
    flat = jnp.concatenate(parts)
    rows = -(-flat.size // (8 * LANES)) * 8
    return jnp.pad(flat, (0, rows * LANES - flat.size)).reshape(rows, LANES), [p.size for p in parts]


def _unpack_small(packed, grads, d_final):
    flat = packed.reshape(-1)
    out, off = [], 0
    for l in range(DEPTH):
        d = {}
        for k in _SMALL_GRADS:
            t = grads[l][k]
            d[k] = flat[off:off + t.size].reshape(t.shape)
            off += t.size
        out.append(d)
    return out, flat[off:off + d_final.size].reshape(d_final.shape)


def kernel(x, norm_mix_g, w_in, qkv_conv_w, a_log, dt_bias, gdn_norm_g, w_branch_a, sgu_ln_g, sgu_ln_b, sgu_w, sgu_b, w_branch_b, w_out, norm_ffn_g, w_up, ffn_conv_w, ffn_conv_b, w_down, final_norm_g, loss_target, m_norm_mix_g, m_w_in, m_qkv_conv_w, m_a_log, m_dt_bias, m_gdn_norm_g, m_w_branch_a, m_sgu_ln_g, m_sgu_ln_b, m_sgu_w, m_sgu_b, m_w_branch_b, m_w_out, m_norm_ffn_g, m_w_up, m_ffn_conv_w, m_ffn_conv_b, m_w_down, m_final_norm_g, v_norm_mix_g, v_w_in, v_qkv_conv_w, v_a_log, v_dt_bias, v_gdn_norm_g, v_w_branch_a, v_sgu_ln_g, v_sgu_ln_b, v_sgu_w, v_sgu_b, v_w_branch_b, v_w_out, v_norm_ffn_g, v_w_up, v_ffn_conv_w, v_ffn_conv_b, v_w_down, v_final_norm_g):
    w = dict(norm_mix_g=norm_mix_g, w_in=w_in, qkv_conv_w=qkv_conv_w, a_log=a_log, dt_bias=dt_bias,
             gdn_norm_g=gdn_norm_g, w_branch_a=w_branch_a, sgu_ln_g=sgu_ln_g, sgu_ln_b=sgu_ln_b, sgu_w=sgu_w,
             sgu_b=sgu_b, w_branch_b=w_branch_b, w_out=w_out, norm_ffn_g=norm_ffn_g, w_up=w_up,
             ffn_conv_w=ffn_conv_w, ffn_conv_b=ffn_conv_b, w_down=w_down, final_norm_g=final_norm_g)
    m = dict(norm_mix_g=m_norm_mix_g, w_in=m_w_in, qkv_conv_w=m_qkv_conv_w, a_log=m_a_log, dt_bias=m_dt_bias,
             gdn_norm_g=m_gdn_norm_g, w_branch_a=m_w_branch_a, sgu_ln_g=m_sgu_ln_g, sgu_ln_b=m_sgu_ln_b,
             sgu_w=m_sgu_w, sgu_b=m_sgu_b, w_branch_b=m_w_branch_b, w_out=m_w_out, norm_ffn_g=m_norm_ffn_g,
             w_up=m_w_up, ffn_conv_w=m_ffn_conv_w, ffn_conv_b=m_ffn_conv_b, w_down=m_w_down,
             final_norm_g=m_final_norm_g)
    v = dict(norm_mix_g=v_norm_mix_g, w_in=v_w_in, qkv_conv_w=v_qkv_conv_w, a_log=v_a_log, dt_bias=v_dt_bias,
             gdn_norm_g=v_gdn_norm_g, w_branch_a=v_w_branch_a, sgu_ln_g=v_sgu_ln_g, sgu_ln_b=v_sgu_ln_b,
             sgu_w=v_sgu_w, sgu_b=v_sgu_b, w_branch_b=v_w_branch_b, w_out=v_w_out, norm_ffn_g=v_norm_ffn_g,
             w_up=v_w_up, ffn_conv_w=v_ffn_conv_w, ffn_conv_b=v_ffn_conv_b, w_down=v_w_down,
             final_norm_g=v_final_norm_g)
    chip = _chip_index()

    placed = [_place_shard(w[k], BF16, "place_" + k) for k in _BIG]
    placed += [_place_shard(w[k], F32, "place_" + k) for k in ('qkv_conv_w', 'ffn_conv_w')]
    gathered = _all_gather_chips(placed)
    big = dict(zip(_BIG, gathered[:len(_BIG)]))
    small = dict(w, qkv_conv_w=gathered[-2], ffn_conv_w=gathered[-1])

    stacked = [None]

    def reduce_layer(l, g):
        stacked[0] = _reduce_scatter_layer(_big_grad_slices(g), stacked[0], l)

    loss, grad_x, grads, d_final = _local_step(x[0], loss_target[0], big, small, reduce_layer)
    loss = lax.psum(loss[0, 0], ("x", "y", "c"))
    big_g = stacked[0]
    packed, _ = _pack_small(grads, d_final)
    small_g, d_final = _unpack_small(_all_reduce_small(packed), grads, d_final)

    def stack(k):
        return jnp.stack([small_g[l][k] for l in range(DEPTH)])

    nd = 2 * GDN_HEADS
    g_out = {k: big_g[k].reshape(w[k].shape) for k in _BIG}
    for k in ('norm_mix_g', 'gdn_norm_g', 'sgu_ln_g', 'sgu_ln_b', 'norm_ffn_g', 'ffn_conv_b'):
        g_out[k] = stack(k).reshape(w[k].shape)
    g_out['sgu_w'] = stack('sgu_w')
    g_out['a_log'] = stack('a_log')[:, 0, :nd].reshape(w['a_log'].shape)
    g_out['dt_bias'] = stack('dt_bias')[:, 0, :nd].reshape(w['dt_bias'].shape)
    g_out['sgu_b'] = jnp.swapaxes(stack('sgu_bt')[:, :, :SGU_GROUPS], 1, 2)
    for k in ('qkv_conv_w', 'ffn_conv_w'):
        full = stack(k)
        width = w[k].shape[-1]
        g_out[k] = lax.dynamic_slice_in_dim(full, chip * width, width, axis=2)
    g_out['final_norm_g'] = d_final.reshape(w['final_norm_g'].shape)

    deltas, new_m, new_v = {}, {}, {}
    for k in _WEIGHTS:
        deltas[k], new_m[k], new_v[k] = _adamw(w[k], g_out[k], m[k], v[k], "adamw_" + k)
    return (loss, grad_x[None], *[g_out[k] for k in _WEIGHTS], *[deltas[k] for k in _WEIGHTS],
            *[new_m[k] for k in _WEIGHTS], *[new_v[k] for k in _WEIGHTS])
```

```python
import functools
import math

import jax
import jax.numpy as jnp
from jax import lax
from jax.experimental import pallas as pl
from jax.experimental.pallas import tpu as pltpu

F32 = jnp.float32
BF16 = jnp.bfloat16

D_MODEL = 2048
SEQ = 2048
DEPTH = 4
GDN_HEADS = 8
HEAD_DIM = 128
GDN_CHUNK = 64
QKV_CONV = 5
SGU_GROUPS = 8
SGU_DIM = 128
SGU_BLOCK = 128
D_FF = 5632
FFN_CONV = 3
NORM_EPS = 1e-6
N_CHIPS = 4

ADAM_LR = 0.001
ADAM_B1 = 0.9
ADAM_B2 = 0.999
ADAM_EPS = 1e-08
ADAM_WD = 0.01
ADAM_STEP = 10

LANES = 128
VMEM_LIMIT = 56 * 1024 * 1024
AB_PAD = LANES


def _gw():
    return GDN_HEADS * HEAD_DIM


def _sw():
    return SGU_GROUPS * SGU_DIM


def _n_all():
    return 4 * _gw() + 2 * _sw() + 2 * D_MODEL + AB_PAD


def _pick(n, target, align=LANES):
    if n <= target:
        return n
    best = None
    t = align
    while t <= target:
        if n % t == 0:
            best = t
        t += align
    assert best is not None, (n, target, align)
    return best


def _params(sem=None):
    return pltpu.CompilerParams(dimension_semantics=sem, vmem_limit_bytes=VMEM_LIMIT)


def _dot(a, b, dims, hi):
    dn = (dims, ((), ()))
    if not hi:
        return lax.dot_general(a.astype(BF16), b.astype(BF16), dn, preferred_element_type=F32)
    a_hi, b_hi = a.astype(BF16), b.astype(BF16)
    a_lo = (a - a_hi.astype(F32)).astype(BF16)
    b_lo = (b - b_hi.astype(F32)).astype(BF16)
    d = lambda p, q: lax.dot_general(p, q, dn, preferred_element_type=F32)
    return d(a_hi, b_hi) + (d(a_hi, b_lo) + d(a_lo, b_hi))


def _make_mm(hi):
    @jax.custom_vjp
    def mm(a, b):
        return _dot(a, b, ((1,), (0,)), hi)

    @jax.custom_vjp
    def mm_nt(a, b):
        return _dot(a, b, ((1,), (1,)), hi)

    @jax.custom_vjp
    def mm_tn(a, b):
        return _dot(a, b, ((0,), (0,)), hi)

    mm.defvjp(lambda a, b: (mm(a, b), (a, b)), lambda r, g: (mm_nt(g, r[1]), mm_tn(r[0], g)))
    mm_nt.defvjp(lambda a, b: (mm_nt(a, b), (a, b)), lambda r, g: (mm(g, r[1]), mm_tn(g, r[0])))
    mm_tn.defvjp(lambda a, b: (mm_tn(a, b), (a, b)), lambda r, g: (mm_nt(r[1], g), mm(r[0], g)))
    return mm, mm_nt, mm_tn


_mm, _mm_nt, _mm_tn = _make_mm(False)
_mmh, _mmh_nt, _mmh_tn = _make_mm(True)


def _shift_rows_raw(x, s):
    if s == 0:
        return x
    n = x.shape[0]
    rolled = pltpu.roll(x, (-s) % n, 0)
    t = lax.broadcasted_iota(jnp.int32, x.shape, 0)
    ok = (t + s >= 0) & (t + s < n)
    return jnp.where(ok, rolled, 0.0)


@functools.partial(jax.custom_vjp, nondiff_argnums=(1,))
def _shift_rows(x, s):
    return _shift_rows_raw(x, s)


_shift_rows.defvjp(lambda x, s: (_shift_rows_raw(x, s), None), lambda s, _, g: (_shift_rows_raw(g, -s),))


def _sigmoid(x):
    return 1.0 / (1.0 + jnp.exp(-x))


def _silu(x):
    return x * _sigmoid(x)


def _gelu(x):
    return 0.5 * x * (1.0 + jnp.tanh(math.sqrt(2.0 / math.pi) * (x + 0.044715 * x * x * x)))


def _softplus(x):
    return jnp.maximum(x, 0.0) + jnp.log(1.0 + jnp.exp(-jnp.abs(x)))


def _rms(x, g):
    return x * lax.rsqrt(jnp.mean(x * x, axis=-1, keepdims=True) + NORM_EPS) * g


def _conv_rows(x, taps):
    pad = len(taps) // 2
    acc = None
    for j, w in enumerate(taps):
        term = _shift_rows(x, j - pad) * w
        acc = term if acc is None else acc + term
    return acc


@jax.custom_vjp
def _inv_unit(mats):
    n = mats[0].shape[0]
    eye = (lax.broadcasted_iota(jnp.int32, (n, n), 0) == lax.broadcasted_iota(jnp.int32, (n, n), 1)).astype(F32)
    ps = [eye - a for a in mats]
    aks = list(mats)
    for _ in range(int(math.log2(n)) - 1):
        aks = [_mmh(ak, ak) for ak in aks]
        ps = [p + _mmh(p, ak) for p, ak in zip(ps, aks)]
    return ps


def _inv_unit_fwd(mats):
    ts = _inv_unit(mats)
    return ts, ts


def _inv_unit_bwd(ts, gs):
    inner = [_mmh_nt(g, t) for g, t in zip(gs, ts)]
    return ([-_mmh_tn(t, m) for t, m in zip(ts, inner)],)


_inv_unit.defvjp(_inv_unit_fwd, _inv_unit_bwd)


def _tiled(fn, name, grid, ins, outs, sem=None):
    n_in = len(ins)

    def body(*refs):
        vals = [r[...] for r in refs[:n_in]]
        res = fn(*vals)
        if not isinstance(res, (tuple, list)):
            res = (res,)
        for o_ref, r, spec in zip(refs[n_in:], res, outs):
            acc = spec[4]
            if not acc:
                o_ref[...] = r.astype(o_ref.dtype)
            else:
                first = functools.reduce(jnp.logical_and, [pl.program_id(a) == 0 for a in acc])

                @pl.when(first)
                def _(o_ref=o_ref, r=r):
                    o_ref[...] = r.astype(o_ref.dtype)

                @pl.when(jnp.logical_not(first))
                def _(o_ref=o_ref, r=r):
                    o_ref[...] += r.astype(o_ref.dtype)

    return pl.pallas_call(
        body, name=name, grid=grid,
        in_specs=[pl.BlockSpec(b, m) for _, b, m in ins],
        out_specs=[pl.BlockSpec(s[2], s[3]) for s in outs],
        out_shape=[jax.ShapeDtypeStruct(s[0], s[1]) for s in outs],
        compiler_params=_params(sem),
    )(*[a for a, _, _ in ins])


def _matmul(a, b, mode, out_dtype, name, add=None, a_split=1, b_split=1, o_split=1, after=()):
    def dims2(x, split):
        return (x.shape[-2], x.shape[-1] * split)

    ar, ac = dims2(a, a_split)
    br, bc = dims2(b, b_split)
    if mode == 'nn':
        M, K, N = ar, ac, bc
        assert br == K
    elif mode == 'nt':
        M, K, N = ar, ac, br
        assert bc == K
    else:
        K, M, N = ar, ac, bc
        assert br == K
    tm = _pick(M // (a_split if mode == 'tn' else 1), 512)
    tn = _pick(N // max(o_split, b_split if mode != 'nt' else 1), 1408)
    tk = _pick(K // max(a_split if mode != 'tn' else 1, b_split if mode == 'nt' else 1), 2048)
    gm, gn, gk = M // tm, N // tn, K // tk

    def col_map(split, total_cols, tile):
        per = total_cols // split // tile

        def f(r, c):
            return (c // per, r, c % per) if split > 1 else (r, c)
        return f

    if mode == 'nn':
        a_idx = col_map(a_split, K, tk)
        b_idx = col_map(b_split, N, tn)
        a_spec = pl.BlockSpec(((None,) if a_split > 1 else ()) + (tm, tk), lambda m, n, k: a_idx(m, k))
        b_spec = pl.BlockSpec(((None,) if b_split > 1 else ()) + (tk, tn), lambda m, n, k: b_idx(k, n))
        dn = ((1,), (0,))
    elif mode == 'nt':
        a_idx = col_map(a_split, K, tk)
        b_idx = col_map(b_split, K, tk)
        a_spec = pl.BlockSpec(((None,) if a_split > 1 else ()) + (tm, tk), lambda m, n, k: a_idx(m, k))
        b_spec = pl.BlockSpec(((None,) if b_split > 1 else ()) + (tn, tk), lambda m, n, k: b_idx(n, k))
        dn = ((1,), (1,))
    else:
        a_idx = col_map(a_split, M, tm)
        b_idx = col_map(b_split, N, tn)
        a_spec = pl.BlockSpec(((None,) if a_split > 1 else ()) + (tk, tm), lambda m, n, k: a_idx(k, m))
        b_spec = pl.BlockSpec(((None,) if b_split > 1 else ()) + (tk, tn), lambda m, n, k: b_idx(k, n))
        dn = ((0,), (0,))
    o_idx = col_map(o_split, N, tn)
    o_block = ((None,) if o_split > 1 else ()) + (tm, tn)
    o_spec = pl.BlockSpec(o_block, lambda m, n, k: o_idx(m, n))
    o_shape = ((o_split, M, N // o_split) if o_split > 1 else (M, N))
    in_specs = [a_spec, b_spec]
    args = [a, b]
    if add is not None:
        in_specs.append(pl.BlockSpec((tm, tn), lambda m, n, k: (m, n)))
        args.append(add)
    n_lead = len(args)
    for t in after:
        in_specs.append(pl.BlockSpec(memory_space=pl.ANY))
        args.append(t)

    def body(*refs):
        a_ref, b_ref = refs[0], refs[1]
        add_ref = refs[2] if add is not None else None
        o_ref = refs[n_lead + len(after)]
        part = lax.dot_general(a_ref[...].astype(BF16), b_ref[...].astype(BF16), (dn, ((), ())),
                               preferred_element_type=F32)

        def finish(total):
            if add_ref is not None:
                total = total + add_ref[...]
            o_ref[...] = total.astype(o_ref.dtype)

        if gk == 1:
            finish(part)
        else:
            acc_ref = refs[-1]
            k = pl.program_id(2)

            @pl.when(k == 0)
            def _():
                acc_ref[...] = part

            @pl.when(jnp.logical_and(k > 0, k < gk - 1))
            def _():
                acc_ref[...] += part

            @pl.when(k == gk - 1)
            def _():
                finish(acc_ref[...] + part)

    return pl.pallas_call(
        body, name=name, grid=(gm, gn, gk), in_specs=in_specs, out_specs=o_spec,
        out_shape=jax.ShapeDtypeStruct(o_shape, out_dtype),
        scratch_shapes=([pltpu.VMEM((tm, tn), F32)] if gk > 1 else []),
        compiler_params=_params(("parallel", "parallel", "arbitrary")),
    )(*args)


def _row_tile():
    return _pick(SEQ, 256, 8)


def _rms_fwd(x, g, name):
    tm = _row_tile()
    (h,) = _tiled(lambda xv, gv: _rms(xv, gv), name, (SEQ // tm,),
                  [(x, (tm, D_MODEL), lambda i: (i, 0)), (g, (1, D_MODEL), lambda i: (0, 0))],
                  [((SEQ, D_MODEL), BF16, (tm, D_MODEL), lambda i: (i, 0), ())])
    return h


def _rms_bwd(x, g, dh, dres, name):
    tm = _row_tile()

    def fn(xv, gv, dhv, drv):
        _, vjp = jax.vjp(_rms, xv, gv)
        dx, dg = vjp(dhv)
        dx = dx + drv
        return dx, dx, dg

    row = lambda i: (i, 0)
    return _tiled(fn, name, (SEQ // tm,),
                  [(x, (tm, D_MODEL), row), (g, (1, D_MODEL), lambda i: (0, 0)),
                   (dh, (tm, D_MODEL), row), (dres, (tm, D_MODEL), row)],
                  [((SEQ, D_MODEL), F32, (tm, D_MODEL), row, ()),
                   ((SEQ, D_MODEL), BF16, (tm, D_MODEL), row, ()),
                   ((1, D_MODEL), F32, (1, D_MODEL), lambda i: (0, 0), (0,))])


def _loss_head(x, g, target):
    tm = _row_tile()

    def fn(xv, gv, tv):
        y, vjp = jax.vjp(_rms, xv, gv)
        err = y - tv
        part = 0.5 * jnp.sum(jnp.sum(err * err, axis=1, keepdims=True), axis=0, keepdims=True) / D_MODEL
        dx, dg = vjp(err / D_MODEL)
        return part, dx, dx, dg

    row = lambda i: (i, 0)
    one = lambda i: (0, 0)
    return _tiled(fn, "loss_head", (SEQ // tm,),
                  [(x, (tm, D_MODEL), row), (g, (1, D_MODEL), one), (target, (tm, D_MODEL), row)],
                  [((1, 1), F32, (1, 1), one, (0,)),
                   ((SEQ, D_MODEL), F32, (tm, D_MODEL), row, ()),
                   ((SEQ, D_MODEL), BF16, (tm, D_MODEL), row, ()),
                   ((1, D_MODEL), F32, (1, D_MODEL), one, (0,))])


def _p_col(width, index):
    return lambda tm: ((tm, width), lambda i: (i, index))


def _merge_fn(ga, gb, ta, tb):
    return _sigmoid(ga) * ta + _sigmoid(gb) * tb


def _merge_fwd(p, ta, tb):
    tm = _row_tile()
    d = D_MODEL
    ga_blk = 4 * _gw() + 2 * _sw()
    assert ga_blk % d == 0
    ia, ib = ga_blk // d, ga_blk // d + 1
    row = lambda i: (i, 0)
    (m,) = _tiled(_merge_fn, "merge_fwd", (SEQ // tm,),
                  [(p, (tm, d), lambda i: (i, ia)), (p, (tm, d), lambda i: (i, ib)),
                   (ta, (tm, d), row), (tb, (tm, d), row)],
                  [((SEQ, d), BF16, (tm, d), row, ())])
    return m


def _merge_bwd(p, ta, tb, dm):
    tm = _row_tile()
    d = D_MODEL
    ga_blk = 4 * _gw() + 2 * _sw()
    ia, ib = ga_blk // d, ga_blk // d + 1

    def fn(ga, gb, tav, tbv, dmv):
        _, vjp = jax.vjp(_merge_fn, ga, gb, tav, tbv)
        return vjp(dmv)

    row = lambda i: (i, 0)
    out = ((SEQ, d), BF16, (tm, d), row, ())
    return _tiled(fn, "merge_bwd", (SEQ // tm,),
                  [(p, (tm, d), lambda i: (i, ia)), (p, (tm, d), lambda i: (i, ib)),
                   (ta, (tm, d), row), (tb, (tm, d), row), (dm, (tm, d), row)],
                  [out, out, out, out])


def _gate_fn(ab, alog, dtb):
    lane = lax.broadcasted_iota(jnp.int32, ab.shape, 1)
    nd = 2 * GDN_HEADS
    g = -jnp.exp(alog) * _softplus(ab + dtb)
    beta = _sigmoid(ab)
    return jnp.where(lane < nd, g, jnp.where(lane < 2 * nd, beta, 0.0))


def _ab_index():
    off = 4 * _gw() + 2 * _sw() + 2 * D_MODEL
    assert off % AB_PAD == 0
    return off // AB_PAD


def _gate_fwd(p, alog, dtb):
    tm = _row_tile()
    iab = _ab_index()
    one = lambda i: (0, 0)
    (g,) = _tiled(_gate_fn, "gdn_gate_fwd", (SEQ // tm,),
                  [(p, (tm, AB_PAD), lambda i: (i, iab)), (alog, (1, AB_PAD), one), (dtb, (1, AB_PAD), one)],
                  [((SEQ, AB_PAD), F32, (tm, AB_PAD), lambda i: (i, 0), ())])
    return g


def _gate_bwd(p, alog, dtb, dg):
    tm = _row_tile()
    iab = _ab_index()
    one = lambda i: (0, 0)

    def fn(ab, al, db, dgv):
        _, vjp = jax.vjp(_gate_fn, ab, al, db)
        return vjp(dgv)

    return _tiled(fn, "gdn_gate_bwd", (SEQ // tm,),
                  [(p, (tm, AB_PAD), lambda i: (i, iab)), (alog, (1, AB_PAD), one), (dtb, (1, AB_PAD), one),
                   (dg, (tm, AB_PAD), lambda i: (i, 0))],
                  [((SEQ, AB_PAD), BF16, (tm, AB_PAD), lambda i: (i, 0), ()),
                   ((1, AB_PAD), F32, (1, AB_PAD), one, (0,)),
                   ((1, AB_PAD), F32, (1, AB_PAD), one, (0,))])


def _l2n(x):
    return x * lax.rsqrt(jnp.sum(x * x, axis=-1, keepdims=True) + NORM_EPS)


def _qkv_fn(xq, xk, xv, *taps):
    n = QKV_CONV
    q = _l2n(_silu(_conv_rows(xq, taps[0:n])))
    k = _l2n(_silu(_conv_rows(xk, taps[n:2 * n])))
    v = _silu(_conv_rows(xv, taps[2 * n:3 * n]))
    return q, k, v


def _qkv_specs(p, conv_w):
    hd, nh = HEAD_DIM, GDN_HEADS
    ins = [(p, (SEQ, hd), (lambda h, s=s: (0, s * nh + h))) for s in range(3)]
    ins += [(conv_w, (QKV_CONV, hd), (lambda h, s=s: (0, s * nh + h))) for s in range(3)]
    return ins


def _qkv_fwd(p, conv_w):
    hd, nh = HEAD_DIM, GDN_HEADS
    ins = _qkv_specs(p, conv_w)
    n_in = len(ins)
    out_spec = pl.BlockSpec((SEQ, hd), lambda h: (0, h))

    def body(*refs):
        xs = [r[...] for r in refs[:3]]
        taps = [refs[3 + s][j:j + 1, :] for s in range(3) for j in range(QKV_CONV)]
        q, k, v = _qkv_fn(*xs, *taps)
        refs[n_in][...] = q
        refs[n_in + 1][...] = k
        refs[n_in + 2][...] = v

    return pl.pallas_call(
        body, name="gdn_qkv_fwd", grid=(nh,),
        in_specs=[pl.BlockSpec(b, m) for _, b, m in ins], out_specs=[out_spec] * 3,
        out_shape=[jax.ShapeDtypeStruct((SEQ, nh * hd), F32)] * 3,
        compiler_params=_params(("parallel",)),
    )(*[a for a, _, _ in ins])


def _qkv_bwd(p, conv_w, dq, dk, dv):
    hd, nh = HEAD_DIM, GDN_HEADS
    ins = _qkv_specs(p, conv_w) + [(t, (SEQ, hd), lambda h: (0, h)) for t in (dq, dk, dv)]
    n_in = len(ins)

    def body(*refs):
        xs = [r[...] for r in refs[:3]]
        taps = [refs[3 + s][j:j + 1, :] for s in range(3) for j in range(QKV_CONV)]
        cts = tuple(r[...] for r in refs[6:9])
        _, vjp = jax.vjp(_qkv_fn, *xs, *taps)
        grads = vjp(cts)
        for s in range(3):
            refs[n_in + s][...] = grads[s].astype(BF16)
            for j in range(QKV_CONV):
                refs[n_in + 3 + s][j:j + 1, :] = jnp.sum(grads[3 + s * QKV_CONV + j], axis=0, keepdims=True)

    dx_spec = pl.BlockSpec((SEQ, hd), lambda h: (0, h))
    dw_spec = pl.BlockSpec((QKV_CONV, hd), lambda h: (0, h))
    outs = pl.pallas_call(
        body, name="gdn_qkv_bwd", grid=(nh,),
        in_specs=[pl.BlockSpec(b, m) for _, b, m in ins], out_specs=[dx_spec] * 3 + [dw_spec] * 3,
        out_shape=[jax.ShapeDtypeStruct((SEQ, nh * hd), BF16)] * 3
        + [jax.ShapeDtypeStruct((QKV_CONV, nh * hd), F32)] * 3,
        compiler_params=_params(("parallel",)),
    )(*[a for a, _, _ in ins])
    return outs[0:3], jnp.concatenate(outs[3:6], axis=1)


def _prep_fn(qs, ks, vs, gblk):
    nh = len(qs)
    c = qs[0].shape[0]
    scale = HEAD_DIM ** -0.5
    ii = lax.broadcasted_iota(jnp.int32, (c, c), 0)
    jj = lax.broadcasted_iota(jnp.int32, (c, c), 1)
    eye = (ii == jj).astype(F32)
    lane = lax.broadcasted_iota(jnp.int32, gblk.shape, 1)
    kk_t = [_mm_nt(k, k) for k in ks]
    qk_t = [_mm_nt(q, k) for q, k in zip(qs, ks)]
    chains = [(h, d) for h in range(nh) for d in range(2)]
    pre = []
    for h, d in chains:
        g = jnp.sum(jnp.where(lane == h + d * nh, gblk, 0.0), axis=1, keepdims=True)
        beta = jnp.sum(jnp.where(lane == h + (2 + d) * nh, gblk, 0.0), axis=1, keepdims=True)
        incl = (jj <= ii) if d == 0 else (jj >= ii)
        strict = (jj < ii) if d == 0 else (jj > ii)
        incl_t = (ii <= jj) if d == 0 else (ii >= jj)
        g_row = jnp.sum(eye * g, axis=0, keepdims=True)
        gc = jnp.sum(jnp.where(incl, g_row, 0.0), axis=1, keepdims=True)
        gc_row = jnp.sum(jnp.where(incl_t, g, 0.0), axis=0, keepdims=True)
        decay = jnp.where(incl, jnp.exp(jnp.where(incl, gc - gc_row, 0.0)), 0.0)
        a = jnp.where(strict, kk_t[h] * beta * decay, 0.0)
        pre.append((g, beta, gc, decay, incl, a))
    ts = _inv_unit([p[5] for p in pre])
    us = [_mmh(t, vs[h] * p[1]) for (h, d), p, t in zip(chains, pre, ts)]
    ws = [_mmh(t, ks[h] * (p[1] * jnp.exp(p[2]))) for (h, d), p, t in zip(chains, pre, ts)]
    outs = [[None, None] for _ in range(nh)]
    for (h, d), (g, beta, gc, decay, incl, a), u, w in zip(chains, pre, us, ws):
        qk = jnp.where(incl, qk_t[h] * (scale * decay), 0.0)
        g_last = jnp.sum(g, axis=0, keepdims=True)
        q_dec = qs[h] * (scale * jnp.exp(gc))
        k_dec = ks[h] * jnp.exp(g_last - gc)
        outs[h][d] = (u, w, q_dec, k_dec, qk, jnp.exp(g_last))
    return outs


def _n_chunks():
    return SEQ // GDN_CHUNK


def _gdn_prep_fwd(q, k, v, gates):
    c, hd, nh, nc = GDN_CHUNK, HEAD_DIM, GDN_HEADS, _n_chunks()
    gw = nh * hd
    row = lambda n: (n, 0)

    def body(q_ref, k_ref, v_ref, g_ref, u_ref, w_ref, qd_ref, kd_ref, qk_ref, dec_ref):
        heads = [slice(h * hd, (h + 1) * hd) for h in range(nh)]
        res = _prep_fn([q_ref[:, s] for s in heads], [k_ref[:, s] for s in heads], [v_ref[:, s] for s in heads],
                       g_ref[...])
        for h, cols in enumerate(heads):
            for d in range(2):
                u, w, qd, kd, qk, dec = res[h][d]
                u_ref[d, :, cols] = u
                w_ref[d, :, cols] = w
                qd_ref[d, :, cols] = qd
                kd_ref[d, :, cols] = kd
                qk_ref[d, h] = qk
                dec_ref[d, h:h + 1, :] = jnp.broadcast_to(dec, (1, LANES))

    wide = pl.BlockSpec((2, c, gw), lambda n: (0, n, 0))
    return pl.pallas_call(
        body, name="gdn_prep_fwd", grid=(nc,),
        in_specs=[pl.BlockSpec((c, gw), row)] * 3 + [pl.BlockSpec((c, LANES), row)],
        out_specs=[wide] * 4 + [pl.BlockSpec((2, nh, c, c), lambda n: (0, 0, n, 0)),
                                pl.BlockSpec((2, None, nh, LANES), lambda n: (0, n, 0, 0))],
        out_shape=[jax.ShapeDtypeStruct((2, SEQ, gw), F32)] * 4
        + [jax.ShapeDtypeStruct((2, nh, SEQ, c), F32), jax.ShapeDtypeStruct((2, nc, nh, LANES), F32)],
        compiler_params=_params(("parallel",)),
    )(q, k, v, gates)


def _gdn_prep_bwd(q, k, v, gates, cts):
    c, hd, nh, nc = GDN_CHUNK, HEAD_DIM, GDN_HEADS, _n_chunks()
    gw = nh * hd
    row = lambda n: (n, 0)

    def body(*refs):
        q_ref, k_ref, v_ref, g_ref = refs[:4]
        ct_refs = refs[4:16]
        dq_ref, dk_ref, dv_ref, dg_ref = refs[16:20]
        heads = [slice(h * hd, (h + 1) * hd) for h in range(nh)]
        _, vjp = jax.vjp(_prep_fn, [q_ref[:, s] for s in heads], [k_ref[:, s] for s in heads],
                         [v_ref[:, s] for s in heads], g_ref[...])
        cts_in = []
        for h, cols in enumerate(heads):
            per_dir = []
            for d in range(2):
                r = ct_refs[6 * d:6 * d + 6]
                per_dir.append((r[0][:, cols], r[1][:, cols], r[2][:, cols], r[3][:, cols], r[4][h],
                                r[5][h:h + 1, 0:1]))
            cts_in.append(per_dir)
        dqs, dks, dvs, dg = vjp(cts_in)
        for h, cols in enumerate(heads):
            dq_ref[:, cols] = dqs[h]
            dk_ref[:, cols] = dks[h]
            dv_ref[:, cols] = dvs[h]
        dg_ref[...] = dg

    one_dir = [pl.BlockSpec((c, gw), row)] * 4 + [pl.BlockSpec((nh, c, c), lambda n: (0, n, 0)),
                                                  pl.BlockSpec((None, nh, LANES), lambda n: (n, 0, 0))]
    return pl.pallas_call(
        body, name="gdn_prep_bwd", grid=(nc,),
        in_specs=[pl.BlockSpec((c, gw), row)] * 3 + [pl.BlockSpec((c, LANES), row)] + one_dir * 2,
        out_specs=[pl.BlockSpec((c, gw), row)] * 3 + [pl.BlockSpec((c, LANES), row)],
        out_shape=[jax.ShapeDtypeStruct((SEQ, gw), F32)] * 3 + [jax.ShapeDtypeStruct((SEQ, LANES), F32)],
        compiler_params=_params(("parallel",)),
    )(q, k, v, gates, *cts)


def _scan_steps(states, us, ws, qds, kds, qks, decs):
    w_s = [_mm(w, s) for w, s in zip(ws, states)]
    q_s = [_mm(qd, s) for qd, s in zip(qds, states)]
    v_new = [u - x for u, x in zip(us, w_s)]
    outs = [x + _mm(qk, vn) for x, qk, vn in zip(q_s, qks, v_new)]
    k_v = [_mm_tn(kd, vn) for kd, vn in zip(kds, v_new)]
    new_states = [s * dec + x for s, dec, x in zip(states, decs, k_v)]
    return new_states, outs


def _scan_operands(ins, chains, dec_lanes):
    cols = lambda h: slice(h * HEAD_DIM, (h + 1) * HEAD_DIM)
    wide = [[ins[6 * d + j][:, cols(h)] for d, h in chains] for j in range(4)]
    qks = [ins[6 * d + 4][h] for d, h in chains]
    decs = [ins[6 * d + 5][h:h + 1, dec_lanes] for d, h in chains]
    return (*wide, qks, decs)


def _scan_in_specs(chunk_of):
    c, hd, nh = GDN_CHUNK, HEAD_DIM, GDN_HEADS
    gw = nh * hd
    specs = []
    for d in range(2):
        f = chunk_of[d]
        specs += [pl.BlockSpec((None, c, gw), lambda n, d=d, f=f: (d, f(n), 0))] * 4
        specs += [pl.BlockSpec((None, nh, c, c), lambda n, d=d, f=f: (d, 0, f(n), 0)),
                  pl.BlockSpec((None, None, nh, LANES), lambda n, d=d, f=f: (d, f(n), 0, 0))]
    return specs


def _gdn_scan_fwd(u, w, qd, kd, qk, dec):
    c, hd, nh, nc = GDN_CHUNK, HEAD_DIM, GDN_HEADS, _n_chunks()
    gw = nh * hd
    chunk_of = (lambda n: n, lambda n: nc - 1 - n)

    def body(*refs):
        ins = refs[:12]
        o_refs = refs[12:14]
        st_refs = refs[14:16]
        s_ref = refs[16]

        @pl.when(pl.program_id(0) == 0)
        def _():
            s_ref[...] = jnp.zeros_like(s_ref)

        chains = [(d, h) for d in range(2) for h in range(nh)]
        states = [s_ref[d * nh + h] for d, h in chains]
        for (d, h), st in zip(chains, states):
            st_refs[d][h] = st
        new_states, outs = _scan_steps(states, *_scan_operands(ins, chains, slice(None)))
        for (d, h), st, o in zip(chains, new_states, outs):
            s_ref[d * nh + h] = st
            o_refs[d][:, h * hd:(h + 1) * hd] = o

    o_specs = [pl.BlockSpec((c, gw), lambda n, f=f: (f(n), 0)) for f in chunk_of]
    st_specs = [pl.BlockSpec((None, nh, hd, hd), lambda n, f=f: (f(n), 0, 0, 0)) for f in chunk_of]
    return pl.pallas_call(
        body, name="gdn_scan_fwd", grid=(nc,), in_specs=_scan_in_specs(chunk_of),
        out_specs=o_specs + st_specs,
        out_shape=[jax.ShapeDtypeStruct((SEQ, gw), F32)] * 2 + [jax.ShapeDtypeStruct((nc, nh, hd, hd), F32)] * 2,
        scratch_shapes=[pltpu.VMEM((2 * nh, hd, hd), F32)],
        compiler_params=_params(("arbitrary",)),
    )(*([u, w, qd, kd, qk, dec] * 2))


def _gdn_scan_bwd(u, w, qd, kd, qk, dec, st0, st1, do):
    c, hd, nh, nc = GDN_CHUNK, HEAD_DIM, GDN_HEADS, _n_chunks()
    gw = nh * hd
    chunk_of = (lambda n: nc - 1 - n, lambda n: n)

    def body(*refs):
        ins = refs[:12]
        st_in = refs[12:14]
        do_in = refs[14:16]
        outs = refs[16:28]
        ds_ref = refs[28]

        @pl.when(pl.program_id(0) == 0)
        def _():
            ds_ref[...] = jnp.zeros_like(ds_ref)

        chains = [(d, h) for d in range(2) for h in range(nh)]
        states = [st_in[d][h] for d, h in chains]
        _, vjp = jax.vjp(_scan_steps, states, *_scan_operands(ins, chains, slice(0, 1)))
        ct_state = [ds_ref[d * nh + h] for d, h in chains]
        ct_out = [do_in[d][:, h * hd:(h + 1) * hd] for d, h in chains]
        grads = vjp((ct_state, ct_out))
        for i, (d, h) in enumerate(chains):
            cols = slice(h * hd, (h + 1) * hd)
            du_r, dw_r, dqd_r, dkd_r, dqk_r, ddec_r = outs[6 * d:6 * d + 6]
            ds_ref[d * nh + h] = grads[0][i]
            du_r[:, cols] = grads[1][i]
            dw_r[:, cols] = grads[2][i]
            dqd_r[:, cols] = grads[3][i]
            dkd_r[:, cols] = grads[4][i]
            dqk_r[h] = grads[5][i]
            ddec_r[h:h + 1, :] = jnp.broadcast_to(grads[6][i], (1, LANES))

    st_specs = [pl.BlockSpec((None, nh, hd, hd), lambda n, f=f: (f(n), 0, 0, 0)) for f in chunk_of]
    do_specs = [pl.BlockSpec((c, gw), lambda n, f=f: (f(n), 0)) for f in chunk_of]
    out_specs, out_shape = [], []
    for f in chunk_of:
        out_specs += [pl.BlockSpec((c, gw), lambda n, f=f: (f(n), 0))] * 4
        out_specs += [pl.BlockSpec((nh, c, c), lambda n, f=f: (0, f(n), 0)),
                      pl.BlockSpec((None, nh, LANES), lambda n, f=f: (f(n), 0, 0))]
        out_shape += [jax.ShapeDtypeStruct((SEQ, gw), F32)] * 4
        out_shape += [jax.ShapeDtypeStruct((nh, SEQ, c), F32), jax.ShapeDtypeStruct((nc, nh, LANES), F32)]
    return pl.pallas_call(
        body, name="gdn_scan_bwd", grid=(nc,), in_specs=_scan_in_specs(chunk_of) + st_specs + do_specs,
        out_specs=out_specs, out_shape=out_shape,
        scratch_shapes=[pltpu.VMEM((2 * nh, hd, hd), F32)],
        compiler_params=_params(("arbitrary",)),
    )(*([u, w, qd, kd, qk, dec] * 2), st0, st1, do, do)


def _post_fn(o0, o1, z, gn):
    return _rms(o0 + o1, gn) * _silu(z)


def _gdn_post_fwd(o0, o1, p, gn):
    tm, hd, nh = _row_tile(), HEAD_DIM, GDN_HEADS
    zoff = 3 * nh
    blk = lambda i, h: (i, h)
    (ya,) = _tiled(_post_fn, "gdn_post_fwd", (SEQ // tm, nh),
                   [(o0, (tm, hd), blk), (o1, (tm, hd), blk),
                    (p, (tm, hd), lambda i, h: (i, zoff + h)), (gn, (1, hd), lambda i, h: (0, 0))],
                   [((SEQ, nh * hd), BF16, (tm, hd), blk, ())])
    return ya


def _gdn_post_bwd(o0, o1, p, gn, dya):
    tm, hd, nh = _row_tile(), HEAD_DIM, GDN_HEADS
    zoff = 3 * nh
    blk = lambda i, h: (i, h)

    def fn(a, b, z, g, dy):
        _, vjp = jax.vjp(_post_fn, a, b, z, g)
        do, _, dz, dg = vjp(dy.astype(F32))
        return do, dz, dg

    return _tiled(fn, "gdn_post_bwd", (SEQ // tm, nh),
                  [(o0, (tm, hd), blk), (o1, (tm, hd), blk),
                   (p, (tm, hd), lambda i, h: (i, zoff + h)), (gn, (1, hd), lambda i, h: (0, 0)),
                   (dya, (tm, hd), blk)],
                  [((SEQ, nh * hd), F32, (tm, hd), blk, ()),
                   ((SEQ, nh * hd), BF16, (tm, hd), blk, ()),
                   ((1, hd), F32, (1, hd), lambda i, h: (0, 0), (0, 1))],
                  sem=("arbitrary", "arbitrary"))


def _sgu_ln(v, lg, lb):
    gv = _gelu(v)
    mu = jnp.mean(gv, axis=-1, keepdims=True)
    cen = gv - mu
    var = jnp.mean(cen * cen, axis=-1, keepdims=True)
    return cen * lax.rsqrt(var + NORM_EPS) * lg + lb


def _sgu_mix(vn, w_ref, bt_ref):
    parts = []
    for g in range(SGU_GROUPS):
        cols = slice(g * SGU_DIM, (g + 1) * SGU_DIM)
        parts.append(_mm(w_ref[g], vn[:, cols]) + bt_ref[:, g:g + 1])
    return jnp.concatenate(parts, axis=1)


def _sgu_specs(p):
    sw, sb = _sw(), SGU_BLOCK
    uoff = 4 * _gw()
    assert uoff % sw == 0
    iu = uoff // sw
    return [(p, (sb, sw), lambda n: (n, iu)), (p, (sb, sw), lambda n: (n, iu + 1))]


def _sgu_fwd(p, lg, lb, w_s, b_t):
    sw, sb = _sw(), SGU_BLOCK
    ins = _sgu_specs(p)

    def body(u_ref, v_ref, lg_ref, lb_ref, w_ref, bt_ref, y_ref):
        vn = _sgu_ln(v_ref[...], lg_ref[...], lb_ref[...])
        y_ref[...] = (_gelu(u_ref[...]) * _sgu_mix(vn, w_ref, bt_ref)).astype(BF16)

    one2 = lambda n: (0, 0)
    return pl.pallas_call(
        body, name="sgu_fwd", grid=(SEQ // sb,),
        in_specs=[pl.BlockSpec(b, m) for _, b, m in ins]
        + [pl.BlockSpec((1, sw), one2), pl.BlockSpec((1, sw), one2),
           pl.BlockSpec((SGU_GROUPS, sb, sb), lambda n: (0, 0, 0)), pl.BlockSpec((sb, LANES), one2)],
        out_specs=pl.BlockSpec((sb, sw), lambda n: (n, 0)),
        out_shape=jax.ShapeDtypeStruct((SEQ, sw), BF16),
        compiler_params=_params(("parallel",)),
    )(p, p, lg, lb, w_s, b_t)


def _sgu_bwd(p, lg, lb, w_s, b_t, dy):
    sw, sb, ng = _sw(), SGU_BLOCK, SGU_GROUPS
    ins = _sgu_specs(p)

    def body(u_ref, v_ref, lg_ref, lb_ref, w_ref, bt_ref, dy_ref, du_ref, dv_ref, dlg_ref, dlb_ref, dw_ref, dbt_ref):
        first = pl.program_id(0) == 0
        u, v = u_ref[...], v_ref[...]
        gu, gelu_vjp = jax.vjp(_gelu, u)
        vn, ln_vjp = jax.vjp(_sgu_ln, v, lg_ref[...], lb_ref[...])
        s = _sgu_mix(vn, w_ref, bt_ref)
        dyv = dy_ref[...].astype(F32)
        ds = dyv * gu
        (du,) = gelu_vjp(dyv * s)
        lane = lax.broadcasted_iota(jnp.int32, (sb, LANES), 1)
        dvn_parts = []
        dbt = jnp.zeros((sb, LANES), F32)
        for g in range(ng):
            cols = slice(g * SGU_DIM, (g + 1) * SGU_DIM)
            ds_g = ds[:, cols]
            dw_g = _mm_nt(ds_g, vn[:, cols])
            dvn_parts.append(_mm_tn(w_ref[g], ds_g))
            dbt = dbt + jnp.where(lane == g, jnp.sum(ds_g, axis=1, keepdims=True), 0.0)

            @pl.when(first)
            def _(g=g, dw_g=dw_g):
                dw_ref[g] = dw_g

            @pl.when(jnp.logical_not(first))
            def _(g=g, dw_g=dw_g):
                dw_ref[g] += dw_g

        dv, dlg, dlb = ln_vjp(jnp.concatenate(dvn_parts, axis=1))
        du_ref[...] = du.astype(BF16)
        dv_ref[...] = dv.astype(BF16)

        @pl.when(first)
        def _():
            dlg_ref[...] = dlg
            dlb_ref[...] = dlb
            dbt_ref[...] = dbt

        @pl.when(jnp.logical_not(first))
        def _():
            dlg_ref[...] += dlg
            dlb_ref[...] += dlb
            dbt_ref[...] += dbt

    one2 = lambda n: (0, 0)
    row = pl.BlockSpec((sb, sw), lambda n: (n, 0))
    return pl.pallas_call(
        body, name="sgu_bwd", grid=(SEQ // sb,),
        in_specs=[pl.BlockSpec(b, m) for _, b, m in ins]
        + [pl.BlockSpec((1, sw), one2), pl.BlockSpec((1, sw), one2),
           pl.BlockSpec((ng, sb, sb), lambda n: (0, 0, 0)), pl.BlockSpec((sb, LANES), one2), row],
        out_specs=[row, row, pl.BlockSpec((1, sw), one2), pl.BlockSpec((1, sw), one2),
                   pl.BlockSpec((ng, sb, sb), lambda n: (0, 0, 0)), pl.BlockSpec((sb, LANES), one2)],
        out_shape=[jax.ShapeDtypeStruct((SEQ, sw), BF16)] * 2 + [jax.ShapeDtypeStruct((1, sw), F32)] * 2
        + [jax.ShapeDtypeStruct((ng, sb, sb), F32), jax.ShapeDtypeStruct((sb, LANES), F32)],
        compiler_params=_params(("arbitrary",)),
    )(p, p, lg, lb, w_s, b_t, dy)


def _ffn_fn(xg, xv, bg, bv, *taps):
    n = FFN_CONV
    cg = _conv_rows(xg, taps[0:n]) + bg
    cv = _conv_rows(xv, taps[n:2 * n]) + bv
    return _silu(cg) * cv


def _ffn_tile():
    return _pick(D_FF, 256)


def _ffn_specs(up, conv_w, conv_b):
    tc = _ffn_tile()
    nt = D_FF // tc
    ins = [(up, (None, SEQ, tc), (lambda j, s=s: (s, 0, j))) for s in range(2)]
    ins += [(conv_b, (1, tc), (lambda j, s=s: (0, s * nt + j))) for s in range(2)]
    ins += [(conv_w, (FFN_CONV, tc), (lambda j, s=s: (0, s * nt + j))) for s in range(2)]
    return ins


def _ffn_act_fwd(up, conv_w, conv_b):
    tc = _ffn_tile()
    ins = _ffn_specs(up, conv_w, conv_b)

    def body(xg, xv, bg, bv, wg, wv, o_ref):
        taps = [w[j:j + 1, :] for w in (wg, wv) for j in range(FFN_CONV)]
        o_ref[...] = _ffn_fn(xg[...], xv[...], bg[...], bv[...], *taps).astype(BF16)

    return pl.pallas_call(
        body, name="ffn_act_fwd", grid=(D_FF // tc,),
        in_specs=[pl.BlockSpec(b, m) for _, b, m in ins],
        out_specs=pl.BlockSpec((SEQ, tc), lambda j: (0, j)),
        out_shape=jax.ShapeDtypeStruct((SEQ, D_FF), BF16),
        compiler_params=_params(("parallel",)),
    )(*[a for a, _, _ in ins])


def _ffn_act_bwd(up, conv_w, conv_b, dact):
    tc = _ffn_tile()
    nt = D_FF // tc
    ins = _ffn_specs(up, conv_w, conv_b) + [(dact, (SEQ, tc), lambda j: (0, j))]

    def body(xg, xv, bg, bv, wg, wv, dact_ref, dup_ref, dwg_ref, dwv_ref, dbg_ref, dbv_ref):
        taps = [w[j:j + 1, :] for w in (wg, wv) for j in range(FFN_CONV)]
        _, vjp = jax.vjp(_ffn_fn, xg[...], xv[...], bg[...], bv[...], *taps)
        grads = vjp(dact_ref[...].astype(F32))
        dup_ref[0] = grads[0].astype(BF16)
        dup_ref[1] = grads[1].astype(BF16)
        dbg_ref[...] = jnp.sum(grads[2], axis=0, keepdims=True)
        dbv_ref[...] = jnp.sum(grads[3], axis=0, keepdims=True)
        for j in range(FFN_CONV):
            dwg_ref[j:j + 1, :] = jnp.sum(grads[4 + j], axis=0, keepdims=True)
            dwv_ref[j:j + 1, :] = jnp.sum(grads[4 + FFN_CONV + j], axis=0, keepdims=True)

    col = lambda j: (0, j)
    outs = pl.pallas_call(
        body, name="ffn_act_bwd", grid=(nt,),
        in_specs=[pl.BlockSpec(b, m) for _, b, m in ins],
        out_specs=[pl.BlockSpec((2, SEQ, tc), lambda j: (0, 0, j)),
                   pl.BlockSpec((FFN_CONV, tc), col), pl.BlockSpec((FFN_CONV, tc), col),
                   pl.BlockSpec((1, tc), col), pl.BlockSpec((1, tc), col)],
        out_shape=[jax.ShapeDtypeStruct((2, SEQ, D_FF), BF16),
                   jax.ShapeDtypeStruct((FFN_CONV, D_FF), F32), jax.ShapeDtypeStruct((FFN_CONV, D_FF), F32),
                   jax.ShapeDtypeStruct((1, D_FF), F32), jax.ShapeDtypeStruct((1, D_FF), F32)],
        compiler_params=_params(("parallel",)),
    )(*[a for a, _, _ in ins])
    dup, dwg, dwv, dbg, dbv = outs
    return dup, jnp.concatenate([dwg, dwv], axis=1), jnp.concatenate([dbg, dbv], axis=1)


def _pad_lanes(v):
    return jnp.pad(v.reshape(1, -1), ((0, 0), (0, LANES - v.size)))


def _layer_fwd(x, lw):
    h = _rms_fwd(x, lw['norm_mix_g'], "rms_mix_fwd")
    p = _matmul(h, lw['w_all'], 'nn', F32, "mm_in")
    gates = _gate_fwd(p, lw['a_log'], lw['dt_bias'])
    q, k, v = _qkv_fwd(p, lw['qkv_conv_w'])
    u, w, qd, kd, qk, dec = _gdn_prep_fwd(q, k, v, gates)
    o0, o1, st0, st1 = _gdn_scan_fwd(u, w, qd, kd, qk, dec)
    ya = _gdn_post_fwd(o0, o1, p, lw['gdn_norm_g'])
    yb = _sgu_fwd(p, lw['sgu_ln_g'], lw['sgu_ln_b'], lw['sgu_w'], lw['sgu_bt'])
    ta = _matmul(ya, lw['w_branch_a'], 'nn', F32, "mm_branch_a", b_split=N_CHIPS)
    tb = _matmul(yb, lw['w_branch_b'], 'nn', F32, "mm_branch_b", b_split=N_CHIPS)
    m = _merge_fwd(p, ta, tb)
    x1 = _matmul(m, lw['w_out'], 'nn', F32, "mm_out", add=x)
    h2 = _rms_fwd(x1, lw['norm_ffn_g'], "rms_ffn_fwd")
    up = _matmul(h2, lw['w_up'], 'nn', F32, "mm_up", b_split=N_CHIPS, o_split=2)
    act = _ffn_act_fwd(up, lw['ffn_conv_w'], lw['ffn_conv_b'])
    x2 = _matmul(act, lw['w_down'], 'nn', F32, "mm_down", add=x1)
    saved = dict(x=x, h=h, p=p, gates=gates, q=q, k=k, v=v, u=u, w=w, qd=qd, kd=kd, qk=qk, dec=dec,
                 o0=o0, o1=o1, st0=st0, st1=st1, ya=ya, yb=yb, ta=ta, tb=tb, m=m, x1=x1, h2=h2, up=up, act=act)
    return x2, saved


def _layer_bwd(dx, dx_bf, lw, s, after=()):
    g = {}
    dact = _matmul(dx_bf, lw['w_down'], 'nt', BF16, "mm_down_dgrad", after=after)
    g['w_down'] = _matmul(s['act'], dx_bf, 'tn', F32, "mm_down_wgrad")
    dup, g['ffn_conv_w'], g['ffn_conv_b'] = _ffn_act_bwd(s['up'], lw['ffn_conv_w'], lw['ffn_conv_b'], dact)
    dh2 = _matmul(dup, lw['w_up'], 'nt', F32, "mm_up_dgrad", a_split=2, b_split=N_CHIPS)
    g['w_up'] = _matmul(s['h2'], dup, 'tn', F32, "mm_up_wgrad", b_split=2, o_split=N_CHIPS)
    dx1, dx1_bf, g['norm_ffn_g'] = _rms_bwd(s['x1'], lw['norm_ffn_g'], dh2, dx, "rms_ffn_bwd")
    dm = _matmul(dx1_bf, lw['w_out'], 'nt', F32, "mm_out_dgrad")
    g['w_out'] = _matmul(s['m'], dx1_bf, 'tn', F32, "mm_out_wgrad")
    d_ga, d_gb, d_ta, d_tb = _merge_bwd(s['p'], s['ta'], s['tb'], dm)
    dya = _matmul(d_ta, lw['w_branch_a'], 'nt', BF16, "mm_branch_a_dgrad", b_split=N_CHIPS)
    dyb = _matmul(d_tb, lw['w_branch_b'], 'nt', BF16, "mm_branch_b_dgrad", b_split=N_CHIPS)
    g['w_branch_a'] = _matmul(s['ya'], d_ta, 'tn', F32, "mm_branch_a_wgrad", o_split=N_CHIPS)
    g['w_branch_b'] = _matmul(s['yb'], d_tb, 'tn', F32, "mm_branch_b_wgrad", o_split=N_CHIPS)
    du_s, dv_s, g['sgu_ln_g'], g['sgu_ln_b'], g['sgu_w'], g['sgu_bt'] = _sgu_bwd(
        s['p'], lw['sgu_ln_g'], lw['sgu_ln_b'], lw['sgu_w'], lw['sgu_bt'], dyb)
    do, dz, g['gdn_norm_g'] = _gdn_post_bwd(s['o0'], s['o1'], s['p'], lw['gdn_norm_g'], dya)
    cts = _gdn_scan_bwd(s['u'], s['w'], s['qd'], s['kd'], s['qk'], s['dec'], s['st0'], s['st1'], do)
    dq, dk, dv, dgates = _gdn_prep_bwd(s['q'], s['k'], s['v'], s['gates'], cts)
    (dxq, dxk, dxv), g['qkv_conv_w'] = _qkv_bwd(s['p'], lw['qkv_conv_w'], dq, dk, dv)
    d_ab, g['a_log'], g['dt_bias'] = _gate_bwd(s['p'], lw['a_log'], lw['dt_bias'], dgates)
    dp = jnp.concatenate([dxq, dxk, dxv, dz, du_s, dv_s, d_ga, d_gb, d_ab], axis=1)
    dh = _matmul(dp, lw['w_all'], 'nt', F32, "mm_in_dgrad")
    g['w_all'] = _matmul(s['h'], dp, 'tn', F32, "mm_in_wgrad")
    dx0, dx0_bf, g['norm_mix_g'] = _rms_bwd(s['x'], lw['norm_mix_g'], dh, dx1, "rms_mix_bwd")
    return dx0, dx0_bf, g


def _w_all_from_w_in(w_in):
    n_ab = 4 * GDN_HEADS
    cut = 4 * _gw()
    pad = jnp.zeros((w_in.shape[0], AB_PAD - n_ab), w_in.dtype)
    return jnp.concatenate([w_in[:, :cut], w_in[:, cut + n_ab:], w_in[:, cut:cut + n_ab], pad], axis=1)


def _w_in_grad_from_all(g_all):
    n_ab = 4 * GDN_HEADS
    cut = 4 * _gw()
    n_main = cut + 2 * _sw() + 2 * D_MODEL
    return jnp.concatenate([g_all[:, :cut], g_all[:, n_main:n_main + n_ab], g_all[:, cut:n_main]], axis=1)


def _layer_weights(l, big, small):
    cat_cols = lambda t: jnp.concatenate([t[j] for j in range(N_CHIPS)], axis=-1)
    return dict(
        w_all=_w_all_from_w_in(cat_cols(big['w_in'])),
        w_branch_a=big['w_branch_a'], w_branch_b=big['w_branch_b'], w_up=big['w_up'],
        w_out=big['w_out'].reshape(D_MODEL, D_MODEL), w_down=big['w_down'].reshape(D_FF, D_MODEL),
        qkv_conv_w=cat_cols(big['qkv_conv_w']), ffn_conv_w=cat_cols(big['ffn_conv_w']),
        norm_mix_g=small['norm_mix_g'][l:l + 1], norm_ffn_g=small['norm_ffn_g'][l:l + 1],
        a_log=_pad_lanes(small['a_log'][l]), dt_bias=_pad_lanes(small['dt_bias'][l]),
        gdn_norm_g=small['gdn_norm_g'][l:l + 1],
        sgu_ln_g=small['sgu_ln_g'][l:l + 1], sgu_ln_b=small['sgu_ln_b'][l:l + 1], sgu_w=small['sgu_w'][l],
        sgu_bt=jnp.pad(small['sgu_b'][l].T, ((0, 0), (0, LANES - SGU_GROUPS))),
        ffn_conv_b=small['ffn_conv_b'][l:l + 1])


_SMALL_GRADS = ('norm_mix_g', 'a_log', 'dt_bias', 'gdn_norm_g', 'sgu_ln_g', 'sgu_ln_b', 'sgu_w', 'sgu_bt',
                'norm_ffn_g', 'ffn_conv_b', 'qkv_conv_w', 'ffn_conv_w')
_BIG = ('w_in', 'w_branch_a', 'w_branch_b', 'w_out', 'w_up', 'w_down')


def _big_grad_slices(g):
    w_in = _w_in_grad_from_all(g['w_all'])
    w_in = w_in.reshape(D_MODEL, N_CHIPS, -1).transpose(1, 0, 2)
    return dict(w_in=w_in, w_branch_a=g['w_branch_a'], w_branch_b=g['w_branch_b'], w_up=g['w_up'],
                w_out=g['w_out'].reshape(N_CHIPS, D_MODEL // N_CHIPS, D_MODEL),
                w_down=g['w_down'].reshape(N_CHIPS, D_FF // N_CHIPS, D_MODEL))


def _adamw(w, g, m, v, name):
    shape = w.shape
    cols = shape[-1]
    rows = w.size // cols
    tr = _pick(rows, max(8, (1 << 18) // cols // 8 * 8), 8) if rows % 8 == 0 else rows

    def fn(wv, gv, mv, vv):
        m2 = ADAM_B1 * mv + (1.0 - ADAM_B1) * gv
        v2 = ADAM_B2 * vv + (1.0 - ADAM_B2) * (gv * gv)
        m_hat = m2 / (1.0 - ADAM_B1 ** ADAM_STEP)
        v_hat = v2 / (1.0 - ADAM_B2 ** ADAM_STEP)
        delta = -ADAM_LR * (m_hat / (jnp.sqrt(v_hat) + ADAM_EPS) + ADAM_WD * wv)
        return delta, m2, v2

    row = lambda i: (i, 0)
    outs = _tiled(fn, name, (rows // tr,),
                  [(t.reshape(rows, cols), (tr, cols), row) for t in (w, g, m, v)],
                  [((rows, cols), F32, (tr, cols), row, ())] * 3, sem=("parallel",))
    return [o.reshape(shape) for o in outs]


MESH_IDS = pl.DeviceIdType.MESH
ANY = pl.BlockSpec(memory_space=pl.ANY)


def _place():
    x, y, c = lax.axis_index("x"), lax.axis_index("y"), lax.axis_index("c")
    chips = [(1 - x, y), (x, 1 - y), (1 - x, 1 - y)]
    return x, y, c, 2 * x + y, chips


def _chip_index():
    return 2 * lax.axis_index("x") + lax.axis_index("y")


HBM = pl.BlockSpec(memory_space=pltpu.HBM)
SEM = pl.BlockSpec(memory_space=pltpu.SEMAPHORE)
DATAFLOW = pltpu.SideEffectType.DATAFLOW_SIDE_EFFECTING
TOKEN = jax.ShapeDtypeStruct((8, LANES), F32)


def _in_hbm(t):
    return pltpu.with_memory_space_constraint(t, pltpu.HBM)


def _place_shard(w, l, dtype, name):
    _, r, cols = w.shape
    tr = _pick(r, max(16, (1 << 18) // cols // 16 * 16), 16) if r % 16 == 0 else r

    def body(w_ref, o_ref):
        o_ref[...] = w_ref[...].astype(dtype)

    return pl.pallas_call(
        body, name=name, grid=(r // tr,),
        in_specs=[pl.BlockSpec((None, tr, cols), lambda i: (l, i, 0))],
        out_specs=pl.BlockSpec((None, tr, cols), lambda i: (_chip_index(), i, 0)),
        out_shape=jax.ShapeDtypeStruct((N_CHIPS, r, cols), dtype),
        compiler_params=_params(("parallel",)),
    )(w)


def _my_rows(ref_or_shape_rows, c):
    r = ref_or_shape_rows
    if r % 32 == 0:
        return pl.ds(c * (r // 2), r // 2), pl.ds((1 - c) * (r // 2), r // 2), True
    return pl.ds(0, r), pl.ds(0, r), False


def _gather_copies(bufs, send_sems, recv_sems):
    x, y, c, me, chips = _place()
    out = []
    for i, buf in enumerate(bufs):
        mine, _, _ = _my_rows(buf.shape[1], c)
        for j, (cx, cy) in enumerate(chips):
            def rcopy(slab, i=i, j=j, cx=cx, cy=cy):
                return pltpu.make_async_remote_copy(src_ref=slab, dst_ref=slab, send_sem=send_sems.at[3 * i + j],
                                                    recv_sem=recv_sems.at[3 * i + j], device_id=(cx, cy, c),
                                                    device_id_type=MESH_IDS)
            out.append((rcopy(buf.at[me, mine]), rcopy(buf.at[2 * cx + cy, mine])))
    return out


def _gather_start(xs, name):
    n = len(xs)

    def body(*refs):
        send_sems, recv_sems = refs[n], refs[n + 1]
        bufs = refs[n + 2:2 * n + 2]
        token = refs[2 * n + 2]
        for out_going, _ in _gather_copies(bufs, send_sems, recv_sems):
            out_going.start()
        token[...] = jnp.zeros_like(token)

    res = pl.pallas_call(
        body, name=name, in_specs=[HBM] * n,
        out_specs=[SEM, SEM] + [HBM] * n + [pl.BlockSpec(memory_space=pltpu.VMEM)],
        out_shape=[pltpu.SemaphoreType.DMA((3 * n,)), pltpu.SemaphoreType.DMA((3 * n,))]
        + [pltpu.HBM(t.shape, t.dtype) for t in xs] + [TOKEN],
        input_output_aliases={i: i + 2 for i in range(n)},
        compiler_params=pltpu.CompilerParams(has_side_effects=DATAFLOW),
    )(*[_in_hbm(t) for t in xs])
    return res[0], res[1], res[2:2 + n], res[2 + n]


def _gather_wait(bufs, send_sems, recv_sems, after, name):
    n = len(bufs)

    def body(*refs):
        b_refs = refs[:n]
        s_sems, r_sems = refs[n], refs[n + 1]
        for out_going, in_coming in _gather_copies(b_refs, s_sems, r_sems):
            out_going.wait_send()
            in_coming.wait_recv()

    return pl.pallas_call(
        body, name=name, in_specs=[HBM] * n + [SEM, SEM, ANY], out_specs=[HBM] * n,
        out_shape=[pltpu.HBM(t.shape, t.dtype) for t in bufs],
        input_output_aliases={i: i for i in range(n)},
        compiler_params=pltpu.CompilerParams(has_side_effects=DATAFLOW),
    )(*bufs, send_sems, recv_sems, after)


def _gather_forward(bufs):
    idx = [i for i, t in enumerate(bufs) if t.shape[1] % 32 == 0]
    xs = [bufs[i] for i in idx]
    n = len(xs)

    def body(*refs):
        o_refs = refs[n:2 * n]
        send_sems, recv_sems = refs[2 * n:]
        x, y, c, _, chips = _place()
        copies = []
        for i in range(n):
            mine, theirs, _ = _my_rows(xs[i].shape[1], c)
            for j, (cx, cy) in enumerate(chips):
                def rcopy(slab, i=i, j=j):
                    return pltpu.make_async_remote_copy(src_ref=slab, dst_ref=slab, send_sem=send_sems.at[3 * i + j],
                                                        recv_sem=recv_sems.at[3 * i + j], device_id=(x, y, 1 - c),
                                                        device_id_type=MESH_IDS)
                copies.append((rcopy(o_refs[i].at[2 * cx + cy, mine]), rcopy(o_refs[i].at[2 * cx + cy, theirs])))
                copies[-1][0].start()
        for out_going, in_coming in copies:
            out_going.wait_send()
            in_coming.wait_recv()

    res = pl.pallas_call(
        body, name="gather_forward", in_specs=[ANY] * n, out_specs=[ANY] * n,
        out_shape=[jax.ShapeDtypeStruct(t.shape, t.dtype) for t in xs],
        input_output_aliases={i: i for i in range(n)},
        scratch_shapes=[pltpu.SemaphoreType.DMA((3 * n,)), pltpu.SemaphoreType.DMA((3 * n,))],
        compiler_params=pltpu.CompilerParams(has_side_effects=True),
    )(*xs)
    out = list(bufs)
    for i, t in zip(idx, res):
        out[i] = t
    return out


def _swap_halves(gs):
    n = len(gs)

    def body(*refs):
        g_refs, b_refs = refs[:n], refs[n:2 * n]
        send_sems, recv_sems = refs[2 * n:]
        x, y, c, _, _ = _place()
        copies = []
        for i in range(n):
            rh = gs[i].shape[1] // 2
            src = g_refs[i].at[:, pl.ds((1 - c) * rh, rh), :]
            copies.append(pltpu.make_async_remote_copy(src_ref=src, dst_ref=b_refs[i], send_sem=send_sems.at[i],
                                                       recv_sem=recv_sems.at[i], device_id=(x, y, 1 - c),
                                                       device_id_type=MESH_IDS))
            copies[-1].start()
        for cp in copies:
            cp.wait()

    return pl.pallas_call(
        body, name="grad_swap_halves", in_specs=[ANY] * n, out_specs=[ANY] * n,
        out_shape=[jax.ShapeDtypeStruct((t.shape[0], t.shape[1] // 2, t.shape[2]), t.dtype) for t in gs],
        scratch_shapes=[pltpu.SemaphoreType.DMA((n,)), pltpu.SemaphoreType.DMA((n,))],
        compiler_params=pltpu.CompilerParams(has_side_effects=True),
    )(*gs)


def _half_tile(rh, cols):
    return _pick(rh, max(16, (1 << 18) // cols // 16 * 16), 16)


def _chip_partial(g, b, name):
    nchip, r, cols = g.shape
    rh = r // 2
    tr = _half_tile(rh, cols)
    nt = rh // tr

    def body(g_ref, b_ref, o_ref):
        o_ref[...] = (g_ref[...] + b_ref[...]).astype(BF16)

    return pl.pallas_call(
        body, name=name, grid=(nchip, nt),
        in_specs=[pl.BlockSpec((None, tr, cols), lambda j, i: (j, lax.axis_index("c") * nt + i, 0)),
                  pl.BlockSpec((None, tr, cols), lambda j, i: (j, i, 0))],
        out_specs=pl.BlockSpec((None, tr, cols), lambda j, i: (j, i, 0)),
        out_shape=jax.ShapeDtypeStruct((nchip, rh, cols), BF16),
        compiler_params=_params(("parallel", "parallel")),
    )(g, b)


def _scatter_copies(p_refs, r_refs, send_sems, recv_sems):
    _, _, c, _, chips = _place()
    copies = []
    for j, (cx, cy) in enumerate(chips):
        for i in range(len(p_refs)):
            copies.append(pltpu.make_async_remote_copy(
                src_ref=p_refs[i].at[2 * cx + cy], dst_ref=r_refs[i].at[j], send_sem=send_sems.at[3 * i + j],
                recv_sem=recv_sems.at[3 * i + j], device_id=(cx, cy, c), device_id_type=MESH_IDS))
    return copies


def _scatter_start(ps, name):
    n = len(ps)
    lands = [lax.empty((3,) + t.shape[1:], t.dtype) for t in ps]

    def body(*refs):
        send_sems, recv_sems = refs[2 * n], refs[2 * n + 1]
        p_refs = refs[2 * n + 2:3 * n + 2]
        r_refs = refs[3 * n + 2:4 * n + 2]
        token = refs[4 * n + 2]
        for cp in _scatter_copies(p_refs, r_refs, send_sems, recv_sems):
            cp.start()
        token[...] = jnp.zeros_like(token)

    res = pl.pallas_call(
        body, name=name, in_specs=[HBM] * (2 * n),
        out_specs=[SEM, SEM] + [HBM] * (2 * n) + [pl.BlockSpec(memory_space=pltpu.VMEM)],
        out_shape=[pltpu.SemaphoreType.DMA((3 * n,)), pltpu.SemaphoreType.DMA((3 * n,))]
        + [pltpu.HBM(t.shape, t.dtype) for t in list(ps) + lands] + [TOKEN],
        input_output_aliases={i: i + 2 for i in range(2 * n)},
        compiler_params=pltpu.CompilerParams(has_side_effects=DATAFLOW),
    )(*[_in_hbm(t) for t in list(ps) + lands])
    return res[0], res[1], res[2:2 + n], res[2 + n:2 + 2 * n], res[2 + 2 * n]


def _scatter_wait(ps, lands, send_sems, recv_sems, after, name):
    n = len(ps)

    def body(*refs):
        p_refs, r_refs = refs[:n], refs[n:2 * n]
        s_sems, r_sems = refs[2 * n], refs[2 * n + 1]
        for cp in _scatter_copies(p_refs, r_refs, s_sems, r_sems):
            cp.wait_send()
            cp.wait_recv()

    res = pl.pallas_call(
        body, name=name, in_specs=[HBM] * (2 * n) + [SEM, SEM, ANY], out_specs=[HBM] * (2 * n),
        out_shape=[pltpu.HBM(t.shape, t.dtype) for t in list(ps) + list(lands)],
        input_output_aliases={i: i for i in range(2 * n)},
        compiler_params=pltpu.CompilerParams(has_side_effects=DATAFLOW),
    )(*ps, *lands, send_sems, recv_sems, after)
    return res[n:]


def _reduce_own(g, b, rcv, acc, l, name):
    nchip, r, cols = g.shape
    rh = r // 2
    tr = _half_tile(rh, cols)
    nt = rh // tr

    def body(*refs):
        g_ref, b_ref, r_ref = refs[:3]
        o_ref = refs[-1]
        total = g_ref[...] + b_ref[...]
        for j in range(3):
            total = total + r_ref[j].astype(F32)
        o_ref[...] = total

    mine = lambda i: lax.axis_index("c") * nt + i
    in_specs = [pl.BlockSpec((None, tr, cols), lambda i: (_chip_index(), mine(i), 0)),
                pl.BlockSpec((None, tr, cols), lambda i: (_chip_index(), i, 0)),
                pl.BlockSpec((3, tr, cols), lambda i: (0, i, 0))]
    args = [g, b, rcv]
    aliases = {}
    if acc is not None:
        in_specs.append(ANY)
        args.append(acc)
        aliases = {3: 0}
    return pl.pallas_call(
        body, name=name, grid=(nt,), in_specs=in_specs,
        out_specs=pl.BlockSpec((None, tr, cols), lambda i: (l, mine(i), 0)),
        out_shape=jax.ShapeDtypeStruct((DEPTH, r, cols), F32), input_output_aliases=aliases,
        compiler_params=_params(("parallel",)),
    )(*args)


def _share_halves(fs, l):
    n = len(fs)

    def body(*refs):
        o_refs = refs[n:2 * n]
        send_sems, recv_sems = refs[2 * n:]
        x, y, c, _, _ = _place()

        def halves(i):
            rh = fs[i].shape[1] // 2
            return o_refs[i].at[l, pl.ds(c * rh, rh), :], o_refs[i].at[l, pl.ds((1 - c) * rh, rh), :]

        def copy(i, rows):
            return pltpu.make_async_remote_copy(src_ref=rows, dst_ref=rows, send_sem=send_sems.at[i],
                                                recv_sem=recv_sems.at[i], device_id=(x, y, 1 - c),
                                                device_id_type=MESH_IDS)

        for i in range(n):
            copy(i, halves(i)[0]).start()
        for i in range(n):
            mine, theirs = halves(i)
            copy(i, mine).wait_send()
            copy(i, theirs).wait_recv()

    return pl.pallas_call(
        body, name="grad_share_halves", in_specs=[ANY] * n, out_specs=[ANY] * n,
        out_shape=[jax.ShapeDtypeStruct(t.shape, t.dtype) for t in fs],
        input_output_aliases={i: i for i in range(n)},
        scratch_shapes=[pltpu.SemaphoreType.DMA((n,)), pltpu.SemaphoreType.DMA((n,))],
        compiler_params=pltpu.CompilerParams(has_side_effects=True),
    )(*fs)


def _reduce_scatter_begin(slices, l):
    names = list(slices)
    gs = [slices[k] for k in names]
    bs = _swap_halves(gs)
    ps = [_chip_partial(g, b, "grad_chip_partial_" + k) for k, g, b in zip(names, gs, bs)]
    send_sems, recv_sems, ps, lands, token = _scatter_start(ps, "grad_scatter_start_%d" % l)
    return dict(names=names, gs=gs, bs=bs, ps=ps, lands=lands, sems=(send_sems, recv_sems), token=token, l=l)


def _reduce_scatter_end(st, stacked, after):
    l = st['l']
    rs = _scatter_wait(st['ps'], st['lands'], *st['sems'], after, "grad_scatter_wait_%d" % l)
    fs = [_reduce_own(g, b, r, None if stacked is None else stacked[k], l, "grad_reduce_own_" + k)
          for k, g, b, r in zip(st['names'], st['gs'], st['bs'], rs)]
    return dict(zip(st['names'], _share_halves(fs, l)))


def _all_reduce_small(packed):
    rows = packed.shape[0]
    n_dev = 2 * N_CHIPS
    tr = _pick(rows, 512, 8)

    def body(x_ref, o_ref, buf, send_sems, recv_sems):
        x, y, c, _, _ = _place()
        me = 4 * x + 2 * y + c
        copies = []
        for px in range(2):
            for py in range(2):
                for pc in range(2):
                    peer = 4 * px + 2 * py + pc
                    cp = pltpu.make_async_remote_copy(src_ref=x_ref, dst_ref=buf.at[me], send_sem=send_sems.at[peer],
                                                      recv_sem=recv_sems.at[me], device_id=(px, py, pc),
                                                      device_id_type=MESH_IDS)

                    @pl.when(peer != me)
                    def _(cp=cp):
                        cp.start()
                    copies.append((peer, cp))
        buf[me] = x_ref[...]
        for peer, cp in copies:
            @pl.when(peer != me)
            def _(cp=cp, peer=peer):
                cp.wait_send()
                pltpu.make_async_remote_copy(src_ref=x_ref, dst_ref=buf.at[peer], send_sem=send_sems.at[peer],
                                             recv_sem=recv_sems.at[peer], device_id=(x, y, c),
                                             device_id_type=MESH_IDS).wait_recv()

        @pl.loop(0, rows // tr)
        def _(t):
            sl = pl.ds(pl.multiple_of(t * tr, 8), tr)
            acc = buf[0, sl, :]
            for s in range(1, n_dev):
                acc = acc + buf[s, sl, :]
            o_ref[sl, :] = acc

    vm = pl.BlockSpec(memory_space=pltpu.VMEM)
    return pl.pallas_call(
        body, name="all_reduce_small", in_specs=[vm], out_specs=vm,
        out_shape=jax.ShapeDtypeStruct(packed.shape, F32),
        scratch_shapes=[pltpu.VMEM((n_dev, rows, LANES), F32), pltpu.SemaphoreType.DMA((n_dev,)),
                        pltpu.SemaphoreType.DMA((n_dev,))],
        compiler_params=pltpu.CompilerParams(vmem_limit_bytes=VMEM_LIMIT, has_side_effects=True),
    )(packed)


_WEIGHTS = ('norm_mix_g', 'w_in', 'qkv_conv_w', 'a_log', 'dt_bias', 'gdn_norm_g', 'w_branch_a', 'sgu_ln_g',
            'sgu_ln_b', 'sgu_w', 'sgu_b', 'w_branch_b', 'w_out', 'norm_ffn_g', 'w_up', 'ffn_conv_w', 'ffn_conv_b',
            'w_down', 'final_norm_g')


def _local_step(x, target, weights_of, final_norm_g, bwd_after=None, after_layer=None):
    lws, saves = [], []
    for l in range(DEPTH):
        lw = weights_of(l, x)
        x, s = _layer_fwd(x, lw)
        lws.append(lw)
        saves.append(s)
    loss, dx, dx_bf, d_final = _loss_head(x, final_norm_g.reshape(1, -1), target)
    grads = [None] * DEPTH
    for l in reversed(range(DEPTH)):
        after = bwd_after(l) if bwd_after is not None else ()
        dx, dx_bf, grads[l] = _layer_bwd(dx, dx_bf, lws[l], saves[l], after)
        if after_layer is not None:
            after_layer(l, grads[l], dx)
    return loss, dx, grads, d_final


def _pack_small(grads, d_final):
    parts = [grads[l][k].reshape(-1) for l in range(DEPTH) for k in _SMALL_GRADS] + [d_final.reshape(-1)]
    flat = jnp.concatenate(parts)
    rows = -(-flat.size // (8 * LANES)) * 8
    return jnp.pad(flat, (0, rows * LANES - flat.size)).reshape(rows, LANES), [p.size for p in parts]


def _unpack_small(packed, grads, d_final):
    flat = packed.reshape(-1)
    out, off = [], 0
    for l in range(DEPTH):
        d = {}
        for k in _SMALL_GRADS:
            t = grads[l][k]
            d[k] = flat[off:off + t.size].reshape(t.shape)
            off += t.size
        out.append(d)
    return out, flat[off:off + d_final.size].reshape(d_final.shape)


def kernel(x, norm_mix_g, w_in, qkv_conv_w, a_log, dt_bias, gdn_norm_g, w_branch_a, sgu_ln_g, sgu_ln_b, sgu_w, sgu_b, w_branch_b, w_out, norm_ffn_g, w_up, ffn_conv_w, ffn_conv_b, w_down, final_norm_g, loss_target, m_norm_mix_g, m_w_in, m_qkv_conv_w, m_a_log, m_dt_bias, m_gdn_norm_g, m_w_branch_a, m_sgu_ln_g, m_sgu_ln_b, m_sgu_w, m_sgu_b, m_w_branch_b, m_w_out, m_norm_ffn_g, m_w_up, m_ffn_conv_w, m_ffn_conv_b, m_w_down, m_final_norm_g, v_norm_mix_g, v_w_in, v_qkv_conv_w, v_a_log, v_dt_bias, v_gdn_norm_g, v_w_branch_a, v_sgu_ln_g, v_sgu_ln_b, v_sgu_w, v_sgu_b, v_w_branch_b, v_w_out, v_norm_ffn_g, v_w_up, v_ffn_conv_w, v_ffn_conv_b, v_w_down, v_final_norm_g):
    w = dict(norm_mix_g=norm_mix_g, w_in=w_in, qkv_conv_w=qkv_conv_w, a_log=a_log, dt_bias=dt_bias,
             gdn_norm_g=gdn_norm_g, w_branch_a=w_branch_a, sgu_ln_g=sgu_ln_g, sgu_ln_b=sgu_ln_b, sgu_w=sgu_w,
             sgu_b=sgu_b, w_branch_b=w_branch_b, w_out=w_out, norm_ffn_g=norm_ffn_g, w_up=w_up,
             ffn_conv_w=ffn_conv_w, ffn_conv_b=ffn_conv_b, w_down=w_down, final_norm_g=final_norm_g)
    m = dict(norm_mix_g=m_norm_mix_g, w_in=m_w_in, qkv_conv_w=m_qkv_conv_w, a_log=m_a_log, dt_bias=m_dt_bias,
             gdn_norm_g=m_gdn_norm_g, w_branch_a=m_w_branch_a, sgu_ln_g=m_sgu_ln_g, sgu_ln_b=m_sgu_ln_b,
             sgu_w=m_sgu_w, sgu_b=m_sgu_b, w_branch_b=m_w_branch_b, w_out=m_w_out, norm_ffn_g=m_norm_ffn_g,
             w_up=m_w_up, ffn_conv_w=m_ffn_conv_w, ffn_conv_b=m_ffn_conv_b, w_down=m_w_down,
             final_norm_g=m_final_norm_g)
    v = dict(norm_mix_g=v_norm_mix_g, w_in=v_w_in, qkv_conv_w=v_qkv_conv_w, a_log=v_a_log, dt_bias=v_dt_bias,
             gdn_norm_g=v_gdn_norm_g, w_branch_a=v_w_branch_a, sgu_ln_g=v_sgu_ln_g, sgu_ln_b=v_sgu_ln_b,
             sgu_w=v_sgu_w, sgu_b=v_sgu_b, w_branch_b=v_w_branch_b, w_out=v_w_out, norm_ffn_g=v_norm_ffn_g,
             w_up=v_w_up, ffn_conv_w=v_ffn_conv_w, ffn_conv_b=v_ffn_conv_b, w_down=v_w_down,
             final_norm_g=v_final_norm_g)
    chip = _chip_index()

    gathered_names = _BIG + ('qkv_conv_w', 'ffn_conv_w')
    in_flight = []
    for l in range(DEPTH):
        placed = [_place_shard(w[k], l, BF16 if k in _BIG else F32, "place_" + k) for k in gathered_names]
        in_flight.append(_gather_start(placed, "gather_start_%d" % l))

    def weights_of(l, x_in):
        send_sems, recv_sems, bufs, _ = in_flight[l]
        after = in_flight[-1][3] if l == 0 else x_in
        bufs = _gather_forward(_gather_wait(bufs, send_sems, recv_sems, after, "gather_wait_%d" % l))
        return _layer_weights(l, dict(zip(gathered_names, bufs)), w)

    stacked = [None]
    pending = [None]

    def bwd_after(l):
        return () if pending[0] is None else (pending[0]['token'],)

    def after_layer(l, g, dx):
        if pending[0] is not None:
            stacked[0] = _reduce_scatter_end(pending[0], stacked[0], dx)
        pending[0] = _reduce_scatter_begin(_big_grad_slices(g), l)

    loss, grad_x, grads, d_final = _local_step(x[0], loss_target[0], weights_of, w['final_norm_g'], bwd_after,
                                               after_layer)
    loss = lax.psum(loss[0, 0], ("x", "y", "c"))
    packed, _ = _pack_small(grads, d_final)
    reduced_small = _all_reduce_small(packed)
    small_g, d_final = _unpack_small(reduced_small, grads, d_final)
    big_g = _reduce_scatter_end(pending[0], stacked[0], reduced_small)

    def stack(k):
        return jnp.stack([small_g[l][k] for l in range(DEPTH)])

    nd = 2 * GDN_HEADS
    g_out = {k: big_g[k].reshape(w[k].shape) for k in _BIG}
    for k in ('norm_mix_g', 'gdn_norm_g', 'sgu_ln_g', 'sgu_ln_b', 'norm_ffn_g', 'ffn_conv_b'):
        g_out[k] = stack(k).reshape(w[k].shape)
    g_out['sgu_w'] = stack('sgu_w')
    g_out['a_log'] = stack('a_log')[:, 0, :nd].reshape(w['a_log'].shape)
    g_out['dt_bias'] = stack('dt_bias')[:, 0, :nd].reshape(w['dt_bias'].shape)
    g_out['sgu_b'] = jnp.swapaxes(stack('sgu_bt')[:, :, :SGU_GROUPS], 1, 2)
    for k in ('qkv_conv_w', 'ffn_conv_w'):
        full = stack(k)
        width = w[k].shape[-1]
        g_out[k] = lax.dynamic_slice_in_dim(full, chip * width, width, axis=2)
    g_out['final_norm_g'] = d_final.reshape(w['final_norm_g'].shape)

    deltas, new_m, new_v = {}, {}, {}
    for k in _WEIGHTS:
        deltas[k], new_m[k], new_v[k] = _adamw(w[k], g_out[k], m[k], v[k], "adamw_" + k)
    return (loss, grad_x[None], *[g_out[k] for k in _WEIGHTS], *[deltas[k] for k in _WEIGHTS],
            *[new_m[k] for k in _WEIGHTS], *[new_v[k] for k in _WEIGHTS])
```

```python
import functools
import math

import jax
import jax.numpy as jnp
from jax import lax
from jax.experimental import pallas as pl
from jax.experimental.pallas import tpu as pltpu

F32 = jnp.float32
BF16 = jnp.bfloat16

D_MODEL = 2048
SEQ = 2048
DEPTH = 4
GDN_HEADS = 8
HEAD_DIM = 128
GDN_CHUNK = 64
QKV_CONV = 5
SGU_GROUPS = 8
SGU_DIM = 128
SGU_BLOCK = 128
D_FF = 5632
FFN_CONV = 3
NORM_EPS = 1e-6
N_CHIPS = 4

ADAM_LR = 0.001
ADAM_B1 = 0.9
ADAM_B2 = 0.999
ADAM_EPS = 1e-08
ADAM_WD = 0.01
ADAM_STEP = 10

LANES = 128
VMEM_LIMIT = 56 * 1024 * 1024
AB_PAD = LANES


def _gw():
    return GDN_HEADS * HEAD_DIM


def _sw():
    return SGU_GROUPS * SGU_DIM


def _n_all():
    return 4 * _gw() + 2 * _sw() + 2 * D_MODEL + AB_PAD


def _pick(n, target, align=LANES):
    if n <= target:
        return n
    best = None
    t = align
    while t <= target:
        if n % t == 0:
            best = t
        t += align
    assert best is not None, (n, target, align)
    return best


def _params(sem=None):
    return pltpu.CompilerParams(dimension_semantics=sem, vmem_limit_bytes=VMEM_LIMIT)


def _dot(a, b, dims, hi):
    dn = (dims, ((), ()))
    if not hi:
        return lax.dot_general(a.astype(BF16), b.astype(BF16), dn, preferred_element_type=F32)
    a_hi, b_hi = a.astype(BF16), b.astype(BF16)
    a_lo = (a - a_hi.astype(F32)).astype(BF16)
    b_lo = (b - b_hi.astype(F32)).astype(BF16)
    d = lambda p, q: lax.dot_general(p, q, dn, preferred_element_type=F32)
    return d(a_hi, b_hi) + (d(a_hi, b_lo) + d(a_lo, b_hi))


def _make_mm(hi):
    @jax.custom_vjp
    def mm(a, b):
        return _dot(a, b, ((1,), (0,)), hi)

    @jax.custom_vjp
    def mm_nt(a, b):
        return _dot(a, b, ((1,), (1,)), hi)

    @jax.custom_vjp
    def mm_tn(a, b):
        return _dot(a, b, ((0,), (0,)), hi)

    mm.defvjp(lambda a, b: (mm(a, b), (a, b)), lambda r, g: (mm_nt(g, r[1]), mm_tn(r[0], g)))
    mm_nt.defvjp(lambda a, b: (mm_nt(a, b), (a, b)), lambda r, g: (mm(g, r[1]), mm_tn(g, r[0])))
    mm_tn.defvjp(lambda a, b: (mm_tn(a, b), (a, b)), lambda r, g: (mm_nt(r[1], g), mm(r[0], g)))
    return mm, mm_nt, mm_tn


_mm, _mm_nt, _mm_tn = _make_mm(False)
_mmh, _mmh_nt, _mmh_tn = _make_mm(True)


def _shift_rows_raw(x, s):
    if s == 0:
        return x
    n = x.shape[0]
    rolled = pltpu.roll(x, (-s) % n, 0)
    t = lax.broadcasted_iota(jnp.int32, x.shape, 0)
    ok = (t + s >= 0) & (t + s < n)
    return jnp.where(ok, rolled, 0.0)


@functools.partial(jax.custom_vjp, nondiff_argnums=(1,))
def _shift_rows(x, s):
    return _shift_rows_raw(x, s)


_shift_rows.defvjp(lambda x, s: (_shift_rows_raw(x, s), None), lambda s, _, g: (_shift_rows_raw(g, -s),))


def _sigmoid(x):
    return 1.0 / (1.0 + jnp.exp(-x))


def _silu(x):
    return x * _sigmoid(x)


def _gelu(x):
    return 0.5 * x * (1.0 + jnp.tanh(math.sqrt(2.0 / math.pi) * (x + 0.044715 * x * x * x)))


def _softplus(x):
    return jnp.maximum(x, 0.0) + jnp.log(1.0 + jnp.exp(-jnp.abs(x)))


def _rms(x, g):
    return x * lax.rsqrt(jnp.mean(x * x, axis=-1, keepdims=True) + NORM_EPS) * g


def _conv_rows(x, taps):
    pad = len(taps) // 2
    acc = None
    for j, w in enumerate(taps):
        term = _shift_rows(x, j - pad) * w
        acc = term if acc is None else acc + term
    return acc


@jax.custom_vjp
def _inv_unit(mats):
    n = mats[0].shape[0]
    eye = (lax.broadcasted_iota(jnp.int32, (n, n), 0) == lax.broadcasted_iota(jnp.int32, (n, n), 1)).astype(F32)
    ps = [eye - a for a in mats]
    aks = list(mats)
    for _ in range(int(math.log2(n)) - 1):
        aks = [_mmh(ak, ak) for ak in aks]
        ps = [p + _mmh(p, ak) for p, ak in zip(ps, aks)]
    return ps


def _inv_unit_fwd(mats):
    ts = _inv_unit(mats)
    return ts, ts


def _inv_unit_bwd(ts, gs):
    inner = [_mmh_nt(g, t) for g, t in zip(gs, ts)]
    return ([-_mmh_tn(t, m) for t, m in zip(ts, inner)],)


_inv_unit.defvjp(_inv_unit_fwd, _inv_unit_bwd)


def _tiled(fn, name, grid, ins, outs, sem=None):
    n_in = len(ins)

    def body(*refs):
        vals = [r[...] for r in refs[:n_in]]
        res = fn(*vals)
        if not isinstance(res, (tuple, list)):
            res = (res,)
        for o_ref, r, spec in zip(refs[n_in:], res, outs):
            acc = spec[4]
            if not acc:
                o_ref[...] = r.astype(o_ref.dtype)
            else:
                first = functools.reduce(jnp.logical_and, [pl.program_id(a) == 0 for a in acc])

                @pl.when(first)
                def _(o_ref=o_ref, r=r):
                    o_ref[...] = r.astype(o_ref.dtype)

                @pl.when(jnp.logical_not(first))
                def _(o_ref=o_ref, r=r):
                    o_ref[...] += r.astype(o_ref.dtype)

    return pl.pallas_call(
        body, name=name, grid=grid,
        in_specs=[pl.BlockSpec(b, m) for _, b, m in ins],
        out_specs=[pl.BlockSpec(s[2], s[3]) for s in outs],
        out_shape=[jax.ShapeDtypeStruct(s[0], s[1]) for s in outs],
        compiler_params=_params(sem),
    )(*[a for a, _, _ in ins])


def _matmul(a, b, mode, out_dtype, name, add=None, a_split=1, b_split=1, o_split=1, after=()):
    def dims2(x, split):
        return (x.shape[-2], x.shape[-1] * split)

    ar, ac = dims2(a, a_split)
    br, bc = dims2(b, b_split)
    if mode == 'nn':
        M, K, N = ar, ac, bc
        assert br == K
    elif mode == 'nt':
        M, K, N = ar, ac, br
        assert bc == K
    else:
        K, M, N = ar, ac, bc
        assert br == K
    tm = _pick(M // (a_split if mode == 'tn' else 1), 512)
    tn = _pick(N // max(o_split, b_split if mode != 'nt' else 1), 1408)
    tk = _pick(K // max(a_split if mode != 'tn' else 1, b_split if mode == 'nt' else 1), 2048)
    gm, gn, gk = M // tm, N // tn, K // tk

    def col_map(split, total_cols, tile):
        per = total_cols // split // tile

        def f(r, c):
            return (c // per, r, c % per) if split > 1 else (r, c)
        return f

    if mode == 'nn':
        a_idx = col_map(a_split, K, tk)
        b_idx = col_map(b_split, N, tn)
        a_spec = pl.BlockSpec(((None,) if a_split > 1 else ()) + (tm, tk), lambda m, n, k: a_idx(m, k))
        b_spec = pl.BlockSpec(((None,) if b_split > 1 else ()) + (tk, tn), lambda m, n, k: b_idx(k, n))
        dn = ((1,), (0,))
    elif mode == 'nt':
        a_idx = col_map(a_split, K, tk)
        b_idx = col_map(b_split, K, tk)
        a_spec = pl.BlockSpec(((None,) if a_split > 1 else ()) + (tm, tk), lambda m, n, k: a_idx(m, k))
        b_spec = pl.BlockSpec(((None,) if b_split > 1 else ()) + (tn, tk), lambda m, n, k: b_idx(n, k))
        dn = ((1,), (1,))
    else:
        a_idx = col_map(a_split, M, tm)
        b_idx = col_map(b_split, N, tn)
        a_spec = pl.BlockSpec(((None,) if a_split > 1 else ()) + (tk, tm), lambda m, n, k: a_idx(k, m))
        b_spec = pl.BlockSpec(((None,) if b_split > 1 else ()) + (tk, tn), lambda m, n, k: b_idx(k, n))
        dn = ((0,), (0,))
    o_idx = col_map(o_split, N, tn)
    o_block = ((None,) if o_split > 1 else ()) + (tm, tn)
    o_spec = pl.BlockSpec(o_block, lambda m, n, k: o_idx(m, n))
    o_shape = ((o_split, M, N // o_split) if o_split > 1 else (M, N))
    in_specs = [a_spec, b_spec]
    args = [a, b]
    if add is not None:
        in_specs.append(pl.BlockSpec((tm, tn), lambda m, n, k: (m, n)))
        args.append(add)
    n_lead = len(args)
    for t in after:
        in_specs.append(pl.BlockSpec(memory_space=pl.ANY))
        args.append(t)

    def body(*refs):
        a_ref, b_ref = refs[0], refs[1]
        add_ref = refs[2] if add is not None else None
        o_ref = refs[n_lead + len(after)]
        part = lax.dot_general(a_ref[...].astype(BF16), b_ref[...].astype(BF16), (dn, ((), ())),
                               preferred_element_type=F32)

        def finish(total):
            if add_ref is not None:
                total = total + add_ref[...]
            o_ref[...] = total.astype(o_ref.dtype)

        if gk == 1:
            finish(part)
        else:
            acc_ref = refs[-1]
            k = pl.program_id(2)

            @pl.when(k == 0)
            def _():
                acc_ref[...] = part

            @pl.when(jnp.logical_and(k > 0, k < gk - 1))
            def _():
                acc_ref[...] += part

            @pl.when(k == gk - 1)
            def _():
                finish(acc_ref[...] + part)

    return pl.pallas_call(
        body, name=name, grid=(gm, gn, gk), in_specs=in_specs, out_specs=o_spec,
        out_shape=jax.ShapeDtypeStruct(o_shape, out_dtype),
        scratch_shapes=([pltpu.VMEM((tm, tn), F32)] if gk > 1 else []),
        compiler_params=_params(("parallel", "parallel", "arbitrary")),
    )(*args)


def _row_tile():
    return _pick(SEQ, 256, 8)


def _rms_fwd(x, g, name):
    tm = _row_tile()
    (h,) = _tiled(lambda xv, gv: _rms(xv, gv), name, (SEQ // tm,),
                  [(x, (tm, D_MODEL), lambda i: (i, 0)), (g, (1, D_MODEL), lambda i: (0, 0))],
                  [((SEQ, D_MODEL), BF16, (tm, D_MODEL), lambda i: (i, 0), ())])
    return h


def _rms_bwd(x, g, dh, dres, name):
    tm = _row_tile()

    def fn(xv, gv, dhv, drv):
        _, vjp = jax.vjp(_rms, xv, gv)
        dx, dg = vjp(dhv)
        dx = dx + drv
        return dx, dx, dg

    row = lambda i: (i, 0)
    return _tiled(fn, name, (SEQ // tm,),
                  [(x, (tm, D_MODEL), row), (g, (1, D_MODEL), lambda i: (0, 0)),
                   (dh, (tm, D_MODEL), row), (dres, (tm, D_MODEL), row)],
                  [((SEQ, D_MODEL), F32, (tm, D_MODEL), row, ()),
                   ((SEQ, D_MODEL), BF16, (tm, D_MODEL), row, ()),
                   ((1, D_MODEL), F32, (1, D_MODEL), lambda i: (0, 0), (0,))])


def _loss_head(x, g, target):
    tm = _row_tile()

    def fn(xv, gv, tv):
        y, vjp = jax.vjp(_rms, xv, gv)
        err = y - tv
        part = 0.5 * jnp.sum(jnp.sum(err * err, axis=1, keepdims=True), axis=0, keepdims=True) / D_MODEL
        dx, dg = vjp(err / D_MODEL)
        return part, dx, dx, dg

    row = lambda i: (i, 0)
    one = lambda i: (0, 0)
    return _tiled(fn, "loss_head", (SEQ // tm,),
                  [(x, (tm, D_MODEL), row), (g, (1, D_MODEL), one), (target, (tm, D_MODEL), row)],
                  [((1, 1), F32, (1, 1), one, (0,)),
                   ((SEQ, D_MODEL), F32, (tm, D_MODEL), row, ()),
                   ((SEQ, D_MODEL), BF16, (tm, D_MODEL), row, ()),
                   ((1, D_MODEL), F32, (1, D_MODEL), one, (0,))])


def _p_col(width, index):
    return lambda tm: ((tm, width), lambda i: (i, index))


def _merge_fn(ga, gb, ta, tb):
    return _sigmoid(ga) * ta + _sigmoid(gb) * tb


def _merge_fwd(p, ta, tb):
    tm = _row_tile()
    d = D_MODEL
    ga_blk = 4 * _gw() + 2 * _sw()
    assert ga_blk % d == 0
    ia, ib = ga_blk // d, ga_blk // d + 1
    row = lambda i: (i, 0)
    (m,) = _tiled(_merge_fn, "merge_fwd", (SEQ // tm,),
                  [(p, (tm, d), lambda i: (i, ia)), (p, (tm, d), lambda i: (i, ib)),
                   (ta, (tm, d), row), (tb, (tm, d), row)],
                  [((SEQ, d), BF16, (tm, d), row, ())])
    return m


def _merge_bwd(p, ta, tb, dm):
    tm = _row_tile()
    d = D_MODEL
    ga_blk = 4 * _gw() + 2 * _sw()
    ia, ib = ga_blk // d, ga_blk // d + 1

    def fn(ga, gb, tav, tbv, dmv):
        _, vjp = jax.vjp(_merge_fn, ga, gb, tav, tbv)
        return vjp(dmv)

    row = lambda i: (i, 0)
    out = ((SEQ, d), BF16, (tm, d), row, ())
    return _tiled(fn, "merge_bwd", (SEQ // tm,),
                  [(p, (tm, d), lambda i: (i, ia)), (p, (tm, d), lambda i: (i, ib)),
                   (ta, (tm, d), row), (tb, (tm, d), row), (dm, (tm, d), row)],
                  [out, out, out, out])


def _gate_fn(ab, alog, dtb):
    lane = lax.broadcasted_iota(jnp.int32, ab.shape, 1)
    nd = 2 * GDN_HEADS
    g = -jnp.exp(alog) * _softplus(ab + dtb)
    beta = _sigmoid(ab)
    return jnp.where(lane < nd, g, jnp.where(lane < 2 * nd, beta, 0.0))


def _ab_index():
    off = 4 * _gw() + 2 * _sw() + 2 * D_MODEL
    assert off % AB_PAD == 0
    return off // AB_PAD


def _gate_fwd(p, alog, dtb):
    tm = _row_tile()
    iab = _ab_index()
    one = lambda i: (0, 0)
    (g,) = _tiled(_gate_fn, "gdn_gate_fwd", (SEQ // tm,),
                  [(p, (tm, AB_PAD), lambda i: (i, iab)), (alog, (1, AB_PAD), one), (dtb, (1, AB_PAD), one)],
                  [((SEQ, AB_PAD), F32, (tm, AB_PAD), lambda i: (i, 0), ())])
    return g


def _gate_bwd(p, alog, dtb, dg):
    tm = _row_tile()
    iab = _ab_index()
    one = lambda i: (0, 0)

    def fn(ab, al, db, dgv):
        _, vjp = jax.vjp(_gate_fn, ab, al, db)
        return vjp(dgv)

    return _tiled(fn, "gdn_gate_bwd", (SEQ // tm,),
                  [(p, (tm, AB_PAD), lambda i: (i, iab)), (alog, (1, AB_PAD), one), (dtb, (1, AB_PAD), one),
                   (dg, (tm, AB_PAD), lambda i: (i, 0))],
                  [((SEQ, AB_PAD), BF16, (tm, AB_PAD), lambda i: (i, 0), ()),
                   ((1, AB_PAD), F32, (1, AB_PAD), one, (0,)),
                   ((1, AB_PAD), F32, (1, AB_PAD), one, (0,))])


def _l2n(x):
    return x * lax.rsqrt(jnp.sum(x * x, axis=-1, keepdims=True) + NORM_EPS)


def _qkv_fn(xq, xk, xv, *taps):
    n = QKV_CONV
    q = _l2n(_silu(_conv_rows(xq, taps[0:n])))
    k = _l2n(_silu(_conv_rows(xk, taps[n:2 * n])))
    v = _silu(_conv_rows(xv, taps[2 * n:3 * n]))
    return q, k, v


def _qkv_specs(p, conv_w):
    hd, nh = HEAD_DIM, GDN_HEADS
    ins = [(p, (SEQ, hd), (lambda h, s=s: (0, s * nh + h))) for s in range(3)]
    ins += [(conv_w, (QKV_CONV, hd), (lambda h, s=s: (0, s * nh + h))) for s in range(3)]
    return ins


def _qkv_fwd(p, conv_w):
    hd, nh = HEAD_DIM, GDN_HEADS
    ins = _qkv_specs(p, conv_w)
    n_in = len(ins)
    out_spec = pl.BlockSpec((SEQ, hd), lambda h: (0, h))

    def body(*refs):
        xs = [r[...] for r in refs[:3]]
        taps = [refs[3 + s][j:j + 1, :] for s in range(3) for j in range(QKV_CONV)]
        q, k, v = _qkv_fn(*xs, *taps)
        refs[n_in][...] = q
        refs[n_in + 1][...] = k
        refs[n_in + 2][...] = v

    return pl.pallas_call(
        body, name="gdn_qkv_fwd", grid=(nh,),
        in_specs=[pl.BlockSpec(b, m) for _, b, m in ins], out_specs=[out_spec] * 3,
        out_shape=[jax.ShapeDtypeStruct((SEQ, nh * hd), F32)] * 3,
        compiler_params=_params(("parallel",)),
    )(*[a for a, _, _ in ins])


def _qkv_bwd(p, conv_w, dq, dk, dv):
    hd, nh = HEAD_DIM, GDN_HEADS
    ins = _qkv_specs(p, conv_w) + [(t, (SEQ, hd), lambda h: (0, h)) for t in (dq, dk, dv)]
    n_in = len(ins)

    def body(*refs):
        xs = [r[...] for r in refs[:3]]
        taps = [refs[3 + s][j:j + 1, :] for s in range(3) for j in range(QKV_CONV)]
        cts = tuple(r[...] for r in refs[6:9])
        _, vjp = jax.vjp(_qkv_fn, *xs, *taps)
        grads = vjp(cts)
        for s in range(3):
            refs[n_in + s][...] = grads[s].astype(BF16)
            for j in range(QKV_CONV):
                refs[n_in + 3 + s][j:j + 1, :] = jnp.sum(grads[3 + s * QKV_CONV + j], axis=0, keepdims=True)

    dx_spec = pl.BlockSpec((SEQ, hd), lambda h: (0, h))
    dw_spec = pl.BlockSpec((QKV_CONV, hd), lambda h: (0, h))
    outs = pl.pallas_call(
        body, name="gdn_qkv_bwd", grid=(nh,),
        in_specs=[pl.BlockSpec(b, m) for _, b, m in ins], out_specs=[dx_spec] * 3 + [dw_spec] * 3,
        out_shape=[jax.ShapeDtypeStruct((SEQ, nh * hd), BF16)] * 3
        + [jax.ShapeDtypeStruct((QKV_CONV, nh * hd), F32)] * 3,
        compiler_params=_params(("parallel",)),
    )(*[a for a, _, _ in ins])
    return outs[0:3], jnp.concatenate(outs[3:6], axis=1)


def _prep_fn(qs, ks, vs, gblk):
    nh = len(qs)
    c = qs[0].shape[0]
    scale = HEAD_DIM ** -0.5
    ii = lax.broadcasted_iota(jnp.int32, (c, c), 0)
    jj = lax.broadcasted_iota(jnp.int32, (c, c), 1)
    eye = (ii == jj).astype(F32)
    lane = lax.broadcasted_iota(jnp.int32, gblk.shape, 1)
    kk_t = [_mm_nt(k, k) for k in ks]
    qk_t = [_mm_nt(q, k) for q, k in zip(qs, ks)]
    chains = [(h, d) for h in range(nh) for d in range(2)]
    pre = []
    for h, d in chains:
        g = jnp.sum(jnp.where(lane == h + d * nh, gblk, 0.0), axis=1, keepdims=True)
        beta = jnp.sum(jnp.where(lane == h + (2 + d) * nh, gblk, 0.0), axis=1, keepdims=True)
        incl = (jj <= ii) if d == 0 else (jj >= ii)
        strict = (jj < ii) if d == 0 else (jj > ii)
        incl_t = (ii <= jj) if d == 0 else (ii >= jj)
        g_row = jnp.sum(eye * g, axis=0, keepdims=True)
        gc = jnp.sum(jnp.where(incl, g_row, 0.0), axis=1, keepdims=True)
        gc_row = jnp.sum(jnp.where(incl_t, g, 0.0), axis=0, keepdims=True)
        decay = jnp.where(incl, jnp.exp(jnp.where(incl, gc - gc_row, 0.0)), 0.0)
        a = jnp.where(strict, kk_t[h] * beta * decay, 0.0)
        pre.append((g, beta, gc, decay, incl, a))
    ts = _inv_unit([p[5] for p in pre])
    us = [_mmh(t, vs[h] * p[1]) for (h, d), p, t in zip(chains, pre, ts)]
    ws = [_mmh(t, ks[h] * (p[1] * jnp.exp(p[2]))) for (h, d), p, t in zip(chains, pre, ts)]
    outs = [[None, None] for _ in range(nh)]
    for (h, d), (g, beta, gc, decay, incl, a), u, w in zip(chains, pre, us, ws):
        qk = jnp.where(incl, qk_t[h] * (scale * decay), 0.0)
        g_last = jnp.sum(g, axis=0, keepdims=True)
        q_dec = qs[h] * (scale * jnp.exp(gc))
        k_dec = ks[h] * jnp.exp(g_last - gc)
        outs[h][d] = (u, w, q_dec, k_dec, qk, jnp.exp(g_last))
    return outs


def _n_chunks():
    return SEQ // GDN_CHUNK


def _gdn_prep_fwd(q, k, v, gates):
    c, hd, nh, nc = GDN_CHUNK, HEAD_DIM, GDN_HEADS, _n_chunks()
    gw = nh * hd
    row = lambda n: (n, 0)

    def body(q_ref, k_ref, v_ref, g_ref, u_ref, w_ref, qd_ref, kd_ref, qk_ref, dec_ref):
        heads = [slice(h * hd, (h + 1) * hd) for h in range(nh)]
        res = _prep_fn([q_ref[:, s] for s in heads], [k_ref[:, s] for s in heads], [v_ref[:, s] for s in heads],
                       g_ref[...])
        for h, cols in enumerate(heads):
            for d in range(2):
                u, w, qd, kd, qk, dec = res[h][d]
                u_ref[d, :, cols] = u
                w_ref[d, :, cols] = w
                qd_ref[d, :, cols] = qd
                kd_ref[d, :, cols] = kd
                qk_ref[d, h] = qk
                dec_ref[d, h:h + 1, :] = jnp.broadcast_to(dec, (1, LANES))

    wide = pl.BlockSpec((2, c, gw), lambda n: (0, n, 0))
    return pl.pallas_call(
        body, name="gdn_prep_fwd", grid=(nc,),
        in_specs=[pl.BlockSpec((c, gw), row)] * 3 + [pl.BlockSpec((c, LANES), row)],
        out_specs=[wide] * 4 + [pl.BlockSpec((2, nh, c, c), lambda n: (0, 0, n, 0)),
                                pl.BlockSpec((2, None, nh, LANES), lambda n: (0, n, 0, 0))],
        out_shape=[jax.ShapeDtypeStruct((2, SEQ, gw), F32)] * 4
        + [jax.ShapeDtypeStruct((2, nh, SEQ, c), F32), jax.ShapeDtypeStruct((2, nc, nh, LANES), F32)],
        compiler_params=_params(("parallel",)),
    )(q, k, v, gates)


def _gdn_prep_bwd(q, k, v, gates, cts):
    c, hd, nh, nc = GDN_CHUNK, HEAD_DIM, GDN_HEADS, _n_chunks()
    gw = nh * hd
    row = lambda n: (n, 0)

    def body(*refs):
        q_ref, k_ref, v_ref, g_ref = refs[:4]
        ct_refs = refs[4:16]
        dq_ref, dk_ref, dv_ref, dg_ref = refs[16:20]
        heads = [slice(h * hd, (h + 1) * hd) for h in range(nh)]
        _, vjp = jax.vjp(_prep_fn, [q_ref[:, s] for s in heads], [k_ref[:, s] for s in heads],
                         [v_ref[:, s] for s in heads], g_ref[...])
        cts_in = []
        for h, cols in enumerate(heads):
            per_dir = []
            for d in range(2):
                r = ct_refs[6 * d:6 * d + 6]
                per_dir.append((r[0][:, cols], r[1][:, cols], r[2][:, cols], r[3][:, cols], r[4][h],
                                r[5][h:h + 1, 0:1]))
            cts_in.append(per_dir)
        dqs, dks, dvs, dg = vjp(cts_in)
        for h, cols in enumerate(heads):
            dq_ref[:, cols] = dqs[h]
            dk_ref[:, cols] = dks[h]
            dv_ref[:, cols] = dvs[h]
        dg_ref[...] = dg

    one_dir = [pl.BlockSpec((c, gw), row)] * 4 + [pl.BlockSpec((nh, c, c), lambda n: (0, n, 0)),
                                                  pl.BlockSpec((None, nh, LANES), lambda n: (n, 0, 0))]
    return pl.pallas_call(
        body, name="gdn_prep_bwd", grid=(nc,),
        in_specs=[pl.BlockSpec((c, gw), row)] * 3 + [pl.BlockSpec((c, LANES), row)] + one_dir * 2,
        out_specs=[pl.BlockSpec((c, gw), row)] * 3 + [pl.BlockSpec((c, LANES), row)],
        out_shape=[jax.ShapeDtypeStruct((SEQ, gw), F32)] * 3 + [jax.ShapeDtypeStruct((SEQ, LANES), F32)],
        compiler_params=_params(("parallel",)),
    )(q, k, v, gates, *cts)


def _scan_steps(states, us, ws, qds, kds, qks, decs):
    w_s = [_mm(w, s) for w, s in zip(ws, states)]
    q_s = [_mm(qd, s) for qd, s in zip(qds, states)]
    v_new = [u - x for u, x in zip(us, w_s)]
    outs = [x + _mm(qk, vn) for x, qk, vn in zip(q_s, qks, v_new)]
    k_v = [_mm_tn(kd, vn) for kd, vn in zip(kds, v_new)]
    new_states = [s * dec + x for s, dec, x in zip(states, decs, k_v)]
    return new_states, outs


def _scan_operands(ins, chains, dec_lanes):
    cols = lambda h: slice(h * HEAD_DIM, (h + 1) * HEAD_DIM)
    wide = [[ins[6 * d + j][:, cols(h)] for d, h in chains] for j in range(4)]
    qks = [ins[6 * d + 4][h] for d, h in chains]
    decs = [ins[6 * d + 5][h:h + 1, dec_lanes] for d, h in chains]
    return (*wide, qks, decs)


def _scan_in_specs(chunk_of):
    c, hd, nh = GDN_CHUNK, HEAD_DIM, GDN_HEADS
    gw = nh * hd
    specs = []
    for d in range(2):
        f = chunk_of[d]
        specs += [pl.BlockSpec((None, c, gw), lambda n, d=d, f=f: (d, f(n), 0))] * 4
        specs += [pl.BlockSpec((None, nh, c, c), lambda n, d=d, f=f: (d, 0, f(n), 0)),
                  pl.BlockSpec((None, None, nh, LANES), lambda n, d=d, f=f: (d, f(n), 0, 0))]
    return specs


def _gdn_scan_fwd(u, w, qd, kd, qk, dec):
    c, hd, nh, nc = GDN_CHUNK, HEAD_DIM, GDN_HEADS, _n_chunks()
    gw = nh * hd
    chunk_of = (lambda n: n, lambda n: nc - 1 - n)

    def body(*refs):
        ins = refs[:12]
        o_refs = refs[12:14]
        st_refs = refs[14:16]
        s_ref = refs[16]

        @pl.when(pl.program_id(0) == 0)
        def _():
            s_ref[...] = jnp.zeros_like(s_ref)

        chains = [(d, h) for d in range(2) for h in range(nh)]
        states = [s_ref[d * nh + h] for d, h in chains]
        for (d, h), st in zip(chains, states):
            st_refs[d][h] = st
        new_states, outs = _scan_steps(states, *_scan_operands(ins, chains, slice(None)))
        for (d, h), st, o in zip(chains, new_states, outs):
            s_ref[d * nh + h] = st
            o_refs[d][:, h * hd:(h + 1) * hd] = o

    o_specs = [pl.BlockSpec((c, gw), lambda n, f=f: (f(n), 0)) for f in chunk_of]
    st_specs = [pl.BlockSpec((None, nh, hd, hd), lambda n, f=f: (f(n), 0, 0, 0)) for f in chunk_of]
    return pl.pallas_call(
        body, name="gdn_scan_fwd", grid=(nc,), in_specs=_scan_in_specs(chunk_of),
        out_specs=o_specs + st_specs,
        out_shape=[jax.ShapeDtypeStruct((SEQ, gw), F32)] * 2 + [jax.ShapeDtypeStruct((nc, nh, hd, hd), F32)] * 2,
        scratch_shapes=[pltpu.VMEM((2 * nh, hd, hd), F32)],
        compiler_params=_params(("arbitrary",)),
    )(*([u, w, qd, kd, qk, dec] * 2))


def _gdn_scan_bwd(u, w, qd, kd, qk, dec, st0, st1, do):
    c, hd, nh, nc = GDN_CHUNK, HEAD_DIM, GDN_HEADS, _n_chunks()
    gw = nh * hd
    chunk_of = (lambda n: nc - 1 - n, lambda n: n)

    def body(*refs):
        ins = refs[:12]
        st_in = refs[12:14]
        do_in = refs[14:16]
        outs = refs[16:28]
        ds_ref = refs[28]

        @pl.when(pl.program_id(0) == 0)
        def _():
            ds_ref[...] = jnp.zeros_like(ds_ref)

        chains = [(d, h) for d in range(2) for h in range(nh)]
        states = [st_in[d][h] for d, h in chains]
        _, vjp = jax.vjp(_scan_steps, states, *_scan_operands(ins, chains, slice(0, 1)))
        ct_state = [ds_ref[d * nh + h] for d, h in chains]
        ct_out = [do_in[d][:, h * hd:(h + 1) * hd] for d, h in chains]
        grads = vjp((ct_state, ct_out))
        for i, (d, h) in enumerate(chains):
            cols = slice(h * hd, (h + 1) * hd)
            du_r, dw_r, dqd_r, dkd_r, dqk_r, ddec_r = outs[6 * d:6 * d + 6]
            ds_ref[d * nh + h] = grads[0][i]
            du_r[:, cols] = grads[1][i]
            dw_r[:, cols] = grads[2][i]
            dqd_r[:, cols] = grads[3][i]
            dkd_r[:, cols] = grads[4][i]
            dqk_r[h] = grads[5][i]
            ddec_r[h:h + 1, :] = jnp.broadcast_to(grads[6][i], (1, LANES))

    st_specs = [pl.BlockSpec((None, nh, hd, hd), lambda n, f=f: (f(n), 0, 0, 0)) for f in chunk_of]
    do_specs = [pl.BlockSpec((c, gw), lambda n, f=f: (f(n), 0)) for f in chunk_of]
    out_specs, out_shape = [], []
    for f in chunk_of:
        out_specs += [pl.BlockSpec((c, gw), lambda n, f=f: (f(n), 0))] * 4
        out_specs += [pl.BlockSpec((nh, c, c), lambda n, f=f: (0, f(n), 0)),
                      pl.BlockSpec((None, nh, LANES), lambda n, f=f: (f(n), 0, 0))]
        out_shape += [jax.ShapeDtypeStruct((SEQ, gw), F32)] * 4
        out_shape += [jax.ShapeDtypeStruct((nh, SEQ, c), F32), jax.ShapeDtypeStruct((nc, nh, LANES), F32)]
    return pl.pallas_call(
        body, name="gdn_scan_bwd", grid=(nc,), in_specs=_scan_in_specs(chunk_of) + st_specs + do_specs,
        out_specs=out_specs, out_shape=out_shape,
        scratch_shapes=[pltpu.VMEM((2 * nh, hd, hd), F32)],
        compiler_params=_params(("arbitrary",)),
    )(*([u, w, qd, kd, qk, dec] * 2), st0, st1, do, do)


def _post_fn(o0, o1, z, gn):
    return _rms(o0 + o1, gn) * _silu(z)


def _gdn_post_fwd(o0, o1, p, gn):
    tm, hd, nh = _row_tile(), HEAD_DIM, GDN_HEADS
    zoff = 3 * nh
    blk = lambda i, h: (i, h)
    (ya,) = _tiled(_post_fn, "gdn_post_fwd", (SEQ // tm, nh),
                   [(o0, (tm, hd), blk), (o1, (tm, hd), blk),
                    (p, (tm, hd), lambda i, h: (i, zoff + h)), (gn, (1, hd), lambda i, h: (0, 0))],
                   [((SEQ, nh * hd), BF16, (tm, hd), blk, ())])
    return ya


def _gdn_post_bwd(o0, o1, p, gn, dya):
    tm, hd, nh = _row_tile(), HEAD_DIM, GDN_HEADS
    zoff = 3 * nh
    blk = lambda i, h: (i, h)

    def fn(a, b, z, g, dy):
        _, vjp = jax.vjp(_post_fn, a, b, z, g)
        do, _, dz, dg = vjp(dy.astype(F32))
        return do, dz, dg

    return _tiled(fn, "gdn_post_bwd", (SEQ // tm, nh),
                  [(o0, (tm, hd), blk), (o1, (tm, hd), blk),
                   (p, (tm, hd), lambda i, h: (i, zoff + h)), (gn, (1, hd), lambda i, h: (0, 0)),
                   (dya, (tm, hd), blk)],
                  [((SEQ, nh * hd), F32, (tm, hd), blk, ()),
                   ((SEQ, nh * hd), BF16, (tm, hd), blk, ()),
                   ((1, hd), F32, (1, hd), lambda i, h: (0, 0), (0, 1))],
                  sem=("arbitrary", "arbitrary"))


def _sgu_ln(v, lg, lb):
    gv = _gelu(v)
    mu = jnp.mean(gv, axis=-1, keepdims=True)
    cen = gv - mu
    var = jnp.mean(cen * cen, axis=-1, keepdims=True)
    return cen * lax.rsqrt(var + NORM_EPS) * lg + lb


def _sgu_mix(vn, w_ref, bt_ref):
    parts = []
    for g in range(SGU_GROUPS):
        cols = slice(g * SGU_DIM, (g + 1) * SGU_DIM)
        parts.append(_mm(w_ref[g], vn[:, cols]) + bt_ref[:, g:g + 1])
    return jnp.concatenate(parts, axis=1)


def _sgu_specs(p):
    sw, sb = _sw(), SGU_BLOCK
    uoff = 4 * _gw()
    assert uoff % sw == 0
    iu = uoff // sw
    return [(p, (sb, sw), lambda n: (n, iu)), (p, (sb, sw), lambda n: (n, iu + 1))]


def _sgu_fwd(p, lg, lb, w_s, b_t):
    sw, sb = _sw(), SGU_BLOCK
    ins = _sgu_specs(p)

    def body(u_ref, v_ref, lg_ref, lb_ref, w_ref, bt_ref, y_ref):
        vn = _sgu_ln(v_ref[...], lg_ref[...], lb_ref[...])
        y_ref[...] = (_gelu(u_ref[...]) * _sgu_mix(vn, w_ref, bt_ref)).astype(BF16)

    one2 = lambda n: (0, 0)
    return pl.pallas_call(
        body, name="sgu_fwd", grid=(SEQ // sb,),
        in_specs=[pl.BlockSpec(b, m) for _, b, m in ins]
        + [pl.BlockSpec((1, sw), one2), pl.BlockSpec((1, sw), one2),
           pl.BlockSpec((SGU_GROUPS, sb, sb), lambda n: (0, 0, 0)), pl.BlockSpec((sb, LANES), one2)],
        out_specs=pl.BlockSpec((sb, sw), lambda n: (n, 0)),
        out_shape=jax.ShapeDtypeStruct((SEQ, sw), BF16),
        compiler_params=_params(("parallel",)),
    )(p, p, lg, lb, w_s, b_t)


def _sgu_bwd(p, lg, lb, w_s, b_t, dy):
    sw, sb, ng = _sw(), SGU_BLOCK, SGU_GROUPS
    ins = _sgu_specs(p)

    def body(u_ref, v_ref, lg_ref, lb_ref, w_ref, bt_ref, dy_ref, du_ref, dv_ref, dlg_ref, dlb_ref, dw_ref, dbt_ref):
        first = pl.program_id(0) == 0
        u, v = u_ref[...], v_ref[...]
        gu, gelu_vjp = jax.vjp(_gelu, u)
        vn, ln_vjp = jax.vjp(_sgu_ln, v, lg_ref[...], lb_ref[...])
        s = _sgu_mix(vn, w_ref, bt_ref)
        dyv = dy_ref[...].astype(F32)
        ds = dyv * gu
        (du,) = gelu_vjp(dyv * s)
        lane = lax.broadcasted_iota(jnp.int32, (sb, LANES), 1)
        dvn_parts = []
        dbt = jnp.zeros((sb, LANES), F32)
        for g in range(ng):
            cols = slice(g * SGU_DIM, (g + 1) * SGU_DIM)
            ds_g = ds[:, cols]
            dw_g = _mm_nt(ds_g, vn[:, cols])
            dvn_parts.append(_mm_tn(w_ref[g], ds_g))
            dbt = dbt + jnp.where(lane == g, jnp.sum(ds_g, axis=1, keepdims=True), 0.0)

            @pl.when(first)
            def _(g=g, dw_g=dw_g):
                dw_ref[g] = dw_g

            @pl.when(jnp.logical_not(first))
            def _(g=g, dw_g=dw_g):
                dw_ref[g] += dw_g

        dv, dlg, dlb = ln_vjp(jnp.concatenate(dvn_parts, axis=1))
        du_ref[...] = du.astype(BF16)
        dv_ref[...] = dv.astype(BF16)

        @pl.when(first)
        def _():
            dlg_ref[...] = dlg
            dlb_ref[...] = dlb
            dbt_ref[...] = dbt

        @pl.when(jnp.logical_not(first))
        def _():
            dlg_ref[...] += dlg
            dlb_ref[...] += dlb
            dbt_ref[...] += dbt

    one2 = lambda n: (0, 0)
    row = pl.BlockSpec((sb, sw), lambda n: (n, 0))
    return pl.pallas_call(
        body, name="sgu_bwd", grid=(SEQ // sb,),
        in_specs=[pl.BlockSpec(b, m) for _, b, m in ins]
        + [pl.BlockSpec((1, sw), one2), pl.BlockSpec((1, sw), one2),
           pl.BlockSpec((ng, sb, sb), lambda n: (0, 0, 0)), pl.BlockSpec((sb, LANES), one2), row],
        out_specs=[row, row, pl.BlockSpec((1, sw), one2), pl.BlockSpec((1, sw), one2),
                   pl.BlockSpec((ng, sb, sb), lambda n: (0, 0, 0)), pl.BlockSpec((sb, LANES), one2)],
        out_shape=[jax.ShapeDtypeStruct((SEQ, sw), BF16)] * 2 + [jax.ShapeDtypeStruct((1, sw), F32)] * 2
        + [jax.ShapeDtypeStruct((ng, sb, sb), F32), jax.ShapeDtypeStruct((sb, LANES), F32)],
        compiler_params=_params(("arbitrary",)),
    )(p, p, lg, lb, w_s, b_t, dy)


def _ffn_fn(xg, xv, bg, bv, *taps):
    n = FFN_CONV
    cg = _conv_rows(xg, taps[0:n]) + bg
    cv = _conv_rows(xv, taps[n:2 * n]) + bv
    return _silu(cg) * cv


def _ffn_tile():
    return _pick(D_FF, 256)


def _ffn_specs(up, conv_w, conv_b):
    tc = _ffn_tile()
    nt = D_FF // tc
    ins = [(up, (None, SEQ, tc), (lambda j, s=s: (s, 0, j))) for s in range(2)]
    ins += [(conv_b, (1, tc), (lambda j, s=s: (0, s * nt + j))) for s in range(2)]
    ins += [(conv_w, (FFN_CONV, tc), (lambda j, s=s: (0, s * nt + j))) for s in range(2)]
    return ins


def _ffn_act_fwd(up, conv_w, conv_b):
    tc = _ffn_tile()
    ins = _ffn_specs(up, conv_w, conv_b)

    def body(xg, xv, bg, bv, wg, wv, o_ref):
        taps = [w[j:j + 1, :] for w in (wg, wv) for j in range(FFN_CONV)]
        o_ref[...] = _ffn_fn(xg[...], xv[...], bg[...], bv[...], *taps).astype(BF16)

    return pl.pallas_call(
        body, name="ffn_act_fwd", grid=(D_FF // tc,),
        in_specs=[pl.BlockSpec(b, m) for _, b, m in ins],
        out_specs=pl.BlockSpec((SEQ, tc), lambda j: (0, j)),
        out_shape=jax.ShapeDtypeStruct((SEQ, D_FF), BF16),
        compiler_params=_params(("parallel",)),
    )(*[a for a, _, _ in ins])


def _ffn_act_bwd(up, conv_w, conv_b, dact):
    tc = _ffn_tile()
    nt = D_FF // tc
    ins = _ffn_specs(up, conv_w, conv_b) + [(dact, (SEQ, tc), lambda j: (0, j))]

    def body(xg, xv, bg, bv, wg, wv, dact_ref, dup_ref, dwg_ref, dwv_ref, dbg_ref, dbv_ref):
        taps = [w[j:j + 1, :] for w in (wg, wv) for j in range(FFN_CONV)]
        _, vjp = jax.vjp(_ffn_fn, xg[...], xv[...], bg[...], bv[...], *taps)
        grads = vjp(dact_ref[...].astype(F32))
        dup_ref[0] = grads[0].astype(BF16)
        dup_ref[1] = grads[1].astype(BF16)
        dbg_ref[...] = jnp.sum(grads[2], axis=0, keepdims=True)
        dbv_ref[...] = jnp.sum(grads[3], axis=0, keepdims=True)
        for j in range(FFN_CONV):
            dwg_ref[j:j + 1, :] = jnp.sum(grads[4 + j], axis=0, keepdims=True)
            dwv_ref[j:j + 1, :] = jnp.sum(grads[4 + FFN_CONV + j], axis=0, keepdims=True)

    col = lambda j: (0, j)
    outs = pl.pallas_call(
        body, name="ffn_act_bwd", grid=(nt,),
        in_specs=[pl.BlockSpec(b, m) for _, b, m in ins],
        out_specs=[pl.BlockSpec((2, SEQ, tc), lambda j: (0, 0, j)),
                   pl.BlockSpec((FFN_CONV, tc), col), pl.BlockSpec((FFN_CONV, tc), col),
                   pl.BlockSpec((1, tc), col), pl.BlockSpec((1, tc), col)],
        out_shape=[jax.ShapeDtypeStruct((2, SEQ, D_FF), BF16),
                   jax.ShapeDtypeStruct((FFN_CONV, D_FF), F32), jax.ShapeDtypeStruct((FFN_CONV, D_FF), F32),
                   jax.ShapeDtypeStruct((1, D_FF), F32), jax.ShapeDtypeStruct((1, D_FF), F32)],
        compiler_params=_params(("parallel",)),
    )(*[a for a, _, _ in ins])
    dup, dwg, dwv, dbg, dbv = outs
    return dup, jnp.concatenate([dwg, dwv], axis=1), jnp.concatenate([dbg, dbv], axis=1)


def _pad_lanes(v):
    return jnp.pad(v.reshape(1, -1), ((0, 0), (0, LANES - v.size)))


def _layer_fwd(x, lw):
    h = _rms_fwd(x, lw['norm_mix_g'], "rms_mix_fwd")
    p = _matmul(h, lw['w_all'], 'nn', F32, "mm_in")
    gates = _gate_fwd(p, lw['a_log'], lw['dt_bias'])
    q, k, v = _qkv_fwd(p, lw['qkv_conv_w'])
    u, w, qd, kd, qk, dec = _gdn_prep_fwd(q, k, v, gates)
    o0, o1, st0, st1 = _gdn_scan_fwd(u, w, qd, kd, qk, dec)
    ya = _gdn_post_fwd(o0, o1, p, lw['gdn_norm_g'])
    yb = _sgu_fwd(p, lw['sgu_ln_g'], lw['sgu_ln_b'], lw['sgu_w'], lw['sgu_bt'])
    ta = _matmul(ya, lw['w_branch_a'], 'nn', F32, "mm_branch_a", b_split=N_CHIPS)
    tb = _matmul(yb, lw['w_branch_b'], 'nn', F32, "mm_branch_b", b_split=N_CHIPS)
    m = _merge_fwd(p, ta, tb)
    x1 = _matmul(m, lw['w_out'], 'nn', F32, "mm_out", add=x)
    h2 = _rms_fwd(x1, lw['norm_ffn_g'], "rms_ffn_fwd")
    up = _matmul(h2, lw['w_up'], 'nn', F32, "mm_up", b_split=N_CHIPS, o_split=2)
    act = _ffn_act_fwd(up, lw['ffn_conv_w'], lw['ffn_conv_b'])
    x2 = _matmul(act, lw['w_down'], 'nn', F32, "mm_down", add=x1)
    saved = dict(x=x, h=h, p=p, gates=gates, q=q, k=k, v=v, u=u, w=w, qd=qd, kd=kd, qk=qk, dec=dec,
                 o0=o0, o1=o1, st0=st0, st1=st1, ya=ya, yb=yb, ta=ta, tb=tb, m=m, x1=x1, h2=h2, up=up, act=act)
    return x2, saved


def _layer_bwd(dx, dx_bf, lw, s, after=(), mid=None):
    g = {}
    dact = _matmul(dx_bf, lw['w_down'], 'nt', BF16, "mm_down_dgrad", after=after)
    g['w_down'] = _matmul(s['act'], dx_bf, 'tn', F32, "mm_down_wgrad")
    dup, g['ffn_conv_w'], g['ffn_conv_b'] = _ffn_act_bwd(s['up'], lw['ffn_conv_w'], lw['ffn_conv_b'], dact)
    dh2 = _matmul(dup, lw['w_up'], 'nt', F32, "mm_up_dgrad", a_split=2, b_split=N_CHIPS)
    g['w_up'] = _matmul(s['h2'], dup, 'tn', F32, "mm_up_wgrad", b_split=2, o_split=N_CHIPS)
    dx1, dx1_bf, g['norm_ffn_g'] = _rms_bwd(s['x1'], lw['norm_ffn_g'], dh2, dx, "rms_ffn_bwd")
    after_mid = mid(g, dh2) if mid is not None else ()
    dm = _matmul(dx1_bf, lw['w_out'], 'nt', F32, "mm_out_dgrad", after=after_mid)
    g['w_out'] = _matmul(s['m'], dx1_bf, 'tn', F32, "mm_out_wgrad")
    d_ga, d_gb, d_ta, d_tb = _merge_bwd(s['p'], s['ta'], s['tb'], dm)
    dya = _matmul(d_ta, lw['w_branch_a'], 'nt', BF16, "mm_branch_a_dgrad", b_split=N_CHIPS)
    dyb = _matmul(d_tb, lw['w_branch_b'], 'nt', BF16, "mm_branch_b_dgrad", b_split=N_CHIPS)
    g['w_branch_a'] = _matmul(s['ya'], d_ta, 'tn', F32, "mm_branch_a_wgrad", o_split=N_CHIPS)
    g['w_branch_b'] = _matmul(s['yb'], d_tb, 'tn', F32, "mm_branch_b_wgrad", o_split=N_CHIPS)
    du_s, dv_s, g['sgu_ln_g'], g['sgu_ln_b'], g['sgu_w'], g['sgu_bt'] = _sgu_bwd(
        s['p'], lw['sgu_ln_g'], lw['sgu_ln_b'], lw['sgu_w'], lw['sgu_bt'], dyb)
    do, dz, g['gdn_norm_g'] = _gdn_post_bwd(s['o0'], s['o1'], s['p'], lw['gdn_norm_g'], dya)
    cts = _gdn_scan_bwd(s['u'], s['w'], s['qd'], s['kd'], s['qk'], s['dec'], s['st0'], s['st1'], do)
    dq, dk, dv, dgates = _gdn_prep_bwd(s['q'], s['k'], s['v'], s['gates'], cts)
    (dxq, dxk, dxv), g['qkv_conv_w'] = _qkv_bwd(s['p'], lw['qkv_conv_w'], dq, dk, dv)
    d_ab, g['a_log'], g['dt_bias'] = _gate_bwd(s['p'], lw['a_log'], lw['dt_bias'], dgates)
    dp = jnp.concatenate([dxq, dxk, dxv, dz, du_s, dv_s, d_ga, d_gb, d_ab], axis=1)
    dh = _matmul(dp, lw['w_all'], 'nt', F32, "mm_in_dgrad")
    g['w_all'] = _matmul(s['h'], dp, 'tn', F32, "mm_in_wgrad")
    dx0, dx0_bf, g['norm_mix_g'] = _rms_bwd(s['x'], lw['norm_mix_g'], dh, dx1, "rms_mix_bwd")
    return dx0, dx0_bf, g


def _w_all_from_w_in(w_in):
    n_ab = 4 * GDN_HEADS
    cut = 4 * _gw()
    pad = jnp.zeros((w_in.shape[0], AB_PAD - n_ab), w_in.dtype)
    return jnp.concatenate([w_in[:, :cut], w_in[:, cut + n_ab:], w_in[:, cut:cut + n_ab], pad], axis=1)


def _w_in_grad_from_all(g_all):
    n_ab = 4 * GDN_HEADS
    cut = 4 * _gw()
    n_main = cut + 2 * _sw() + 2 * D_MODEL
    return jnp.concatenate([g_all[:, :cut], g_all[:, n_main:n_main + n_ab], g_all[:, cut:n_main]], axis=1)


def _layer_weights(l, big, small):
    cat_cols = lambda t: jnp.concatenate([t[j] for j in range(N_CHIPS)], axis=-1)
    return dict(
        w_all=_w_all_from_w_in(cat_cols(big['w_in'])),
        w_branch_a=big['w_branch_a'], w_branch_b=big['w_branch_b'], w_up=big['w_up'],
        w_out=big['w_out'].reshape(D_MODEL, D_MODEL), w_down=big['w_down'].reshape(D_FF, D_MODEL),
        qkv_conv_w=cat_cols(big['qkv_conv_w']), ffn_conv_w=cat_cols(big['ffn_conv_w']),
        norm_mix_g=small['norm_mix_g'][l:l + 1], norm_ffn_g=small['norm_ffn_g'][l:l + 1],
        a_log=_pad_lanes(small['a_log'][l]), dt_bias=_pad_lanes(small['dt_bias'][l]),
        gdn_norm_g=small['gdn_norm_g'][l:l + 1],
        sgu_ln_g=small['sgu_ln_g'][l:l + 1], sgu_ln_b=small['sgu_ln_b'][l:l + 1], sgu_w=small['sgu_w'][l],
        sgu_bt=jnp.pad(small['sgu_b'][l].T, ((0, 0), (0, LANES - SGU_GROUPS))),
        ffn_conv_b=small['ffn_conv_b'][l:l + 1])


_SMALL_GRADS = ('norm_mix_g', 'a_log', 'dt_bias', 'gdn_norm_g', 'sgu_ln_g', 'sgu_ln_b', 'sgu_w', 'sgu_bt',
                'norm_ffn_g', 'ffn_conv_b', 'qkv_conv_w', 'ffn_conv_w')
_BIG = ('w_in', 'w_branch_a', 'w_branch_b', 'w_out', 'w_up', 'w_down')


def _big_grad_slices(g, names=_BIG):
    def one(k):
        if k == 'w_in':
            return _w_in_grad_from_all(g['w_all']).reshape(D_MODEL, N_CHIPS, -1).transpose(1, 0, 2)
        if k == 'w_out':
            return g[k].reshape(N_CHIPS, D_MODEL // N_CHIPS, D_MODEL)
        if k == 'w_down':
            return g[k].reshape(N_CHIPS, D_FF // N_CHIPS, D_MODEL)
        return g[k]
    return {k: one(k) for k in names}


def _adamw(w, g, m, v, name):
    shape = w.shape
    cols = shape[-1]
    rows = w.size // cols
    tr = _pick(rows, max(8, (1 << 18) // cols // 8 * 8), 8) if rows % 8 == 0 else rows

    def fn(wv, gv, mv, vv):
        m2 = ADAM_B1 * mv + (1.0 - ADAM_B1) * gv
        v2 = ADAM_B2 * vv + (1.0 - ADAM_B2) * (gv * gv)
        m_hat = m2 / (1.0 - ADAM_B1 ** ADAM_STEP)
        v_hat = v2 / (1.0 - ADAM_B2 ** ADAM_STEP)
        delta = -ADAM_LR * (m_hat / (jnp.sqrt(v_hat) + ADAM_EPS) + ADAM_WD * wv)
        return delta, m2, v2

    row = lambda i: (i, 0)
    outs = _tiled(fn, name, (rows // tr,),
                  [(t.reshape(rows, cols), (tr, cols), row) for t in (w, g, m, v)],
                  [((rows, cols), F32, (tr, cols), row, ())] * 3, sem=("parallel",))
    return [o.reshape(shape) for o in outs]


MESH_IDS = pl.DeviceIdType.MESH
ANY = pl.BlockSpec(memory_space=pl.ANY)


def _place():
    x, y, c = lax.axis_index("x"), lax.axis_index("y"), lax.axis_index("c")
    chips = [(1 - x, y), (x, 1 - y), (1 - x, 1 - y)]
    return x, y, c, 2 * x + y, chips


def _chip_index():
    return 2 * lax.axis_index("x") + lax.axis_index("y")


HBM = pl.BlockSpec(memory_space=pltpu.HBM)
SEM = pl.BlockSpec(memory_space=pltpu.SEMAPHORE)
DATAFLOW = pltpu.SideEffectType.DATAFLOW_SIDE_EFFECTING
TOKEN = jax.ShapeDtypeStruct((8, LANES), F32)


def _in_hbm(t):
    return pltpu.with_memory_space_constraint(t, pltpu.HBM)


def _place_shard(w, l, dtype, name):
    _, r, cols = w.shape
    tr = _pick(r, max(16, (1 << 18) // cols // 16 * 16), 16) if r % 16 == 0 else r

    def body(w_ref, o_ref):
        o_ref[...] = w_ref[...].astype(dtype)

    return pl.pallas_call(
        body, name=name, grid=(r // tr,),
        in_specs=[pl.BlockSpec((None, tr, cols), lambda i: (l, i, 0))],
        out_specs=pl.BlockSpec((None, tr, cols), lambda i: (_chip_index(), i, 0)),
        out_shape=jax.ShapeDtypeStruct((N_CHIPS, r, cols), dtype),
        compiler_params=_params(("parallel",)),
    )(w)


def _my_rows(ref_or_shape_rows, c):
    r = ref_or_shape_rows
    if r % 32 == 0:
        return pl.ds(c * (r // 2), r // 2), pl.ds((1 - c) * (r // 2), r // 2), True
    return pl.ds(0, r), pl.ds(0, r), False


def _gather_copies(bufs, send_sems, recv_sems):
    x, y, c, me, chips = _place()
    out = []
    for i, buf in enumerate(bufs):
        mine, _, _ = _my_rows(buf.shape[1], c)
        for j, (cx, cy) in enumerate(chips):
            def rcopy(slab, i=i, j=j, cx=cx, cy=cy):
                return pltpu.make_async_remote_copy(src_ref=slab, dst_ref=slab, send_sem=send_sems.at[3 * i + j],
                                                    recv_sem=recv_sems.at[3 * i + j], device_id=(cx, cy, c),
                                                    device_id_type=MESH_IDS)
            out.append((rcopy(buf.at[me, mine]), rcopy(buf.at[2 * cx + cy, mine])))
    return out


def _gather_start(xs, name, after=()):
    n = len(xs)
    n_after = len(after)

    def body(*refs):
        refs = refs[n + n_after:]
        send_sems, recv_sems = refs[0], refs[1]
        bufs = refs[2:n + 2]
        token = refs[n + 2]
        for out_going, _ in _gather_copies(bufs, send_sems, recv_sems):
            out_going.start()
        token[...] = jnp.zeros_like(token)

    res = pl.pallas_call(
        body, name=name, in_specs=[HBM] * n + [ANY] * n_after,
        out_specs=[SEM, SEM] + [HBM] * n + [pl.BlockSpec(memory_space=pltpu.VMEM)],
        out_shape=[pltpu.SemaphoreType.DMA((3 * n,)), pltpu.SemaphoreType.DMA((3 * n,))]
        + [pltpu.HBM(t.shape, t.dtype) for t in xs] + [TOKEN],
        input_output_aliases={i: i + 2 for i in range(n)},
        compiler_params=pltpu.CompilerParams(has_side_effects=DATAFLOW),
    )(*[_in_hbm(t) for t in xs], *after)
    return res[0], res[1], res[2:2 + n], res[2 + n]


def _gather_wait(bufs, send_sems, recv_sems, after, name):
    n = len(bufs)

    def body(*refs):
        b_refs = refs[:n]
        s_sems, r_sems = refs[n], refs[n + 1]
        for out_going, in_coming in _gather_copies(b_refs, s_sems, r_sems):
            out_going.wait_send()
            in_coming.wait_recv()

    return pl.pallas_call(
        body, name=name, in_specs=[HBM] * n + [SEM, SEM, ANY], out_specs=[HBM] * n,
        out_shape=[pltpu.HBM(t.shape, t.dtype) for t in bufs],
        input_output_aliases={i: i for i in range(n)},
        compiler_params=pltpu.CompilerParams(has_side_effects=DATAFLOW),
    )(*bufs, send_sems, recv_sems, after)


def _gather_forward(bufs):
    idx = [i for i, t in enumerate(bufs) if t.shape[1] % 32 == 0]
    xs = [bufs[i] for i in idx]
    n = len(xs)

    def body(*refs):
        o_refs = refs[n:2 * n]
        send_sems, recv_sems = refs[2 * n:]
        x, y, c, _, chips = _place()
        copies = []
        for i in range(n):
            mine, theirs, _ = _my_rows(xs[i].shape[1], c)
            for j, (cx, cy) in enumerate(chips):
                def rcopy(slab, i=i, j=j):
                    return pltpu.make_async_remote_copy(src_ref=slab, dst_ref=slab, send_sem=send_sems.at[3 * i + j],
                                                        recv_sem=recv_sems.at[3 * i + j], device_id=(x, y, 1 - c),
                                                        device_id_type=MESH_IDS)
                copies.append((rcopy(o_refs[i].at[2 * cx + cy, mine]), rcopy(o_refs[i].at[2 * cx + cy, theirs])))
                copies[-1][0].start()
        for out_going, in_coming in copies:
            out_going.wait_send()
            in_coming.wait_recv()

    res = pl.pallas_call(
        body, name="gather_forward", in_specs=[ANY] * n, out_specs=[ANY] * n,
        out_shape=[jax.ShapeDtypeStruct(t.shape, t.dtype) for t in xs],
        input_output_aliases={i: i for i in range(n)},
        scratch_shapes=[pltpu.SemaphoreType.DMA((3 * n,)), pltpu.SemaphoreType.DMA((3 * n,))],
        compiler_params=pltpu.CompilerParams(has_side_effects=True),
    )(*xs)
    out = list(bufs)
    for i, t in zip(idx, res):
        out[i] = t
    return out


def _swap_halves(gs):
    n = len(gs)

    def body(*refs):
        g_refs, b_refs = refs[:n], refs[n:2 * n]
        send_sems, recv_sems = refs[2 * n:]
        x, y, c, _, _ = _place()
        copies = []
        for i in range(n):
            rh = gs[i].shape[1] // 2
            src = g_refs[i].at[:, pl.ds((1 - c) * rh, rh), :]
            copies.append(pltpu.make_async_remote_copy(src_ref=src, dst_ref=b_refs[i], send_sem=send_sems.at[i],
                                                       recv_sem=recv_sems.at[i], device_id=(x, y, 1 - c),
                                                       device_id_type=MESH_IDS))
            copies[-1].start()
        for cp in copies:
            cp.wait()

    return pl.pallas_call(
        body, name="grad_swap_halves", in_specs=[ANY] * n, out_specs=[ANY] * n,
        out_shape=[jax.ShapeDtypeStruct((t.shape[0], t.shape[1] // 2, t.shape[2]), t.dtype) for t in gs],
        scratch_shapes=[pltpu.SemaphoreType.DMA((n,)), pltpu.SemaphoreType.DMA((n,))],
        compiler_params=pltpu.CompilerParams(has_side_effects=True),
    )(*gs)


def _half_tile(rh, cols):
    return _pick(rh, max(16, (1 << 18) // cols // 16 * 16), 16)


def _chip_partial(g, b, name):
    nchip, r, cols = g.shape
    rh = r // 2
    tr = _half_tile(rh, cols)
    nt = rh // tr

    def body(g_ref, b_ref, o_ref):
        o_ref[...] = (g_ref[...] + b_ref[...]).astype(BF16)

    return pl.pallas_call(
        body, name=name, grid=(nchip, nt),
        in_specs=[pl.BlockSpec((None, tr, cols), lambda j, i: (j, lax.axis_index("c") * nt + i, 0)),
                  pl.BlockSpec((None, tr, cols), lambda j, i: (j, i, 0))],
        out_specs=pl.BlockSpec((None, tr, cols), lambda j, i: (j, i, 0)),
        out_shape=jax.ShapeDtypeStruct((nchip, rh, cols), BF16),
        compiler_params=_params(("parallel", "parallel")),
    )(g, b)


def _scatter_copies(p_refs, r_refs, send_sems, recv_sems):
    _, _, c, _, chips = _place()
    copies = []
    for j, (cx, cy) in enumerate(chips):
        for i in range(len(p_refs)):
            copies.append(pltpu.make_async_remote_copy(
                src_ref=p_refs[i].at[2 * cx + cy], dst_ref=r_refs[i].at[j], send_sem=send_sems.at[3 * i + j],
                recv_sem=recv_sems.at[3 * i + j], device_id=(cx, cy, c), device_id_type=MESH_IDS))
    return copies


def _scatter_start(ps, name):
    n = len(ps)
    lands = [lax.empty((3,) + t.shape[1:], t.dtype) for t in ps]

    def body(*refs):
        send_sems, recv_sems = refs[2 * n], refs[2 * n + 1]
        p_refs = refs[2 * n + 2:3 * n + 2]
        r_refs = refs[3 * n + 2:4 * n + 2]
        token = refs[4 * n + 2]
        for cp in _scatter_copies(p_refs, r_refs, send_sems, recv_sems):
            cp.start()
        token[...] = jnp.zeros_like(token)

    res = pl.pallas_call(
        body, name=name, in_specs=[HBM] * (2 * n),
        out_specs=[SEM, SEM] + [HBM] * (2 * n) + [pl.BlockSpec(memory_space=pltpu.VMEM)],
        out_shape=[pltpu.SemaphoreType.DMA((3 * n,)), pltpu.SemaphoreType.DMA((3 * n,))]
        + [pltpu.HBM(t.shape, t.dtype) for t in list(ps) + lands] + [TOKEN],
        input_output_aliases={i: i + 2 for i in range(2 * n)},
        compiler_params=pltpu.CompilerParams(has_side_effects=DATAFLOW),
    )(*[_in_hbm(t) for t in list(ps) + lands])
    return res[0], res[1], res[2:2 + n], res[2 + n:2 + 2 * n], res[2 + 2 * n]


def _scatter_wait(ps, lands, send_sems, recv_sems, after, name):
    n = len(ps)

    def body(*refs):
        p_refs, r_refs = refs[:n], refs[n:2 * n]
        s_sems, r_sems = refs[2 * n], refs[2 * n + 1]
        for cp in _scatter_copies(p_refs, r_refs, s_sems, r_sems):
            cp.wait_send()
            cp.wait_recv()

    res = pl.pallas_call(
        body, name=name, in_specs=[HBM] * (2 * n) + [SEM, SEM, ANY], out_specs=[HBM] * (2 * n),
        out_shape=[pltpu.HBM(t.shape, t.dtype) for t in list(ps) + list(lands)],
        input_output_aliases={i: i for i in range(2 * n)},
        compiler_params=pltpu.CompilerParams(has_side_effects=DATAFLOW),
    )(*ps, *lands, send_sems, recv_sems, after)
    return res[n:]


def _reduce_own(g, b, rcv, acc, l, name):
    nchip, r, cols = g.shape
    rh = r // 2
    tr = _half_tile(rh, cols)
    nt = rh // tr

    def body(*refs):
        g_ref, b_ref, r_ref = refs[:3]
        o_ref = refs[-1]
        total = g_ref[...] + b_ref[...]
        for j in range(3):
            total = total + r_ref[j].astype(F32)
        o_ref[...] = total

    mine = lambda i: lax.axis_index("c") * nt + i
    in_specs = [pl.BlockSpec((None, tr, cols), lambda i: (_chip_index(), mine(i), 0)),
                pl.BlockSpec((None, tr, cols), lambda i: (_chip_index(), i, 0)),
                pl.BlockSpec((3, tr, cols), lambda i: (0, i, 0))]
    args = [g, b, rcv]
    aliases = {}
    if acc is not None:
        in_specs.append(ANY)
        args.append(acc)
        aliases = {3: 0}
    return pl.pallas_call(
        body, name=name, grid=(nt,), in_specs=in_specs,
        out_specs=pl.BlockSpec((None, tr, cols), lambda i: (l, mine(i), 0)),
        out_shape=jax.ShapeDtypeStruct((DEPTH, r, cols), F32), input_output_aliases=aliases,
        compiler_params=_params(("parallel",)),
    )(*args)


def _share_halves(fs, l):
    n = len(fs)

    def body(*refs):
        o_refs = refs[n:2 * n]
        send_sems, recv_sems = refs[2 * n:]
        x, y, c, _, _ = _place()

        def halves(i):
            rh = fs[i].shape[1] // 2
            return o_refs[i].at[l, pl.ds(c * rh, rh), :], o_refs[i].at[l, pl.ds((1 - c) * rh, rh), :]

        def copy(i, rows):
            return pltpu.make_async_remote_copy(src_ref=rows, dst_ref=rows, send_sem=send_sems.at[i],
                                                recv_sem=recv_sems.at[i], device_id=(x, y, 1 - c),
                                                device_id_type=MESH_IDS)

        for i in range(n):
            copy(i, halves(i)[0]).start()
        for i in range(n):
            mine, theirs = halves(i)
            copy(i, mine).wait_send()
            copy(i, theirs).wait_recv()

    return pl.pallas_call(
        body, name="grad_share_halves", in_specs=[ANY] * n, out_specs=[ANY] * n,
        out_shape=[jax.ShapeDtypeStruct(t.shape, t.dtype) for t in fs],
        input_output_aliases={i: i for i in range(n)},
        scratch_shapes=[pltpu.SemaphoreType.DMA((n,)), pltpu.SemaphoreType.DMA((n,))],
        compiler_params=pltpu.CompilerParams(has_side_effects=True),
    )(*fs)


def _reduce_scatter_begin(slices, l, tag):
    names = list(slices)
    gs = [slices[k] for k in names]
    bs = _swap_halves(gs)
    ps = [_chip_partial(g, b, "grad_chip_partial_" + k) for k, g, b in zip(names, gs, bs)]
    send_sems, recv_sems, ps, lands, token = _scatter_start(ps, "grad_scatter_start_%s%d" % (tag, l))
    return dict(names=names, gs=gs, bs=bs, ps=ps, lands=lands, sems=(send_sems, recv_sems), token=token, l=l,
                tag=tag)


def _reduce_scatter_end(st, stacked, after):
    l = st['l']
    rs = _scatter_wait(st['ps'], st['lands'], *st['sems'], after, "grad_scatter_wait_%s%d" % (st['tag'], l))
    fs = [_reduce_own(g, b, r, stacked.get(k), l, "grad_reduce_own_" + k)
          for k, g, b, r in zip(st['names'], st['gs'], st['bs'], rs)]
    stacked.update(zip(st['names'], _share_halves(fs, l)))


def _all_reduce_small(packed, after=()):
    rows = packed.shape[0]
    n_dev = 2 * N_CHIPS
    tr = _pick(rows, 512, 8)

    def body(x_ref, *rest):
        o_ref, buf, send_sems, recv_sems = rest[len(after):]
        x, y, c, _, _ = _place()
        me = 4 * x + 2 * y + c
        copies = []
        for px in range(2):
            for py in range(2):
                for pc in range(2):
                    peer = 4 * px + 2 * py + pc
                    cp = pltpu.make_async_remote_copy(src_ref=x_ref, dst_ref=buf.at[me], send_sem=send_sems.at[peer],
                                                      recv_sem=recv_sems.at[me], device_id=(px, py, pc),
                                                      device_id_type=MESH_IDS)

                    @pl.when(peer != me)
                    def _(cp=cp):
                        cp.start()
                    copies.append((peer, cp))
        buf[me] = x_ref[...]
        for peer, cp in copies:
            @pl.when(peer != me)
            def _(cp=cp, peer=peer):
                cp.wait_send()
                pltpu.make_async_remote_copy(src_ref=x_ref, dst_ref=buf.at[peer], send_sem=send_sems.at[peer],
                                             recv_sem=recv_sems.at[peer], device_id=(x, y, c),
                                             device_id_type=MESH_IDS).wait_recv()

        @pl.loop(0, rows // tr)
        def _(t):
            sl = pl.ds(pl.multiple_of(t * tr, 8), tr)
            acc = buf[0, sl, :]
            for s in range(1, n_dev):
                acc = acc + buf[s, sl, :]
            o_ref[sl, :] = acc

    vm = pl.BlockSpec(memory_space=pltpu.VMEM)
    return pl.pallas_call(
        body, name="all_reduce_small", in_specs=[vm] + [ANY] * len(after), out_specs=vm,
        out_shape=jax.ShapeDtypeStruct(packed.shape, F32),
        scratch_shapes=[pltpu.VMEM((n_dev, rows, LANES), F32), pltpu.SemaphoreType.DMA((n_dev,)),
                        pltpu.SemaphoreType.DMA((n_dev,))],
        compiler_params=pltpu.CompilerParams(vmem_limit_bytes=VMEM_LIMIT, has_side_effects=True),
    )(packed, *after)


_WEIGHTS = ('norm_mix_g', 'w_in', 'qkv_conv_w', 'a_log', 'dt_bias', 'gdn_norm_g', 'w_branch_a', 'sgu_ln_g',
            'sgu_ln_b', 'sgu_w', 'sgu_b', 'w_branch_b', 'w_out', 'norm_ffn_g', 'w_up', 'ffn_conv_w', 'ffn_conv_b',
            'w_down', 'final_norm_g')


def _local_step(x, target, weights_of, final_norm_g, bwd_after=None, mid_layer=None, after_layer=None):
    lws, saves = [], []
    for l in range(DEPTH):
        lw = weights_of(l, x)
        x, s = _layer_fwd(x, lw)
        lws.append(lw)
        saves.append(s)
    loss, dx, dx_bf, d_final = _loss_head(x, final_norm_g.reshape(1, -1), target)
    grads = [None] * DEPTH
    for l in reversed(range(DEPTH)):
        after = bwd_after(l) if bwd_after is not None else ()
        mid = (lambda g, dh2, l=l: mid_layer(l, g, dh2)) if mid_layer is not None else None
        dx, dx_bf, grads[l] = _layer_bwd(dx, dx_bf, lws[l], saves[l], after, mid)
        if after_layer is not None:
            after_layer(l, grads[l], dx)
    return loss, dx, grads, d_final


def _pack_small(grads, d_final):
    parts = [grads[l][k].reshape(-1) for l in range(DEPTH) for k in _SMALL_GRADS] + [d_final.reshape(-1)]
    flat = jnp.concatenate(parts)
    rows = -(-flat.size // (8 * LANES)) * 8
    return jnp.pad(flat, (0, rows * LANES - flat.size)).reshape(rows, LANES), [p.size for p in parts]


def _unpack_small(packed, grads, d_final):
    flat = packed.reshape(-1)
    out, off = [], 0
    for l in range(DEPTH):
        d = {}
        for k in _SMALL_GRADS:
            t = grads[l][k]
            d[k] = flat[off:off + t.size].reshape(t.shape)
            off += t.size
        out.append(d)
    return out, flat[off:off + d_final.size].reshape(d_final.shape)


def kernel(x, norm_mix_g, w_in, qkv_conv_w, a_log, dt_bias, gdn_norm_g, w_branch_a, sgu_ln_g, sgu_ln_b, sgu_w, sgu_b, w_branch_b, w_out, norm_ffn_g, w_up, ffn_conv_w, ffn_conv_b, w_down, final_norm_g, loss_target, m_norm_mix_g, m_w_in, m_qkv_conv_w, m_a_log, m_dt_bias, m_gdn_norm_g, m_w_branch_a, m_sgu_ln_g, m_sgu_ln_b, m_sgu_w, m_sgu_b, m_w_branch_b, m_w_out, m_norm_ffn_g, m_w_up, m_ffn_conv_w, m_ffn_conv_b, m_w_down, m_final_norm_g, v_norm_mix_g, v_w_in, v_qkv_conv_w, v_a_log, v_dt_bias, v_gdn_norm_g, v_w_branch_a, v_sgu_ln_g, v_sgu_ln_b, v_sgu_w, v_sgu_b, v_w_branch_b, v_w_out, v_norm_ffn_g, v_w_up, v_ffn_conv_w, v_ffn_conv_b, v_w_down, v_final_norm_g):
    w = dict(norm_mix_g=norm_mix_g, w_in=w_in, qkv_conv_w=qkv_conv_w, a_log=a_log, dt_bias=dt_bias,
             gdn_norm_g=gdn_norm_g, w_branch_a=w_branch_a, sgu_ln_g=sgu_ln_g, sgu_ln_b=sgu_ln_b, sgu_w=sgu_w,
             sgu_b=sgu_b, w_branch_b=w_branch_b, w_out=w_out, norm_ffn_g=norm_ffn_g, w_up=w_up,
             ffn_conv_w=ffn_conv_w, ffn_conv_b=ffn_conv_b, w_down=w_down, final_norm_g=final_norm_g)
    m = dict(norm_mix_g=m_norm_mix_g, w_in=m_w_in, qkv_conv_w=m_qkv_conv_w, a_log=m_a_log, dt_bias=m_dt_bias,
             gdn_norm_g=m_gdn_norm_g, w_branch_a=m_w_branch_a, sgu_ln_g=m_sgu_ln_g, sgu_ln_b=m_sgu_ln_b,
             sgu_w=m_sgu_w, sgu_b=m_sgu_b, w_branch_b=m_w_branch_b, w_out=m_w_out, norm_ffn_g=m_norm_ffn_g,
             w_up=m_w_up, ffn_conv_w=m_ffn_conv_w, ffn_conv_b=m_ffn_conv_b, w_down=m_w_down,
             final_norm_g=m_final_norm_g)
    v = dict(norm_mix_g=v_norm_mix_g, w_in=v_w_in, qkv_conv_w=v_qkv_conv_w, a_log=v_a_log, dt_bias=v_dt_bias,
             gdn_norm_g=v_gdn_norm_g, w_branch_a=v_w_branch_a, sgu_ln_g=v_sgu_ln_g, sgu_ln_b=v_sgu_ln_b,
             sgu_w=v_sgu_w, sgu_b=v_sgu_b, w_branch_b=v_w_branch_b, w_out=v_w_out, norm_ffn_g=v_norm_ffn_g,
             w_up=v_w_up, ffn_conv_w=v_ffn_conv_w, ffn_conv_b=v_ffn_conv_b, w_down=v_w_down,
             final_norm_g=v_final_norm_g)
    chip = _chip_index()

    gathered_names = _BIG + ('qkv_conv_w', 'ffn_conv_w')
    in_flight = []
    for l in range(DEPTH):
        placed = [_place_shard(w[k], l, BF16 if k in _BIG else F32, "place_" + k) for k in gathered_names]
        earlier = (in_flight[-1][3],) if in_flight else ()
        in_flight.append(_gather_start(placed, "gather_start_%d" % l, earlier))

    def weights_of(l, x_in):
        send_sems, recv_sems, bufs, _ = in_flight[l]
        after = in_flight[-1][3] if l == 0 else x_in
        bufs = _gather_forward(_gather_wait(bufs, send_sems, recv_sems, after, "gather_wait_%d" % l))
        return _layer_weights(l, dict(zip(gathered_names, bufs)), w)

    group_a, group_b = ('w_up', 'w_down'), ('w_in', 'w_branch_a', 'w_branch_b', 'w_out')
    big_g = {}
    pending = {'a': None, 'b': None}

    def finish(tag, after):
        if pending[tag] is not None:
            _reduce_scatter_end(pending[tag], big_g, after)
            pending[tag] = None

    def bwd_after(l):
        return () if pending['b'] is None else (pending['b']['token'],)

    def mid_layer(l, g, dh2):
        finish('b', dh2)
        pending['a'] = _reduce_scatter_begin(_big_grad_slices(g, group_a), l, 'a')
        return (pending['a']['token'],)

    def after_layer(l, g, dx):
        finish('a', dx)
        pending['b'] = _reduce_scatter_begin(_big_grad_slices(g, group_b), l, 'b')

    loss, grad_x, grads, d_final = _local_step(x[0], loss_target[0], weights_of, w['final_norm_g'], bwd_after,
                                               mid_layer, after_layer)
    loss = lax.psum(loss[0, 0], ("x", "y", "c"))
    packed, _ = _pack_small(grads, d_final)
    reduced_small = _all_reduce_small(packed, (pending['b']['token'],))
    small_g, d_final = _unpack_small(reduced_small, grads, d_final)
    finish('b', reduced_small)

    def stack(k):
        return jnp.stack([small_g[l][k] for l in range(DEPTH)])

    nd = 2 * GDN_HEADS
    g_out = {k: big_g[k].reshape(w[k].shape) for k in _BIG}
    for k in ('norm_mix_g', 'gdn_norm_g', 'sgu_ln_g', 'sgu_ln_b', 'norm_ffn_g', 'ffn_conv_b'):
        g_out[k] = stack(k).reshape(w[k].shape)
    g_out['sgu_w'] = stack('sgu_w')
    g_out['a_log'] = stack('a_log')[:, 0, :nd].reshape(w['a_log'].shape)
    g_out['dt_bias'] = stack('dt_bias')[:, 0, :nd].reshape(w['dt_bias'].shape)
    g_out['sgu_b'] = jnp.swapaxes(stack('sgu_bt')[:, :, :SGU_GROUPS], 1, 2)
    for k in ('qkv_conv_w', 'ffn_conv_w'):
        full = stack(k)
        width = w[k].shape[-1]
        g_out[k] = lax.dynamic_slice_in_dim(full, chip * width, width, axis=2)
    g_out['final_norm_g'] = d_final.reshape(w['final_norm_g'].shape)

    deltas, new_m, new_v = {}, {}, {}
    for k in _WEIGHTS:
        deltas[k], new_m[k], new_v[k] = _adamw(w[k], g_out[k], m[k], v[k], "adamw_" + k)
    return (loss, grad_x[None], *[g_out[k] for k in _WEIGHTS], *[deltas[k] for k in _WEIGHTS],
            *[new_m[k] for k in _WEIGHTS], *[new_v[k] for k in _WEIGHTS])
```

```python
import functools
import math

import jax
import jax.numpy as jnp
from jax import lax
from jax.experimental import pallas as pl
from jax.experimental.pallas import tpu as pltpu

F32 = jnp.float32
BF16 = jnp.bfloat16
PARTIAL = BF16

D_MODEL = 2048
SEQ = 2048
DEPTH = 4
GDN_HEADS = 8
HEAD_DIM = 128
GDN_CHUNK = 64
QKV_CONV = 5
SGU_GROUPS = 8
SGU_DIM = 128
SGU_BLOCK = 128
D_FF = 5632
FFN_CONV = 3
NORM_EPS = 1e-6
N_CHIPS = 4

ADAM_LR = 0.001
ADAM_B1 = 0.9
ADAM_B2 = 0.999
ADAM_EPS = 1e-08
ADAM_WD = 0.01
ADAM_STEP = 10

LANES = 128
VMEM_LIMIT = 56 * 1024 * 1024
AB_PAD = LANES


def _gw():
    return GDN_HEADS * HEAD_DIM


def _sw():
    return SGU_GROUPS * SGU_DIM


def _n_all():
    return 4 * _gw() + 2 * _sw() + 2 * D_MODEL + AB_PAD


def _pick(n, target, align=LANES):
    if n <= target:
        return n
    best = None
    t = align
    while t <= target:
        if n % t == 0:
            best = t
        t += align
    assert best is not None, (n, target, align)
    return best


def _params(sem=None):
    return pltpu.CompilerParams(dimension_semantics=sem, vmem_limit_bytes=VMEM_LIMIT)


def _dot(a, b, dims, hi):
    dn = (dims, ((), ()))
    if not hi:
        return lax.dot_general(a.astype(BF16), b.astype(BF16), dn, preferred_element_type=F32)
    a_hi, b_hi = a.astype(BF16), b.astype(BF16)
    a_lo = (a - a_hi.astype(F32)).astype(BF16)
    b_lo = (b - b_hi.astype(F32)).astype(BF16)
    d = lambda p, q: lax.dot_general(p, q, dn, preferred_element_type=F32)
    return d(a_hi, b_hi) + (d(a_hi, b_lo) + d(a_lo, b_hi))


def _make_mm(hi):
    @jax.custom_vjp
    def mm(a, b):
        return _dot(a, b, ((1,), (0,)), hi)

    @jax.custom_vjp
    def mm_nt(a, b):
        return _dot(a, b, ((1,), (1,)), hi)

    @jax.custom_vjp
    def mm_tn(a, b):
        return _dot(a, b, ((0,), (0,)), hi)

    mm.defvjp(lambda a, b: (mm(a, b), (a, b)), lambda r, g: (mm_nt(g, r[1]), mm_tn(r[0], g)))
    mm_nt.defvjp(lambda a, b: (mm_nt(a, b), (a, b)), lambda r, g: (mm(g, r[1]), mm_tn(g, r[0])))
    mm_tn.defvjp(lambda a, b: (mm_tn(a, b), (a, b)), lambda r, g: (mm_nt(r[1], g), mm(r[0], g)))
    return mm, mm_nt, mm_tn


_mm, _mm_nt, _mm_tn = _make_mm(False)
_mmh, _mmh_nt, _mmh_tn = _make_mm(True)


def _shift_rows_raw(x, s):
    if s == 0:
        return x
    n = x.shape[0]
    rolled = pltpu.roll(x, (-s) % n, 0)
    t = lax.broadcasted_iota(jnp.int32, x.shape, 0)
    ok = (t + s >= 0) & (t + s < n)
    return jnp.where(ok, rolled, 0.0)


@functools.partial(jax.custom_vjp, nondiff_argnums=(1,))
def _shift_rows(x, s):
    return _shift_rows_raw(x, s)


_shift_rows.defvjp(lambda x, s: (_shift_rows_raw(x, s), None), lambda s, _, g: (_shift_rows_raw(g, -s),))


def _sigmoid(x):
    return 1.0 / (1.0 + jnp.exp(-x))


def _silu(x):
    return x * _sigmoid(x)


def _gelu(x):
    return 0.5 * x * (1.0 + jnp.tanh(math.sqrt(2.0 / math.pi) * (x + 0.044715 * x * x * x)))


def _softplus(x):
    return jnp.maximum(x, 0.0) + jnp.log(1.0 + jnp.exp(-jnp.abs(x)))


def _rms(x, g):
    return x * lax.rsqrt(jnp.mean(x * x, axis=-1, keepdims=True) + NORM_EPS) * g


def _conv_rows(x, taps):
    pad = len(taps) // 2
    acc = None
    for j, w in enumerate(taps):
        term = _shift_rows(x, j - pad) * w
        acc = term if acc is None else acc + term
    return acc


@jax.custom_vjp
def _inv_unit(mats):
    n = mats[0].shape[0]
    eye = (lax.broadcasted_iota(jnp.int32, (n, n), 0) == lax.broadcasted_iota(jnp.int32, (n, n), 1)).astype(F32)
    ps = [eye - a for a in mats]
    aks = list(mats)
    for _ in range(int(math.log2(n)) - 1):
        aks = [_mmh(ak, ak) for ak in aks]
        ps = [p + _mmh(p, ak) for p, ak in zip(ps, aks)]
    return ps


def _inv_unit_fwd(mats):
    ts = _inv_unit(mats)
    return ts, ts


def _inv_unit_bwd(ts, gs):
    inner = [_mmh_nt(g, t) for g, t in zip(gs, ts)]
    return ([-_mmh_tn(t, m) for t, m in zip(ts, inner)],)


_inv_unit.defvjp(_inv_unit_fwd, _inv_unit_bwd)


def _tiled(fn, name, grid, ins, outs, sem=None):
    n_in = len(ins)

    def body(*refs):
        vals = [r[...] for r in refs[:n_in]]
        res = fn(*vals)
        if not isinstance(res, (tuple, list)):
            res = (res,)
        for o_ref, r, spec in zip(refs[n_in:], res, outs):
            acc = spec[4]
            if not acc:
                o_ref[...] = r.astype(o_ref.dtype)
            else:
                first = functools.reduce(jnp.logical_and, [pl.program_id(a) == 0 for a in acc])

                @pl.when(first)
                def _(o_ref=o_ref, r=r):
                    o_ref[...] = r.astype(o_ref.dtype)

                @pl.when(jnp.logical_not(first))
                def _(o_ref=o_ref, r=r):
                    o_ref[...] += r.astype(o_ref.dtype)

    return pl.pallas_call(
        body, name=name, grid=grid,
        in_specs=[pl.BlockSpec(b, m) for _, b, m in ins],
        out_specs=[pl.BlockSpec(s[2], s[3]) for s in outs],
        out_shape=[jax.ShapeDtypeStruct(s[0], s[1]) for s in outs],
        compiler_params=_params(sem),
    )(*[a for a, _, _ in ins])


def _matmul(a, b, mode, out_dtype, name, add=None, a_split=1, b_split=1, o_split=1, after=()):
    def dims2(x, split):
        return (x.shape[-2], x.shape[-1] * split)

    ar, ac = dims2(a, a_split)
    br, bc = dims2(b, b_split)
    if mode == 'nn':
        M, K, N = ar, ac, bc
        assert br == K
    elif mode == 'nt':
        M, K, N = ar, ac, br
        assert bc == K
    else:
        K, M, N = ar, ac, bc
        assert br == K
    tm = _pick(M // (a_split if mode == 'tn' else 1), 512)
    tn = _pick(N // max(o_split, b_split if mode != 'nt' else 1), 1408)
    tk = _pick(K // max(a_split if mode != 'tn' else 1, b_split if mode == 'nt' else 1), 2048)
    gm, gn, gk = M // tm, N // tn, K // tk

    def col_map(split, total_cols, tile):
        per = total_cols // split // tile

        def f(r, c):
            return (c // per, r, c % per) if split > 1 else (r, c)
        return f

    if mode == 'nn':
        a_idx = col_map(a_split, K, tk)
        b_idx = col_map(b_split, N, tn)
        a_spec = pl.BlockSpec(((None,) if a_split > 1 else ()) + (tm, tk), lambda m, n, k: a_idx(m, k))
        b_spec = pl.BlockSpec(((None,) if b_split > 1 else ()) + (tk, tn), lambda m, n, k: b_idx(k, n))
        dn = ((1,), (0,))
    elif mode == 'nt':
        a_idx = col_map(a_split, K, tk)
        b_idx = col_map(b_split, K, tk)
        a_spec = pl.BlockSpec(((None,) if a_split > 1 else ()) + (tm, tk), lambda m, n, k: a_idx(m, k))
        b_spec = pl.BlockSpec(((None,) if b_split > 1 else ()) + (tn, tk), lambda m, n, k: b_idx(n, k))
        dn = ((1,), (1,))
    else:
        a_idx = col_map(a_split, M, tm)
        b_idx = col_map(b_split, N, tn)
        a_spec = pl.BlockSpec(((None,) if a_split > 1 else ()) + (tk, tm), lambda m, n, k: a_idx(k, m))
        b_spec = pl.BlockSpec(((None,) if b_split > 1 else ()) + (tk, tn), lambda m, n, k: b_idx(k, n))
        dn = ((0,), (0,))
    o_idx = col_map(o_split, N, tn)
    o_block = ((None,) if o_split > 1 else ()) + (tm, tn)
    o_spec = pl.BlockSpec(o_block, lambda m, n, k: o_idx(m, n))
    o_shape = ((o_split, M, N // o_split) if o_split > 1 else (M, N))
    in_specs = [a_spec, b_spec]
    args = [a, b]
    if add is not None:
        in_specs.append(pl.BlockSpec((tm, tn), lambda m, n, k: (m, n)))
        args.append(add)
    n_lead = len(args)
    for t in after:
        in_specs.append(pl.BlockSpec(memory_space=pl.ANY))
        args.append(t)

    def body(*refs):
        a_ref, b_ref = refs[0], refs[1]
        add_ref = refs[2] if add is not None else None
        o_ref = refs[n_lead + len(after)]
        part = lax.dot_general(a_ref[...].astype(BF16), b_ref[...].astype(BF16), (dn, ((), ())),
                               preferred_element_type=F32)

        def finish(total):
            if add_ref is not None:
                total = total + add_ref[...]
            o_ref[...] = total.astype(o_ref.dtype)

        if gk == 1:
            finish(part)
        else:
            acc_ref = refs[-1]
            k = pl.program_id(2)

            @pl.when(k == 0)
            def _():
                acc_ref[...] = part

            @pl.when(jnp.logical_and(k > 0, k < gk - 1))
            def _():
                acc_ref[...] += part

            @pl.when(k == gk - 1)
            def _():
                finish(acc_ref[...] + part)

    return pl.pallas_call(
        body, name=name, grid=(gm, gn, gk), in_specs=in_specs, out_specs=o_spec,
        out_shape=jax.ShapeDtypeStruct(o_shape, out_dtype),
        scratch_shapes=([pltpu.VMEM((tm, tn), F32)] if gk > 1 else []),
        compiler_params=_params(("parallel", "parallel", "arbitrary")),
    )(*args)


def _row_tile():
    return _pick(SEQ, 256, 8)


def _rms_fwd(x, g, name):
    tm = _row_tile()
    (h,) = _tiled(lambda xv, gv: _rms(xv, gv), name, (SEQ // tm,),
                  [(x, (tm, D_MODEL), lambda i: (i, 0)), (g, (1, D_MODEL), lambda i: (0, 0))],
                  [((SEQ, D_MODEL), BF16, (tm, D_MODEL), lambda i: (i, 0), ())])
    return h


def _rms_bwd(x, g, dh, dres, name):
    tm = _row_tile()

    def fn(xv, gv, dhv, drv):
        _, vjp = jax.vjp(_rms, xv, gv)
        dx, dg = vjp(dhv)
        dx = dx + drv
        return dx, dx, dg

    row = lambda i: (i, 0)
    return _tiled(fn, name, (SEQ // tm,),
                  [(x, (tm, D_MODEL), row), (g, (1, D_MODEL), lambda i: (0, 0)),
                   (dh, (tm, D_MODEL), row), (dres, (tm, D_MODEL), row)],
                  [((SEQ, D_MODEL), F32, (tm, D_MODEL), row, ()),
                   ((SEQ, D_MODEL), BF16, (tm, D_MODEL), row, ()),
                   ((1, D_MODEL), F32, (1, D_MODEL), lambda i: (0, 0), (0,))])


def _loss_head(x, g, target):
    tm = _row_tile()

    def fn(xv, gv, tv):
        y, vjp = jax.vjp(_rms, xv, gv)
        err = y - tv
        part = 0.5 * jnp.sum(jnp.sum(err * err, axis=1, keepdims=True), axis=0, keepdims=True) / D_MODEL
        dx, dg = vjp(err / D_MODEL)
        return part, dx, dx, dg

    row = lambda i: (i, 0)
    one = lambda i: (0, 0)
    return _tiled(fn, "loss_head", (SEQ // tm,),
                  [(x, (tm, D_MODEL), row), (g, (1, D_MODEL), one), (target, (tm, D_MODEL), row)],
                  [((1, 1), F32, (1, 1), one, (0,)),
                   ((SEQ, D_MODEL), F32, (tm, D_MODEL), row, ()),
                   ((SEQ, D_MODEL), BF16, (tm, D_MODEL), row, ()),
                   ((1, D_MODEL), F32, (1, D_MODEL), one, (0,))])


def _p_col(width, index):
    return lambda tm: ((tm, width), lambda i: (i, index))


def _merge_fn(ga, gb, ta, tb):
    return _sigmoid(ga) * ta + _sigmoid(gb) * tb


def _merge_fwd(p, ta, tb):
    tm = _row_tile()
    d = D_MODEL
    ga_blk = 4 * _gw() + 2 * _sw()
    assert ga_blk % d == 0
    ia, ib = ga_blk // d, ga_blk // d + 1
    row = lambda i: (i, 0)
    (m,) = _tiled(_merge_fn, "merge_fwd", (SEQ // tm,),
                  [(p, (tm, d), lambda i: (i, ia)), (p, (tm, d), lambda i: (i, ib)),
                   (ta, (tm, d), row), (tb, (tm, d), row)],
                  [((SEQ, d), BF16, (tm, d), row, ())])
    return m


def _merge_bwd(p, ta, tb, dm):
    tm = _row_tile()
    d = D_MODEL
    ga_blk = 4 * _gw() + 2 * _sw()
    ia, ib = ga_blk // d, ga_blk // d + 1

    def fn(ga, gb, tav, tbv, dmv):
        _, vjp = jax.vjp(_merge_fn, ga, gb, tav, tbv)
        return vjp(dmv)

    row = lambda i: (i, 0)
    out = ((SEQ, d), BF16, (tm, d), row, ())
    return _tiled(fn, "merge_bwd", (SEQ // tm,),
                  [(p, (tm, d), lambda i: (i, ia)), (p, (tm, d), lambda i: (i, ib)),
                   (ta, (tm, d), row), (tb, (tm, d), row), (dm, (tm, d), row)],
                  [out, out, out, out])


def _gate_fn(ab, alog, dtb):
    lane = lax.broadcasted_iota(jnp.int32, ab.shape, 1)
    nd = 2 * GDN_HEADS
    g = -jnp.exp(alog) * _softplus(ab + dtb)
    beta = _sigmoid(ab)
    return jnp.where(lane < nd, g, jnp.where(lane < 2 * nd, beta, 0.0))


def _ab_index():
    off = 4 * _gw() + 2 * _sw() + 2 * D_MODEL
    assert off % AB_PAD == 0
    return off // AB_PAD


def _gate_fwd(p, alog, dtb):
    tm = _row_tile()
    iab = _ab_index()
    one = lambda i: (0, 0)
    (g,) = _tiled(_gate_fn, "gdn_gate_fwd", (SEQ // tm,),
                  [(p, (tm, AB_PAD), lambda i: (i, iab)), (alog, (1, AB_PAD), one), (dtb, (1, AB_PAD), one)],
                  [((SEQ, AB_PAD), F32, (tm, AB_PAD), lambda i: (i, 0), ())])
    return g


def _gate_bwd(p, alog, dtb, dg):
    tm = _row_tile()
    iab = _ab_index()
    one = lambda i: (0, 0)

    def fn(ab, al, db, dgv):
        _, vjp = jax.vjp(_gate_fn, ab, al, db)
        return vjp(dgv)

    return _tiled(fn, "gdn_gate_bwd", (SEQ // tm,),
                  [(p, (tm, AB_PAD), lambda i: (i, iab)), (alog, (1, AB_PAD), one), (dtb, (1, AB_PAD), one),
                   (dg, (tm, AB_PAD), lambda i: (i, 0))],
                  [((SEQ, AB_PAD), BF16, (tm, AB_PAD), lambda i: (i, 0), ()),
                   ((1, AB_PAD), F32, (1, AB_PAD), one, (0,)),
                   ((1, AB_PAD), F32, (1, AB_PAD), one, (0,))])


def _l2n(x):
    return x * lax.rsqrt(jnp.sum(x * x, axis=-1, keepdims=True) + NORM_EPS)


def _qkv_fn(xq, xk, xv, *taps):
    n = QKV_CONV
    q = _l2n(_silu(_conv_rows(xq, taps[0:n])))
    k = _l2n(_silu(_conv_rows(xk, taps[n:2 * n])))
    v = _silu(_conv_rows(xv, taps[2 * n:3 * n]))
    return q, k, v


def _qkv_specs(p, conv_w):
    hd, nh = HEAD_DIM, GDN_HEADS
    ins = [(p, (SEQ, hd), (lambda h, s=s: (0, s * nh + h))) for s in range(3)]
    ins += [(conv_w, (QKV_CONV, hd), (lambda h, s=s: (0, s * nh + h))) for s in range(3)]
    return ins


def _qkv_fwd(p, conv_w):
    hd, nh = HEAD_DIM, GDN_HEADS
    ins = _qkv_specs(p, conv_w)
    n_in = len(ins)
    out_spec = pl.BlockSpec((SEQ, hd), lambda h: (0, h))

    def body(*refs):
        xs = [r[...] for r in refs[:3]]
        taps = [refs[3 + s][j:j + 1, :] for s in range(3) for j in range(QKV_CONV)]
        q, k, v = _qkv_fn(*xs, *taps)
        refs[n_in][...] = q
        refs[n_in + 1][...] = k
        refs[n_in + 2][...] = v

    return pl.pallas_call(
        body, name="gdn_qkv_fwd", grid=(nh,),
        in_specs=[pl.BlockSpec(b, m) for _, b, m in ins], out_specs=[out_spec] * 3,
        out_shape=[jax.ShapeDtypeStruct((SEQ, nh * hd), F32)] * 3,
        compiler_params=_params(("parallel",)),
    )(*[a for a, _, _ in ins])


def _qkv_bwd(p, conv_w, dq, dk, dv):
    hd, nh = HEAD_DIM, GDN_HEADS
    ins = _qkv_specs(p, conv_w) + [(t, (SEQ, hd), lambda h: (0, h)) for t in (dq, dk, dv)]
    n_in = len(ins)

    def body(*refs):
        xs = [r[...] for r in refs[:3]]
        taps = [refs[3 + s][j:j + 1, :] for s in range(3) for j in range(QKV_CONV)]
        cts = tuple(r[...] for r in refs[6:9])
        _, vjp = jax.vjp(_qkv_fn, *xs, *taps)
        grads = vjp(cts)
        for s in range(3):
            refs[n_in + s][...] = grads[s].astype(BF16)
            for j in range(QKV_CONV):
                refs[n_in + 3 + s][j:j + 1, :] = jnp.sum(grads[3 + s * QKV_CONV + j], axis=0, keepdims=True)

    dx_spec = pl.BlockSpec((SEQ, hd), lambda h: (0, h))
    dw_spec = pl.BlockSpec((QKV_CONV, hd), lambda h: (0, h))
    outs = pl.pallas_call(
        body, name="gdn_qkv_bwd", grid=(nh,),
        in_specs=[pl.BlockSpec(b, m) for _, b, m in ins], out_specs=[dx_spec] * 3 + [dw_spec] * 3,
        out_shape=[jax.ShapeDtypeStruct((SEQ, nh * hd), BF16)] * 3
        + [jax.ShapeDtypeStruct((QKV_CONV, nh * hd), F32)] * 3,
        compiler_params=_params(("parallel",)),
    )(*[a for a, _, _ in ins])
    return outs[0:3], jnp.concatenate(outs[3:6], axis=1)


def _prep_fn(qs, ks, vs, gblk):
    nh = len(qs)
    c = qs[0].shape[0]
    scale = HEAD_DIM ** -0.5
    ii = lax.broadcasted_iota(jnp.int32, (c, c), 0)
    jj = lax.broadcasted_iota(jnp.int32, (c, c), 1)
    eye = (ii == jj).astype(F32)
    lane = lax.broadcasted_iota(jnp.int32, gblk.shape, 1)
    kk_t = [_mm_nt(k, k) for k in ks]
    qk_t = [_mm_nt(q, k) for q, k in zip(qs, ks)]
    chains = [(h, d) for h in range(nh) for d in range(2)]
    pre = []
    for h, d in chains:
        g = jnp.sum(jnp.where(lane == h + d * nh, gblk, 0.0), axis=1, keepdims=True)
        beta = jnp.sum(jnp.where(lane == h + (2 + d) * nh, gblk, 0.0), axis=1, keepdims=True)
        incl = (jj <= ii) if d == 0 else (jj >= ii)
        strict = (jj < ii) if d == 0 else (jj > ii)
        incl_t = (ii <= jj) if d == 0 else (ii >= jj)
        g_row = jnp.sum(eye * g, axis=0, keepdims=True)
        gc = jnp.sum(jnp.where(incl, g_row, 0.0), axis=1, keepdims=True)
        gc_row = jnp.sum(jnp.where(incl_t, g, 0.0), axis=0, keepdims=True)
        decay = jnp.where(incl, jnp.exp(jnp.where(incl, gc - gc_row, 0.0)), 0.0)
        a = jnp.where(strict, kk_t[h] * beta * decay, 0.0)
        pre.append((g, beta, gc, decay, incl, a))
    ts = _inv_unit([p[5] for p in pre])
    us = [_mmh(t, vs[h] * p[1]) for (h, d), p, t in zip(chains, pre, ts)]
    ws = [_mmh(t, ks[h] * (p[1] * jnp.exp(p[2]))) for (h, d), p, t in zip(chains, pre, ts)]
    outs = [[None, None] for _ in range(nh)]
    for (h, d), (g, beta, gc, decay, incl, a), u, w in zip(chains, pre, us, ws):
        qk = jnp.where(incl, qk_t[h] * (scale * decay), 0.0)
        g_last = jnp.sum(g, axis=0, keepdims=True)
        q_dec = qs[h] * (scale * jnp.exp(gc))
        k_dec = ks[h] * jnp.exp(g_last - gc)
        outs[h][d] = (u, w, q_dec, k_dec, qk, jnp.exp(g_last))
    return outs


def _n_chunks():
    return SEQ // GDN_CHUNK


def _gdn_prep_fwd(q, k, v, gates):
    c, hd, nh, nc = GDN_CHUNK, HEAD_DIM, GDN_HEADS, _n_chunks()
    gw = nh * hd
    row = lambda n: (n, 0)

    def body(q_ref, k_ref, v_ref, g_ref, u_ref, w_ref, qd_ref, kd_ref, qk_ref, dec_ref):
        heads = [slice(h * hd, (h + 1) * hd) for h in range(nh)]
        res = _prep_fn([q_ref[:, s] for s in heads], [k_ref[:, s] for s in heads], [v_ref[:, s] for s in heads],
                       g_ref[...])
        for h, cols in enumerate(heads):
            for d in range(2):
                u, w, qd, kd, qk, dec = res[h][d]
                u_ref[d, :, cols] = u
                w_ref[d, :, cols] = w
                qd_ref[d, :, cols] = qd
                kd_ref[d, :, cols] = kd
                qk_ref[d, h] = qk
                dec_ref[d, h:h + 1, :] = jnp.broadcast_to(dec, (1, LANES))

    wide = pl.BlockSpec((2, c, gw), lambda n: (0, n, 0))
    return pl.pallas_call(
        body, name="gdn_prep_fwd", grid=(nc,),
        in_specs=[pl.BlockSpec((c, gw), row)] * 3 + [pl.BlockSpec((c, LANES), row)],
        out_specs=[wide] * 4 + [pl.BlockSpec((2, nh, c, c), lambda n: (0, 0, n, 0)),
                                pl.BlockSpec((2, None, nh, LANES), lambda n: (0, n, 0, 0))],
        out_shape=[jax.ShapeDtypeStruct((2, SEQ, gw), F32)] * 4
        + [jax.ShapeDtypeStruct((2, nh, SEQ, c), F32), jax.ShapeDtypeStruct((2, nc, nh, LANES), F32)],
        compiler_params=_params(("parallel",)),
    )(q, k, v, gates)


def _gdn_prep_bwd(q, k, v, gates, cts):
    c, hd, nh, nc = GDN_CHUNK, HEAD_DIM, GDN_HEADS, _n_chunks()
    gw = nh * hd
    row = lambda n: (n, 0)

    def body(*refs):
        q_ref, k_ref, v_ref, g_ref = refs[:4]
        ct_refs = refs[4:16]
        dq_ref, dk_ref, dv_ref, dg_ref = refs[16:20]
        heads = [slice(h * hd, (h + 1) * hd) for h in range(nh)]
        _, vjp = jax.vjp(_prep_fn, [q_ref[:, s] for s in heads], [k_ref[:, s] for s in heads],
                         [v_ref[:, s] for s in heads], g_ref[...])
        cts_in = []
        for h, cols in enumerate(heads):
            per_dir = []
            for d in range(2):
                r = ct_refs[6 * d:6 * d + 6]
                per_dir.append((r[0][:, cols], r[1][:, cols], r[2][:, cols], r[3][:, cols], r[4][h],
                                r[5][h:h + 1, 0:1]))
            cts_in.append(per_dir)
        dqs, dks, dvs, dg = vjp(cts_in)
        for h, cols in enumerate(heads):
            dq_ref[:, cols] = dqs[h]
            dk_ref[:, cols] = dks[h]
            dv_ref[:, cols] = dvs[h]
        dg_ref[...] = dg

    one_dir = [pl.BlockSpec((c, gw), row)] * 4 + [pl.BlockSpec((nh, c, c), lambda n: (0, n, 0)),
                                                  pl.BlockSpec((None, nh, LANES), lambda n: (n, 0, 0))]
    return pl.pallas_call(
        body, name="gdn_prep_bwd", grid=(nc,),
        in_specs=[pl.BlockSpec((c, gw), row)] * 3 + [pl.BlockSpec((c, LANES), row)] + one_dir * 2,
        out_specs=[pl.BlockSpec((c, gw), row)] * 3 + [pl.BlockSpec((c, LANES), row)],
        out_shape=[jax.ShapeDtypeStruct((SEQ, gw), F32)] * 3 + [jax.ShapeDtypeStruct((SEQ, LANES), F32)],
        compiler_params=_params(("parallel",)),
    )(q, k, v, gates, *cts)


def _scan_steps(states, us, ws, qds, kds, qks, decs):
    w_s = [_mm(w, s) for w, s in zip(ws, states)]
    q_s = [_mm(qd, s) for qd, s in zip(qds, states)]
    v_new = [u - x for u, x in zip(us, w_s)]
    outs = [x + _mm(qk, vn) for x, qk, vn in zip(q_s, qks, v_new)]
    k_v = [_mm_tn(kd, vn) for kd, vn in zip(kds, v_new)]
    new_states = [s * dec + x for s, dec, x in zip(states, decs, k_v)]
    return new_states, outs


def _scan_operands(ins, chains, dec_lanes):
    cols = lambda h: slice(h * HEAD_DIM, (h + 1) * HEAD_DIM)
    wide = [[ins[6 * d + j][:, cols(h)] for d, h in chains] for j in range(4)]
    qks = [ins[6 * d + 4][h] for d, h in chains]
    decs = [ins[6 * d + 5][h:h + 1, dec_lanes] for d, h in chains]
    return (*wide, qks, decs)


def _scan_in_specs(chunk_of):
    c, hd, nh = GDN_CHUNK, HEAD_DIM, GDN_HEADS
    gw = nh * hd
    specs = []
    for d in range(2):
        f = chunk_of[d]
        specs += [pl.BlockSpec((None, c, gw), lambda n, d=d, f=f: (d, f(n), 0))] * 4
        specs += [pl.BlockSpec((None, nh, c, c), lambda n, d=d, f=f: (d, 0, f(n), 0)),
                  pl.BlockSpec((None, None, nh, LANES), lambda n, d=d, f=f: (d, f(n), 0, 0))]
    return specs


def _gdn_scan_fwd(u, w, qd, kd, qk, dec):
    c, hd, nh, nc = GDN_CHUNK, HEAD_DIM, GDN_HEADS, _n_chunks()
    gw = nh * hd
    chunk_of = (lambda n: n, lambda n: nc - 1 - n)

    def body(*refs):
        ins = refs[:12]
        o_refs = refs[12:14]
        st_refs = refs[14:16]
        s_ref = refs[16]

        @pl.when(pl.program_id(0) == 0)
        def _():
            s_ref[...] = jnp.zeros_like(s_ref)

        chains = [(d, h) for d in range(2) for h in range(nh)]
        states = [s_ref[d * nh + h] for d, h in chains]
        for (d, h), st in zip(chains, states):
            st_refs[d][h] = st
        new_states, outs = _scan_steps(states, *_scan_operands(ins, chains, slice(None)))
        for (d, h), st, o in zip(chains, new_states, outs):
            s_ref[d * nh + h] = st
            o_refs[d][:, h * hd:(h + 1) * hd] = o

    o_specs = [pl.BlockSpec((c, gw), lambda n, f=f: (f(n), 0)) for f in chunk_of]
    st_specs = [pl.BlockSpec((None, nh, hd, hd), lambda n, f=f: (f(n), 0, 0, 0)) for f in chunk_of]
    return pl.pallas_call(
        body, name="gdn_scan_fwd", grid=(nc,), in_specs=_scan_in_specs(chunk_of),
        out_specs=o_specs + st_specs,
        out_shape=[jax.ShapeDtypeStruct((SEQ, gw), F32)] * 2 + [jax.ShapeDtypeStruct((nc, nh, hd, hd), F32)] * 2,
        scratch_shapes=[pltpu.VMEM((2 * nh, hd, hd), F32)],
        compiler_params=_params(("arbitrary",)),
    )(*([u, w, qd, kd, qk, dec] * 2))


def _gdn_scan_bwd(u, w, qd, kd, qk, dec, st0, st1, do):
    c, hd, nh, nc = GDN_CHUNK, HEAD_DIM, GDN_HEADS, _n_chunks()
    gw = nh * hd
    chunk_of = (lambda n: nc - 1 - n, lambda n: n)

    def body(*refs):
        ins = refs[:12]
        st_in = refs[12:14]
        do_in = refs[14:16]
        outs = refs[16:28]
        ds_ref = refs[28]

        @pl.when(pl.program_id(0) == 0)
        def _():
            ds_ref[...] = jnp.zeros_like(ds_ref)

        chains = [(d, h) for d in range(2) for h in range(nh)]
        states = [st_in[d][h] for d, h in chains]
        _, vjp = jax.vjp(_scan_steps, states, *_scan_operands(ins, chains, slice(0, 1)))
        ct_state = [ds_ref[d * nh + h] for d, h in chains]
        ct_out = [do_in[d][:, h * hd:(h + 1) * hd] for d, h in chains]
        grads = vjp((ct_state, ct_out))
        for i, (d, h) in enumerate(chains):
            cols = slice(h * hd, (h + 1) * hd)
            du_r, dw_r, dqd_r, dkd_r, dqk_r, ddec_r = outs[6 * d:6 * d + 6]
            ds_ref[d * nh + h] = grads[0][i]
            du_r[:, cols] = grads[1][i]
            dw_r[:, cols] = grads[2][i]
            dqd_r[:, cols] = grads[3][i]
            dkd_r[:, cols] = grads[4][i]
            dqk_r[h] = grads[5][i]
            ddec_r[h:h + 1, :] = jnp.broadcast_to(grads[6][i], (1, LANES))

    st_specs = [pl.BlockSpec((None, nh, hd, hd), lambda n, f=f: (f(n), 0, 0, 0)) for f in chunk_of]
    do_specs = [pl.BlockSpec((c, gw), lambda n, f=f: (f(n), 0)) for f in chunk_of]
    out_specs, out_shape = [], []
    for f in chunk_of:
        out_specs += [pl.BlockSpec((c, gw), lambda n, f=f: (f(n), 0))] * 4
        out_specs += [pl.BlockSpec((nh, c, c), lambda n, f=f: (0, f(n), 0)),
                      pl.BlockSpec((None, nh, LANES), lambda n, f=f: (f(n), 0, 0))]
        out_shape += [jax.ShapeDtypeStruct((SEQ, gw), F32)] * 4
        out_shape += [jax.ShapeDtypeStruct((nh, SEQ, c), F32), jax.ShapeDtypeStruct((nc, nh, LANES), F32)]
    return pl.pallas_call(
        body, name="gdn_scan_bwd", grid=(nc,), in_specs=_scan_in_specs(chunk_of) + st_specs + do_specs,
        out_specs=out_specs, out_shape=out_shape,
        scratch_shapes=[pltpu.VMEM((2 * nh, hd, hd), F32)],
        compiler_params=_params(("arbitrary",)),
    )(*([u, w, qd, kd, qk, dec] * 2), st0, st1, do, do)


def _post_fn(o0, o1, z, gn):
    return _rms(o0 + o1, gn) * _silu(z)


def _gdn_post_fwd(o0, o1, p, gn):
    tm, hd, nh = _row_tile(), HEAD_DIM, GDN_HEADS
    zoff = 3 * nh
    blk = lambda i, h: (i, h)
    (ya,) = _tiled(_post_fn, "gdn_post_fwd", (SEQ // tm, nh),
                   [(o0, (tm, hd), blk), (o1, (tm, hd), blk),
                    (p, (tm, hd), lambda i, h: (i, zoff + h)), (gn, (1, hd), lambda i, h: (0, 0))],
                   [((SEQ, nh * hd), BF16, (tm, hd), blk, ())])
    return ya


def _gdn_post_bwd(o0, o1, p, gn, dya):
    tm, hd, nh = _row_tile(), HEAD_DIM, GDN_HEADS
    zoff = 3 * nh
    blk = lambda i, h: (i, h)

    def fn(a, b, z, g, dy):
        _, vjp = jax.vjp(_post_fn, a, b, z, g)
        do, _, dz, dg = vjp(dy.astype(F32))
        return do, dz, dg

    return _tiled(fn, "gdn_post_bwd", (SEQ // tm, nh),
                  [(o0, (tm, hd), blk), (o1, (tm, hd), blk),
                   (p, (tm, hd), lambda i, h: (i, zoff + h)), (gn, (1, hd), lambda i, h: (0, 0)),
                   (dya, (tm, hd), blk)],
                  [((SEQ, nh * hd), F32, (tm, hd), blk, ()),
                   ((SEQ, nh * hd), BF16, (tm, hd), blk, ()),
                   ((1, hd), F32, (1, hd), lambda i, h: (0, 0), (0, 1))],
                  sem=("arbitrary", "arbitrary"))


def _sgu_ln(v, lg, lb):
    gv = _gelu(v)
    mu = jnp.mean(gv, axis=-1, keepdims=True)
    cen = gv - mu
    var = jnp.mean(cen * cen, axis=-1, keepdims=True)
    return cen * lax.rsqrt(var + NORM_EPS) * lg + lb


def _sgu_mix(vn, w_ref, bt_ref):
    parts = []
    for g in range(SGU_GROUPS):
        cols = slice(g * SGU_DIM, (g + 1) * SGU_DIM)
        parts.append(_mm(w_ref[g], vn[:, cols]) + bt_ref[:, g:g + 1])
    return jnp.concatenate(parts, axis=1)


def _sgu_specs(p):
    sw, sb = _sw(), SGU_BLOCK
    uoff = 4 * _gw()
    assert uoff % sw == 0
    iu = uoff // sw
    return [(p, (sb, sw), lambda n: (n, iu)), (p, (sb, sw), lambda n: (n, iu + 1))]


def _sgu_fwd(p, lg, lb, w_s, b_t):
    sw, sb = _sw(), SGU_BLOCK
    ins = _sgu_specs(p)

    def body(u_ref, v_ref, lg_ref, lb_ref, w_ref, bt_ref, y_ref):
        vn = _sgu_ln(v_ref[...], lg_ref[...], lb_ref[...])
        y_ref[...] = (_gelu(u_ref[...]) * _sgu_mix(vn, w_ref, bt_ref)).astype(BF16)

    one2 = lambda n: (0, 0)
    return pl.pallas_call(
        body, name="sgu_fwd", grid=(SEQ // sb,),
        in_specs=[pl.BlockSpec(b, m) for _, b, m in ins]
        + [pl.BlockSpec((1, sw), one2), pl.BlockSpec((1, sw), one2),
           pl.BlockSpec((SGU_GROUPS, sb, sb), lambda n: (0, 0, 0)), pl.BlockSpec((sb, LANES), one2)],
        out_specs=pl.BlockSpec((sb, sw), lambda n: (n, 0)),
        out_shape=jax.ShapeDtypeStruct((SEQ, sw), BF16),
        compiler_params=_params(("parallel",)),
    )(p, p, lg, lb, w_s, b_t)


def _sgu_bwd(p, lg, lb, w_s, b_t, dy):
    sw, sb, ng = _sw(), SGU_BLOCK, SGU_GROUPS
    ins = _sgu_specs(p)

    def body(u_ref, v_ref, lg_ref, lb_ref, w_ref, bt_ref, dy_ref, du_ref, dv_ref, dlg_ref, dlb_ref, dw_ref, dbt_ref):
        first = pl.program_id(0) == 0
        u, v = u_ref[...], v_ref[...]
        gu, gelu_vjp = jax.vjp(_gelu, u)
        vn, ln_vjp = jax.vjp(_sgu_ln, v, lg_ref[...], lb_ref[...])
        s = _sgu_mix(vn, w_ref, bt_ref)
        dyv = dy_ref[...].astype(F32)
        ds = dyv * gu
        (du,) = gelu_vjp(dyv * s)
        lane = lax.broadcasted_iota(jnp.int32, (sb, LANES), 1)
        dvn_parts = []
        dbt = jnp.zeros((sb, LANES), F32)
        for g in range(ng):
            cols = slice(g * SGU_DIM, (g + 1) * SGU_DIM)
            ds_g = ds[:, cols]
            dw_g = _mm_nt(ds_g, vn[:, cols])
            dvn_parts.append(_mm_tn(w_ref[g], ds_g))
            dbt = dbt + jnp.where(lane == g, jnp.sum(ds_g, axis=1, keepdims=True), 0.0)

            @pl.when(first)
            def _(g=g, dw_g=dw_g):
                dw_ref[g] = dw_g

            @pl.when(jnp.logical_not(first))
            def _(g=g, dw_g=dw_g):
                dw_ref[g] += dw_g

        dv, dlg, dlb = ln_vjp(jnp.concatenate(dvn_parts, axis=1))
        du_ref[...] = du.astype(BF16)
        dv_ref[...] = dv.astype(BF16)

        @pl.when(first)
        def _():
            dlg_ref[...] = dlg
            dlb_ref[...] = dlb
            dbt_ref[...] = dbt

        @pl.when(jnp.logical_not(first))
        def _():
            dlg_ref[...] += dlg
            dlb_ref[...] += dlb
            dbt_ref[...] += dbt

    one2 = lambda n: (0, 0)
    row = pl.BlockSpec((sb, sw), lambda n: (n, 0))
    return pl.pallas_call(
        body, name="sgu_bwd", grid=(SEQ // sb,),
        in_specs=[pl.BlockSpec(b, m) for _, b, m in ins]
        + [pl.BlockSpec((1, sw), one2), pl.BlockSpec((1, sw), one2),
           pl.BlockSpec((ng, sb, sb), lambda n: (0, 0, 0)), pl.BlockSpec((sb, LANES), one2), row],
        out_specs=[row, row, pl.BlockSpec((1, sw), one2), pl.BlockSpec((1, sw), one2),
                   pl.BlockSpec((ng, sb, sb), lambda n: (0, 0, 0)), pl.BlockSpec((sb, LANES), one2)],
        out_shape=[jax.ShapeDtypeStruct((SEQ, sw), BF16)] * 2 + [jax.ShapeDtypeStruct((1, sw), F32)] * 2
        + [jax.ShapeDtypeStruct((ng, sb, sb), F32), jax.ShapeDtypeStruct((sb, LANES), F32)],
        compiler_params=_params(("arbitrary",)),
    )(p, p, lg, lb, w_s, b_t, dy)


def _ffn_fn(xg, xv, bg, bv, *taps):
    n = FFN_CONV
    cg = _conv_rows(xg, taps[0:n]) + bg
    cv = _conv_rows(xv, taps[n:2 * n]) + bv
    return _silu(cg) * cv


def _ffn_tile():
    return _pick(D_FF, 256)


def _ffn_specs(up, conv_w, conv_b):
    tc = _ffn_tile()
    nt = D_FF // tc
    ins = [(up, (None, SEQ, tc), (lambda j, s=s: (s, 0, j))) for s in range(2)]
    ins += [(conv_b, (1, tc), (lambda j, s=s: (0, s * nt + j))) for s in range(2)]
    ins += [(conv_w, (FFN_CONV, tc), (lambda j, s=s: (0, s * nt + j))) for s in range(2)]
    return ins


def _ffn_act_fwd(up, conv_w, conv_b):
    tc = _ffn_tile()
    ins = _ffn_specs(up, conv_w, conv_b)

    def body(xg, xv, bg, bv, wg, wv, o_ref):
        taps = [w[j:j + 1, :] for w in (wg, wv) for j in range(FFN_CONV)]
        o_ref[...] = _ffn_fn(xg[...], xv[...], bg[...], bv[...], *taps).astype(BF16)

    return pl.pallas_call(
        body, name="ffn_act_fwd", grid=(D_FF // tc,),
        in_specs=[pl.BlockSpec(b, m) for _, b, m in ins],
        out_specs=pl.BlockSpec((SEQ, tc), lambda j: (0, j)),
        out_shape=jax.ShapeDtypeStruct((SEQ, D_FF), BF16),
        compiler_params=_params(("parallel",)),
    )(*[a for a, _, _ in ins])


def _ffn_act_bwd(up, conv_w, conv_b, dact):
    tc = _ffn_tile()
    nt = D_FF // tc
    ins = _ffn_specs(up, conv_w, conv_b) + [(dact, (SEQ, tc), lambda j: (0, j))]

    def body(xg, xv, bg, bv, wg, wv, dact_ref, dup_ref, dwg_ref, dwv_ref, dbg_ref, dbv_ref):
        taps = [w[j:j + 1, :] for w in (wg, wv) for j in range(FFN_CONV)]
        _, vjp = jax.vjp(_ffn_fn, xg[...], xv[...], bg[...], bv[...], *taps)
        grads = vjp(dact_ref[...].astype(F32))
        dup_ref[0] = grads[0].astype(BF16)
        dup_ref[1] = grads[1].astype(BF16)
        dbg_ref[...] = jnp.sum(grads[2], axis=0, keepdims=True)
        dbv_ref[...] = jnp.sum(grads[3], axis=0, keepdims=True)
        for j in range(FFN_CONV):
            dwg_ref[j:j + 1, :] = jnp.sum(grads[4 + j], axis=0, keepdims=True)
            dwv_ref[j:j + 1, :] = jnp.sum(grads[4 + FFN_CONV + j], axis=0, keepdims=True)

    col = lambda j: (0, j)
    outs = pl.pallas_call(
        body, name="ffn_act_bwd", grid=(nt,),
        in_specs=[pl.BlockSpec(b, m) for _, b, m in ins],
        out_specs=[pl.BlockSpec((2, SEQ, tc), lambda j: (0, 0, j)),
                   pl.BlockSpec((FFN_CONV, tc), col), pl.BlockSpec((FFN_CONV, tc), col),
                   pl.BlockSpec((1, tc), col), pl.BlockSpec((1, tc), col)],
        out_shape=[jax.ShapeDtypeStruct((2, SEQ, D_FF), BF16),
                   jax.ShapeDtypeStruct((FFN_CONV, D_FF), F32), jax.ShapeDtypeStruct((FFN_CONV, D_FF), F32),
                   jax.ShapeDtypeStruct((1, D_FF), F32), jax.ShapeDtypeStruct((1, D_FF), F32)],
        compiler_params=_params(("parallel",)),
    )(*[a for a, _, _ in ins])
    dup, dwg, dwv, dbg, dbv = outs
    return dup, jnp.concatenate([dwg, dwv], axis=1), jnp.concatenate([dbg, dbv], axis=1)


def _pad_lanes(v):
    return jnp.pad(v.reshape(1, -1), ((0, 0), (0, LANES - v.size)))


def _layer_fwd(x, lw, late=None):
    h = _rms_fwd(x, lw['norm_mix_g'], "rms_mix_fwd")
    p = _matmul(h, lw['w_all'], 'nn', F32, "mm_in")
    if late is not None:
        lw = dict(lw, **late(p))
    gates = _gate_fwd(p, lw['a_log'], lw['dt_bias'])
    q, k, v = _qkv_fwd(p, lw['qkv_conv_w'])
    u, w, qd, kd, qk, dec = _gdn_prep_fwd(q, k, v, gates)
    o0, o1, st0, st1 = _gdn_scan_fwd(u, w, qd, kd, qk, dec)
    ya = _gdn_post_fwd(o0, o1, p, lw['gdn_norm_g'])
    yb = _sgu_fwd(p, lw['sgu_ln_g'], lw['sgu_ln_b'], lw['sgu_w'], lw['sgu_bt'])
    ta = _matmul(ya, lw['w_branch_a'], 'nn', F32, "mm_branch_a", b_split=N_CHIPS)
    tb = _matmul(yb, lw['w_branch_b'], 'nn', F32, "mm_branch_b", b_split=N_CHIPS)
    m = _merge_fwd(p, ta, tb)
    x1 = _matmul(m, lw['w_out'], 'nn', F32, "mm_out", add=x)
    h2 = _rms_fwd(x1, lw['norm_ffn_g'], "rms_ffn_fwd")
    up = _matmul(h2, lw['w_up'], 'nn', F32, "mm_up", b_split=N_CHIPS, o_split=2)
    act = _ffn_act_fwd(up, lw['ffn_conv_w'], lw['ffn_conv_b'])
    x2 = _matmul(act, lw['w_down'], 'nn', F32, "mm_down", add=x1)
    saved = dict(x=x, h=h, p=p, gates=gates, q=q, k=k, v=v, u=u, w=w, qd=qd, kd=kd, qk=qk, dec=dec,
                 o0=o0, o1=o1, st0=st0, st1=st1, ya=ya, yb=yb, ta=ta, tb=tb, m=m, x1=x1, h2=h2, up=up, act=act)
    return x2, saved, lw


def _layer_bwd(dx, dx_bf, lw, s, after=(), mid=None):
    g = {}
    dact = _matmul(dx_bf, lw['w_down'], 'nt', BF16, "mm_down_dgrad", after=after)
    g['w_down'] = _matmul(s['act'], dx_bf, 'tn', PARTIAL, "mm_down_wgrad")
    dup, g['ffn_conv_w'], g['ffn_conv_b'] = _ffn_act_bwd(s['up'], lw['ffn_conv_w'], lw['ffn_conv_b'], dact)
    dh2 = _matmul(dup, lw['w_up'], 'nt', F32, "mm_up_dgrad", a_split=2, b_split=N_CHIPS)
    g['w_up'] = _matmul(s['h2'], dup, 'tn', PARTIAL, "mm_up_wgrad", b_split=2, o_split=N_CHIPS)
    dx1, dx1_bf, g['norm_ffn_g'] = _rms_bwd(s['x1'], lw['norm_ffn_g'], dh2, dx, "rms_ffn_bwd")
    after_mid = mid(g, dh2) if mid is not None else ()
    dm = _matmul(dx1_bf, lw['w_out'], 'nt', F32, "mm_out_dgrad", after=after_mid)
    g['w_out'] = _matmul(s['m'], dx1_bf, 'tn', PARTIAL, "mm_out_wgrad")
    d_ga, d_gb, d_ta, d_tb = _merge_bwd(s['p'], s['ta'], s['tb'], dm)
    dya = _matmul(d_ta, lw['w_branch_a'], 'nt', BF16, "mm_branch_a_dgrad", b_split=N_CHIPS)
    dyb = _matmul(d_tb, lw['w_branch_b'], 'nt', BF16, "mm_branch_b_dgrad", b_split=N_CHIPS)
    g['w_branch_a'] = _matmul(s['ya'], d_ta, 'tn', PARTIAL, "mm_branch_a_wgrad", o_split=N_CHIPS)
    g['w_branch_b'] = _matmul(s['yb'], d_tb, 'tn', PARTIAL, "mm_branch_b_wgrad", o_split=N_CHIPS)
    du_s, dv_s, g['sgu_ln_g'], g['sgu_ln_b'], g['sgu_w'], g['sgu_bt'] = _sgu_bwd(
        s['p'], lw['sgu_ln_g'], lw['sgu_ln_b'], lw['sgu_w'], lw['sgu_bt'], dyb)
    do, dz, g['gdn_norm_g'] = _gdn_post_bwd(s['o0'], s['o1'], s['p'], lw['gdn_norm_g'], dya)
    cts = _gdn_scan_bwd(s['u'], s['w'], s['qd'], s['kd'], s['qk'], s['dec'], s['st0'], s['st1'], do)
    dq, dk, dv, dgates = _gdn_prep_bwd(s['q'], s['k'], s['v'], s['gates'], cts)
    (dxq, dxk, dxv), g['qkv_conv_w'] = _qkv_bwd(s['p'], lw['qkv_conv_w'], dq, dk, dv)
    d_ab, g['a_log'], g['dt_bias'] = _gate_bwd(s['p'], lw['a_log'], lw['dt_bias'], dgates)
    dp = jnp.concatenate([dxq, dxk, dxv, dz, du_s, dv_s, d_ga, d_gb, d_ab], axis=1)
    dh = _matmul(dp, lw['w_all'], 'nt', F32, "mm_in_dgrad")
    g['w_all'] = _matmul(s['h'], dp, 'tn', PARTIAL, "mm_in_wgrad")
    dx0, dx0_bf, g['norm_mix_g'] = _rms_bwd(s['x'], lw['norm_mix_g'], dh, dx1, "rms_mix_bwd")
    return dx0, dx0_bf, g


def _w_all_from_w_in(w_in):
    n_ab = 4 * GDN_HEADS
    cut = 4 * _gw()
    pad = jnp.zeros((w_in.shape[0], AB_PAD - n_ab), w_in.dtype)
    return jnp.concatenate([w_in[:, :cut], w_in[:, cut + n_ab:], w_in[:, cut:cut + n_ab], pad], axis=1)


def _w_in_grad_from_all(g_all):
    n_ab = 4 * GDN_HEADS
    cut = 4 * _gw()
    n_main = cut + 2 * _sw() + 2 * D_MODEL
    return jnp.concatenate([g_all[:, :cut], g_all[:, n_main:n_main + n_ab], g_all[:, cut:n_main]], axis=1)


_FIRST = ('w_in', 'qkv_conv_w')
_REST = ('w_branch_a', 'w_branch_b', 'w_out', 'w_up', 'w_down', 'ffn_conv_w')


def _cat_cols(t):
    return jnp.concatenate([t[j] for j in range(N_CHIPS)], axis=-1)


def _layer_weights_rest(big):
    return dict(
        w_branch_a=big['w_branch_a'], w_branch_b=big['w_branch_b'], w_up=big['w_up'],
        w_out=big['w_out'].reshape(D_MODEL, D_MODEL), w_down=big['w_down'].reshape(D_FF, D_MODEL),
        ffn_conv_w=_cat_cols(big['ffn_conv_w']))


def _layer_weights_first(l, big, small):
    return dict(
        w_all=_w_all_from_w_in(_cat_cols(big['w_in'])), qkv_conv_w=_cat_cols(big['qkv_conv_w']),
        norm_mix_g=small['norm_mix_g'][l:l + 1], norm_ffn_g=small['norm_ffn_g'][l:l + 1],
        a_log=_pad_lanes(small['a_log'][l]), dt_bias=_pad_lanes(small['dt_bias'][l]),
        gdn_norm_g=small['gdn_norm_g'][l:l + 1],
        sgu_ln_g=small['sgu_ln_g'][l:l + 1], sgu_ln_b=small['sgu_ln_b'][l:l + 1], sgu_w=small['sgu_w'][l],
        sgu_bt=jnp.pad(small['sgu_b'][l].T, ((0, 0), (0, LANES - SGU_GROUPS))),
        ffn_conv_b=small['ffn_conv_b'][l:l + 1])


_SMALL_GRADS = ('norm_mix_g', 'a_log', 'dt_bias', 'gdn_norm_g', 'sgu_ln_g', 'sgu_ln_b', 'sgu_w', 'sgu_bt',
                'norm_ffn_g', 'ffn_conv_b', 'qkv_conv_w', 'ffn_conv_w')
_BIG = ('w_in', 'w_branch_a', 'w_branch_b', 'w_out', 'w_up', 'w_down')


def _big_grad_slices(g, names=_BIG):
    def one(k):
        if k == 'w_in':
            return _w_in_grad_from_all(g['w_all']).reshape(D_MODEL, N_CHIPS, -1).transpose(1, 0, 2)
        if k == 'w_out':
            return g[k].reshape(N_CHIPS, D_MODEL // N_CHIPS, D_MODEL)
        if k == 'w_down':
            return g[k].reshape(N_CHIPS, D_FF // N_CHIPS, D_MODEL)
        return g[k]
    return {k: one(k) for k in names}


def _adamw(w, g, m, v, name):
    shape = w.shape
    cols = shape[-1]
    rows = w.size // cols
    tr = _pick(rows, max(8, (1 << 18) // cols // 8 * 8), 8) if rows % 8 == 0 else rows

    def fn(wv, gv, mv, vv):
        m2 = ADAM_B1 * mv + (1.0 - ADAM_B1) * gv
        v2 = ADAM_B2 * vv + (1.0 - ADAM_B2) * (gv * gv)
        m_hat = m2 / (1.0 - ADAM_B1 ** ADAM_STEP)
        v_hat = v2 / (1.0 - ADAM_B2 ** ADAM_STEP)
        delta = -ADAM_LR * (m_hat / (jnp.sqrt(v_hat) + ADAM_EPS) + ADAM_WD * wv)
        return delta, m2, v2

    row = lambda i: (i, 0)
    outs = _tiled(fn, name, (rows // tr,),
                  [(t.reshape(rows, cols), (tr, cols), row) for t in (w, g, m, v)],
                  [((rows, cols), F32, (tr, cols), row, ())] * 3, sem=("parallel",))
    return [o.reshape(shape) for o in outs]


MESH_IDS = pl.DeviceIdType.MESH
ANY = pl.BlockSpec(memory_space=pl.ANY)


def _place():
    x, y, c = lax.axis_index("x"), lax.axis_index("y"), lax.axis_index("c")
    chips = [(1 - x, y), (x, 1 - y), (1 - x, 1 - y)]
    return x, y, c, 2 * x + y, chips


def _chip_index():
    return 2 * lax.axis_index("x") + lax.axis_index("y")


HBM = pl.BlockSpec(memory_space=pltpu.HBM)
SEM = pl.BlockSpec(memory_space=pltpu.SEMAPHORE)
DATAFLOW = pltpu.SideEffectType.DATAFLOW_SIDE_EFFECTING
TOKEN = jax.ShapeDtypeStruct((8, LANES), F32)


def _in_hbm(t):
    return pltpu.with_memory_space_constraint(t, pltpu.HBM)


def _place_shard(w, l, dtype, name):
    _, r, cols = w.shape
    tr = _pick(r, max(16, (1 << 18) // cols // 16 * 16), 16) if r % 16 == 0 else r

    def body(w_ref, o_ref):
        o_ref[...] = w_ref[...].astype(dtype)

    return pl.pallas_call(
        body, name=name, grid=(r // tr,),
        in_specs=[pl.BlockSpec((None, tr, cols), lambda i: (l, i, 0))],
        out_specs=pl.BlockSpec((None, tr, cols), lambda i: (_chip_index(), i, 0)),
        out_shape=jax.ShapeDtypeStruct((N_CHIPS, r, cols), dtype),
        compiler_params=_params(("parallel",)),
    )(w)


def _my_rows(ref_or_shape_rows, c):
    r = ref_or_shape_rows
    if r % 32 == 0:
        return pl.ds(c * (r // 2), r // 2), pl.ds((1 - c) * (r // 2), r // 2), True
    return pl.ds(0, r), pl.ds(0, r), False


def _gather_copies(bufs, send_sems, recv_sems):
    x, y, c, me, chips = _place()
    out = []
    for i, buf in enumerate(bufs):
        mine, _, _ = _my_rows(buf.shape[1], c)
        for j, (cx, cy) in enumerate(chips):
            def rcopy(slab, i=i, j=j, cx=cx, cy=cy):
                return pltpu.make_async_remote_copy(src_ref=slab, dst_ref=slab, send_sem=send_sems.at[3 * i + j],
                                                    recv_sem=recv_sems.at[3 * i + j], device_id=(cx, cy, c),
                                                    device_id_type=MESH_IDS)
            out.append((rcopy(buf.at[me, mine]), rcopy(buf.at[2 * cx + cy, mine])))
    return out


def _gather_start(xs, name, after=()):
    n = len(xs)
    n_after = len(after)

    def body(*refs):
        refs = refs[n + n_after:]
        send_sems, recv_sems = refs[0], refs[1]
        bufs = refs[2:n + 2]
        token = refs[n + 2]
        for out_going, _ in _gather_copies(bufs, send_sems, recv_sems):
            out_going.start()
        token[...] = jnp.zeros_like(token)

    res = pl.pallas_call(
        body, name=name, in_specs=[HBM] * n + [ANY] * n_after,
        out_specs=[SEM, SEM] + [HBM] * n + [pl.BlockSpec(memory_space=pltpu.VMEM)],
        out_shape=[pltpu.SemaphoreType.DMA((3 * n,)), pltpu.SemaphoreType.DMA((3 * n,))]
        + [pltpu.HBM(t.shape, t.dtype) for t in xs] + [TOKEN],
        input_output_aliases={i: i + 2 for i in range(n)},
        compiler_params=pltpu.CompilerParams(has_side_effects=DATAFLOW),
    )(*[_in_hbm(t) for t in xs], *after)
    return res[0], res[1], res[2:2 + n], res[2 + n]


def _gather_wait(bufs, send_sems, recv_sems, after, name):
    n = len(bufs)

    def body(*refs):
        b_refs = refs[:n]
        s_sems, r_sems = refs[n], refs[n + 1]
        for out_going, in_coming in _gather_copies(b_refs, s_sems, r_sems):
            out_going.wait_send()
            in_coming.wait_recv()

    return pl.pallas_call(
        body, name=name, in_specs=[HBM] * n + [SEM, SEM, ANY], out_specs=[HBM] * n,
        out_shape=[pltpu.HBM(t.shape, t.dtype) for t in bufs],
        input_output_aliases={i: i for i in range(n)},
        compiler_params=pltpu.CompilerParams(has_side_effects=DATAFLOW),
    )(*bufs, send_sems, recv_sems, after)


def _gather_forward(bufs, name):
    idx = [i for i, t in enumerate(bufs) if t.shape[1] % 32 == 0]
    xs = [bufs[i] for i in idx]
    n = len(xs)

    def body(*refs):
        o_refs = refs[n:2 * n]
        send_sems, recv_sems = refs[2 * n:]
        x, y, c, _, chips = _place()
        copies = []
        for i in range(n):
            mine, theirs, _ = _my_rows(xs[i].shape[1], c)
            for j, (cx, cy) in enumerate(chips):
                def rcopy(slab, i=i, j=j):
                    return pltpu.make_async_remote_copy(src_ref=slab, dst_ref=slab, send_sem=send_sems.at[3 * i + j],
                                                        recv_sem=recv_sems.at[3 * i + j], device_id=(x, y, 1 - c),
                                                        device_id_type=MESH_IDS)
                copies.append((rcopy(o_refs[i].at[2 * cx + cy, mine]), rcopy(o_refs[i].at[2 * cx + cy, theirs])))
                copies[-1][0].start()
        for out_going, in_coming in copies:
            out_going.wait_send()
            in_coming.wait_recv()

    res = pl.pallas_call(
        body, name=name, in_specs=[ANY] * n, out_specs=[ANY] * n,
        out_shape=[jax.ShapeDtypeStruct(t.shape, t.dtype) for t in xs],
        input_output_aliases={i: i for i in range(n)},
        scratch_shapes=[pltpu.SemaphoreType.DMA((3 * n,)), pltpu.SemaphoreType.DMA((3 * n,))],
        compiler_params=pltpu.CompilerParams(has_side_effects=True),
    )(*xs)
    out = list(bufs)
    for i, t in zip(idx, res):
        out[i] = t
    return out


def _half_tile(rh, cols):
    return _pick(rh, max(16, (1 << 18) // cols // 16 * 16), 16)


N_LAND = 7


def _scatter_copies(p_refs, r_refs, send_sems, recv_sems):
    x, y, c, me, chips = _place()
    outgoing, incoming = [], []
    for i, (p, land) in enumerate(zip(p_refs, r_refs)):
        rh = p.shape[1] // 2

        def copy(src, slot, k_send, k_recv, to, i=i, land=land):
            return pltpu.make_async_remote_copy(
                src_ref=src, dst_ref=land.at[slot], send_sem=send_sems.at[N_LAND * i + k_send],
                recv_sem=recv_sems.at[N_LAND * i + k_recv], device_id=to, device_id_type=MESH_IDS)

        for r, (cx, cy) in enumerate(chips):
            for k in range(2):
                outgoing.append(copy(p.at[2 * cx + cy, pl.ds(k * rh, rh)], 2 * r + c, 2 * r + k, 2 * r + c,
                                     (cx, cy, k)))
                incoming.append(copy(land.at[2 * r + k], 2 * r + k, 2 * r + k, 2 * r + k, (x, y, c)))
        outgoing.append(copy(p.at[me, pl.ds((1 - c) * rh, rh)], 6, 6, 6, (x, y, 1 - c)))
        incoming.append(copy(land.at[6], 6, 6, 6, (x, y, c)))
    return outgoing, incoming


def _scatter_start(ps, name):
    n = len(ps)
    lands = [lax.empty((N_LAND, t.shape[1] // 2, t.shape[2]), t.dtype) for t in ps]

    def body(*refs):
        send_sems, recv_sems = refs[2 * n], refs[2 * n + 1]
        p_refs = refs[2 * n + 2:3 * n + 2]
        r_refs = refs[3 * n + 2:4 * n + 2]
        token = refs[4 * n + 2]
        for cp in _scatter_copies(p_refs, r_refs, send_sems, recv_sems)[0]:
            cp.start()
        token[...] = jnp.zeros_like(token)

    res = pl.pallas_call(
        body, name=name, in_specs=[HBM] * (2 * n),
        out_specs=[SEM, SEM] + [HBM] * (2 * n) + [pl.BlockSpec(memory_space=pltpu.VMEM)],
        out_shape=[pltpu.SemaphoreType.DMA((N_LAND * n,)), pltpu.SemaphoreType.DMA((N_LAND * n,))]
        + [pltpu.HBM(t.shape, t.dtype) for t in list(ps) + lands] + [TOKEN],
        input_output_aliases={i: i + 2 for i in range(2 * n)},
        compiler_params=pltpu.CompilerParams(has_side_effects=DATAFLOW),
    )(*[_in_hbm(t) for t in list(ps) + lands])
    return res[0], res[1], res[2:2 + n], res[2 + n:2 + 2 * n], res[2 + 2 * n]


def _scatter_wait(ps, lands, send_sems, recv_sems, after, name):
    n = len(ps)
    after = tuple(after) if isinstance(after, (tuple, list)) else (after,)

    def body(*refs):
        p_refs, r_refs = refs[:n], refs[n:2 * n]
        s_sems, r_sems = refs[2 * n], refs[2 * n + 1]
        outgoing, incoming = _scatter_copies(p_refs, r_refs, s_sems, r_sems)
        for cp in outgoing:
            cp.wait_send()
        for cp in incoming:
            cp.wait_recv()

    res = pl.pallas_call(
        body, name=name, in_specs=[HBM] * (2 * n) + [SEM, SEM] + [ANY] * len(after), out_specs=[HBM] * (2 * n),
        out_shape=[pltpu.HBM(t.shape, t.dtype) for t in list(ps) + list(lands)],
        input_output_aliases={i: i for i in range(2 * n)},
        compiler_params=pltpu.CompilerParams(has_side_effects=DATAFLOW),
    )(*ps, *lands, send_sems, recv_sems, *after)
    return res[:n], res[n:]


def _reduce_own(p, rcv, acc, l, name):
    nchip, r, cols = p.shape
    rh = r // 2
    tr = _half_tile(rh, cols)
    nt = rh // tr

    def body(*refs):
        p_ref, r_ref = refs[:2]
        o_ref = refs[-1]
        total = p_ref[...].astype(F32)
        for j in range(N_LAND):
            total = total + r_ref[j].astype(F32)
        o_ref[...] = total

    mine = lambda i: lax.axis_index("c") * nt + i
    in_specs = [pl.BlockSpec((None, tr, cols), lambda i: (_chip_index(), mine(i), 0)),
                pl.BlockSpec((N_LAND, tr, cols), lambda i: (0, i, 0))]
    args = [p, rcv]
    aliases = {}
    if acc is not None:
        in_specs.append(ANY)
        args.append(acc)
        aliases = {2: 0}
    return pl.pallas_call(
        body, name=name, grid=(nt,), in_specs=in_specs,
        out_specs=pl.BlockSpec((None, tr, cols), lambda i: (l, mine(i), 0)),
        out_shape=jax.ShapeDtypeStruct((DEPTH, r, cols), F32), input_output_aliases=aliases,
        compiler_params=_params(("parallel",)),
    )(*args)


def _share_halves(fs, l):
    n = len(fs)

    def body(*refs):
        o_refs = refs[n:2 * n]
        send_sems, recv_sems = refs[2 * n:]
        x, y, c, _, _ = _place()

        def halves(i):
            rh = fs[i].shape[1] // 2
            return o_refs[i].at[l, pl.ds(c * rh, rh), :], o_refs[i].at[l, pl.ds((1 - c) * rh, rh), :]

        def copy(i, rows):
            return pltpu.make_async_remote_copy(src_ref=rows, dst_ref=rows, send_sem=send_sems.at[i],
                                                recv_sem=recv_sems.at[i], device_id=(x, y, 1 - c),
                                                device_id_type=MESH_IDS)

        for i in range(n):
            copy(i, halves(i)[0]).start()
        for i in range(n):
            mine, theirs = halves(i)
            copy(i, mine).wait_send()
            copy(i, theirs).wait_recv()

    return pl.pallas_call(
        body, name="grad_share_halves", in_specs=[ANY] * n, out_specs=[ANY] * n,
        out_shape=[jax.ShapeDtypeStruct(t.shape, t.dtype) for t in fs],
        input_output_aliases={i: i for i in range(n)},
        scratch_shapes=[pltpu.SemaphoreType.DMA((n,)), pltpu.SemaphoreType.DMA((n,))],
        compiler_params=pltpu.CompilerParams(has_side_effects=True),
    )(*fs)


def _reduce_scatter_begin(slices, l, tag):
    names = list(slices)
    send_sems, recv_sems, ps, lands, token = _scatter_start([slices[k] for k in names],
                                                            "grad_scatter_start_%s%d" % (tag, l))
    return dict(names=names, ps=ps, lands=lands, sems=(send_sems, recv_sems), token=token, l=l, tag=tag)


def _reduce_scatter_end(st, stacked, after):
    l = st['l']
    ps, rs = _scatter_wait(st['ps'], st['lands'], *st['sems'], after, "grad_scatter_wait_%s%d" % (st['tag'], l))
    fs = [_reduce_own(p, r, stacked.get(k), l, "grad_reduce_own_" + k)
          for k, p, r in zip(st['names'], ps, rs)]
    stacked.update(zip(st['names'], _share_halves(fs, l)))


def _all_reduce_small(packed, after=()):
    rows = packed.shape[0]
    n_dev = 2 * N_CHIPS
    tr = _pick(rows, 512, 8)

    def body(x_ref, *rest):
        o_ref, buf, send_sems, recv_sems = rest[len(after):]
        x, y, c, _, _ = _place()
        me = 4 * x + 2 * y + c
        copies = []
        for px in range(2):
            for py in range(2):
                for pc in range(2):
                    peer = 4 * px + 2 * py + pc
                    cp = pltpu.make_async_remote_copy(src_ref=x_ref, dst_ref=buf.at[me], send_sem=send_sems.at[peer],
                                                      recv_sem=recv_sems.at[me], device_id=(px, py, pc),
                                                      device_id_type=MESH_IDS)

                    @pl.when(peer != me)
                    def _(cp=cp):
                        cp.start()
                    copies.append((peer, cp))
        buf[me] = x_ref[...]
        for peer, cp in copies:
            @pl.when(peer != me)
            def _(cp=cp, peer=peer):
                cp.wait_send()
                pltpu.make_async_remote_copy(src_ref=x_ref, dst_ref=buf.at[peer], send_sem=send_sems.at[peer],
                                             recv_sem=recv_sems.at[peer], device_id=(x, y, c),
                                             device_id_type=MESH_IDS).wait_recv()

        @pl.loop(0, rows // tr)
        def _(t):
            sl = pl.ds(pl.multiple_of(t * tr, 8), tr)
            acc = buf[0, sl, :]
            for s in range(1, n_dev):
                acc = acc + buf[s, sl, :]
            o_ref[sl, :] = acc

    vm = pl.BlockSpec(memory_space=pltpu.VMEM)
    return pl.pallas_call(
        body, name="all_reduce_small", in_specs=[vm] + [ANY] * len(after), out_specs=vm,
        out_shape=jax.ShapeDtypeStruct(packed.shape, F32),
        scratch_shapes=[pltpu.VMEM((n_dev, rows, LANES), F32), pltpu.SemaphoreType.DMA((n_dev,)),
                        pltpu.SemaphoreType.DMA((n_dev,))],
        compiler_params=pltpu.CompilerParams(vmem_limit_bytes=VMEM_LIMIT, has_side_effects=True),
    )(packed, *after)


_WEIGHTS = ('norm_mix_g', 'w_in', 'qkv_conv_w', 'a_log', 'dt_bias', 'gdn_norm_g', 'w_branch_a', 'sgu_ln_g',
            'sgu_ln_b', 'sgu_w', 'sgu_b', 'w_branch_b', 'w_out', 'norm_ffn_g', 'w_up', 'ffn_conv_w', 'ffn_conv_b',
            'w_down', 'final_norm_g')


def _local_step(x, target, weights_of, final_norm_g, bwd_after=None, mid_layer=None, after_layer=None):
    lws, saves = [], []
    for l in range(DEPTH):
        lw, late = weights_of(l, x)
        x, s, lw = _layer_fwd(x, lw, late)
        lws.append(lw)
        saves.append(s)
    loss, dx, dx_bf, d_final = _loss_head(x, final_norm_g.reshape(1, -1), target)
    grads = [None] * DEPTH
    for l in reversed(range(DEPTH)):
        after = bwd_after(l) if bwd_after is not None else ()
        mid = (lambda g, dh2, l=l: mid_layer(l, g, dh2)) if mid_layer is not None else None
        dx, dx_bf, grads[l] = _layer_bwd(dx, dx_bf, lws[l], saves[l], after, mid)
        if after_layer is not None:
            after_layer(l, grads[l], dx)
    return loss, dx, grads, d_final


def _pack_small(grads, d_final):
    parts = [grads[l][k].reshape(-1) for l in range(DEPTH) for k in _SMALL_GRADS] + [d_final.reshape(-1)]
    flat = jnp.concatenate(parts)
    rows = -(-flat.size // (8 * LANES)) * 8
    return jnp.pad(flat, (0, rows * LANES - flat.size)).reshape(rows, LANES), [p.size for p in parts]


def _unpack_small(packed, grads, d_final):
    flat = packed.reshape(-1)
    out, off = [], 0
    for l in range(DEPTH):
        d = {}
        for k in _SMALL_GRADS:
            t = grads[l][k]
            d[k] = flat[off:off + t.size].reshape(t.shape)
            off += t.size
        out.append(d)
    return out, flat[off:off + d_final.size].reshape(d_final.shape)


def kernel(x, norm_mix_g, w_in, qkv_conv_w, a_log, dt_bias, gdn_norm_g, w_branch_a, sgu_ln_g, sgu_ln_b, sgu_w, sgu_b, w_branch_b, w_out, norm_ffn_g, w_up, ffn_conv_w, ffn_conv_b, w_down, final_norm_g, loss_target, m_norm_mix_g, m_w_in, m_qkv_conv_w, m_a_log, m_dt_bias, m_gdn_norm_g, m_w_branch_a, m_sgu_ln_g, m_sgu_ln_b, m_sgu_w, m_sgu_b, m_w_branch_b, m_w_out, m_norm_ffn_g, m_w_up, m_ffn_conv_w, m_ffn_conv_b, m_w_down, m_final_norm_g, v_norm_mix_g, v_w_in, v_qkv_conv_w, v_a_log, v_dt_bias, v_gdn_norm_g, v_w_branch_a, v_sgu_ln_g, v_sgu_ln_b, v_sgu_w, v_sgu_b, v_w_branch_b, v_w_out, v_norm_ffn_g, v_w_up, v_ffn_conv_w, v_ffn_conv_b, v_w_down, v_final_norm_g):
    w = dict(norm_mix_g=norm_mix_g, w_in=w_in, qkv_conv_w=qkv_conv_w, a_log=a_log, dt_bias=dt_bias,
             gdn_norm_g=gdn_norm_g, w_branch_a=w_branch_a, sgu_ln_g=sgu_ln_g, sgu_ln_b=sgu_ln_b, sgu_w=sgu_w,
             sgu_b=sgu_b, w_branch_b=w_branch_b, w_out=w_out, norm_ffn_g=norm_ffn_g, w_up=w_up,
             ffn_conv_w=ffn_conv_w, ffn_conv_b=ffn_conv_b, w_down=w_down, final_norm_g=final_norm_g)
    m = dict(norm_mix_g=m_norm_mix_g, w_in=m_w_in, qkv_conv_w=m_qkv_conv_w, a_log=m_a_log, dt_bias=m_dt_bias,
             gdn_norm_g=m_gdn_norm_g, w_branch_a=m_w_branch_a, sgu_ln_g=m_sgu_ln_g, sgu_ln_b=m_sgu_ln_b,
             sgu_w=m_sgu_w, sgu_b=m_sgu_b, w_branch_b=m_w_branch_b, w_out=m_w_out, norm_ffn_g=m_norm_ffn_g,
             w_up=m_w_up, ffn_conv_w=m_ffn_conv_w, ffn_conv_b=m_ffn_conv_b, w_down=m_w_down,
             final_norm_g=m_final_norm_g)
    v = dict(norm_mix_g=v_norm_mix_g, w_in=v_w_in, qkv_conv_w=v_qkv_conv_w, a_log=v_a_log, dt_bias=v_dt_bias,
             gdn_norm_g=v_gdn_norm_g, w_branch_a=v_w_branch_a, sgu_ln_g=v_sgu_ln_g, sgu_ln_b=v_sgu_ln_b,
             sgu_w=v_sgu_w, sgu_b=v_sgu_b, w_branch_b=v_w_branch_b, w_out=v_w_out, norm_ffn_g=v_norm_ffn_g,
             w_up=v_w_up, ffn_conv_w=v_ffn_conv_w, ffn_conv_b=v_ffn_conv_b, w_down=v_w_down,
             final_norm_g=v_final_norm_g)
    chip = _chip_index()

    in_flight = []
    for l in range(DEPTH):
        for tag, names in (("first", _FIRST), ("rest", _REST)):
            placed = [_place_shard(w[k], l, BF16 if k in _BIG else F32, "place_" + k) for k in names]
            earlier = (in_flight[-1][3],) if in_flight else ()
            in_flight.append(_gather_start(placed, "gather_start_%s_%d" % (tag, l), earlier))

    def arrived(l, tag, names, after):
        send_sems, recv_sems, bufs, _ = in_flight[2 * l + (tag == "rest")]
        bufs = _gather_wait(bufs, send_sems, recv_sems, after, "gather_wait_%s_%d" % (tag, l))
        return dict(zip(names, _gather_forward(bufs, "gather_forward_" + tag)))

    def weights_of(l, x_in):
        first = arrived(l, "first", _FIRST, in_flight[-1][3] if l == 0 else x_in)
        return (_layer_weights_first(l, first, w),
                lambda p: _layer_weights_rest(arrived(l, "rest", _REST, p)))

    group_a, group_b = ('w_up', 'w_down'), ('w_in', 'w_branch_a', 'w_branch_b', 'w_out')
    big_g = {}
    pending = {'a': None, 'b': None}

    def finish(tag, after):
        if pending[tag] is not None:
            _reduce_scatter_end(pending[tag], big_g, after)
            pending[tag] = None

    def bwd_after(l):
        return () if pending['b'] is None else (pending['b']['token'],)

    def mid_layer(l, g, dh2):
        finish('b', dh2)
        pending['a'] = _reduce_scatter_begin(_big_grad_slices(g, group_a), l, 'a')
        return (pending['a']['token'],)

    def after_layer(l, g, dx):
        finish('a', dx)
        pending['b'] = _reduce_scatter_begin(_big_grad_slices(g, group_b), l, 'b')

    loss, grad_x, grads, d_final = _local_step(x[0], loss_target[0], weights_of, w['final_norm_g'], bwd_after,
                                               mid_layer, after_layer)
    loss = lax.psum(loss[0, 0], ("x", "y", "c"))
    packed, _ = _pack_small(grads, d_final)
    reduced_small = _all_reduce_small(packed, (pending['b']['token'],))
    small_g, d_final = _unpack_small(reduced_small, grads, d_final)
    deltas, new_m, new_v, g_out = {}, {}, {}, {}
    for k in group_a:
        g_out[k] = big_g[k].reshape(w[k].shape)
        deltas[k], new_m[k], new_v[k] = _adamw(w[k], g_out[k], m[k], v[k], "adamw_" + k)
    finish('b', (reduced_small,) + tuple(deltas[k] for k in group_a))

    def stack(k):
        return jnp.stack([small_g[l][k] for l in range(DEPTH)])

    nd = 2 * GDN_HEADS
    for k in group_b:
        g_out[k] = big_g[k].reshape(w[k].shape)
    for k in ('norm_mix_g', 'gdn_norm_g', 'sgu_ln_g', 'sgu_ln_b', 'norm_ffn_g', 'ffn_conv_b'):
        g_out[k] = stack(k).reshape(w[k].shape)
    g_out['sgu_w'] = stack('sgu_w')
    g_out['a_log'] = stack('a_log')[:, 0, :nd].reshape(w['a_log'].shape)
    g_out['dt_bias'] = stack('dt_bias')[:, 0, :nd].reshape(w['dt_bias'].shape)
    g_out['sgu_b'] = jnp.swapaxes(stack('sgu_bt')[:, :, :SGU_GROUPS], 1, 2)
    for k in ('qkv_conv_w', 'ffn_conv_w'):
        full = stack(k)
        width = w[k].shape[-1]
        g_out[k] = lax.dynamic_slice_in_dim(full, chip * width, width, axis=2)
    g_out['final_norm_g'] = d_final.reshape(w['final_norm_g'].shape)

    for k in _WEIGHTS:
        if k not in group_a:
            deltas[k], new_m[k], new_v[k] = _adamw(w[k], g_out[k], m[k], v[k], "adamw_" + k)
    return (loss, grad_x[None], *[g_out[k] for k in _WEIGHTS], *[deltas[k] for k in _WEIGHTS],
            *[new_m[k] for k in _WEIGHTS], *[new_v[k] for k in _WEIGHTS])
```

```python
import functools
import math

import jax
import jax.numpy as jnp
from jax import lax
from jax.experimental import pallas as pl
from jax.experimental.pallas import tpu as pltpu

F32 = jnp.float32
BF16 = jnp.bfloat16
PARTIAL = BF16

D_MODEL = 2048
SEQ = 2048
DEPTH = 4
GDN_HEADS = 8
HEAD_DIM = 128
GDN_CHUNK = 64
QKV_CONV = 5
SGU_GROUPS = 8
SGU_DIM = 128
SGU_BLOCK = 128
D_FF = 5632
FFN_CONV = 3
NORM_EPS = 1e-6
N_CHIPS = 4

ADAM_LR = 0.001
ADAM_B1 = 0.9
ADAM_B2 = 0.999
ADAM_EPS = 1e-08
ADAM_WD = 0.01
ADAM_STEP = 10

LANES = 128
VMEM_LIMIT = 56 * 1024 * 1024
AB_PAD = LANES


def _gw():
    return GDN_HEADS * HEAD_DIM


def _sw():
    return SGU_GROUPS * SGU_DIM


def _n_all():
    return 4 * _gw() + 2 * _sw() + 2 * D_MODEL + AB_PAD


def _pick(n, target, align=LANES):
    if n <= target:
        return n
    best = None
    t = align
    while t <= target:
        if n % t == 0:
            best = t
        t += align
    assert best is not None, (n, target, align)
    return best


def _params(sem=None):
    return pltpu.CompilerParams(dimension_semantics=sem, vmem_limit_bytes=VMEM_LIMIT)


def _dot(a, b, dims, hi):
    dn = (dims, ((), ()))
    if not hi:
        return lax.dot_general(a.astype(BF16), b.astype(BF16), dn, preferred_element_type=F32)
    a_hi, b_hi = a.astype(BF16), b.astype(BF16)
    a_lo = (a - a_hi.astype(F32)).astype(BF16)
    b_lo = (b - b_hi.astype(F32)).astype(BF16)
    d = lambda p, q: lax.dot_general(p, q, dn, preferred_element_type=F32)
    return d(a_hi, b_hi) + (d(a_hi, b_lo) + d(a_lo, b_hi))


def _make_mm(hi):
    @jax.custom_vjp
    def mm(a, b):
        return _dot(a, b, ((1,), (0,)), hi)

    @jax.custom_vjp
    def mm_nt(a, b):
        return _dot(a, b, ((1,), (1,)), hi)

    @jax.custom_vjp
    def mm_tn(a, b):
        return _dot(a, b, ((0,), (0,)), hi)

    mm.defvjp(lambda a, b: (mm(a, b), (a, b)), lambda r, g: (mm_nt(g, r[1]), mm_tn(r[0], g)))
    mm_nt.defvjp(lambda a, b: (mm_nt(a, b), (a, b)), lambda r, g: (mm(g, r[1]), mm_tn(g, r[0])))
    mm_tn.defvjp(lambda a, b: (mm_tn(a, b), (a, b)), lambda r, g: (mm_nt(r[1], g), mm(r[0], g)))
    return mm, mm_nt, mm_tn


_mm, _mm_nt, _mm_tn = _make_mm(False)
_mmh, _mmh_nt, _mmh_tn = _make_mm(True)


def _shift_rows_raw(x, s):
    if s == 0:
        return x
    n = x.shape[0]
    rolled = pltpu.roll(x, (-s) % n, 0)
    t = lax.broadcasted_iota(jnp.int32, x.shape, 0)
    ok = (t + s >= 0) & (t + s < n)
    return jnp.where(ok, rolled, 0.0)


@functools.partial(jax.custom_vjp, nondiff_argnums=(1,))
def _shift_rows(x, s):
    return _shift_rows_raw(x, s)


_shift_rows.defvjp(lambda x, s: (_shift_rows_raw(x, s), None), lambda s, _, g: (_shift_rows_raw(g, -s),))


def _sigmoid(x):
    return 1.0 / (1.0 + jnp.exp(-x))


def _silu(x):
    return x * _sigmoid(x)


def _gelu(x):
    return 0.5 * x * (1.0 + jnp.tanh(math.sqrt(2.0 / math.pi) * (x + 0.044715 * x * x * x)))


def _softplus(x):
    return jnp.maximum(x, 0.0) + jnp.log(1.0 + jnp.exp(-jnp.abs(x)))


def _rms(x, g):
    return x * lax.rsqrt(jnp.mean(x * x, axis=-1, keepdims=True) + NORM_EPS) * g


def _conv_rows(x, taps):
    pad = len(taps) // 2
    acc = None
    for j, w in enumerate(taps):
        term = _shift_rows(x, j - pad) * w
        acc = term if acc is None else acc + term
    return acc


@jax.custom_vjp
def _inv_unit(mats):
    n = mats[0].shape[0]
    eye = (lax.broadcasted_iota(jnp.int32, (n, n), 0) == lax.broadcasted_iota(jnp.int32, (n, n), 1)).astype(F32)
    ps = [eye - a for a in mats]
    aks = list(mats)
    for _ in range(int(math.log2(n)) - 1):
        aks = [_mmh(ak, ak) for ak in aks]
        ps = [p + _mmh(p, ak) for p, ak in zip(ps, aks)]
    return ps


def _inv_unit_fwd(mats):
    ts = _inv_unit(mats)
    return ts, ts


def _inv_unit_bwd(ts, gs):
    inner = [_mmh_nt(g, t) for g, t in zip(gs, ts)]
    return ([-_mmh_tn(t, m) for t, m in zip(ts, inner)],)


_inv_unit.defvjp(_inv_unit_fwd, _inv_unit_bwd)


def _tiled(fn, name, grid, ins, outs, sem=None, after=()):
    n_in = len(ins)

    def body(*refs):
        vals = [r[...] for r in refs[:n_in]]
        res = fn(*vals)
        if not isinstance(res, (tuple, list)):
            res = (res,)
        for o_ref, r, spec in zip(refs[n_in + len(after):], res, outs):
            acc = spec[4]
            if not acc:
                o_ref[...] = r.astype(o_ref.dtype)
            else:
                first = functools.reduce(jnp.logical_and, [pl.program_id(a) == 0 for a in acc])

                @pl.when(first)
                def _(o_ref=o_ref, r=r):
                    o_ref[...] = r.astype(o_ref.dtype)

                @pl.when(jnp.logical_not(first))
                def _(o_ref=o_ref, r=r):
                    o_ref[...] += r.astype(o_ref.dtype)

    return pl.pallas_call(
        body, name=name, grid=grid,
        in_specs=[pl.BlockSpec(b, m) for _, b, m in ins] + [pl.BlockSpec(memory_space=pl.ANY)] * len(after),
        out_specs=[pl.BlockSpec(s[2], s[3]) for s in outs],
        out_shape=[jax.ShapeDtypeStruct(s[0], s[1]) for s in outs],
        compiler_params=_params(sem),
    )(*[a for a, _, _ in ins], *after)


def _matmul(a, b, mode, out_dtype, name, add=None, a_split=1, b_split=1, o_split=1, after=()):
    def dims2(x, split):
        return (x.shape[-2], x.shape[-1] * split)

    ar, ac = dims2(a, a_split)
    br, bc = dims2(b, b_split)
    if mode == 'nn':
        M, K, N = ar, ac, bc
        assert br == K
    elif mode == 'nt':
        M, K, N = ar, ac, br
        assert bc == K
    else:
        K, M, N = ar, ac, bc
        assert br == K
    tm = _pick(M // (a_split if mode == 'tn' else 1), 512)
    tn = _pick(N // max(o_split, b_split if mode != 'nt' else 1), 1408)
    tk = _pick(K // max(a_split if mode != 'tn' else 1, b_split if mode == 'nt' else 1), 2048)
    gm, gn, gk = M // tm, N // tn, K // tk

    def col_map(split, total_cols, tile):
        per = total_cols // split // tile

        def f(r, c):
            return (c // per, r, c % per) if split > 1 else (r, c)
        return f

    if mode == 'nn':
        a_idx = col_map(a_split, K, tk)
        b_idx = col_map(b_split, N, tn)
        a_spec = pl.BlockSpec(((None,) if a_split > 1 else ()) + (tm, tk), lambda m, n, k: a_idx(m, k))
        b_spec = pl.BlockSpec(((None,) if b_split > 1 else ()) + (tk, tn), lambda m, n, k: b_idx(k, n))
        dn = ((1,), (0,))
    elif mode == 'nt':
        a_idx = col_map(a_split, K, tk)
        b_idx = col_map(b_split, K, tk)
        a_spec = pl.BlockSpec(((None,) if a_split > 1 else ()) + (tm, tk), lambda m, n, k: a_idx(m, k))
        b_spec = pl.BlockSpec(((None,) if b_split > 1 else ()) + (tn, tk), lambda m, n, k: b_idx(n, k))
        dn = ((1,), (1,))
    else:
        a_idx = col_map(a_split, M, tm)
        b_idx = col_map(b_split, N, tn)
        a_spec = pl.BlockSpec(((None,) if a_split > 1 else ()) + (tk, tm), lambda m, n, k: a_idx(k, m))
        b_spec = pl.BlockSpec(((None,) if b_split > 1 else ()) + (tk, tn), lambda m, n, k: b_idx(k, n))
        dn = ((0,), (0,))
    o_idx = col_map(o_split, N, tn)
    o_block = ((None,) if o_split > 1 else ()) + (tm, tn)
    o_spec = pl.BlockSpec(o_block, lambda m, n, k: o_idx(m, n))
    o_shape = ((o_split, M, N // o_split) if o_split > 1 else (M, N))
    in_specs = [a_spec, b_spec]
    args = [a, b]
    if add is not None:
        in_specs.append(pl.BlockSpec((tm, tn), lambda m, n, k: (m, n)))
        args.append(add)
    n_lead = len(args)
    for t in after:
        in_specs.append(pl.BlockSpec(memory_space=pl.ANY))
        args.append(t)

    def body(*refs):
        a_ref, b_ref = refs[0], refs[1]
        add_ref = refs[2] if add is not None else None
        o_ref = refs[n_lead + len(after)]
        part = lax.dot_general(a_ref[...].astype(BF16), b_ref[...].astype(BF16), (dn, ((), ())),
                               preferred_element_type=F32)

        def finish(total):
            if add_ref is not None:
                total = total + add_ref[...]
            o_ref[...] = total.astype(o_ref.dtype)

        if gk == 1:
            finish(part)
        else:
            acc_ref = refs[-1]
            k = pl.program_id(2)

            @pl.when(k == 0)
            def _():
                acc_ref[...] = part

            @pl.when(jnp.logical_and(k > 0, k < gk - 1))
            def _():
                acc_ref[...] += part

            @pl.when(k == gk - 1)
            def _():
                finish(acc_ref[...] + part)

    return pl.pallas_call(
        body, name=name, grid=(gm, gn, gk), in_specs=in_specs, out_specs=o_spec,
        out_shape=jax.ShapeDtypeStruct(o_shape, out_dtype),
        scratch_shapes=([pltpu.VMEM((tm, tn), F32)] if gk > 1 else []),
        compiler_params=_params(("parallel", "parallel", "arbitrary")),
    )(*args)


def _row_tile():
    return _pick(SEQ, 256, 8)


def _rms_fwd(x, g, name):
    tm = _row_tile()
    (h,) = _tiled(lambda xv, gv: _rms(xv, gv), name, (SEQ // tm,),
                  [(x, (tm, D_MODEL), lambda i: (i, 0)), (g, (1, D_MODEL), lambda i: (0, 0))],
                  [((SEQ, D_MODEL), BF16, (tm, D_MODEL), lambda i: (i, 0), ())])
    return h


def _rms_bwd(x, g, dh, dres, name):
    tm = _row_tile()

    def fn(xv, gv, dhv, drv):
        _, vjp = jax.vjp(_rms, xv, gv)
        dx, dg = vjp(dhv)
        dx = dx + drv
        return dx, dx, dg

    row = lambda i: (i, 0)
    return _tiled(fn, name, (SEQ // tm,),
                  [(x, (tm, D_MODEL), row), (g, (1, D_MODEL), lambda i: (0, 0)),
                   (dh, (tm, D_MODEL), row), (dres, (tm, D_MODEL), row)],
                  [((SEQ, D_MODEL), F32, (tm, D_MODEL), row, ()),
                   ((SEQ, D_MODEL), BF16, (tm, D_MODEL), row, ()),
                   ((1, D_MODEL), F32, (1, D_MODEL), lambda i: (0, 0), (0,))])


def _loss_head(x, g, target):
    tm = _row_tile()

    def fn(xv, gv, tv):
        y, vjp = jax.vjp(_rms, xv, gv)
        err = y - tv
        part = 0.5 * jnp.sum(jnp.sum(err * err, axis=1, keepdims=True), axis=0, keepdims=True) / D_MODEL
        dx, dg = vjp(err / D_MODEL)
        return part, dx, dx, dg

    row = lambda i: (i, 0)
    one = lambda i: (0, 0)
    return _tiled(fn, "loss_head", (SEQ // tm,),
                  [(x, (tm, D_MODEL), row), (g, (1, D_MODEL), one), (target, (tm, D_MODEL), row)],
                  [((1, 1), F32, (1, 1), one, (0,)),
                   ((SEQ, D_MODEL), F32, (tm, D_MODEL), row, ()),
                   ((SEQ, D_MODEL), BF16, (tm, D_MODEL), row, ()),
                   ((1, D_MODEL), F32, (1, D_MODEL), one, (0,))])


def _p_col(width, index):
    return lambda tm: ((tm, width), lambda i: (i, index))


def _merge_fn(ga, gb, ta, tb):
    return _sigmoid(ga) * ta + _sigmoid(gb) * tb


def _merge_fwd(p, ta, tb):
    tm = _row_tile()
    d = D_MODEL
    ga_blk = 4 * _gw() + 2 * _sw()
    assert ga_blk % d == 0
    ia, ib = ga_blk // d, ga_blk // d + 1
    row = lambda i: (i, 0)
    (m,) = _tiled(_merge_fn, "merge_fwd", (SEQ // tm,),
                  [(p, (tm, d), lambda i: (i, ia)), (p, (tm, d), lambda i: (i, ib)),
                   (ta, (tm, d), row), (tb, (tm, d), row)],
                  [((SEQ, d), BF16, (tm, d), row, ())])
    return m


def _merge_bwd(p, ta, tb, dm):
    tm = _row_tile()
    d = D_MODEL
    ga_blk = 4 * _gw() + 2 * _sw()
    ia, ib = ga_blk // d, ga_blk // d + 1

    def fn(ga, gb, tav, tbv, dmv):
        _, vjp = jax.vjp(_merge_fn, ga, gb, tav, tbv)
        return vjp(dmv)

    row = lambda i: (i, 0)
    out = ((SEQ, d), BF16, (tm, d), row, ())
    return _tiled(fn, "merge_bwd", (SEQ // tm,),
                  [(p, (tm, d), lambda i: (i, ia)), (p, (tm, d), lambda i: (i, ib)),
                   (ta, (tm, d), row), (tb, (tm, d), row), (dm, (tm, d), row)],
                  [out, out, out, out])


def _gate_fn(ab, alog, dtb):
    lane = lax.broadcasted_iota(jnp.int32, ab.shape, 1)
    nd = 2 * GDN_HEADS
    g = -jnp.exp(alog) * _softplus(ab + dtb)
    beta = _sigmoid(ab)
    return jnp.where(lane < nd, g, jnp.where(lane < 2 * nd, beta, 0.0))


def _ab_index():
    off = 4 * _gw() + 2 * _sw() + 2 * D_MODEL
    assert off % AB_PAD == 0
    return off // AB_PAD


def _gate_fwd(p, alog, dtb):
    tm = _row_tile()
    iab = _ab_index()
    one = lambda i: (0, 0)
    (g,) = _tiled(_gate_fn, "gdn_gate_fwd", (SEQ // tm,),
                  [(p, (tm, AB_PAD), lambda i: (i, iab)), (alog, (1, AB_PAD), one), (dtb, (1, AB_PAD), one)],
                  [((SEQ, AB_PAD), F32, (tm, AB_PAD), lambda i: (i, 0), ())])
    return g


def _gate_bwd(p, alog, dtb, dg):
    tm = _row_tile()
    iab = _ab_index()
    one = lambda i: (0, 0)

    def fn(ab, al, db, dgv):
        _, vjp = jax.vjp(_gate_fn, ab, al, db)
        return vjp(dgv)

    return _tiled(fn, "gdn_gate_bwd", (SEQ // tm,),
                  [(p, (tm, AB_PAD), lambda i: (i, iab)), (alog, (1, AB_PAD), one), (dtb, (1, AB_PAD), one),
                   (dg, (tm, AB_PAD), lambda i: (i, 0))],
                  [((SEQ, AB_PAD), BF16, (tm, AB_PAD), lambda i: (i, 0), ()),
                   ((1, AB_PAD), F32, (1, AB_PAD), one, (0,)),
                   ((1, AB_PAD), F32, (1, AB_PAD), one, (0,))])


def _l2n(x):
    return x * lax.rsqrt(jnp.sum(x * x, axis=-1, keepdims=True) + NORM_EPS)


def _qkv_fn(xq, xk, xv, *taps):
    n = QKV_CONV
    q = _l2n(_silu(_conv_rows(xq, taps[0:n])))
    k = _l2n(_silu(_conv_rows(xk, taps[n:2 * n])))
    v = _silu(_conv_rows(xv, taps[2 * n:3 * n]))
    return q, k, v


def _qkv_specs(p, conv_w):
    hd, nh = HEAD_DIM, GDN_HEADS
    ins = [(p, (SEQ, hd), (lambda h, s=s: (0, s * nh + h))) for s in range(3)]
    ins += [(conv_w, (QKV_CONV, hd), (lambda h, s=s: (0, s * nh + h))) for s in range(3)]
    return ins


def _qkv_fwd(p, conv_w):
    hd, nh = HEAD_DIM, GDN_HEADS
    ins = _qkv_specs(p, conv_w)
    n_in = len(ins)
    out_spec = pl.BlockSpec((SEQ, hd), lambda h: (0, h))

    def body(*refs):
        xs = [r[...] for r in refs[:3]]
        taps = [refs[3 + s][j:j + 1, :] for s in range(3) for j in range(QKV_CONV)]
        q, k, v = _qkv_fn(*xs, *taps)
        refs[n_in][...] = q
        refs[n_in + 1][...] = k
        refs[n_in + 2][...] = v

    return pl.pallas_call(
        body, name="gdn_qkv_fwd", grid=(nh,),
        in_specs=[pl.BlockSpec(b, m) for _, b, m in ins], out_specs=[out_spec] * 3,
        out_shape=[jax.ShapeDtypeStruct((SEQ, nh * hd), F32)] * 3,
        compiler_params=_params(("parallel",)),
    )(*[a for a, _, _ in ins])


def _qkv_bwd(p, conv_w, dq, dk, dv):
    hd, nh = HEAD_DIM, GDN_HEADS
    ins = _qkv_specs(p, conv_w) + [(t, (SEQ, hd), lambda h: (0, h)) for t in (dq, dk, dv)]
    n_in = len(ins)

    def body(*refs):
        xs = [r[...] for r in refs[:3]]
        taps = [refs[3 + s][j:j + 1, :] for s in range(3) for j in range(QKV_CONV)]
        cts = tuple(r[...] for r in refs[6:9])
        _, vjp = jax.vjp(_qkv_fn, *xs, *taps)
        grads = vjp(cts)
        for s in range(3):
            refs[n_in + s][...] = grads[s].astype(BF16)
            for j in range(QKV_CONV):
                refs[n_in + 3 + s][j:j + 1, :] = jnp.sum(grads[3 + s * QKV_CONV + j], axis=0, keepdims=True)

    dx_spec = pl.BlockSpec((SEQ, hd), lambda h: (0, h))
    dw_spec = pl.BlockSpec((QKV_CONV, hd), lambda h: (0, h))
    outs = pl.pallas_call(
        body, name="gdn_qkv_bwd", grid=(nh,),
        in_specs=[pl.BlockSpec(b, m) for _, b, m in ins], out_specs=[dx_spec] * 3 + [dw_spec] * 3,
        out_shape=[jax.ShapeDtypeStruct((SEQ, nh * hd), BF16)] * 3
        + [jax.ShapeDtypeStruct((QKV_CONV, nh * hd), F32)] * 3,
        compiler_params=_params(("parallel",)),
    )(*[a for a, _, _ in ins])
    return outs[0:3], jnp.concatenate(outs[3:6], axis=1)


def _prep_fn(qs, ks, vs, gblk):
    nh = len(qs)
    c = qs[0].shape[0]
    scale = HEAD_DIM ** -0.5
    ii = lax.broadcasted_iota(jnp.int32, (c, c), 0)
    jj = lax.broadcasted_iota(jnp.int32, (c, c), 1)
    eye = (ii == jj).astype(F32)
    lane = lax.broadcasted_iota(jnp.int32, gblk.shape, 1)
    kk_t = [_mm_nt(k, k) for k in ks]
    qk_t = [_mm_nt(q, k) for q, k in zip(qs, ks)]
    chains = [(h, d) for h in range(nh) for d in range(2)]
    pre = []
    for h, d in chains:
        g = jnp.sum(jnp.where(lane == h + d * nh, gblk, 0.0), axis=1, keepdims=True)
        beta = jnp.sum(jnp.where(lane == h + (2 + d) * nh, gblk, 0.0), axis=1, keepdims=True)
        incl = (jj <= ii) if d == 0 else (jj >= ii)
        strict = (jj < ii) if d == 0 else (jj > ii)
        incl_t = (ii <= jj) if d == 0 else (ii >= jj)
        g_row = jnp.sum(eye * g, axis=0, keepdims=True)
        gc = jnp.sum(jnp.where(incl, g_row, 0.0), axis=1, keepdims=True)
        gc_row = jnp.sum(jnp.where(incl_t, g, 0.0), axis=0, keepdims=True)
        decay = jnp.where(incl, jnp.exp(jnp.where(incl, gc - gc_row, 0.0)), 0.0)
        a = jnp.where(strict, kk_t[h] * beta * decay, 0.0)
        pre.append((g, beta, gc, decay, incl, a))
    ts = _inv_unit([p[5] for p in pre])
    us = [_mmh(t, vs[h] * p[1]) for (h, d), p, t in zip(chains, pre, ts)]
    ws = [_mmh(t, ks[h] * (p[1] * jnp.exp(p[2]))) for (h, d), p, t in zip(chains, pre, ts)]
    outs = [[None, None] for _ in range(nh)]
    for (h, d), (g, beta, gc, decay, incl, a), u, w in zip(chains, pre, us, ws):
        qk = jnp.where(incl, qk_t[h] * (scale * decay), 0.0)
        g_last = jnp.sum(g, axis=0, keepdims=True)
        q_dec = qs[h] * (scale * jnp.exp(gc))
        k_dec = ks[h] * jnp.exp(g_last - gc)
        outs[h][d] = (u, w, q_dec, k_dec, qk, jnp.exp(g_last))
    return outs


def _n_chunks():
    return SEQ // GDN_CHUNK


def _gdn_prep_fwd(q, k, v, gates):
    c, hd, nh, nc = GDN_CHUNK, HEAD_DIM, GDN_HEADS, _n_chunks()
    gw = nh * hd
    row = lambda n: (n, 0)

    def body(q_ref, k_ref, v_ref, g_ref, u_ref, w_ref, qd_ref, kd_ref, qk_ref, dec_ref):
        heads = [slice(h * hd, (h + 1) * hd) for h in range(nh)]
        res = _prep_fn([q_ref[:, s] for s in heads], [k_ref[:, s] for s in heads], [v_ref[:, s] for s in heads],
                       g_ref[...])
        for h, cols in enumerate(heads):
            for d in range(2):
                u, w, qd, kd, qk, dec = res[h][d]
                u_ref[d, :, cols] = u
                w_ref[d, :, cols] = w
                qd_ref[d, :, cols] = qd
                kd_ref[d, :, cols] = kd
                qk_ref[d, h] = qk
                dec_ref[d, h:h + 1, :] = jnp.broadcast_to(dec, (1, LANES))

    wide = pl.BlockSpec((2, c, gw), lambda n: (0, n, 0))
    return pl.pallas_call(
        body, name="gdn_prep_fwd", grid=(nc,),
        in_specs=[pl.BlockSpec((c, gw), row)] * 3 + [pl.BlockSpec((c, LANES), row)],
        out_specs=[wide] * 4 + [pl.BlockSpec((2, nh, c, c), lambda n: (0, 0, n, 0)),
                                pl.BlockSpec((2, None, nh, LANES), lambda n: (0, n, 0, 0))],
        out_shape=[jax.ShapeDtypeStruct((2, SEQ, gw), F32)] * 4
        + [jax.ShapeDtypeStruct((2, nh, SEQ, c), F32), jax.ShapeDtypeStruct((2, nc, nh, LANES), F32)],
        compiler_params=_params(("parallel",)),
    )(q, k, v, gates)


def _gdn_prep_bwd(q, k, v, gates, cts):
    c, hd, nh, nc = GDN_CHUNK, HEAD_DIM, GDN_HEADS, _n_chunks()
    gw = nh * hd
    row = lambda n: (n, 0)

    def body(*refs):
        q_ref, k_ref, v_ref, g_ref = refs[:4]
        ct_refs = refs[4:16]
        dq_ref, dk_ref, dv_ref, dg_ref = refs[16:20]
        heads = [slice(h * hd, (h + 1) * hd) for h in range(nh)]
        _, vjp = jax.vjp(_prep_fn, [q_ref[:, s] for s in heads], [k_ref[:, s] for s in heads],
                         [v_ref[:, s] for s in heads], g_ref[...])
        cts_in = []
        for h, cols in enumerate(heads):
            per_dir = []
            for d in range(2):
                r = ct_refs[6 * d:6 * d + 6]
                per_dir.append((r[0][:, cols], r[1][:, cols], r[2][:, cols], r[3][:, cols], r[4][h],
                                r[5][h:h + 1, 0:1]))
            cts_in.append(per_dir)
        dqs, dks, dvs, dg = vjp(cts_in)
        for h, cols in enumerate(heads):
            dq_ref[:, cols] = dqs[h]
            dk_ref[:, cols] = dks[h]
            dv_ref[:, cols] = dvs[h]
        dg_ref[...] = dg

    one_dir = [pl.BlockSpec((c, gw), row)] * 4 + [pl.BlockSpec((nh, c, c), lambda n: (0, n, 0)),
                                                  pl.BlockSpec((None, nh, LANES), lambda n: (n, 0, 0))]
    return pl.pallas_call(
        body, name="gdn_prep_bwd", grid=(nc,),
        in_specs=[pl.BlockSpec((c, gw), row)] * 3 + [pl.BlockSpec((c, LANES), row)] + one_dir * 2,
        out_specs=[pl.BlockSpec((c, gw), row)] * 3 + [pl.BlockSpec((c, LANES), row)],
        out_shape=[jax.ShapeDtypeStruct((SEQ, gw), F32)] * 3 + [jax.ShapeDtypeStruct((SEQ, LANES), F32)],
        compiler_params=_params(("parallel",)),
    )(q, k, v, gates, *cts)


def _scan_steps(states, us, ws, qds, kds, qks, decs):
    w_s = [_mm(w, s) for w, s in zip(ws, states)]
    q_s = [_mm(qd, s) for qd, s in zip(qds, states)]
    v_new = [u - x for u, x in zip(us, w_s)]
    outs = [x + _mm(qk, vn) for x, qk, vn in zip(q_s, qks, v_new)]
    k_v = [_mm_tn(kd, vn) for kd, vn in zip(kds, v_new)]
    new_states = [s * dec + x for s, dec, x in zip(states, decs, k_v)]
    return new_states, outs


def _scan_operands(ins, chains, dec_lanes):
    cols = lambda h: slice(h * HEAD_DIM, (h + 1) * HEAD_DIM)
    wide = [[ins[6 * d + j][:, cols(h)] for d, h in chains] for j in range(4)]
    qks = [ins[6 * d + 4][h] for d, h in chains]
    decs = [ins[6 * d + 5][h:h + 1, dec_lanes] for d, h in chains]
    return (*wide, qks, decs)


def _scan_in_specs(chunk_of):
    c, hd, nh = GDN_CHUNK, HEAD_DIM, GDN_HEADS
    gw = nh * hd
    specs = []
    for d in range(2):
        f = chunk_of[d]
        specs += [pl.BlockSpec((None, c, gw), lambda n, d=d, f=f: (d, f(n), 0))] * 4
        specs += [pl.BlockSpec((None, nh, c, c), lambda n, d=d, f=f: (d, 0, f(n), 0)),
                  pl.BlockSpec((None, None, nh, LANES), lambda n, d=d, f=f: (d, f(n), 0, 0))]
    return specs


def _gdn_scan_fwd(u, w, qd, kd, qk, dec):
    c, hd, nh, nc = GDN_CHUNK, HEAD_DIM, GDN_HEADS, _n_chunks()
    gw = nh * hd
    chunk_of = (lambda n: n, lambda n: nc - 1 - n)

    def body(*refs):
        ins = refs[:12]
        o_refs = refs[12:14]
        st_refs = refs[14:16]
        s_ref = refs[16]

        @pl.when(pl.program_id(0) == 0)
        def _():
            s_ref[...] = jnp.zeros_like(s_ref)

        chains = [(d, h) for d in range(2) for h in range(nh)]
        states = [s_ref[d * nh + h] for d, h in chains]
        for (d, h), st in zip(chains, states):
            st_refs[d][h] = st
        new_states, outs = _scan_steps(states, *_scan_operands(ins, chains, slice(None)))
        for (d, h), st, o in zip(chains, new_states, outs):
            s_ref[d * nh + h] = st
            o_refs[d][:, h * hd:(h + 1) * hd] = o

    o_specs = [pl.BlockSpec((c, gw), lambda n, f=f: (f(n), 0)) for f in chunk_of]
    st_specs = [pl.BlockSpec((None, nh, hd, hd), lambda n, f=f: (f(n), 0, 0, 0)) for f in chunk_of]
    return pl.pallas_call(
        body, name="gdn_scan_fwd", grid=(nc,), in_specs=_scan_in_specs(chunk_of),
        out_specs=o_specs + st_specs,
        out_shape=[jax.ShapeDtypeStruct((SEQ, gw), F32)] * 2 + [jax.ShapeDtypeStruct((nc, nh, hd, hd), F32)] * 2,
        scratch_shapes=[pltpu.VMEM((2 * nh, hd, hd), F32)],
        compiler_params=_params(("arbitrary",)),
    )(*([u, w, qd, kd, qk, dec] * 2))


def _gdn_scan_bwd(u, w, qd, kd, qk, dec, st0, st1, do):
    c, hd, nh, nc = GDN_CHUNK, HEAD_DIM, GDN_HEADS, _n_chunks()
    gw = nh * hd
    chunk_of = (lambda n: nc - 1 - n, lambda n: n)

    def body(*refs):
        ins = refs[:12]
        st_in = refs[12:14]
        do_in = refs[14:16]
        outs = refs[16:28]
        ds_ref = refs[28]

        @pl.when(pl.program_id(0) == 0)
        def _():
            ds_ref[...] = jnp.zeros_like(ds_ref)

        chains = [(d, h) for d in range(2) for h in range(nh)]
        states = [st_in[d][h] for d, h in chains]
        _, vjp = jax.vjp(_scan_steps, states, *_scan_operands(ins, chains, slice(0, 1)))
        ct_state = [ds_ref[d * nh + h] for d, h in chains]
        ct_out = [do_in[d][:, h * hd:(h + 1) * hd] for d, h in chains]
        grads = vjp((ct_state, ct_out))
        for i, (d, h) in enumerate(chains):
            cols = slice(h * hd, (h + 1) * hd)
            du_r, dw_r, dqd_r, dkd_r, dqk_r, ddec_r = outs[6 * d:6 * d + 6]
            ds_ref[d * nh + h] = grads[0][i]
            du_r[:, cols] = grads[1][i]
            dw_r[:, cols] = grads[2][i]
            dqd_r[:, cols] = grads[3][i]
            dkd_r[:, cols] = grads[4][i]
            dqk_r[h] = grads[5][i]
            ddec_r[h:h + 1, :] = jnp.broadcast_to(grads[6][i], (1, LANES))

    st_specs = [pl.BlockSpec((None, nh, hd, hd), lambda n, f=f: (f(n), 0, 0, 0)) for f in chunk_of]
    do_specs = [pl.BlockSpec((c, gw), lambda n, f=f: (f(n), 0)) for f in chunk_of]
    out_specs, out_shape = [], []
    for f in chunk_of:
        out_specs += [pl.BlockSpec((c, gw), lambda n, f=f: (f(n), 0))] * 4
        out_specs += [pl.BlockSpec((nh, c, c), lambda n, f=f: (0, f(n), 0)),
                      pl.BlockSpec((None, nh, LANES), lambda n, f=f: (f(n), 0, 0))]
        out_shape += [jax.ShapeDtypeStruct((SEQ, gw), F32)] * 4
        out_shape += [jax.ShapeDtypeStruct((nh, SEQ, c), F32), jax.ShapeDtypeStruct((nc, nh, LANES), F32)]
    return pl.pallas_call(
        body, name="gdn_scan_bwd", grid=(nc,), in_specs=_scan_in_specs(chunk_of) + st_specs + do_specs,
        out_specs=out_specs, out_shape=out_shape,
        scratch_shapes=[pltpu.VMEM((2 * nh, hd, hd), F32)],
        compiler_params=_params(("arbitrary",)),
    )(*([u, w, qd, kd, qk, dec] * 2), st0, st1, do, do)


def _post_fn(o0, o1, z, gn):
    return _rms(o0 + o1, gn) * _silu(z)


def _gdn_post_fwd(o0, o1, p, gn):
    tm, hd, nh = _row_tile(), HEAD_DIM, GDN_HEADS
    zoff = 3 * nh
    blk = lambda i, h: (i, h)
    (ya,) = _tiled(_post_fn, "gdn_post_fwd", (SEQ // tm, nh),
                   [(o0, (tm, hd), blk), (o1, (tm, hd), blk),
                    (p, (tm, hd), lambda i, h: (i, zoff + h)), (gn, (1, hd), lambda i, h: (0, 0))],
                   [((SEQ, nh * hd), BF16, (tm, hd), blk, ())])
    return ya


def _gdn_post_bwd(o0, o1, p, gn, dya):
    tm, hd, nh = _row_tile(), HEAD_DIM, GDN_HEADS
    zoff = 3 * nh
    blk = lambda i, h: (i, h)

    def fn(a, b, z, g, dy):
        _, vjp = jax.vjp(_post_fn, a, b, z, g)
        do, _, dz, dg = vjp(dy.astype(F32))
        return do, dz, dg

    return _tiled(fn, "gdn_post_bwd", (SEQ // tm, nh),
                  [(o0, (tm, hd), blk), (o1, (tm, hd), blk),
                   (p, (tm, hd), lambda i, h: (i, zoff + h)), (gn, (1, hd), lambda i, h: (0, 0)),
                   (dya, (tm, hd), blk)],
                  [((SEQ, nh * hd), F32, (tm, hd), blk, ()),
                   ((SEQ, nh * hd), BF16, (tm, hd), blk, ()),
                   ((1, hd), F32, (1, hd), lambda i, h: (0, 0), (0, 1))],
                  sem=("arbitrary", "arbitrary"))


def _sgu_ln(v, lg, lb):
    gv = _gelu(v)
    mu = jnp.mean(gv, axis=-1, keepdims=True)
    cen = gv - mu
    var = jnp.mean(cen * cen, axis=-1, keepdims=True)
    return cen * lax.rsqrt(var + NORM_EPS) * lg + lb


def _sgu_mix(vn, w_ref, bt_ref):
    parts = []
    for g in range(SGU_GROUPS):
        cols = slice(g * SGU_DIM, (g + 1) * SGU_DIM)
        parts.append(_mm(w_ref[g], vn[:, cols]) + bt_ref[:, g:g + 1])
    return jnp.concatenate(parts, axis=1)


def _sgu_specs(p):
    sw, sb = _sw(), SGU_BLOCK
    uoff = 4 * _gw()
    assert uoff % sw == 0
    iu = uoff // sw
    return [(p, (sb, sw), lambda n: (n, iu)), (p, (sb, sw), lambda n: (n, iu + 1))]


def _sgu_fwd(p, lg, lb, w_s, b_t):
    sw, sb = _sw(), SGU_BLOCK
    ins = _sgu_specs(p)

    def body(u_ref, v_ref, lg_ref, lb_ref, w_ref, bt_ref, y_ref):
        vn = _sgu_ln(v_ref[...], lg_ref[...], lb_ref[...])
        y_ref[...] = (_gelu(u_ref[...]) * _sgu_mix(vn, w_ref, bt_ref)).astype(BF16)

    one2 = lambda n: (0, 0)
    return pl.pallas_call(
        body, name="sgu_fwd", grid=(SEQ // sb,),
        in_specs=[pl.BlockSpec(b, m) for _, b, m in ins]
        + [pl.BlockSpec((1, sw), one2), pl.BlockSpec((1, sw), one2),
           pl.BlockSpec((SGU_GROUPS, sb, sb), lambda n: (0, 0, 0)), pl.BlockSpec((sb, LANES), one2)],
        out_specs=pl.BlockSpec((sb, sw), lambda n: (n, 0)),
        out_shape=jax.ShapeDtypeStruct((SEQ, sw), BF16),
        compiler_params=_params(("parallel",)),
    )(p, p, lg, lb, w_s, b_t)


def _sgu_bwd(p, lg, lb, w_s, b_t, dy):
    sw, sb, ng = _sw(), SGU_BLOCK, SGU_GROUPS
    ins = _sgu_specs(p)

    def body(u_ref, v_ref, lg_ref, lb_ref, w_ref, bt_ref, dy_ref, du_ref, dv_ref, dlg_ref, dlb_ref, dw_ref, dbt_ref):
        first = pl.program_id(0) == 0
        u, v = u_ref[...], v_ref[...]
        gu, gelu_vjp = jax.vjp(_gelu, u)
        vn, ln_vjp = jax.vjp(_sgu_ln, v, lg_ref[...], lb_ref[...])
        s = _sgu_mix(vn, w_ref, bt_ref)
        dyv = dy_ref[...].astype(F32)
        ds = dyv * gu
        (du,) = gelu_vjp(dyv * s)
        lane = lax.broadcasted_iota(jnp.int32, (sb, LANES), 1)
        dvn_parts = []
        dbt = jnp.zeros((sb, LANES), F32)
        for g in range(ng):
            cols = slice(g * SGU_DIM, (g + 1) * SGU_DIM)
            ds_g = ds[:, cols]
            dw_g = _mm_nt(ds_g, vn[:, cols])
            dvn_parts.append(_mm_tn(w_ref[g], ds_g))
            dbt = dbt + jnp.where(lane == g, jnp.sum(ds_g, axis=1, keepdims=True), 0.0)

            @pl.when(first)
            def _(g=g, dw_g=dw_g):
                dw_ref[g] = dw_g

            @pl.when(jnp.logical_not(first))
            def _(g=g, dw_g=dw_g):
                dw_ref[g] += dw_g

        dv, dlg, dlb = ln_vjp(jnp.concatenate(dvn_parts, axis=1))
        du_ref[...] = du.astype(BF16)
        dv_ref[...] = dv.astype(BF16)

        @pl.when(first)
        def _():
            dlg_ref[...] = dlg
            dlb_ref[...] = dlb
            dbt_ref[...] = dbt

        @pl.when(jnp.logical_not(first))
        def _():
            dlg_ref[...] += dlg
            dlb_ref[...] += dlb
            dbt_ref[...] += dbt

    one2 = lambda n: (0, 0)
    row = pl.BlockSpec((sb, sw), lambda n: (n, 0))
    return pl.pallas_call(
        body, name="sgu_bwd", grid=(SEQ // sb,),
        in_specs=[pl.BlockSpec(b, m) for _, b, m in ins]
        + [pl.BlockSpec((1, sw), one2), pl.BlockSpec((1, sw), one2),
           pl.BlockSpec((ng, sb, sb), lambda n: (0, 0, 0)), pl.BlockSpec((sb, LANES), one2), row],
        out_specs=[row, row, pl.BlockSpec((1, sw), one2), pl.BlockSpec((1, sw), one2),
                   pl.BlockSpec((ng, sb, sb), lambda n: (0, 0, 0)), pl.BlockSpec((sb, LANES), one2)],
        out_shape=[jax.ShapeDtypeStruct((SEQ, sw), BF16)] * 2 + [jax.ShapeDtypeStruct((1, sw), F32)] * 2
        + [jax.ShapeDtypeStruct((ng, sb, sb), F32), jax.ShapeDtypeStruct((sb, LANES), F32)],
        compiler_params=_params(("arbitrary",)),
    )(p, p, lg, lb, w_s, b_t, dy)


def _ffn_fn(xg, xv, bg, bv, *taps):
    n = FFN_CONV
    cg = _conv_rows(xg, taps[0:n]) + bg
    cv = _conv_rows(xv, taps[n:2 * n]) + bv
    return _silu(cg) * cv


def _ffn_tile():
    return _pick(D_FF, 256)


def _ffn_specs(up, conv_w, conv_b):
    tc = _ffn_tile()
    nt = D_FF // tc
    ins = [(up, (None, SEQ, tc), (lambda j, s=s: (s, 0, j))) for s in range(2)]
    ins += [(conv_b, (1, tc), (lambda j, s=s: (0, s * nt + j))) for s in range(2)]
    ins += [(conv_w, (FFN_CONV, tc), (lambda j, s=s: (0, s * nt + j))) for s in range(2)]
    return ins


def _ffn_act_fwd(up, conv_w, conv_b):
    tc = _ffn_tile()
    ins = _ffn_specs(up, conv_w, conv_b)

    def body(xg, xv, bg, bv, wg, wv, o_ref):
        taps = [w[j:j + 1, :] for w in (wg, wv) for j in range(FFN_CONV)]
        o_ref[...] = _ffn_fn(xg[...], xv[...], bg[...], bv[...], *taps).astype(BF16)

    return pl.pallas_call(
        body, name="ffn_act_fwd", grid=(D_FF // tc,),
        in_specs=[pl.BlockSpec(b, m) for _, b, m in ins],
        out_specs=pl.BlockSpec((SEQ, tc), lambda j: (0, j)),
        out_shape=jax.ShapeDtypeStruct((SEQ, D_FF), BF16),
        compiler_params=_params(("parallel",)),
    )(*[a for a, _, _ in ins])


def _ffn_act_bwd(up, conv_w, conv_b, dact):
    tc = _ffn_tile()
    nt = D_FF // tc
    ins = _ffn_specs(up, conv_w, conv_b) + [(dact, (SEQ, tc), lambda j: (0, j))]

    def body(xg, xv, bg, bv, wg, wv, dact_ref, dup_ref, dwg_ref, dwv_ref, dbg_ref, dbv_ref):
        taps = [w[j:j + 1, :] for w in (wg, wv) for j in range(FFN_CONV)]
        _, vjp = jax.vjp(_ffn_fn, xg[...], xv[...], bg[...], bv[...], *taps)
        grads = vjp(dact_ref[...].astype(F32))
        dup_ref[0] = grads[0].astype(BF16)
        dup_ref[1] = grads[1].astype(BF16)
        dbg_ref[...] = jnp.sum(grads[2], axis=0, keepdims=True)
        dbv_ref[...] = jnp.sum(grads[3], axis=0, keepdims=True)
        for j in range(FFN_CONV):
            dwg_ref[j:j + 1, :] = jnp.sum(grads[4 + j], axis=0, keepdims=True)
            dwv_ref[j:j + 1, :] = jnp.sum(grads[4 + FFN_CONV + j], axis=0, keepdims=True)

    col = lambda j: (0, j)
    outs = pl.pallas_call(
        body, name="ffn_act_bwd", grid=(nt,),
        in_specs=[pl.BlockSpec(b, m) for _, b, m in ins],
        out_specs=[pl.BlockSpec((2, SEQ, tc), lambda j: (0, 0, j)),
                   pl.BlockSpec((FFN_CONV, tc), col), pl.BlockSpec((FFN_CONV, tc), col),
                   pl.BlockSpec((1, tc), col), pl.BlockSpec((1, tc), col)],
        out_shape=[jax.ShapeDtypeStruct((2, SEQ, D_FF), BF16),
                   jax.ShapeDtypeStruct((FFN_CONV, D_FF), F32), jax.ShapeDtypeStruct((FFN_CONV, D_FF), F32),
                   jax.ShapeDtypeStruct((1, D_FF), F32), jax.ShapeDtypeStruct((1, D_FF), F32)],
        compiler_params=_params(("parallel",)),
    )(*[a for a, _, _ in ins])
    dup, dwg, dwv, dbg, dbv = outs
    return dup, jnp.concatenate([dwg, dwv], axis=1), jnp.concatenate([dbg, dbv], axis=1)


def _pad_lanes(v):
    return jnp.pad(v.reshape(1, -1), ((0, 0), (0, LANES - v.size)))


def _layer_fwd(x, lw, late=None):
    h = _rms_fwd(x, lw['norm_mix_g'], "rms_mix_fwd")
    p = _matmul(h, lw['w_all'], 'nn', F32, "mm_in")
    if late is not None:
        lw = dict(lw, **late(p))
    gates = _gate_fwd(p, lw['a_log'], lw['dt_bias'])
    q, k, v = _qkv_fwd(p, lw['qkv_conv_w'])
    u, w, qd, kd, qk, dec = _gdn_prep_fwd(q, k, v, gates)
    o0, o1, st0, st1 = _gdn_scan_fwd(u, w, qd, kd, qk, dec)
    ya = _gdn_post_fwd(o0, o1, p, lw['gdn_norm_g'])
    yb = _sgu_fwd(p, lw['sgu_ln_g'], lw['sgu_ln_b'], lw['sgu_w'], lw['sgu_bt'])
    ta = _matmul(ya, lw['w_branch_a'], 'nn', F32, "mm_branch_a", b_split=N_CHIPS)
    tb = _matmul(yb, lw['w_branch_b'], 'nn', F32, "mm_branch_b", b_split=N_CHIPS)
    m = _merge_fwd(p, ta, tb)
    x1 = _matmul(m, lw['w_out'], 'nn', F32, "mm_out", add=x)
    h2 = _rms_fwd(x1, lw['norm_ffn_g'], "rms_ffn_fwd")
    up = _matmul(h2, lw['w_up'], 'nn', F32, "mm_up", b_split=N_CHIPS, o_split=2)
    act = _ffn_act_fwd(up, lw['ffn_conv_w'], lw['ffn_conv_b'])
    x2 = _matmul(act, lw['w_down'], 'nn', F32, "mm_down", add=x1)
    saved = dict(x=x, h=h, p=p, gates=gates, q=q, k=k, v=v, u=u, w=w, qd=qd, kd=kd, qk=qk, dec=dec,
                 o0=o0, o1=o1, st0=st0, st1=st1, ya=ya, yb=yb, ta=ta, tb=tb, m=m, x1=x1, h2=h2, up=up, act=act)
    return x2, saved, lw


def _layer_bwd(dx, dx_bf, lw, s, after=(), mid=None, last=None):
    g = {}
    dact = _matmul(dx_bf, lw['w_down'], 'nt', BF16, "mm_down_dgrad", after=after)
    g['w_down'] = _matmul(s['act'], dx_bf, 'tn', PARTIAL, "mm_down_wgrad")
    dup, g['ffn_conv_w'], g['ffn_conv_b'] = _ffn_act_bwd(s['up'], lw['ffn_conv_w'], lw['ffn_conv_b'], dact)
    dh2 = _matmul(dup, lw['w_up'], 'nt', F32, "mm_up_dgrad", a_split=2, b_split=N_CHIPS)
    g['w_up'] = _matmul(s['h2'], dup, 'tn', PARTIAL, "mm_up_wgrad", b_split=2, o_split=N_CHIPS)
    dx1, dx1_bf, g['norm_ffn_g'] = _rms_bwd(s['x1'], lw['norm_ffn_g'], dh2, dx, "rms_ffn_bwd")
    after_mid = mid(g, dh2) if mid is not None else ()
    dm = _matmul(dx1_bf, lw['w_out'], 'nt', F32, "mm_out_dgrad", after=after_mid)
    g['w_out'] = _matmul(s['m'], dx1_bf, 'tn', PARTIAL, "mm_out_wgrad")
    d_ga, d_gb, d_ta, d_tb = _merge_bwd(s['p'], s['ta'], s['tb'], dm)
    dya = _matmul(d_ta, lw['w_branch_a'], 'nt', BF16, "mm_branch_a_dgrad", b_split=N_CHIPS)
    dyb = _matmul(d_tb, lw['w_branch_b'], 'nt', BF16, "mm_branch_b_dgrad", b_split=N_CHIPS)
    g['w_branch_a'] = _matmul(s['ya'], d_ta, 'tn', PARTIAL, "mm_branch_a_wgrad", o_split=N_CHIPS)
    g['w_branch_b'] = _matmul(s['yb'], d_tb, 'tn', PARTIAL, "mm_branch_b_wgrad", o_split=N_CHIPS)
    du_s, dv_s, g['sgu_ln_g'], g['sgu_ln_b'], g['sgu_w'], g['sgu_bt'] = _sgu_bwd(
        s['p'], lw['sgu_ln_g'], lw['sgu_ln_b'], lw['sgu_w'], lw['sgu_bt'], dyb)
    do, dz, g['gdn_norm_g'] = _gdn_post_bwd(s['o0'], s['o1'], s['p'], lw['gdn_norm_g'], dya)
    cts = _gdn_scan_bwd(s['u'], s['w'], s['qd'], s['kd'], s['qk'], s['dec'], s['st0'], s['st1'], do)
    dq, dk, dv, dgates = _gdn_prep_bwd(s['q'], s['k'], s['v'], s['gates'], cts)
    (dxq, dxk, dxv), g['qkv_conv_w'] = _qkv_bwd(s['p'], lw['qkv_conv_w'], dq, dk, dv)
    d_ab, g['a_log'], g['dt_bias'] = _gate_bwd(s['p'], lw['a_log'], lw['dt_bias'], dgates)
    dp = jnp.concatenate([dxq, dxk, dxv, dz, du_s, dv_s, d_ga, d_gb, d_ab], axis=1)
    g['w_all'] = _matmul(s['h'], dp, 'tn', PARTIAL, "mm_in_wgrad")
    after_wgrads = last(g, dp) if last is not None else ()
    dh = _matmul(dp, lw['w_all'], 'nt', F32, "mm_in_dgrad", after=after_wgrads)
    dx0, dx0_bf, g['norm_mix_g'] = _rms_bwd(s['x'], lw['norm_mix_g'], dh, dx1, "rms_mix_bwd")
    return dx0, dx0_bf, g


def _w_in_segments():
    n_ab = 4 * GDN_HEADS
    cut = 4 * _gw()
    n_main = cut + 2 * _sw() + 2 * D_MODEL
    return [(0, cut, 0), (cut, cut + n_ab, n_main), (cut + n_ab, n_main + n_ab, cut)]


def _w_all_from_chips(t):
    width = t.shape[-1]
    runs = sorted(_w_in_segments(), key=lambda s: s[2])
    pieces = []
    for lo, hi, _ in runs:
        for j in range(N_CHIPS):
            a, b = max(lo, j * width), min(hi, (j + 1) * width)
            if a < b:
                pieces.append(t[j][:, a - j * width:b - j * width])
    pieces.append(jnp.zeros((t.shape[1], AB_PAD - 4 * GDN_HEADS), t.dtype))
    return jnp.concatenate(pieces, axis=1)


def _w_in_grad_chips(g_all):
    segs = _w_in_segments()
    width = segs[-1][1] // N_CHIPS
    chips = []
    for j in range(N_CHIPS):
        pieces = []
        for lo, hi, start in segs:
            a, b = max(lo, j * width), min(hi, (j + 1) * width)
            if a < b:
                pieces.append(g_all[:, start + a - lo:start + b - lo])
        chips.append(jnp.concatenate(pieces, axis=1))
    return jnp.stack(chips)


_FIRST = ('w_in', 'qkv_conv_w')
_REST = ('w_branch_a', 'w_branch_b', 'w_out', 'w_up', 'w_down', 'ffn_conv_w')


def _cat_cols(t):
    return jnp.concatenate([t[j] for j in range(N_CHIPS)], axis=-1)


def _layer_weights_rest(big):
    return dict(
        w_branch_a=big['w_branch_a'], w_branch_b=big['w_branch_b'], w_up=big['w_up'],
        w_out=big['w_out'].reshape(D_MODEL, D_MODEL), w_down=big['w_down'].reshape(D_FF, D_MODEL),
        ffn_conv_w=_cat_cols(big['ffn_conv_w']))


def _layer_weights_first(l, big, small):
    return dict(
        w_all=_w_all_from_chips(big['w_in']), qkv_conv_w=_cat_cols(big['qkv_conv_w']),
        norm_mix_g=small['norm_mix_g'][l:l + 1], norm_ffn_g=small['norm_ffn_g'][l:l + 1],
        a_log=_pad_lanes(small['a_log'][l]), dt_bias=_pad_lanes(small['dt_bias'][l]),
        gdn_norm_g=small['gdn_norm_g'][l:l + 1],
        sgu_ln_g=small['sgu_ln_g'][l:l + 1], sgu_ln_b=small['sgu_ln_b'][l:l + 1], sgu_w=small['sgu_w'][l],
        sgu_bt=jnp.pad(small['sgu_b'][l].T, ((0, 0), (0, LANES - SGU_GROUPS))),
        ffn_conv_b=small['ffn_conv_b'][l:l + 1])


_SMALL_GRADS = ('norm_mix_g', 'a_log', 'dt_bias', 'gdn_norm_g', 'sgu_ln_g', 'sgu_ln_b', 'sgu_w', 'sgu_bt',
                'norm_ffn_g', 'ffn_conv_b', 'qkv_conv_w', 'ffn_conv_w')
_BIG = ('w_in', 'w_branch_a', 'w_branch_b', 'w_out', 'w_up', 'w_down')


def _big_grad_slices(g, names=_BIG):
    def one(k):
        if k == 'w_in':
            return _w_in_grad_chips(g['w_all'])
        if k == 'w_out':
            return g[k].reshape(N_CHIPS, D_MODEL // N_CHIPS, D_MODEL)
        if k == 'w_down':
            return g[k].reshape(N_CHIPS, D_FF // N_CHIPS, D_MODEL)
        return g[k]
    return {k: one(k) for k in names}


def _adamw(w, g, m, v, name, after=()):
    shape = w.shape
    cols = shape[-1]
    rows = w.size // cols
    tr = _pick(rows, max(8, (1 << 18) // cols // 8 * 8), 8) if rows % 8 == 0 else rows

    def fn(wv, gv, mv, vv):
        m2 = ADAM_B1 * mv + (1.0 - ADAM_B1) * gv
        v2 = ADAM_B2 * vv + (1.0 - ADAM_B2) * (gv * gv)
        m_hat = m2 / (1.0 - ADAM_B1 ** ADAM_STEP)
        v_hat = v2 / (1.0 - ADAM_B2 ** ADAM_STEP)
        delta = -ADAM_LR * (m_hat / (jnp.sqrt(v_hat) + ADAM_EPS) + ADAM_WD * wv)
        return delta, m2, v2

    row = lambda i: (i, 0)
    outs = _tiled(fn, name, (rows // tr,),
                  [(t.reshape(rows, cols), (tr, cols), row) for t in (w, g, m, v)],
                  [((rows, cols), F32, (tr, cols), row, ())] * 3, sem=("parallel",), after=after)
    return [o.reshape(shape) for o in outs]


MESH_IDS = pl.DeviceIdType.MESH
ANY = pl.BlockSpec(memory_space=pl.ANY)


def _place():
    x, y, c = lax.axis_index("x"), lax.axis_index("y"), lax.axis_index("c")
    chips = [(1 - x, y), (x, 1 - y), (1 - x, 1 - y)]
    return x, y, c, 2 * x + y, chips


def _chip_index():
    return 2 * lax.axis_index("x") + lax.axis_index("y")


HBM = pl.BlockSpec(memory_space=pltpu.HBM)
SEM = pl.BlockSpec(memory_space=pltpu.SEMAPHORE)
DATAFLOW = pltpu.SideEffectType.DATAFLOW_SIDE_EFFECTING
TOKEN = jax.ShapeDtypeStruct((8, LANES), F32)


def _in_hbm(t):
    return pltpu.with_memory_space_constraint(t, pltpu.HBM)


def _place_shard(w, l, dtype, name):
    _, r, cols = w.shape
    tr = _pick(r, max(16, (1 << 18) // cols // 16 * 16), 16) if r % 16 == 0 else r

    def body(w_ref, o_ref):
        o_ref[...] = w_ref[...].astype(dtype)

    return pl.pallas_call(
        body, name=name, grid=(r // tr,),
        in_specs=[pl.BlockSpec((None, tr, cols), lambda i: (l, i, 0))],
        out_specs=pl.BlockSpec((None, tr, cols), lambda i: (_chip_index(), i, 0)),
        out_shape=jax.ShapeDtypeStruct((N_CHIPS, r, cols), dtype),
        compiler_params=_params(("parallel",)),
    )(w)


def _my_rows(ref_or_shape_rows, c):
    r = ref_or_shape_rows
    if r % 32 == 0:
        return pl.ds(c * (r // 2), r // 2), pl.ds((1 - c) * (r // 2), r // 2), True
    return pl.ds(0, r), pl.ds(0, r), False


def _gather_copies(bufs, send_sems, recv_sems):
    x, y, c, me, chips = _place()
    out = []
    for i, buf in enumerate(bufs):
        mine, _, _ = _my_rows(buf.shape[1], c)
        for j, (cx, cy) in enumerate(chips):
            def rcopy(slab, i=i, j=j, cx=cx, cy=cy):
                return pltpu.make_async_remote_copy(src_ref=slab, dst_ref=slab, send_sem=send_sems.at[3 * i + j],
                                                    recv_sem=recv_sems.at[3 * i + j], device_id=(cx, cy, c),
                                                    device_id_type=MESH_IDS)
            out.append((rcopy(buf.at[me, mine]), rcopy(buf.at[2 * cx + cy, mine])))
    return out


def _gather_start(xs, name, after=()):
    n = len(xs)
    n_after = len(after)

    def body(*refs):
        refs = refs[n + n_after:]
        send_sems, recv_sems = refs[0], refs[1]
        bufs = refs[2:n + 2]
        token = refs[n + 2]
        for out_going, _ in _gather_copies(bufs, send_sems, recv_sems):
            out_going.start()
        token[...] = jnp.zeros_like(token)

    res = pl.pallas_call(
        body, name=name, in_specs=[HBM] * n + [ANY] * n_after,
        out_specs=[SEM, SEM] + [HBM] * n + [pl.BlockSpec(memory_space=pltpu.VMEM)],
        out_shape=[pltpu.SemaphoreType.DMA((3 * n,)), pltpu.SemaphoreType.DMA((3 * n,))]
        + [pltpu.HBM(t.shape, t.dtype) for t in xs] + [TOKEN],
        input_output_aliases={i: i + 2 for i in range(n)},
        compiler_params=pltpu.CompilerParams(has_side_effects=DATAFLOW),
    )(*[_in_hbm(t) for t in xs], *after)
    return res[0], res[1], res[2:2 + n], res[2 + n]


def _gather_wait(bufs, send_sems, recv_sems, after, name):
    n = len(bufs)

    def body(*refs):
        b_refs = refs[:n]
        s_sems, r_sems = refs[n], refs[n + 1]
        for out_going, in_coming in _gather_copies(b_refs, s_sems, r_sems):
            out_going.wait_send()
            in_coming.wait_recv()

    return pl.pallas_call(
        body, name=name, in_specs=[HBM] * n + [SEM, SEM, ANY], out_specs=[HBM] * n,
        out_shape=[pltpu.HBM(t.shape, t.dtype) for t in bufs],
        input_output_aliases={i: i for i in range(n)},
        compiler_params=pltpu.CompilerParams(has_side_effects=DATAFLOW),
    )(*bufs, send_sems, recv_sems, after)


def _gather_forward(bufs, name):
    idx = [i for i, t in enumerate(bufs) if t.shape[1] % 32 == 0]
    xs = [bufs[i] for i in idx]
    n = len(xs)

    def body(*refs):
        o_refs = refs[n:2 * n]
        send_sems, recv_sems = refs[2 * n:]
        x, y, c, _, chips = _place()
        copies = []
        for i in range(n):
            mine, theirs, _ = _my_rows(xs[i].shape[1], c)
            for j, (cx, cy) in enumerate(chips):
                def rcopy(slab, i=i, j=j):
                    return pltpu.make_async_remote_copy(src_ref=slab, dst_ref=slab, send_sem=send_sems.at[3 * i + j],
                                                        recv_sem=recv_sems.at[3 * i + j], device_id=(x, y, 1 - c),
                                                        device_id_type=MESH_IDS)
                copies.append((rcopy(o_refs[i].at[2 * cx + cy, mine]), rcopy(o_refs[i].at[2 * cx + cy, theirs])))
                copies[-1][0].start()
        for out_going, in_coming in copies:
            out_going.wait_send()
            in_coming.wait_recv()

    res = pl.pallas_call(
        body, name=name, in_specs=[ANY] * n, out_specs=[ANY] * n,
        out_shape=[jax.ShapeDtypeStruct(t.shape, t.dtype) for t in xs],
        input_output_aliases={i: i for i in range(n)},
        scratch_shapes=[pltpu.SemaphoreType.DMA((3 * n,)), pltpu.SemaphoreType.DMA((3 * n,))],
        compiler_params=pltpu.CompilerParams(has_side_effects=True),
    )(*xs)
    out = list(bufs)
    for i, t in zip(idx, res):
        out[i] = t
    return out


def _half_tile(rh, cols):
    return _pick(rh, max(16, (1 << 18) // cols // 16 * 16), 16)


N_LAND = 7


def _scatter_copies(p_refs, r_refs, send_sems, recv_sems):
    x, y, c, me, chips = _place()
    outgoing, incoming = [], []
    for i, (p, land) in enumerate(zip(p_refs, r_refs)):
        rh = p.shape[1] // 2

        def copy(src, slot, k_send, k_recv, to, i=i, land=land):
            return pltpu.make_async_remote_copy(
                src_ref=src, dst_ref=land.at[slot], send_sem=send_sems.at[N_LAND * i + k_send],
                recv_sem=recv_sems.at[N_LAND * i + k_recv], device_id=to, device_id_type=MESH_IDS)

        for r, (cx, cy) in enumerate(chips):
            for k in range(2):
                outgoing.append(copy(p.at[2 * cx + cy, pl.ds(k * rh, rh)], 2 * r + c, 2 * r + k, 2 * r + c,
                                     (cx, cy, k)))
                incoming.append(copy(land.at[2 * r + k], 2 * r + k, 2 * r + k, 2 * r + k, (x, y, c)))
        outgoing.append(copy(p.at[me, pl.ds((1 - c) * rh, rh)], 6, 6, 6, (x, y, 1 - c)))
        incoming.append(copy(land.at[6], 6, 6, 6, (x, y, c)))
    return outgoing, incoming


def _scatter_start(ps, name):
    n = len(ps)
    lands = [lax.empty((N_LAND, t.shape[1] // 2, t.shape[2]), t.dtype) for t in ps]

    def body(*refs):
        send_sems, recv_sems = refs[2 * n], refs[2 * n + 1]
        p_refs = refs[2 * n + 2:3 * n + 2]
        r_refs = refs[3 * n + 2:4 * n + 2]
        token = refs[4 * n + 2]
        for cp in _scatter_copies(p_refs, r_refs, send_sems, recv_sems)[0]:
            cp.start()
        token[...] = jnp.zeros_like(token)

    res = pl.pallas_call(
        body, name=name, in_specs=[HBM] * (2 * n),
        out_specs=[SEM, SEM] + [HBM] * (2 * n) + [pl.BlockSpec(memory_space=pltpu.VMEM)],
        out_shape=[pltpu.SemaphoreType.DMA((N_LAND * n,)), pltpu.SemaphoreType.DMA((N_LAND * n,))]
        + [pltpu.HBM(t.shape, t.dtype) for t in list(ps) + lands] + [TOKEN],
        input_output_aliases={i: i + 2 for i in range(2 * n)},
        compiler_params=pltpu.CompilerParams(has_side_effects=DATAFLOW),
    )(*[_in_hbm(t) for t in list(ps) + lands])
    return res[0], res[1], res[2:2 + n], res[2 + n:2 + 2 * n], res[2 + 2 * n]


def _scatter_wait(ps, lands, send_sems, recv_sems, after, name):
    n = len(ps)
    after = tuple(after) if isinstance(after, (tuple, list)) else (after,)

    def body(*refs):
        p_refs, r_refs = refs[:n], refs[n:2 * n]
        s_sems, r_sems = refs[2 * n], refs[2 * n + 1]
        outgoing, incoming = _scatter_copies(p_refs, r_refs, s_sems, r_sems)
        for cp in outgoing:
            cp.wait_send()
        for cp in incoming:
            cp.wait_recv()

    res = pl.pallas_call(
        body, name=name, in_specs=[HBM] * (2 * n) + [SEM, SEM] + [ANY] * len(after), out_specs=[HBM] * (2 * n),
        out_shape=[pltpu.HBM(t.shape, t.dtype) for t in list(ps) + list(lands)],
        input_output_aliases={i: i for i in range(2 * n)},
        compiler_params=pltpu.CompilerParams(has_side_effects=DATAFLOW),
    )(*ps, *lands, send_sems, recv_sems, *after)
    return res[:n], res[n:]


def _reduce_own(p, rcv, acc, l, name):
    nchip, r, cols = p.shape
    rh = r // 2
    tr = _half_tile(rh, cols)
    nt = rh // tr

    def body(*refs):
        p_ref, r_ref = refs[:2]
        o_ref = refs[-1]
        total = p_ref[...].astype(F32)
        for j in range(N_LAND):
            total = total + r_ref[j].astype(F32)
        o_ref[...] = total

    mine = lambda i: lax.axis_index("c") * nt + i
    in_specs = [pl.BlockSpec((None, tr, cols), lambda i: (_chip_index(), mine(i), 0)),
                pl.BlockSpec((N_LAND, tr, cols), lambda i: (0, i, 0))]
    args = [p, rcv]
    aliases = {}
    if acc is not None:
        in_specs.append(ANY)
        args.append(acc)
        aliases = {2: 0}
    return pl.pallas_call(
        body, name=name, grid=(nt,), in_specs=in_specs,
        out_specs=pl.BlockSpec((None, tr, cols), lambda i: (l, mine(i), 0)),
        out_shape=jax.ShapeDtypeStruct((DEPTH, r, cols), F32), input_output_aliases=aliases,
        compiler_params=_params(("parallel",)),
    )(*args)


def _share_halves(fs, l):
    n = len(fs)

    def body(*refs):
        o_refs = refs[n:2 * n]
        send_sems, recv_sems = refs[2 * n:]
        x, y, c, _, _ = _place()

        def halves(i):
            rh = fs[i].shape[1] // 2
            return o_refs[i].at[l, pl.ds(c * rh, rh), :], o_refs[i].at[l, pl.ds((1 - c) * rh, rh), :]

        def copy(i, rows):
            return pltpu.make_async_remote_copy(src_ref=rows, dst_ref=rows, send_sem=send_sems.at[i],
                                                recv_sem=recv_sems.at[i], device_id=(x, y, 1 - c),
                                                device_id_type=MESH_IDS)

        for i in range(n):
            copy(i, halves(i)[0]).start()
        for i in range(n):
            mine, theirs = halves(i)
            copy(i, mine).wait_send()
            copy(i, theirs).wait_recv()

    return pl.pallas_call(
        body, name="grad_share_halves", in_specs=[ANY] * n, out_specs=[ANY] * n,
        out_shape=[jax.ShapeDtypeStruct(t.shape, t.dtype) for t in fs],
        input_output_aliases={i: i for i in range(n)},
        scratch_shapes=[pltpu.SemaphoreType.DMA((n,)), pltpu.SemaphoreType.DMA((n,))],
        compiler_params=pltpu.CompilerParams(has_side_effects=True),
    )(*fs)


def _reduce_scatter_begin(slices, l, tag):
    names = list(slices)
    send_sems, recv_sems, ps, lands, token = _scatter_start([slices[k] for k in names],
                                                            "grad_scatter_start_%s%d" % (tag, l))
    return dict(names=names, ps=ps, lands=lands, sems=(send_sems, recv_sems), token=token, l=l, tag=tag)


def _reduce_scatter_end(st, stacked, after):
    l = st['l']
    ps, rs = _scatter_wait(st['ps'], st['lands'], *st['sems'], after, "grad_scatter_wait_%s%d" % (st['tag'], l))
    fs = [_reduce_own(p, r, stacked.get(k), l, "grad_reduce_own_" + k)
          for k, p, r in zip(st['names'], ps, rs)]
    stacked.update(zip(st['names'], _share_halves(fs, l)))


def _all_reduce_small(packed, after=()):
    rows = packed.shape[0]
    rh = rows // 2
    tr = _pick(rh, 512, 8)

    def body(x_ref, *rest):
        o_ref, sib, chip_sums, send_sems, recv_sems = rest[len(after):]
        x, y, c, me, chips = _place()
        sibling = (x, y, 1 - c)
        mine, theirs = pl.ds(c * rh, rh), pl.ds((1 - c) * rh, rh)

        def copy(src, dst, k, to):
            return pltpu.make_async_remote_copy(src_ref=src, dst_ref=dst, send_sem=send_sems.at[k],
                                                recv_sem=recv_sems.at[k], device_id=to, device_id_type=MESH_IDS)

        def tiles(fn):
            @pl.loop(0, rh // tr)
            def _(t):
                fn(pl.ds(pl.multiple_of(t * tr, 8), tr))

        to_sibling = copy(x_ref.at[theirs], sib, 0, sibling)
        to_sibling.start()
        to_sibling.wait()

        def chip_sum(sl):
            chip_sums[me, sl, :] = x_ref.at[mine][sl, :] + sib[sl, :]
        tiles(chip_sum)

        out = [copy(chip_sums.at[me], chip_sums.at[me], 1 + j, (cx, cy, c)) for j, (cx, cy) in enumerate(chips)]
        for cp in out:
            cp.start()
        for j, (cx, cy) in enumerate(chips):
            copy(chip_sums.at[2 * cx + cy], chip_sums.at[2 * cx + cy], 1 + j, (x, y, c)).wait_recv()
        for cp in out:
            cp.wait_send()

        def total(sl):
            acc = chip_sums[0, sl, :]
            for s in range(1, N_CHIPS):
                acc = acc + chip_sums[s, sl, :]
            o_ref.at[mine][sl, :] = acc
        tiles(total)

        share = copy(o_ref.at[mine], o_ref.at[mine], 4, sibling)
        share.start()
        share.wait_send()
        copy(o_ref.at[theirs], o_ref.at[theirs], 4, sibling).wait_recv()

    vm = pl.BlockSpec(memory_space=pltpu.VMEM)
    return pl.pallas_call(
        body, name="all_reduce_small", in_specs=[vm] + [ANY] * len(after), out_specs=vm,
        out_shape=jax.ShapeDtypeStruct(packed.shape, F32),
        scratch_shapes=[pltpu.VMEM((rh, LANES), F32), pltpu.VMEM((N_CHIPS, rh, LANES), F32),
                        pltpu.SemaphoreType.DMA((5,)), pltpu.SemaphoreType.DMA((5,))],
        compiler_params=pltpu.CompilerParams(vmem_limit_bytes=VMEM_LIMIT, has_side_effects=True),
    )(packed, *after)


_WEIGHTS = ('norm_mix_g', 'w_in', 'qkv_conv_w', 'a_log', 'dt_bias', 'gdn_norm_g', 'w_branch_a', 'sgu_ln_g',
            'sgu_ln_b', 'sgu_w', 'sgu_b', 'w_branch_b', 'w_out', 'norm_ffn_g', 'w_up', 'ffn_conv_w', 'ffn_conv_b',
            'w_down', 'final_norm_g')


def _local_step(x, target, weights_of, final_norm_g, bwd_after=None, mid_layer=None, last_layer=None):
    lws, saves = [], []
    for l in range(DEPTH):
        lw, late = weights_of(l, x)
        x, s, lw = _layer_fwd(x, lw, late)
        lws.append(lw)
        saves.append(s)
    loss, dx, dx_bf, d_final = _loss_head(x, final_norm_g.reshape(1, -1), target)
    grads = [None] * DEPTH
    for l in reversed(range(DEPTH)):
        after = bwd_after(l) if bwd_after is not None else ()
        mid = (lambda g, dh2, l=l: mid_layer(l, g, dh2)) if mid_layer is not None else None
        last = (lambda g, dp, l=l: last_layer(l, g, dp)) if last_layer is not None else None
        dx, dx_bf, grads[l] = _layer_bwd(dx, dx_bf, lws[l], saves[l], after, mid, last)
    return loss, dx, grads, d_final


def _pack_small(grads, d_final):
    parts = [grads[l][k].reshape(-1) for l in range(DEPTH) for k in _SMALL_GRADS] + [d_final.reshape(-1)]
    flat = jnp.concatenate(parts)
    rows = -(-flat.size // (16 * LANES)) * 16
    return jnp.pad(flat, (0, rows * LANES - flat.size)).reshape(rows, LANES), [p.size for p in parts]


def _unpack_small(packed, grads, d_final):
    flat = packed.reshape(-1)
    out, off = [], 0
    for l in range(DEPTH):
        d = {}
        for k in _SMALL_GRADS:
            t = grads[l][k]
            d[k] = flat[off:off + t.size].reshape(t.shape)
            off += t.size
        out.append(d)
    return out, flat[off:off + d_final.size].reshape(d_final.shape)


def kernel(x, norm_mix_g, w_in, qkv_conv_w, a_log, dt_bias, gdn_norm_g, w_branch_a, sgu_ln_g, sgu_ln_b, sgu_w, sgu_b, w_branch_b, w_out, norm_ffn_g, w_up, ffn_conv_w, ffn_conv_b, w_down, final_norm_g, loss_target, m_norm_mix_g, m_w_in, m_qkv_conv_w, m_a_log, m_dt_bias, m_gdn_norm_g, m_w_branch_a, m_sgu_ln_g, m_sgu_ln_b, m_sgu_w, m_sgu_b, m_w_branch_b, m_w_out, m_norm_ffn_g, m_w_up, m_ffn_conv_w, m_ffn_conv_b, m_w_down, m_final_norm_g, v_norm_mix_g, v_w_in, v_qkv_conv_w, v_a_log, v_dt_bias, v_gdn_norm_g, v_w_branch_a, v_sgu_ln_g, v_sgu_ln_b, v_sgu_w, v_sgu_b, v_w_branch_b, v_w_out, v_norm_ffn_g, v_w_up, v_ffn_conv_w, v_ffn_conv_b, v_w_down, v_final_norm_g):
    w = dict(norm_mix_g=norm_mix_g, w_in=w_in, qkv_conv_w=qkv_conv_w, a_log=a_log, dt_bias=dt_bias,
             gdn_norm_g=gdn_norm_g, w_branch_a=w_branch_a, sgu_ln_g=sgu_ln_g, sgu_ln_b=sgu_ln_b, sgu_w=sgu_w,
             sgu_b=sgu_b, w_branch_b=w_branch_b, w_out=w_out, norm_ffn_g=norm_ffn_g, w_up=w_up,
             ffn_conv_w=ffn_conv_w, ffn_conv_b=ffn_conv_b, w_down=w_down, final_norm_g=final_norm_g)
    m = dict(norm_mix_g=m_norm_mix_g, w_in=m_w_in, qkv_conv_w=m_qkv_conv_w, a_log=m_a_log, dt_bias=m_dt_bias,
             gdn_norm_g=m_gdn_norm_g, w_branch_a=m_w_branch_a, sgu_ln_g=m_sgu_ln_g, sgu_ln_b=m_sgu_ln_b,
             sgu_w=m_sgu_w, sgu_b=m_sgu_b, w_branch_b=m_w_branch_b, w_out=m_w_out, norm_ffn_g=m_norm_ffn_g,
             w_up=m_w_up, ffn_conv_w=m_ffn_conv_w, ffn_conv_b=m_ffn_conv_b, w_down=m_w_down,
             final_norm_g=m_final_norm_g)
    v = dict(norm_mix_g=v_norm_mix_g, w_in=v_w_in, qkv_conv_w=v_qkv_conv_w, a_log=v_a_log, dt_bias=v_dt_bias,
             gdn_norm_g=v_gdn_norm_g, w_branch_a=v_w_branch_a, sgu_ln_g=v_sgu_ln_g, sgu_ln_b=v_sgu_ln_b,
             sgu_w=v_sgu_w, sgu_b=v_sgu_b, w_branch_b=v_w_branch_b, w_out=v_w_out, norm_ffn_g=v_norm_ffn_g,
             w_up=v_w_up, ffn_conv_w=v_ffn_conv_w, ffn_conv_b=v_ffn_conv_b, w_down=v_w_down,
             final_norm_g=v_final_norm_g)
    chip = _chip_index()

    in_flight = []
    for l in range(DEPTH):
        for tag, names in (("first", _FIRST), ("rest", _REST)):
            placed = [_place_shard(w[k], l, BF16 if k in _BIG else F32, "place_" + k) for k in names]
            earlier = (in_flight[-1][3],) if in_flight else ()
            in_flight.append(_gather_start(placed, "gather_start_%s_%d" % (tag, l), earlier))

    def arrived(l, tag, names, after):
        send_sems, recv_sems, bufs, _ = in_flight[2 * l + (tag == "rest")]
        bufs = _gather_wait(bufs, send_sems, recv_sems, after, "gather_wait_%s_%d" % (tag, l))
        return dict(zip(names, _gather_forward(bufs, "gather_forward_" + tag)))

    def weights_of(l, x_in):
        first = arrived(l, "first", _FIRST, in_flight[-1][3] if l == 0 else x_in)
        return (_layer_weights_first(l, first, w),
                lambda p: _layer_weights_rest(arrived(l, "rest", _REST, p)))

    group_a, group_b = ('w_up', 'w_down'), ('w_in', 'w_branch_a', 'w_branch_b', 'w_out')
    big_g = {}
    pending = {'a': None, 'b': None}

    def finish(tag, after):
        if pending[tag] is not None:
            _reduce_scatter_end(pending[tag], big_g, after)
            pending[tag] = None

    def bwd_after(l):
        return () if pending['b'] is None else (pending['b']['token'],)

    def mid_layer(l, g, dh2):
        finish('b', dh2)
        pending['a'] = _reduce_scatter_begin(_big_grad_slices(g, group_a), l, 'a')
        return (pending['a']['token'],)

    def last_layer(l, g, dp):
        finish('a', dp)
        pending['b'] = _reduce_scatter_begin(_big_grad_slices(g, group_b), l, 'b')
        return (pending['b']['token'],)

    loss, grad_x, grads, d_final = _local_step(x[0], loss_target[0], weights_of, w['final_norm_g'], bwd_after,
                                               mid_layer, last_layer)
    loss = lax.psum(loss[0, 0], ("x", "y", "c"))
    packed, _ = _pack_small(grads, d_final)
    reduced_small = _all_reduce_small(packed, (pending['b']['token'],))
    small_g, d_final = _unpack_small(reduced_small, grads, d_final)
    deltas, new_m, new_v, g_out = {}, {}, {}, {}
    for k in group_a:
        g_out[k] = big_g[k].reshape(w[k].shape)
        deltas[k], new_m[k], new_v[k] = _adamw(w[k], g_out[k], m[k], v[k], "adamw_" + k, (pending['b']['token'],))
    finish('b', (reduced_small,) + tuple(deltas[k] for k in group_a))

    def stack(k):
        return jnp.stack([small_g[l][k] for l in range(DEPTH)])

    nd = 2 * GDN_HEADS
    for k in group_b:
        g_out[k] = big_g[k].reshape(w[k].shape)
    for k in ('norm_mix_g', 'gdn_norm_g', 'sgu_ln_g', 'sgu_ln_b', 'norm_ffn_g', 'ffn_conv_b'):
        g_out[k] = stack(k).reshape(w[k].shape)
    g_out['sgu_w'] = stack('sgu_w')
    g_out['a_log'] = stack('a_log')[:, 0, :nd].reshape(w['a_log'].shape)
    g_out['dt_bias'] = stack('dt_bias')[:, 0, :nd].reshape(w['dt_bias'].shape)
    g_out['sgu_b'] = jnp.swapaxes(stack('sgu_bt')[:, :, :SGU_GROUPS], 1, 2)
    for k in ('qkv_conv_w', 'ffn_conv_w'):
        full = stack(k)
        width = w[k].shape[-1]
        g_out[k] = lax.dynamic_slice_in_dim(full, chip * width, width, axis=2)
    g_out['final_norm_g'] = d_final.reshape(w['final_norm_g'].shape)

    for k in _WEIGHTS:
        if k not in group_a:
            deltas[k], new_m[k], new_v[k] = _adamw(w[k], g_out[k], m[k], v[k], "adamw_" + k)
    return (loss, grad_x[None], *[g_out[k] for k in _WEIGHTS], *[deltas[k] for k in _WEIGHTS],
            *[new_m[k] for k in _WEIGHTS], *[new_v[k] for k in _WEIGHTS])
```

```python
import functools
import math

import jax
import jax.numpy as jnp
from jax import lax
from jax.experimental import pallas as pl
from jax.experimental.pallas import tpu as pltpu

F32 = jnp.float32
BF16 = jnp.bfloat16
PARTIAL = BF16

D_MODEL = 2048
SEQ = 2048
DEPTH = 4
GDN_HEADS = 8
HEAD_DIM = 128
GDN_CHUNK = 64
QKV_CONV = 5
SGU_GROUPS = 8
SGU_DIM = 128
SGU_BLOCK = 128
D_FF = 5632
FFN_CONV = 3
NORM_EPS = 1e-6
N_CHIPS = 4

ADAM_LR = 0.001
ADAM_B1 = 0.9
ADAM_B2 = 0.999
ADAM_EPS = 1e-08
ADAM_WD = 0.01
ADAM_STEP = 10

LANES = 128
VMEM_LIMIT = 56 * 1024 * 1024
AB_PAD = LANES


def _gw():
    return GDN_HEADS * HEAD_DIM


def _sw():
    return SGU_GROUPS * SGU_DIM


def _n_all():
    return 4 * _gw() + 2 * _sw() + 2 * D_MODEL + AB_PAD


def _pick(n, target, align=LANES):
    if n <= target:
        return n
    best = None
    t = align
    while t <= target:
        if n % t == 0:
            best = t
        t += align
    assert best is not None, (n, target, align)
    return best


def _params(sem=None):
    return pltpu.CompilerParams(dimension_semantics=sem, vmem_limit_bytes=VMEM_LIMIT)


def _dot(a, b, dims, hi):
    dn = (dims, ((), ()))
    if not hi:
        return lax.dot_general(a.astype(BF16), b.astype(BF16), dn, preferred_element_type=F32)
    a_hi, b_hi = a.astype(BF16), b.astype(BF16)
    a_lo = (a - a_hi.astype(F32)).astype(BF16)
    b_lo = (b - b_hi.astype(F32)).astype(BF16)
    d = lambda p, q: lax.dot_general(p, q, dn, preferred_element_type=F32)
    return d(a_hi, b_hi) + (d(a_hi, b_lo) + d(a_lo, b_hi))


def _make_mm(hi):
    @jax.custom_vjp
    def mm(a, b):
        return _dot(a, b, ((1,), (0,)), hi)

    @jax.custom_vjp
    def mm_nt(a, b):
        return _dot(a, b, ((1,), (1,)), hi)

    @jax.custom_vjp
    def mm_tn(a, b):
        return _dot(a, b, ((0,), (0,)), hi)

    mm.defvjp(lambda a, b: (mm(a, b), (a, b)), lambda r, g: (mm_nt(g, r[1]), mm_tn(r[0], g)))
    mm_nt.defvjp(lambda a, b: (mm_nt(a, b), (a, b)), lambda r, g: (mm(g, r[1]), mm_tn(g, r[0])))
    mm_tn.defvjp(lambda a, b: (mm_tn(a, b), (a, b)), lambda r, g: (mm_nt(r[1], g), mm(r[0], g)))
    return mm, mm_nt, mm_tn


_mm, _mm_nt, _mm_tn = _make_mm(False)
_mmh, _mmh_nt, _mmh_tn = _make_mm(True)


def _shift_rows_raw(x, s):
    if s == 0:
        return x
    n = x.shape[0]
    rolled = pltpu.roll(x, (-s) % n, 0)
    t = lax.broadcasted_iota(jnp.int32, x.shape, 0)
    ok = (t + s >= 0) & (t + s < n)
    return jnp.where(ok, rolled, 0.0)


@functools.partial(jax.custom_vjp, nondiff_argnums=(1,))
def _shift_rows(x, s):
    return _shift_rows_raw(x, s)


_shift_rows.defvjp(lambda x, s: (_shift_rows_raw(x, s), None), lambda s, _, g: (_shift_rows_raw(g, -s),))


def _sigmoid(x):
    return 1.0 / (1.0 + jnp.exp(-x))


def _silu(x):
    return x * _sigmoid(x)


def _gelu(x):
    return 0.5 * x * (1.0 + jnp.tanh(math.sqrt(2.0 / math.pi) * (x + 0.044715 * x * x * x)))


def _softplus(x):
    return jnp.maximum(x, 0.0) + jnp.log(1.0 + jnp.exp(-jnp.abs(x)))


def _rms(x, g):
    return x * lax.rsqrt(jnp.mean(x * x, axis=-1, keepdims=True) + NORM_EPS) * g


def _conv_rows(x, taps):
    pad = len(taps) // 2
    acc = None
    for j, w in enumerate(taps):
        term = _shift_rows(x, j - pad) * w
        acc = term if acc is None else acc + term
    return acc


@jax.custom_vjp
def _inv_unit(mats):
    n = mats[0].shape[0]
    eye = (lax.broadcasted_iota(jnp.int32, (n, n), 0) == lax.broadcasted_iota(jnp.int32, (n, n), 1)).astype(F32)
    ps = [eye - a for a in mats]
    aks = list(mats)
    for _ in range(int(math.log2(n)) - 1):
        aks = [_mmh(ak, ak) for ak in aks]
        ps = [p + _mmh(p, ak) for p, ak in zip(ps, aks)]
    return ps


def _inv_unit_fwd(mats):
    ts = _inv_unit(mats)
    return ts, ts


def _inv_unit_bwd(ts, gs):
    inner = [_mmh_nt(g, t) for g, t in zip(gs, ts)]
    return ([-_mmh_tn(t, m) for t, m in zip(ts, inner)],)


_inv_unit.defvjp(_inv_unit_fwd, _inv_unit_bwd)


def _tiled(fn, name, grid, ins, outs, sem=None, after=()):
    n_in = len(ins)

    def body(*refs):
        vals = [r[...] for r in refs[:n_in]]
        res = fn(*vals)
        if not isinstance(res, (tuple, list)):
            res = (res,)
        for o_ref, r, spec in zip(refs[n_in + len(after):], res, outs):
            acc = spec[4]
            if not acc:
                o_ref[...] = r.astype(o_ref.dtype)
            else:
                first = functools.reduce(jnp.logical_and, [pl.program_id(a) == 0 for a in acc])

                @pl.when(first)
                def _(o_ref=o_ref, r=r):
                    o_ref[...] = r.astype(o_ref.dtype)

                @pl.when(jnp.logical_not(first))
                def _(o_ref=o_ref, r=r):
                    o_ref[...] += r.astype(o_ref.dtype)

    return pl.pallas_call(
        body, name=name, grid=grid,
        in_specs=[pl.BlockSpec(b, m) for _, b, m in ins] + [pl.BlockSpec(memory_space=pl.ANY)] * len(after),
        out_specs=[pl.BlockSpec(s[2], s[3]) for s in outs],
        out_shape=[jax.ShapeDtypeStruct(s[0], s[1]) for s in outs],
        compiler_params=_params(sem),
    )(*[a for a, _, _ in ins], *after)


def _matmul(a, b, mode, out_dtype, name, add=None, a_split=1, b_split=1, o_split=1, after=()):
    def dims2(x, split):
        return (x.shape[-2], x.shape[-1] * split)

    ar, ac = dims2(a, a_split)
    br, bc = dims2(b, b_split)
    if mode == 'nn':
        M, K, N = ar, ac, bc
        assert br == K
    elif mode == 'nt':
        M, K, N = ar, ac, br
        assert bc == K
    else:
        K, M, N = ar, ac, bc
        assert br == K
    tm = _pick(M // (a_split if mode == 'tn' else 1), 512)
    tn = _pick(N // max(o_split, b_split if mode != 'nt' else 1), 1408)
    tk = _pick(K // max(a_split if mode != 'tn' else 1, b_split if mode == 'nt' else 1), 2048)
    gm, gn, gk = M // tm, N // tn, K // tk

    def col_map(split, total_cols, tile):
        per = total_cols // split // tile

        def f(r, c):
            return (c // per, r, c % per) if split > 1 else (r, c)
        return f

    if mode == 'nn':
        a_idx = col_map(a_split, K, tk)
        b_idx = col_map(b_split, N, tn)
        a_spec = pl.BlockSpec(((None,) if a_split > 1 else ()) + (tm, tk), lambda m, n, k: a_idx(m, k))
        b_spec = pl.BlockSpec(((None,) if b_split > 1 else ()) + (tk, tn), lambda m, n, k: b_idx(k, n))
        dn = ((1,), (0,))
    elif mode == 'nt':
        a_idx = col_map(a_split, K, tk)
        b_idx = col_map(b_split, K, tk)
        a_spec = pl.BlockSpec(((None,) if a_split > 1 else ()) + (tm, tk), lambda m, n, k: a_idx(m, k))
        b_spec = pl.BlockSpec(((None,) if b_split > 1 else ()) + (tn, tk), lambda m, n, k: b_idx(n, k))
        dn = ((1,), (1,))
    else:
        a_idx = col_map(a_split, M, tm)
        b_idx = col_map(b_split, N, tn)
        a_spec = pl.BlockSpec(((None,) if a_split > 1 else ()) + (tk, tm), lambda m, n, k: a_idx(k, m))
        b_spec = pl.BlockSpec(((None,) if b_split > 1 else ()) + (tk, tn), lambda m, n, k: b_idx(k, n))
        dn = ((0,), (0,))
    o_idx = col_map(o_split, N, tn)
    o_block = ((None,) if o_split > 1 else ()) + (tm, tn)
    o_spec = pl.BlockSpec(o_block, lambda m, n, k: o_idx(m, n))
    o_shape = ((o_split, M, N // o_split) if o_split > 1 else (M, N))
    in_specs = [a_spec, b_spec]
    args = [a, b]
    if add is not None:
        in_specs.append(pl.BlockSpec((tm, tn), lambda m, n, k: (m, n)))
        args.append(add)
    n_lead = len(args)
    for t in after:
        in_specs.append(pl.BlockSpec(memory_space=pl.ANY))
        args.append(t)

    def body(*refs):
        a_ref, b_ref = refs[0], refs[1]
        add_ref = refs[2] if add is not None else None
        o_ref = refs[n_lead + len(after)]
        part = lax.dot_general(a_ref[...].astype(BF16), b_ref[...].astype(BF16), (dn, ((), ())),
                               preferred_element_type=F32)

        def finish(total):
            if add_ref is not None:
                total = total + add_ref[...]
            o_ref[...] = total.astype(o_ref.dtype)

        if gk == 1:
            finish(part)
        else:
            acc_ref = refs[-1]
            k = pl.program_id(2)

            @pl.when(k == 0)
            def _():
                acc_ref[...] = part

            @pl.when(jnp.logical_and(k > 0, k < gk - 1))
            def _():
                acc_ref[...] += part

            @pl.when(k == gk - 1)
            def _():
                finish(acc_ref[...] + part)

    return pl.pallas_call(
        body, name=name, grid=(gm, gn, gk), in_specs=in_specs, out_specs=o_spec,
        out_shape=jax.ShapeDtypeStruct(o_shape, out_dtype),
        scratch_shapes=([pltpu.VMEM((tm, tn), F32)] if gk > 1 else []),
        compiler_params=_params(("parallel", "parallel", "arbitrary")),
    )(*args)


def _row_tile():
    return _pick(SEQ, 256, 8)


def _rms_fwd(x, g, name):
    tm = _row_tile()
    (h,) = _tiled(lambda xv, gv: _rms(xv, gv), name, (SEQ // tm,),
                  [(x, (tm, D_MODEL), lambda i: (i, 0)), (g, (1, D_MODEL), lambda i: (0, 0))],
                  [((SEQ, D_MODEL), BF16, (tm, D_MODEL), lambda i: (i, 0), ())])
    return h


def _rms_bwd(x, g, dh, dres, name):
    tm = _row_tile()

    def fn(xv, gv, dhv, drv):
        _, vjp = jax.vjp(_rms, xv, gv)
        dx, dg = vjp(dhv)
        dx = dx + drv
        return dx, dx, dg

    row = lambda i: (i, 0)
    return _tiled(fn, name, (SEQ // tm,),
                  [(x, (tm, D_MODEL), row), (g, (1, D_MODEL), lambda i: (0, 0)),
                   (dh, (tm, D_MODEL), row), (dres, (tm, D_MODEL), row)],
                  [((SEQ, D_MODEL), F32, (tm, D_MODEL), row, ()),
                   ((SEQ, D_MODEL), BF16, (tm, D_MODEL), row, ()),
                   ((1, D_MODEL), F32, (1, D_MODEL), lambda i: (0, 0), (0,))])


def _loss_head(x, g, target):
    tm = _row_tile()

    def fn(xv, gv, tv):
        y, vjp = jax.vjp(_rms, xv, gv)
        err = y - tv
        part = 0.5 * jnp.sum(jnp.sum(err * err, axis=1, keepdims=True), axis=0, keepdims=True) / D_MODEL
        dx, dg = vjp(err / D_MODEL)
        return part, dx, dx, dg

    row = lambda i: (i, 0)
    one = lambda i: (0, 0)
    return _tiled(fn, "loss_head", (SEQ // tm,),
                  [(x, (tm, D_MODEL), row), (g, (1, D_MODEL), one), (target, (tm, D_MODEL), row)],
                  [((1, 1), F32, (1, 1), one, (0,)),
                   ((SEQ, D_MODEL), F32, (tm, D_MODEL), row, ()),
                   ((SEQ, D_MODEL), BF16, (tm, D_MODEL), row, ()),
                   ((1, D_MODEL), F32, (1, D_MODEL), one, (0,))])


def _p_col(width, index):
    return lambda tm: ((tm, width), lambda i: (i, index))


def _merge_fn(ga, gb, ta, tb):
    return _sigmoid(ga) * ta + _sigmoid(gb) * tb


def _merge_fwd(p, ta, tb):
    tm = _row_tile()
    d = D_MODEL
    ga_blk = 4 * _gw() + 2 * _sw()
    assert ga_blk % d == 0
    ia, ib = ga_blk // d, ga_blk // d + 1
    row = lambda i: (i, 0)
    (m,) = _tiled(_merge_fn, "merge_fwd", (SEQ // tm,),
                  [(p, (tm, d), lambda i: (i, ia)), (p, (tm, d), lambda i: (i, ib)),
                   (ta, (tm, d), row), (tb, (tm, d), row)],
                  [((SEQ, d), BF16, (tm, d), row, ())])
    return m


def _merge_bwd(p, ta, tb, dm):
    tm = _row_tile()
    d = D_MODEL
    ga_blk = 4 * _gw() + 2 * _sw()
    ia, ib = ga_blk // d, ga_blk // d + 1

    def fn(ga, gb, tav, tbv, dmv):
        _, vjp = jax.vjp(_merge_fn, ga, gb, tav, tbv)
        return vjp(dmv)

    row = lambda i: (i, 0)
    out = ((SEQ, d), BF16, (tm, d), row, ())
    return _tiled(fn, "merge_bwd", (SEQ // tm,),
                  [(p, (tm, d), lambda i: (i, ia)), (p, (tm, d), lambda i: (i, ib)),
                   (ta, (tm, d), row), (tb, (tm, d), row), (dm, (tm, d), row)],
                  [out, out, out, out])


def _gate_fn(ab, alog, dtb):
    lane = lax.broadcasted_iota(jnp.int32, ab.shape, 1)
    nd = 2 * GDN_HEADS
    g = -jnp.exp(alog) * _softplus(ab + dtb)
    beta = _sigmoid(ab)
    return jnp.where(lane < nd, g, jnp.where(lane < 2 * nd, beta, 0.0))


def _ab_index():
    off = 4 * _gw() + 2 * _sw() + 2 * D_MODEL
    assert off % AB_PAD == 0
    return off // AB_PAD


def _gate_fwd(p, alog, dtb):
    tm = _row_tile()
    iab = _ab_index()
    one = lambda i: (0, 0)
    (g,) = _tiled(_gate_fn, "gdn_gate_fwd", (SEQ // tm,),
                  [(p, (tm, AB_PAD), lambda i: (i, iab)), (alog, (1, AB_PAD), one), (dtb, (1, AB_PAD), one)],
                  [((SEQ, AB_PAD), F32, (tm, AB_PAD), lambda i: (i, 0), ())])
    return g


def _gate_bwd(p, alog, dtb, dg):
    tm = _row_tile()
    iab = _ab_index()
    one = lambda i: (0, 0)

    def fn(ab, al, db, dgv):
        _, vjp = jax.vjp(_gate_fn, ab, al, db)
        return vjp(dgv)

    return _tiled(fn, "gdn_gate_bwd", (SEQ // tm,),
                  [(p, (tm, AB_PAD), lambda i: (i, iab)), (alog, (1, AB_PAD), one), (dtb, (1, AB_PAD), one),
                   (dg, (tm, AB_PAD), lambda i: (i, 0))],
                  [((SEQ, AB_PAD), BF16, (tm, AB_PAD), lambda i: (i, 0), ()),
                   ((1, AB_PAD), F32, (1, AB_PAD), one, (0,)),
                   ((1, AB_PAD), F32, (1, AB_PAD), one, (0,))])


def _l2n(x):
    return x * lax.rsqrt(jnp.sum(x * x, axis=-1, keepdims=True) + NORM_EPS)


def _qkv_fn(xq, xk, xv, *taps):
    n = QKV_CONV
    q = _l2n(_silu(_conv_rows(xq, taps[0:n])))
    k = _l2n(_silu(_conv_rows(xk, taps[n:2 * n])))
    v = _silu(_conv_rows(xv, taps[2 * n:3 * n]))
    return q, k, v


def _qkv_specs(p, conv_w):
    hd, nh = HEAD_DIM, GDN_HEADS
    ins = [(p, (SEQ, hd), (lambda h, s=s: (0, s * nh + h))) for s in range(3)]
    ins += [(conv_w, (QKV_CONV, hd), (lambda h, s=s: (0, s * nh + h))) for s in range(3)]
    return ins


def _qkv_fwd(p, conv_w):
    hd, nh = HEAD_DIM, GDN_HEADS
    ins = _qkv_specs(p, conv_w)
    n_in = len(ins)
    out_spec = pl.BlockSpec((SEQ, hd), lambda h: (0, h))

    def body(*refs):
        xs = [r[...] for r in refs[:3]]
        taps = [refs[3 + s][j:j + 1, :] for s in range(3) for j in range(QKV_CONV)]
        q, k, v = _qkv_fn(*xs, *taps)
        refs[n_in][...] = q
        refs[n_in + 1][...] = k
        refs[n_in + 2][...] = v

    return pl.pallas_call(
        body, name="gdn_qkv_fwd", grid=(nh,),
        in_specs=[pl.BlockSpec(b, m) for _, b, m in ins], out_specs=[out_spec] * 3,
        out_shape=[jax.ShapeDtypeStruct((SEQ, nh * hd), F32)] * 3,
        compiler_params=_params(("parallel",)),
    )(*[a for a, _, _ in ins])


def _qkv_bwd(p, conv_w, dq, dk, dv):
    hd, nh = HEAD_DIM, GDN_HEADS
    ins = _qkv_specs(p, conv_w) + [(t, (SEQ, hd), lambda h: (0, h)) for t in (dq, dk, dv)]
    n_in = len(ins)

    def body(*refs):
        xs = [r[...] for r in refs[:3]]
        taps = [refs[3 + s][j:j + 1, :] for s in range(3) for j in range(QKV_CONV)]
        cts = tuple(r[...] for r in refs[6:9])
        _, vjp = jax.vjp(_qkv_fn, *xs, *taps)
        grads = vjp(cts)
        for s in range(3):
            refs[n_in + s][...] = grads[s].astype(BF16)
            for j in range(QKV_CONV):
                refs[n_in + 3 + s][j:j + 1, :] = jnp.sum(grads[3 + s * QKV_CONV + j], axis=0, keepdims=True)

    dx_spec = pl.BlockSpec((SEQ, hd), lambda h: (0, h))
    dw_spec = pl.BlockSpec((QKV_CONV, hd), lambda h: (0, h))
    outs = pl.pallas_call(
        body, name="gdn_qkv_bwd", grid=(nh,),
        in_specs=[pl.BlockSpec(b, m) for _, b, m in ins], out_specs=[dx_spec] * 3 + [dw_spec] * 3,
        out_shape=[jax.ShapeDtypeStruct((SEQ, nh * hd), BF16)] * 3
        + [jax.ShapeDtypeStruct((QKV_CONV, nh * hd), F32)] * 3,
        compiler_params=_params(("parallel",)),
    )(*[a for a, _, _ in ins])
    return outs[0:3], jnp.concatenate(outs[3:6], axis=1)


def _prep_fn(qs, ks, vs, gblk):
    nh = len(qs)
    c = qs[0].shape[0]
    scale = HEAD_DIM ** -0.5
    ii = lax.broadcasted_iota(jnp.int32, (c, c), 0)
    jj = lax.broadcasted_iota(jnp.int32, (c, c), 1)
    eye = (ii == jj).astype(F32)
    lane = lax.broadcasted_iota(jnp.int32, gblk.shape, 1)
    kk_t = [_mm_nt(k, k) for k in ks]
    qk_t = [_mm_nt(q, k) for q, k in zip(qs, ks)]
    chains = [(h, d) for h in range(nh) for d in range(2)]
    pre = []
    for h, d in chains:
        g = jnp.sum(jnp.where(lane == h + d * nh, gblk, 0.0), axis=1, keepdims=True)
        beta = jnp.sum(jnp.where(lane == h + (2 + d) * nh, gblk, 0.0), axis=1, keepdims=True)
        incl = (jj <= ii) if d == 0 else (jj >= ii)
        strict = (jj < ii) if d == 0 else (jj > ii)
        incl_t = (ii <= jj) if d == 0 else (ii >= jj)
        g_row = jnp.sum(eye * g, axis=0, keepdims=True)
        gc = jnp.sum(jnp.where(incl, g_row, 0.0), axis=1, keepdims=True)
        gc_row = jnp.sum(jnp.where(incl_t, g, 0.0), axis=0, keepdims=True)
        decay = jnp.where(incl, jnp.exp(jnp.where(incl, gc - gc_row, 0.0)), 0.0)
        a = jnp.where(strict, kk_t[h] * beta * decay, 0.0)
        pre.append((g, beta, gc, decay, incl, a))
    ts = _inv_unit([p[5] for p in pre])
    us = [_mmh(t, vs[h] * p[1]) for (h, d), p, t in zip(chains, pre, ts)]
    ws = [_mmh(t, ks[h] * (p[1] * jnp.exp(p[2]))) for (h, d), p, t in zip(chains, pre, ts)]
    outs = [[None, None] for _ in range(nh)]
    for (h, d), (g, beta, gc, decay, incl, a), u, w in zip(chains, pre, us, ws):
        qk = jnp.where(incl, qk_t[h] * (scale * decay), 0.0)
        g_last = jnp.sum(g, axis=0, keepdims=True)
        q_dec = qs[h] * (scale * jnp.exp(gc))
        k_dec = ks[h] * jnp.exp(g_last - gc)
        outs[h][d] = (u, w, q_dec, k_dec, qk, jnp.exp(g_last))
    return outs


def _n_chunks():
    return SEQ // GDN_CHUNK


def _gdn_prep_fwd(q, k, v, gates):
    c, hd, nh, nc = GDN_CHUNK, HEAD_DIM, GDN_HEADS, _n_chunks()
    gw = nh * hd
    row = lambda n: (n, 0)

    def body(q_ref, k_ref, v_ref, g_ref, u_ref, w_ref, qd_ref, kd_ref, qk_ref, dec_ref):
        heads = [slice(h * hd, (h + 1) * hd) for h in range(nh)]
        res = _prep_fn([q_ref[:, s] for s in heads], [k_ref[:, s] for s in heads], [v_ref[:, s] for s in heads],
                       g_ref[...])
        for h, cols in enumerate(heads):
            for d in range(2):
                u, w, qd, kd, qk, dec = res[h][d]
                u_ref[d, :, cols] = u
                w_ref[d, :, cols] = w
                qd_ref[d, :, cols] = qd
                kd_ref[d, :, cols] = kd
                qk_ref[d, h] = qk
                dec_ref[d, h:h + 1, :] = jnp.broadcast_to(dec, (1, LANES))

    wide = pl.BlockSpec((2, c, gw), lambda n: (0, n, 0))
    return pl.pallas_call(
        body, name="gdn_prep_fwd", grid=(nc,),
        in_specs=[pl.BlockSpec((c, gw), row)] * 3 + [pl.BlockSpec((c, LANES), row)],
        out_specs=[wide] * 4 + [pl.BlockSpec((2, nh, c, c), lambda n: (0, 0, n, 0)),
                                pl.BlockSpec((2, None, nh, LANES), lambda n: (0, n, 0, 0))],
        out_shape=[jax.ShapeDtypeStruct((2, SEQ, gw), F32)] * 4
        + [jax.ShapeDtypeStruct((2, nh, SEQ, c), F32), jax.ShapeDtypeStruct((2, nc, nh, LANES), F32)],
        compiler_params=_params(("parallel",)),
    )(q, k, v, gates)


def _gdn_prep_bwd(q, k, v, gates, cts):
    c, hd, nh, nc = GDN_CHUNK, HEAD_DIM, GDN_HEADS, _n_chunks()
    gw = nh * hd
    row = lambda n: (n, 0)

    def body(*refs):
        q_ref, k_ref, v_ref, g_ref = refs[:4]
        ct_refs = refs[4:16]
        dq_ref, dk_ref, dv_ref, dg_ref = refs[16:20]
        heads = [slice(h * hd, (h + 1) * hd) for h in range(nh)]
        _, vjp = jax.vjp(_prep_fn, [q_ref[:, s] for s in heads], [k_ref[:, s] for s in heads],
                         [v_ref[:, s] for s in heads], g_ref[...])
        cts_in = []
        for h, cols in enumerate(heads):
            per_dir = []
            for d in range(2):
                r = ct_refs[6 * d:6 * d + 6]
                per_dir.append((r[0][:, cols], r[1][:, cols], r[2][:, cols], r[3][:, cols], r[4][h],
                                r[5][h:h + 1, 0:1]))
            cts_in.append(per_dir)
        dqs, dks, dvs, dg = vjp(cts_in)
        for h, cols in enumerate(heads):
            dq_ref[:, cols] = dqs[h]
            dk_ref[:, cols] = dks[h]
            dv_ref[:, cols] = dvs[h]
        dg_ref[...] = dg

    one_dir = [pl.BlockSpec((c, gw), row)] * 4 + [pl.BlockSpec((nh, c, c), lambda n: (0, n, 0)),
                                                  pl.BlockSpec((None, nh, LANES), lambda n: (n, 0, 0))]
    return pl.pallas_call(
        body, name="gdn_prep_bwd", grid=(nc,),
        in_specs=[pl.BlockSpec((c, gw), row)] * 3 + [pl.BlockSpec((c, LANES), row)] + one_dir * 2,
        out_specs=[pl.BlockSpec((c, gw), row)] * 3 + [pl.BlockSpec((c, LANES), row)],
        out_shape=[jax.ShapeDtypeStruct((SEQ, gw), F32)] * 3 + [jax.ShapeDtypeStruct((SEQ, LANES), F32)],
        compiler_params=_params(("parallel",)),
    )(q, k, v, gates, *cts)


def _scan_steps(states, us, ws, qds, kds, qks, decs):
    w_s = [_mm(w, s) for w, s in zip(ws, states)]
    q_s = [_mm(qd, s) for qd, s in zip(qds, states)]
    v_new = [u - x for u, x in zip(us, w_s)]
    outs = [x + _mm(qk, vn) for x, qk, vn in zip(q_s, qks, v_new)]
    k_v = [_mm_tn(kd, vn) for kd, vn in zip(kds, v_new)]
    new_states = [s * dec + x for s, dec, x in zip(states, decs, k_v)]
    return new_states, outs


def _scan_operands(ins, chains, dec_lanes):
    cols = lambda h: slice(h * HEAD_DIM, (h + 1) * HEAD_DIM)
    wide = [[ins[6 * d + j][:, cols(h)] for d, h in chains] for j in range(4)]
    qks = [ins[6 * d + 4][h] for d, h in chains]
    decs = [ins[6 * d + 5][h:h + 1, dec_lanes] for d, h in chains]
    return (*wide, qks, decs)


def _scan_in_specs(chunk_of):
    c, hd, nh = GDN_CHUNK, HEAD_DIM, GDN_HEADS
    gw = nh * hd
    specs = []
    for d in range(2):
        f = chunk_of[d]
        specs += [pl.BlockSpec((None, c, gw), lambda n, d=d, f=f: (d, f(n), 0))] * 4
        specs += [pl.BlockSpec((None, nh, c, c), lambda n, d=d, f=f: (d, 0, f(n), 0)),
                  pl.BlockSpec((None, None, nh, LANES), lambda n, d=d, f=f: (d, f(n), 0, 0))]
    return specs


def _gdn_scan_fwd(u, w, qd, kd, qk, dec):
    c, hd, nh, nc = GDN_CHUNK, HEAD_DIM, GDN_HEADS, _n_chunks()
    gw = nh * hd
    chunk_of = (lambda n: n, lambda n: nc - 1 - n)

    def body(*refs):
        ins = refs[:12]
        o_refs = refs[12:14]
        st_refs = refs[14:16]
        s_ref = refs[16]

        @pl.when(pl.program_id(0) == 0)
        def _():
            s_ref[...] = jnp.zeros_like(s_ref)

        chains = [(d, h) for d in range(2) for h in range(nh)]
        states = [s_ref[d * nh + h] for d, h in chains]
        for (d, h), st in zip(chains, states):
            st_refs[d][h] = st
        new_states, outs = _scan_steps(states, *_scan_operands(ins, chains, slice(None)))
        for (d, h), st, o in zip(chains, new_states, outs):
            s_ref[d * nh + h] = st
            o_refs[d][:, h * hd:(h + 1) * hd] = o

    o_specs = [pl.BlockSpec((c, gw), lambda n, f=f: (f(n), 0)) for f in chunk_of]
    st_specs = [pl.BlockSpec((None, nh, hd, hd), lambda n, f=f: (f(n), 0, 0, 0)) for f in chunk_of]
    return pl.pallas_call(
        body, name="gdn_scan_fwd", grid=(nc,), in_specs=_scan_in_specs(chunk_of),
        out_specs=o_specs + st_specs,
        out_shape=[jax.ShapeDtypeStruct((SEQ, gw), F32)] * 2 + [jax.ShapeDtypeStruct((nc, nh, hd, hd), F32)] * 2,
        scratch_shapes=[pltpu.VMEM((2 * nh, hd, hd), F32)],
        compiler_params=_params(("arbitrary",)),
    )(*([u, w, qd, kd, qk, dec] * 2))


def _gdn_scan_bwd(u, w, qd, kd, qk, dec, st0, st1, do):
    c, hd, nh, nc = GDN_CHUNK, HEAD_DIM, GDN_HEADS, _n_chunks()
    gw = nh * hd
    chunk_of = (lambda n: nc - 1 - n, lambda n: n)

    def body(*refs):
        ins = refs[:12]
        st_in = refs[12:14]
        do_in = refs[14:16]
        outs = refs[16:28]
        ds_ref = refs[28]

        @pl.when(pl.program_id(0) == 0)
        def _():
            ds_ref[...] = jnp.zeros_like(ds_ref)

        chains = [(d, h) for d in range(2) for h in range(nh)]
        states = [st_in[d][h] for d, h in chains]
        _, vjp = jax.vjp(_scan_steps, states, *_scan_operands(ins, chains, slice(0, 1)))
        ct_state = [ds_ref[d * nh + h] for d, h in chains]
        ct_out = [do_in[d][:, h * hd:(h + 1) * hd] for d, h in chains]
        grads = vjp((ct_state, ct_out))
        for i, (d, h) in enumerate(chains):
            cols = slice(h * hd, (h + 1) * hd)
            du_r, dw_r, dqd_r, dkd_r, dqk_r, ddec_r = outs[6 * d:6 * d + 6]
            ds_ref[d * nh + h] = grads[0][i]
            du_r[:, cols] = grads[1][i]
            dw_r[:, cols] = grads[2][i]
            dqd_r[:, cols] = grads[3][i]
            dkd_r[:, cols] = grads[4][i]
            dqk_r[h] = grads[5][i]
            ddec_r[h:h + 1, :] = jnp.broadcast_to(grads[6][i], (1, LANES))

    st_specs = [pl.BlockSpec((None, nh, hd, hd), lambda n, f=f: (f(n), 0, 0, 0)) for f in chunk_of]
    do_specs = [pl.BlockSpec((c, gw), lambda n, f=f: (f(n), 0)) for f in chunk_of]
    out_specs, out_shape = [], []
    for f in chunk_of:
        out_specs += [pl.BlockSpec((c, gw), lambda n, f=f: (f(n), 0))] * 4
        out_specs += [pl.BlockSpec((nh, c, c), lambda n, f=f: (0, f(n), 0)),
                      pl.BlockSpec((None, nh, LANES), lambda n, f=f: (f(n), 0, 0))]
        out_shape += [jax.ShapeDtypeStruct((SEQ, gw), F32)] * 4
        out_shape += [jax.ShapeDtypeStruct((nh, SEQ, c), F32), jax.ShapeDtypeStruct((nc, nh, LANES), F32)]
    return pl.pallas_call(
        body, name="gdn_scan_bwd", grid=(nc,), in_specs=_scan_in_specs(chunk_of) + st_specs + do_specs,
        out_specs=out_specs, out_shape=out_shape,
        scratch_shapes=[pltpu.VMEM((2 * nh, hd, hd), F32)],
        compiler_params=_params(("arbitrary",)),
    )(*([u, w, qd, kd, qk, dec] * 2), st0, st1, do, do)


def _post_fn(o0, o1, z, gn):
    return _rms(o0 + o1, gn) * _silu(z)


def _gdn_post_fwd(o0, o1, p, gn):
    tm, hd, nh = _row_tile(), HEAD_DIM, GDN_HEADS
    zoff = 3 * nh
    blk = lambda i, h: (i, h)
    (ya,) = _tiled(_post_fn, "gdn_post_fwd", (SEQ // tm, nh),
                   [(o0, (tm, hd), blk), (o1, (tm, hd), blk),
                    (p, (tm, hd), lambda i, h: (i, zoff + h)), (gn, (1, hd), lambda i, h: (0, 0))],
                   [((SEQ, nh * hd), BF16, (tm, hd), blk, ())])
    return ya


def _gdn_post_bwd(o0, o1, p, gn, dya):
    tm, hd, nh = _row_tile(), HEAD_DIM, GDN_HEADS
    zoff = 3 * nh
    blk = lambda i, h: (i, h)

    def fn(a, b, z, g, dy):
        _, vjp = jax.vjp(_post_fn, a, b, z, g)
        do, _, dz, dg = vjp(dy.astype(F32))
        return do, dz, dg

    return _tiled(fn, "gdn_post_bwd", (SEQ // tm, nh),
                  [(o0, (tm, hd), blk), (o1, (tm, hd), blk),
                   (p, (tm, hd), lambda i, h: (i, zoff + h)), (gn, (1, hd), lambda i, h: (0, 0)),
                   (dya, (tm, hd), blk)],
                  [((SEQ, nh * hd), F32, (tm, hd), blk, ()),
                   ((SEQ, nh * hd), BF16, (tm, hd), blk, ()),
                   ((1, hd), F32, (1, hd), lambda i, h: (0, 0), (0, 1))],
                  sem=("arbitrary", "arbitrary"))


def _sgu_ln(v, lg, lb):
    gv = _gelu(v)
    mu = jnp.mean(gv, axis=-1, keepdims=True)
    cen = gv - mu
    var = jnp.mean(cen * cen, axis=-1, keepdims=True)
    return cen * lax.rsqrt(var + NORM_EPS) * lg + lb


def _sgu_mix(vn, w_ref, bt_ref):
    parts = []
    for g in range(SGU_GROUPS):
        cols = slice(g * SGU_DIM, (g + 1) * SGU_DIM)
        parts.append(_mm(w_ref[g], vn[:, cols]) + bt_ref[:, g:g + 1])
    return jnp.concatenate(parts, axis=1)


def _sgu_specs(p):
    sw, sb = _sw(), SGU_BLOCK
    uoff = 4 * _gw()
    assert uoff % sw == 0
    iu = uoff // sw
    return [(p, (sb, sw), lambda n: (n, iu)), (p, (sb, sw), lambda n: (n, iu + 1))]


def _sgu_fwd(p, lg, lb, w_s, b_t):
    sw, sb = _sw(), SGU_BLOCK
    ins = _sgu_specs(p)

    def body(u_ref, v_ref, lg_ref, lb_ref, w_ref, bt_ref, y_ref):
        vn = _sgu_ln(v_ref[...], lg_ref[...], lb_ref[...])
        y_ref[...] = (_gelu(u_ref[...]) * _sgu_mix(vn, w_ref, bt_ref)).astype(BF16)

    one2 = lambda n: (0, 0)
    return pl.pallas_call(
        body, name="sgu_fwd", grid=(SEQ // sb,),
        in_specs=[pl.BlockSpec(b, m) for _, b, m in ins]
        + [pl.BlockSpec((1, sw), one2), pl.BlockSpec((1, sw), one2),
           pl.BlockSpec((SGU_GROUPS, sb, sb), lambda n: (0, 0, 0)), pl.BlockSpec((sb, LANES), one2)],
        out_specs=pl.BlockSpec((sb, sw), lambda n: (n, 0)),
        out_shape=jax.ShapeDtypeStruct((SEQ, sw), BF16),
        compiler_params=_params(("parallel",)),
    )(p, p, lg, lb, w_s, b_t)


def _sgu_bwd(p, lg, lb, w_s, b_t, dy):
    sw, sb, ng = _sw(), SGU_BLOCK, SGU_GROUPS
    ins = _sgu_specs(p)

    def body(u_ref, v_ref, lg_ref, lb_ref, w_ref, bt_ref, dy_ref, du_ref, dv_ref, dlg_ref, dlb_ref, dw_ref, dbt_ref):
        first = pl.program_id(0) == 0
        u, v = u_ref[...], v_ref[...]
        gu, gelu_vjp = jax.vjp(_gelu, u)
        vn, ln_vjp = jax.vjp(_sgu_ln, v, lg_ref[...], lb_ref[...])
        s = _sgu_mix(vn, w_ref, bt_ref)
        dyv = dy_ref[...].astype(F32)
        ds = dyv * gu
        (du,) = gelu_vjp(dyv * s)
        lane = lax.broadcasted_iota(jnp.int32, (sb, LANES), 1)
        dvn_parts = []
        dbt = jnp.zeros((sb, LANES), F32)
        for g in range(ng):
            cols = slice(g * SGU_DIM, (g + 1) * SGU_DIM)
            ds_g = ds[:, cols]
            dw_g = _mm_nt(ds_g, vn[:, cols])
            dvn_parts.append(_mm_tn(w_ref[g], ds_g))
            dbt = dbt + jnp.where(lane == g, jnp.sum(ds_g, axis=1, keepdims=True), 0.0)

            @pl.when(first)
            def _(g=g, dw_g=dw_g):
                dw_ref[g] = dw_g

            @pl.when(jnp.logical_not(first))
            def _(g=g, dw_g=dw_g):
                dw_ref[g] += dw_g

        dv, dlg, dlb = ln_vjp(jnp.concatenate(dvn_parts, axis=1))
        du_ref[...] = du.astype(BF16)
        dv_ref[...] = dv.astype(BF16)

        @pl.when(first)
        def _():
            dlg_ref[...] = dlg
            dlb_ref[...] = dlb
            dbt_ref[...] = dbt

        @pl.when(jnp.logical_not(first))
        def _():
            dlg_ref[...] += dlg
            dlb_ref[...] += dlb
            dbt_ref[...] += dbt

    one2 = lambda n: (0, 0)
    row = pl.BlockSpec((sb, sw), lambda n: (n, 0))
    return pl.pallas_call(
        body, name="sgu_bwd", grid=(SEQ // sb,),
        in_specs=[pl.BlockSpec(b, m) for _, b, m in ins]
        + [pl.BlockSpec((1, sw), one2), pl.BlockSpec((1, sw), one2),
           pl.BlockSpec((ng, sb, sb), lambda n: (0, 0, 0)), pl.BlockSpec((sb, LANES), one2), row],
        out_specs=[row, row, pl.BlockSpec((1, sw), one2), pl.BlockSpec((1, sw), one2),
                   pl.BlockSpec((ng, sb, sb), lambda n: (0, 0, 0)), pl.BlockSpec((sb, LANES), one2)],
        out_shape=[jax.ShapeDtypeStruct((SEQ, sw), BF16)] * 2 + [jax.ShapeDtypeStruct((1, sw), F32)] * 2
        + [jax.ShapeDtypeStruct((ng, sb, sb), F32), jax.ShapeDtypeStruct((sb, LANES), F32)],
        compiler_params=_params(("arbitrary",)),
    )(p, p, lg, lb, w_s, b_t, dy)


def _ffn_fn(xg, xv, bg, bv, *taps):
    n = FFN_CONV
    cg = _conv_rows(xg, taps[0:n]) + bg
    cv = _conv_rows(xv, taps[n:2 * n]) + bv
    return _silu(cg) * cv


def _ffn_tile():
    return _pick(D_FF, 256)


def _ffn_specs(up, conv_w, conv_b):
    tc = _ffn_tile()
    nt = D_FF // tc
    ins = [(up, (None, SEQ, tc), (lambda j, s=s: (s, 0, j))) for s in range(2)]
    ins += [(conv_b, (1, tc), (lambda j, s=s: (0, s * nt + j))) for s in range(2)]
    ins += [(conv_w, (FFN_CONV, tc), (lambda j, s=s: (0, s * nt + j))) for s in range(2)]
    return ins


def _ffn_act_fwd(up, conv_w, conv_b):
    tc = _ffn_tile()
    ins = _ffn_specs(up, conv_w, conv_b)

    def body(xg, xv, bg, bv, wg, wv, o_ref):
        taps = [w[j:j + 1, :] for w in (wg, wv) for j in range(FFN_CONV)]
        o_ref[...] = _ffn_fn(xg[...], xv[...], bg[...], bv[...], *taps).astype(BF16)

    return pl.pallas_call(
        body, name="ffn_act_fwd", grid=(D_FF // tc,),
        in_specs=[pl.BlockSpec(b, m) for _, b, m in ins],
        out_specs=pl.BlockSpec((SEQ, tc), lambda j: (0, j)),
        out_shape=jax.ShapeDtypeStruct((SEQ, D_FF), BF16),
        compiler_params=_params(("parallel",)),
    )(*[a for a, _, _ in ins])


def _ffn_act_bwd(up, conv_w, conv_b, dact):
    tc = _ffn_tile()
    nt = D_FF // tc
    ins = _ffn_specs(up, conv_w, conv_b) + [(dact, (SEQ, tc), lambda j: (0, j))]

    def body(xg, xv, bg, bv, wg, wv, dact_ref, dup_ref, dwg_ref, dwv_ref, dbg_ref, dbv_ref):
        taps = [w[j:j + 1, :] for w in (wg, wv) for j in range(FFN_CONV)]
        _, vjp = jax.vjp(_ffn_fn, xg[...], xv[...], bg[...], bv[...], *taps)
        grads = vjp(dact_ref[...].astype(F32))
        dup_ref[0] = grads[0].astype(BF16)
        dup_ref[1] = grads[1].astype(BF16)
        dbg_ref[...] = jnp.sum(grads[2], axis=0, keepdims=True)
        dbv_ref[...] = jnp.sum(grads[3], axis=0, keepdims=True)
        for j in range(FFN_CONV):
            dwg_ref[j:j + 1, :] = jnp.sum(grads[4 + j], axis=0, keepdims=True)
            dwv_ref[j:j + 1, :] = jnp.sum(grads[4 + FFN_CONV + j], axis=0, keepdims=True)

    col = lambda j: (0, j)
    outs = pl.pallas_call(
        body, name="ffn_act_bwd", grid=(nt,),
        in_specs=[pl.BlockSpec(b, m) for _, b, m in ins],
        out_specs=[pl.BlockSpec((2, SEQ, tc), lambda j: (0, 0, j)),
                   pl.BlockSpec((FFN_CONV, tc), col), pl.BlockSpec((FFN_CONV, tc), col),
                   pl.BlockSpec((1, tc), col), pl.BlockSpec((1, tc), col)],
        out_shape=[jax.ShapeDtypeStruct((2, SEQ, D_FF), BF16),
                   jax.ShapeDtypeStruct((FFN_CONV, D_FF), F32), jax.ShapeDtypeStruct((FFN_CONV, D_FF), F32),
                   jax.ShapeDtypeStruct((1, D_FF), F32), jax.ShapeDtypeStruct((1, D_FF), F32)],
        compiler_params=_params(("parallel",)),
    )(*[a for a, _, _ in ins])
    dup, dwg, dwv, dbg, dbv = outs
    return dup, jnp.concatenate([dwg, dwv], axis=1), jnp.concatenate([dbg, dbv], axis=1)


def _pad_lanes(v):
    return jnp.pad(v.reshape(1, -1), ((0, 0), (0, LANES - v.size)))


def _layer_fwd(x, lw, late=None):
    h = _rms_fwd(x, lw['norm_mix_g'], "rms_mix_fwd")
    p = _matmul(h, lw['w_all'], 'nn', F32, "mm_in")
    if late is not None:
        lw = dict(lw, **late(p))
    gates = _gate_fwd(p, lw['a_log'], lw['dt_bias'])
    q, k, v = _qkv_fwd(p, lw['qkv_conv_w'])
    u, w, qd, kd, qk, dec = _gdn_prep_fwd(q, k, v, gates)
    o0, o1, st0, st1 = _gdn_scan_fwd(u, w, qd, kd, qk, dec)
    ya = _gdn_post_fwd(o0, o1, p, lw['gdn_norm_g'])
    yb = _sgu_fwd(p, lw['sgu_ln_g'], lw['sgu_ln_b'], lw['sgu_w'], lw['sgu_bt'])
    ta = _matmul(ya, lw['w_branch_a'], 'nn', F32, "mm_branch_a", b_split=N_CHIPS)
    tb = _matmul(yb, lw['w_branch_b'], 'nn', F32, "mm_branch_b", b_split=N_CHIPS)
    m = _merge_fwd(p, ta, tb)
    x1 = _matmul(m, lw['w_out'], 'nn', F32, "mm_out", add=x)
    h2 = _rms_fwd(x1, lw['norm_ffn_g'], "rms_ffn_fwd")
    up = _matmul(h2, lw['w_up'], 'nn', F32, "mm_up", b_split=N_CHIPS, o_split=2)
    act = _ffn_act_fwd(up, lw['ffn_conv_w'], lw['ffn_conv_b'])
    x2 = _matmul(act, lw['w_down'], 'nn', F32, "mm_down", add=x1)
    saved = dict(x=x, h=h, p=p, gates=gates, q=q, k=k, v=v, u=u, w=w, qd=qd, kd=kd, qk=qk, dec=dec,
                 o0=o0, o1=o1, st0=st0, st1=st1, ya=ya, yb=yb, ta=ta, tb=tb, m=m, x1=x1, h2=h2, up=up, act=act)
    return x2, saved, lw


def _layer_bwd(dx, dx_bf, lw, s, after=(), mid=None, last=None):
    g = {}
    dact = _matmul(dx_bf, lw['w_down'], 'nt', BF16, "mm_down_dgrad", after=after)
    g['w_down'] = _matmul(s['act'], dx_bf, 'tn', PARTIAL, "mm_down_wgrad")
    dup, g['ffn_conv_w'], g['ffn_conv_b'] = _ffn_act_bwd(s['up'], lw['ffn_conv_w'], lw['ffn_conv_b'], dact)
    dh2 = _matmul(dup, lw['w_up'], 'nt', F32, "mm_up_dgrad", a_split=2, b_split=N_CHIPS)
    g['w_up'] = _matmul(s['h2'], dup, 'tn', PARTIAL, "mm_up_wgrad", b_split=2, o_split=N_CHIPS)
    dx1, dx1_bf, g['norm_ffn_g'] = _rms_bwd(s['x1'], lw['norm_ffn_g'], dh2, dx, "rms_ffn_bwd")
    after_mid = mid(g, dh2) if mid is not None else ()
    dm = _matmul(dx1_bf, lw['w_out'], 'nt', F32, "mm_out_dgrad", after=after_mid)
    g['w_out'] = _matmul(s['m'], dx1_bf, 'tn', PARTIAL, "mm_out_wgrad")
    d_ga, d_gb, d_ta, d_tb = _merge_bwd(s['p'], s['ta'], s['tb'], dm)
    dya = _matmul(d_ta, lw['w_branch_a'], 'nt', BF16, "mm_branch_a_dgrad", b_split=N_CHIPS)
    dyb = _matmul(d_tb, lw['w_branch_b'], 'nt', BF16, "mm_branch_b_dgrad", b_split=N_CHIPS)
    g['w_branch_a'] = _matmul(s['ya'], d_ta, 'tn', PARTIAL, "mm_branch_a_wgrad", o_split=N_CHIPS)
    g['w_branch_b'] = _matmul(s['yb'], d_tb, 'tn', PARTIAL, "mm_branch_b_wgrad", o_split=N_CHIPS)
    du_s, dv_s, g['sgu_ln_g'], g['sgu_ln_b'], g['sgu_w'], g['sgu_bt'] = _sgu_bwd(
        s['p'], lw['sgu_ln_g'], lw['sgu_ln_b'], lw['sgu_w'], lw['sgu_bt'], dyb)
    do, dz, g['gdn_norm_g'] = _gdn_post_bwd(s['o0'], s['o1'], s['p'], lw['gdn_norm_g'], dya)
    cts = _gdn_scan_bwd(s['u'], s['w'], s['qd'], s['kd'], s['qk'], s['dec'], s['st0'], s['st1'], do)
    dq, dk, dv, dgates = _gdn_prep_bwd(s['q'], s['k'], s['v'], s['gates'], cts)
    (dxq, dxk, dxv), g['qkv_conv_w'] = _qkv_bwd(s['p'], lw['qkv_conv_w'], dq, dk, dv)
    d_ab, g['a_log'], g['dt_bias'] = _gate_bwd(s['p'], lw['a_log'], lw['dt_bias'], dgates)
    dp = jnp.concatenate([dxq, dxk, dxv, dz, du_s, dv_s, d_ga, d_gb, d_ab], axis=1)
    g['w_all'] = _matmul(s['h'], dp, 'tn', PARTIAL, "mm_in_wgrad")
    after_wgrads = last(g, dp) if last is not None else ()
    dh = _matmul(dp, lw['w_all'], 'nt', F32, "mm_in_dgrad", after=after_wgrads)
    dx0, dx0_bf, g['norm_mix_g'] = _rms_bwd(s['x'], lw['norm_mix_g'], dh, dx1, "rms_mix_bwd")
    return dx0, dx0_bf, g


def _w_in_segments():
    n_ab = 4 * GDN_HEADS
    cut = 4 * _gw()
    n_main = cut + 2 * _sw() + 2 * D_MODEL
    return [(0, cut, 0), (cut, cut + n_ab, n_main), (cut + n_ab, n_main + n_ab, cut)]


def _w_all_from_chips(t):
    width = t.shape[-1]
    runs = sorted(_w_in_segments(), key=lambda s: s[2])
    pieces = []
    for lo, hi, _ in runs:
        for j in range(N_CHIPS):
            a, b = max(lo, j * width), min(hi, (j + 1) * width)
            if a < b:
                pieces.append(t[j][:, a - j * width:b - j * width])
    pieces.append(jnp.zeros((t.shape[1], AB_PAD - 4 * GDN_HEADS), t.dtype))
    return jnp.concatenate(pieces, axis=1)


def _w_in_grad_chips(g_all):
    segs = _w_in_segments()
    width = segs[-1][1] // N_CHIPS
    chips = []
    for j in range(N_CHIPS):
        pieces = []
        for lo, hi, start in segs:
            a, b = max(lo, j * width), min(hi, (j + 1) * width)
            if a < b:
                pieces.append(g_all[:, start + a - lo:start + b - lo])
        chips.append(jnp.concatenate(pieces, axis=1))
    return jnp.stack(chips)


_FIRST = ('w_in', 'qkv_conv_w')
_REST = ('w_branch_a', 'w_branch_b', 'w_out', 'w_up', 'w_down', 'ffn_conv_w')


def _cat_cols(t):
    return jnp.concatenate([t[j] for j in range(N_CHIPS)], axis=-1)


def _layer_weights_rest(big):
    return dict(
        w_branch_a=big['w_branch_a'], w_branch_b=big['w_branch_b'], w_up=big['w_up'],
        w_out=big['w_out'].reshape(D_MODEL, D_MODEL), w_down=big['w_down'].reshape(D_FF, D_MODEL),
        ffn_conv_w=_cat_cols(big['ffn_conv_w']))


def _layer_weights_first(l, big, small):
    return dict(
        w_all=_w_all_from_chips(big['w_in']), qkv_conv_w=_cat_cols(big['qkv_conv_w']),
        norm_mix_g=small['norm_mix_g'][l:l + 1], norm_ffn_g=small['norm_ffn_g'][l:l + 1],
        a_log=_pad_lanes(small['a_log'][l]), dt_bias=_pad_lanes(small['dt_bias'][l]),
        gdn_norm_g=small['gdn_norm_g'][l:l + 1],
        sgu_ln_g=small['sgu_ln_g'][l:l + 1], sgu_ln_b=small['sgu_ln_b'][l:l + 1], sgu_w=small['sgu_w'][l],
        sgu_bt=jnp.pad(small['sgu_b'][l].T, ((0, 0), (0, LANES - SGU_GROUPS))),
        ffn_conv_b=small['ffn_conv_b'][l:l + 1])


_SMALL_GRADS = ('norm_mix_g', 'a_log', 'dt_bias', 'gdn_norm_g', 'sgu_ln_g', 'sgu_ln_b', 'sgu_w', 'sgu_bt',
                'norm_ffn_g', 'ffn_conv_b', 'qkv_conv_w', 'ffn_conv_w')
_BIG = ('w_in', 'w_branch_a', 'w_branch_b', 'w_out', 'w_up', 'w_down')


def _big_grad_slices(g, names=_BIG):
    def one(k):
        if k == 'w_in':
            return _w_in_grad_chips(g['w_all'])
        if k == 'w_out':
            return g[k].reshape(N_CHIPS, D_MODEL // N_CHIPS, D_MODEL)
        if k == 'w_down':
            return g[k].reshape(N_CHIPS, D_FF // N_CHIPS, D_MODEL)
        return g[k]
    return {k: one(k) for k in names}


def _adamw(w, g, m, v, name, after=(), pass_g=False):
    shape = w.shape
    cols = shape[-1]
    rows = w.size // cols
    tr = _pick(rows, max(8, (1 << 18) // cols // 8 * 8), 8) if rows % 8 == 0 else rows

    def fn(wv, gv, mv, vv):
        m2 = ADAM_B1 * mv + (1.0 - ADAM_B1) * gv
        v2 = ADAM_B2 * vv + (1.0 - ADAM_B2) * (gv * gv)
        m_hat = m2 / (1.0 - ADAM_B1 ** ADAM_STEP)
        v_hat = v2 / (1.0 - ADAM_B2 ** ADAM_STEP)
        delta = -ADAM_LR * (m_hat / (jnp.sqrt(v_hat) + ADAM_EPS) + ADAM_WD * wv)
        return (delta, m2, v2, gv) if pass_g else (delta, m2, v2)

    row = lambda i: (i, 0)
    outs = _tiled(fn, name, (rows // tr,),
                  [(t.reshape(rows, cols), (tr, cols), row) for t in (w, g, m, v)],
                  [((rows, cols), F32, (tr, cols), row, ())] * (4 if pass_g else 3), sem=("parallel",),
                  after=after)
    return [o.reshape(shape) for o in outs]


MESH_IDS = pl.DeviceIdType.MESH
ANY = pl.BlockSpec(memory_space=pl.ANY)


def _place():
    x, y, c = lax.axis_index("x"), lax.axis_index("y"), lax.axis_index("c")
    chips = [(1 - x, y), (x, 1 - y), (1 - x, 1 - y)]
    return x, y, c, 2 * x + y, chips


def _chip_index():
    return 2 * lax.axis_index("x") + lax.axis_index("y")


HBM = pl.BlockSpec(memory_space=pltpu.HBM)
SEM = pl.BlockSpec(memory_space=pltpu.SEMAPHORE)
DATAFLOW = pltpu.SideEffectType.DATAFLOW_SIDE_EFFECTING
TOKEN = jax.ShapeDtypeStruct((8, LANES), F32)


def _in_hbm(t):
    return pltpu.with_memory_space_constraint(t, pltpu.HBM)


def _place_shard(w, l, dtype, name):
    _, r, cols = w.shape
    tr = _pick(r, max(16, (1 << 18) // cols // 16 * 16), 16) if r % 16 == 0 else r

    def body(w_ref, o_ref):
        o_ref[...] = w_ref[...].astype(dtype)

    return pl.pallas_call(
        body, name=name, grid=(r // tr,),
        in_specs=[pl.BlockSpec((None, tr, cols), lambda i: (l, i, 0))],
        out_specs=pl.BlockSpec((None, tr, cols), lambda i: (_chip_index(), i, 0)),
        out_shape=jax.ShapeDtypeStruct((N_CHIPS, r, cols), dtype),
        compiler_params=_params(("parallel",)),
    )(w)


def _my_rows(ref_or_shape_rows, c):
    r = ref_or_shape_rows
    if r % 32 == 0:
        return pl.ds(c * (r // 2), r // 2), pl.ds((1 - c) * (r // 2), r // 2), True
    return pl.ds(0, r), pl.ds(0, r), False


def _gather_copies(bufs, send_sems, recv_sems):
    x, y, c, me, chips = _place()
    out = []
    for i, buf in enumerate(bufs):
        mine, _, _ = _my_rows(buf.shape[1], c)
        for j, (cx, cy) in enumerate(chips):
            def rcopy(slab, i=i, j=j, cx=cx, cy=cy):
                return pltpu.make_async_remote_copy(src_ref=slab, dst_ref=slab, send_sem=send_sems.at[3 * i + j],
                                                    recv_sem=recv_sems.at[3 * i + j], device_id=(cx, cy, c),
                                                    device_id_type=MESH_IDS)
            out.append((rcopy(buf.at[me, mine]), rcopy(buf.at[2 * cx + cy, mine])))
    return out


def _gather_start(xs, name, after=()):
    n = len(xs)
    n_after = len(after)

    def body(*refs):
        refs = refs[n + n_after:]
        send_sems, recv_sems = refs[0], refs[1]
        bufs = refs[2:n + 2]
        token = refs[n + 2]
        for out_going, _ in _gather_copies(bufs, send_sems, recv_sems):
            out_going.start()
        token[...] = jnp.zeros_like(token)

    res = pl.pallas_call(
        body, name=name, in_specs=[HBM] * n + [ANY] * n_after,
        out_specs=[SEM, SEM] + [HBM] * n + [pl.BlockSpec(memory_space=pltpu.VMEM)],
        out_shape=[pltpu.SemaphoreType.DMA((3 * n,)), pltpu.SemaphoreType.DMA((3 * n,))]
        + [pltpu.HBM(t.shape, t.dtype) for t in xs] + [TOKEN],
        input_output_aliases={i: i + 2 for i in range(n)},
        compiler_params=pltpu.CompilerParams(has_side_effects=DATAFLOW),
    )(*[_in_hbm(t) for t in xs], *after)
    return res[0], res[1], res[2:2 + n], res[2 + n]


def _gather_wait(bufs, send_sems, recv_sems, after, name):
    n = len(bufs)

    def body(*refs):
        b_refs = refs[:n]
        s_sems, r_sems = refs[n], refs[n + 1]
        for out_going, in_coming in _gather_copies(b_refs, s_sems, r_sems):
            out_going.wait_send()
            in_coming.wait_recv()

    return pl.pallas_call(
        body, name=name, in_specs=[HBM] * n + [SEM, SEM, ANY], out_specs=[HBM] * n,
        out_shape=[pltpu.HBM(t.shape, t.dtype) for t in bufs],
        input_output_aliases={i: i for i in range(n)},
        compiler_params=pltpu.CompilerParams(has_side_effects=DATAFLOW),
    )(*bufs, send_sems, recv_sems, after)


def _gather_forward(bufs, name):
    idx = [i for i, t in enumerate(bufs) if t.shape[1] % 32 == 0]
    xs = [bufs[i] for i in idx]
    n = len(xs)

    def body(*refs):
        o_refs = refs[n:2 * n]
        send_sems, recv_sems = refs[2 * n:]
        x, y, c, _, chips = _place()
        copies = []
        for i in range(n):
            mine, theirs, _ = _my_rows(xs[i].shape[1], c)
            for j, (cx, cy) in enumerate(chips):
                def rcopy(slab, i=i, j=j):
                    return pltpu.make_async_remote_copy(src_ref=slab, dst_ref=slab, send_sem=send_sems.at[3 * i + j],
                                                        recv_sem=recv_sems.at[3 * i + j], device_id=(x, y, 1 - c),
                                                        device_id_type=MESH_IDS)
                copies.append((rcopy(o_refs[i].at[2 * cx + cy, mine]), rcopy(o_refs[i].at[2 * cx + cy, theirs])))
                copies[-1][0].start()
        for out_going, in_coming in copies:
            out_going.wait_send()
            in_coming.wait_recv()

    res = pl.pallas_call(
        body, name=name, in_specs=[ANY] * n, out_specs=[ANY] * n,
        out_shape=[jax.ShapeDtypeStruct(t.shape, t.dtype) for t in xs],
        input_output_aliases={i: i for i in range(n)},
        scratch_shapes=[pltpu.SemaphoreType.DMA((3 * n,)), pltpu.SemaphoreType.DMA((3 * n,))],
        compiler_params=pltpu.CompilerParams(has_side_effects=True),
    )(*xs)
    out = list(bufs)
    for i, t in zip(idx, res):
        out[i] = t
    return out


def _half_tile(rh, cols):
    return _pick(rh, max(16, (1 << 18) // cols // 16 * 16), 16)


N_LAND = 7


def _scatter_copies(p_refs, r_refs, send_sems, recv_sems):
    x, y, c, me, chips = _place()
    outgoing, incoming = [], []
    for i, (p, land) in enumerate(zip(p_refs, r_refs)):
        rh = p.shape[1] // 2

        def copy(src, slot, k_send, k_recv, to, i=i, land=land):
            return pltpu.make_async_remote_copy(
                src_ref=src, dst_ref=land.at[slot], send_sem=send_sems.at[N_LAND * i + k_send],
                recv_sem=recv_sems.at[N_LAND * i + k_recv], device_id=to, device_id_type=MESH_IDS)

        for r, (cx, cy) in enumerate(chips):
            for k in range(2):
                outgoing.append(copy(p.at[2 * cx + cy, pl.ds(k * rh, rh)], 2 * r + c, 2 * r + k, 2 * r + c,
                                     (cx, cy, k)))
                incoming.append(copy(land.at[2 * r + k], 2 * r + k, 2 * r + k, 2 * r + k, (x, y, c)))
        outgoing.append(copy(p.at[me, pl.ds((1 - c) * rh, rh)], 6, 6, 6, (x, y, 1 - c)))
        incoming.append(copy(land.at[6], 6, 6, 6, (x, y, c)))
    return outgoing, incoming


def _scatter_start(ps, name):
    n = len(ps)
    lands = [lax.empty((N_LAND, t.shape[1] // 2, t.shape[2]), t.dtype) for t in ps]

    def body(*refs):
        send_sems, recv_sems = refs[2 * n], refs[2 * n + 1]
        p_refs = refs[2 * n + 2:3 * n + 2]
        r_refs = refs[3 * n + 2:4 * n + 2]
        token = refs[4 * n + 2]
        for cp in _scatter_copies(p_refs, r_refs, send_sems, recv_sems)[0]:
            cp.start()
        token[...] = jnp.zeros_like(token)

    res = pl.pallas_call(
        body, name=name, in_specs=[HBM] * (2 * n),
        out_specs=[SEM, SEM] + [HBM] * (2 * n) + [pl.BlockSpec(memory_space=pltpu.VMEM)],
        out_shape=[pltpu.SemaphoreType.DMA((N_LAND * n,)), pltpu.SemaphoreType.DMA((N_LAND * n,))]
        + [pltpu.HBM(t.shape, t.dtype) for t in list(ps) + lands] + [TOKEN],
        input_output_aliases={i: i + 2 for i in range(2 * n)},
        compiler_params=pltpu.CompilerParams(has_side_effects=DATAFLOW),
    )(*[_in_hbm(t) for t in list(ps) + lands])
    return res[0], res[1], res[2:2 + n], res[2 + n:2 + 2 * n], res[2 + 2 * n]


def _scatter_wait(ps, lands, send_sems, recv_sems, after, name):
    n = len(ps)
    after = tuple(after) if isinstance(after, (tuple, list)) else (after,)

    def body(*refs):
        p_refs, r_refs = refs[:n], refs[n:2 * n]
        s_sems, r_sems = refs[2 * n], refs[2 * n + 1]
        outgoing, incoming = _scatter_copies(p_refs, r_refs, s_sems, r_sems)
        for cp in outgoing:
            cp.wait_send()
        for cp in incoming:
            cp.wait_recv()

    res = pl.pallas_call(
        body, name=name, in_specs=[HBM] * (2 * n) + [SEM, SEM] + [ANY] * len(after), out_specs=[HBM] * (2 * n),
        out_shape=[pltpu.HBM(t.shape, t.dtype) for t in list(ps) + list(lands)],
        input_output_aliases={i: i for i in range(2 * n)},
        compiler_params=pltpu.CompilerParams(has_side_effects=DATAFLOW),
    )(*ps, *lands, send_sems, recv_sems, *after)
    return res[:n], res[n:]


def _reduce_own(p, rcv, acc, l, name):
    nchip, r, cols = p.shape
    rh = r // 2
    tr = _half_tile(rh, cols)
    nt = rh // tr

    def body(*refs):
        p_ref, r_ref = refs[:2]
        o_ref = refs[-1]
        total = p_ref[...].astype(F32)
        for j in range(N_LAND):
            total = total + r_ref[j].astype(F32)
        o_ref[...] = total

    mine = lambda i: lax.axis_index("c") * nt + i
    in_specs = [pl.BlockSpec((None, tr, cols), lambda i: (_chip_index(), mine(i), 0)),
                pl.BlockSpec((N_LAND, tr, cols), lambda i: (0, i, 0))]
    args = [p, rcv]
    aliases = {}
    if acc is not None:
        in_specs.append(ANY)
        args.append(acc)
        aliases = {2: 0}
    return pl.pallas_call(
        body, name=name, grid=(nt,), in_specs=in_specs,
        out_specs=pl.BlockSpec((None, tr, cols), lambda i: (l, mine(i), 0)),
        out_shape=jax.ShapeDtypeStruct((DEPTH, r, cols), F32), input_output_aliases=aliases,
        compiler_params=_params(("parallel",)),
    )(*args)


def _share_halves(fs, l):
    n = len(fs)

    def body(*refs):
        o_refs = refs[n:2 * n]
        send_sems, recv_sems = refs[2 * n:]
        x, y, c, _, _ = _place()

        def halves(i):
            rh = fs[i].shape[1] // 2
            return o_refs[i].at[l, pl.ds(c * rh, rh), :], o_refs[i].at[l, pl.ds((1 - c) * rh, rh), :]

        def copy(i, rows):
            return pltpu.make_async_remote_copy(src_ref=rows, dst_ref=rows, send_sem=send_sems.at[i],
                                                recv_sem=recv_sems.at[i], device_id=(x, y, 1 - c),
                                                device_id_type=MESH_IDS)

        for i in range(n):
            copy(i, halves(i)[0]).start()
        for i in range(n):
            mine, theirs = halves(i)
            copy(i, mine).wait_send()
            copy(i, theirs).wait_recv()

    return pl.pallas_call(
        body, name="grad_share_halves", in_specs=[ANY] * n, out_specs=[ANY] * n,
        out_shape=[jax.ShapeDtypeStruct(t.shape, t.dtype) for t in fs],
        input_output_aliases={i: i for i in range(n)},
        scratch_shapes=[pltpu.SemaphoreType.DMA((n,)), pltpu.SemaphoreType.DMA((n,))],
        compiler_params=pltpu.CompilerParams(has_side_effects=True),
    )(*fs)


def _reduce_scatter_begin(slices, l, tag):
    names = list(slices)
    send_sems, recv_sems, ps, lands, token = _scatter_start([slices[k] for k in names],
                                                            "grad_scatter_start_%s%d" % (tag, l))
    return dict(names=names, ps=ps, lands=lands, sems=(send_sems, recv_sems), token=token, l=l, tag=tag)


def _reduce_scatter_end(st, stacked, after):
    l = st['l']
    ps, rs = _scatter_wait(st['ps'], st['lands'], *st['sems'], after, "grad_scatter_wait_%s%d" % (st['tag'], l))
    fs = [_reduce_own(p, r, stacked.get(k), l, "grad_reduce_own_" + k)
          for k, p, r in zip(st['names'], ps, rs)]
    stacked.update(zip(st['names'], _share_halves(fs, l)))


def _all_reduce_small(packed, after=()):
    rows = packed.shape[0]
    rh = rows // 2
    tr = _pick(rh, 512, 8)

    def body(x_ref, *rest):
        o_ref, sib, chip_sums, send_sems, recv_sems = rest[len(after):]
        x, y, c, me, chips = _place()
        sibling = (x, y, 1 - c)
        mine, theirs = pl.ds(c * rh, rh), pl.ds((1 - c) * rh, rh)

        def copy(src, dst, k, to):
            return pltpu.make_async_remote_copy(src_ref=src, dst_ref=dst, send_sem=send_sems.at[k],
                                                recv_sem=recv_sems.at[k], device_id=to, device_id_type=MESH_IDS)

        def tiles(fn):
            @pl.loop(0, rh // tr)
            def _(t):
                fn(pl.ds(pl.multiple_of(t * tr, 8), tr))

        to_sibling = copy(x_ref.at[theirs], sib, 0, sibling)
        to_sibling.start()
        to_sibling.wait()

        def chip_sum(sl):
            chip_sums[me, sl, :] = x_ref.at[mine][sl, :] + sib[sl, :]
        tiles(chip_sum)

        out = [copy(chip_sums.at[me], chip_sums.at[me], 1 + j, (cx, cy, c)) for j, (cx, cy) in enumerate(chips)]
        for cp in out:
            cp.start()
        for j, (cx, cy) in enumerate(chips):
            copy(chip_sums.at[2 * cx + cy], chip_sums.at[2 * cx + cy], 1 + j, (x, y, c)).wait_recv()
        for cp in out:
            cp.wait_send()

        def total(sl):
            acc = chip_sums[0, sl, :]
            for s in range(1, N_CHIPS):
                acc = acc + chip_sums[s, sl, :]
            o_ref.at[mine][sl, :] = acc
        tiles(total)

        share = copy(o_ref.at[mine], o_ref.at[mine], 4, sibling)
        share.start()
        share.wait_send()
        copy(o_ref.at[theirs], o_ref.at[theirs], 4, sibling).wait_recv()

    vm = pl.BlockSpec(memory_space=pltpu.VMEM)
    return pl.pallas_call(
        body, name="all_reduce_small", in_specs=[vm] + [ANY] * len(after), out_specs=vm,
        out_shape=jax.ShapeDtypeStruct(packed.shape, F32),
        scratch_shapes=[pltpu.VMEM((rh, LANES), F32), pltpu.VMEM((N_CHIPS, rh, LANES), F32),
                        pltpu.SemaphoreType.DMA((5,)), pltpu.SemaphoreType.DMA((5,))],
        compiler_params=pltpu.CompilerParams(vmem_limit_bytes=VMEM_LIMIT, has_side_effects=True),
    )(packed, *after)


_WEIGHTS = ('norm_mix_g', 'w_in', 'qkv_conv_w', 'a_log', 'dt_bias', 'gdn_norm_g', 'w_branch_a', 'sgu_ln_g',
            'sgu_ln_b', 'sgu_w', 'sgu_b', 'w_branch_b', 'w_out', 'norm_ffn_g', 'w_up', 'ffn_conv_w', 'ffn_conv_b',
            'w_down', 'final_norm_g')


def _local_step(x, target, weights_of, final_norm_g, bwd_after=None, mid_layer=None, last_layer=None):
    lws, saves = [], []
    for l in range(DEPTH):
        lw, late = weights_of(l, x)
        x, s, lw = _layer_fwd(x, lw, late)
        lws.append(lw)
        saves.append(s)
    loss, dx, dx_bf, d_final = _loss_head(x, final_norm_g.reshape(1, -1), target)
    grads = [None] * DEPTH
    for l in reversed(range(DEPTH)):
        after = bwd_after(l) if bwd_after is not None else ()
        mid = (lambda g, dh2, l=l: mid_layer(l, g, dh2)) if mid_layer is not None else None
        last = (lambda g, dp, l=l: last_layer(l, g, dp)) if last_layer is not None else None
        dx, dx_bf, grads[l] = _layer_bwd(dx, dx_bf, lws[l], saves[l], after, mid, last)
    return loss, dx, grads, d_final


def _pack_small(grads, d_final):
    parts = [grads[l][k].reshape(-1) for l in range(DEPTH) for k in _SMALL_GRADS] + [d_final.reshape(-1)]
    flat = jnp.concatenate(parts)
    rows = -(-flat.size // (16 * LANES)) * 16
    return jnp.pad(flat, (0, rows * LANES - flat.size)).reshape(rows, LANES), [p.size for p in parts]


def _unpack_small(packed, grads, d_final):
    flat = packed.reshape(-1)
    out, off = [], 0
    for l in range(DEPTH):
        d = {}
        for k in _SMALL_GRADS:
            t = grads[l][k]
            d[k] = flat[off:off + t.size].reshape(t.shape)
            off += t.size
        out.append(d)
    return out, flat[off:off + d_final.size].reshape(d_final.shape)


def kernel(x, norm_mix_g, w_in, qkv_conv_w, a_log, dt_bias, gdn_norm_g, w_branch_a, sgu_ln_g, sgu_ln_b, sgu_w, sgu_b, w_branch_b, w_out, norm_ffn_g, w_up, ffn_conv_w, ffn_conv_b, w_down, final_norm_g, loss_target, m_norm_mix_g, m_w_in, m_qkv_conv_w, m_a_log, m_dt_bias, m_gdn_norm_g, m_w_branch_a, m_sgu_ln_g, m_sgu_ln_b, m_sgu_w, m_sgu_b, m_w_branch_b, m_w_out, m_norm_ffn_g, m_w_up, m_ffn_conv_w, m_ffn_conv_b, m_w_down, m_final_norm_g, v_norm_mix_g, v_w_in, v_qkv_conv_w, v_a_log, v_dt_bias, v_gdn_norm_g, v_w_branch_a, v_sgu_ln_g, v_sgu_ln_b, v_sgu_w, v_sgu_b, v_w_branch_b, v_w_out, v_norm_ffn_g, v_w_up, v_ffn_conv_w, v_ffn_conv_b, v_w_down, v_final_norm_g):
    w = dict(norm_mix_g=norm_mix_g, w_in=w_in, qkv_conv_w=qkv_conv_w, a_log=a_log, dt_bias=dt_bias,
             gdn_norm_g=gdn_norm_g, w_branch_a=w_branch_a, sgu_ln_g=sgu_ln_g, sgu_ln_b=sgu_ln_b, sgu_w=sgu_w,
             sgu_b=sgu_b, w_branch_b=w_branch_b, w_out=w_out, norm_ffn_g=norm_ffn_g, w_up=w_up,
             ffn_conv_w=ffn_conv_w, ffn_conv_b=ffn_conv_b, w_down=w_down, final_norm_g=final_norm_g)
    m = dict(norm_mix_g=m_norm_mix_g, w_in=m_w_in, qkv_conv_w=m_qkv_conv_w, a_log=m_a_log, dt_bias=m_dt_bias,
             gdn_norm_g=m_gdn_norm_g, w_branch_a=m_w_branch_a, sgu_ln_g=m_sgu_ln_g, sgu_ln_b=m_sgu_ln_b,
             sgu_w=m_sgu_w, sgu_b=m_sgu_b, w_branch_b=m_w_branch_b, w_out=m_w_out, norm_ffn_g=m_norm_ffn_g,
             w_up=m_w_up, ffn_conv_w=m_ffn_conv_w, ffn_conv_b=m_ffn_conv_b, w_down=m_w_down,
             final_norm_g=m_final_norm_g)
    v = dict(norm_mix_g=v_norm_mix_g, w_in=v_w_in, qkv_conv_w=v_qkv_conv_w, a_log=v_a_log, dt_bias=v_dt_bias,
             gdn_norm_g=v_gdn_norm_g, w_branch_a=v_w_branch_a, sgu_ln_g=v_sgu_ln_g, sgu_ln_b=v_sgu_ln_b,
             sgu_w=v_sgu_w, sgu_b=v_sgu_b, w_branch_b=v_w_branch_b, w_out=v_w_out, norm_ffn_g=v_norm_ffn_g,
             w_up=v_w_up, ffn_conv_w=v_ffn_conv_w, ffn_conv_b=v_ffn_conv_b, w_down=v_w_down,
             final_norm_g=v_final_norm_g)
    chip = _chip_index()

    in_flight = []
    for l in range(DEPTH):
        for tag, names in (("first", _FIRST), ("rest", _REST)):
            placed = [_place_shard(w[k], l, BF16 if k in _BIG else F32, "place_" + k) for k in names]
            earlier = (in_flight[-1][3],) if in_flight else ()
            in_flight.append(_gather_start(placed, "gather_start_%s_%d" % (tag, l), earlier))

    def arrived(l, tag, names, after):
        send_sems, recv_sems, bufs, _ = in_flight[2 * l + (tag == "rest")]
        bufs = _gather_wait(bufs, send_sems, recv_sems, after, "gather_wait_%s_%d" % (tag, l))
        return dict(zip(names, _gather_forward(bufs, "gather_forward_" + tag)))

    def weights_of(l, x_in):
        first = arrived(l, "first", _FIRST, in_flight[-1][3] if l == 0 else x_in)
        return (_layer_weights_first(l, first, w),
                lambda p: _layer_weights_rest(arrived(l, "rest", _REST, p)))

    group_a, group_b = ('w_up', 'w_down'), ('w_in', 'w_branch_a', 'w_branch_b', 'w_out')
    big_g = {}
    pending = {'a': None, 'b': None}

    def finish(tag, after):
        if pending[tag] is not None:
            _reduce_scatter_end(pending[tag], big_g, after)
            pending[tag] = None

    def bwd_after(l):
        return () if pending['b'] is None else (pending['b']['token'],)

    def mid_layer(l, g, dh2):
        finish('b', dh2)
        pending['a'] = _reduce_scatter_begin(_big_grad_slices(g, group_a), l, 'a')
        return (pending['a']['token'],)

    def last_layer(l, g, dp):
        finish('a', dp)
        pending['b'] = _reduce_scatter_begin(_big_grad_slices(g, group_b), l, 'b')
        return (pending['b']['token'],)

    loss, grad_x, grads, d_final = _local_step(x[0], loss_target[0], weights_of, w['final_norm_g'], bwd_after,
                                               mid_layer, last_layer)
    loss = lax.psum(loss[0, 0], ("x", "y", "c"))
    packed, _ = _pack_small(grads, d_final)
    reduced_small = _all_reduce_small(packed, (pending['b']['token'],))
    small_g, d_final = _unpack_small(reduced_small, grads, d_final)
    deltas, new_m, new_v, g_out = {}, {}, {}, {}
    for k in group_a:
        deltas[k], new_m[k], new_v[k], g_out[k] = _adamw(w[k], big_g[k].reshape(w[k].shape), m[k], v[k],
                                                         "adamw_" + k, (pending['b']['token'],), pass_g=True)
    finish('b', (reduced_small,) + tuple(deltas[k] for k in group_a))

    def stack(k):
        return jnp.stack([small_g[l][k] for l in range(DEPTH)])

    nd = 2 * GDN_HEADS
    for k in ('norm_mix_g', 'gdn_norm_g', 'sgu_ln_g', 'sgu_ln_b', 'norm_ffn_g', 'ffn_conv_b'):
        g_out[k] = stack(k).reshape(w[k].shape)
    g_out['sgu_w'] = stack('sgu_w')
    g_out['a_log'] = stack('a_log')[:, 0, :nd].reshape(w['a_log'].shape)
    g_out['dt_bias'] = stack('dt_bias')[:, 0, :nd].reshape(w['dt_bias'].shape)
    g_out['sgu_b'] = jnp.swapaxes(stack('sgu_bt')[:, :, :SGU_GROUPS], 1, 2)
    for k in ('qkv_conv_w', 'ffn_conv_w'):
        full = stack(k)
        width = w[k].shape[-1]
        g_out[k] = lax.dynamic_slice_in_dim(full, chip * width, width, axis=2)
    g_out['final_norm_g'] = d_final.reshape(w['final_norm_g'].shape)

    for k in group_b:
        deltas[k], new_m[k], new_v[k], g_out[k] = _adamw(w[k], big_g[k].reshape(w[k].shape), m[k], v[k],
                                                         "adamw_" + k, pass_g=True)
    for k in _WEIGHTS:
        if k not in _BIG:
            deltas[k], new_m[k], new_v[k] = _adamw(w[k], g_out[k], m[k], v[k], "adamw_" + k)
    return (loss, grad_x[None], *[g_out[k] for k in _WEIGHTS], *[deltas[k] for k in _WEIGHTS],
            *[new_m[k] for k in _WEIGHTS], *[new_v[k] for k in _WEIGHTS])
```

```python
import functools
import math

import jax
import jax.numpy as jnp
from jax import lax
from jax.experimental import pallas as pl
from jax.experimental.pallas import tpu as pltpu

F32 = jnp.float32
BF16 = jnp.bfloat16
PARTIAL = BF16

D_MODEL = 2048
SEQ = 2048
DEPTH = 4
GDN_HEADS = 8
HEAD_DIM = 128
GDN_CHUNK = 64
QKV_CONV = 5
SGU_GROUPS = 8
SGU_DIM = 128
SGU_BLOCK = 128
D_FF = 5632
FFN_CONV = 3
NORM_EPS = 1e-6
N_CHIPS = 4

ADAM_LR = 0.001
ADAM_B1 = 0.9
ADAM_B2 = 0.999
ADAM_EPS = 1e-08
ADAM_WD = 0.01
ADAM_STEP = 10

LANES = 128
VMEM_LIMIT = 56 * 1024 * 1024
AB_PAD = LANES


def _gw():
    return GDN_HEADS * HEAD_DIM


def _sw():
    return SGU_GROUPS * SGU_DIM


def _n_all():
    return 4 * _gw() + 2 * _sw() + 2 * D_MODEL + AB_PAD


def _pick(n, target, align=LANES):
    if n <= target:
        return n
    best = None
    t = align
    while t <= target:
        if n % t == 0:
            best = t
        t += align
    assert best is not None, (n, target, align)
    return best


def _params(sem=None):
    return pltpu.CompilerParams(dimension_semantics=sem, vmem_limit_bytes=VMEM_LIMIT)


def _dot(a, b, dims, hi):
    dn = (dims, ((), ()))
    if not hi:
        return lax.dot_general(a.astype(BF16), b.astype(BF16), dn, preferred_element_type=F32)
    a_hi, b_hi = a.astype(BF16), b.astype(BF16)
    a_lo = (a - a_hi.astype(F32)).astype(BF16)
    b_lo = (b - b_hi.astype(F32)).astype(BF16)
    d = lambda p, q: lax.dot_general(p, q, dn, preferred_element_type=F32)
    return d(a_hi, b_hi) + (d(a_hi, b_lo) + d(a_lo, b_hi))


def _make_mm(hi):
    @jax.custom_vjp
    def mm(a, b):
        return _dot(a, b, ((1,), (0,)), hi)

    @jax.custom_vjp
    def mm_nt(a, b):
        return _dot(a, b, ((1,), (1,)), hi)

    @jax.custom_vjp
    def mm_tn(a, b):
        return _dot(a, b, ((0,), (0,)), hi)

    mm.defvjp(lambda a, b: (mm(a, b), (a, b)), lambda r, g: (mm_nt(g, r[1]), mm_tn(r[0], g)))
    mm_nt.defvjp(lambda a, b: (mm_nt(a, b), (a, b)), lambda r, g: (mm(g, r[1]), mm_tn(g, r[0])))
    mm_tn.defvjp(lambda a, b: (mm_tn(a, b), (a, b)), lambda r, g: (mm_nt(r[1], g), mm(r[0], g)))
    return mm, mm_nt, mm_tn


_mm, _mm_nt, _mm_tn = _make_mm(False)
_mmh, _mmh_nt, _mmh_tn = _make_mm(True)


def _shift_rows_raw(x, s):
    if s == 0:
        return x
    n = x.shape[0]
    rolled = pltpu.roll(x, (-s) % n, 0)
    t = lax.broadcasted_iota(jnp.int32, x.shape, 0)
    ok = (t + s >= 0) & (t + s < n)
    return jnp.where(ok, rolled, 0.0)


@functools.partial(jax.custom_vjp, nondiff_argnums=(1,))
def _shift_rows(x, s):
    return _shift_rows_raw(x, s)


_shift_rows.defvjp(lambda x, s: (_shift_rows_raw(x, s), None), lambda s, _, g: (_shift_rows_raw(g, -s),))


def _sigmoid(x):
    return 1.0 / (1.0 + jnp.exp(-x))


def _silu(x):
    return x * _sigmoid(x)


def _gelu(x):
    return 0.5 * x * (1.0 + jnp.tanh(math.sqrt(2.0 / math.pi) * (x + 0.044715 * x * x * x)))


def _softplus(x):
    return jnp.maximum(x, 0.0) + jnp.log(1.0 + jnp.exp(-jnp.abs(x)))


def _rms(x, g):
    return x * lax.rsqrt(jnp.mean(x * x, axis=-1, keepdims=True) + NORM_EPS) * g


def _conv_rows(x, taps):
    pad = len(taps) // 2
    acc = None
    for j, w in enumerate(taps):
        term = _shift_rows(x, j - pad) * w
        acc = term if acc is None else acc + term
    return acc


@jax.custom_vjp
def _inv_unit(mats):
    n = mats[0].shape[0]
    eye = (lax.broadcasted_iota(jnp.int32, (n, n), 0) == lax.broadcasted_iota(jnp.int32, (n, n), 1)).astype(F32)
    ps = [eye - a for a in mats]
    aks = list(mats)
    for _ in range(int(math.log2(n)) - 1):
        aks = [_mmh(ak, ak) for ak in aks]
        ps = [p + _mmh(p, ak) for p, ak in zip(ps, aks)]
    return ps


def _inv_unit_fwd(mats):
    ts = _inv_unit(mats)
    return ts, ts


def _inv_unit_bwd(ts, gs):
    inner = [_mmh_nt(g, t) for g, t in zip(gs, ts)]
    return ([-_mmh_tn(t, m) for t, m in zip(ts, inner)],)


_inv_unit.defvjp(_inv_unit_fwd, _inv_unit_bwd)


def _tiled(fn, name, grid, ins, outs, sem=None, after=()):
    n_in = len(ins)

    def body(*refs):
        vals = [r[...] for r in refs[:n_in]]
        res = fn(*vals)
        if not isinstance(res, (tuple, list)):
            res = (res,)
        for o_ref, r, spec in zip(refs[n_in + len(after):], res, outs):
            acc = spec[4]
            if not acc:
                o_ref[...] = r.astype(o_ref.dtype)
            else:
                first = functools.reduce(jnp.logical_and, [pl.program_id(a) == 0 for a in acc])

                @pl.when(first)
                def _(o_ref=o_ref, r=r):
                    o_ref[...] = r.astype(o_ref.dtype)

                @pl.when(jnp.logical_not(first))
                def _(o_ref=o_ref, r=r):
                    o_ref[...] += r.astype(o_ref.dtype)

    return pl.pallas_call(
        body, name=name, grid=grid,
        in_specs=[pl.BlockSpec(b, m) for _, b, m in ins] + [pl.BlockSpec(memory_space=pl.ANY)] * len(after),
        out_specs=[pl.BlockSpec(s[2], s[3]) for s in outs],
        out_shape=[jax.ShapeDtypeStruct(s[0], s[1]) for s in outs],
        compiler_params=_params(sem),
    )(*[a for a, _, _ in ins], *after)


def _matmul(a, b, mode, out_dtype, name, add=None, a_split=1, b_split=1, o_split=1, after=()):
    def dims2(x, split):
        return (x.shape[-2], x.shape[-1] * split)

    ar, ac = dims2(a, a_split)
    br, bc = dims2(b, b_split)
    if mode == 'nn':
        M, K, N = ar, ac, bc
        assert br == K
    elif mode == 'nt':
        M, K, N = ar, ac, br
        assert bc == K
    else:
        K, M, N = ar, ac, bc
        assert br == K
    tm = _pick(M // (a_split if mode == 'tn' else 1), 1024)
    tn = _pick(N // max(o_split, b_split if mode != 'nt' else 1), 1408)
    tk = _pick(K // max(a_split if mode != 'tn' else 1, b_split if mode == 'nt' else 1),
               3584 if mode == 'nt' else 2048)
    gm, gn, gk = M // tm, N // tn, K // tk

    def col_map(split, total_cols, tile):
        per = total_cols // split // tile

        def f(r, c):
            return (c // per, r, c % per) if split > 1 else (r, c)
        return f

    if mode == 'nn':
        a_idx = col_map(a_split, K, tk)
        b_idx = col_map(b_split, N, tn)
        a_spec = pl.BlockSpec(((None,) if a_split > 1 else ()) + (tm, tk), lambda m, n, k: a_idx(m, k))
        b_spec = pl.BlockSpec(((None,) if b_split > 1 else ()) + (tk, tn), lambda m, n, k: b_idx(k, n))
        dn = ((1,), (0,))
    elif mode == 'nt':
        a_idx = col_map(a_split, K, tk)
        b_idx = col_map(b_split, K, tk)
        a_spec = pl.BlockSpec(((None,) if a_split > 1 else ()) + (tm, tk), lambda m, n, k: a_idx(m, k))
        b_spec = pl.BlockSpec(((None,) if b_split > 1 else ()) + (tn, tk), lambda m, n, k: b_idx(n, k))
        dn = ((1,), (1,))
    else:
        a_idx = col_map(a_split, M, tm)
        b_idx = col_map(b_split, N, tn)
        a_spec = pl.BlockSpec(((None,) if a_split > 1 else ()) + (tk, tm), lambda m, n, k: a_idx(k, m))
        b_spec = pl.BlockSpec(((None,) if b_split > 1 else ()) + (tk, tn), lambda m, n, k: b_idx(k, n))
        dn = ((0,), (0,))
    o_idx = col_map(o_split, N, tn)
    o_block = ((None,) if o_split > 1 else ()) + (tm, tn)
    o_spec = pl.BlockSpec(o_block, lambda m, n, k: o_idx(m, n))
    o_shape = ((o_split, M, N // o_split) if o_split > 1 else (M, N))
    in_specs = [a_spec, b_spec]
    args = [a, b]
    if add is not None:
        in_specs.append(pl.BlockSpec((tm, tn), lambda m, n, k: (m, n)))
        args.append(add)
    n_lead = len(args)
    for t in after:
        in_specs.append(pl.BlockSpec(memory_space=pl.ANY))
        args.append(t)

    def body(*refs):
        a_ref, b_ref = refs[0], refs[1]
        add_ref = refs[2] if add is not None else None
        o_ref = refs[n_lead + len(after)]
        part = lax.dot_general(a_ref[...].astype(BF16), b_ref[...].astype(BF16), (dn, ((), ())),
                               preferred_element_type=F32)

        def finish(total):
            if add_ref is not None:
                total = total + add_ref[...]
            o_ref[...] = total.astype(o_ref.dtype)

        if gk == 1:
            finish(part)
        else:
            acc_ref = refs[-1]
            k = pl.program_id(2)

            @pl.when(k == 0)
            def _():
                acc_ref[...] = part

            @pl.when(jnp.logical_and(k > 0, k < gk - 1))
            def _():
                acc_ref[...] += part

            @pl.when(k == gk - 1)
            def _():
                finish(acc_ref[...] + part)

    return pl.pallas_call(
        body, name=name, grid=(gm, gn, gk), in_specs=in_specs, out_specs=o_spec,
        out_shape=jax.ShapeDtypeStruct(o_shape, out_dtype),
        scratch_shapes=([pltpu.VMEM((tm, tn), F32)] if gk > 1 else []),
        compiler_params=_params(("parallel", "parallel", "arbitrary")),
    )(*args)


def _row_tile():
    return _pick(SEQ, 256, 8)


def _rms_fwd(x, g, name):
    tm = _row_tile()
    (h,) = _tiled(lambda xv, gv: _rms(xv, gv), name, (SEQ // tm,),
                  [(x, (tm, D_MODEL), lambda i: (i, 0)), (g, (1, D_MODEL), lambda i: (0, 0))],
                  [((SEQ, D_MODEL), BF16, (tm, D_MODEL), lambda i: (i, 0), ())])
    return h


def _rms_bwd(x, g, dh, dres, name):
    tm = _row_tile()

    def fn(xv, gv, dhv, drv):
        _, vjp = jax.vjp(_rms, xv, gv)
        dx, dg = vjp(dhv)
        dx = dx + drv
        return dx, dx, dg

    row = lambda i: (i, 0)
    return _tiled(fn, name, (SEQ // tm,),
                  [(x, (tm, D_MODEL), row), (g, (1, D_MODEL), lambda i: (0, 0)),
                   (dh, (tm, D_MODEL), row), (dres, (tm, D_MODEL), row)],
                  [((SEQ, D_MODEL), F32, (tm, D_MODEL), row, ()),
                   ((SEQ, D_MODEL), BF16, (tm, D_MODEL), row, ()),
                   ((1, D_MODEL), F32, (1, D_MODEL), lambda i: (0, 0), (0,))])


def _loss_head(x, g, target):
    tm = _row_tile()

    def fn(xv, gv, tv):
        y, vjp = jax.vjp(_rms, xv, gv)
        err = y - tv
        part = 0.5 * jnp.sum(jnp.sum(err * err, axis=1, keepdims=True), axis=0, keepdims=True) / D_MODEL
        dx, dg = vjp(err / D_MODEL)
        return part, dx, dx, dg

    row = lambda i: (i, 0)
    one = lambda i: (0, 0)
    return _tiled(fn, "loss_head", (SEQ // tm,),
                  [(x, (tm, D_MODEL), row), (g, (1, D_MODEL), one), (target, (tm, D_MODEL), row)],
                  [((1, 1), F32, (1, 1), one, (0,)),
                   ((SEQ, D_MODEL), F32, (tm, D_MODEL), row, ()),
                   ((SEQ, D_MODEL), BF16, (tm, D_MODEL), row, ()),
                   ((1, D_MODEL), F32, (1, D_MODEL), one, (0,))])


def _p_col(width, index):
    return lambda tm: ((tm, width), lambda i: (i, index))


def _merge_fn(ga, gb, ta, tb):
    return _sigmoid(ga) * ta + _sigmoid(gb) * tb


def _merge_fwd(p, ta, tb):
    tm = _row_tile()
    d = D_MODEL
    ga_blk = 4 * _gw() + 2 * _sw()
    assert ga_blk % d == 0
    ia, ib = ga_blk // d, ga_blk // d + 1
    row = lambda i: (i, 0)
    (m,) = _tiled(_merge_fn, "merge_fwd", (SEQ // tm,),
                  [(p, (tm, d), lambda i: (i, ia)), (p, (tm, d), lambda i: (i, ib)),
                   (ta, (tm, d), row), (tb, (tm, d), row)],
                  [((SEQ, d), BF16, (tm, d), row, ())])
    return m


def _merge_bwd(p, ta, tb, dm):
    tm = _row_tile()
    d = D_MODEL
    ga_blk = 4 * _gw() + 2 * _sw()
    ia, ib = ga_blk // d, ga_blk // d + 1

    def fn(ga, gb, tav, tbv, dmv):
        _, vjp = jax.vjp(_merge_fn, ga, gb, tav, tbv)
        return vjp(dmv)

    row = lambda i: (i, 0)
    out = ((SEQ, d), BF16, (tm, d), row, ())
    return _tiled(fn, "merge_bwd", (SEQ // tm,),
                  [(p, (tm, d), lambda i: (i, ia)), (p, (tm, d), lambda i: (i, ib)),
                   (ta, (tm, d), row), (tb, (tm, d), row), (dm, (tm, d), row)],
                  [out, out, out, out])


def _gate_fn(ab, alog, dtb):
    lane = lax.broadcasted_iota(jnp.int32, ab.shape, 1)
    nd = 2 * GDN_HEADS
    g = -jnp.exp(alog) * _softplus(ab + dtb)
    beta = _sigmoid(ab)
    return jnp.where(lane < nd, g, jnp.where(lane < 2 * nd, beta, 0.0))


def _ab_index():
    off = 4 * _gw() + 2 * _sw() + 2 * D_MODEL
    assert off % AB_PAD == 0
    return off // AB_PAD


def _gate_fwd(p, alog, dtb):
    tm = _row_tile()
    iab = _ab_index()
    one = lambda i: (0, 0)
    (g,) = _tiled(_gate_fn, "gdn_gate_fwd", (SEQ // tm,),
                  [(p, (tm, AB_PAD), lambda i: (i, iab)), (alog, (1, AB_PAD), one), (dtb, (1, AB_PAD), one)],
                  [((SEQ, AB_PAD), F32, (tm, AB_PAD), lambda i: (i, 0), ())])
    return g


def _gate_bwd(p, alog, dtb, dg):
    tm = _row_tile()
    iab = _ab_index()
    one = lambda i: (0, 0)

    def fn(ab, al, db, dgv):
        _, vjp = jax.vjp(_gate_fn, ab, al, db)
        return vjp(dgv)

    return _tiled(fn, "gdn_gate_bwd", (SEQ // tm,),
                  [(p, (tm, AB_PAD), lambda i: (i, iab)), (alog, (1, AB_PAD), one), (dtb, (1, AB_PAD), one),
                   (dg, (tm, AB_PAD), lambda i: (i, 0))],
                  [((SEQ, AB_PAD), BF16, (tm, AB_PAD), lambda i: (i, 0), ()),
                   ((1, AB_PAD), F32, (1, AB_PAD), one, (0,)),
                   ((1, AB_PAD), F32, (1, AB_PAD), one, (0,))])


def _l2n(x):
    return x * lax.rsqrt(jnp.sum(x * x, axis=-1, keepdims=True) + NORM_EPS)


def _qkv_fn(xq, xk, xv, *taps):
    n = QKV_CONV
    q = _l2n(_silu(_conv_rows(xq, taps[0:n])))
    k = _l2n(_silu(_conv_rows(xk, taps[n:2 * n])))
    v = _silu(_conv_rows(xv, taps[2 * n:3 * n]))
    return q, k, v


def _qkv_specs(p, conv_w):
    hd, nh = HEAD_DIM, GDN_HEADS
    ins = [(p, (SEQ, hd), (lambda h, s=s: (0, s * nh + h))) for s in range(3)]
    ins += [(conv_w, (QKV_CONV, hd), (lambda h, s=s: (0, s * nh + h))) for s in range(3)]
    return ins


def _qkv_fwd(p, conv_w):
    hd, nh = HEAD_DIM, GDN_HEADS
    ins = _qkv_specs(p, conv_w)
    n_in = len(ins)
    out_spec = pl.BlockSpec((SEQ, hd), lambda h: (0, h))

    def body(*refs):
        xs = [r[...] for r in refs[:3]]
        taps = [refs[3 + s][j:j + 1, :] for s in range(3) for j in range(QKV_CONV)]
        q, k, v = _qkv_fn(*xs, *taps)
        refs[n_in][...] = q
        refs[n_in + 1][...] = k
        refs[n_in + 2][...] = v

    return pl.pallas_call(
        body, name="gdn_qkv_fwd", grid=(nh,),
        in_specs=[pl.BlockSpec(b, m) for _, b, m in ins], out_specs=[out_spec] * 3,
        out_shape=[jax.ShapeDtypeStruct((SEQ, nh * hd), F32)] * 3,
        compiler_params=_params(("parallel",)),
    )(*[a for a, _, _ in ins])


def _qkv_bwd(p, conv_w, dq, dk, dv):
    hd, nh = HEAD_DIM, GDN_HEADS
    ins = _qkv_specs(p, conv_w) + [(t, (SEQ, hd), lambda h: (0, h)) for t in (dq, dk, dv)]
    n_in = len(ins)

    def body(*refs):
        xs = [r[...] for r in refs[:3]]
        taps = [refs[3 + s][j:j + 1, :] for s in range(3) for j in range(QKV_CONV)]
        cts = tuple(r[...] for r in refs[6:9])
        _, vjp = jax.vjp(_qkv_fn, *xs, *taps)
        grads = vjp(cts)
        for s in range(3):
            refs[n_in + s][...] = grads[s].astype(BF16)
            for j in range(QKV_CONV):
                refs[n_in + 3 + s][j:j + 1, :] = jnp.sum(grads[3 + s * QKV_CONV + j], axis=0, keepdims=True)

    dx_spec = pl.BlockSpec((SEQ, hd), lambda h: (0, h))
    dw_spec = pl.BlockSpec((QKV_CONV, hd), lambda h: (0, h))
    outs = pl.pallas_call(
        body, name="gdn_qkv_bwd", grid=(nh,),
        in_specs=[pl.BlockSpec(b, m) for _, b, m in ins], out_specs=[dx_spec] * 3 + [dw_spec] * 3,
        out_shape=[jax.ShapeDtypeStruct((SEQ, nh * hd), BF16)] * 3
        + [jax.ShapeDtypeStruct((QKV_CONV, nh * hd), F32)] * 3,
        compiler_params=_params(("parallel",)),
    )(*[a for a, _, _ in ins])
    return outs[0:3], jnp.concatenate(outs[3:6], axis=1)


def _prep_fn(qs, ks, vs, gblk):
    nh = len(qs)
    c = qs[0].shape[0]
    scale = HEAD_DIM ** -0.5
    ii = lax.broadcasted_iota(jnp.int32, (c, c), 0)
    jj = lax.broadcasted_iota(jnp.int32, (c, c), 1)
    eye = (ii == jj).astype(F32)
    lane = lax.broadcasted_iota(jnp.int32, gblk.shape, 1)
    kk_t = [_mm_nt(k, k) for k in ks]
    qk_t = [_mm_nt(q, k) for q, k in zip(qs, ks)]
    chains = [(h, d) for h in range(nh) for d in range(2)]
    pre = []
    for h, d in chains:
        g = jnp.sum(jnp.where(lane == h + d * nh, gblk, 0.0), axis=1, keepdims=True)
        beta = jnp.sum(jnp.where(lane == h + (2 + d) * nh, gblk, 0.0), axis=1, keepdims=True)
        incl = (jj <= ii) if d == 0 else (jj >= ii)
        strict = (jj < ii) if d == 0 else (jj > ii)
        incl_t = (ii <= jj) if d == 0 else (ii >= jj)
        g_row = jnp.sum(eye * g, axis=0, keepdims=True)
        gc = jnp.sum(jnp.where(incl, g_row, 0.0), axis=1, keepdims=True)
        gc_row = jnp.sum(jnp.where(incl_t, g, 0.0), axis=0, keepdims=True)
        decay = jnp.where(incl, jnp.exp(jnp.where(incl, gc - gc_row, 0.0)), 0.0)
        a = jnp.where(strict, kk_t[h] * beta * decay, 0.0)
        pre.append((g, beta, gc, decay, incl, a))
    ts = _inv_unit([p[5] for p in pre])
    us = [_mmh(t, vs[h] * p[1]) for (h, d), p, t in zip(chains, pre, ts)]
    ws = [_mmh(t, ks[h] * (p[1] * jnp.exp(p[2]))) for (h, d), p, t in zip(chains, pre, ts)]
    outs = [[None, None] for _ in range(nh)]
    for (h, d), (g, beta, gc, decay, incl, a), u, w in zip(chains, pre, us, ws):
        qk = jnp.where(incl, qk_t[h] * (scale * decay), 0.0)
        g_last = jnp.sum(g, axis=0, keepdims=True)
        q_dec = qs[h] * (scale * jnp.exp(gc))
        k_dec = ks[h] * jnp.exp(g_last - gc)
        outs[h][d] = (u, w, q_dec, k_dec, qk, jnp.exp(g_last))
    return outs


def _n_chunks():
    return SEQ // GDN_CHUNK


def _gdn_prep_fwd(q, k, v, gates):
    c, hd, nh, nc = GDN_CHUNK, HEAD_DIM, GDN_HEADS, _n_chunks()
    gw = nh * hd
    row = lambda n: (n, 0)

    def body(q_ref, k_ref, v_ref, g_ref, u_ref, w_ref, qd_ref, kd_ref, qk_ref, dec_ref):
        heads = [slice(h * hd, (h + 1) * hd) for h in range(nh)]
        res = _prep_fn([q_ref[:, s] for s in heads], [k_ref[:, s] for s in heads], [v_ref[:, s] for s in heads],
                       g_ref[...])
        for h, cols in enumerate(heads):
            for d in range(2):
                u, w, qd, kd, qk, dec = res[h][d]
                u_ref[d, :, cols] = u
                w_ref[d, :, cols] = w
                qd_ref[d, :, cols] = qd
                kd_ref[d, :, cols] = kd
                qk_ref[d, h] = qk
                dec_ref[d, h:h + 1, :] = jnp.broadcast_to(dec, (1, LANES))

    wide = pl.BlockSpec((2, c, gw), lambda n: (0, n, 0))
    return pl.pallas_call(
        body, name="gdn_prep_fwd", grid=(nc,),
        in_specs=[pl.BlockSpec((c, gw), row)] * 3 + [pl.BlockSpec((c, LANES), row)],
        out_specs=[wide] * 4 + [pl.BlockSpec((2, nh, c, c), lambda n: (0, 0, n, 0)),
                                pl.BlockSpec((2, None, nh, LANES), lambda n: (0, n, 0, 0))],
        out_shape=[jax.ShapeDtypeStruct((2, SEQ, gw), F32)] * 4
        + [jax.ShapeDtypeStruct((2, nh, SEQ, c), F32), jax.ShapeDtypeStruct((2, nc, nh, LANES), F32)],
        compiler_params=_params(("parallel",)),
    )(q, k, v, gates)


def _gdn_prep_bwd(q, k, v, gates, cts):
    c, hd, nh, nc = GDN_CHUNK, HEAD_DIM, GDN_HEADS, _n_chunks()
    gw = nh * hd
    row = lambda n: (n, 0)

    def body(*refs):
        q_ref, k_ref, v_ref, g_ref = refs[:4]
        ct_refs = refs[4:16]
        dq_ref, dk_ref, dv_ref, dg_ref = refs[16:20]
        heads = [slice(h * hd, (h + 1) * hd) for h in range(nh)]
        _, vjp = jax.vjp(_prep_fn, [q_ref[:, s] for s in heads], [k_ref[:, s] for s in heads],
                         [v_ref[:, s] for s in heads], g_ref[...])
        cts_in = []
        for h, cols in enumerate(heads):
            per_dir = []
            for d in range(2):
                r = ct_refs[6 * d:6 * d + 6]
                per_dir.append((r[0][:, cols], r[1][:, cols], r[2][:, cols], r[3][:, cols], r[4][h],
                                r[5][h:h + 1, 0:1]))
            cts_in.append(per_dir)
        dqs, dks, dvs, dg = vjp(cts_in)
        for h, cols in enumerate(heads):
            dq_ref[:, cols] = dqs[h]
            dk_ref[:, cols] = dks[h]
            dv_ref[:, cols] = dvs[h]
        dg_ref[...] = dg

    one_dir = [pl.BlockSpec((c, gw), row)] * 4 + [pl.BlockSpec((nh, c, c), lambda n: (0, n, 0)),
                                                  pl.BlockSpec((None, nh, LANES), lambda n: (n, 0, 0))]
    return pl.pallas_call(
        body, name="gdn_prep_bwd", grid=(nc,),
        in_specs=[pl.BlockSpec((c, gw), row)] * 3 + [pl.BlockSpec((c, LANES), row)] + one_dir * 2,
        out_specs=[pl.BlockSpec((c, gw), row)] * 3 + [pl.BlockSpec((c, LANES), row)],
        out_shape=[jax.ShapeDtypeStruct((SEQ, gw), F32)] * 3 + [jax.ShapeDtypeStruct((SEQ, LANES), F32)],
        compiler_params=_params(("parallel",)),
    )(q, k, v, gates, *cts)


def _scan_steps(states, us, ws, qds, kds, qks, decs):
    w_s = [_mm(w, s) for w, s in zip(ws, states)]
    q_s = [_mm(qd, s) for qd, s in zip(qds, states)]
    v_new = [u - x for u, x in zip(us, w_s)]
    outs = [x + _mm(qk, vn) for x, qk, vn in zip(q_s, qks, v_new)]
    k_v = [_mm_tn(kd, vn) for kd, vn in zip(kds, v_new)]
    new_states = [s * dec + x for s, dec, x in zip(states, decs, k_v)]
    return new_states, outs


def _scan_operands(ins, chains, dec_lanes):
    cols = lambda h: slice(h * HEAD_DIM, (h + 1) * HEAD_DIM)
    wide = [[ins[6 * d + j][:, cols(h)] for d, h in chains] for j in range(4)]
    qks = [ins[6 * d + 4][h] for d, h in chains]
    decs = [ins[6 * d + 5][h:h + 1, dec_lanes] for d, h in chains]
    return (*wide, qks, decs)


def _scan_in_specs(chunk_of):
    c, hd, nh = GDN_CHUNK, HEAD_DIM, GDN_HEADS
    gw = nh * hd
    specs = []
    for d in range(2):
        f = chunk_of[d]
        specs += [pl.BlockSpec((None, c, gw), lambda n, d=d, f=f: (d, f(n), 0))] * 4
        specs += [pl.BlockSpec((None, nh, c, c), lambda n, d=d, f=f: (d, 0, f(n), 0)),
                  pl.BlockSpec((None, None, nh, LANES), lambda n, d=d, f=f: (d, f(n), 0, 0))]
    return specs


def _gdn_scan_fwd(u, w, qd, kd, qk, dec):
    c, hd, nh, nc = GDN_CHUNK, HEAD_DIM, GDN_HEADS, _n_chunks()
    gw = nh * hd
    chunk_of = (lambda n: n, lambda n: nc - 1 - n)

    def body(*refs):
        ins = refs[:12]
        o_refs = refs[12:14]
        st_refs = refs[14:16]
        s_ref = refs[16]

        @pl.when(pl.program_id(0) == 0)
        def _():
            s_ref[...] = jnp.zeros_like(s_ref)

        chains = [(d, h) for d in range(2) for h in range(nh)]
        states = [s_ref[d * nh + h] for d, h in chains]
        for (d, h), st in zip(chains, states):
            st_refs[d][h] = st
        new_states, outs = _scan_steps(states, *_scan_operands(ins, chains, slice(None)))
        for (d, h), st, o in zip(chains, new_states, outs):
            s_ref[d * nh + h] = st
            o_refs[d][:, h * hd:(h + 1) * hd] = o

    o_specs = [pl.BlockSpec((c, gw), lambda n, f=f: (f(n), 0)) for f in chunk_of]
    st_specs = [pl.BlockSpec((None, nh, hd, hd), lambda n, f=f: (f(n), 0, 0, 0)) for f in chunk_of]
    return pl.pallas_call(
        body, name="gdn_scan_fwd", grid=(nc,), in_specs=_scan_in_specs(chunk_of),
        out_specs=o_specs + st_specs,
        out_shape=[jax.ShapeDtypeStruct((SEQ, gw), F32)] * 2 + [jax.ShapeDtypeStruct((nc, nh, hd, hd), F32)] * 2,
        scratch_shapes=[pltpu.VMEM((2 * nh, hd, hd), F32)],
        compiler_params=_params(("arbitrary",)),
    )(*([u, w, qd, kd, qk, dec] * 2))


def _gdn_scan_bwd(u, w, qd, kd, qk, dec, st0, st1, do):
    c, hd, nh, nc = GDN_CHUNK, HEAD_DIM, GDN_HEADS, _n_chunks()
    gw = nh * hd
    chunk_of = (lambda n: nc - 1 - n, lambda n: n)

    def body(*refs):
        ins = refs[:12]
        st_in = refs[12:14]
        do_in = refs[14:16]
        outs = refs[16:28]
        ds_ref = refs[28]

        @pl.when(pl.program_id(0) == 0)
        def _():
            ds_ref[...] = jnp.zeros_like(ds_ref)

        chains = [(d, h) for d in range(2) for h in range(nh)]
        states = [st_in[d][h] for d, h in chains]
        _, vjp = jax.vjp(_scan_steps, states, *_scan_operands(ins, chains, slice(0, 1)))
        ct_state = [ds_ref[d * nh + h] for d, h in chains]
        ct_out = [do_in[d][:, h * hd:(h + 1) * hd] for d, h in chains]
        grads = vjp((ct_state, ct_out))
        for i, (d, h) in enumerate(chains):
            cols = slice(h * hd, (h + 1) * hd)
            du_r, dw_r, dqd_r, dkd_r, dqk_r, ddec_r = outs[6 * d:6 * d + 6]
            ds_ref[d * nh + h] = grads[0][i]
            du_r[:, cols] = grads[1][i]
            dw_r[:, cols] = grads[2][i]
            dqd_r[:, cols] = grads[3][i]
            dkd_r[:, cols] = grads[4][i]
            dqk_r[h] = grads[5][i]
            ddec_r[h:h + 1, :] = jnp.broadcast_to(grads[6][i], (1, LANES))

    st_specs = [pl.BlockSpec((None, nh, hd, hd), lambda n, f=f: (f(n), 0, 0, 0)) for f in chunk_of]
    do_specs = [pl.BlockSpec((c, gw), lambda n, f=f: (f(n), 0)) for f in chunk_of]
    out_specs, out_shape = [], []
    for f in chunk_of:
        out_specs += [pl.BlockSpec((c, gw), lambda n, f=f: (f(n), 0))] * 4
        out_specs += [pl.BlockSpec((nh, c, c), lambda n, f=f: (0, f(n), 0)),
                      pl.BlockSpec((None, nh, LANES), lambda n, f=f: (f(n), 0, 0))]
        out_shape += [jax.ShapeDtypeStruct((SEQ, gw), F32)] * 4
        out_shape += [jax.ShapeDtypeStruct((nh, SEQ, c), F32), jax.ShapeDtypeStruct((nc, nh, LANES), F32)]
    return pl.pallas_call(
        body, name="gdn_scan_bwd", grid=(nc,), in_specs=_scan_in_specs(chunk_of) + st_specs + do_specs,
        out_specs=out_specs, out_shape=out_shape,
        scratch_shapes=[pltpu.VMEM((2 * nh, hd, hd), F32)],
        compiler_params=_params(("arbitrary",)),
    )(*([u, w, qd, kd, qk, dec] * 2), st0, st1, do, do)


def _post_fn(o0, o1, z, gn):
    return _rms(o0 + o1, gn) * _silu(z)


def _gdn_post_fwd(o0, o1, p, gn):
    tm, hd, nh = _row_tile(), HEAD_DIM, GDN_HEADS
    zoff = 3 * nh
    blk = lambda i, h: (i, h)
    (ya,) = _tiled(_post_fn, "gdn_post_fwd", (SEQ // tm, nh),
                   [(o0, (tm, hd), blk), (o1, (tm, hd), blk),
                    (p, (tm, hd), lambda i, h: (i, zoff + h)), (gn, (1, hd), lambda i, h: (0, 0))],
                   [((SEQ, nh * hd), BF16, (tm, hd), blk, ())])
    return ya


def _gdn_post_bwd(o0, o1, p, gn, dya):
    tm, hd, nh = _row_tile(), HEAD_DIM, GDN_HEADS
    zoff = 3 * nh
    blk = lambda i, h: (i, h)

    def fn(a, b, z, g, dy):
        _, vjp = jax.vjp(_post_fn, a, b, z, g)
        do, _, dz, dg = vjp(dy.astype(F32))
        return do, dz, dg

    return _tiled(fn, "gdn_post_bwd", (SEQ // tm, nh),
                  [(o0, (tm, hd), blk), (o1, (tm, hd), blk),
                   (p, (tm, hd), lambda i, h: (i, zoff + h)), (gn, (1, hd), lambda i, h: (0, 0)),
                   (dya, (tm, hd), blk)],
                  [((SEQ, nh * hd), F32, (tm, hd), blk, ()),
                   ((SEQ, nh * hd), BF16, (tm, hd), blk, ()),
                   ((1, hd), F32, (1, hd), lambda i, h: (0, 0), (0, 1))],
                  sem=("arbitrary", "arbitrary"))


def _sgu_ln(v, lg, lb):
    gv = _gelu(v)
    mu = jnp.mean(gv, axis=-1, keepdims=True)
    cen = gv - mu
    var = jnp.mean(cen * cen, axis=-1, keepdims=True)
    return cen * lax.rsqrt(var + NORM_EPS) * lg + lb


def _sgu_mix(vn, w_ref, bt_ref):
    parts = []
    for g in range(SGU_GROUPS):
        cols = slice(g * SGU_DIM, (g + 1) * SGU_DIM)
        parts.append(_mm(w_ref[g], vn[:, cols]) + bt_ref[:, g:g + 1])
    return jnp.concatenate(parts, axis=1)


def _sgu_specs(p):
    sw, sb = _sw(), SGU_BLOCK
    uoff = 4 * _gw()
    assert uoff % sw == 0
    iu = uoff // sw
    return [(p, (sb, sw), lambda n: (n, iu)), (p, (sb, sw), lambda n: (n, iu + 1))]


def _sgu_fwd(p, lg, lb, w_s, b_t):
    sw, sb = _sw(), SGU_BLOCK
    ins = _sgu_specs(p)

    def body(u_ref, v_ref, lg_ref, lb_ref, w_ref, bt_ref, y_ref):
        vn = _sgu_ln(v_ref[...], lg_ref[...], lb_ref[...])
        y_ref[...] = (_gelu(u_ref[...]) * _sgu_mix(vn, w_ref, bt_ref)).astype(BF16)

    one2 = lambda n: (0, 0)
    return pl.pallas_call(
        body, name="sgu_fwd", grid=(SEQ // sb,),
        in_specs=[pl.BlockSpec(b, m) for _, b, m in ins]
        + [pl.BlockSpec((1, sw), one2), pl.BlockSpec((1, sw), one2),
           pl.BlockSpec((SGU_GROUPS, sb, sb), lambda n: (0, 0, 0)), pl.BlockSpec((sb, LANES), one2)],
        out_specs=pl.BlockSpec((sb, sw), lambda n: (n, 0)),
        out_shape=jax.ShapeDtypeStruct((SEQ, sw), BF16),
        compiler_params=_params(("parallel",)),
    )(p, p, lg, lb, w_s, b_t)


def _sgu_bwd(p, lg, lb, w_s, b_t, dy):
    sw, sb, ng = _sw(), SGU_BLOCK, SGU_GROUPS
    ins = _sgu_specs(p)

    def body(u_ref, v_ref, lg_ref, lb_ref, w_ref, bt_ref, dy_ref, du_ref, dv_ref, dlg_ref, dlb_ref, dw_ref, dbt_ref):
        first = pl.program_id(0) == 0
        u, v = u_ref[...], v_ref[...]
        gu, gelu_vjp = jax.vjp(_gelu, u)
        vn, ln_vjp = jax.vjp(_sgu_ln, v, lg_ref[...], lb_ref[...])
        s = _sgu_mix(vn, w_ref, bt_ref)
        dyv = dy_ref[...].astype(F32)
        ds = dyv * gu
        (du,) = gelu_vjp(dyv * s)
        lane = lax.broadcasted_iota(jnp.int32, (sb, LANES), 1)
        dvn_parts = []
        dbt = jnp.zeros((sb, LANES), F32)
        for g in range(ng):
            cols = slice(g * SGU_DIM, (g + 1) * SGU_DIM)
            ds_g = ds[:, cols]
            dw_g = _mm_nt(ds_g, vn[:, cols])
            dvn_parts.append(_mm_tn(w_ref[g], ds_g))
            dbt = dbt + jnp.where(lane == g, jnp.sum(ds_g, axis=1, keepdims=True), 0.0)

            @pl.when(first)
            def _(g=g, dw_g=dw_g):
                dw_ref[g] = dw_g

            @pl.when(jnp.logical_not(first))
            def _(g=g, dw_g=dw_g):
                dw_ref[g] += dw_g

        dv, dlg, dlb = ln_vjp(jnp.concatenate(dvn_parts, axis=1))
        du_ref[...] = du.astype(BF16)
        dv_ref[...] = dv.astype(BF16)

        @pl.when(first)
        def _():
            dlg_ref[...] = dlg
            dlb_ref[...] = dlb
            dbt_ref[...] = dbt

        @pl.when(jnp.logical_not(first))
        def _():
            dlg_ref[...] += dlg
            dlb_ref[...] += dlb
            dbt_ref[...] += dbt

    one2 = lambda n: (0, 0)
    row = pl.BlockSpec((sb, sw), lambda n: (n, 0))
    return pl.pallas_call(
        body, name="sgu_bwd", grid=(SEQ // sb,),
        in_specs=[pl.BlockSpec(b, m) for _, b, m in ins]
        + [pl.BlockSpec((1, sw), one2), pl.BlockSpec((1, sw), one2),
           pl.BlockSpec((ng, sb, sb), lambda n: (0, 0, 0)), pl.BlockSpec((sb, LANES), one2), row],
        out_specs=[row, row, pl.BlockSpec((1, sw), one2), pl.BlockSpec((1, sw), one2),
                   pl.BlockSpec((ng, sb, sb), lambda n: (0, 0, 0)), pl.BlockSpec((sb, LANES), one2)],
        out_shape=[jax.ShapeDtypeStruct((SEQ, sw), BF16)] * 2 + [jax.ShapeDtypeStruct((1, sw), F32)] * 2
        + [jax.ShapeDtypeStruct((ng, sb, sb), F32), jax.ShapeDtypeStruct((sb, LANES), F32)],
        compiler_params=_params(("arbitrary",)),
    )(p, p, lg, lb, w_s, b_t, dy)


def _ffn_fn(xg, xv, bg, bv, *taps):
    n = FFN_CONV
    cg = _conv_rows(xg, taps[0:n]) + bg
    cv = _conv_rows(xv, taps[n:2 * n]) + bv
    return _silu(cg) * cv


def _ffn_tile():
    return _pick(D_FF, 256)


def _ffn_specs(up, conv_w, conv_b):
    tc = _ffn_tile()
    nt = D_FF // tc
    ins = [(up, (None, SEQ, tc), (lambda j, s=s: (s, 0, j))) for s in range(2)]
    ins += [(conv_b, (1, tc), (lambda j, s=s: (0, s * nt + j))) for s in range(2)]
    ins += [(conv_w, (FFN_CONV, tc), (lambda j, s=s: (0, s * nt + j))) for s in range(2)]
    return ins


def _ffn_act_fwd(up, conv_w, conv_b):
    tc = _ffn_tile()
    ins = _ffn_specs(up, conv_w, conv_b)

    def body(xg, xv, bg, bv, wg, wv, o_ref):
        taps = [w[j:j + 1, :] for w in (wg, wv) for j in range(FFN_CONV)]
        o_ref[...] = _ffn_fn(xg[...], xv[...], bg[...], bv[...], *taps).astype(BF16)

    return pl.pallas_call(
        body, name="ffn_act_fwd", grid=(D_FF // tc,),
        in_specs=[pl.BlockSpec(b, m) for _, b, m in ins],
        out_specs=pl.BlockSpec((SEQ, tc), lambda j: (0, j)),
        out_shape=jax.ShapeDtypeStruct((SEQ, D_FF), BF16),
        compiler_params=_params(("parallel",)),
    )(*[a for a, _, _ in ins])


def _ffn_act_bwd(up, conv_w, conv_b, dact):
    tc = _ffn_tile()
    nt = D_FF // tc
    ins = _ffn_specs(up, conv_w, conv_b) + [(dact, (SEQ, tc), lambda j: (0, j))]

    def body(xg, xv, bg, bv, wg, wv, dact_ref, dup_ref, dwg_ref, dwv_ref, dbg_ref, dbv_ref):
        taps = [w[j:j + 1, :] for w in (wg, wv) for j in range(FFN_CONV)]
        _, vjp = jax.vjp(_ffn_fn, xg[...], xv[...], bg[...], bv[...], *taps)
        grads = vjp(dact_ref[...].astype(F32))
        dup_ref[0] = grads[0].astype(BF16)
        dup_ref[1] = grads[1].astype(BF16)
        dbg_ref[...] = jnp.sum(grads[2], axis=0, keepdims=True)
        dbv_ref[...] = jnp.sum(grads[3], axis=0, keepdims=True)
        for j in range(FFN_CONV):
            dwg_ref[j:j + 1, :] = jnp.sum(grads[4 + j], axis=0, keepdims=True)
            dwv_ref[j:j + 1, :] = jnp.sum(grads[4 + FFN_CONV + j], axis=0, keepdims=True)

    col = lambda j: (0, j)
    outs = pl.pallas_call(
        body, name="ffn_act_bwd", grid=(nt,),
        in_specs=[pl.BlockSpec(b, m) for _, b, m in ins],
        out_specs=[pl.BlockSpec((2, SEQ, tc), lambda j: (0, 0, j)),
                   pl.BlockSpec((FFN_CONV, tc), col), pl.BlockSpec((FFN_CONV, tc), col),
                   pl.BlockSpec((1, tc), col), pl.BlockSpec((1, tc), col)],
        out_shape=[jax.ShapeDtypeStruct((2, SEQ, D_FF), BF16),
                   jax.ShapeDtypeStruct((FFN_CONV, D_FF), F32), jax.ShapeDtypeStruct((FFN_CONV, D_FF), F32),
                   jax.ShapeDtypeStruct((1, D_FF), F32), jax.ShapeDtypeStruct((1, D_FF), F32)],
        compiler_params=_params(("parallel",)),
    )(*[a for a, _, _ in ins])
    dup, dwg, dwv, dbg, dbv = outs
    return dup, jnp.concatenate([dwg, dwv], axis=1), jnp.concatenate([dbg, dbv], axis=1)


def _pad_lanes(v):
    return jnp.pad(v.reshape(1, -1), ((0, 0), (0, LANES - v.size)))


def _layer_fwd(x, lw, late=None):
    h = _rms_fwd(x, lw['norm_mix_g'], "rms_mix_fwd")
    p = _matmul(h, lw['w_all'], 'nn', F32, "mm_in")
    if late is not None:
        lw = dict(lw, **late(p))
    gates = _gate_fwd(p, lw['a_log'], lw['dt_bias'])
    q, k, v = _qkv_fwd(p, lw['qkv_conv_w'])
    u, w, qd, kd, qk, dec = _gdn_prep_fwd(q, k, v, gates)
    o0, o1, st0, st1 = _gdn_scan_fwd(u, w, qd, kd, qk, dec)
    ya = _gdn_post_fwd(o0, o1, p, lw['gdn_norm_g'])
    yb = _sgu_fwd(p, lw['sgu_ln_g'], lw['sgu_ln_b'], lw['sgu_w'], lw['sgu_bt'])
    ta = _matmul(ya, lw['w_branch_a'], 'nn', F32, "mm_branch_a", b_split=N_CHIPS)
    tb = _matmul(yb, lw['w_branch_b'], 'nn', F32, "mm_branch_b", b_split=N_CHIPS)
    m = _merge_fwd(p, ta, tb)
    x1 = _matmul(m, lw['w_out'], 'nn', F32, "mm_out", add=x)
    h2 = _rms_fwd(x1, lw['norm_ffn_g'], "rms_ffn_fwd")
    up = _matmul(h2, lw['w_up'], 'nn', F32, "mm_up", b_split=N_CHIPS, o_split=2)
    act = _ffn_act_fwd(up, lw['ffn_conv_w'], lw['ffn_conv_b'])
    x2 = _matmul(act, lw['w_down'], 'nn', F32, "mm_down", add=x1)
    saved = dict(x=x, h=h, p=p, gates=gates, q=q, k=k, v=v, u=u, w=w, qd=qd, kd=kd, qk=qk, dec=dec,
                 o0=o0, o1=o1, st0=st0, st1=st1, ya=ya, yb=yb, ta=ta, tb=tb, m=m, x1=x1, h2=h2, up=up, act=act)
    return x2, saved, lw


def _layer_bwd(dx, dx_bf, lw, s, after=(), mid=None, last=None):
    g = {}
    dact = _matmul(dx_bf, lw['w_down'], 'nt', BF16, "mm_down_dgrad", after=after)
    g['w_down'] = _matmul(s['act'], dx_bf, 'tn', PARTIAL, "mm_down_wgrad")
    dup, g['ffn_conv_w'], g['ffn_conv_b'] = _ffn_act_bwd(s['up'], lw['ffn_conv_w'], lw['ffn_conv_b'], dact)
    dh2 = _matmul(dup, lw['w_up'], 'nt', F32, "mm_up_dgrad", a_split=2, b_split=N_CHIPS)
    g['w_up'] = _matmul(s['h2'], dup, 'tn', PARTIAL, "mm_up_wgrad", b_split=2, o_split=N_CHIPS)
    dx1, dx1_bf, g['norm_ffn_g'] = _rms_bwd(s['x1'], lw['norm_ffn_g'], dh2, dx, "rms_ffn_bwd")
    after_mid = mid(g, dh2) if mid is not None else ()
    dm = _matmul(dx1_bf, lw['w_out'], 'nt', F32, "mm_out_dgrad", after=after_mid)
    g['w_out'] = _matmul(s['m'], dx1_bf, 'tn', PARTIAL, "mm_out_wgrad")
    d_ga, d_gb, d_ta, d_tb = _merge_bwd(s['p'], s['ta'], s['tb'], dm)
    dya = _matmul(d_ta, lw['w_branch_a'], 'nt', BF16, "mm_branch_a_dgrad", b_split=N_CHIPS)
    dyb = _matmul(d_tb, lw['w_branch_b'], 'nt', BF16, "mm_branch_b_dgrad", b_split=N_CHIPS)
    g['w_branch_a'] = _matmul(s['ya'], d_ta, 'tn', PARTIAL, "mm_branch_a_wgrad", o_split=N_CHIPS)
    g['w_branch_b'] = _matmul(s['yb'], d_tb, 'tn', PARTIAL, "mm_branch_b_wgrad", o_split=N_CHIPS)
    du_s, dv_s, g['sgu_ln_g'], g['sgu_ln_b'], g['sgu_w'], g['sgu_bt'] = _sgu_bwd(
        s['p'], lw['sgu_ln_g'], lw['sgu_ln_b'], lw['sgu_w'], lw['sgu_bt'], dyb)
    do, dz, g['gdn_norm_g'] = _gdn_post_bwd(s['o0'], s['o1'], s['p'], lw['gdn_norm_g'], dya)
    cts = _gdn_scan_bwd(s['u'], s['w'], s['qd'], s['kd'], s['qk'], s['dec'], s['st0'], s['st1'], do)
    dq, dk, dv, dgates = _gdn_prep_bwd(s['q'], s['k'], s['v'], s['gates'], cts)
    (dxq, dxk, dxv), g['qkv_conv_w'] = _qkv_bwd(s['p'], lw['qkv_conv_w'], dq, dk, dv)
    d_ab, g['a_log'], g['dt_bias'] = _gate_bwd(s['p'], lw['a_log'], lw['dt_bias'], dgates)
    dp = jnp.concatenate([dxq, dxk, dxv, dz, du_s, dv_s, d_ga, d_gb, d_ab], axis=1)
    g['w_all'] = _matmul(s['h'], dp, 'tn', PARTIAL, "mm_in_wgrad")
    after_wgrads = last(g, dp) if last is not None else ()
    dh = _matmul(dp, lw['w_all'], 'nt', F32, "mm_in_dgrad", after=after_wgrads)
    dx0, dx0_bf, g['norm_mix_g'] = _rms_bwd(s['x'], lw['norm_mix_g'], dh, dx1, "rms_mix_bwd")
    return dx0, dx0_bf, g


def _w_in_segments():
    n_ab = 4 * GDN_HEADS
    cut = 4 * _gw()
    n_main = cut + 2 * _sw() + 2 * D_MODEL
    return [(0, cut, 0), (cut, cut + n_ab, n_main), (cut + n_ab, n_main + n_ab, cut)]


def _w_all_from_chips(t):
    width = t.shape[-1]
    runs = sorted(_w_in_segments(), key=lambda s: s[2])
    pieces = []
    for lo, hi, _ in runs:
        for j in range(N_CHIPS):
            a, b = max(lo, j * width), min(hi, (j + 1) * width)
            if a < b:
                pieces.append(t[j][:, a - j * width:b - j * width])
    pieces.append(jnp.zeros((t.shape[1], AB_PAD - 4 * GDN_HEADS), t.dtype))
    return jnp.concatenate(pieces, axis=1)


def _w_in_grad_chips(g_all):
    segs = _w_in_segments()
    width = segs[-1][1] // N_CHIPS
    chips = []
    for j in range(N_CHIPS):
        pieces = []
        for lo, hi, start in segs:
            a, b = max(lo, j * width), min(hi, (j + 1) * width)
            if a < b:
                pieces.append(g_all[:, start + a - lo:start + b - lo])
        chips.append(jnp.concatenate(pieces, axis=1))
    return jnp.stack(chips)


_FIRST = ('w_in', 'qkv_conv_w')
_REST = ('w_branch_a', 'w_branch_b', 'w_out', 'w_up', 'w_down', 'ffn_conv_w')


def _cat_cols(t):
    return jnp.concatenate([t[j] for j in range(N_CHIPS)], axis=-1)


def _layer_weights_rest(big):
    return dict(
        w_branch_a=big['w_branch_a'], w_branch_b=big['w_branch_b'], w_up=big['w_up'],
        w_out=big['w_out'].reshape(D_MODEL, D_MODEL), w_down=big['w_down'].reshape(D_FF, D_MODEL),
        ffn_conv_w=_cat_cols(big['ffn_conv_w']))


def _layer_weights_first(l, big, small):
    return dict(
        w_all=_w_all_from_chips(big['w_in']), qkv_conv_w=_cat_cols(big['qkv_conv_w']),
        norm_mix_g=small['norm_mix_g'][l:l + 1], norm_ffn_g=small['norm_ffn_g'][l:l + 1],
        a_log=_pad_lanes(small['a_log'][l]), dt_bias=_pad_lanes(small['dt_bias'][l]),
        gdn_norm_g=small['gdn_norm_g'][l:l + 1],
        sgu_ln_g=small['sgu_ln_g'][l:l + 1], sgu_ln_b=small['sgu_ln_b'][l:l + 1], sgu_w=small['sgu_w'][l],
        sgu_bt=jnp.pad(small['sgu_b'][l].T, ((0, 0), (0, LANES - SGU_GROUPS))),
        ffn_conv_b=small['ffn_conv_b'][l:l + 1])


_SMALL_GRADS = ('norm_mix_g', 'a_log', 'dt_bias', 'gdn_norm_g', 'sgu_ln_g', 'sgu_ln_b', 'sgu_w', 'sgu_bt',
                'norm_ffn_g', 'ffn_conv_b', 'qkv_conv_w', 'ffn_conv_w')
_BIG = ('w_in', 'w_branch_a', 'w_branch_b', 'w_out', 'w_up', 'w_down')


def _big_grad_slices(g, names=_BIG):
    def one(k):
        if k == 'w_in':
            return _w_in_grad_chips(g['w_all'])
        if k == 'w_out':
            return g[k].reshape(N_CHIPS, D_MODEL // N_CHIPS, D_MODEL)
        if k == 'w_down':
            return g[k].reshape(N_CHIPS, D_FF // N_CHIPS, D_MODEL)
        return g[k]
    return {k: one(k) for k in names}


def _adamw(w, g, m, v, name, after=(), pass_g=False):
    shape = w.shape
    cols = shape[-1]
    rows = w.size // cols
    tr = _pick(rows, max(8, (1 << 18) // cols // 8 * 8), 8) if rows % 8 == 0 else rows

    def fn(wv, gv, mv, vv):
        m2 = ADAM_B1 * mv + (1.0 - ADAM_B1) * gv
        v2 = ADAM_B2 * vv + (1.0 - ADAM_B2) * (gv * gv)
        m_hat = m2 / (1.0 - ADAM_B1 ** ADAM_STEP)
        v_hat = v2 / (1.0 - ADAM_B2 ** ADAM_STEP)
        delta = -ADAM_LR * (m_hat / (jnp.sqrt(v_hat) + ADAM_EPS) + ADAM_WD * wv)
        return (delta, m2, v2, gv) if pass_g else (delta, m2, v2)

    row = lambda i: (i, 0)
    outs = _tiled(fn, name, (rows // tr,),
                  [(t.reshape(rows, cols), (tr, cols), row) for t in (w, g, m, v)],
                  [((rows, cols), F32, (tr, cols), row, ())] * (4 if pass_g else 3), sem=("parallel",),
                  after=after)
    return [o.reshape(shape) for o in outs]


MESH_IDS = pl.DeviceIdType.MESH
ANY = pl.BlockSpec(memory_space=pl.ANY)


def _place():
    x, y, c = lax.axis_index("x"), lax.axis_index("y"), lax.axis_index("c")
    chips = [(1 - x, y), (x, 1 - y), (1 - x, 1 - y)]
    return x, y, c, 2 * x + y, chips


def _chip_index():
    return 2 * lax.axis_index("x") + lax.axis_index("y")


HBM = pl.BlockSpec(memory_space=pltpu.HBM)
SEM = pl.BlockSpec(memory_space=pltpu.SEMAPHORE)
DATAFLOW = pltpu.SideEffectType.DATAFLOW_SIDE_EFFECTING
TOKEN = jax.ShapeDtypeStruct((8, LANES), F32)


def _in_hbm(t):
    return pltpu.with_memory_space_constraint(t, pltpu.HBM)


def _place_shard(w, l, dtype, name):
    _, r, cols = w.shape
    tr = _pick(r, max(16, (1 << 18) // cols // 16 * 16), 16) if r % 16 == 0 else r

    def body(w_ref, o_ref):
        o_ref[...] = w_ref[...].astype(dtype)

    return pl.pallas_call(
        body, name=name, grid=(r // tr,),
        in_specs=[pl.BlockSpec((None, tr, cols), lambda i: (l, i, 0))],
        out_specs=pl.BlockSpec((None, tr, cols), lambda i: (_chip_index(), i, 0)),
        out_shape=jax.ShapeDtypeStruct((N_CHIPS, r, cols), dtype),
        compiler_params=_params(("parallel",)),
    )(w)


def _my_rows(ref_or_shape_rows, c):
    r = ref_or_shape_rows
    if r % 32 == 0:
        return pl.ds(c * (r // 2), r // 2), pl.ds((1 - c) * (r // 2), r // 2), True
    return pl.ds(0, r), pl.ds(0, r), False


def _gather_copies(bufs, send_sems, recv_sems):
    x, y, c, me, chips = _place()
    out = []
    for i, buf in enumerate(bufs):
        mine, _, _ = _my_rows(buf.shape[1], c)
        for j, (cx, cy) in enumerate(chips):
            def rcopy(slab, i=i, j=j, cx=cx, cy=cy):
                return pltpu.make_async_remote_copy(src_ref=slab, dst_ref=slab, send_sem=send_sems.at[3 * i + j],
                                                    recv_sem=recv_sems.at[3 * i + j], device_id=(cx, cy, c),
                                                    device_id_type=MESH_IDS)
            out.append((rcopy(buf.at[me, mine]), rcopy(buf.at[2 * cx + cy, mine])))
    return out


def _gather_start(xs, name, after=()):
    n = len(xs)
    n_after = len(after)

    def body(*refs):
        refs = refs[n + n_after:]
        send_sems, recv_sems = refs[0], refs[1]
        bufs = refs[2:n + 2]
        token = refs[n + 2]
        for out_going, _ in _gather_copies(bufs, send_sems, recv_sems):
            out_going.start()
        token[...] = jnp.zeros_like(token)

    res = pl.pallas_call(
        body, name=name, in_specs=[HBM] * n + [ANY] * n_after,
        out_specs=[SEM, SEM] + [HBM] * n + [pl.BlockSpec(memory_space=pltpu.VMEM)],
        out_shape=[pltpu.SemaphoreType.DMA((3 * n,)), pltpu.SemaphoreType.DMA((3 * n,))]
        + [pltpu.HBM(t.shape, t.dtype) for t in xs] + [TOKEN],
        input_output_aliases={i: i + 2 for i in range(n)},
        compiler_params=pltpu.CompilerParams(has_side_effects=DATAFLOW),
    )(*[_in_hbm(t) for t in xs], *after)
    return res[0], res[1], res[2:2 + n], res[2 + n]


def _gather_wait(bufs, send_sems, recv_sems, after, name):
    n = len(bufs)

    def body(*refs):
        b_refs = refs[:n]
        s_sems, r_sems = refs[n], refs[n + 1]
        for out_going, in_coming in _gather_copies(b_refs, s_sems, r_sems):
            out_going.wait_send()
            in_coming.wait_recv()

    return pl.pallas_call(
        body, name=name, in_specs=[HBM] * n + [SEM, SEM, ANY], out_specs=[HBM] * n,
        out_shape=[pltpu.HBM(t.shape, t.dtype) for t in bufs],
        input_output_aliases={i: i for i in range(n)},
        compiler_params=pltpu.CompilerParams(has_side_effects=DATAFLOW),
    )(*bufs, send_sems, recv_sems, after)


def _gather_forward(bufs, name):
    idx = [i for i, t in enumerate(bufs) if t.shape[1] % 32 == 0]
    xs = [bufs[i] for i in idx]
    n = len(xs)

    def body(*refs):
        o_refs = refs[n:2 * n]
        send_sems, recv_sems = refs[2 * n:]
        x, y, c, _, chips = _place()
        copies = []
        for i in range(n):
            mine, theirs, _ = _my_rows(xs[i].shape[1], c)
            for j, (cx, cy) in enumerate(chips):
                def rcopy(slab, i=i, j=j):
                    return pltpu.make_async_remote_copy(src_ref=slab, dst_ref=slab, send_sem=send_sems.at[3 * i + j],
                                                        recv_sem=recv_sems.at[3 * i + j], device_id=(x, y, 1 - c),
                                                        device_id_type=MESH_IDS)
                copies.append((rcopy(o_refs[i].at[2 * cx + cy, mine]), rcopy(o_refs[i].at[2 * cx + cy, theirs])))
                copies[-1][0].start()
        for out_going, in_coming in copies:
            out_going.wait_send()
            in_coming.wait_recv()

    res = pl.pallas_call(
        body, name=name, in_specs=[ANY] * n, out_specs=[ANY] * n,
        out_shape=[jax.ShapeDtypeStruct(t.shape, t.dtype) for t in xs],
        input_output_aliases={i: i for i in range(n)},
        scratch_shapes=[pltpu.SemaphoreType.DMA((3 * n,)), pltpu.SemaphoreType.DMA((3 * n,))],
        compiler_params=pltpu.CompilerParams(has_side_effects=True),
    )(*xs)
    out = list(bufs)
    for i, t in zip(idx, res):
        out[i] = t
    return out


def _half_tile(rh, cols):
    return _pick(rh, max(16, (1 << 18) // cols // 16 * 16), 16)


N_LAND = 7


def _scatter_copies(p_refs, r_refs, send_sems, recv_sems):
    x, y, c, me, chips = _place()
    outgoing, incoming = [], []
    for i, (p, land) in enumerate(zip(p_refs, r_refs)):
        rh = p.shape[1] // 2

        def copy(src, slot, k_send, k_recv, to, i=i, land=land):
            return pltpu.make_async_remote_copy(
                src_ref=src, dst_ref=land.at[slot], send_sem=send_sems.at[N_LAND * i + k_send],
                recv_sem=recv_sems.at[N_LAND * i + k_recv], device_id=to, device_id_type=MESH_IDS)

        for r, (cx, cy) in enumerate(chips):
            for k in range(2):
                outgoing.append(copy(p.at[2 * cx + cy, pl.ds(k * rh, rh)], 2 * r + c, 2 * r + k, 2 * r + c,
                                     (cx, cy, k)))
                incoming.append(copy(land.at[2 * r + k], 2 * r + k, 2 * r + k, 2 * r + k, (x, y, c)))
        outgoing.append(copy(p.at[me, pl.ds((1 - c) * rh, rh)], 6, 6, 6, (x, y, 1 - c)))
        incoming.append(copy(land.at[6], 6, 6, 6, (x, y, c)))
    return outgoing, incoming


def _scatter_start(ps, name):
    n = len(ps)
    lands = [lax.empty((N_LAND, t.shape[1] // 2, t.shape[2]), t.dtype) for t in ps]

    def body(*refs):
        send_sems, recv_sems = refs[2 * n], refs[2 * n + 1]
        p_refs = refs[2 * n + 2:3 * n + 2]
        r_refs = refs[3 * n + 2:4 * n + 2]
        token = refs[4 * n + 2]
        for cp in _scatter_copies(p_refs, r_refs, send_sems, recv_sems)[0]:
            cp.start()
        token[...] = jnp.zeros_like(token)

    res = pl.pallas_call(
        body, name=name, in_specs=[HBM] * (2 * n),
        out_specs=[SEM, SEM] + [HBM] * (2 * n) + [pl.BlockSpec(memory_space=pltpu.VMEM)],
        out_shape=[pltpu.SemaphoreType.DMA((N_LAND * n,)), pltpu.SemaphoreType.DMA((N_LAND * n,))]
        + [pltpu.HBM(t.shape, t.dtype) for t in list(ps) + lands] + [TOKEN],
        input_output_aliases={i: i + 2 for i in range(2 * n)},
        compiler_params=pltpu.CompilerParams(has_side_effects=DATAFLOW),
    )(*[_in_hbm(t) for t in list(ps) + lands])
    return res[0], res[1], res[2:2 + n], res[2 + n:2 + 2 * n], res[2 + 2 * n]


def _scatter_wait(ps, lands, send_sems, recv_sems, after, name):
    n = len(ps)
    after = tuple(after) if isinstance(after, (tuple, list)) else (after,)

    def body(*refs):
        p_refs, r_refs = refs[:n], refs[n:2 * n]
        s_sems, r_sems = refs[2 * n], refs[2 * n + 1]
        outgoing, incoming = _scatter_copies(p_refs, r_refs, s_sems, r_sems)
        for cp in outgoing:
            cp.wait_send()
        for cp in incoming:
            cp.wait_recv()

    res = pl.pallas_call(
        body, name=name, in_specs=[HBM] * (2 * n) + [SEM, SEM] + [ANY] * len(after), out_specs=[HBM] * (2 * n),
        out_shape=[pltpu.HBM(t.shape, t.dtype) for t in list(ps) + list(lands)],
        input_output_aliases={i: i for i in range(2 * n)},
        compiler_params=pltpu.CompilerParams(has_side_effects=DATAFLOW),
    )(*ps, *lands, send_sems, recv_sems, *after)
    return res[:n], res[n:]


def _reduce_own(p, rcv, acc, l, name):
    nchip, r, cols = p.shape
    rh = r // 2
    tr = _half_tile(rh, cols)
    nt = rh // tr

    def body(*refs):
        p_ref, r_ref = refs[:2]
        o_ref = refs[-1]
        total = p_ref[...].astype(F32)
        for j in range(N_LAND):
            total = total + r_ref[j].astype(F32)
        o_ref[...] = total

    mine = lambda i: lax.axis_index("c") * nt + i
    in_specs = [pl.BlockSpec((None, tr, cols), lambda i: (_chip_index(), mine(i), 0)),
                pl.BlockSpec((N_LAND, tr, cols), lambda i: (0, i, 0))]
    args = [p, rcv]
    aliases = {}
    if acc is not None:
        in_specs.append(ANY)
        args.append(acc)
        aliases = {2: 0}
    return pl.pallas_call(
        body, name=name, grid=(nt,), in_specs=in_specs,
        out_specs=pl.BlockSpec((None, tr, cols), lambda i: (l, mine(i), 0)),
        out_shape=jax.ShapeDtypeStruct((DEPTH, r, cols), F32), input_output_aliases=aliases,
        compiler_params=_params(("parallel",)),
    )(*args)


def _share_halves(fs, l):
    n = len(fs)

    def body(*refs):
        o_refs = refs[n:2 * n]
        send_sems, recv_sems = refs[2 * n:]
        x, y, c, _, _ = _place()

        def halves(i):
            rh = fs[i].shape[1] // 2
            return o_refs[i].at[l, pl.ds(c * rh, rh), :], o_refs[i].at[l, pl.ds((1 - c) * rh, rh), :]

        def copy(i, rows):
            return pltpu.make_async_remote_copy(src_ref=rows, dst_ref=rows, send_sem=send_sems.at[i],
                                                recv_sem=recv_sems.at[i], device_id=(x, y, 1 - c),
                                                device_id_type=MESH_IDS)

        for i in range(n):
            copy(i, halves(i)[0]).start()
        for i in range(n):
            mine, theirs = halves(i)
            copy(i, mine).wait_send()
            copy(i, theirs).wait_recv()

    return pl.pallas_call(
        body, name="grad_share_halves", in_specs=[ANY] * n, out_specs=[ANY] * n,
        out_shape=[jax.ShapeDtypeStruct(t.shape, t.dtype) for t in fs],
        input_output_aliases={i: i for i in range(n)},
        scratch_shapes=[pltpu.SemaphoreType.DMA((n,)), pltpu.SemaphoreType.DMA((n,))],
        compiler_params=pltpu.CompilerParams(has_side_effects=True),
    )(*fs)


def _reduce_scatter_begin(slices, l, tag):
    names = list(slices)
    send_sems, recv_sems, ps, lands, token = _scatter_start([slices[k] for k in names],
                                                            "grad_scatter_start_%s%d" % (tag, l))
    return dict(names=names, ps=ps, lands=lands, sems=(send_sems, recv_sems), token=token, l=l, tag=tag)


def _reduce_scatter_end(st, stacked, after):
    l = st['l']
    ps, rs = _scatter_wait(st['ps'], st['lands'], *st['sems'], after, "grad_scatter_wait_%s%d" % (st['tag'], l))
    fs = [_reduce_own(p, r, stacked.get(k), l, "grad_reduce_own_" + k)
          for k, p, r in zip(st['names'], ps, rs)]
    stacked.update(zip(st['names'], _share_halves(fs, l)))


def _all_reduce_small(packed, after=()):
    rows = packed.shape[0]
    rh = rows // 2
    tr = _pick(rh, 512, 8)

    def body(x_ref, *rest):
        o_ref, sib, chip_sums, send_sems, recv_sems = rest[len(after):]
        x, y, c, me, chips = _place()
        sibling = (x, y, 1 - c)
        mine, theirs = pl.ds(c * rh, rh), pl.ds((1 - c) * rh, rh)

        def copy(src, dst, k, to):
            return pltpu.make_async_remote_copy(src_ref=src, dst_ref=dst, send_sem=send_sems.at[k],
                                                recv_sem=recv_sems.at[k], device_id=to, device_id_type=MESH_IDS)

        def tiles(fn):
            @pl.loop(0, rh // tr)
            def _(t):
                fn(pl.ds(pl.multiple_of(t * tr, 8), tr))

        to_sibling = copy(x_ref.at[theirs], sib, 0, sibling)
        to_sibling.start()
        to_sibling.wait()

        def chip_sum(sl):
            chip_sums[me, sl, :] = x_ref.at[mine][sl, :] + sib[sl, :]
        tiles(chip_sum)

        out = [copy(chip_sums.at[me], chip_sums.at[me], 1 + j, (cx, cy, c)) for j, (cx, cy) in enumerate(chips)]
        for cp in out:
            cp.start()
        for j, (cx, cy) in enumerate(chips):
            copy(chip_sums.at[2 * cx + cy], chip_sums.at[2 * cx + cy], 1 + j, (x, y, c)).wait_recv()
        for cp in out:
            cp.wait_send()

        def total(sl):
            acc = chip_sums[0, sl, :]
            for s in range(1, N_CHIPS):
                acc = acc + chip_sums[s, sl, :]
            o_ref.at[mine][sl, :] = acc
        tiles(total)

        share = copy(o_ref.at[mine], o_ref.at[mine], 4, sibling)
        share.start()
        share.wait_send()
        copy(o_ref.at[theirs], o_ref.at[theirs], 4, sibling).wait_recv()

    vm = pl.BlockSpec(memory_space=pltpu.VMEM)
    return pl.pallas_call(
        body, name="all_reduce_small", in_specs=[vm] + [ANY] * len(after), out_specs=vm,
        out_shape=jax.ShapeDtypeStruct(packed.shape, F32),
        scratch_shapes=[pltpu.VMEM((rh, LANES), F32), pltpu.VMEM((N_CHIPS, rh, LANES), F32),
                        pltpu.SemaphoreType.DMA((5,)), pltpu.SemaphoreType.DMA((5,))],
        compiler_params=pltpu.CompilerParams(vmem_limit_bytes=VMEM_LIMIT, has_side_effects=True),
    )(packed, *after)


_WEIGHTS = ('norm_mix_g', 'w_in', 'qkv_conv_w', 'a_log', 'dt_bias', 'gdn_norm_g', 'w_branch_a', 'sgu_ln_g',
            'sgu_ln_b', 'sgu_w', 'sgu_b', 'w_branch_b', 'w_out', 'norm_ffn_g', 'w_up', 'ffn_conv_w', 'ffn_conv_b',
            'w_down', 'final_norm_g')


def _local_step(x, target, weights_of, final_norm_g, bwd_after=None, mid_layer=None, last_layer=None):
    lws, saves = [], []
    for l in range(DEPTH):
        lw, late = weights_of(l, x)
        x, s, lw = _layer_fwd(x, lw, late)
        lws.append(lw)
        saves.append(s)
    loss, dx, dx_bf, d_final = _loss_head(x, final_norm_g.reshape(1, -1), target)
    grads = [None] * DEPTH
    for l in reversed(range(DEPTH)):
        after = bwd_after(l) if bwd_after is not None else ()
        mid = (lambda g, dh2, l=l: mid_layer(l, g, dh2)) if mid_layer is not None else None
        last = (lambda g, dp, l=l: last_layer(l, g, dp)) if last_layer is not None else None
        dx, dx_bf, grads[l] = _layer_bwd(dx, dx_bf, lws[l], saves[l], after, mid, last)
    return loss, dx, grads, d_final


def _pack_small(grads, d_final):
    parts = [grads[l][k].reshape(-1) for l in range(DEPTH) for k in _SMALL_GRADS] + [d_final.reshape(-1)]
    flat = jnp.concatenate(parts)
    rows = -(-flat.size // (16 * LANES)) * 16
    return jnp.pad(flat, (0, rows * LANES - flat.size)).reshape(rows, LANES), [p.size for p in parts]


def _unpack_small(packed, grads, d_final):
    flat = packed.reshape(-1)
    out, off = [], 0
    for l in range(DEPTH):
        d = {}
        for k in _SMALL_GRADS:
            t = grads[l][k]
            d[k] = flat[off:off + t.size].reshape(t.shape)
            off += t.size
        out.append(d)
    return out, flat[off:off + d_final.size].reshape(d_final.shape)


def kernel(x, norm_mix_g, w_in, qkv_conv_w, a_log, dt_bias, gdn_norm_g, w_branch_a, sgu_ln_g, sgu_ln_b, sgu_w, sgu_b, w_branch_b, w_out, norm_ffn_g, w_up, ffn_conv_w, ffn_conv_b, w_down, final_norm_g, loss_target, m_norm_mix_g, m_w_in, m_qkv_conv_w, m_a_log, m_dt_bias, m_gdn_norm_g, m_w_branch_a, m_sgu_ln_g, m_sgu_ln_b, m_sgu_w, m_sgu_b, m_w_branch_b, m_w_out, m_norm_ffn_g, m_w_up, m_ffn_conv_w, m_ffn_conv_b, m_w_down, m_final_norm_g, v_norm_mix_g, v_w_in, v_qkv_conv_w, v_a_log, v_dt_bias, v_gdn_norm_g, v_w_branch_a, v_sgu_ln_g, v_sgu_ln_b, v_sgu_w, v_sgu_b, v_w_branch_b, v_w_out, v_norm_ffn_g, v_w_up, v_ffn_conv_w, v_ffn_conv_b, v_w_down, v_final_norm_g):
    w = dict(norm_mix_g=norm_mix_g, w_in=w_in, qkv_conv_w=qkv_conv_w, a_log=a_log, dt_bias=dt_bias,
             gdn_norm_g=gdn_norm_g, w_branch_a=w_branch_a, sgu_ln_g=sgu_ln_g, sgu_ln_b=sgu_ln_b, sgu_w=sgu_w,
             sgu_b=sgu_b, w_branch_b=w_branch_b, w_out=w_out, norm_ffn_g=norm_ffn_g, w_up=w_up,
             ffn_conv_w=ffn_conv_w, ffn_conv_b=ffn_conv_b, w_down=w_down, final_norm_g=final_norm_g)
    m = dict(norm_mix_g=m_norm_mix_g, w_in=m_w_in, qkv_conv_w=m_qkv_conv_w, a_log=m_a_log, dt_bias=m_dt_bias,
             gdn_norm_g=m_gdn_norm_g, w_branch_a=m_w_branch_a, sgu_ln_g=m_sgu_ln_g, sgu_ln_b=m_sgu_ln_b,
             sgu_w=m_sgu_w, sgu_b=m_sgu_b, w_branch_b=m_w_branch_b, w_out=m_w_out, norm_ffn_g=m_norm_ffn_g,
             w_up=m_w_up, ffn_conv_w=m_ffn_conv_w, ffn_conv_b=m_ffn_conv_b, w_down=m_w_down,
             final_norm_g=m_final_norm_g)
    v = dict(norm_mix_g=v_norm_mix_g, w_in=v_w_in, qkv_conv_w=v_qkv_conv_w, a_log=v_a_log, dt_bias=v_dt_bias,
             gdn_norm_g=v_gdn_norm_g, w_branch_a=v_w_branch_a, sgu_ln_g=v_sgu_ln_g, sgu_ln_b=v_sgu_ln_b,
             sgu_w=v_sgu_w, sgu_b=v_sgu_b, w_branch_b=v_w_branch_b, w_out=v_w_out, norm_ffn_g=v_norm_ffn_g,
             w_up=v_w_up, ffn_conv_w=v_ffn_conv_w, ffn_conv_b=v_ffn_conv_b, w_down=v_w_down,
             final_norm_g=v_final_norm_g)
    chip = _chip_index()

    in_flight = []
    for l in range(DEPTH):
        for tag, names in (("first", _FIRST), ("rest", _REST)):
            placed = [_place_shard(w[k], l, BF16 if k in _BIG else F32, "place_" + k) for k in names]
            earlier = (in_flight[-1][3],) if in_flight else ()
            in_flight.append(_gather_start(placed, "gather_start_%s_%d" % (tag, l), earlier))

    def arrived(l, tag, names, after):
        send_sems, recv_sems, bufs, _ = in_flight[2 * l + (tag == "rest")]
        bufs = _gather_wait(bufs, send_sems, recv_sems, after, "gather_wait_%s_%d" % (tag, l))
        return dict(zip(names, _gather_forward(bufs, "gather_forward_" + tag)))

    def weights_of(l, x_in):
        first = arrived(l, "first", _FIRST, in_flight[-1][3] if l == 0 else x_in)
        return (_layer_weights_first(l, first, w),
                lambda p: _layer_weights_rest(arrived(l, "rest", _REST, p)))

    group_a, group_b = ('w_up', 'w_down'), ('w_in', 'w_branch_a', 'w_branch_b', 'w_out')
    big_g = {}
    pending = {'a': None, 'b': None}

    def finish(tag, after):
        if pending[tag] is not None:
            _reduce_scatter_end(pending[tag], big_g, after)
            pending[tag] = None

    def bwd_after(l):
        return () if pending['b'] is None else (pending['b']['token'],)

    def mid_layer(l, g, dh2):
        finish('b', dh2)
        pending['a'] = _reduce_scatter_begin(_big_grad_slices(g, group_a), l, 'a')
        return (pending['a']['token'],)

    def last_layer(l, g, dp):
        finish('a', dp)
        pending['b'] = _reduce_scatter_begin(_big_grad_slices(g, group_b), l, 'b')
        return (pending['b']['token'],)

    loss, grad_x, grads, d_final = _local_step(x[0], loss_target[0], weights_of, w['final_norm_g'], bwd_after,
                                               mid_layer, last_layer)
    loss = lax.psum(loss[0, 0], ("x", "y", "c"))
    packed, _ = _pack_small(grads, d_final)
    reduced_small = _all_reduce_small(packed, (pending['b']['token'],))
    small_g, d_final = _unpack_small(reduced_small, grads, d_final)
    deltas, new_m, new_v, g_out = {}, {}, {}, {}
    for k in group_a:
        deltas[k], new_m[k], new_v[k], g_out[k] = _adamw(w[k], big_g[k].reshape(w[k].shape), m[k], v[k],
                                                         "adamw_" + k, (pending['b']['token'],), pass_g=True)
    finish('b', (reduced_small,) + tuple(deltas[k] for k in group_a))

    def stack(k):
        return jnp.stack([small_g[l][k] for l in range(DEPTH)])

    nd = 2 * GDN_HEADS
    for k in ('norm_mix_g', 'gdn_norm_g', 'sgu_ln_g', 'sgu_ln_b', 'norm_ffn_g', 'ffn_conv_b'):
        g_out[k] = stack(k).reshape(w[k].shape)
    g_out['sgu_w'] = stack('sgu_w')
    g_out['a_log'] = stack('a_log')[:, 0, :nd].reshape(w['a_log'].shape)
    g_out['dt_bias'] = stack('dt_bias')[:, 0, :nd].reshape(w['dt_bias'].shape)
    g_out['sgu_b'] = jnp.swapaxes(stack('sgu_bt')[:, :, :SGU_GROUPS], 1, 2)
    for k in ('qkv_conv_w', 'ffn_conv_w'):
        full = stack(k)
        width = w[k].shape[-1]
        g_out[k] = lax.dynamic_slice_in_dim(full, chip * width, width, axis=2)
    g_out['final_norm_g'] = d_final.reshape(w['final_norm_g'].shape)

    for k in group_b:
        deltas[k], new_m[k], new_v[k], g_out[k] = _adamw(w[k], big_g[k].reshape(w[k].shape), m[k], v[k],
                                                         "adamw_" + k, pass_g=True)
    for k in _WEIGHTS:
        if k not in _BIG:
            deltas[k], new_m[k], new_v[k] = _adamw(w[k], g_out[k], m[k], v[k], "adamw_" + k)
    return (loss, grad_x[None], *[g_out[k] for k in _WEIGHTS], *[deltas[k] for k in _WEIGHTS],
            *[new_m[k] for k in _WEIGHTS], *[new_v[k] for k in _WEIGHTS])
```

```python
import functools
import math

import jax
import jax.numpy as jnp
from jax import lax
from jax.experimental import pallas as pl
from jax.experimental.pallas import tpu as pltpu

F32 = jnp.float32
BF16 = jnp.bfloat16
PARTIAL = BF16

D_MODEL = 2048
SEQ = 2048
DEPTH = 4
GDN_HEADS = 8
HEAD_DIM = 128
GDN_CHUNK = 64
QKV_CONV = 5
SGU_GROUPS = 8
SGU_DIM = 128
SGU_BLOCK = 128
D_FF = 5632
FFN_CONV = 3
NORM_EPS = 1e-6
N_CHIPS = 4

ADAM_LR = 0.001
ADAM_B1 = 0.9
ADAM_B2 = 0.999
ADAM_EPS = 1e-08
ADAM_WD = 0.01
ADAM_STEP = 10

LANES = 128
VMEM_LIMIT = 56 * 1024 * 1024
AB_PAD = LANES


def _gw():
    return GDN_HEADS * HEAD_DIM


def _sw():
    return SGU_GROUPS * SGU_DIM


def _n_all():
    return 4 * _gw() + 2 * _sw() + 2 * D_MODEL + AB_PAD


def _pick(n, target, align=LANES):
    if n <= target:
        return n
    best = None
    t = align
    while t <= target:
        if n % t == 0:
            best = t
        t += align
    assert best is not None, (n, target, align)
    return best


def _params(sem=None):
    return pltpu.CompilerParams(dimension_semantics=sem, vmem_limit_bytes=VMEM_LIMIT)


def _dot(a, b, dims, hi):
    dn = (dims, ((), ()))
    if not hi:
        return lax.dot_general(a.astype(BF16), b.astype(BF16), dn, preferred_element_type=F32)
    a_hi, b_hi = a.astype(BF16), b.astype(BF16)
    a_lo = (a - a_hi.astype(F32)).astype(BF16)
    b_lo = (b - b_hi.astype(F32)).astype(BF16)
    d = lambda p, q: lax.dot_general(p, q, dn, preferred_element_type=F32)
    return d(a_hi, b_hi) + (d(a_hi, b_lo) + d(a_lo, b_hi))


def _make_mm(hi):
    @jax.custom_vjp
    def mm(a, b):
        return _dot(a, b, ((1,), (0,)), hi)

    @jax.custom_vjp
    def mm_nt(a, b):
        return _dot(a, b, ((1,), (1,)), hi)

    @jax.custom_vjp
    def mm_tn(a, b):
        return _dot(a, b, ((0,), (0,)), hi)

    mm.defvjp(lambda a, b: (mm(a, b), (a, b)), lambda r, g: (mm_nt(g, r[1]), mm_tn(r[0], g)))
    mm_nt.defvjp(lambda a, b: (mm_nt(a, b), (a, b)), lambda r, g: (mm(g, r[1]), mm_tn(g, r[0])))
    mm_tn.defvjp(lambda a, b: (mm_tn(a, b), (a, b)), lambda r, g: (mm_nt(r[1], g), mm(r[0], g)))
    return mm, mm_nt, mm_tn


_mm, _mm_nt, _mm_tn = _make_mm(False)
_mmh, _mmh_nt, _mmh_tn = _make_mm(True)


def _shift_rows_raw(x, s):
    if s == 0:
        return x
    n = x.shape[0]
    rolled = pltpu.roll(x, (-s) % n, 0)
    t = lax.broadcasted_iota(jnp.int32, x.shape, 0)
    ok = (t + s >= 0) & (t + s < n)
    return jnp.where(ok, rolled, 0.0)


@functools.partial(jax.custom_vjp, nondiff_argnums=(1,))
def _shift_rows(x, s):
    return _shift_rows_raw(x, s)


_shift_rows.defvjp(lambda x, s: (_shift_rows_raw(x, s), None), lambda s, _, g: (_shift_rows_raw(g, -s),))


def _sigmoid(x):
    return 1.0 / (1.0 + jnp.exp(-x))


def _silu(x):
    return x * _sigmoid(x)


def _gelu(x):
    return 0.5 * x * (1.0 + jnp.tanh(math.sqrt(2.0 / math.pi) * (x + 0.044715 * x * x * x)))


def _softplus(x):
    return jnp.maximum(x, 0.0) + jnp.log(1.0 + jnp.exp(-jnp.abs(x)))


def _rms(x, g):
    return x * lax.rsqrt(jnp.mean(x * x, axis=-1, keepdims=True) + NORM_EPS) * g


def _conv_rows(x, taps):
    pad = len(taps) // 2
    acc = None
    for j, w in enumerate(taps):
        term = _shift_rows(x, j - pad) * w
        acc = term if acc is None else acc + term
    return acc


@jax.custom_vjp
def _inv_unit(mats):
    n = mats[0].shape[0]
    eye = (lax.broadcasted_iota(jnp.int32, (n, n), 0) == lax.broadcasted_iota(jnp.int32, (n, n), 1)).astype(F32)
    ps = [eye - a for a in mats]
    aks = list(mats)
    for _ in range(int(math.log2(n)) - 1):
        aks = [_mmh(ak, ak) for ak in aks]
        ps = [p + _mmh(p, ak) for p, ak in zip(ps, aks)]
    return ps


def _inv_unit_fwd(mats):
    ts = _inv_unit(mats)
    return ts, ts


def _inv_unit_bwd(ts, gs):
    inner = [_mmh_nt(g, t) for g, t in zip(gs, ts)]
    return ([-_mmh_tn(t, m) for t, m in zip(ts, inner)],)


_inv_unit.defvjp(_inv_unit_fwd, _inv_unit_bwd)


def _tiled(fn, name, grid, ins, outs, sem=None, after=()):
    n_in = len(ins)

    def body(*refs):
        vals = [r[...] for r in refs[:n_in]]
        res = fn(*vals)
        if not isinstance(res, (tuple, list)):
            res = (res,)
        for o_ref, r, spec in zip(refs[n_in + len(after):], res, outs):
            acc = spec[4]
            if not acc:
                o_ref[...] = r.astype(o_ref.dtype)
            else:
                first = functools.reduce(jnp.logical_and, [pl.program_id(a) == 0 for a in acc])

                @pl.when(first)
                def _(o_ref=o_ref, r=r):
                    o_ref[...] = r.astype(o_ref.dtype)

                @pl.when(jnp.logical_not(first))
                def _(o_ref=o_ref, r=r):
                    o_ref[...] += r.astype(o_ref.dtype)

    return pl.pallas_call(
        body, name=name, grid=grid,
        in_specs=[pl.BlockSpec(b, m) for _, b, m in ins] + [pl.BlockSpec(memory_space=pl.ANY)] * len(after),
        out_specs=[pl.BlockSpec(s[2], s[3]) for s in outs],
        out_shape=[jax.ShapeDtypeStruct(s[0], s[1]) for s in outs],
        compiler_params=_params(sem),
    )(*[a for a, _, _ in ins], *after)


def _matmul(a, b, mode, out_dtype, name, add=None, a_split=1, b_split=1, o_split=1, after=()):
    def dims2(x, split):
        return (x.shape[-2], x.shape[-1] * split)

    ar, ac = dims2(a, a_split)
    br, bc = dims2(b, b_split)
    if mode == 'nn':
        M, K, N = ar, ac, bc
        assert br == K
    elif mode == 'nt':
        M, K, N = ar, ac, br
        assert bc == K
    else:
        K, M, N = ar, ac, bc
        assert br == K
    tm = _pick(M // (a_split if mode == 'tn' else 1), 2048 if mode == 'tn' else 1024)
    tn = _pick(N // max(o_split, b_split if mode != 'nt' else 1), 1408)
    tk = _pick(K // max(a_split if mode != 'tn' else 1, b_split if mode == 'nt' else 1),
               {'nt': 3584, 'nn': 2816, 'tn': 2048}[mode])
    gm, gn, gk = M // tm, N // tn, K // tk

    def col_map(split, total_cols, tile):
        per = total_cols // split // tile

        def f(r, c):
            return (c // per, r, c % per) if split > 1 else (r, c)
        return f

    if mode == 'nn':
        a_idx = col_map(a_split, K, tk)
        b_idx = col_map(b_split, N, tn)
        a_spec = pl.BlockSpec(((None,) if a_split > 1 else ()) + (tm, tk), lambda m, n, k: a_idx(m, k))
        b_spec = pl.BlockSpec(((None,) if b_split > 1 else ()) + (tk, tn), lambda m, n, k: b_idx(k, n))
        dn = ((1,), (0,))
    elif mode == 'nt':
        a_idx = col_map(a_split, K, tk)
        b_idx = col_map(b_split, K, tk)
        a_spec = pl.BlockSpec(((None,) if a_split > 1 else ()) + (tm, tk), lambda m, n, k: a_idx(m, k))
        b_spec = pl.BlockSpec(((None,) if b_split > 1 else ()) + (tn, tk), lambda m, n, k: b_idx(n, k))
        dn = ((1,), (1,))
    else:
        a_idx = col_map(a_split, M, tm)
        b_idx = col_map(b_split, N, tn)
        a_spec = pl.BlockSpec(((None,) if a_split > 1 else ()) + (tk, tm), lambda m, n, k: a_idx(k, m))
        b_spec = pl.BlockSpec(((None,) if b_split > 1 else ()) + (tk, tn), lambda m, n, k: b_idx(k, n))
        dn = ((0,), (0,))
    o_idx = col_map(o_split, N, tn)
    o_block = ((None,) if o_split > 1 else ()) + (tm, tn)
    o_spec = pl.BlockSpec(o_block, lambda m, n, k: o_idx(m, n))
    o_shape = ((o_split, M, N // o_split) if o_split > 1 else (M, N))
    in_specs = [a_spec, b_spec]
    args = [a, b]
    if add is not None:
        in_specs.append(pl.BlockSpec((tm, tn), lambda m, n, k: (m, n)))
        args.append(add)
    n_lead = len(args)
    for t in after:
        in_specs.append(pl.BlockSpec(memory_space=pl.ANY))
        args.append(t)

    def body(*refs):
        a_ref, b_ref = refs[0], refs[1]
        add_ref = refs[2] if add is not None else None
        o_ref = refs[n_lead + len(after)]
        part = lax.dot_general(a_ref[...].astype(BF16), b_ref[...].astype(BF16), (dn, ((), ())),
                               preferred_element_type=F32)

        def finish(total):
            if add_ref is not None:
                total = total + add_ref[...]
            o_ref[...] = total.astype(o_ref.dtype)

        if gk == 1:
            finish(part)
        else:
            acc_ref = refs[-1]
            k = pl.program_id(2)

            @pl.when(k == 0)
            def _():
                acc_ref[...] = part

            @pl.when(jnp.logical_and(k > 0, k < gk - 1))
            def _():
                acc_ref[...] += part

            @pl.when(k == gk - 1)
            def _():
                finish(acc_ref[...] + part)

    return pl.pallas_call(
        body, name=name, grid=(gm, gn, gk), in_specs=in_specs, out_specs=o_spec,
        out_shape=jax.ShapeDtypeStruct(o_shape, out_dtype),
        scratch_shapes=([pltpu.VMEM((tm, tn), F32)] if gk > 1 else []),
        compiler_params=_params(("parallel", "parallel", "arbitrary")),
    )(*args)


def _row_tile():
    return _pick(SEQ, 256, 8)


def _rms_fwd(x, g, name):
    tm = _row_tile()
    (h,) = _tiled(lambda xv, gv: _rms(xv, gv), name, (SEQ // tm,),
                  [(x, (tm, D_MODEL), lambda i: (i, 0)), (g, (1, D_MODEL), lambda i: (0, 0))],
                  [((SEQ, D_MODEL), BF16, (tm, D_MODEL), lambda i: (i, 0), ())])
    return h


def _rms_bwd(x, g, dh, dres, name):
    tm = _row_tile()

    def fn(xv, gv, dhv, drv):
        _, vjp = jax.vjp(_rms, xv, gv)
        dx, dg = vjp(dhv)
        dx = dx + drv
        return dx, dx, dg

    row = lambda i: (i, 0)
    return _tiled(fn, name, (SEQ // tm,),
                  [(x, (tm, D_MODEL), row), (g, (1, D_MODEL), lambda i: (0, 0)),
                   (dh, (tm, D_MODEL), row), (dres, (tm, D_MODEL), row)],
                  [((SEQ, D_MODEL), F32, (tm, D_MODEL), row, ()),
                   ((SEQ, D_MODEL), BF16, (tm, D_MODEL), row, ()),
                   ((1, D_MODEL), F32, (1, D_MODEL), lambda i: (0, 0), (0,))])


def _loss_head(x, g, target):
    tm = _row_tile()

    def fn(xv, gv, tv):
        y, vjp = jax.vjp(_rms, xv, gv)
        err = y - tv
        part = 0.5 * jnp.sum(jnp.sum(err * err, axis=1, keepdims=True), axis=0, keepdims=True) / D_MODEL
        dx, dg = vjp(err / D_MODEL)
        return part, dx, dx, dg

    row = lambda i: (i, 0)
    one = lambda i: (0, 0)
    return _tiled(fn, "loss_head", (SEQ // tm,),
                  [(x, (tm, D_MODEL), row), (g, (1, D_MODEL), one), (target, (tm, D_MODEL), row)],
                  [((1, 1), F32, (1, 1), one, (0,)),
                   ((SEQ, D_MODEL), F32, (tm, D_MODEL), row, ()),
                   ((SEQ, D_MODEL), BF16, (tm, D_MODEL), row, ()),
                   ((1, D_MODEL), F32, (1, D_MODEL), one, (0,))])


def _p_col(width, index):
    return lambda tm: ((tm, width), lambda i: (i, index))


def _merge_fn(ga, gb, ta, tb):
    return _sigmoid(ga) * ta + _sigmoid(gb) * tb


def _merge_fwd(p, ta, tb):
    tm = _row_tile()
    d = D_MODEL
    ga_blk = 4 * _gw() + 2 * _sw()
    assert ga_blk % d == 0
    ia, ib = ga_blk // d, ga_blk // d + 1
    row = lambda i: (i, 0)
    (m,) = _tiled(_merge_fn, "merge_fwd", (SEQ // tm,),
                  [(p, (tm, d), lambda i: (i, ia)), (p, (tm, d), lambda i: (i, ib)),
                   (ta, (tm, d), row), (tb, (tm, d), row)],
                  [((SEQ, d), BF16, (tm, d), row, ())])
    return m


def _merge_bwd(p, ta, tb, dm):
    tm = _row_tile()
    d = D_MODEL
    ga_blk = 4 * _gw() + 2 * _sw()
    ia, ib = ga_blk // d, ga_blk // d + 1

    def fn(ga, gb, tav, tbv, dmv):
        _, vjp = jax.vjp(_merge_fn, ga, gb, tav, tbv)
        return vjp(dmv)

    row = lambda i: (i, 0)
    out = ((SEQ, d), BF16, (tm, d), row, ())
    return _tiled(fn, "merge_bwd", (SEQ // tm,),
                  [(p, (tm, d), lambda i: (i, ia)), (p, (tm, d), lambda i: (i, ib)),
                   (ta, (tm, d), row), (tb, (tm, d), row), (dm, (tm, d), row)],
                  [out, out, out, out])


def _gate_fn(ab, alog, dtb):
    lane = lax.broadcasted_iota(jnp.int32, ab.shape, 1)
    nd = 2 * GDN_HEADS
    g = -jnp.exp(alog) * _softplus(ab + dtb)
    beta = _sigmoid(ab)
    return jnp.where(lane < nd, g, jnp.where(lane < 2 * nd, beta, 0.0))


def _ab_index():
    off = 4 * _gw() + 2 * _sw() + 2 * D_MODEL
    assert off % AB_PAD == 0
    return off // AB_PAD


def _gate_fwd(p, alog, dtb):
    tm = _row_tile()
    iab = _ab_index()
    one = lambda i: (0, 0)
    (g,) = _tiled(_gate_fn, "gdn_gate_fwd", (SEQ // tm,),
                  [(p, (tm, AB_PAD), lambda i: (i, iab)), (alog, (1, AB_PAD), one), (dtb, (1, AB_PAD), one)],
                  [((SEQ, AB_PAD), F32, (tm, AB_PAD), lambda i: (i, 0), ())])
    return g


def _gate_bwd(p, alog, dtb, dg):
    tm = _row_tile()
    iab = _ab_index()
    one = lambda i: (0, 0)

    def fn(ab, al, db, dgv):
        _, vjp = jax.vjp(_gate_fn, ab, al, db)
        return vjp(dgv)

    return _tiled(fn, "gdn_gate_bwd", (SEQ // tm,),
                  [(p, (tm, AB_PAD), lambda i: (i, iab)), (alog, (1, AB_PAD), one), (dtb, (1, AB_PAD), one),
                   (dg, (tm, AB_PAD), lambda i: (i, 0))],
                  [((SEQ, AB_PAD), BF16, (tm, AB_PAD), lambda i: (i, 0), ()),
                   ((1, AB_PAD), F32, (1, AB_PAD), one, (0,)),
                   ((1, AB_PAD), F32, (1, AB_PAD), one, (0,))])


def _l2n(x):
    return x * lax.rsqrt(jnp.sum(x * x, axis=-1, keepdims=True) + NORM_EPS)


def _qkv_fn(xq, xk, xv, *taps):
    n = QKV_CONV
    q = _l2n(_silu(_conv_rows(xq, taps[0:n])))
    k = _l2n(_silu(_conv_rows(xk, taps[n:2 * n])))
    v = _silu(_conv_rows(xv, taps[2 * n:3 * n]))
    return q, k, v


def _qkv_specs(p, conv_w):
    hd, nh = HEAD_DIM, GDN_HEADS
    ins = [(p, (SEQ, hd), (lambda h, s=s: (0, s * nh + h))) for s in range(3)]
    ins += [(conv_w, (QKV_CONV, hd), (lambda h, s=s: (0, s * nh + h))) for s in range(3)]
    return ins


def _qkv_fwd(p, conv_w):
    hd, nh = HEAD_DIM, GDN_HEADS
    ins = _qkv_specs(p, conv_w)
    n_in = len(ins)
    out_spec = pl.BlockSpec((SEQ, hd), lambda h: (0, h))

    def body(*refs):
        xs = [r[...] for r in refs[:3]]
        taps = [refs[3 + s][j:j + 1, :] for s in range(3) for j in range(QKV_CONV)]
        q, k, v = _qkv_fn(*xs, *taps)
        refs[n_in][...] = q
        refs[n_in + 1][...] = k
        refs[n_in + 2][...] = v

    return pl.pallas_call(
        body, name="gdn_qkv_fwd", grid=(nh,),
        in_specs=[pl.BlockSpec(b, m) for _, b, m in ins], out_specs=[out_spec] * 3,
        out_shape=[jax.ShapeDtypeStruct((SEQ, nh * hd), F32)] * 3,
        compiler_params=_params(("parallel",)),
    )(*[a for a, _, _ in ins])


def _qkv_bwd(p, conv_w, dq, dk, dv):
    hd, nh = HEAD_DIM, GDN_HEADS
    ins = _qkv_specs(p, conv_w) + [(t, (SEQ, hd), lambda h: (0, h)) for t in (dq, dk, dv)]
    n_in = len(ins)

    def body(*refs):
        xs = [r[...] for r in refs[:3]]
        taps = [refs[3 + s][j:j + 1, :] for s in range(3) for j in range(QKV_CONV)]
        cts = tuple(r[...] for r in refs[6:9])
        _, vjp = jax.vjp(_qkv_fn, *xs, *taps)
        grads = vjp(cts)
        for s in range(3):
            refs[n_in + s][...] = grads[s].astype(BF16)
            for j in range(QKV_CONV):
                refs[n_in + 3 + s][j:j + 1, :] = jnp.sum(grads[3 + s * QKV_CONV + j], axis=0, keepdims=True)

    dx_spec = pl.BlockSpec((SEQ, hd), lambda h: (0, h))
    dw_spec = pl.BlockSpec((QKV_CONV, hd), lambda h: (0, h))
    outs = pl.pallas_call(
        body, name="gdn_qkv_bwd", grid=(nh,),
        in_specs=[pl.BlockSpec(b, m) for _, b, m in ins], out_specs=[dx_spec] * 3 + [dw_spec] * 3,
        out_shape=[jax.ShapeDtypeStruct((SEQ, nh * hd), BF16)] * 3
        + [jax.ShapeDtypeStruct((QKV_CONV, nh * hd), F32)] * 3,
        compiler_params=_params(("parallel",)),
    )(*[a for a, _, _ in ins])
    return outs[0:3], jnp.concatenate(outs[3:6], axis=1)


def _prep_fn(qs, ks, vs, gblk):
    nh = len(qs)
    c = qs[0].shape[0]
    scale = HEAD_DIM ** -0.5
    ii = lax.broadcasted_iota(jnp.int32, (c, c), 0)
    jj = lax.broadcasted_iota(jnp.int32, (c, c), 1)
    eye = (ii == jj).astype(F32)
    lane = lax.broadcasted_iota(jnp.int32, gblk.shape, 1)
    kk_t = [_mm_nt(k, k) for k in ks]
    qk_t = [_mm_nt(q, k) for q, k in zip(qs, ks)]
    chains = [(h, d) for h in range(nh) for d in range(2)]
    pre = []
    for h, d in chains:
        g = jnp.sum(jnp.where(lane == h + d * nh, gblk, 0.0), axis=1, keepdims=True)
        beta = jnp.sum(jnp.where(lane == h + (2 + d) * nh, gblk, 0.0), axis=1, keepdims=True)
        incl = (jj <= ii) if d == 0 else (jj >= ii)
        strict = (jj < ii) if d == 0 else (jj > ii)
        incl_t = (ii <= jj) if d == 0 else (ii >= jj)
        g_row = jnp.sum(eye * g, axis=0, keepdims=True)
        gc = jnp.sum(jnp.where(incl, g_row, 0.0), axis=1, keepdims=True)
        gc_row = jnp.sum(jnp.where(incl_t, g, 0.0), axis=0, keepdims=True)
        decay = jnp.where(incl, jnp.exp(jnp.where(incl, gc - gc_row, 0.0)), 0.0)
        a = jnp.where(strict, kk_t[h] * beta * decay, 0.0)
        pre.append((g, beta, gc, decay, incl, a))
    ts = _inv_unit([p[5] for p in pre])
    us = [_mmh(t, vs[h] * p[1]) for (h, d), p, t in zip(chains, pre, ts)]
    ws = [_mmh(t, ks[h] * (p[1] * jnp.exp(p[2]))) for (h, d), p, t in zip(chains, pre, ts)]
    outs = [[None, None] for _ in range(nh)]
    for (h, d), (g, beta, gc, decay, incl, a), u, w in zip(chains, pre, us, ws):
        qk = jnp.where(incl, qk_t[h] * (scale * decay), 0.0)
        g_last = jnp.sum(g, axis=0, keepdims=True)
        q_dec = qs[h] * (scale * jnp.exp(gc))
        k_dec = ks[h] * jnp.exp(g_last - gc)
        outs[h][d] = (u, w, q_dec, k_dec, qk, jnp.exp(g_last))
    return outs


def _n_chunks():
    return SEQ // GDN_CHUNK


def _gdn_prep_fwd(q, k, v, gates):
    c, hd, nh, nc = GDN_CHUNK, HEAD_DIM, GDN_HEADS, _n_chunks()
    gw = nh * hd
    row = lambda n: (n, 0)

    def body(q_ref, k_ref, v_ref, g_ref, u_ref, w_ref, qd_ref, kd_ref, qk_ref, dec_ref):
        heads = [slice(h * hd, (h + 1) * hd) for h in range(nh)]
        res = _prep_fn([q_ref[:, s] for s in heads], [k_ref[:, s] for s in heads], [v_ref[:, s] for s in heads],
                       g_ref[...])
        for h, cols in enumerate(heads):
            for d in range(2):
                u, w, qd, kd, qk, dec = res[h][d]
                u_ref[d, :, cols] = u
                w_ref[d, :, cols] = w
                qd_ref[d, :, cols] = qd
                kd_ref[d, :, cols] = kd
                qk_ref[d, h] = qk
                dec_ref[d, h:h + 1, :] = jnp.broadcast_to(dec, (1, LANES))

    wide = pl.BlockSpec((2, c, gw), lambda n: (0, n, 0))
    return pl.pallas_call(
        body, name="gdn_prep_fwd", grid=(nc,),
        in_specs=[pl.BlockSpec((c, gw), row)] * 3 + [pl.BlockSpec((c, LANES), row)],
        out_specs=[wide] * 4 + [pl.BlockSpec((2, nh, c, c), lambda n: (0, 0, n, 0)),
                                pl.BlockSpec((2, None, nh, LANES), lambda n: (0, n, 0, 0))],
        out_shape=[jax.ShapeDtypeStruct((2, SEQ, gw), F32)] * 4
        + [jax.ShapeDtypeStruct((2, nh, SEQ, c), F32), jax.ShapeDtypeStruct((2, nc, nh, LANES), F32)],
        compiler_params=_params(("parallel",)),
    )(q, k, v, gates)


def _gdn_prep_bwd(q, k, v, gates, cts):
    c, hd, nh, nc = GDN_CHUNK, HEAD_DIM, GDN_HEADS, _n_chunks()
    gw = nh * hd
    row = lambda n: (n, 0)

    def body(*refs):
        q_ref, k_ref, v_ref, g_ref = refs[:4]
        ct_refs = refs[4:16]
        dq_ref, dk_ref, dv_ref, dg_ref = refs[16:20]
        heads = [slice(h * hd, (h + 1) * hd) for h in range(nh)]
        _, vjp = jax.vjp(_prep_fn, [q_ref[:, s] for s in heads], [k_ref[:, s] for s in heads],
                         [v_ref[:, s] for s in heads], g_ref[...])
        cts_in = []
        for h, cols in enumerate(heads):
            per_dir = []
            for d in range(2):
                r = ct_refs[6 * d:6 * d + 6]
                per_dir.append((r[0][:, cols], r[1][:, cols], r[2][:, cols], r[3][:, cols], r[4][h],
                                r[5][h:h + 1, 0:1]))
            cts_in.append(per_dir)
        dqs, dks, dvs, dg = vjp(cts_in)
        for h, cols in enumerate(heads):
            dq_ref[:, cols] = dqs[h]
            dk_ref[:, cols] = dks[h]
            dv_ref[:, cols] = dvs[h]
        dg_ref[...] = dg

    one_dir = [pl.BlockSpec((c, gw), row)] * 4 + [pl.BlockSpec((nh, c, c), lambda n: (0, n, 0)),
                                                  pl.BlockSpec((None, nh, LANES), lambda n: (n, 0, 0))]
    return pl.pallas_call(
        body, name="gdn_prep_bwd", grid=(nc,),
        in_specs=[pl.BlockSpec((c, gw), row)] * 3 + [pl.BlockSpec((c, LANES), row)] + one_dir * 2,
        out_specs=[pl.BlockSpec((c, gw), row)] * 3 + [pl.BlockSpec((c, LANES), row)],
        out_shape=[jax.ShapeDtypeStruct((SEQ, gw), F32)] * 3 + [jax.ShapeDtypeStruct((SEQ, LANES), F32)],
        compiler_params=_params(("parallel",)),
    )(q, k, v, gates, *cts)


def _scan_steps(states, us, ws, qds, kds, qks, decs):
    w_s = [_mm(w, s) for w, s in zip(ws, states)]
    q_s = [_mm(qd, s) for qd, s in zip(qds, states)]
    v_new = [u - x for u, x in zip(us, w_s)]
    outs = [x + _mm(qk, vn) for x, qk, vn in zip(q_s, qks, v_new)]
    k_v = [_mm_tn(kd, vn) for kd, vn in zip(kds, v_new)]
    new_states = [s * dec + x for s, dec, x in zip(states, decs, k_v)]
    return new_states, outs


def _scan_operands(ins, chains, dec_lanes):
    cols = lambda h: slice(h * HEAD_DIM, (h + 1) * HEAD_DIM)
    wide = [[ins[6 * d + j][:, cols(h)] for d, h in chains] for j in range(4)]
    qks = [ins[6 * d + 4][h] for d, h in chains]
    decs = [ins[6 * d + 5][h:h + 1, dec_lanes] for d, h in chains]
    return (*wide, qks, decs)


def _scan_in_specs(chunk_of):
    c, hd, nh = GDN_CHUNK, HEAD_DIM, GDN_HEADS
    gw = nh * hd
    specs = []
    for d in range(2):
        f = chunk_of[d]
        specs += [pl.BlockSpec((None, c, gw), lambda n, d=d, f=f: (d, f(n), 0))] * 4
        specs += [pl.BlockSpec((None, nh, c, c), lambda n, d=d, f=f: (d, 0, f(n), 0)),
                  pl.BlockSpec((None, None, nh, LANES), lambda n, d=d, f=f: (d, f(n), 0, 0))]
    return specs


def _gdn_scan_fwd(u, w, qd, kd, qk, dec):
    c, hd, nh, nc = GDN_CHUNK, HEAD_DIM, GDN_HEADS, _n_chunks()
    gw = nh * hd
    chunk_of = (lambda n: n, lambda n: nc - 1 - n)

    def body(*refs):
        ins = refs[:12]
        o_refs = refs[12:14]
        st_refs = refs[14:16]
        s_ref = refs[16]

        @pl.when(pl.program_id(0) == 0)
        def _():
            s_ref[...] = jnp.zeros_like(s_ref)

        chains = [(d, h) for d in range(2) for h in range(nh)]
        states = [s_ref[d * nh + h] for d, h in chains]
        for (d, h), st in zip(chains, states):
            st_refs[d][h] = st
        new_states, outs = _scan_steps(states, *_scan_operands(ins, chains, slice(None)))
        for (d, h), st, o in zip(chains, new_states, outs):
            s_ref[d * nh + h] = st
            o_refs[d][:, h * hd:(h + 1) * hd] = o

    o_specs = [pl.BlockSpec((c, gw), lambda n, f=f: (f(n), 0)) for f in chunk_of]
    st_specs = [pl.BlockSpec((None, nh, hd, hd), lambda n, f=f: (f(n), 0, 0, 0)) for f in chunk_of]
    return pl.pallas_call(
        body, name="gdn_scan_fwd", grid=(nc,), in_specs=_scan_in_specs(chunk_of),
        out_specs=o_specs + st_specs,
        out_shape=[jax.ShapeDtypeStruct((SEQ, gw), F32)] * 2 + [jax.ShapeDtypeStruct((nc, nh, hd, hd), F32)] * 2,
        scratch_shapes=[pltpu.VMEM((2 * nh, hd, hd), F32)],
        compiler_params=_params(("arbitrary",)),
    )(*([u, w, qd, kd, qk, dec] * 2))


def _gdn_scan_bwd(u, w, qd, kd, qk, dec, st0, st1, do):
    c, hd, nh, nc = GDN_CHUNK, HEAD_DIM, GDN_HEADS, _n_chunks()
    gw = nh * hd
    chunk_of = (lambda n: nc - 1 - n, lambda n: n)

    def body(*refs):
        ins = refs[:12]
        st_in = refs[12:14]
        do_in = refs[14:16]
        outs = refs[16:28]
        ds_ref = refs[28]

        @pl.when(pl.program_id(0) == 0)
        def _():
            ds_ref[...] = jnp.zeros_like(ds_ref)

        chains = [(d, h) for d in range(2) for h in range(nh)]
        states = [st_in[d][h] for d, h in chains]
        _, vjp = jax.vjp(_scan_steps, states, *_scan_operands(ins, chains, slice(0, 1)))
        ct_state = [ds_ref[d * nh + h] for d, h in chains]
        ct_out = [do_in[d][:, h * hd:(h + 1) * hd] for d, h in chains]
        grads = vjp((ct_state, ct_out))
        for i, (d, h) in enumerate(chains):
            cols = slice(h * hd, (h + 1) * hd)
            du_r, dw_r, dqd_r, dkd_r, dqk_r, ddec_r = outs[6 * d:6 * d + 6]
            ds_ref[d * nh + h] = grads[0][i]
            du_r[:, cols] = grads[1][i]
            dw_r[:, cols] = grads[2][i]
            dqd_r[:, cols] = grads[3][i]
            dkd_r[:, cols] = grads[4][i]
            dqk_r[h] = grads[5][i]
            ddec_r[h:h + 1, :] = jnp.broadcast_to(grads[6][i], (1, LANES))

    st_specs = [pl.BlockSpec((None, nh, hd, hd), lambda n, f=f: (f(n), 0, 0, 0)) for f in chunk_of]
    do_specs = [pl.BlockSpec((c, gw), lambda n, f=f: (f(n), 0)) for f in chunk_of]
    out_specs, out_shape = [], []
    for f in chunk_of:
        out_specs += [pl.BlockSpec((c, gw), lambda n, f=f: (f(n), 0))] * 4
        out_specs += [pl.BlockSpec((nh, c, c), lambda n, f=f: (0, f(n), 0)),
                      pl.BlockSpec((None, nh, LANES), lambda n, f=f: (f(n), 0, 0))]
        out_shape += [jax.ShapeDtypeStruct((SEQ, gw), F32)] * 4
        out_shape += [jax.ShapeDtypeStruct((nh, SEQ, c), F32), jax.ShapeDtypeStruct((nc, nh, LANES), F32)]
    return pl.pallas_call(
        body, name="gdn_scan_bwd", grid=(nc,), in_specs=_scan_in_specs(chunk_of) + st_specs + do_specs,
        out_specs=out_specs, out_shape=out_shape,
        scratch_shapes=[pltpu.VMEM((2 * nh, hd, hd), F32)],
        compiler_params=_params(("arbitrary",)),
    )(*([u, w, qd, kd, qk, dec] * 2), st0, st1, do, do)


def _post_fn(o0, o1, z, gn):
    return _rms(o0 + o1, gn) * _silu(z)


def _gdn_post_fwd(o0, o1, p, gn):
    tm, hd, nh = _row_tile(), HEAD_DIM, GDN_HEADS
    zoff = 3 * nh
    blk = lambda i, h: (i, h)
    (ya,) = _tiled(_post_fn, "gdn_post_fwd", (SEQ // tm, nh),
                   [(o0, (tm, hd), blk), (o1, (tm, hd), blk),
                    (p, (tm, hd), lambda i, h: (i, zoff + h)), (gn, (1, hd), lambda i, h: (0, 0))],
                   [((SEQ, nh * hd), BF16, (tm, hd), blk, ())])
    return ya


def _gdn_post_bwd(o0, o1, p, gn, dya):
    tm, hd, nh = _row_tile(), HEAD_DIM, GDN_HEADS
    zoff = 3 * nh
    blk = lambda i, h: (i, h)

    def fn(a, b, z, g, dy):
        _, vjp = jax.vjp(_post_fn, a, b, z, g)
        do, _, dz, dg = vjp(dy.astype(F32))
        return do, dz, dg

    return _tiled(fn, "gdn_post_bwd", (SEQ // tm, nh),
                  [(o0, (tm, hd), blk), (o1, (tm, hd), blk),
                   (p, (tm, hd), lambda i, h: (i, zoff + h)), (gn, (1, hd), lambda i, h: (0, 0)),
                   (dya, (tm, hd), blk)],
                  [((SEQ, nh * hd), F32, (tm, hd), blk, ()),
                   ((SEQ, nh * hd), BF16, (tm, hd), blk, ()),
                   ((1, hd), F32, (1, hd), lambda i, h: (0, 0), (0, 1))],
                  sem=("arbitrary", "arbitrary"))


def _sgu_ln(v, lg, lb):
    gv = _gelu(v)
    mu = jnp.mean(gv, axis=-1, keepdims=True)
    cen = gv - mu
    var = jnp.mean(cen * cen, axis=-1, keepdims=True)
    return cen * lax.rsqrt(var + NORM_EPS) * lg + lb


def _sgu_mix(vn, w_ref, bt_ref):
    parts = []
    for g in range(SGU_GROUPS):
        cols = slice(g * SGU_DIM, (g + 1) * SGU_DIM)
        parts.append(_mm(w_ref[g], vn[:, cols]) + bt_ref[:, g:g + 1])
    return jnp.concatenate(parts, axis=1)


def _sgu_specs(p):
    sw, sb = _sw(), SGU_BLOCK
    uoff = 4 * _gw()
    assert uoff % sw == 0
    iu = uoff // sw
    return [(p, (sb, sw), lambda n: (n, iu)), (p, (sb, sw), lambda n: (n, iu + 1))]


def _sgu_fwd(p, lg, lb, w_s, b_t):
    sw, sb = _sw(), SGU_BLOCK
    ins = _sgu_specs(p)

    def body(u_ref, v_ref, lg_ref, lb_ref, w_ref, bt_ref, y_ref):
        vn = _sgu_ln(v_ref[...], lg_ref[...], lb_ref[...])
        y_ref[...] = (_gelu(u_ref[...]) * _sgu_mix(vn, w_ref, bt_ref)).astype(BF16)

    one2 = lambda n: (0, 0)
    return pl.pallas_call(
        body, name="sgu_fwd", grid=(SEQ // sb,),
        in_specs=[pl.BlockSpec(b, m) for _, b, m in ins]
        + [pl.BlockSpec((1, sw), one2), pl.BlockSpec((1, sw), one2),
           pl.BlockSpec((SGU_GROUPS, sb, sb), lambda n: (0, 0, 0)), pl.BlockSpec((sb, LANES), one2)],
        out_specs=pl.BlockSpec((sb, sw), lambda n: (n, 0)),
        out_shape=jax.ShapeDtypeStruct((SEQ, sw), BF16),
        compiler_params=_params(("parallel",)),
    )(p, p, lg, lb, w_s, b_t)


def _sgu_bwd(p, lg, lb, w_s, b_t, dy):
    sw, sb, ng = _sw(), SGU_BLOCK, SGU_GROUPS
    ins = _sgu_specs(p)

    def body(u_ref, v_ref, lg_ref, lb_ref, w_ref, bt_ref, dy_ref, du_ref, dv_ref, dlg_ref, dlb_ref, dw_ref, dbt_ref):
        first = pl.program_id(0) == 0
        u, v = u_ref[...], v_ref[...]
        gu, gelu_vjp = jax.vjp(_gelu, u)
        vn, ln_vjp = jax.vjp(_sgu_ln, v, lg_ref[...], lb_ref[...])
        s = _sgu_mix(vn, w_ref, bt_ref)
        dyv = dy_ref[...].astype(F32)
        ds = dyv * gu
        (du,) = gelu_vjp(dyv * s)
        lane = lax.broadcasted_iota(jnp.int32, (sb, LANES), 1)
        dvn_parts = []
        dbt = jnp.zeros((sb, LANES), F32)
        for g in range(ng):
            cols = slice(g * SGU_DIM, (g + 1) * SGU_DIM)
            ds_g = ds[:, cols]
            dw_g = _mm_nt(ds_g, vn[:, cols])
            dvn_parts.append(_mm_tn(w_ref[g], ds_g))
            dbt = dbt + jnp.where(lane == g, jnp.sum(ds_g, axis=1, keepdims=True), 0.0)

            @pl.when(first)
            def _(g=g, dw_g=dw_g):
                dw_ref[g] = dw_g

            @pl.when(jnp.logical_not(first))
            def _(g=g, dw_g=dw_g):
                dw_ref[g] += dw_g

        dv, dlg, dlb = ln_vjp(jnp.concatenate(dvn_parts, axis=1))
        du_ref[...] = du.astype(BF16)
        dv_ref[...] = dv.astype(BF16)

        @pl.when(first)
        def _():
            dlg_ref[...] = dlg
            dlb_ref[...] = dlb
            dbt_ref[...] = dbt

        @pl.when(jnp.logical_not(first))
        def _():
            dlg_ref[...] += dlg
            dlb_ref[...] += dlb
            dbt_ref[...] += dbt

    one2 = lambda n: (0, 0)
    row = pl.BlockSpec((sb, sw), lambda n: (n, 0))
    return pl.pallas_call(
        body, name="sgu_bwd", grid=(SEQ // sb,),
        in_specs=[pl.BlockSpec(b, m) for _, b, m in ins]
        + [pl.BlockSpec((1, sw), one2), pl.BlockSpec((1, sw), one2),
           pl.BlockSpec((ng, sb, sb), lambda n: (0, 0, 0)), pl.BlockSpec((sb, LANES), one2), row],
        out_specs=[row, row, pl.BlockSpec((1, sw), one2), pl.BlockSpec((1, sw), one2),
                   pl.BlockSpec((ng, sb, sb), lambda n: (0, 0, 0)), pl.BlockSpec((sb, LANES), one2)],
        out_shape=[jax.ShapeDtypeStruct((SEQ, sw), BF16)] * 2 + [jax.ShapeDtypeStruct((1, sw), F32)] * 2
        + [jax.ShapeDtypeStruct((ng, sb, sb), F32), jax.ShapeDtypeStruct((sb, LANES), F32)],
        compiler_params=_params(("arbitrary",)),
    )(p, p, lg, lb, w_s, b_t, dy)


def _ffn_fn(xg, xv, bg, bv, *taps):
    n = FFN_CONV
    cg = _conv_rows(xg, taps[0:n]) + bg
    cv = _conv_rows(xv, taps[n:2 * n]) + bv
    return _silu(cg) * cv


def _ffn_tile():
    return _pick(D_FF, 256)


def _ffn_specs(up, conv_w, conv_b):
    tc = _ffn_tile()
    nt = D_FF // tc
    ins = [(up, (None, SEQ, tc), (lambda j, s=s: (s, 0, j))) for s in range(2)]
    ins += [(conv_b, (1, tc), (lambda j, s=s: (0, s * nt + j))) for s in range(2)]
    ins += [(conv_w, (FFN_CONV, tc), (lambda j, s=s: (0, s * nt + j))) for s in range(2)]
    return ins


def _ffn_act_fwd(up, conv_w, conv_b):
    tc = _ffn_tile()
    ins = _ffn_specs(up, conv_w, conv_b)

    def body(xg, xv, bg, bv, wg, wv, o_ref):
        taps = [w[j:j + 1, :] for w in (wg, wv) for j in range(FFN_CONV)]
        o_ref[...] = _ffn_fn(xg[...], xv[...], bg[...], bv[...], *taps).astype(BF16)

    return pl.pallas_call(
        body, name="ffn_act_fwd", grid=(D_FF // tc,),
        in_specs=[pl.BlockSpec(b, m) for _, b, m in ins],
        out_specs=pl.BlockSpec((SEQ, tc), lambda j: (0, j)),
        out_shape=jax.ShapeDtypeStruct((SEQ, D_FF), BF16),
        compiler_params=_params(("parallel",)),
    )(*[a for a, _, _ in ins])


def _ffn_act_bwd(up, conv_w, conv_b, dact):
    tc = _ffn_tile()
    nt = D_FF // tc
    ins = _ffn_specs(up, conv_w, conv_b) + [(dact, (SEQ, tc), lambda j: (0, j))]

    def body(xg, xv, bg, bv, wg, wv, dact_ref, dup_ref, dwg_ref, dwv_ref, dbg_ref, dbv_ref):
        taps = [w[j:j + 1, :] for w in (wg, wv) for j in range(FFN_CONV)]
        _, vjp = jax.vjp(_ffn_fn, xg[...], xv[...], bg[...], bv[...], *taps)
        grads = vjp(dact_ref[...].astype(F32))
        dup_ref[0] = grads[0].astype(BF16)
        dup_ref[1] = grads[1].astype(BF16)
        dbg_ref[...] = jnp.sum(grads[2], axis=0, keepdims=True)
        dbv_ref[...] = jnp.sum(grads[3], axis=0, keepdims=True)
        for j in range(FFN_CONV):
            dwg_ref[j:j + 1, :] = jnp.sum(grads[4 + j], axis=0, keepdims=True)
            dwv_ref[j:j + 1, :] = jnp.sum(grads[4 + FFN_CONV + j], axis=0, keepdims=True)

    col = lambda j: (0, j)
    outs = pl.pallas_call(
        body, name="ffn_act_bwd", grid=(nt,),
        in_specs=[pl.BlockSpec(b, m) for _, b, m in ins],
        out_specs=[pl.BlockSpec((2, SEQ, tc), lambda j: (0, 0, j)),
                   pl.BlockSpec((FFN_CONV, tc), col), pl.BlockSpec((FFN_CONV, tc), col),
                   pl.BlockSpec((1, tc), col), pl.BlockSpec((1, tc), col)],
        out_shape=[jax.ShapeDtypeStruct((2, SEQ, D_FF), BF16),
                   jax.ShapeDtypeStruct((FFN_CONV, D_FF), F32), jax.ShapeDtypeStruct((FFN_CONV, D_FF), F32),
                   jax.ShapeDtypeStruct((1, D_FF), F32), jax.ShapeDtypeStruct((1, D_FF), F32)],
        compiler_params=_params(("parallel",)),
    )(*[a for a, _, _ in ins])
    dup, dwg, dwv, dbg, dbv = outs
    return dup, jnp.concatenate([dwg, dwv], axis=1), jnp.concatenate([dbg, dbv], axis=1)


def _pad_lanes(v):
    return jnp.pad(v.reshape(1, -1), ((0, 0), (0, LANES - v.size)))


def _layer_fwd(x, lw, late=None):
    h = _rms_fwd(x, lw['norm_mix_g'], "rms_mix_fwd")
    p = _matmul(h, lw['w_all'], 'nn', F32, "mm_in")
    if late is not None:
        lw = dict(lw, **late(p))
    gates = _gate_fwd(p, lw['a_log'], lw['dt_bias'])
    q, k, v = _qkv_fwd(p, lw['qkv_conv_w'])
    u, w, qd, kd, qk, dec = _gdn_prep_fwd(q, k, v, gates)
    o0, o1, st0, st1 = _gdn_scan_fwd(u, w, qd, kd, qk, dec)
    ya = _gdn_post_fwd(o0, o1, p, lw['gdn_norm_g'])
    yb = _sgu_fwd(p, lw['sgu_ln_g'], lw['sgu_ln_b'], lw['sgu_w'], lw['sgu_bt'])
    ta = _matmul(ya, lw['w_branch_a'], 'nn', F32, "mm_branch_a", b_split=N_CHIPS)
    tb = _matmul(yb, lw['w_branch_b'], 'nn', F32, "mm_branch_b", b_split=N_CHIPS)
    m = _merge_fwd(p, ta, tb)
    x1 = _matmul(m, lw['w_out'], 'nn', F32, "mm_out", add=x)
    h2 = _rms_fwd(x1, lw['norm_ffn_g'], "rms_ffn_fwd")
    up = _matmul(h2, lw['w_up'], 'nn', F32, "mm_up", b_split=N_CHIPS, o_split=2)
    act = _ffn_act_fwd(up, lw['ffn_conv_w'], lw['ffn_conv_b'])
    x2 = _matmul(act, lw['w_down'], 'nn', F32, "mm_down", add=x1)
    saved = dict(x=x, h=h, p=p, gates=gates, q=q, k=k, v=v, u=u, w=w, qd=qd, kd=kd, qk=qk, dec=dec,
                 o0=o0, o1=o1, st0=st0, st1=st1, ya=ya, yb=yb, ta=ta, tb=tb, m=m, x1=x1, h2=h2, up=up, act=act)
    return x2, saved, lw


def _layer_bwd(dx, dx_bf, lw, s, after=(), mid=None, last=None):
    g = {}
    dact = _matmul(dx_bf, lw['w_down'], 'nt', BF16, "mm_down_dgrad", after=after)
    g['w_down'] = _matmul(s['act'], dx_bf, 'tn', PARTIAL, "mm_down_wgrad")
    dup, g['ffn_conv_w'], g['ffn_conv_b'] = _ffn_act_bwd(s['up'], lw['ffn_conv_w'], lw['ffn_conv_b'], dact)
    dh2 = _matmul(dup, lw['w_up'], 'nt', F32, "mm_up_dgrad", a_split=2, b_split=N_CHIPS)
    g['w_up'] = _matmul(s['h2'], dup, 'tn', PARTIAL, "mm_up_wgrad", b_split=2, o_split=N_CHIPS)
    dx1, dx1_bf, g['norm_ffn_g'] = _rms_bwd(s['x1'], lw['norm_ffn_g'], dh2, dx, "rms_ffn_bwd")
    after_mid = mid(g, dh2) if mid is not None else ()
    dm = _matmul(dx1_bf, lw['w_out'], 'nt', F32, "mm_out_dgrad", after=after_mid)
    g['w_out'] = _matmul(s['m'], dx1_bf, 'tn', PARTIAL, "mm_out_wgrad")
    d_ga, d_gb, d_ta, d_tb = _merge_bwd(s['p'], s['ta'], s['tb'], dm)
    dya = _matmul(d_ta, lw['w_branch_a'], 'nt', BF16, "mm_branch_a_dgrad", b_split=N_CHIPS)
    dyb = _matmul(d_tb, lw['w_branch_b'], 'nt', BF16, "mm_branch_b_dgrad", b_split=N_CHIPS)
    g['w_branch_a'] = _matmul(s['ya'], d_ta, 'tn', PARTIAL, "mm_branch_a_wgrad", o_split=N_CHIPS)
    g['w_branch_b'] = _matmul(s['yb'], d_tb, 'tn', PARTIAL, "mm_branch_b_wgrad", o_split=N_CHIPS)
    du_s, dv_s, g['sgu_ln_g'], g['sgu_ln_b'], g['sgu_w'], g['sgu_bt'] = _sgu_bwd(
        s['p'], lw['sgu_ln_g'], lw['sgu_ln_b'], lw['sgu_w'], lw['sgu_bt'], dyb)
    do, dz, g['gdn_norm_g'] = _gdn_post_bwd(s['o0'], s['o1'], s['p'], lw['gdn_norm_g'], dya)
    cts = _gdn_scan_bwd(s['u'], s['w'], s['qd'], s['kd'], s['qk'], s['dec'], s['st0'], s['st1'], do)
    dq, dk, dv, dgates = _gdn_prep_bwd(s['q'], s['k'], s['v'], s['gates'], cts)
    (dxq, dxk, dxv), g['qkv_conv_w'] = _qkv_bwd(s['p'], lw['qkv_conv_w'], dq, dk, dv)
    d_ab, g['a_log'], g['dt_bias'] = _gate_bwd(s['p'], lw['a_log'], lw['dt_bias'], dgates)
    dp = jnp.concatenate([dxq, dxk, dxv, dz, du_s, dv_s, d_ga, d_gb, d_ab], axis=1)
    g['w_all'] = _matmul(s['h'], dp, 'tn', PARTIAL, "mm_in_wgrad")
    after_wgrads = last(g, dp) if last is not None else ()
    dh = _matmul(dp, lw['w_all'], 'nt', F32, "mm_in_dgrad", after=after_wgrads)
    dx0, dx0_bf, g['norm_mix_g'] = _rms_bwd(s['x'], lw['norm_mix_g'], dh, dx1, "rms_mix_bwd")
    return dx0, dx0_bf, g


def _w_in_segments():
    n_ab = 4 * GDN_HEADS
    cut = 4 * _gw()
    n_main = cut + 2 * _sw() + 2 * D_MODEL
    return [(0, cut, 0), (cut, cut + n_ab, n_main), (cut + n_ab, n_main + n_ab, cut)]


def _w_all_from_chips(t):
    width = t.shape[-1]
    runs = sorted(_w_in_segments(), key=lambda s: s[2])
    pieces = []
    for lo, hi, _ in runs:
        for j in range(N_CHIPS):
            a, b = max(lo, j * width), min(hi, (j + 1) * width)
            if a < b:
                pieces.append(t[j][:, a - j * width:b - j * width])
    pieces.append(jnp.zeros((t.shape[1], AB_PAD - 4 * GDN_HEADS), t.dtype))
    return jnp.concatenate(pieces, axis=1)


def _w_in_grad_chips(g_all):
    segs = _w_in_segments()
    width = segs[-1][1] // N_CHIPS
    chips = []
    for j in range(N_CHIPS):
        pieces = []
        for lo, hi, start in segs:
            a, b = max(lo, j * width), min(hi, (j + 1) * width)
            if a < b:
                pieces.append(g_all[:, start + a - lo:start + b - lo])
        chips.append(jnp.concatenate(pieces, axis=1))
    return jnp.stack(chips)


_FIRST = ('w_in', 'qkv_conv_w')
_REST = ('w_branch_a', 'w_branch_b', 'w_out', 'w_up', 'w_down', 'ffn_conv_w')


def _cat_cols(t):
    return jnp.concatenate([t[j] for j in range(N_CHIPS)], axis=-1)


def _layer_weights_rest(big):
    return dict(
        w_branch_a=big['w_branch_a'], w_branch_b=big['w_branch_b'], w_up=big['w_up'],
        w_out=big['w_out'].reshape(D_MODEL, D_MODEL), w_down=big['w_down'].reshape(D_FF, D_MODEL),
        ffn_conv_w=_cat_cols(big['ffn_conv_w']))


def _layer_weights_first(l, big, small):
    return dict(
        w_all=_w_all_from_chips(big['w_in']), qkv_conv_w=_cat_cols(big['qkv_conv_w']),
        norm_mix_g=small['norm_mix_g'][l:l + 1], norm_ffn_g=small['norm_ffn_g'][l:l + 1],
        a_log=_pad_lanes(small['a_log'][l]), dt_bias=_pad_lanes(small['dt_bias'][l]),
        gdn_norm_g=small['gdn_norm_g'][l:l + 1],
        sgu_ln_g=small['sgu_ln_g'][l:l + 1], sgu_ln_b=small['sgu_ln_b'][l:l + 1], sgu_w=small['sgu_w'][l],
        sgu_bt=jnp.pad(small['sgu_b'][l].T, ((0, 0), (0, LANES - SGU_GROUPS))),
        ffn_conv_b=small['ffn_conv_b'][l:l + 1])


_SMALL_GRADS = ('norm_mix_g', 'a_log', 'dt_bias', 'gdn_norm_g', 'sgu_ln_g', 'sgu_ln_b', 'sgu_w', 'sgu_bt',
                'norm_ffn_g', 'ffn_conv_b', 'qkv_conv_w', 'ffn_conv_w')
_BIG = ('w_in', 'w_branch_a', 'w_branch_b', 'w_out', 'w_up', 'w_down')


def _big_grad_slices(g, names=_BIG):
    def one(k):
        if k == 'w_in':
            return _w_in_grad_chips(g['w_all'])
        if k == 'w_out':
            return g[k].reshape(N_CHIPS, D_MODEL // N_CHIPS, D_MODEL)
        if k == 'w_down':
            return g[k].reshape(N_CHIPS, D_FF // N_CHIPS, D_MODEL)
        return g[k]
    return {k: one(k) for k in names}


def _adamw(w, g, m, v, name, after=(), pass_g=False):
    shape = w.shape
    cols = shape[-1]
    rows = w.size // cols
    tr = _pick(rows, max(8, (1 << 18) // cols // 8 * 8), 8) if rows % 8 == 0 else rows

    def fn(wv, gv, mv, vv):
        m2 = ADAM_B1 * mv + (1.0 - ADAM_B1) * gv
        v2 = ADAM_B2 * vv + (1.0 - ADAM_B2) * (gv * gv)
        m_hat = m2 / (1.0 - ADAM_B1 ** ADAM_STEP)
        v_hat = v2 / (1.0 - ADAM_B2 ** ADAM_STEP)
        delta = -ADAM_LR * (m_hat / (jnp.sqrt(v_hat) + ADAM_EPS) + ADAM_WD * wv)
        return (delta, m2, v2, gv) if pass_g else (delta, m2, v2)

    row = lambda i: (i, 0)
    outs = _tiled(fn, name, (rows // tr,),
                  [(t.reshape(rows, cols), (tr, cols), row) for t in (w, g, m, v)],
                  [((rows, cols), F32, (tr, cols), row, ())] * (4 if pass_g else 3), sem=("parallel",),
                  after=after)
    return [o.reshape(shape) for o in outs]


MESH_IDS = pl.DeviceIdType.MESH
ANY = pl.BlockSpec(memory_space=pl.ANY)


def _place():
    x, y, c = lax.axis_index("x"), lax.axis_index("y"), lax.axis_index("c")
    chips = [(1 - x, y), (x, 1 - y), (1 - x, 1 - y)]
    return x, y, c, 2 * x + y, chips


def _chip_index():
    return 2 * lax.axis_index("x") + lax.axis_index("y")


HBM = pl.BlockSpec(memory_space=pltpu.HBM)
SEM = pl.BlockSpec(memory_space=pltpu.SEMAPHORE)
DATAFLOW = pltpu.SideEffectType.DATAFLOW_SIDE_EFFECTING
TOKEN = jax.ShapeDtypeStruct((8, LANES), F32)


def _in_hbm(t):
    return pltpu.with_memory_space_constraint(t, pltpu.HBM)


def _place_shard(w, l, dtype, name, after=()):
    _, r, cols = w.shape
    tr = _pick(r, max(16, (1 << 18) // cols // 16 * 16), 16) if r % 16 == 0 else r

    def body(w_ref, *rest):
        rest[-1][...] = w_ref[...].astype(dtype)

    return pl.pallas_call(
        body, name=name, grid=(r // tr,),
        in_specs=[pl.BlockSpec((None, tr, cols), lambda i: (l, i, 0))] + [ANY] * len(after),
        out_specs=pl.BlockSpec((None, tr, cols), lambda i: (_chip_index(), i, 0)),
        out_shape=jax.ShapeDtypeStruct((N_CHIPS, r, cols), dtype),
        compiler_params=_params(("parallel",)),
    )(w, *after)


def _my_rows(ref_or_shape_rows, c):
    r = ref_or_shape_rows
    if r % 32 == 0:
        return pl.ds(c * (r // 2), r // 2), pl.ds((1 - c) * (r // 2), r // 2), True
    return pl.ds(0, r), pl.ds(0, r), False


def _gather_copies(bufs, send_sems, recv_sems):
    x, y, c, me, chips = _place()
    out = []
    for i, buf in enumerate(bufs):
        mine, _, _ = _my_rows(buf.shape[1], c)
        for j, (cx, cy) in enumerate(chips):
            def rcopy(slab, i=i, j=j, cx=cx, cy=cy):
                return pltpu.make_async_remote_copy(src_ref=slab, dst_ref=slab, send_sem=send_sems.at[3 * i + j],
                                                    recv_sem=recv_sems.at[3 * i + j], device_id=(cx, cy, c),
                                                    device_id_type=MESH_IDS)
            out.append((rcopy(buf.at[me, mine]), rcopy(buf.at[2 * cx + cy, mine])))
    return out


def _gather_start(xs, name, after=()):
    n = len(xs)
    n_after = len(after)

    def body(*refs):
        refs = refs[n + n_after:]
        send_sems, recv_sems = refs[0], refs[1]
        bufs = refs[2:n + 2]
        token = refs[n + 2]
        for out_going, _ in _gather_copies(bufs, send_sems, recv_sems):
            out_going.start()
        token[...] = jnp.zeros_like(token)

    res = pl.pallas_call(
        body, name=name, in_specs=[HBM] * n + [ANY] * n_after,
        out_specs=[SEM, SEM] + [HBM] * n + [pl.BlockSpec(memory_space=pltpu.VMEM)],
        out_shape=[pltpu.SemaphoreType.DMA((3 * n,)), pltpu.SemaphoreType.DMA((3 * n,))]
        + [pltpu.HBM(t.shape, t.dtype) for t in xs] + [TOKEN],
        input_output_aliases={i: i + 2 for i in range(n)},
        compiler_params=pltpu.CompilerParams(has_side_effects=DATAFLOW),
    )(*[_in_hbm(t) for t in xs], *after)
    return res[0], res[1], res[2:2 + n], res[2 + n]


def _gather_wait(bufs, send_sems, recv_sems, after, name):
    n = len(bufs)

    def body(*refs):
        b_refs = refs[:n]
        s_sems, r_sems = refs[n], refs[n + 1]
        for out_going, in_coming in _gather_copies(b_refs, s_sems, r_sems):
            out_going.wait_send()
            in_coming.wait_recv()

    return pl.pallas_call(
        body, name=name, in_specs=[HBM] * n + [SEM, SEM, ANY], out_specs=[HBM] * n,
        out_shape=[pltpu.HBM(t.shape, t.dtype) for t in bufs],
        input_output_aliases={i: i for i in range(n)},
        compiler_params=pltpu.CompilerParams(has_side_effects=DATAFLOW),
    )(*bufs, send_sems, recv_sems, after)


def _gather_forward(bufs, name):
    idx = [i for i, t in enumerate(bufs) if t.shape[1] % 32 == 0]
    xs = [bufs[i] for i in idx]
    n = len(xs)

    def body(*refs):
        o_refs = refs[n:2 * n]
        send_sems, recv_sems = refs[2 * n:]
        x, y, c, _, chips = _place()
        copies = []
        for i in range(n):
            mine, theirs, _ = _my_rows(xs[i].shape[1], c)
            for j, (cx, cy) in enumerate(chips):
                def rcopy(slab, i=i, j=j):
                    return pltpu.make_async_remote_copy(src_ref=slab, dst_ref=slab, send_sem=send_sems.at[3 * i + j],
                                                        recv_sem=recv_sems.at[3 * i + j], device_id=(x, y, 1 - c),
                                                        device_id_type=MESH_IDS)
                copies.append((rcopy(o_refs[i].at[2 * cx + cy, mine]), rcopy(o_refs[i].at[2 * cx + cy, theirs])))
                copies[-1][0].start()
        for out_going, in_coming in copies:
            out_going.wait_send()
            in_coming.wait_recv()

    res = pl.pallas_call(
        body, name=name, in_specs=[ANY] * n, out_specs=[ANY] * n,
        out_shape=[jax.ShapeDtypeStruct(t.shape, t.dtype) for t in xs],
        input_output_aliases={i: i for i in range(n)},
        scratch_shapes=[pltpu.SemaphoreType.DMA((3 * n,)), pltpu.SemaphoreType.DMA((3 * n,))],
        compiler_params=pltpu.CompilerParams(has_side_effects=True),
    )(*xs)
    out = list(bufs)
    for i, t in zip(idx, res):
        out[i] = t
    return out


def _half_tile(rh, cols):
    return _pick(rh, max(16, (1 << 18) // cols // 16 * 16), 16)


N_LAND = 7


def _scatter_copies(p_refs, r_refs, send_sems, recv_sems):
    x, y, c, me, chips = _place()
    outgoing, incoming = [], []
    for i, (p, land) in enumerate(zip(p_refs, r_refs)):
        rh = p.shape[1] // 2

        def copy(src, slot, k_send, k_recv, to, i=i, land=land):
            return pltpu.make_async_remote_copy(
                src_ref=src, dst_ref=land.at[slot], send_sem=send_sems.at[N_LAND * i + k_send],
                recv_sem=recv_sems.at[N_LAND * i + k_recv], device_id=to, device_id_type=MESH_IDS)

        for r, (cx, cy) in enumerate(chips):
            for k in range(2):
                outgoing.append(copy(p.at[2 * cx + cy, pl.ds(k * rh, rh)], 2 * r + c, 2 * r + k, 2 * r + c,
                                     (cx, cy, k)))
                incoming.append(copy(land.at[2 * r + k], 2 * r + k, 2 * r + k, 2 * r + k, (x, y, c)))
        outgoing.append(copy(p.at[me, pl.ds((1 - c) * rh, rh)], 6, 6, 6, (x, y, 1 - c)))
        incoming.append(copy(land.at[6], 6, 6, 6, (x, y, c)))
    return outgoing, incoming


def _scatter_start(ps, name):
    n = len(ps)
    lands = [lax.empty((N_LAND, t.shape[1] // 2, t.shape[2]), t.dtype) for t in ps]

    def body(*refs):
        send_sems, recv_sems = refs[2 * n], refs[2 * n + 1]
        p_refs = refs[2 * n + 2:3 * n + 2]
        r_refs = refs[3 * n + 2:4 * n + 2]
        token = refs[4 * n + 2]
        for cp in _scatter_copies(p_refs, r_refs, send_sems, recv_sems)[0]:
            cp.start()
        token[...] = jnp.zeros_like(token)

    res = pl.pallas_call(
        body, name=name, in_specs=[HBM] * (2 * n),
        out_specs=[SEM, SEM] + [HBM] * (2 * n) + [pl.BlockSpec(memory_space=pltpu.VMEM)],
        out_shape=[pltpu.SemaphoreType.DMA((N_LAND * n,)), pltpu.SemaphoreType.DMA((N_LAND * n,))]
        + [pltpu.HBM(t.shape, t.dtype) for t in list(ps) + lands] + [TOKEN],
        input_output_aliases={i: i + 2 for i in range(2 * n)},
        compiler_params=pltpu.CompilerParams(has_side_effects=DATAFLOW),
    )(*[_in_hbm(t) for t in list(ps) + lands])
    return res[0], res[1], res[2:2 + n], res[2 + n:2 + 2 * n], res[2 + 2 * n]


def _scatter_wait(ps, lands, send_sems, recv_sems, after, name):
    n = len(ps)
    after = tuple(after) if isinstance(after, (tuple, list)) else (after,)

    def body(*refs):
        p_refs, r_refs = refs[:n], refs[n:2 * n]
        s_sems, r_sems = refs[2 * n], refs[2 * n + 1]
        outgoing, incoming = _scatter_copies(p_refs, r_refs, s_sems, r_sems)
        for cp in outgoing:
            cp.wait_send()
        for cp in incoming:
            cp.wait_recv()

    res = pl.pallas_call(
        body, name=name, in_specs=[HBM] * (2 * n) + [SEM, SEM] + [ANY] * len(after), out_specs=[HBM] * (2 * n),
        out_shape=[pltpu.HBM(t.shape, t.dtype) for t in list(ps) + list(lands)],
        input_output_aliases={i: i for i in range(2 * n)},
        compiler_params=pltpu.CompilerParams(has_side_effects=DATAFLOW),
    )(*ps, *lands, send_sems, recv_sems, *after)
    return res[:n], res[n:]


def _reduce_own(p, rcv, acc, l, name):
    nchip, r, cols = p.shape
    rh = r // 2
    tr = _half_tile(rh, cols)
    nt = rh // tr

    def body(*refs):
        p_ref, r_ref = refs[:2]
        o_ref = refs[-1]
        total = p_ref[...].astype(F32)
        for j in range(N_LAND):
            total = total + r_ref[j].astype(F32)
        o_ref[...] = total

    mine = lambda i: lax.axis_index("c") * nt + i
    in_specs = [pl.BlockSpec((None, tr, cols), lambda i: (_chip_index(), mine(i), 0)),
                pl.BlockSpec((N_LAND, tr, cols), lambda i: (0, i, 0))]
    args = [p, rcv]
    aliases = {}
    if acc is not None:
        in_specs.append(ANY)
        args.append(acc)
        aliases = {2: 0}
    return pl.pallas_call(
        body, name=name, grid=(nt,), in_specs=in_specs,
        out_specs=pl.BlockSpec((None, tr, cols), lambda i: (l, mine(i), 0)),
        out_shape=jax.ShapeDtypeStruct((DEPTH, r, cols), F32), input_output_aliases=aliases,
        compiler_params=_params(("parallel",)),
    )(*args)


def _share_halves(fs, l):
    n = len(fs)

    def body(*refs):
        o_refs = refs[n:2 * n]
        send_sems, recv_sems = refs[2 * n:]
        x, y, c, _, _ = _place()

        def halves(i):
            rh = fs[i].shape[1] // 2
            return o_refs[i].at[l, pl.ds(c * rh, rh), :], o_refs[i].at[l, pl.ds((1 - c) * rh, rh), :]

        def copy(i, rows):
            return pltpu.make_async_remote_copy(src_ref=rows, dst_ref=rows, send_sem=send_sems.at[i],
                                                recv_sem=recv_sems.at[i], device_id=(x, y, 1 - c),
                                                device_id_type=MESH_IDS)

        for i in range(n):
            copy(i, halves(i)[0]).start()
        for i in range(n):
            mine, theirs = halves(i)
            copy(i, mine).wait_send()
            copy(i, theirs).wait_recv()

    return pl.pallas_call(
        body, name="grad_share_halves", in_specs=[ANY] * n, out_specs=[ANY] * n,
        out_shape=[jax.ShapeDtypeStruct(t.shape, t.dtype) for t in fs],
        input_output_aliases={i: i for i in range(n)},
        scratch_shapes=[pltpu.SemaphoreType.DMA((n,)), pltpu.SemaphoreType.DMA((n,))],
        compiler_params=pltpu.CompilerParams(has_side_effects=True),
    )(*fs)


def _reduce_scatter_begin(slices, l, tag):
    names = list(slices)
    send_sems, recv_sems, ps, lands, token = _scatter_start([slices[k] for k in names],
                                                            "grad_scatter_start_%s%d" % (tag, l))
    return dict(names=names, ps=ps, lands=lands, sems=(send_sems, recv_sems), token=token, l=l, tag=tag)


def _reduce_scatter_end(st, stacked, after):
    l = st['l']
    ps, rs = _scatter_wait(st['ps'], st['lands'], *st['sems'], after, "grad_scatter_wait_%s%d" % (st['tag'], l))
    fs = [_reduce_own(p, r, stacked.get(k), l, "grad_reduce_own_" + k)
          for k, p, r in zip(st['names'], ps, rs)]
    stacked.update(zip(st['names'], _share_halves(fs, l)))


def _all_reduce_small(packed, after=()):
    rows = packed.shape[0]
    rh = rows // 2
    tr = _pick(rh, 512, 8)

    def body(x_ref, *rest):
        o_ref, sib, chip_sums, send_sems, recv_sems = rest[len(after):]
        x, y, c, me, chips = _place()
        sibling = (x, y, 1 - c)
        mine, theirs = pl.ds(c * rh, rh), pl.ds((1 - c) * rh, rh)

        def copy(src, dst, k, to):
            return pltpu.make_async_remote_copy(src_ref=src, dst_ref=dst, send_sem=send_sems.at[k],
                                                recv_sem=recv_sems.at[k], device_id=to, device_id_type=MESH_IDS)

        def tiles(fn):
            @pl.loop(0, rh // tr)
            def _(t):
                fn(pl.ds(pl.multiple_of(t * tr, 8), tr))

        to_sibling = copy(x_ref.at[theirs], sib, 0, sibling)
        to_sibling.start()
        to_sibling.wait()

        def chip_sum(sl):
            chip_sums[me, sl, :] = x_ref.at[mine][sl, :] + sib[sl, :]
        tiles(chip_sum)

        out = [copy(chip_sums.at[me], chip_sums.at[me], 1 + j, (cx, cy, c)) for j, (cx, cy) in enumerate(chips)]
        for cp in out:
            cp.start()
        for j, (cx, cy) in enumerate(chips):
            copy(chip_sums.at[2 * cx + cy], chip_sums.at[2 * cx + cy], 1 + j, (x, y, c)).wait_recv()
        for cp in out:
            cp.wait_send()

        def total(sl):
            acc = chip_sums[0, sl, :]
            for s in range(1, N_CHIPS):
                acc = acc + chip_sums[s, sl, :]
            o_ref.at[mine][sl, :] = acc
        tiles(total)

        share = copy(o_ref.at[mine], o_ref.at[mine], 4, sibling)
        share.start()
        share.wait_send()
        copy(o_ref.at[theirs], o_ref.at[theirs], 4, sibling).wait_recv()

    vm = pl.BlockSpec(memory_space=pltpu.VMEM)
    return pl.pallas_call(
        body, name="all_reduce_small", in_specs=[vm] + [ANY] * len(after), out_specs=vm,
        out_shape=jax.ShapeDtypeStruct(packed.shape, F32),
        scratch_shapes=[pltpu.VMEM((rh, LANES), F32), pltpu.VMEM((N_CHIPS, rh, LANES), F32),
                        pltpu.SemaphoreType.DMA((5,)), pltpu.SemaphoreType.DMA((5,))],
        compiler_params=pltpu.CompilerParams(vmem_limit_bytes=VMEM_LIMIT, has_side_effects=True),
    )(packed, *after)


_WEIGHTS = ('norm_mix_g', 'w_in', 'qkv_conv_w', 'a_log', 'dt_bias', 'gdn_norm_g', 'w_branch_a', 'sgu_ln_g',
            'sgu_ln_b', 'sgu_w', 'sgu_b', 'w_branch_b', 'w_out', 'norm_ffn_g', 'w_up', 'ffn_conv_w', 'ffn_conv_b',
            'w_down', 'final_norm_g')


def _local_step(x, target, weights_of, final_norm_g, bwd_after=None, mid_layer=None, last_layer=None):
    lws, saves = [], []
    for l in range(DEPTH):
        lw, late = weights_of(l, x)
        x, s, lw = _layer_fwd(x, lw, late)
        lws.append(lw)
        saves.append(s)
    loss, dx, dx_bf, d_final = _loss_head(x, final_norm_g.reshape(1, -1), target)
    grads = [None] * DEPTH
    for l in reversed(range(DEPTH)):
        after = bwd_after(l) if bwd_after is not None else ()
        mid = (lambda g, dh2, l=l: mid_layer(l, g, dh2)) if mid_layer is not None else None
        last = (lambda g, dp, l=l: last_layer(l, g, dp)) if last_layer is not None else None
        dx, dx_bf, grads[l] = _layer_bwd(dx, dx_bf, lws[l], saves[l], after, mid, last)
    return loss, dx, grads, d_final


def _pack_small(grads, d_final):
    parts = [grads[l][k].reshape(-1) for l in range(DEPTH) for k in _SMALL_GRADS] + [d_final.reshape(-1)]
    flat = jnp.concatenate(parts)
    rows = -(-flat.size // (16 * LANES)) * 16
    return jnp.pad(flat, (0, rows * LANES - flat.size)).reshape(rows, LANES), [p.size for p in parts]


def _unpack_small(packed, grads, d_final):
    flat = packed.reshape(-1)
    out, off = [], 0
    for l in range(DEPTH):
        d = {}
        for k in _SMALL_GRADS:
            t = grads[l][k]
            d[k] = flat[off:off + t.size].reshape(t.shape)
            off += t.size
        out.append(d)
    return out, flat[off:off + d_final.size].reshape(d_final.shape)


def kernel(x, norm_mix_g, w_in, qkv_conv_w, a_log, dt_bias, gdn_norm_g, w_branch_a, sgu_ln_g, sgu_ln_b, sgu_w, sgu_b, w_branch_b, w_out, norm_ffn_g, w_up, ffn_conv_w, ffn_conv_b, w_down, final_norm_g, loss_target, m_norm_mix_g, m_w_in, m_qkv_conv_w, m_a_log, m_dt_bias, m_gdn_norm_g, m_w_branch_a, m_sgu_ln_g, m_sgu_ln_b, m_sgu_w, m_sgu_b, m_w_branch_b, m_w_out, m_norm_ffn_g, m_w_up, m_ffn_conv_w, m_ffn_conv_b, m_w_down, m_final_norm_g, v_norm_mix_g, v_w_in, v_qkv_conv_w, v_a_log, v_dt_bias, v_gdn_norm_g, v_w_branch_a, v_sgu_ln_g, v_sgu_ln_b, v_sgu_w, v_sgu_b, v_w_branch_b, v_w_out, v_norm_ffn_g, v_w_up, v_ffn_conv_w, v_ffn_conv_b, v_w_down, v_final_norm_g):
    w = dict(norm_mix_g=norm_mix_g, w_in=w_in, qkv_conv_w=qkv_conv_w, a_log=a_log, dt_bias=dt_bias,
             gdn_norm_g=gdn_norm_g, w_branch_a=w_branch_a, sgu_ln_g=sgu_ln_g, sgu_ln_b=sgu_ln_b, sgu_w=sgu_w,
             sgu_b=sgu_b, w_branch_b=w_branch_b, w_out=w_out, norm_ffn_g=norm_ffn_g, w_up=w_up,
             ffn_conv_w=ffn_conv_w, ffn_conv_b=ffn_conv_b, w_down=w_down, final_norm_g=final_norm_g)
    m = dict(norm_mix_g=m_norm_mix_g, w_in=m_w_in, qkv_conv_w=m_qkv_conv_w, a_log=m_a_log, dt_bias=m_dt_bias,
             gdn_norm_g=m_gdn_norm_g, w_branch_a=m_w_branch_a, sgu_ln_g=m_sgu_ln_g, sgu_ln_b=m_sgu_ln_b,
             sgu_w=m_sgu_w, sgu_b=m_sgu_b, w_branch_b=m_w_branch_b, w_out=m_w_out, norm_ffn_g=m_norm_ffn_g,
             w_up=m_w_up, ffn_conv_w=m_ffn_conv_w, ffn_conv_b=m_ffn_conv_b, w_down=m_w_down,
             final_norm_g=m_final_norm_g)
    v = dict(norm_mix_g=v_norm_mix_g, w_in=v_w_in, qkv_conv_w=v_qkv_conv_w, a_log=v_a_log, dt_bias=v_dt_bias,
             gdn_norm_g=v_gdn_norm_g, w_branch_a=v_w_branch_a, sgu_ln_g=v_sgu_ln_g, sgu_ln_b=v_sgu_ln_b,
             sgu_w=v_sgu_w, sgu_b=v_sgu_b, w_branch_b=v_w_branch_b, w_out=v_w_out, norm_ffn_g=v_norm_ffn_g,
             w_up=v_w_up, ffn_conv_w=v_ffn_conv_w, ffn_conv_b=v_ffn_conv_b, w_down=v_w_down,
             final_norm_g=v_final_norm_g)
    chip = _chip_index()

    in_flight = []
    for l in range(DEPTH):
        for tag, names in (("first", _FIRST), ("rest", _REST)):
            earlier = (in_flight[-1][3],) if in_flight else ()
            placed = [_place_shard(w[k], l, BF16 if k in _BIG else F32, "place_" + k, earlier) for k in names]
            in_flight.append(_gather_start(placed, "gather_start_%s_%d" % (tag, l), earlier))

    def arrived(l, tag, names, after):
        send_sems, recv_sems, bufs, _ = in_flight[2 * l + (tag == "rest")]
        bufs = _gather_wait(bufs, send_sems, recv_sems, after, "gather_wait_%s_%d" % (tag, l))
        return dict(zip(names, _gather_forward(bufs, "gather_forward_" + tag)))

    def weights_of(l, x_in):
        first = arrived(l, "first", _FIRST, in_flight[-1][3] if l == 0 else x_in)
        return (_layer_weights_first(l, first, w),
                lambda p: _layer_weights_rest(arrived(l, "rest", _REST, p)))

    group_a, group_b = ('w_up', 'w_down'), ('w_in', 'w_branch_a', 'w_branch_b', 'w_out')
    big_g = {}
    pending = {'a': None, 'b': None}

    def finish(tag, after):
        if pending[tag] is not None:
            _reduce_scatter_end(pending[tag], big_g, after)
            pending[tag] = None

    def bwd_after(l):
        return () if pending['b'] is None else (pending['b']['token'],)

    def mid_layer(l, g, dh2):
        finish('b', dh2)
        pending['a'] = _reduce_scatter_begin(_big_grad_slices(g, group_a), l, 'a')
        return (pending['a']['token'],)

    def last_layer(l, g, dp):
        finish('a', dp)
        pending['b'] = _reduce_scatter_begin(_big_grad_slices(g, group_b), l, 'b')
        return (pending['b']['token'],)

    loss, grad_x, grads, d_final = _local_step(x[0], loss_target[0], weights_of, w['final_norm_g'], bwd_after,
                                               mid_layer, last_layer)
    loss = lax.psum(loss[0, 0], ("x", "y", "c"))
    packed, _ = _pack_small(grads, d_final)
    reduced_small = _all_reduce_small(packed, (pending['b']['token'],))
    small_g, d_final = _unpack_small(reduced_small, grads, d_final)
    deltas, new_m, new_v, g_out = {}, {}, {}, {}
    for k in group_a:
        deltas[k], new_m[k], new_v[k], g_out[k] = _adamw(w[k], big_g[k].reshape(w[k].shape), m[k], v[k],
                                                         "adamw_" + k, (pending['b']['token'],), pass_g=True)
    finish('b', (reduced_small,) + tuple(deltas[k] for k in group_a))

    def stack(k):
        return jnp.stack([small_g[l][k] for l in range(DEPTH)])

    nd = 2 * GDN_HEADS
    for k in ('norm_mix_g', 'gdn_norm_g', 'sgu_ln_g', 'sgu_ln_b', 'norm_ffn_g', 'ffn_conv_b'):
        g_out[k] = stack(k).reshape(w[k].shape)
    g_out['sgu_w'] = stack('sgu_w')
    g_out['a_log'] = stack('a_log')[:, 0, :nd].reshape(w['a_log'].shape)
    g_out['dt_bias'] = stack('dt_bias')[:, 0, :nd].reshape(w['dt_bias'].shape)
    g_out['sgu_b'] = jnp.swapaxes(stack('sgu_bt')[:, :, :SGU_GROUPS], 1, 2)
    for k in ('qkv_conv_w', 'ffn_conv_w'):
        full = stack(k)
        width = w[k].shape[-1]
        g_out[k] = lax.dynamic_slice_in_dim(full, chip * width, width, axis=2)
    g_out['final_norm_g'] = d_final.reshape(w['final_norm_g'].shape)

    for k in group_b:
        deltas[k], new_m[k], new_v[k], g_out[k] = _adamw(w[k], big_g[k].reshape(w[k].shape), m[k], v[k],
                                                         "adamw_" + k, pass_g=True)
    for k in _WEIGHTS:
        if k not in _BIG:
            deltas[k], new_m[k], new_v[k] = _adamw(w[k], g_out[k], m[k], v[k], "adamw_" + k)
    return (loss, grad_x[None], *[g_out[k] for k in _WEIGHTS], *[deltas[k] for k in _WEIGHTS],
            *[new_m[k] for k in _WEIGHTS], *[new_v[k] for k in _WEIGHTS])
```

```python
import functools
import math

import jax
import jax.numpy as jnp
from jax import lax
from jax.experimental import pallas as pl
from jax.experimental.pallas import tpu as pltpu

F32 = jnp.float32
BF16 = jnp.bfloat16
PARTIAL = BF16

D_MODEL = 2048
SEQ = 2048
DEPTH = 4
GDN_HEADS = 8
HEAD_DIM = 128
GDN_CHUNK = 64
QKV_CONV = 5
SGU_GROUPS = 8
SGU_DIM = 128
SGU_BLOCK = 128
D_FF = 5632
FFN_CONV = 3
NORM_EPS = 1e-6
N_CHIPS = 4

ADAM_LR = 0.001
ADAM_B1 = 0.9
ADAM_B2 = 0.999
ADAM_EPS = 1e-08
ADAM_WD = 0.01
ADAM_STEP = 10

LANES = 128
VMEM_LIMIT = 56 * 1024 * 1024
MATMUL_TILE_BYTES = 46 * 1024 * 1024
AB_PAD = LANES


def _gw():
    return GDN_HEADS * HEAD_DIM


def _sw():
    return SGU_GROUPS * SGU_DIM


def _n_all():
    return 4 * _gw() + 2 * _sw() + 2 * D_MODEL + AB_PAD


def _pick(n, target, align=LANES):
    if n <= target:
        return n
    best = None
    t = align
    while t <= target:
        if n % t == 0:
            best = t
        t += align
    assert best is not None, (n, target, align)
    return best


def _params(sem=None):
    return pltpu.CompilerParams(dimension_semantics=sem, vmem_limit_bytes=VMEM_LIMIT)


def _dot(a, b, dims, hi):
    dn = (dims, ((), ()))
    if not hi:
        return lax.dot_general(a.astype(BF16), b.astype(BF16), dn, preferred_element_type=F32)
    a_hi, b_hi = a.astype(BF16), b.astype(BF16)
    a_lo = (a - a_hi.astype(F32)).astype(BF16)
    b_lo = (b - b_hi.astype(F32)).astype(BF16)
    d = lambda p, q: lax.dot_general(p, q, dn, preferred_element_type=F32)
    return d(a_hi, b_hi) + (d(a_hi, b_lo) + d(a_lo, b_hi))


def _make_mm(hi):
    @jax.custom_vjp
    def mm(a, b):
        return _dot(a, b, ((1,), (0,)), hi)

    @jax.custom_vjp
    def mm_nt(a, b):
        return _dot(a, b, ((1,), (1,)), hi)

    @jax.custom_vjp
    def mm_tn(a, b):
        return _dot(a, b, ((0,), (0,)), hi)

    mm.defvjp(lambda a, b: (mm(a, b), (a, b)), lambda r, g: (mm_nt(g, r[1]), mm_tn(r[0], g)))
    mm_nt.defvjp(lambda a, b: (mm_nt(a, b), (a, b)), lambda r, g: (mm(g, r[1]), mm_tn(g, r[0])))
    mm_tn.defvjp(lambda a, b: (mm_tn(a, b), (a, b)), lambda r, g: (mm_nt(r[1], g), mm(r[0], g)))
    return mm, mm_nt, mm_tn


_mm, _mm_nt, _mm_tn = _make_mm(False)
_mmh, _mmh_nt, _mmh_tn = _make_mm(True)


def _shift_rows_raw(x, s):
    if s == 0:
        return x
    n = x.shape[0]
    rolled = pltpu.roll(x, (-s) % n, 0)
    t = lax.broadcasted_iota(jnp.int32, x.shape, 0)
    ok = (t + s >= 0) & (t + s < n)
    return jnp.where(ok, rolled, 0.0)


@functools.partial(jax.custom_vjp, nondiff_argnums=(1,))
def _shift_rows(x, s):
    return _shift_rows_raw(x, s)


_shift_rows.defvjp(lambda x, s: (_shift_rows_raw(x, s), None), lambda s, _, g: (_shift_rows_raw(g, -s),))


def _sigmoid(x):
    return 1.0 / (1.0 + jnp.exp(-x))


def _silu(x):
    return x * _sigmoid(x)


def _gelu(x):
    return 0.5 * x * (1.0 + jnp.tanh(math.sqrt(2.0 / math.pi) * (x + 0.044715 * x * x * x)))


def _softplus(x):
    return jnp.maximum(x, 0.0) + jnp.log(1.0 + jnp.exp(-jnp.abs(x)))


def _rms(x, g):
    return x * lax.rsqrt(jnp.mean(x * x, axis=-1, keepdims=True) + NORM_EPS) * g


def _conv_rows(x, taps):
    pad = len(taps) // 2
    acc = None
    for j, w in enumerate(taps):
        term = _shift_rows(x, j - pad) * w
        acc = term if acc is None else acc + term
    return acc


@jax.custom_vjp
def _inv_unit(mats):
    n = mats[0].shape[0]
    eye = (lax.broadcasted_iota(jnp.int32, (n, n), 0) == lax.broadcasted_iota(jnp.int32, (n, n), 1)).astype(F32)
    ps = [eye - a for a in mats]
    aks = list(mats)
    for _ in range(int(math.log2(n)) - 1):
        aks = [_mmh(ak, ak) for ak in aks]
        ps = [p + _mmh(p, ak) for p, ak in zip(ps, aks)]
    return ps


def _inv_unit_fwd(mats):
    ts = _inv_unit(mats)
    return ts, ts


def _inv_unit_bwd(ts, gs):
    inner = [_mmh_nt(g, t) for g, t in zip(gs, ts)]
    return ([-_mmh_tn(t, m) for t, m in zip(ts, inner)],)


_inv_unit.defvjp(_inv_unit_fwd, _inv_unit_bwd)


def _tiled(fn, name, grid, ins, outs, sem=None, after=()):
    n_in = len(ins)

    def body(*refs):
        vals = [r[...] for r in refs[:n_in]]
        res = fn(*vals)
        if not isinstance(res, (tuple, list)):
            res = (res,)
        for o_ref, r, spec in zip(refs[n_in + len(after):], res, outs):
            acc = spec[4]
            if not acc:
                o_ref[...] = r.astype(o_ref.dtype)
            else:
                first = functools.reduce(jnp.logical_and, [pl.program_id(a) == 0 for a in acc])

                @pl.when(first)
                def _(o_ref=o_ref, r=r):
                    o_ref[...] = r.astype(o_ref.dtype)

                @pl.when(jnp.logical_not(first))
                def _(o_ref=o_ref, r=r):
                    o_ref[...] += r.astype(o_ref.dtype)

    return pl.pallas_call(
        body, name=name, grid=grid,
        in_specs=[pl.BlockSpec(b, m) for _, b, m in ins] + [pl.BlockSpec(memory_space=pl.ANY)] * len(after),
        out_specs=[pl.BlockSpec(s[2], s[3]) for s in outs],
        out_shape=[jax.ShapeDtypeStruct(s[0], s[1]) for s in outs],
        compiler_params=_params(sem),
    )(*[a for a, _, _ in ins], *after)


def _matmul(a, b, mode, out_dtype, name, add=None, a_split=1, b_split=1, o_split=1, after=()):
    def dims2(x, split):
        return (x.shape[-2], x.shape[-1] * split)

    ar, ac = dims2(a, a_split)
    br, bc = dims2(b, b_split)
    if mode == 'nn':
        M, K, N = ar, ac, bc
        assert br == K
    elif mode == 'nt':
        M, K, N = ar, ac, br
        assert bc == K
    else:
        K, M, N = ar, ac, bc
        assert br == K
    tn = _pick(N // max(o_split, b_split if mode != 'nt' else 1), 1408)
    tk = _pick(K // max(a_split if mode != 'tn' else 1, b_split if mode == 'nt' else 1),
               {'nt': 3584, 'nn': 2816, 'tn': 2048}[mode])

    def tile_bytes(rows):
        out_b = rows * tn * (jnp.dtype(out_dtype).itemsize + (4 if add is not None else 0))
        return 2 * (2 * rows * tk + 2 * tn * tk + out_b) + (4 * rows * tn if K > tk else 0)

    m_unit = M // (a_split if mode == 'tn' else 1)
    tm = _pick(m_unit, 2048)
    if tile_bytes(tm) > MATMUL_TILE_BYTES:
        tm = _pick(m_unit, 1024)
    gm, gn, gk = M // tm, N // tn, K // tk

    def col_map(split, total_cols, tile):
        per = total_cols // split // tile

        def f(r, c):
            return (c // per, r, c % per) if split > 1 else (r, c)
        return f

    if mode == 'nn':
        a_idx = col_map(a_split, K, tk)
        b_idx = col_map(b_split, N, tn)
        a_spec = pl.BlockSpec(((None,) if a_split > 1 else ()) + (tm, tk), lambda m, n, k: a_idx(m, k))
        b_spec = pl.BlockSpec(((None,) if b_split > 1 else ()) + (tk, tn), lambda m, n, k: b_idx(k, n))
        dn = ((1,), (0,))
    elif mode == 'nt':
        a_idx = col_map(a_split, K, tk)
        b_idx = col_map(b_split, K, tk)
        a_spec = pl.BlockSpec(((None,) if a_split > 1 else ()) + (tm, tk), lambda m, n, k: a_idx(m, k))
        b_spec = pl.BlockSpec(((None,) if b_split > 1 else ()) + (tn, tk), lambda m, n, k: b_idx(n, k))
        dn = ((1,), (1,))
    else:
        a_idx = col_map(a_split, M, tm)
        b_idx = col_map(b_split, N, tn)
        a_spec = pl.BlockSpec(((None,) if a_split > 1 else ()) + (tk, tm), lambda m, n, k: a_idx(k, m))
        b_spec = pl.BlockSpec(((None,) if b_split > 1 else ()) + (tk, tn), lambda m, n, k: b_idx(k, n))
        dn = ((0,), (0,))
    o_idx = col_map(o_split, N, tn)
    o_block = ((None,) if o_split > 1 else ()) + (tm, tn)
    o_spec = pl.BlockSpec(o_block, lambda m, n, k: o_idx(m, n))
    o_shape = ((o_split, M, N // o_split) if o_split > 1 else (M, N))
    in_specs = [a_spec, b_spec]
    args = [a, b]
    if add is not None:
        in_specs.append(pl.BlockSpec((tm, tn), lambda m, n, k: (m, n)))
        args.append(add)
    n_lead = len(args)
    for t in after:
        in_specs.append(pl.BlockSpec(memory_space=pl.ANY))
        args.append(t)

    def body(*refs):
        a_ref, b_ref = refs[0], refs[1]
        add_ref = refs[2] if add is not None else None
        o_ref = refs[n_lead + len(after)]
        part = lax.dot_general(a_ref[...].astype(BF16), b_ref[...].astype(BF16), (dn, ((), ())),
                               preferred_element_type=F32)

        def finish(total):
            if add_ref is not None:
                total = total + add_ref[...]
            o_ref[...] = total.astype(o_ref.dtype)

        if gk == 1:
            finish(part)
        else:
            acc_ref = refs[-1]
            k = pl.program_id(2)

            @pl.when(k == 0)
            def _():
                acc_ref[...] = part

            @pl.when(jnp.logical_and(k > 0, k < gk - 1))
            def _():
                acc_ref[...] += part

            @pl.when(k == gk - 1)
            def _():
                finish(acc_ref[...] + part)

    return pl.pallas_call(
        body, name=name, grid=(gm, gn, gk), in_specs=in_specs, out_specs=o_spec,
        out_shape=jax.ShapeDtypeStruct(o_shape, out_dtype),
        scratch_shapes=([pltpu.VMEM((tm, tn), F32)] if gk > 1 else []),
        compiler_params=_params(("parallel", "parallel", "arbitrary")),
    )(*args)


def _row_tile():
    return _pick(SEQ, 256, 8)


def _rms_fwd(x, g, name):
    tm = _row_tile()
    (h,) = _tiled(lambda xv, gv: _rms(xv, gv), name, (SEQ // tm,),
                  [(x, (tm, D_MODEL), lambda i: (i, 0)), (g, (1, D_MODEL), lambda i: (0, 0))],
                  [((SEQ, D_MODEL), BF16, (tm, D_MODEL), lambda i: (i, 0), ())])
    return h


def _rms_bwd(x, g, dh, dres, name):
    tm = _row_tile()

    def fn(xv, gv, dhv, drv):
        _, vjp = jax.vjp(_rms, xv, gv)
        dx, dg = vjp(dhv)
        dx = dx + drv
        return dx, dx, dg

    row = lambda i: (i, 0)
    return _tiled(fn, name, (SEQ // tm,),
                  [(x, (tm, D_MODEL), row), (g, (1, D_MODEL), lambda i: (0, 0)),
                   (dh, (tm, D_MODEL), row), (dres, (tm, D_MODEL), row)],
                  [((SEQ, D_MODEL), F32, (tm, D_MODEL), row, ()),
                   ((SEQ, D_MODEL), BF16, (tm, D_MODEL), row, ()),
                   ((1, D_MODEL), F32, (1, D_MODEL), lambda i: (0, 0), (0,))])


def _loss_head(x, g, target):
    tm = _row_tile()

    def fn(xv, gv, tv):
        y, vjp = jax.vjp(_rms, xv, gv)
        err = y - tv
        part = 0.5 * jnp.sum(jnp.sum(err * err, axis=1, keepdims=True), axis=0, keepdims=True) / D_MODEL
        dx, dg = vjp(err / D_MODEL)
        return part, dx, dx, dg

    row = lambda i: (i, 0)
    one = lambda i: (0, 0)
    return _tiled(fn, "loss_head", (SEQ // tm,),
                  [(x, (tm, D_MODEL), row), (g, (1, D_MODEL), one), (target, (tm, D_MODEL), row)],
                  [((1, 1), F32, (1, 1), one, (0,)),
                   ((SEQ, D_MODEL), F32, (tm, D_MODEL), row, ()),
                   ((SEQ, D_MODEL), BF16, (tm, D_MODEL), row, ()),
                   ((1, D_MODEL), F32, (1, D_MODEL), one, (0,))])


def _p_col(width, index):
    return lambda tm: ((tm, width), lambda i: (i, index))


def _merge_fn(ga, gb, ta, tb):
    return _sigmoid(ga) * ta + _sigmoid(gb) * tb


def _merge_fwd(p, ta, tb):
    tm = _row_tile()
    d = D_MODEL
    ga_blk = 4 * _gw() + 2 * _sw()
    assert ga_blk % d == 0
    ia, ib = ga_blk // d, ga_blk // d + 1
    row = lambda i: (i, 0)
    (m,) = _tiled(_merge_fn, "merge_fwd", (SEQ // tm,),
                  [(p, (tm, d), lambda i: (i, ia)), (p, (tm, d), lambda i: (i, ib)),
                   (ta, (tm, d), row), (tb, (tm, d), row)],
                  [((SEQ, d), BF16, (tm, d), row, ())])
    return m


def _merge_bwd(p, ta, tb, dm):
    tm = _row_tile()
    d = D_MODEL
    ga_blk = 4 * _gw() + 2 * _sw()
    ia, ib = ga_blk // d, ga_blk // d + 1

    def fn(ga, gb, tav, tbv, dmv):
        _, vjp = jax.vjp(_merge_fn, ga, gb, tav, tbv)
        return vjp(dmv)

    row = lambda i: (i, 0)
    out = ((SEQ, d), BF16, (tm, d), row, ())
    return _tiled(fn, "merge_bwd", (SEQ // tm,),
                  [(p, (tm, d), lambda i: (i, ia)), (p, (tm, d), lambda i: (i, ib)),
                   (ta, (tm, d), row), (tb, (tm, d), row), (dm, (tm, d), row)],
                  [out, out, out, out])


def _gate_fn(ab, alog, dtb):
    lane = lax.broadcasted_iota(jnp.int32, ab.shape, 1)
    nd = 2 * GDN_HEADS
    g = -jnp.exp(alog) * _softplus(ab + dtb)
    beta = _sigmoid(ab)
    return jnp.where(lane < nd, g, jnp.where(lane < 2 * nd, beta, 0.0))


def _ab_index():
    off = 4 * _gw() + 2 * _sw() + 2 * D_MODEL
    assert off % AB_PAD == 0
    return off // AB_PAD


def _gate_fwd(p, alog, dtb):
    tm = _row_tile()
    iab = _ab_index()
    one = lambda i: (0, 0)
    (g,) = _tiled(_gate_fn, "gdn_gate_fwd", (SEQ // tm,),
                  [(p, (tm, AB_PAD), lambda i: (i, iab)), (alog, (1, AB_PAD), one), (dtb, (1, AB_PAD), one)],
                  [((SEQ, AB_PAD), F32, (tm, AB_PAD), lambda i: (i, 0), ())])
    return g


def _gate_bwd(p, alog, dtb, dg):
    tm = _row_tile()
    iab = _ab_index()
    one = lambda i: (0, 0)

    def fn(ab, al, db, dgv):
        _, vjp = jax.vjp(_gate_fn, ab, al, db)
        return vjp(dgv)

    return _tiled(fn, "gdn_gate_bwd", (SEQ // tm,),
                  [(p, (tm, AB_PAD), lambda i: (i, iab)), (alog, (1, AB_PAD), one), (dtb, (1, AB_PAD), one),
                   (dg, (tm, AB_PAD), lambda i: (i, 0))],
                  [((SEQ, AB_PAD), BF16, (tm, AB_PAD), lambda i: (i, 0), ()),
                   ((1, AB_PAD), F32, (1, AB_PAD), one, (0,)),
                   ((1, AB_PAD), F32, (1, AB_PAD), one, (0,))])


def _l2n(x):
    return x * lax.rsqrt(jnp.sum(x * x, axis=-1, keepdims=True) + NORM_EPS)


def _qkv_fn(xq, xk, xv, *taps):
    n = QKV_CONV
    q = _l2n(_silu(_conv_rows(xq, taps[0:n])))
    k = _l2n(_silu(_conv_rows(xk, taps[n:2 * n])))
    v = _silu(_conv_rows(xv, taps[2 * n:3 * n]))
    return q, k, v


def _qkv_specs(p, conv_w):
    hd, nh = HEAD_DIM, GDN_HEADS
    ins = [(p, (SEQ, hd), (lambda h, s=s: (0, s * nh + h))) for s in range(3)]
    ins += [(conv_w, (QKV_CONV, hd), (lambda h, s=s: (0, s * nh + h))) for s in range(3)]
    return ins


def _qkv_fwd(p, conv_w):
    hd, nh = HEAD_DIM, GDN_HEADS
    ins = _qkv_specs(p, conv_w)
    n_in = len(ins)
    out_spec = pl.BlockSpec((SEQ, hd), lambda h: (0, h))

    def body(*refs):
        xs = [r[...] for r in refs[:3]]
        taps = [refs[3 + s][j:j + 1, :] for s in range(3) for j in range(QKV_CONV)]
        q, k, v = _qkv_fn(*xs, *taps)
        refs[n_in][...] = q
        refs[n_in + 1][...] = k
        refs[n_in + 2][...] = v

    return pl.pallas_call(
        body, name="gdn_qkv_fwd", grid=(nh,),
        in_specs=[pl.BlockSpec(b, m) for _, b, m in ins], out_specs=[out_spec] * 3,
        out_shape=[jax.ShapeDtypeStruct((SEQ, nh * hd), F32)] * 3,
        compiler_params=_params(("parallel",)),
    )(*[a for a, _, _ in ins])


def _qkv_bwd(p, conv_w, dq, dk, dv):
    hd, nh = HEAD_DIM, GDN_HEADS
    ins = _qkv_specs(p, conv_w) + [(t, (SEQ, hd), lambda h: (0, h)) for t in (dq, dk, dv)]
    n_in = len(ins)

    def body(*refs):
        xs = [r[...] for r in refs[:3]]
        taps = [refs[3 + s][j:j + 1, :] for s in range(3) for j in range(QKV_CONV)]
        cts = tuple(r[...] for r in refs[6:9])
        _, vjp = jax.vjp(_qkv_fn, *xs, *taps)
        grads = vjp(cts)
        for s in range(3):
            refs[n_in + s][...] = grads[s].astype(BF16)
            for j in range(QKV_CONV):
                refs[n_in + 3 + s][j:j + 1, :] = jnp.sum(grads[3 + s * QKV_CONV + j], axis=0, keepdims=True)

    dx_spec = pl.BlockSpec((SEQ, hd), lambda h: (0, h))
    dw_spec = pl.BlockSpec((QKV_CONV, hd), lambda h: (0, h))
    outs = pl.pallas_call(
        body, name="gdn_qkv_bwd", grid=(nh,),
        in_specs=[pl.BlockSpec(b, m) for _, b, m in ins], out_specs=[dx_spec] * 3 + [dw_spec] * 3,
        out_shape=[jax.ShapeDtypeStruct((SEQ, nh * hd), BF16)] * 3
        + [jax.ShapeDtypeStruct((QKV_CONV, nh * hd), F32)] * 3,
        compiler_params=_params(("parallel",)),
    )(*[a for a, _, _ in ins])
    return outs[0:3], jnp.concatenate(outs[3:6], axis=1)


def _prep_fn(qs, ks, vs, gblk):
    nh = len(qs)
    c = qs[0].shape[0]
    scale = HEAD_DIM ** -0.5
    ii = lax.broadcasted_iota(jnp.int32, (c, c), 0)
    jj = lax.broadcasted_iota(jnp.int32, (c, c), 1)
    eye = (ii == jj).astype(F32)
    lane = lax.broadcasted_iota(jnp.int32, gblk.shape, 1)
    kk_t = [_mm_nt(k, k) for k in ks]
    qk_t = [_mm_nt(q, k) for q, k in zip(qs, ks)]
    chains = [(h, d) for h in range(nh) for d in range(2)]
    pre = []
    for h, d in chains:
        g = jnp.sum(jnp.where(lane == h + d * nh, gblk, 0.0), axis=1, keepdims=True)
        beta = jnp.sum(jnp.where(lane == h + (2 + d) * nh, gblk, 0.0), axis=1, keepdims=True)
        incl = (jj <= ii) if d == 0 else (jj >= ii)
        strict = (jj < ii) if d == 0 else (jj > ii)
        incl_t = (ii <= jj) if d == 0 else (ii >= jj)
        g_row = jnp.sum(eye * g, axis=0, keepdims=True)
        gc = jnp.sum(jnp.where(incl, g_row, 0.0), axis=1, keepdims=True)
        gc_row = jnp.sum(jnp.where(incl_t, g, 0.0), axis=0, keepdims=True)
        decay = jnp.where(incl, jnp.exp(jnp.where(incl, gc - gc_row, 0.0)), 0.0)
        a = jnp.where(strict, kk_t[h] * beta * decay, 0.0)
        pre.append((g, beta, gc, decay, incl, a))
    ts = _inv_unit([p[5] for p in pre])
    us = [_mmh(t, vs[h] * p[1]) for (h, d), p, t in zip(chains, pre, ts)]
    ws = [_mmh(t, ks[h] * (p[1] * jnp.exp(p[2]))) for (h, d), p, t in zip(chains, pre, ts)]
    outs = [[None, None] for _ in range(nh)]
    for (h, d), (g, beta, gc, decay, incl, a), u, w in zip(chains, pre, us, ws):
        qk = jnp.where(incl, qk_t[h] * (scale * decay), 0.0)
        g_last = jnp.sum(g, axis=0, keepdims=True)
        q_dec = qs[h] * (scale * jnp.exp(gc))
        k_dec = ks[h] * jnp.exp(g_last - gc)
        outs[h][d] = (u, w, q_dec, k_dec, qk, jnp.exp(g_last))
    return outs


def _n_chunks():
    return SEQ // GDN_CHUNK


def _gdn_prep_fwd(q, k, v, gates):
    c, hd, nh, nc = GDN_CHUNK, HEAD_DIM, GDN_HEADS, _n_chunks()
    gw = nh * hd
    row = lambda n: (n, 0)

    def body(q_ref, k_ref, v_ref, g_ref, u_ref, w_ref, qd_ref, kd_ref, qk_ref, dec_ref):
        heads = [slice(h * hd, (h + 1) * hd) for h in range(nh)]
        res = _prep_fn([q_ref[:, s] for s in heads], [k_ref[:, s] for s in heads], [v_ref[:, s] for s in heads],
                       g_ref[...])
        for h, cols in enumerate(heads):
            for d in range(2):
                u, w, qd, kd, qk, dec = res[h][d]
                u_ref[d, :, cols] = u
                w_ref[d, :, cols] = w
                qd_ref[d, :, cols] = qd
                kd_ref[d, :, cols] = kd
                qk_ref[d, h] = qk
                dec_ref[d, h:h + 1, :] = jnp.broadcast_to(dec, (1, LANES))

    wide = pl.BlockSpec((2, c, gw), lambda n: (0, n, 0))
    return pl.pallas_call(
        body, name="gdn_prep_fwd", grid=(nc,),
        in_specs=[pl.BlockSpec((c, gw), row)] * 3 + [pl.BlockSpec((c, LANES), row)],
        out_specs=[wide] * 4 + [pl.BlockSpec((2, nh, c, c), lambda n: (0, 0, n, 0)),
                                pl.BlockSpec((2, None, nh, LANES), lambda n: (0, n, 0, 0))],
        out_shape=[jax.ShapeDtypeStruct((2, SEQ, gw), F32)] * 4
        + [jax.ShapeDtypeStruct((2, nh, SEQ, c), F32), jax.ShapeDtypeStruct((2, nc, nh, LANES), F32)],
        compiler_params=_params(("parallel",)),
    )(q, k, v, gates)


def _gdn_prep_bwd(q, k, v, gates, cts):
    c, hd, nh, nc = GDN_CHUNK, HEAD_DIM, GDN_HEADS, _n_chunks()
    gw = nh * hd
    row = lambda n: (n, 0)

    def body(*refs):
        q_ref, k_ref, v_ref, g_ref = refs[:4]
        ct_refs = refs[4:16]
        dq_ref, dk_ref, dv_ref, dg_ref = refs[16:20]
        heads = [slice(h * hd, (h + 1) * hd) for h in range(nh)]
        _, vjp = jax.vjp(_prep_fn, [q_ref[:, s] for s in heads], [k_ref[:, s] for s in heads],
                         [v_ref[:, s] for s in heads], g_ref[...])
        cts_in = []
        for h, cols in enumerate(heads):
            per_dir = []
            for d in range(2):
                r = ct_refs[6 * d:6 * d + 6]
                per_dir.append((r[0][:, cols], r[1][:, cols], r[2][:, cols], r[3][:, cols], r[4][h],
                                r[5][h:h + 1, 0:1]))
            cts_in.append(per_dir)
        dqs, dks, dvs, dg = vjp(cts_in)
        for h, cols in enumerate(heads):
            dq_ref[:, cols] = dqs[h]
            dk_ref[:, cols] = dks[h]
            dv_ref[:, cols] = dvs[h]
        dg_ref[...] = dg

    one_dir = [pl.BlockSpec((c, gw), row)] * 4 + [pl.BlockSpec((nh, c, c), lambda n: (0, n, 0)),
                                                  pl.BlockSpec((None, nh, LANES), lambda n: (n, 0, 0))]
    return pl.pallas_call(
        body, name="gdn_prep_bwd", grid=(nc,),
        in_specs=[pl.BlockSpec((c, gw), row)] * 3 + [pl.BlockSpec((c, LANES), row)] + one_dir * 2,
        out_specs=[pl.BlockSpec((c, gw), row)] * 3 + [pl.BlockSpec((c, LANES), row)],
        out_shape=[jax.ShapeDtypeStruct((SEQ, gw), F32)] * 3 + [jax.ShapeDtypeStruct((SEQ, LANES), F32)],
        compiler_params=_params(("parallel",)),
    )(q, k, v, gates, *cts)


def _scan_steps(states, us, ws, qds, kds, qks, decs):
    w_s = [_mm(w, s) for w, s in zip(ws, states)]
    q_s = [_mm(qd, s) for qd, s in zip(qds, states)]
    v_new = [u - x for u, x in zip(us, w_s)]
    outs = [x + _mm(qk, vn) for x, qk, vn in zip(q_s, qks, v_new)]
    k_v = [_mm_tn(kd, vn) for kd, vn in zip(kds, v_new)]
    new_states = [s * dec + x for s, dec, x in zip(states, decs, k_v)]
    return new_states, outs


def _scan_operands(ins, chains, dec_lanes):
    cols = lambda h: slice(h * HEAD_DIM, (h + 1) * HEAD_DIM)
    wide = [[ins[6 * d + j][:, cols(h)] for d, h in chains] for j in range(4)]
    qks = [ins[6 * d + 4][h] for d, h in chains]
    decs = [ins[6 * d + 5][h:h + 1, dec_lanes] for d, h in chains]
    return (*wide, qks, decs)


def _scan_in_specs(chunk_of):
    c, hd, nh = GDN_CHUNK, HEAD_DIM, GDN_HEADS
    gw = nh * hd
    specs = []
    for d in range(2):
        f = chunk_of[d]
        specs += [pl.BlockSpec((None, c, gw), lambda n, d=d, f=f: (d, f(n), 0))] * 4
        specs += [pl.BlockSpec((None, nh, c, c), lambda n, d=d, f=f: (d, 0, f(n), 0)),
                  pl.BlockSpec((None, None, nh, LANES), lambda n, d=d, f=f: (d, f(n), 0, 0))]
    return specs


def _gdn_scan_fwd(u, w, qd, kd, qk, dec):
    c, hd, nh, nc = GDN_CHUNK, HEAD_DIM, GDN_HEADS, _n_chunks()
    gw = nh * hd
    chunk_of = (lambda n: n, lambda n: nc - 1 - n)

    def body(*refs):
        ins = refs[:12]
        o_refs = refs[12:14]
        st_refs = refs[14:16]
        s_ref = refs[16]

        @pl.when(pl.program_id(0) == 0)
        def _():
            s_ref[...] = jnp.zeros_like(s_ref)

        chains = [(d, h) for d in range(2) for h in range(nh)]
        states = [s_ref[d * nh + h] for d, h in chains]
        for (d, h), st in zip(chains, states):
            st_refs[d][h] = st
        new_states, outs = _scan_steps(states, *_scan_operands(ins, chains, slice(None)))
        for (d, h), st, o in zip(chains, new_states, outs):
            s_ref[d * nh + h] = st
            o_refs[d][:, h * hd:(h + 1) * hd] = o

    o_specs = [pl.BlockSpec((c, gw), lambda n, f=f: (f(n), 0)) for f in chunk_of]
    st_specs = [pl.BlockSpec((None, nh, hd, hd), lambda n, f=f: (f(n), 0, 0, 0)) for f in chunk_of]
    return pl.pallas_call(
        body, name="gdn_scan_fwd", grid=(nc,), in_specs=_scan_in_specs(chunk_of),
        out_specs=o_specs + st_specs,
        out_shape=[jax.ShapeDtypeStruct((SEQ, gw), F32)] * 2 + [jax.ShapeDtypeStruct((nc, nh, hd, hd), F32)] * 2,
        scratch_shapes=[pltpu.VMEM((2 * nh, hd, hd), F32)],
        compiler_params=_params(("arbitrary",)),
    )(*([u, w, qd, kd, qk, dec] * 2))


def _gdn_scan_bwd(u, w, qd, kd, qk, dec, st0, st1, do):
    c, hd, nh, nc = GDN_CHUNK, HEAD_DIM, GDN_HEADS, _n_chunks()
    gw = nh * hd
    chunk_of = (lambda n: nc - 1 - n, lambda n: n)

    def body(*refs):
        ins = refs[:12]
        st_in = refs[12:14]
        do_in = refs[14:16]
        outs = refs[16:28]
        ds_ref = refs[28]

        @pl.when(pl.program_id(0) == 0)
        def _():
            ds_ref[...] = jnp.zeros_like(ds_ref)

        chains = [(d, h) for d in range(2) for h in range(nh)]
        states = [st_in[d][h] for d, h in chains]
        _, vjp = jax.vjp(_scan_steps, states, *_scan_operands(ins, chains, slice(0, 1)))
        ct_state = [ds_ref[d * nh + h] for d, h in chains]
        ct_out = [do_in[d][:, h * hd:(h + 1) * hd] for d, h in chains]
        grads = vjp((ct_state, ct_out))
        for i, (d, h) in enumerate(chains):
            cols = slice(h * hd, (h + 1) * hd)
            du_r, dw_r, dqd_r, dkd_r, dqk_r, ddec_r = outs[6 * d:6 * d + 6]
            ds_ref[d * nh + h] = grads[0][i]
            du_r[:, cols] = grads[1][i]
            dw_r[:, cols] = grads[2][i]
            dqd_r[:, cols] = grads[3][i]
            dkd_r[:, cols] = grads[4][i]
            dqk_r[h] = grads[5][i]
            ddec_r[h:h + 1, :] = jnp.broadcast_to(grads[6][i], (1, LANES))

    st_specs = [pl.BlockSpec((None, nh, hd, hd), lambda n, f=f: (f(n), 0, 0, 0)) for f in chunk_of]
    do_specs = [pl.BlockSpec((c, gw), lambda n, f=f: (f(n), 0)) for f in chunk_of]
    out_specs, out_shape = [], []
    for f in chunk_of:
        out_specs += [pl.BlockSpec((c, gw), lambda n, f=f: (f(n), 0))] * 4
        out_specs += [pl.BlockSpec((nh, c, c), lambda n, f=f: (0, f(n), 0)),
                      pl.BlockSpec((None, nh, LANES), lambda n, f=f: (f(n), 0, 0))]
        out_shape += [jax.ShapeDtypeStruct((SEQ, gw), F32)] * 4
        out_shape += [jax.ShapeDtypeStruct((nh, SEQ, c), F32), jax.ShapeDtypeStruct((nc, nh, LANES), F32)]
    return pl.pallas_call(
        body, name="gdn_scan_bwd", grid=(nc,), in_specs=_scan_in_specs(chunk_of) + st_specs + do_specs,
        out_specs=out_specs, out_shape=out_shape,
        scratch_shapes=[pltpu.VMEM((2 * nh, hd, hd), F32)],
        compiler_params=_params(("arbitrary",)),
    )(*([u, w, qd, kd, qk, dec] * 2), st0, st1, do, do)


def _post_fn(o0, o1, z, gn):
    return _rms(o0 + o1, gn) * _silu(z)


def _gdn_post_fwd(o0, o1, p, gn):
    tm, hd, nh = _row_tile(), HEAD_DIM, GDN_HEADS
    zoff = 3 * nh
    blk = lambda i, h: (i, h)
    (ya,) = _tiled(_post_fn, "gdn_post_fwd", (SEQ // tm, nh),
                   [(o0, (tm, hd), blk), (o1, (tm, hd), blk),
                    (p, (tm, hd), lambda i, h: (i, zoff + h)), (gn, (1, hd), lambda i, h: (0, 0))],
                   [((SEQ, nh * hd), BF16, (tm, hd), blk, ())])
    return ya


def _gdn_post_bwd(o0, o1, p, gn, dya):
    tm, hd, nh = _row_tile(), HEAD_DIM, GDN_HEADS
    zoff = 3 * nh
    blk = lambda i, h: (i, h)

    def fn(a, b, z, g, dy):
        _, vjp = jax.vjp(_post_fn, a, b, z, g)
        do, _, dz, dg = vjp(dy.astype(F32))
        return do, dz, dg

    return _tiled(fn, "gdn_post_bwd", (SEQ // tm, nh),
                  [(o0, (tm, hd), blk), (o1, (tm, hd), blk),
                   (p, (tm, hd), lambda i, h: (i, zoff + h)), (gn, (1, hd), lambda i, h: (0, 0)),
                   (dya, (tm, hd), blk)],
                  [((SEQ, nh * hd), F32, (tm, hd), blk, ()),
                   ((SEQ, nh * hd), BF16, (tm, hd), blk, ()),
                   ((1, hd), F32, (1, hd), lambda i, h: (0, 0), (0, 1))],
                  sem=("arbitrary", "arbitrary"))


def _sgu_ln(v, lg, lb):
    gv = _gelu(v)
    mu = jnp.mean(gv, axis=-1, keepdims=True)
    cen = gv - mu
    var = jnp.mean(cen * cen, axis=-1, keepdims=True)
    return cen * lax.rsqrt(var + NORM_EPS) * lg + lb


def _sgu_mix(vn, w_ref, bt_ref):
    parts = []
    for g in range(SGU_GROUPS):
        cols = slice(g * SGU_DIM, (g + 1) * SGU_DIM)
        parts.append(_mm(w_ref[g], vn[:, cols]) + bt_ref[:, g:g + 1])
    return jnp.concatenate(parts, axis=1)


def _sgu_specs(p):
    sw, sb = _sw(), SGU_BLOCK
    uoff = 4 * _gw()
    assert uoff % sw == 0
    iu = uoff // sw
    return [(p, (sb, sw), lambda n: (n, iu)), (p, (sb, sw), lambda n: (n, iu + 1))]


def _sgu_fwd(p, lg, lb, w_s, b_t):
    sw, sb = _sw(), SGU_BLOCK
    ins = _sgu_specs(p)

    def body(u_ref, v_ref, lg_ref, lb_ref, w_ref, bt_ref, y_ref):
        vn = _sgu_ln(v_ref[...], lg_ref[...], lb_ref[...])
        y_ref[...] = (_gelu(u_ref[...]) * _sgu_mix(vn, w_ref, bt_ref)).astype(BF16)

    one2 = lambda n: (0, 0)
    return pl.pallas_call(
        body, name="sgu_fwd", grid=(SEQ // sb,),
        in_specs=[pl.BlockSpec(b, m) for _, b, m in ins]
        + [pl.BlockSpec((1, sw), one2), pl.BlockSpec((1, sw), one2),
           pl.BlockSpec((SGU_GROUPS, sb, sb), lambda n: (0, 0, 0)), pl.BlockSpec((sb, LANES), one2)],
        out_specs=pl.BlockSpec((sb, sw), lambda n: (n, 0)),
        out_shape=jax.ShapeDtypeStruct((SEQ, sw), BF16),
        compiler_params=_params(("parallel",)),
    )(p, p, lg, lb, w_s, b_t)


def _sgu_bwd(p, lg, lb, w_s, b_t, dy):
    sw, sb, ng = _sw(), SGU_BLOCK, SGU_GROUPS
    ins = _sgu_specs(p)

    def body(u_ref, v_ref, lg_ref, lb_ref, w_ref, bt_ref, dy_ref, du_ref, dv_ref, dlg_ref, dlb_ref, dw_ref, dbt_ref):
        first = pl.program_id(0) == 0
        u, v = u_ref[...], v_ref[...]
        gu, gelu_vjp = jax.vjp(_gelu, u)
        vn, ln_vjp = jax.vjp(_sgu_ln, v, lg_ref[...], lb_ref[...])
        s = _sgu_mix(vn, w_ref, bt_ref)
        dyv = dy_ref[...].astype(F32)
        ds = dyv * gu
        (du,) = gelu_vjp(dyv * s)
        lane = lax.broadcasted_iota(jnp.int32, (sb, LANES), 1)
        dvn_parts = []
        dbt = jnp.zeros((sb, LANES), F32)
        for g in range(ng):
            cols = slice(g * SGU_DIM, (g + 1) * SGU_DIM)
            ds_g = ds[:, cols]
            dw_g = _mm_nt(ds_g, vn[:, cols])
            dvn_parts.append(_mm_tn(w_ref[g], ds_g))
            dbt = dbt + jnp.where(lane == g, jnp.sum(ds_g, axis=1, keepdims=True), 0.0)

            @pl.when(first)
            def _(g=g, dw_g=dw_g):
                dw_ref[g] = dw_g

            @pl.when(jnp.logical_not(first))
            def _(g=g, dw_g=dw_g):
                dw_ref[g] += dw_g

        dv, dlg, dlb = ln_vjp(jnp.concatenate(dvn_parts, axis=1))
        du_ref[...] = du.astype(BF16)
        dv_ref[...] = dv.astype(BF16)

        @pl.when(first)
        def _():
            dlg_ref[...] = dlg
            dlb_ref[...] = dlb
            dbt_ref[...] = dbt

        @pl.when(jnp.logical_not(first))
        def _():
            dlg_ref[...] += dlg
            dlb_ref[...] += dlb
            dbt_ref[...] += dbt

    one2 = lambda n: (0, 0)
    row = pl.BlockSpec((sb, sw), lambda n: (n, 0))
    return pl.pallas_call(
        body, name="sgu_bwd", grid=(SEQ // sb,),
        in_specs=[pl.BlockSpec(b, m) for _, b, m in ins]
        + [pl.BlockSpec((1, sw), one2), pl.BlockSpec((1, sw), one2),
           pl.BlockSpec((ng, sb, sb), lambda n: (0, 0, 0)), pl.BlockSpec((sb, LANES), one2), row],
        out_specs=[row, row, pl.BlockSpec((1, sw), one2), pl.BlockSpec((1, sw), one2),
                   pl.BlockSpec((ng, sb, sb), lambda n: (0, 0, 0)), pl.BlockSpec((sb, LANES), one2)],
        out_shape=[jax.ShapeDtypeStruct((SEQ, sw), BF16)] * 2 + [jax.ShapeDtypeStruct((1, sw), F32)] * 2
        + [jax.ShapeDtypeStruct((ng, sb, sb), F32), jax.ShapeDtypeStruct((sb, LANES), F32)],
        compiler_params=_params(("arbitrary",)),
    )(p, p, lg, lb, w_s, b_t, dy)


def _ffn_fn(xg, xv, bg, bv, *taps):
    n = FFN_CONV
    cg = _conv_rows(xg, taps[0:n]) + bg
    cv = _conv_rows(xv, taps[n:2 * n]) + bv
    return _silu(cg) * cv


def _ffn_tile():
    return _pick(D_FF, 256)


def _ffn_specs(up, conv_w, conv_b):
    tc = _ffn_tile()
    nt = D_FF // tc
    ins = [(up, (None, SEQ, tc), (lambda j, s=s: (s, 0, j))) for s in range(2)]
    ins += [(conv_b, (1, tc), (lambda j, s=s: (0, s * nt + j))) for s in range(2)]
    ins += [(conv_w, (FFN_CONV, tc), (lambda j, s=s: (0, s * nt + j))) for s in range(2)]
    return ins


def _ffn_act_fwd(up, conv_w, conv_b):
    tc = _ffn_tile()
    ins = _ffn_specs(up, conv_w, conv_b)

    def body(xg, xv, bg, bv, wg, wv, o_ref):
        taps = [w[j:j + 1, :] for w in (wg, wv) for j in range(FFN_CONV)]
        o_ref[...] = _ffn_fn(xg[...], xv[...], bg[...], bv[...], *taps).astype(BF16)

    return pl.pallas_call(
        body, name="ffn_act_fwd", grid=(D_FF // tc,),
        in_specs=[pl.BlockSpec(b, m) for _, b, m in ins],
        out_specs=pl.BlockSpec((SEQ, tc), lambda j: (0, j)),
        out_shape=jax.ShapeDtypeStruct((SEQ, D_FF), BF16),
        compiler_params=_params(("parallel",)),
    )(*[a for a, _, _ in ins])


def _ffn_act_bwd(up, conv_w, conv_b, dact):
    tc = _ffn_tile()
    nt = D_FF // tc
    ins = _ffn_specs(up, conv_w, conv_b) + [(dact, (SEQ, tc), lambda j: (0, j))]

    def body(xg, xv, bg, bv, wg, wv, dact_ref, dup_ref, dwg_ref, dwv_ref, dbg_ref, dbv_ref):
        taps = [w[j:j + 1, :] for w in (wg, wv) for j in range(FFN_CONV)]
        _, vjp = jax.vjp(_ffn_fn, xg[...], xv[...], bg[...], bv[...], *taps)
        grads = vjp(dact_ref[...].astype(F32))
        dup_ref[0] = grads[0].astype(BF16)
        dup_ref[1] = grads[1].astype(BF16)
        dbg_ref[...] = jnp.sum(grads[2], axis=0, keepdims=True)
        dbv_ref[...] = jnp.sum(grads[3], axis=0, keepdims=True)
        for j in range(FFN_CONV):
            dwg_ref[j:j + 1, :] = jnp.sum(grads[4 + j], axis=0, keepdims=True)
            dwv_ref[j:j + 1, :] = jnp.sum(grads[4 + FFN_CONV + j], axis=0, keepdims=True)

    col = lambda j: (0, j)
    outs = pl.pallas_call(
        body, name="ffn_act_bwd", grid=(nt,),
        in_specs=[pl.BlockSpec(b, m) for _, b, m in ins],
        out_specs=[pl.BlockSpec((2, SEQ, tc), lambda j: (0, 0, j)),
                   pl.BlockSpec((FFN_CONV, tc), col), pl.BlockSpec((FFN_CONV, tc), col),
                   pl.BlockSpec((1, tc), col), pl.BlockSpec((1, tc), col)],
        out_shape=[jax.ShapeDtypeStruct((2, SEQ, D_FF), BF16),
                   jax.ShapeDtypeStruct((FFN_CONV, D_FF), F32), jax.ShapeDtypeStruct((FFN_CONV, D_FF), F32),
                   jax.ShapeDtypeStruct((1, D_FF), F32), jax.ShapeDtypeStruct((1, D_FF), F32)],
        compiler_params=_params(("parallel",)),
    )(*[a for a, _, _ in ins])
    dup, dwg, dwv, dbg, dbv = outs
    return dup, jnp.concatenate([dwg, dwv], axis=1), jnp.concatenate([dbg, dbv], axis=1)


def _pad_lanes(v):
    return jnp.pad(v.reshape(1, -1), ((0, 0), (0, LANES - v.size)))


def _layer_fwd(x, lw, late=None):
    h = _rms_fwd(x, lw['norm_mix_g'], "rms_mix_fwd")
    p = _matmul(h, lw['w_all'], 'nn', F32, "mm_in")
    if late is not None:
        lw = dict(lw, **late(p))
    gates = _gate_fwd(p, lw['a_log'], lw['dt_bias'])
    q, k, v = _qkv_fwd(p, lw['qkv_conv_w'])
    u, w, qd, kd, qk, dec = _gdn_prep_fwd(q, k, v, gates)
    o0, o1, st0, st1 = _gdn_scan_fwd(u, w, qd, kd, qk, dec)
    ya = _gdn_post_fwd(o0, o1, p, lw['gdn_norm_g'])
    yb = _sgu_fwd(p, lw['sgu_ln_g'], lw['sgu_ln_b'], lw['sgu_w'], lw['sgu_bt'])
    ta = _matmul(ya, lw['w_branch_a'], 'nn', F32, "mm_branch_a", b_split=N_CHIPS)
    tb = _matmul(yb, lw['w_branch_b'], 'nn', F32, "mm_branch_b", b_split=N_CHIPS)
    m = _merge_fwd(p, ta, tb)
    x1 = _matmul(m, lw['w_out'], 'nn', F32, "mm_out", add=x)
    h2 = _rms_fwd(x1, lw['norm_ffn_g'], "rms_ffn_fwd")
    up = _matmul(h2, lw['w_up'], 'nn', F32, "mm_up", b_split=N_CHIPS, o_split=2)
    act = _ffn_act_fwd(up, lw['ffn_conv_w'], lw['ffn_conv_b'])
    x2 = _matmul(act, lw['w_down'], 'nn', F32, "mm_down", add=x1)
    saved = dict(x=x, h=h, p=p, gates=gates, q=q, k=k, v=v, u=u, w=w, qd=qd, kd=kd, qk=qk, dec=dec,
                 o0=o0, o1=o1, st0=st0, st1=st1, ya=ya, yb=yb, ta=ta, tb=tb, m=m, x1=x1, h2=h2, up=up, act=act)
    return x2, saved, lw


def _layer_bwd(dx, dx_bf, lw, s, after=(), mid=None, last=None):
    g = {}
    dact = _matmul(dx_bf, lw['w_down'], 'nt', BF16, "mm_down_dgrad", after=after)
    g['w_down'] = _matmul(s['act'], dx_bf, 'tn', PARTIAL, "mm_down_wgrad")
    dup, g['ffn_conv_w'], g['ffn_conv_b'] = _ffn_act_bwd(s['up'], lw['ffn_conv_w'], lw['ffn_conv_b'], dact)
    dh2 = _matmul(dup, lw['w_up'], 'nt', F32, "mm_up_dgrad", a_split=2, b_split=N_CHIPS)
    g['w_up'] = _matmul(s['h2'], dup, 'tn', PARTIAL, "mm_up_wgrad", b_split=2, o_split=N_CHIPS)
    dx1, dx1_bf, g['norm_ffn_g'] = _rms_bwd(s['x1'], lw['norm_ffn_g'], dh2, dx, "rms_ffn_bwd")
    after_mid = mid(g, dh2) if mid is not None else ()
    dm = _matmul(dx1_bf, lw['w_out'], 'nt', F32, "mm_out_dgrad", after=after_mid)
    g['w_out'] = _matmul(s['m'], dx1_bf, 'tn', PARTIAL, "mm_out_wgrad")
    d_ga, d_gb, d_ta, d_tb = _merge_bwd(s['p'], s['ta'], s['tb'], dm)
    dya = _matmul(d_ta, lw['w_branch_a'], 'nt', BF16, "mm_branch_a_dgrad", b_split=N_CHIPS)
    dyb = _matmul(d_tb, lw['w_branch_b'], 'nt', BF16, "mm_branch_b_dgrad", b_split=N_CHIPS)
    g['w_branch_a'] = _matmul(s['ya'], d_ta, 'tn', PARTIAL, "mm_branch_a_wgrad", o_split=N_CHIPS)
    g['w_branch_b'] = _matmul(s['yb'], d_tb, 'tn', PARTIAL, "mm_branch_b_wgrad", o_split=N_CHIPS)
    du_s, dv_s, g['sgu_ln_g'], g['sgu_ln_b'], g['sgu_w'], g['sgu_bt'] = _sgu_bwd(
        s['p'], lw['sgu_ln_g'], lw['sgu_ln_b'], lw['sgu_w'], lw['sgu_bt'], dyb)
    do, dz, g['gdn_norm_g'] = _gdn_post_bwd(s['o0'], s['o1'], s['p'], lw['gdn_norm_g'], dya)
    cts = _gdn_scan_bwd(s['u'], s['w'], s['qd'], s['kd'], s['qk'], s['dec'], s['st0'], s['st1'], do)
    dq, dk, dv, dgates = _gdn_prep_bwd(s['q'], s['k'], s['v'], s['gates'], cts)
    (dxq, dxk, dxv), g['qkv_conv_w'] = _qkv_bwd(s['p'], lw['qkv_conv_w'], dq, dk, dv)
    d_ab, g['a_log'], g['dt_bias'] = _gate_bwd(s['p'], lw['a_log'], lw['dt_bias'], dgates)
    dp = jnp.concatenate([dxq, dxk, dxv, dz, du_s, dv_s, d_ga, d_gb, d_ab], axis=1)
    g['w_all'] = _matmul(s['h'], dp, 'tn', PARTIAL, "mm_in_wgrad")
    after_wgrads = last(g, dp) if last is not None else ()
    dh = _matmul(dp, lw['w_all'], 'nt', F32, "mm_in_dgrad", after=after_wgrads)
    dx0, dx0_bf, g['norm_mix_g'] = _rms_bwd(s['x'], lw['norm_mix_g'], dh, dx1, "rms_mix_bwd")
    return dx0, dx0_bf, g


def _w_in_segments():
    n_ab = 4 * GDN_HEADS
    cut = 4 * _gw()
    n_main = cut + 2 * _sw() + 2 * D_MODEL
    return [(0, cut, 0), (cut, cut + n_ab, n_main), (cut + n_ab, n_main + n_ab, cut)]


def _w_all_from_chips(t):
    width = t.shape[-1]
    runs = sorted(_w_in_segments(), key=lambda s: s[2])
    pieces = []
    for lo, hi, _ in runs:
        for j in range(N_CHIPS):
            a, b = max(lo, j * width), min(hi, (j + 1) * width)
            if a < b:
                pieces.append(t[j][:, a - j * width:b - j * width])
    pieces.append(jnp.zeros((t.shape[1], AB_PAD - 4 * GDN_HEADS), t.dtype))
    return jnp.concatenate(pieces, axis=1)


def _w_in_grad_chips(g_all):
    segs = _w_in_segments()
    width = segs[-1][1] // N_CHIPS
    chips = []
    for j in range(N_CHIPS):
        pieces = []
        for lo, hi, start in segs:
            a, b = max(lo, j * width), min(hi, (j + 1) * width)
            if a < b:
                pieces.append(g_all[:, start + a - lo:start + b - lo])
        chips.append(jnp.concatenate(pieces, axis=1))
    return jnp.stack(chips)


_FIRST = ('w_in', 'qkv_conv_w')
_REST = ('w_branch_a', 'w_branch_b', 'w_out', 'w_up', 'w_down', 'ffn_conv_w')


def _cat_cols(t):
    return jnp.concatenate([t[j] for j in range(N_CHIPS)], axis=-1)


def _layer_weights_rest(big):
    return dict(
        w_branch_a=big['w_branch_a'], w_branch_b=big['w_branch_b'], w_up=big['w_up'],
        w_out=big['w_out'].reshape(D_MODEL, D_MODEL), w_down=big['w_down'].reshape(D_FF, D_MODEL),
        ffn_conv_w=_cat_cols(big['ffn_conv_w']))


def _layer_weights_first(l, big, small):
    return dict(
        w_all=_w_all_from_chips(big['w_in']), qkv_conv_w=_cat_cols(big['qkv_conv_w']),
        norm_mix_g=small['norm_mix_g'][l:l + 1], norm_ffn_g=small['norm_ffn_g'][l:l + 1],
        a_log=_pad_lanes(small['a_log'][l]), dt_bias=_pad_lanes(small['dt_bias'][l]),
        gdn_norm_g=small['gdn_norm_g'][l:l + 1],
        sgu_ln_g=small['sgu_ln_g'][l:l + 1], sgu_ln_b=small['sgu_ln_b'][l:l + 1], sgu_w=small['sgu_w'][l],
        sgu_bt=jnp.pad(small['sgu_b'][l].T, ((0, 0), (0, LANES - SGU_GROUPS))),
        ffn_conv_b=small['ffn_conv_b'][l:l + 1])


_SMALL_GRADS = ('norm_mix_g', 'a_log', 'dt_bias', 'gdn_norm_g', 'sgu_ln_g', 'sgu_ln_b', 'sgu_w', 'sgu_bt',
                'norm_ffn_g', 'ffn_conv_b', 'qkv_conv_w', 'ffn_conv_w')
_BIG = ('w_in', 'w_branch_a', 'w_branch_b', 'w_out', 'w_up', 'w_down')


def _big_grad_slices(g, names=_BIG):
    def one(k):
        if k == 'w_in':
            return _w_in_grad_chips(g['w_all'])
        if k == 'w_out':
            return g[k].reshape(N_CHIPS, D_MODEL // N_CHIPS, D_MODEL)
        if k == 'w_down':
            return g[k].reshape(N_CHIPS, D_FF // N_CHIPS, D_MODEL)
        return g[k]
    return {k: one(k) for k in names}


def _adamw(w, g, m, v, name, after=(), pass_g=False):
    shape = w.shape
    cols = shape[-1]
    rows = w.size // cols
    tr = _pick(rows, max(8, (1 << 18) // cols // 8 * 8), 8) if rows % 8 == 0 else rows

    def fn(wv, gv, mv, vv):
        m2 = ADAM_B1 * mv + (1.0 - ADAM_B1) * gv
        v2 = ADAM_B2 * vv + (1.0 - ADAM_B2) * (gv * gv)
        m_hat = m2 / (1.0 - ADAM_B1 ** ADAM_STEP)
        v_hat = v2 / (1.0 - ADAM_B2 ** ADAM_STEP)
        delta = -ADAM_LR * (m_hat / (jnp.sqrt(v_hat) + ADAM_EPS) + ADAM_WD * wv)
        return (delta, m2, v2, gv) if pass_g else (delta, m2, v2)

    row = lambda i: (i, 0)
    outs = _tiled(fn, name, (rows // tr,),
                  [(t.reshape(rows, cols), (tr, cols), row) for t in (w, g, m, v)],
                  [((rows, cols), F32, (tr, cols), row, ())] * (4 if pass_g else 3), sem=("parallel",),
                  after=after)
    return [o.reshape(shape) for o in outs]


MESH_IDS = pl.DeviceIdType.MESH
ANY = pl.BlockSpec(memory_space=pl.ANY)


def _place():
    x, y, c = lax.axis_index("x"), lax.axis_index("y"), lax.axis_index("c")
    chips = [(1 - x, y), (x, 1 - y), (1 - x, 1 - y)]
    return x, y, c, 2 * x + y, chips


def _chip_index():
    return 2 * lax.axis_index("x") + lax.axis_index("y")


HBM = pl.BlockSpec(memory_space=pltpu.HBM)
SEM = pl.BlockSpec(memory_space=pltpu.SEMAPHORE)
DATAFLOW = pltpu.SideEffectType.DATAFLOW_SIDE_EFFECTING
TOKEN = jax.ShapeDtypeStruct((8, LANES), F32)


def _in_hbm(t):
    return pltpu.with_memory_space_constraint(t, pltpu.HBM)


def _place_shard(w, l, dtype, name, after=()):
    _, r, cols = w.shape
    tr = _pick(r, max(16, (1 << 18) // cols // 16 * 16), 16) if r % 16 == 0 else r

    def body(w_ref, *rest):
        rest[-1][...] = w_ref[...].astype(dtype)

    return pl.pallas_call(
        body, name=name, grid=(r // tr,),
        in_specs=[pl.BlockSpec((None, tr, cols), lambda i: (l, i, 0))] + [ANY] * len(after),
        out_specs=pl.BlockSpec((None, tr, cols), lambda i: (_chip_index(), i, 0)),
        out_shape=jax.ShapeDtypeStruct((N_CHIPS, r, cols), dtype),
        compiler_params=_params(("parallel",)),
    )(w, *after)


def _my_rows(ref_or_shape_rows, c):
    r = ref_or_shape_rows
    if r % 32 == 0:
        return pl.ds(c * (r // 2), r // 2), pl.ds((1 - c) * (r // 2), r // 2), True
    return pl.ds(0, r), pl.ds(0, r), False


def _gather_copies(bufs, send_sems, recv_sems):
    x, y, c, me, chips = _place()
    out = []
    for i, buf in enumerate(bufs):
        mine, _, _ = _my_rows(buf.shape[1], c)
        for j, (cx, cy) in enumerate(chips):
            def rcopy(slab, i=i, j=j, cx=cx, cy=cy):
                return pltpu.make_async_remote_copy(src_ref=slab, dst_ref=slab, send_sem=send_sems.at[3 * i + j],
                                                    recv_sem=recv_sems.at[3 * i + j], device_id=(cx, cy, c),
                                                    device_id_type=MESH_IDS)
            out.append((rcopy(buf.at[me, mine]), rcopy(buf.at[2 * cx + cy, mine])))
    return out


def _gather_start(xs, name, after=()):
    n = len(xs)
    n_after = len(after)

    def body(*refs):
        refs = refs[n + n_after:]
        send_sems, recv_sems = refs[0], refs[1]
        bufs = refs[2:n + 2]
        token = refs[n + 2]
        for out_going, _ in _gather_copies(bufs, send_sems, recv_sems):
            out_going.start()
        token[...] = jnp.zeros_like(token)

    res = pl.pallas_call(
        body, name=name, in_specs=[HBM] * n + [ANY] * n_after,
        out_specs=[SEM, SEM] + [HBM] * n + [pl.BlockSpec(memory_space=pltpu.VMEM)],
        out_shape=[pltpu.SemaphoreType.DMA((3 * n,)), pltpu.SemaphoreType.DMA((3 * n,))]
        + [pltpu.HBM(t.shape, t.dtype) for t in xs] + [TOKEN],
        input_output_aliases={i: i + 2 for i in range(n)},
        compiler_params=pltpu.CompilerParams(has_side_effects=DATAFLOW),
    )(*[_in_hbm(t) for t in xs], *after)
    return res[0], res[1], res[2:2 + n], res[2 + n]


def _gather_wait(bufs, send_sems, recv_sems, after, name):
    n = len(bufs)

    def body(*refs):
        b_refs = refs[:n]
        s_sems, r_sems = refs[n], refs[n + 1]
        for out_going, in_coming in _gather_copies(b_refs, s_sems, r_sems):
            out_going.wait_send()
            in_coming.wait_recv()

    return pl.pallas_call(
        body, name=name, in_specs=[HBM] * n + [SEM, SEM, ANY], out_specs=[HBM] * n,
        out_shape=[pltpu.HBM(t.shape, t.dtype) for t in bufs],
        input_output_aliases={i: i for i in range(n)},
        compiler_params=pltpu.CompilerParams(has_side_effects=DATAFLOW),
    )(*bufs, send_sems, recv_sems, after)


def _gather_forward(bufs, name):
    idx = [i for i, t in enumerate(bufs) if t.shape[1] % 32 == 0]
    xs = [bufs[i] for i in idx]
    n = len(xs)

    def body(*refs):
        o_refs = refs[n:2 * n]
        send_sems, recv_sems = refs[2 * n:]
        x, y, c, _, chips = _place()
        copies = []
        for i in range(n):
            mine, theirs, _ = _my_rows(xs[i].shape[1], c)
            for j, (cx, cy) in enumerate(chips):
                def rcopy(slab, i=i, j=j):
                    return pltpu.make_async_remote_copy(src_ref=slab, dst_ref=slab, send_sem=send_sems.at[3 * i + j],
                                                        recv_sem=recv_sems.at[3 * i + j], device_id=(x, y, 1 - c),
                                                        device_id_type=MESH_IDS)
                copies.append((rcopy(o_refs[i].at[2 * cx + cy, mine]), rcopy(o_refs[i].at[2 * cx + cy, theirs])))
                copies[-1][0].start()
        for out_going, in_coming in copies:
            out_going.wait_send()
            in_coming.wait_recv()

    res = pl.pallas_call(
        body, name=name, in_specs=[ANY] * n, out_specs=[ANY] * n,
        out_shape=[jax.ShapeDtypeStruct(t.shape, t.dtype) for t in xs],
        input_output_aliases={i: i for i in range(n)},
        scratch_shapes=[pltpu.SemaphoreType.DMA((3 * n,)), pltpu.SemaphoreType.DMA((3 * n,))],
        compiler_params=pltpu.CompilerParams(has_side_effects=True),
    )(*xs)
    out = list(bufs)
    for i, t in zip(idx, res):
        out[i] = t
    return out


def _half_tile(rh, cols):
    return _pick(rh, max(16, (1 << 18) // cols // 16 * 16), 16)


N_LAND = 7


def _scatter_copies(p_refs, r_refs, send_sems, recv_sems):
    x, y, c, me, chips = _place()
    outgoing, incoming = [], []
    for i, (p, land) in enumerate(zip(p_refs, r_refs)):
        rh = p.shape[1] // 2

        def copy(src, slot, k_send, k_recv, to, i=i, land=land):
            return pltpu.make_async_remote_copy(
                src_ref=src, dst_ref=land.at[slot], send_sem=send_sems.at[N_LAND * i + k_send],
                recv_sem=recv_sems.at[N_LAND * i + k_recv], device_id=to, device_id_type=MESH_IDS)

        for r, (cx, cy) in enumerate(chips):
            for k in range(2):
                outgoing.append(copy(p.at[2 * cx + cy, pl.ds(k * rh, rh)], 2 * r + c, 2 * r + k, 2 * r + c,
                                     (cx, cy, k)))
                incoming.append(copy(land.at[2 * r + k], 2 * r + k, 2 * r + k, 2 * r + k, (x, y, c)))
        outgoing.append(copy(p.at[me, pl.ds((1 - c) * rh, rh)], 6, 6, 6, (x, y, 1 - c)))
        incoming.append(copy(land.at[6], 6, 6, 6, (x, y, c)))
    return outgoing, incoming


def _scatter_start(ps, name):
    n = len(ps)
    lands = [lax.empty((N_LAND, t.shape[1] // 2, t.shape[2]), t.dtype) for t in ps]

    def body(*refs):
        send_sems, recv_sems = refs[2 * n], refs[2 * n + 1]
        p_refs = refs[2 * n + 2:3 * n + 2]
        r_refs = refs[3 * n + 2:4 * n + 2]
        token = refs[4 * n + 2]
        for cp in _scatter_copies(p_refs, r_refs, send_sems, recv_sems)[0]:
            cp.start()
        token[...] = jnp.zeros_like(token)

    res = pl.pallas_call(
        body, name=name, in_specs=[HBM] * (2 * n),
        out_specs=[SEM, SEM] + [HBM] * (2 * n) + [pl.BlockSpec(memory_space=pltpu.VMEM)],
        out_shape=[pltpu.SemaphoreType.DMA((N_LAND * n,)), pltpu.SemaphoreType.DMA((N_LAND * n,))]
        + [pltpu.HBM(t.shape, t.dtype) for t in list(ps) + lands] + [TOKEN],
        input_output_aliases={i: i + 2 for i in range(2 * n)},
        compiler_params=pltpu.CompilerParams(has_side_effects=DATAFLOW),
    )(*[_in_hbm(t) for t in list(ps) + lands])
    return res[0], res[1], res[2:2 + n], res[2 + n:2 + 2 * n], res[2 + 2 * n]


def _scatter_wait(ps, lands, send_sems, recv_sems, after, name):
    n = len(ps)
    after = tuple(after) if isinstance(after, (tuple, list)) else (after,)

    def body(*refs):
        p_refs, r_refs = refs[:n], refs[n:2 * n]
        s_sems, r_sems = refs[2 * n], refs[2 * n + 1]
        outgoing, incoming = _scatter_copies(p_refs, r_refs, s_sems, r_sems)
        for cp in outgoing:
            cp.wait_send()
        for cp in incoming:
            cp.wait_recv()

    res = pl.pallas_call(
        body, name=name, in_specs=[HBM] * (2 * n) + [SEM, SEM] + [ANY] * len(after), out_specs=[HBM] * (2 * n),
        out_shape=[pltpu.HBM(t.shape, t.dtype) for t in list(ps) + list(lands)],
        input_output_aliases={i: i for i in range(2 * n)},
        compiler_params=pltpu.CompilerParams(has_side_effects=DATAFLOW),
    )(*ps, *lands, send_sems, recv_sems, *after)
    return res[:n], res[n:]


def _reduce_own(p, rcv, acc, l, name):
    nchip, r, cols = p.shape
    rh = r // 2
    tr = _half_tile(rh, cols)
    nt = rh // tr

    def body(*refs):
        p_ref, r_ref = refs[:2]
        o_ref = refs[-1]
        total = p_ref[...].astype(F32)
        for j in range(N_LAND):
            total = total + r_ref[j].astype(F32)
        o_ref[...] = total

    mine = lambda i: lax.axis_index("c") * nt + i
    in_specs = [pl.BlockSpec((None, tr, cols), lambda i: (_chip_index(), mine(i), 0)),
                pl.BlockSpec((N_LAND, tr, cols), lambda i: (0, i, 0))]
    args = [p, rcv]
    aliases = {}
    if acc is not None:
        in_specs.append(ANY)
        args.append(acc)
        aliases = {2: 0}
    return pl.pallas_call(
        body, name=name, grid=(nt,), in_specs=in_specs,
        out_specs=pl.BlockSpec((None, tr, cols), lambda i: (l, mine(i), 0)),
        out_shape=jax.ShapeDtypeStruct((DEPTH, r, cols), F32), input_output_aliases=aliases,
        compiler_params=_params(("parallel",)),
    )(*args)


def _share_halves(fs, l):
    n = len(fs)

    def body(*refs):
        o_refs = refs[n:2 * n]
        send_sems, recv_sems = refs[2 * n:]
        x, y, c, _, _ = _place()

        def halves(i):
            rh = fs[i].shape[1] // 2
            return o_refs[i].at[l, pl.ds(c * rh, rh), :], o_refs[i].at[l, pl.ds((1 - c) * rh, rh), :]

        def copy(i, rows):
            return pltpu.make_async_remote_copy(src_ref=rows, dst_ref=rows, send_sem=send_sems.at[i],
                                                recv_sem=recv_sems.at[i], device_id=(x, y, 1 - c),
                                                device_id_type=MESH_IDS)

        for i in range(n):
            copy(i, halves(i)[0]).start()
        for i in range(n):
            mine, theirs = halves(i)
            copy(i, mine).wait_send()
            copy(i, theirs).wait_recv()

    return pl.pallas_call(
        body, name="grad_share_halves", in_specs=[ANY] * n, out_specs=[ANY] * n,
        out_shape=[jax.ShapeDtypeStruct(t.shape, t.dtype) for t in fs],
        input_output_aliases={i: i for i in range(n)},
        scratch_shapes=[pltpu.SemaphoreType.DMA((n,)), pltpu.SemaphoreType.DMA((n,))],
        compiler_params=pltpu.CompilerParams(has_side_effects=True),
    )(*fs)


def _reduce_scatter_begin(slices, l, tag):
    names = list(slices)
    send_sems, recv_sems, ps, lands, token = _scatter_start([slices[k] for k in names],
                                                            "grad_scatter_start_%s%d" % (tag, l))
    return dict(names=names, ps=ps, lands=lands, sems=(send_sems, recv_sems), token=token, l=l, tag=tag)


def _reduce_scatter_end(st, stacked, after):
    l = st['l']
    ps, rs = _scatter_wait(st['ps'], st['lands'], *st['sems'], after, "grad_scatter_wait_%s%d" % (st['tag'], l))
    fs = [_reduce_own(p, r, stacked.get(k), l, "grad_reduce_own_" + k)
          for k, p, r in zip(st['names'], ps, rs)]
    stacked.update(zip(st['names'], _share_halves(fs, l)))


def _all_reduce_small(packed, after=()):
    rows = packed.shape[0]
    rh = rows // 2
    tr = _pick(rh, 512, 8)

    def body(x_ref, *rest):
        o_ref, sib, chip_sums, send_sems, recv_sems = rest[len(after):]
        x, y, c, me, chips = _place()
        sibling = (x, y, 1 - c)
        mine, theirs = pl.ds(c * rh, rh), pl.ds((1 - c) * rh, rh)

        def copy(src, dst, k, to):
            return pltpu.make_async_remote_copy(src_ref=src, dst_ref=dst, send_sem=send_sems.at[k],
                                                recv_sem=recv_sems.at[k], device_id=to, device_id_type=MESH_IDS)

        def tiles(fn):
            @pl.loop(0, rh // tr)
            def _(t):
                fn(pl.ds(pl.multiple_of(t * tr, 8), tr))

        to_sibling = copy(x_ref.at[theirs], sib, 0, sibling)
        to_sibling.start()
        to_sibling.wait()

        def chip_sum(sl):
            chip_sums[me, sl, :] = x_ref.at[mine][sl, :] + sib[sl, :]
        tiles(chip_sum)

        out = [copy(chip_sums.at[me], chip_sums.at[me], 1 + j, (cx, cy, c)) for j, (cx, cy) in enumerate(chips)]
        for cp in out:
            cp.start()
        for j, (cx, cy) in enumerate(chips):
            copy(chip_sums.at[2 * cx + cy], chip_sums.at[2 * cx + cy], 1 + j, (x, y, c)).wait_recv()
        for cp in out:
            cp.wait_send()

        def total(sl):
            acc = chip_sums[0, sl, :]
            for s in range(1, N_CHIPS):
                acc = acc + chip_sums[s, sl, :]
            o_ref.at[mine][sl, :] = acc
        tiles(total)

        share = copy(o_ref.at[mine], o_ref.at[mine], 4, sibling)
        share.start()
        share.wait_send()
        copy(o_ref.at[theirs], o_ref.at[theirs], 4, sibling).wait_recv()

    vm = pl.BlockSpec(memory_space=pltpu.VMEM)
    return pl.pallas_call(
        body, name="all_reduce_small", in_specs=[vm] + [ANY] * len(after), out_specs=vm,
        out_shape=jax.ShapeDtypeStruct(packed.shape, F32),
        scratch_shapes=[pltpu.VMEM((rh, LANES), F32), pltpu.VMEM((N_CHIPS, rh, LANES), F32),
                        pltpu.SemaphoreType.DMA((5,)), pltpu.SemaphoreType.DMA((5,))],
        compiler_params=pltpu.CompilerParams(vmem_limit_bytes=VMEM_LIMIT, has_side_effects=True),
    )(packed, *after)


_WEIGHTS = ('norm_mix_g', 'w_in', 'qkv_conv_w', 'a_log', 'dt_bias', 'gdn_norm_g', 'w_branch_a', 'sgu_ln_g',
            'sgu_ln_b', 'sgu_w', 'sgu_b', 'w_branch_b', 'w_out', 'norm_ffn_g', 'w_up', 'ffn_conv_w', 'ffn_conv_b',
            'w_down', 'final_norm_g')


def _local_step(x, target, weights_of, final_norm_g, bwd_after=None, mid_layer=None, last_layer=None):
    lws, saves = [], []
    for l in range(DEPTH):
        lw, late = weights_of(l, x)
        x, s, lw = _layer_fwd(x, lw, late)
        lws.append(lw)
        saves.append(s)
    loss, dx, dx_bf, d_final = _loss_head(x, final_norm_g.reshape(1, -1), target)
    grads = [None] * DEPTH
    for l in reversed(range(DEPTH)):
        after = bwd_after(l) if bwd_after is not None else ()
        mid = (lambda g, dh2, l=l: mid_layer(l, g, dh2)) if mid_layer is not None else None
        last = (lambda g, dp, l=l: last_layer(l, g, dp)) if last_layer is not None else None
        dx, dx_bf, grads[l] = _layer_bwd(dx, dx_bf, lws[l], saves[l], after, mid, last)
    return loss, dx, grads, d_final


def _pack_small(grads, d_final):
    parts = [grads[l][k].reshape(-1) for l in range(DEPTH) for k in _SMALL_GRADS] + [d_final.reshape(-1)]
    flat = jnp.concatenate(parts)
    rows = -(-flat.size // (16 * LANES)) * 16
    return jnp.pad(flat, (0, rows * LANES - flat.size)).reshape(rows, LANES), [p.size for p in parts]


def _unpack_small(packed, grads, d_final):
    flat = packed.reshape(-1)
    out, off = [], 0
    for l in range(DEPTH):
        d = {}
        for k in _SMALL_GRADS:
            t = grads[l][k]
            d[k] = flat[off:off + t.size].reshape(t.shape)
            off += t.size
        out.append(d)
    return out, flat[off:off + d_final.size].reshape(d_final.shape)


def kernel(x, norm_mix_g, w_in, qkv_conv_w, a_log, dt_bias, gdn_norm_g, w_branch_a, sgu_ln_g, sgu_ln_b, sgu_w, sgu_b, w_branch_b, w_out, norm_ffn_g, w_up, ffn_conv_w, ffn_conv_b, w_down, final_norm_g, loss_target, m_norm_mix_g, m_w_in, m_qkv_conv_w, m_a_log, m_dt_bias, m_gdn_norm_g, m_w_branch_a, m_sgu_ln_g, m_sgu_ln_b, m_sgu_w, m_sgu_b, m_w_branch_b, m_w_out, m_norm_ffn_g, m_w_up, m_ffn_conv_w, m_ffn_conv_b, m_w_down, m_final_norm_g, v_norm_mix_g, v_w_in, v_qkv_conv_w, v_a_log, v_dt_bias, v_gdn_norm_g, v_w_branch_a, v_sgu_ln_g, v_sgu_ln_b, v_sgu_w, v_sgu_b, v_w_branch_b, v_w_out, v_norm_ffn_g, v_w_up, v_ffn_conv_w, v_ffn_conv_b, v_w_down, v_final_norm_g):
    w = dict(norm_mix_g=norm_mix_g, w_in=w_in, qkv_conv_w=qkv_conv_w, a_log=a_log, dt_bias=dt_bias,
             gdn_norm_g=gdn_norm_g, w_branch_a=w_branch_a, sgu_ln_g=sgu_ln_g, sgu_ln_b=sgu_ln_b, sgu_w=sgu_w,
             sgu_b=sgu_b, w_branch_b=w_branch_b, w_out=w_out, norm_ffn_g=norm_ffn_g, w_up=w_up,
             ffn_conv_w=ffn_conv_w, ffn_conv_b=ffn_conv_b, w_down=w_down, final_norm_g=final_norm_g)
    m = dict(norm_mix_g=m_norm_mix_g, w_in=m_w_in, qkv_conv_w=m_qkv_conv_w, a_log=m_a_log, dt_bias=m_dt_bias,
             gdn_norm_g=m_gdn_norm_g, w_branch_a=m_w_branch_a, sgu_ln_g=m_sgu_ln_g, sgu_ln_b=m_sgu_ln_b,
             sgu_w=m_sgu_w, sgu_b=m_sgu_b, w_branch_b=m_w_branch_b, w_out=m_w_out, norm_ffn_g=m_norm_ffn_g,
             w_up=m_w_up, ffn_conv_w=m_ffn_conv_w, ffn_conv_b=m_ffn_conv_b, w_down=m_w_down,
             final_norm_g=m_final_norm_g)
    v = dict(norm_mix_g=v_norm_mix_g, w_in=v_w_in, qkv_conv_w=v_qkv_conv_w, a_log=v_a_log, dt_bias=v_dt_bias,
             gdn_norm_g=v_gdn_norm_g, w_branch_a=v_w_branch_a, sgu_ln_g=v_sgu_ln_g, sgu_ln_b=v_sgu_ln_b,
             sgu_w=v_sgu_w, sgu_b=v_sgu_b, w_branch_b=v_w_branch_b, w_out=v_w_out, norm_ffn_g=v_norm_ffn_g,
             w_up=v_w_up, ffn_conv_w=v_ffn_conv_w, ffn_conv_b=v_ffn_conv_b, w_down=v_w_down,
             final_norm_g=v_final_norm_g)
    chip = _chip_index()

    in_flight = []
    for l in range(DEPTH):
        for tag, names in (("first", _FIRST), ("rest", _REST)):
            earlier = (in_flight[-1][3],) if in_flight else ()
            placed = [_place_shard(w[k], l, BF16 if k in _BIG else F32, "place_" + k, earlier) for k in names]
            in_flight.append(_gather_start(placed, "gather_start_%s_%d" % (tag, l), earlier))

    def arrived(l, tag, names, after):
        send_sems, recv_sems, bufs, _ = in_flight[2 * l + (tag == "rest")]
        bufs = _gather_wait(bufs, send_sems, recv_sems, after, "gather_wait_%s_%d" % (tag, l))
        return dict(zip(names, _gather_forward(bufs, "gather_forward_" + tag)))

    def weights_of(l, x_in):
        first = arrived(l, "first", _FIRST, in_flight[-1][3] if l == 0 else x_in)
        return (_layer_weights_first(l, first, w),
                lambda p: _layer_weights_rest(arrived(l, "rest", _REST, p)))

    group_a, group_b = ('w_up', 'w_down'), ('w_in', 'w_branch_a', 'w_branch_b', 'w_out')
    big_g = {}
    pending = {'a': None, 'b': None}

    def finish(tag, after):
        if pending[tag] is not None:
            _reduce_scatter_end(pending[tag], big_g, after)
            pending[tag] = None

    def bwd_after(l):
        return () if pending['b'] is None else (pending['b']['token'],)

    def mid_layer(l, g, dh2):
        finish('b', dh2)
        pending['a'] = _reduce_scatter_begin(_big_grad_slices(g, group_a), l, 'a')
        return (pending['a']['token'],)

    def last_layer(l, g, dp):
        finish('a', dp)
        pending['b'] = _reduce_scatter_begin(_big_grad_slices(g, group_b), l, 'b')
        return (pending['b']['token'],)

    loss, grad_x, grads, d_final = _local_step(x[0], loss_target[0], weights_of, w['final_norm_g'], bwd_after,
                                               mid_layer, last_layer)
    loss = lax.psum(loss[0, 0], ("x", "y", "c"))
    packed, _ = _pack_small(grads, d_final)
    reduced_small = _all_reduce_small(packed, (pending['b']['token'],))
    small_g, d_final = _unpack_small(reduced_small, grads, d_final)
    deltas, new_m, new_v, g_out = {}, {}, {}, {}
    for k in group_a:
        deltas[k], new_m[k], new_v[k], g_out[k] = _adamw(w[k], big_g[k].reshape(w[k].shape), m[k], v[k],
                                                         "adamw_" + k, (pending['b']['token'],), pass_g=True)
    finish('b', (reduced_small,) + tuple(deltas[k] for k in group_a))

    def stack(k):
        return jnp.stack([small_g[l][k] for l in range(DEPTH)])

    nd = 2 * GDN_HEADS
    for k in ('norm_mix_g', 'gdn_norm_g', 'sgu_ln_g', 'sgu_ln_b', 'norm_ffn_g', 'ffn_conv_b'):
        g_out[k] = stack(k).reshape(w[k].shape)
    g_out['sgu_w'] = stack('sgu_w')
    g_out['a_log'] = stack('a_log')[:, 0, :nd].reshape(w['a_log'].shape)
    g_out['dt_bias'] = stack('dt_bias')[:, 0, :nd].reshape(w['dt_bias'].shape)
    g_out['sgu_b'] = jnp.swapaxes(stack('sgu_bt')[:, :, :SGU_GROUPS], 1, 2)
    for k in ('qkv_conv_w', 'ffn_conv_w'):
        full = stack(k)
        width = w[k].shape[-1]
        g_out[k] = lax.dynamic_slice_in_dim(full, chip * width, width, axis=2)
    g_out['final_norm_g'] = d_final.reshape(w['final_norm_g'].shape)

    for k in group_b:
        deltas[k], new_m[k], new_v[k], g_out[k] = _adamw(w[k], big_g[k].reshape(w[k].shape), m[k], v[k],
                                                         "adamw_" + k, pass_g=True)
    for k in _WEIGHTS:
        if k not in _BIG:
            deltas[k], new_m[k], new_v[k] = _adamw(w[k], g_out[k], m[k], v[k], "adamw_" + k)
    return (loss, grad_x[None], *[g_out[k] for k in _WEIGHTS], *[deltas[k] for k in _WEIGHTS],
            *[new_m[k] for k in _WEIGHTS], *[new_v[k] for k in _WEIGHTS])
```

```python
import functools
import math

import jax
import jax.numpy as jnp
from jax import lax
from jax.experimental import pallas as pl
from jax.experimental.pallas import tpu as pltpu

F32 = jnp.float32
BF16 = jnp.bfloat16
PARTIAL = BF16

D_MODEL = 2048
SEQ = 2048
DEPTH = 4
GDN_HEADS = 8
HEAD_DIM = 128
GDN_CHUNK = 64
QKV_CONV = 5
SGU_GROUPS = 8
SGU_DIM = 128
SGU_BLOCK = 128
D_FF = 5632
FFN_CONV = 3
NORM_EPS = 1e-6
N_CHIPS = 4

ADAM_LR = 0.001
ADAM_B1 = 0.9
ADAM_B2 = 0.999
ADAM_EPS = 1e-08
ADAM_WD = 0.01
ADAM_STEP = 10

LANES = 128
VMEM_LIMIT = 56 * 1024 * 1024
MATMUL_TILE_BYTES = 46 * 1024 * 1024
AB_PAD = LANES


def _gw():
    return GDN_HEADS * HEAD_DIM


def _sw():
    return SGU_GROUPS * SGU_DIM


def _n_all():
    return 4 * _gw() + 2 * _sw() + 2 * D_MODEL + AB_PAD


def _pick(n, target, align=LANES):
    if n <= target:
        return n
    best = None
    t = align
    while t <= target:
        if n % t == 0:
            best = t
        t += align
    assert best is not None, (n, target, align)
    return best


def _params(sem=None):
    return pltpu.CompilerParams(dimension_semantics=sem, vmem_limit_bytes=VMEM_LIMIT)


def _dot(a, b, dims, hi):
    dn = (dims, ((), ()))
    if not hi:
        return lax.dot_general(a.astype(BF16), b.astype(BF16), dn, preferred_element_type=F32)
    a_hi, b_hi = a.astype(BF16), b.astype(BF16)
    a_lo = (a - a_hi.astype(F32)).astype(BF16)
    b_lo = (b - b_hi.astype(F32)).astype(BF16)
    d = lambda p, q: lax.dot_general(p, q, dn, preferred_element_type=F32)
    return d(a_hi, b_hi) + (d(a_hi, b_lo) + d(a_lo, b_hi))


def _make_mm(hi):
    @jax.custom_vjp
    def mm(a, b):
        return _dot(a, b, ((1,), (0,)), hi)

    @jax.custom_vjp
    def mm_nt(a, b):
        return _dot(a, b, ((1,), (1,)), hi)

    @jax.custom_vjp
    def mm_tn(a, b):
        return _dot(a, b, ((0,), (0,)), hi)

    mm.defvjp(lambda a, b: (mm(a, b), (a, b)), lambda r, g: (mm_nt(g, r[1]), mm_tn(r[0], g)))
    mm_nt.defvjp(lambda a, b: (mm_nt(a, b), (a, b)), lambda r, g: (mm(g, r[1]), mm_tn(g, r[0])))
    mm_tn.defvjp(lambda a, b: (mm_tn(a, b), (a, b)), lambda r, g: (mm_nt(r[1], g), mm(r[0], g)))
    return mm, mm_nt, mm_tn


_mm, _mm_nt, _mm_tn = _make_mm(False)
_mmh, _mmh_nt, _mmh_tn = _make_mm(True)


def _shift_rows_raw(x, s):
    if s == 0:
        return x
    n = x.shape[0]
    rolled = pltpu.roll(x, (-s) % n, 0)
    t = lax.broadcasted_iota(jnp.int32, x.shape, 0)
    ok = (t + s >= 0) & (t + s < n)
    return jnp.where(ok, rolled, 0.0)


@functools.partial(jax.custom_vjp, nondiff_argnums=(1,))
def _shift_rows(x, s):
    return _shift_rows_raw(x, s)


_shift_rows.defvjp(lambda x, s: (_shift_rows_raw(x, s), None), lambda s, _, g: (_shift_rows_raw(g, -s),))


def _sigmoid(x):
    return 1.0 / (1.0 + jnp.exp(-x))


def _silu(x):
    return x * _sigmoid(x)


def _gelu(x):
    return 0.5 * x * (1.0 + jnp.tanh(math.sqrt(2.0 / math.pi) * (x + 0.044715 * x * x * x)))


def _softplus(x):
    return jnp.maximum(x, 0.0) + jnp.log(1.0 + jnp.exp(-jnp.abs(x)))


def _rms(x, g):
    return x * lax.rsqrt(jnp.mean(x * x, axis=-1, keepdims=True) + NORM_EPS) * g


def _conv_rows(x, taps):
    pad = len(taps) // 2
    acc = None
    for j, w in enumerate(taps):
        term = _shift_rows(x, j - pad) * w
        acc = term if acc is None else acc + term
    return acc


@jax.custom_vjp
def _inv_unit(mats):
    n = mats[0].shape[0]
    eye = (lax.broadcasted_iota(jnp.int32, (n, n), 0) == lax.broadcasted_iota(jnp.int32, (n, n), 1)).astype(F32)
    ps = [eye - a for a in mats]
    aks = list(mats)
    for _ in range(int(math.log2(n)) - 1):
        aks = [_mmh(ak, ak) for ak in aks]
        ps = [p + _mmh(p, ak) for p, ak in zip(ps, aks)]
    return ps


def _inv_unit_fwd(mats):
    ts = _inv_unit(mats)
    return ts, ts


def _inv_unit_bwd(ts, gs):
    inner = [_mmh_nt(g, t) for g, t in zip(gs, ts)]
    return ([-_mmh_tn(t, m) for t, m in zip(ts, inner)],)


_inv_unit.defvjp(_inv_unit_fwd, _inv_unit_bwd)


def _tiled(fn, name, grid, ins, outs, sem=None, after=()):
    n_in = len(ins)

    def body(*refs):
        vals = [r[...] for r in refs[:n_in]]
        res = fn(*vals)
        if not isinstance(res, (tuple, list)):
            res = (res,)
        for o_ref, r, spec in zip(refs[n_in + len(after):], res, outs):
            acc = spec[4]
            if not acc:
                o_ref[...] = r.astype(o_ref.dtype)
            else:
                first = functools.reduce(jnp.logical_and, [pl.program_id(a) == 0 for a in acc])

                @pl.when(first)
                def _(o_ref=o_ref, r=r):
                    o_ref[...] = r.astype(o_ref.dtype)

                @pl.when(jnp.logical_not(first))
                def _(o_ref=o_ref, r=r):
                    o_ref[...] += r.astype(o_ref.dtype)

    return pl.pallas_call(
        body, name=name, grid=grid,
        in_specs=[pl.BlockSpec(b, m) for _, b, m in ins] + [pl.BlockSpec(memory_space=pl.ANY)] * len(after),
        out_specs=[pl.BlockSpec(s[2], s[3]) for s in outs],
        out_shape=[jax.ShapeDtypeStruct(s[0], s[1]) for s in outs],
        compiler_params=_params(sem),
    )(*[a for a, _, _ in ins], *after)


def _matmul(a, b, mode, out_dtype, name, add=None, a_split=1, b_split=1, o_split=1, after=()):
    def dims2(x, split):
        return (x.shape[-2], x.shape[-1] * split)

    ar, ac = dims2(a, a_split)
    br, bc = dims2(b, b_split)
    if mode == 'nn':
        M, K, N = ar, ac, bc
        assert br == K
    elif mode == 'nt':
        M, K, N = ar, ac, br
        assert bc == K
    else:
        K, M, N = ar, ac, bc
        assert br == K
    tn = _pick(N // max(o_split, b_split if mode != 'nt' else 1), 1408)
    tk = _pick(K // max(a_split if mode != 'tn' else 1, b_split if mode == 'nt' else 1),
               {'nt': 3584, 'nn': 2816, 'tn': 2048}[mode])

    def tile_bytes(rows):
        out_b = rows * tn * (jnp.dtype(out_dtype).itemsize + (4 if add is not None else 0))
        return 2 * (2 * rows * tk + 2 * tn * tk + out_b) + (4 * rows * tn if K > tk else 0)

    m_unit = M // (a_split if mode == 'tn' else 1)
    tm = _pick(m_unit, 2048)
    if tile_bytes(tm) > MATMUL_TILE_BYTES:
        tm = _pick(m_unit, 1024)
    gm, gn, gk = M // tm, N // tn, K // tk

    def col_map(split, total_cols, tile):
        per = total_cols // split // tile

        def f(r, c):
            return (c // per, r, c % per) if split > 1 else (r, c)
        return f

    if mode == 'nn':
        a_idx = col_map(a_split, K, tk)
        b_idx = col_map(b_split, N, tn)
        a_spec = pl.BlockSpec(((None,) if a_split > 1 else ()) + (tm, tk), lambda m, n, k: a_idx(m, k))
        b_spec = pl.BlockSpec(((None,) if b_split > 1 else ()) + (tk, tn), lambda m, n, k: b_idx(k, n))
        dn = ((1,), (0,))
    elif mode == 'nt':
        a_idx = col_map(a_split, K, tk)
        b_idx = col_map(b_split, K, tk)
        a_spec = pl.BlockSpec(((None,) if a_split > 1 else ()) + (tm, tk), lambda m, n, k: a_idx(m, k))
        b_spec = pl.BlockSpec(((None,) if b_split > 1 else ()) + (tn, tk), lambda m, n, k: b_idx(n, k))
        dn = ((1,), (1,))
    else:
        a_idx = col_map(a_split, M, tm)
        b_idx = col_map(b_split, N, tn)
        a_spec = pl.BlockSpec(((None,) if a_split > 1 else ()) + (tk, tm), lambda m, n, k: a_idx(k, m))
        b_spec = pl.BlockSpec(((None,) if b_split > 1 else ()) + (tk, tn), lambda m, n, k: b_idx(k, n))
        dn = ((0,), (0,))
    o_idx = col_map(o_split, N, tn)
    o_block = ((None,) if o_split > 1 else ()) + (tm, tn)
    o_spec = pl.BlockSpec(o_block, lambda m, n, k: o_idx(m, n))
    o_shape = ((o_split, M, N // o_split) if o_split > 1 else (M, N))
    in_specs = [a_spec, b_spec]
    args = [a, b]
    if add is not None:
        in_specs.append(pl.BlockSpec((tm, tn), lambda m, n, k: (m, n)))
        args.append(add)
    n_lead = len(args)
    for t in after:
        in_specs.append(pl.BlockSpec(memory_space=pl.ANY))
        args.append(t)

    def body(*refs):
        a_ref, b_ref = refs[0], refs[1]
        add_ref = refs[2] if add is not None else None
        o_ref = refs[n_lead + len(after)]
        part = lax.dot_general(a_ref[...].astype(BF16), b_ref[...].astype(BF16), (dn, ((), ())),
                               preferred_element_type=F32)

        def finish(total):
            if add_ref is not None:
                total = total + add_ref[...]
            o_ref[...] = total.astype(o_ref.dtype)

        if gk == 1:
            finish(part)
        else:
            acc_ref = refs[-1]
            k = pl.program_id(2)

            @pl.when(k == 0)
            def _():
                acc_ref[...] = part

            @pl.when(jnp.logical_and(k > 0, k < gk - 1))
            def _():
                acc_ref[...] += part

            @pl.when(k == gk - 1)
            def _():
                finish(acc_ref[...] + part)

    return pl.pallas_call(
        body, name=name, grid=(gm, gn, gk), in_specs=in_specs, out_specs=o_spec,
        out_shape=jax.ShapeDtypeStruct(o_shape, out_dtype),
        scratch_shapes=([pltpu.VMEM((tm, tn), F32)] if gk > 1 else []),
        compiler_params=_params(("parallel", "parallel", "arbitrary")),
    )(*args)


def _row_tile():
    return _pick(SEQ, 256, 8)


def _rms_fwd(x, g, name):
    tm = _row_tile()
    (h,) = _tiled(lambda xv, gv: _rms(xv, gv), name, (SEQ // tm,),
                  [(x, (tm, D_MODEL), lambda i: (i, 0)), (g, (1, D_MODEL), lambda i: (0, 0))],
                  [((SEQ, D_MODEL), BF16, (tm, D_MODEL), lambda i: (i, 0), ())])
    return h


def _rms_bwd(x, g, dh, dres, name):
    tm = _row_tile()

    def fn(xv, gv, dhv, drv):
        _, vjp = jax.vjp(_rms, xv, gv)
        dx, dg = vjp(dhv)
        dx = dx + drv
        return dx, dx, dg

    row = lambda i: (i, 0)
    return _tiled(fn, name, (SEQ // tm,),
                  [(x, (tm, D_MODEL), row), (g, (1, D_MODEL), lambda i: (0, 0)),
                   (dh, (tm, D_MODEL), row), (dres, (tm, D_MODEL), row)],
                  [((SEQ, D_MODEL), F32, (tm, D_MODEL), row, ()),
                   ((SEQ, D_MODEL), BF16, (tm, D_MODEL), row, ()),
                   ((1, D_MODEL), F32, (1, D_MODEL), lambda i: (0, 0), (0,))])


def _loss_head(x, g, target):
    tm = _row_tile()

    def fn(xv, gv, tv):
        y, vjp = jax.vjp(_rms, xv, gv)
        err = y - tv
        part = 0.5 * jnp.sum(jnp.sum(err * err, axis=1, keepdims=True), axis=0, keepdims=True) / D_MODEL
        dx, dg = vjp(err / D_MODEL)
        return part, dx, dx, dg

    row = lambda i: (i, 0)
    one = lambda i: (0, 0)
    return _tiled(fn, "loss_head", (SEQ // tm,),
                  [(x, (tm, D_MODEL), row), (g, (1, D_MODEL), one), (target, (tm, D_MODEL), row)],
                  [((1, 1), F32, (1, 1), one, (0,)),
                   ((SEQ, D_MODEL), F32, (tm, D_MODEL), row, ()),
                   ((SEQ, D_MODEL), BF16, (tm, D_MODEL), row, ()),
                   ((1, D_MODEL), F32, (1, D_MODEL), one, (0,))])


def _p_col(width, index):
    return lambda tm: ((tm, width), lambda i: (i, index))


def _merge_fn(ga, gb, ta, tb):
    return _sigmoid(ga) * ta + _sigmoid(gb) * tb


def _merge_fwd(p, ta, tb):
    tm = _row_tile()
    d = D_MODEL
    ga_blk = 4 * _gw() + 2 * _sw()
    assert ga_blk % d == 0
    ia, ib = ga_blk // d, ga_blk // d + 1
    row = lambda i: (i, 0)
    (m,) = _tiled(_merge_fn, "merge_fwd", (SEQ // tm,),
                  [(p, (tm, d), lambda i: (i, ia)), (p, (tm, d), lambda i: (i, ib)),
                   (ta, (tm, d), row), (tb, (tm, d), row)],
                  [((SEQ, d), BF16, (tm, d), row, ())])
    return m


def _merge_bwd(p, ta, tb, dm):
    tm = _row_tile()
    d = D_MODEL
    ga_blk = 4 * _gw() + 2 * _sw()
    ia, ib = ga_blk // d, ga_blk // d + 1

    def fn(ga, gb, tav, tbv, dmv):
        _, vjp = jax.vjp(_merge_fn, ga, gb, tav, tbv)
        return vjp(dmv)

    row = lambda i: (i, 0)
    out = ((SEQ, d), BF16, (tm, d), row, ())
    return _tiled(fn, "merge_bwd", (SEQ // tm,),
                  [(p, (tm, d), lambda i: (i, ia)), (p, (tm, d), lambda i: (i, ib)),
                   (ta, (tm, d), row), (tb, (tm, d), row), (dm, (tm, d), row)],
                  [out, out, out, out])


def _gate_fn(ab, alog, dtb):
    lane = lax.broadcasted_iota(jnp.int32, ab.shape, 1)
    nd = 2 * GDN_HEADS
    g = -jnp.exp(alog) * _softplus(ab + dtb)
    beta = _sigmoid(ab)
    return jnp.where(lane < nd, g, jnp.where(lane < 2 * nd, beta, 0.0))


def _ab_index():
    off = 4 * _gw() + 2 * _sw() + 2 * D_MODEL
    assert off % AB_PAD == 0
    return off // AB_PAD


def _gate_fwd(p, alog, dtb):
    tm = _row_tile()
    iab = _ab_index()
    one = lambda i: (0, 0)
    (g,) = _tiled(_gate_fn, "gdn_gate_fwd", (SEQ // tm,),
                  [(p, (tm, AB_PAD), lambda i: (i, iab)), (alog, (1, AB_PAD), one), (dtb, (1, AB_PAD), one)],
                  [((SEQ, AB_PAD), F32, (tm, AB_PAD), lambda i: (i, 0), ())])
    return g


def _gate_bwd(p, alog, dtb, dg):
    tm = _row_tile()
    iab = _ab_index()
    one = lambda i: (0, 0)

    def fn(ab, al, db, dgv):
        _, vjp = jax.vjp(_gate_fn, ab, al, db)
        return vjp(dgv)

    return _tiled(fn, "gdn_gate_bwd", (SEQ // tm,),
                  [(p, (tm, AB_PAD), lambda i: (i, iab)), (alog, (1, AB_PAD), one), (dtb, (1, AB_PAD), one),
                   (dg, (tm, AB_PAD), lambda i: (i, 0))],
                  [((SEQ, AB_PAD), BF16, (tm, AB_PAD), lambda i: (i, 0), ()),
                   ((1, AB_PAD), F32, (1, AB_PAD), one, (0,)),
                   ((1, AB_PAD), F32, (1, AB_PAD), one, (0,))])


def _l2n(x):
    return x * lax.rsqrt(jnp.sum(x * x, axis=-1, keepdims=True) + NORM_EPS)


def _qkv_fn(xq, xk, xv, *taps):
    n = QKV_CONV
    q = _l2n(_silu(_conv_rows(xq, taps[0:n])))
    k = _l2n(_silu(_conv_rows(xk, taps[n:2 * n])))
    v = _silu(_conv_rows(xv, taps[2 * n:3 * n]))
    return q, k, v


def _qkv_specs(p, conv_w):
    hd, nh = HEAD_DIM, GDN_HEADS
    ins = [(p, (SEQ, hd), (lambda h, s=s: (0, s * nh + h))) for s in range(3)]
    ins += [(conv_w, (QKV_CONV, hd), (lambda h, s=s: (0, s * nh + h))) for s in range(3)]
    return ins


def _qkv_fwd(p, conv_w):
    hd, nh = HEAD_DIM, GDN_HEADS
    ins = _qkv_specs(p, conv_w)
    n_in = len(ins)
    out_spec = pl.BlockSpec((SEQ, hd), lambda h: (0, h))

    def body(*refs):
        xs = [r[...] for r in refs[:3]]
        taps = [refs[3 + s][j:j + 1, :] for s in range(3) for j in range(QKV_CONV)]
        q, k, v = _qkv_fn(*xs, *taps)
        refs[n_in][...] = q
        refs[n_in + 1][...] = k
        refs[n_in + 2][...] = v

    return pl.pallas_call(
        body, name="gdn_qkv_fwd", grid=(nh,),
        in_specs=[pl.BlockSpec(b, m) for _, b, m in ins], out_specs=[out_spec] * 3,
        out_shape=[jax.ShapeDtypeStruct((SEQ, nh * hd), F32)] * 3,
        compiler_params=_params(("parallel",)),
    )(*[a for a, _, _ in ins])


def _qkv_bwd(p, conv_w, dq, dk, dv):
    hd, nh = HEAD_DIM, GDN_HEADS
    ins = _qkv_specs(p, conv_w) + [(t, (SEQ, hd), lambda h: (0, h)) for t in (dq, dk, dv)]
    n_in = len(ins)

    def body(*refs):
        xs = [r[...] for r in refs[:3]]
        taps = [refs[3 + s][j:j + 1, :] for s in range(3) for j in range(QKV_CONV)]
        cts = tuple(r[...] for r in refs[6:9])
        _, vjp = jax.vjp(_qkv_fn, *xs, *taps)
        grads = vjp(cts)
        for s in range(3):
            refs[n_in + s][...] = grads[s].astype(BF16)
            for j in range(QKV_CONV):
                refs[n_in + 3 + s][j:j + 1, :] = jnp.sum(grads[3 + s * QKV_CONV + j], axis=0, keepdims=True)

    dx_spec = pl.BlockSpec((SEQ, hd), lambda h: (0, h))
    dw_spec = pl.BlockSpec((QKV_CONV, hd), lambda h: (0, h))
    outs = pl.pallas_call(
        body, name="gdn_qkv_bwd", grid=(nh,),
        in_specs=[pl.BlockSpec(b, m) for _, b, m in ins], out_specs=[dx_spec] * 3 + [dw_spec] * 3,
        out_shape=[jax.ShapeDtypeStruct((SEQ, nh * hd), BF16)] * 3
        + [jax.ShapeDtypeStruct((QKV_CONV, nh * hd), F32)] * 3,
        compiler_params=_params(("parallel",)),
    )(*[a for a, _, _ in ins])
    return outs[0:3], jnp.concatenate(outs[3:6], axis=1)


def _prep_fn(qs, ks, vs, gblk):
    nh = len(qs)
    c = qs[0].shape[0]
    scale = HEAD_DIM ** -0.5
    ii = lax.broadcasted_iota(jnp.int32, (c, c), 0)
    jj = lax.broadcasted_iota(jnp.int32, (c, c), 1)
    eye = (ii == jj).astype(F32)
    lane = lax.broadcasted_iota(jnp.int32, gblk.shape, 1)
    kk_t = [_mm_nt(k, k) for k in ks]
    qk_t = [_mm_nt(q, k) for q, k in zip(qs, ks)]
    chains = [(h, d) for h in range(nh) for d in range(2)]
    pre = []
    for h, d in chains:
        g = jnp.sum(jnp.where(lane == h + d * nh, gblk, 0.0), axis=1, keepdims=True)
        beta = jnp.sum(jnp.where(lane == h + (2 + d) * nh, gblk, 0.0), axis=1, keepdims=True)
        incl = (jj <= ii) if d == 0 else (jj >= ii)
        strict = (jj < ii) if d == 0 else (jj > ii)
        incl_t = (ii <= jj) if d == 0 else (ii >= jj)
        g_row = jnp.sum(eye * g, axis=0, keepdims=True)
        gc = jnp.sum(jnp.where(incl, g_row, 0.0), axis=1, keepdims=True)
        gc_row = jnp.sum(jnp.where(incl_t, g, 0.0), axis=0, keepdims=True)
        decay = jnp.where(incl, jnp.exp(jnp.where(incl, gc - gc_row, 0.0)), 0.0)
        a = jnp.where(strict, kk_t[h] * beta * decay, 0.0)
        pre.append((g, beta, gc, decay, incl, a))
    ts = _inv_unit([p[5] for p in pre])
    us = [_mmh(t, vs[h] * p[1]) for (h, d), p, t in zip(chains, pre, ts)]
    ws = [_mmh(t, ks[h] * (p[1] * jnp.exp(p[2]))) for (h, d), p, t in zip(chains, pre, ts)]
    outs = [[None, None] for _ in range(nh)]
    for (h, d), (g, beta, gc, decay, incl, a), u, w in zip(chains, pre, us, ws):
        qk = jnp.where(incl, qk_t[h] * (scale * decay), 0.0)
        g_last = jnp.sum(g, axis=0, keepdims=True)
        q_dec = qs[h] * (scale * jnp.exp(gc))
        k_dec = ks[h] * jnp.exp(g_last - gc)
        outs[h][d] = (u, w, q_dec, k_dec, qk, jnp.exp(g_last))
    return outs


def _n_chunks():
    return SEQ // GDN_CHUNK


def _gdn_prep_fwd(q, k, v, gates):
    c, hd, nh, nc = GDN_CHUNK, HEAD_DIM, GDN_HEADS, _n_chunks()
    gw = nh * hd
    row = lambda n: (n, 0)

    def body(q_ref, k_ref, v_ref, g_ref, u_ref, w_ref, qd_ref, kd_ref, qk_ref, dec_ref):
        heads = [slice(h * hd, (h + 1) * hd) for h in range(nh)]
        res = _prep_fn([q_ref[:, s] for s in heads], [k_ref[:, s] for s in heads], [v_ref[:, s] for s in heads],
                       g_ref[...])
        for h, cols in enumerate(heads):
            for d in range(2):
                u, w, qd, kd, qk, dec = res[h][d]
                u_ref[d, :, cols] = u
                w_ref[d, :, cols] = w
                qd_ref[d, :, cols] = qd
                kd_ref[d, :, cols] = kd
                qk_ref[d, h] = qk
                dec_ref[d, h:h + 1, :] = jnp.broadcast_to(dec, (1, LANES))

    wide = pl.BlockSpec((2, c, gw), lambda n: (0, n, 0))
    return pl.pallas_call(
        body, name="gdn_prep_fwd", grid=(nc,),
        in_specs=[pl.BlockSpec((c, gw), row)] * 3 + [pl.BlockSpec((c, LANES), row)],
        out_specs=[wide] * 4 + [pl.BlockSpec((2, nh, c, c), lambda n: (0, 0, n, 0)),
                                pl.BlockSpec((2, None, nh, LANES), lambda n: (0, n, 0, 0))],
        out_shape=[jax.ShapeDtypeStruct((2, SEQ, gw), F32)] * 4
        + [jax.ShapeDtypeStruct((2, nh, SEQ, c), F32), jax.ShapeDtypeStruct((2, nc, nh, LANES), F32)],
        compiler_params=_params(("parallel",)),
    )(q, k, v, gates)


def _gdn_prep_bwd(q, k, v, gates, cts):
    c, hd, nh, nc = GDN_CHUNK, HEAD_DIM, GDN_HEADS, _n_chunks()
    gw = nh * hd
    row = lambda n: (n, 0)

    def body(*refs):
        q_ref, k_ref, v_ref, g_ref = refs[:4]
        ct_refs = refs[4:16]
        dq_ref, dk_ref, dv_ref, dg_ref = refs[16:20]
        heads = [slice(h * hd, (h + 1) * hd) for h in range(nh)]
        _, vjp = jax.vjp(_prep_fn, [q_ref[:, s] for s in heads], [k_ref[:, s] for s in heads],
                         [v_ref[:, s] for s in heads], g_ref[...])
        cts_in = []
        for h, cols in enumerate(heads):
            per_dir = []
            for d in range(2):
                r = ct_refs[6 * d:6 * d + 6]
                per_dir.append((r[0][:, cols], r[1][:, cols], r[2][:, cols], r[3][:, cols], r[4][h],
                                r[5][h:h + 1, 0:1]))
            cts_in.append(per_dir)
        dqs, dks, dvs, dg = vjp(cts_in)
        for h, cols in enumerate(heads):
            dq_ref[:, cols] = dqs[h]
            dk_ref[:, cols] = dks[h]
            dv_ref[:, cols] = dvs[h]
        dg_ref[...] = dg

    one_dir = [pl.BlockSpec((c, gw), row)] * 4 + [pl.BlockSpec((nh, c, c), lambda n: (0, n, 0)),
                                                  pl.BlockSpec((None, nh, LANES), lambda n: (n, 0, 0))]
    return pl.pallas_call(
        body, name="gdn_prep_bwd", grid=(nc,),
        in_specs=[pl.BlockSpec((c, gw), row)] * 3 + [pl.BlockSpec((c, LANES), row)] + one_dir * 2,
        out_specs=[pl.BlockSpec((c, gw), row)] * 3 + [pl.BlockSpec((c, LANES), row)],
        out_shape=[jax.ShapeDtypeStruct((SEQ, gw), F32)] * 3 + [jax.ShapeDtypeStruct((SEQ, LANES), F32)],
        compiler_params=_params(("parallel",)),
    )(q, k, v, gates, *cts)


def _scan_steps(states, us, ws, qds, kds, qks, decs):
    w_s = [_mm(w, s) for w, s in zip(ws, states)]
    q_s = [_mm(qd, s) for qd, s in zip(qds, states)]
    v_new = [u - x for u, x in zip(us, w_s)]
    outs = [x + _mm(qk, vn) for x, qk, vn in zip(q_s, qks, v_new)]
    k_v = [_mm_tn(kd, vn) for kd, vn in zip(kds, v_new)]
    new_states = [s * dec + x for s, dec, x in zip(states, decs, k_v)]
    return new_states, outs


def _scan_operands(ins, chains, dec_lanes):
    cols = lambda h: slice(h * HEAD_DIM, (h + 1) * HEAD_DIM)
    wide = [[ins[6 * d + j][:, cols(h)] for d, h in chains] for j in range(4)]
    qks = [ins[6 * d + 4][h] for d, h in chains]
    decs = [ins[6 * d + 5][h:h + 1, dec_lanes] for d, h in chains]
    return (*wide, qks, decs)


def _scan_in_specs(chunk_of):
    c, hd, nh = GDN_CHUNK, HEAD_DIM, GDN_HEADS
    gw = nh * hd
    specs = []
    for d in range(2):
        f = chunk_of[d]
        specs += [pl.BlockSpec((None, c, gw), lambda n, d=d, f=f: (d, f(n), 0))] * 4
        specs += [pl.BlockSpec((None, nh, c, c), lambda n, d=d, f=f: (d, 0, f(n), 0)),
                  pl.BlockSpec((None, None, nh, LANES), lambda n, d=d, f=f: (d, f(n), 0, 0))]
    return specs


def _gdn_scan_fwd(u, w, qd, kd, qk, dec):
    c, hd, nh, nc = GDN_CHUNK, HEAD_DIM, GDN_HEADS, _n_chunks()
    gw = nh * hd
    chunk_of = (lambda n: n, lambda n: nc - 1 - n)

    def body(*refs):
        ins = refs[:12]
        o_refs = refs[12:14]
        st_refs = refs[14:16]
        s_ref = refs[16]

        @pl.when(pl.program_id(0) == 0)
        def _():
            s_ref[...] = jnp.zeros_like(s_ref)

        chains = [(d, h) for d in range(2) for h in range(nh)]
        states = [s_ref[d * nh + h] for d, h in chains]
        for (d, h), st in zip(chains, states):
            st_refs[d][h] = st
        new_states, outs = _scan_steps(states, *_scan_operands(ins, chains, slice(None)))
        for (d, h), st, o in zip(chains, new_states, outs):
            s_ref[d * nh + h] = st
            o_refs[d][:, h * hd:(h + 1) * hd] = o

    o_specs = [pl.BlockSpec((c, gw), lambda n, f=f: (f(n), 0)) for f in chunk_of]
    st_specs = [pl.BlockSpec((None, nh, hd, hd), lambda n, f=f: (f(n), 0, 0, 0)) for f in chunk_of]
    return pl.pallas_call(
        body, name="gdn_scan_fwd", grid=(nc,), in_specs=_scan_in_specs(chunk_of),
        out_specs=o_specs + st_specs,
        out_shape=[jax.ShapeDtypeStruct((SEQ, gw), F32)] * 2 + [jax.ShapeDtypeStruct((nc, nh, hd, hd), F32)] * 2,
        scratch_shapes=[pltpu.VMEM((2 * nh, hd, hd), F32)],
        compiler_params=_params(("arbitrary",)),
    )(*([u, w, qd, kd, qk, dec] * 2))


def _gdn_scan_bwd(u, w, qd, kd, qk, dec, st0, st1, do):
    c, hd, nh, nc = GDN_CHUNK, HEAD_DIM, GDN_HEADS, _n_chunks()
    gw = nh * hd
    chunk_of = (lambda n: nc - 1 - n, lambda n: n)

    def body(*refs):
        ins = refs[:12]
        st_in = refs[12:14]
        do_in = refs[14:16]
        outs = refs[16:28]
        ds_ref = refs[28]

        @pl.when(pl.program_id(0) == 0)
        def _():
            ds_ref[...] = jnp.zeros_like(ds_ref)

        chains = [(d, h) for d in range(2) for h in range(nh)]
        states = [st_in[d][h] for d, h in chains]
        _, vjp = jax.vjp(_scan_steps, states, *_scan_operands(ins, chains, slice(0, 1)))
        ct_state = [ds_ref[d * nh + h] for d, h in chains]
        ct_out = [do_in[d][:, h * hd:(h + 1) * hd] for d, h in chains]
        grads = vjp((ct_state, ct_out))
        for i, (d, h) in enumerate(chains):
            cols = slice(h * hd, (h + 1) * hd)
            du_r, dw_r, dqd_r, dkd_r, dqk_r, ddec_r = outs[6 * d:6 * d + 6]
            ds_ref[d * nh + h] = grads[0][i]
            du_r[:, cols] = grads[1][i]
            dw_r[:, cols] = grads[2][i]
            dqd_r[:, cols] = grads[3][i]
            dkd_r[:, cols] = grads[4][i]
            dqk_r[h] = grads[5][i]
            ddec_r[h:h + 1, :] = jnp.broadcast_to(grads[6][i], (1, LANES))

    st_specs = [pl.BlockSpec((None, nh, hd, hd), lambda n, f=f: (f(n), 0, 0, 0)) for f in chunk_of]
    do_specs = [pl.BlockSpec((c, gw), lambda n, f=f: (f(n), 0)) for f in chunk_of]
    out_specs, out_shape = [], []
    for f in chunk_of:
        out_specs += [pl.BlockSpec((c, gw), lambda n, f=f: (f(n), 0))] * 4
        out_specs += [pl.BlockSpec((nh, c, c), lambda n, f=f: (0, f(n), 0)),
                      pl.BlockSpec((None, nh, LANES), lambda n, f=f: (f(n), 0, 0))]
        out_shape += [jax.ShapeDtypeStruct((SEQ, gw), F32)] * 4
        out_shape += [jax.ShapeDtypeStruct((nh, SEQ, c), F32), jax.ShapeDtypeStruct((nc, nh, LANES), F32)]
    return pl.pallas_call(
        body, name="gdn_scan_bwd", grid=(nc,), in_specs=_scan_in_specs(chunk_of) + st_specs + do_specs,
        out_specs=out_specs, out_shape=out_shape,
        scratch_shapes=[pltpu.VMEM((2 * nh, hd, hd), F32)],
        compiler_params=_params(("arbitrary",)),
    )(*([u, w, qd, kd, qk, dec] * 2), st0, st1, do, do)


def _post_fn(o0, o1, z, gn):
    return _rms(o0 + o1, gn) * _silu(z)


def _gdn_post_fwd(o0, o1, p, gn):
    tm, hd, nh = _row_tile(), HEAD_DIM, GDN_HEADS
    zoff = 3 * nh
    blk = lambda i, h: (i, h)
    (ya,) = _tiled(_post_fn, "gdn_post_fwd", (SEQ // tm, nh),
                   [(o0, (tm, hd), blk), (o1, (tm, hd), blk),
                    (p, (tm, hd), lambda i, h: (i, zoff + h)), (gn, (1, hd), lambda i, h: (0, 0))],
                   [((SEQ, nh * hd), BF16, (tm, hd), blk, ())])
    return ya


def _gdn_post_bwd(o0, o1, p, gn, dya):
    tm, hd, nh = _row_tile(), HEAD_DIM, GDN_HEADS
    zoff = 3 * nh
    blk = lambda i, h: (i, h)

    def fn(a, b, z, g, dy):
        _, vjp = jax.vjp(_post_fn, a, b, z, g)
        do, _, dz, dg = vjp(dy.astype(F32))
        return do, dz, dg

    return _tiled(fn, "gdn_post_bwd", (SEQ // tm, nh),
                  [(o0, (tm, hd), blk), (o1, (tm, hd), blk),
                   (p, (tm, hd), lambda i, h: (i, zoff + h)), (gn, (1, hd), lambda i, h: (0, 0)),
                   (dya, (tm, hd), blk)],
                  [((SEQ, nh * hd), F32, (tm, hd), blk, ()),
                   ((SEQ, nh * hd), BF16, (tm, hd), blk, ()),
                   ((1, hd), F32, (1, hd), lambda i, h: (0, 0), (0, 1))],
                  sem=("arbitrary", "arbitrary"))


def _sgu_ln(v, lg, lb):
    gv = _gelu(v)
    mu = jnp.mean(gv, axis=-1, keepdims=True)
    cen = gv - mu
    var = jnp.mean(cen * cen, axis=-1, keepdims=True)
    return cen * lax.rsqrt(var + NORM_EPS) * lg + lb


def _sgu_mix(vn, w_ref, bt_ref):
    parts = []
    for g in range(SGU_GROUPS):
        cols = slice(g * SGU_DIM, (g + 1) * SGU_DIM)
        parts.append(_mm(w_ref[g], vn[:, cols]) + bt_ref[:, g:g + 1])
    return jnp.concatenate(parts, axis=1)


def _sgu_specs(p):
    sw, sb = _sw(), SGU_BLOCK
    uoff = 4 * _gw()
    assert uoff % sw == 0
    iu = uoff // sw
    return [(p, (sb, sw), lambda n: (n, iu)), (p, (sb, sw), lambda n: (n, iu + 1))]


def _sgu_fwd(p, lg, lb, w_s, b_t):
    sw, sb = _sw(), SGU_BLOCK
    ins = _sgu_specs(p)

    def body(u_ref, v_ref, lg_ref, lb_ref, w_ref, bt_ref, y_ref):
        vn = _sgu_ln(v_ref[...], lg_ref[...], lb_ref[...])
        y_ref[...] = (_gelu(u_ref[...]) * _sgu_mix(vn, w_ref, bt_ref)).astype(BF16)

    one2 = lambda n: (0, 0)
    return pl.pallas_call(
        body, name="sgu_fwd", grid=(SEQ // sb,),
        in_specs=[pl.BlockSpec(b, m) for _, b, m in ins]
        + [pl.BlockSpec((1, sw), one2), pl.BlockSpec((1, sw), one2),
           pl.BlockSpec((SGU_GROUPS, sb, sb), lambda n: (0, 0, 0)), pl.BlockSpec((sb, LANES), one2)],
        out_specs=pl.BlockSpec((sb, sw), lambda n: (n, 0)),
        out_shape=jax.ShapeDtypeStruct((SEQ, sw), BF16),
        compiler_params=_params(("parallel",)),
    )(p, p, lg, lb, w_s, b_t)


def _sgu_bwd(p, lg, lb, w_s, b_t, dy):
    sw, sb, ng = _sw(), SGU_BLOCK, SGU_GROUPS
    ins = _sgu_specs(p)

    def body(u_ref, v_ref, lg_ref, lb_ref, w_ref, bt_ref, dy_ref, du_ref, dv_ref, dlg_ref, dlb_ref, dw_ref, dbt_ref):
        first = pl.program_id(0) == 0
        u, v = u_ref[...], v_ref[...]
        gu, gelu_vjp = jax.vjp(_gelu, u)
        vn, ln_vjp = jax.vjp(_sgu_ln, v, lg_ref[...], lb_ref[...])
        s = _sgu_mix(vn, w_ref, bt_ref)
        dyv = dy_ref[...].astype(F32)
        ds = dyv * gu
        (du,) = gelu_vjp(dyv * s)
        lane = lax.broadcasted_iota(jnp.int32, (sb, LANES), 1)
        dvn_parts = []
        dbt = jnp.zeros((sb, LANES), F32)
        for g in range(ng):
            cols = slice(g * SGU_DIM, (g + 1) * SGU_DIM)
            ds_g = ds[:, cols]
            dw_g = _mm_nt(ds_g, vn[:, cols])
            dvn_parts.append(_mm_tn(w_ref[g], ds_g))
            dbt = dbt + jnp.where(lane == g, jnp.sum(ds_g, axis=1, keepdims=True), 0.0)

            @pl.when(first)
            def _(g=g, dw_g=dw_g):
                dw_ref[g] = dw_g

            @pl.when(jnp.logical_not(first))
            def _(g=g, dw_g=dw_g):
                dw_ref[g] += dw_g

        dv, dlg, dlb = ln_vjp(jnp.concatenate(dvn_parts, axis=1))
        du_ref[...] = du.astype(BF16)
        dv_ref[...] = dv.astype(BF16)

        @pl.when(first)
        def _():
            dlg_ref[...] = dlg
            dlb_ref[...] = dlb
            dbt_ref[...] = dbt

        @pl.when(jnp.logical_not(first))
        def _():
            dlg_ref[...] += dlg
            dlb_ref[...] += dlb
            dbt_ref[...] += dbt

    one2 = lambda n: (0, 0)
    row = pl.BlockSpec((sb, sw), lambda n: (n, 0))
    return pl.pallas_call(
        body, name="sgu_bwd", grid=(SEQ // sb,),
        in_specs=[pl.BlockSpec(b, m) for _, b, m in ins]
        + [pl.BlockSpec((1, sw), one2), pl.BlockSpec((1, sw), one2),
           pl.BlockSpec((ng, sb, sb), lambda n: (0, 0, 0)), pl.BlockSpec((sb, LANES), one2), row],
        out_specs=[row, row, pl.BlockSpec((1, sw), one2), pl.BlockSpec((1, sw), one2),
                   pl.BlockSpec((ng, sb, sb), lambda n: (0, 0, 0)), pl.BlockSpec((sb, LANES), one2)],
        out_shape=[jax.ShapeDtypeStruct((SEQ, sw), BF16)] * 2 + [jax.ShapeDtypeStruct((1, sw), F32)] * 2
        + [jax.ShapeDtypeStruct((ng, sb, sb), F32), jax.ShapeDtypeStruct((sb, LANES), F32)],
        compiler_params=_params(("arbitrary",)),
    )(p, p, lg, lb, w_s, b_t, dy)


def _ffn_fn(xg, xv, bg, bv, *taps):
    n = FFN_CONV
    cg = _conv_rows(xg, taps[0:n]) + bg
    cv = _conv_rows(xv, taps[n:2 * n]) + bv
    return _silu(cg) * cv


def _ffn_tile():
    return _pick(D_FF, 256)


def _ffn_specs(up, conv_w, conv_b):
    tc = _ffn_tile()
    nt = D_FF // tc
    ins = [(up, (None, SEQ, tc), (lambda j, s=s: (s, 0, j))) for s in range(2)]
    ins += [(conv_b, (1, tc), (lambda j, s=s: (0, s * nt + j))) for s in range(2)]
    ins += [(conv_w, (FFN_CONV, tc), (lambda j, s=s: (0, s * nt + j))) for s in range(2)]
    return ins


def _ffn_act_fwd(up, conv_w, conv_b):
    tc = _ffn_tile()
    ins = _ffn_specs(up, conv_w, conv_b)

    def body(xg, xv, bg, bv, wg, wv, o_ref):
        taps = [w[j:j + 1, :] for w in (wg, wv) for j in range(FFN_CONV)]
        o_ref[...] = _ffn_fn(xg[...], xv[...], bg[...], bv[...], *taps).astype(BF16)

    return pl.pallas_call(
        body, name="ffn_act_fwd", grid=(D_FF // tc,),
        in_specs=[pl.BlockSpec(b, m) for _, b, m in ins],
        out_specs=pl.BlockSpec((SEQ, tc), lambda j: (0, j)),
        out_shape=jax.ShapeDtypeStruct((SEQ, D_FF), BF16),
        compiler_params=_params(("parallel",)),
    )(*[a for a, _, _ in ins])


def _ffn_act_bwd(up, conv_w, conv_b, dact):
    tc = _pick(D_FF, LANES)
    nt = D_FF // tc
    ins = [(up, (None, SEQ, tc), (lambda j, s=s: (s, 0, j))) for s in range(2)]
    ins += [(conv_b, (1, tc), (lambda j, s=s: (0, s * nt + j))) for s in range(2)]
    ins += [(conv_w, (FFN_CONV, tc), (lambda j, s=s: (0, s * nt + j))) for s in range(2)]
    ins += [(dact, (SEQ, tc), lambda j: (0, j))]
    rows_out = _pick(SEQ, 128, 16)
    halo = 16
    pad = FFN_CONV // 2
    assert halo >= 2 * pad and SEQ // rows_out >= 2

    def body(xg, xv, bg, bv, wg, wv, dact_ref, dup_ref, dwg_ref, dwv_ref, dbg_ref, dbv_ref):
        taps = [[w[j:j + 1, :] for j in range(FFN_CONV)] for w in (wg, wv)]
        bias = (bg[...], bv[...])
        dw = [[jnp.zeros((1, tc), F32) for _ in range(FFN_CONV)] for _ in range(2)]
        db = [jnp.zeros((1, tc), F32) for _ in range(2)]
        for i in range(SEQ // rows_out):
            lo, hi = max(i * rows_out - halo, 0), min((i + 1) * rows_out + halo, SEQ)
            inner = slice(i * rows_out - lo, (i + 1) * rows_out - lo)
            t = lo + lax.broadcasted_iota(jnp.int32, (hi - lo, tc), 0)

            def shift(a, s, t=t, n=hi - lo):
                if s == 0:
                    return a
                return jnp.where((t + s >= 0) & (t + s < SEQ), pltpu.roll(a, (-s) % n, 0), 0.0)

            x = (xg[lo:hi, :], xv[lo:hi, :])
            da = dact_ref[lo:hi, :].astype(F32)
            c = [sum(taps[s][j] * shift(x[s], j - pad) for j in range(FFN_CONV)) + bias[s] for s in range(2)]
            sg = _sigmoid(c[0])
            dc = (da * c[1] * (sg * (1.0 + c[0] * (1.0 - sg))), da * (c[0] * sg))
            for s in range(2):
                dx = sum(taps[s][j] * shift(dc[s], pad - j) for j in range(FFN_CONV))
                dup_ref[s, i * rows_out:(i + 1) * rows_out, :] = dx[inner].astype(BF16)
                dc_in = dc[s][inner]
                db[s] = db[s] + jnp.sum(dc_in, axis=0, keepdims=True)
                for j in range(FFN_CONV):
                    dw[s][j] = dw[s][j] + jnp.sum(dc_in * shift(x[s], j - pad)[inner], axis=0, keepdims=True)
        dbg_ref[...] = db[0]
        dbv_ref[...] = db[1]
        for j in range(FFN_CONV):
            dwg_ref[j:j + 1, :] = dw[0][j]
            dwv_ref[j:j + 1, :] = dw[1][j]

    col = lambda j: (0, j)
    outs = pl.pallas_call(
        body, name="ffn_act_bwd", grid=(nt,),
        in_specs=[pl.BlockSpec(b, m) for _, b, m in ins],
        out_specs=[pl.BlockSpec((2, SEQ, tc), lambda j: (0, 0, j)),
                   pl.BlockSpec((FFN_CONV, tc), col), pl.BlockSpec((FFN_CONV, tc), col),
                   pl.BlockSpec((1, tc), col), pl.BlockSpec((1, tc), col)],
        out_shape=[jax.ShapeDtypeStruct((2, SEQ, D_FF), BF16),
                   jax.ShapeDtypeStruct((FFN_CONV, D_FF), F32), jax.ShapeDtypeStruct((FFN_CONV, D_FF), F32),
                   jax.ShapeDtypeStruct((1, D_FF), F32), jax.ShapeDtypeStruct((1, D_FF), F32)],
        compiler_params=_params(("parallel",)),
    )(*[a for a, _, _ in ins])
    dup, dwg, dwv, dbg, dbv = outs
    return dup, jnp.concatenate([dwg, dwv], axis=1), jnp.concatenate([dbg, dbv], axis=1)


def _pad_lanes(v):
    return jnp.pad(v.reshape(1, -1), ((0, 0), (0, LANES - v.size)))


def _layer_fwd(x, lw, late=None):
    h = _rms_fwd(x, lw['norm_mix_g'], "rms_mix_fwd")
    p = _matmul(h, lw['w_all'], 'nn', F32, "mm_in")
    if late is not None:
        lw = dict(lw, **late(p))
    gates = _gate_fwd(p, lw['a_log'], lw['dt_bias'])
    q, k, v = _qkv_fwd(p, lw['qkv_conv_w'])
    u, w, qd, kd, qk, dec = _gdn_prep_fwd(q, k, v, gates)
    o0, o1, st0, st1 = _gdn_scan_fwd(u, w, qd, kd, qk, dec)
    ya = _gdn_post_fwd(o0, o1, p, lw['gdn_norm_g'])
    yb = _sgu_fwd(p, lw['sgu_ln_g'], lw['sgu_ln_b'], lw['sgu_w'], lw['sgu_bt'])
    ta = _matmul(ya, lw['w_branch_a'], 'nn', F32, "mm_branch_a", b_split=N_CHIPS)
    tb = _matmul(yb, lw['w_branch_b'], 'nn', F32, "mm_branch_b", b_split=N_CHIPS)
    m = _merge_fwd(p, ta, tb)
    x1 = _matmul(m, lw['w_out'], 'nn', F32, "mm_out", add=x)
    h2 = _rms_fwd(x1, lw['norm_ffn_g'], "rms_ffn_fwd")
    up = _matmul(h2, lw['w_up'], 'nn', F32, "mm_up", b_split=N_CHIPS, o_split=2)
    act = _ffn_act_fwd(up, lw['ffn_conv_w'], lw['ffn_conv_b'])
    x2 = _matmul(act, lw['w_down'], 'nn', F32, "mm_down", add=x1)
    saved = dict(x=x, h=h, p=p, gates=gates, q=q, k=k, v=v, u=u, w=w, qd=qd, kd=kd, qk=qk, dec=dec,
                 o0=o0, o1=o1, st0=st0, st1=st1, ya=ya, yb=yb, ta=ta, tb=tb, m=m, x1=x1, h2=h2, up=up, act=act)
    return x2, saved, lw


def _layer_bwd(dx, dx_bf, lw, s, after=(), mid=None, last=None):
    g = {}
    dact = _matmul(dx_bf, lw['w_down'], 'nt', BF16, "mm_down_dgrad", after=after)
    g['w_down'] = _matmul(s['act'], dx_bf, 'tn', PARTIAL, "mm_down_wgrad")
    dup, g['ffn_conv_w'], g['ffn_conv_b'] = _ffn_act_bwd(s['up'], lw['ffn_conv_w'], lw['ffn_conv_b'], dact)
    dh2 = _matmul(dup, lw['w_up'], 'nt', F32, "mm_up_dgrad", a_split=2, b_split=N_CHIPS)
    g['w_up'] = _matmul(s['h2'], dup, 'tn', PARTIAL, "mm_up_wgrad", b_split=2, o_split=N_CHIPS)
    dx1, dx1_bf, g['norm_ffn_g'] = _rms_bwd(s['x1'], lw['norm_ffn_g'], dh2, dx, "rms_ffn_bwd")
    after_mid = mid(g, dh2) if mid is not None else ()
    dm = _matmul(dx1_bf, lw['w_out'], 'nt', F32, "mm_out_dgrad", after=after_mid)
    g['w_out'] = _matmul(s['m'], dx1_bf, 'tn', PARTIAL, "mm_out_wgrad")
    d_ga, d_gb, d_ta, d_tb = _merge_bwd(s['p'], s['ta'], s['tb'], dm)
    dya = _matmul(d_ta, lw['w_branch_a'], 'nt', BF16, "mm_branch_a_dgrad", b_split=N_CHIPS)
    dyb = _matmul(d_tb, lw['w_branch_b'], 'nt', BF16, "mm_branch_b_dgrad", b_split=N_CHIPS)
    g['w_branch_a'] = _matmul(s['ya'], d_ta, 'tn', PARTIAL, "mm_branch_a_wgrad", o_split=N_CHIPS)
    g['w_branch_b'] = _matmul(s['yb'], d_tb, 'tn', PARTIAL, "mm_branch_b_wgrad", o_split=N_CHIPS)
    du_s, dv_s, g['sgu_ln_g'], g['sgu_ln_b'], g['sgu_w'], g['sgu_bt'] = _sgu_bwd(
        s['p'], lw['sgu_ln_g'], lw['sgu_ln_b'], lw['sgu_w'], lw['sgu_bt'], dyb)
    do, dz, g['gdn_norm_g'] = _gdn_post_bwd(s['o0'], s['o1'], s['p'], lw['gdn_norm_g'], dya)
    cts = _gdn_scan_bwd(s['u'], s['w'], s['qd'], s['kd'], s['qk'], s['dec'], s['st0'], s['st1'], do)
    dq, dk, dv, dgates = _gdn_prep_bwd(s['q'], s['k'], s['v'], s['gates'], cts)
    (dxq, dxk, dxv), g['qkv_conv_w'] = _qkv_bwd(s['p'], lw['qkv_conv_w'], dq, dk, dv)
    d_ab, g['a_log'], g['dt_bias'] = _gate_bwd(s['p'], lw['a_log'], lw['dt_bias'], dgates)
    dp = jnp.concatenate([dxq, dxk, dxv, dz, du_s, dv_s, d_ga, d_gb, d_ab], axis=1)
    g['w_all'] = _matmul(s['h'], dp, 'tn', PARTIAL, "mm_in_wgrad")
    after_wgrads = last(g, dp) if last is not None else ()
    dh = _matmul(dp, lw['w_all'], 'nt', F32, "mm_in_dgrad", after=after_wgrads)
    dx0, dx0_bf, g['norm_mix_g'] = _rms_bwd(s['x'], lw['norm_mix_g'], dh, dx1, "rms_mix_bwd")
    return dx0, dx0_bf, g


def _w_in_segments():
    n_ab = 4 * GDN_HEADS
    cut = 4 * _gw()
    n_main = cut + 2 * _sw() + 2 * D_MODEL
    return [(0, cut, 0), (cut, cut + n_ab, n_main), (cut + n_ab, n_main + n_ab, cut)]


def _w_all_from_chips(t):
    width = t.shape[-1]
    runs = sorted(_w_in_segments(), key=lambda s: s[2])
    pieces = []
    for lo, hi, _ in runs:
        for j in range(N_CHIPS):
            a, b = max(lo, j * width), min(hi, (j + 1) * width)
            if a < b:
                pieces.append(t[j][:, a - j * width:b - j * width])
    pieces.append(jnp.zeros((t.shape[1], AB_PAD - 4 * GDN_HEADS), t.dtype))
    return jnp.concatenate(pieces, axis=1)


def _w_in_grad_chips(g_all):
    segs = _w_in_segments()
    width = segs[-1][1] // N_CHIPS
    chips = []
    for j in range(N_CHIPS):
        pieces = []
        for lo, hi, start in segs:
            a, b = max(lo, j * width), min(hi, (j + 1) * width)
            if a < b:
                pieces.append(g_all[:, start + a - lo:start + b - lo])
        chips.append(jnp.concatenate(pieces, axis=1))
    return jnp.stack(chips)


_FIRST = ('w_in', 'qkv_conv_w')
_REST = ('w_branch_a', 'w_branch_b', 'w_out', 'w_up', 'w_down', 'ffn_conv_w')


def _cat_cols(t):
    return jnp.concatenate([t[j] for j in range(N_CHIPS)], axis=-1)


def _layer_weights_rest(big):
    return dict(
        w_branch_a=big['w_branch_a'], w_branch_b=big['w_branch_b'], w_up=big['w_up'],
        w_out=big['w_out'].reshape(D_MODEL, D_MODEL), w_down=big['w_down'].reshape(D_FF, D_MODEL),
        ffn_conv_w=_cat_cols(big['ffn_conv_w']))


def _layer_weights_first(l, big, small):
    return dict(
        w_all=_w_all_from_chips(big['w_in']), qkv_conv_w=_cat_cols(big['qkv_conv_w']),
        norm_mix_g=small['norm_mix_g'][l:l + 1], norm_ffn_g=small['norm_ffn_g'][l:l + 1],
        a_log=_pad_lanes(small['a_log'][l]), dt_bias=_pad_lanes(small['dt_bias'][l]),
        gdn_norm_g=small['gdn_norm_g'][l:l + 1],
        sgu_ln_g=small['sgu_ln_g'][l:l + 1], sgu_ln_b=small['sgu_ln_b'][l:l + 1], sgu_w=small['sgu_w'][l],
        sgu_bt=jnp.pad(small['sgu_b'][l].T, ((0, 0), (0, LANES - SGU_GROUPS))),
        ffn_conv_b=small['ffn_conv_b'][l:l + 1])


_SMALL_GRADS = ('norm_mix_g', 'a_log', 'dt_bias', 'gdn_norm_g', 'sgu_ln_g', 'sgu_ln_b', 'sgu_w', 'sgu_bt',
                'norm_ffn_g', 'ffn_conv_b', 'qkv_conv_w', 'ffn_conv_w')
_BIG = ('w_in', 'w_branch_a', 'w_branch_b', 'w_out', 'w_up', 'w_down')


def _big_grad_slices(g, names=_BIG):
    def one(k):
        if k == 'w_in':
            return _w_in_grad_chips(g['w_all'])
        if k == 'w_out':
            return g[k].reshape(N_CHIPS, D_MODEL // N_CHIPS, D_MODEL)
        if k == 'w_down':
            return g[k].reshape(N_CHIPS, D_FF // N_CHIPS, D_MODEL)
        return g[k]
    return {k: one(k) for k in names}


def _adamw(w, g, m, v, name, after=(), pass_g=False):
    shape = w.shape
    cols = shape[-1]
    rows = w.size // cols
    tr = _pick(rows, max(8, (1 << 18) // cols // 8 * 8), 8) if rows % 8 == 0 else rows

    def fn(wv, gv, mv, vv):
        m2 = ADAM_B1 * mv + (1.0 - ADAM_B1) * gv
        v2 = ADAM_B2 * vv + (1.0 - ADAM_B2) * (gv * gv)
        m_hat = m2 / (1.0 - ADAM_B1 ** ADAM_STEP)
        v_hat = v2 / (1.0 - ADAM_B2 ** ADAM_STEP)
        delta = -ADAM_LR * (m_hat / (jnp.sqrt(v_hat) + ADAM_EPS) + ADAM_WD * wv)
        return (delta, m2, v2, gv) if pass_g else (delta, m2, v2)

    row = lambda i: (i, 0)
    outs = _tiled(fn, name, (rows // tr,),
                  [(t.reshape(rows, cols), (tr, cols), row) for t in (w, g, m, v)],
                  [((rows, cols), F32, (tr, cols), row, ())] * (4 if pass_g else 3), sem=("parallel",),
                  after=after)
    return [o.reshape(shape) for o in outs]


MESH_IDS = pl.DeviceIdType.MESH
ANY = pl.BlockSpec(memory_space=pl.ANY)


def _place():
    x, y, c = lax.axis_index("x"), lax.axis_index("y"), lax.axis_index("c")
    chips = [(1 - x, y), (x, 1 - y), (1 - x, 1 - y)]
    return x, y, c, 2 * x + y, chips


def _chip_index():
    return 2 * lax.axis_index("x") + lax.axis_index("y")


HBM = pl.BlockSpec(memory_space=pltpu.HBM)
SEM = pl.BlockSpec(memory_space=pltpu.SEMAPHORE)
DATAFLOW = pltpu.SideEffectType.DATAFLOW_SIDE_EFFECTING
TOKEN = jax.ShapeDtypeStruct((8, LANES), F32)


def _in_hbm(t):
    return pltpu.with_memory_space_constraint(t, pltpu.HBM)


def _place_shard(w, l, dtype, name, after=()):
    _, r, cols = w.shape
    tr = _pick(r, max(16, (1 << 18) // cols // 16 * 16), 16) if r % 16 == 0 else r

    def body(w_ref, *rest):
        rest[-1][...] = w_ref[...].astype(dtype)

    return pl.pallas_call(
        body, name=name, grid=(r // tr,),
        in_specs=[pl.BlockSpec((None, tr, cols), lambda i: (l, i, 0))] + [ANY] * len(after),
        out_specs=pl.BlockSpec((None, tr, cols), lambda i: (_chip_index(), i, 0)),
        out_shape=jax.ShapeDtypeStruct((N_CHIPS, r, cols), dtype),
        compiler_params=_params(("parallel",)),
    )(w, *after)


def _my_rows(ref_or_shape_rows, c):
    r = ref_or_shape_rows
    if r % 32 == 0:
        return pl.ds(c * (r // 2), r // 2), pl.ds((1 - c) * (r // 2), r // 2), True
    return pl.ds(0, r), pl.ds(0, r), False


def _gather_copies(bufs, send_sems, recv_sems):
    x, y, c, me, chips = _place()
    out = []
    for i, buf in enumerate(bufs):
        mine, _, _ = _my_rows(buf.shape[1], c)
        for j, (cx, cy) in enumerate(chips):
            def rcopy(slab, i=i, j=j, cx=cx, cy=cy):
                return pltpu.make_async_remote_copy(src_ref=slab, dst_ref=slab, send_sem=send_sems.at[3 * i + j],
                                                    recv_sem=recv_sems.at[3 * i + j], device_id=(cx, cy, c),
                                                    device_id_type=MESH_IDS)
            out.append((rcopy(buf.at[me, mine]), rcopy(buf.at[2 * cx + cy, mine])))
    return out


def _gather_start(xs, name, after=()):
    n = len(xs)
    n_after = len(after)

    def body(*refs):
        refs = refs[n + n_after:]
        send_sems, recv_sems = refs[0], refs[1]
        bufs = refs[2:n + 2]
        token = refs[n + 2]
        for out_going, _ in _gather_copies(bufs, send_sems, recv_sems):
            out_going.start()
        token[...] = jnp.zeros_like(token)

    res = pl.pallas_call(
        body, name=name, in_specs=[HBM] * n + [ANY] * n_after,
        out_specs=[SEM, SEM] + [HBM] * n + [pl.BlockSpec(memory_space=pltpu.VMEM)],
        out_shape=[pltpu.SemaphoreType.DMA((3 * n,)), pltpu.SemaphoreType.DMA((3 * n,))]
        + [pltpu.HBM(t.shape, t.dtype) for t in xs] + [TOKEN],
        input_output_aliases={i: i + 2 for i in range(n)},
        compiler_params=pltpu.CompilerParams(has_side_effects=DATAFLOW),
    )(*[_in_hbm(t) for t in xs], *after)
    return res[0], res[1], res[2:2 + n], res[2 + n]


def _gather_wait(bufs, send_sems, recv_sems, after, name):
    n = len(bufs)

    def body(*refs):
        b_refs = refs[:n]
        s_sems, r_sems = refs[n], refs[n + 1]
        for out_going, in_coming in _gather_copies(b_refs, s_sems, r_sems):
            out_going.wait_send()
            in_coming.wait_recv()

    return pl.pallas_call(
        body, name=name, in_specs=[HBM] * n + [SEM, SEM, ANY], out_specs=[HBM] * n,
        out_shape=[pltpu.HBM(t.shape, t.dtype) for t in bufs],
        input_output_aliases={i: i for i in range(n)},
        compiler_params=pltpu.CompilerParams(has_side_effects=DATAFLOW),
    )(*bufs, send_sems, recv_sems, after)


def _gather_forward(bufs, name):
    idx = [i for i, t in enumerate(bufs) if t.shape[1] % 32 == 0]
    xs = [bufs[i] for i in idx]
    n = len(xs)

    def body(*refs):
        o_refs = refs[n:2 * n]
        send_sems, recv_sems = refs[2 * n:]
        x, y, c, _, chips = _place()
        copies = []
        for i in range(n):
            mine, theirs, _ = _my_rows(xs[i].shape[1], c)
            for j, (cx, cy) in enumerate(chips):
                def rcopy(slab, i=i, j=j):
                    return pltpu.make_async_remote_copy(src_ref=slab, dst_ref=slab, send_sem=send_sems.at[3 * i + j],
                                                        recv_sem=recv_sems.at[3 * i + j], device_id=(x, y, 1 - c),
                                                        device_id_type=MESH_IDS)
                copies.append((rcopy(o_refs[i].at[2 * cx + cy, mine]), rcopy(o_refs[i].at[2 * cx + cy, theirs])))
                copies[-1][0].start()
        for out_going, in_coming in copies:
            out_going.wait_send()
            in_coming.wait_recv()

    res = pl.pallas_call(
        body, name=name, in_specs=[ANY] * n, out_specs=[ANY] * n,
        out_shape=[jax.ShapeDtypeStruct(t.shape, t.dtype) for t in xs],
        input_output_aliases={i: i for i in range(n)},
        scratch_shapes=[pltpu.SemaphoreType.DMA((3 * n,)), pltpu.SemaphoreType.DMA((3 * n,))],
        compiler_params=pltpu.CompilerParams(has_side_effects=True),
    )(*xs)
    out = list(bufs)
    for i, t in zip(idx, res):
        out[i] = t
    return out


def _half_tile(rh, cols):
    return _pick(rh, max(16, (1 << 18) // cols // 16 * 16), 16)


N_LAND = 7


def _scatter_copies(p_refs, r_refs, send_sems, recv_sems):
    x, y, c, me, chips = _place()
    outgoing, incoming = [], []
    for i, (p, land) in enumerate(zip(p_refs, r_refs)):
        rh = p.shape[1] // 2

        def copy(src, slot, k_send, k_recv, to, i=i, land=land):
            return pltpu.make_async_remote_copy(
                src_ref=src, dst_ref=land.at[slot], send_sem=send_sems.at[N_LAND * i + k_send],
                recv_sem=recv_sems.at[N_LAND * i + k_recv], device_id=to, device_id_type=MESH_IDS)

        for r, (cx, cy) in enumerate(chips):
            for k in range(2):
                outgoing.append(copy(p.at[2 * cx + cy, pl.ds(k * rh, rh)], 2 * r + c, 2 * r + k, 2 * r + c,
                                     (cx, cy, k)))
                incoming.append(copy(land.at[2 * r + k], 2 * r + k, 2 * r + k, 2 * r + k, (x, y, c)))
        outgoing.append(copy(p.at[me, pl.ds((1 - c) * rh, rh)], 6, 6, 6, (x, y, 1 - c)))
        incoming.append(copy(land.at[6], 6, 6, 6, (x, y, c)))
    return outgoing, incoming


def _scatter_start(ps, name):
    n = len(ps)
    lands = [lax.empty((N_LAND, t.shape[1] // 2, t.shape[2]), t.dtype) for t in ps]

    def body(*refs):
        send_sems, recv_sems = refs[2 * n], refs[2 * n + 1]
        p_refs = refs[2 * n + 2:3 * n + 2]
        r_refs = refs[3 * n + 2:4 * n + 2]
        token = refs[4 * n + 2]
        for cp in _scatter_copies(p_refs, r_refs, send_sems, recv_sems)[0]:
            cp.start()
        token[...] = jnp.zeros_like(token)

    res = pl.pallas_call(
        body, name=name, in_specs=[HBM] * (2 * n),
        out_specs=[SEM, SEM] + [HBM] * (2 * n) + [pl.BlockSpec(memory_space=pltpu.VMEM)],
        out_shape=[pltpu.SemaphoreType.DMA((N_LAND * n,)), pltpu.SemaphoreType.DMA((N_LAND * n,))]
        + [pltpu.HBM(t.shape, t.dtype) for t in list(ps) + lands] + [TOKEN],
        input_output_aliases={i: i + 2 for i in range(2 * n)},
        compiler_params=pltpu.CompilerParams(has_side_effects=DATAFLOW),
    )(*[_in_hbm(t) for t in list(ps) + lands])
    return res[0], res[1], res[2:2 + n], res[2 + n:2 + 2 * n], res[2 + 2 * n]


def _scatter_wait(ps, lands, send_sems, recv_sems, after, name):
    n = len(ps)
    after = tuple(after) if isinstance(after, (tuple, list)) else (after,)

    def body(*refs):
        p_refs, r_refs = refs[:n], refs[n:2 * n]
        s_sems, r_sems = refs[2 * n], refs[2 * n + 1]
        outgoing, incoming = _scatter_copies(p_refs, r_refs, s_sems, r_sems)
        for cp in outgoing:
            cp.wait_send()
        for cp in incoming:
            cp.wait_recv()

    res = pl.pallas_call(
        body, name=name, in_specs=[HBM] * (2 * n) + [SEM, SEM] + [ANY] * len(after), out_specs=[HBM] * (2 * n),
        out_shape=[pltpu.HBM(t.shape, t.dtype) for t in list(ps) + list(lands)],
        input_output_aliases={i: i for i in range(2 * n)},
        compiler_params=pltpu.CompilerParams(has_side_effects=DATAFLOW),
    )(*ps, *lands, send_sems, recv_sems, *after)
    return res[:n], res[n:]


def _reduce_own(p, rcv, acc, l, name):
    nchip, r, cols = p.shape
    rh = r // 2
    tr = _half_tile(rh, cols)
    nt = rh // tr

    def body(*refs):
        p_ref, r_ref = refs[:2]
        o_ref = refs[-1]
        total = p_ref[...].astype(F32)
        for j in range(N_LAND):
            total = total + r_ref[j].astype(F32)
        o_ref[...] = total

    mine = lambda i: lax.axis_index("c") * nt + i
    in_specs = [pl.BlockSpec((None, tr, cols), lambda i: (_chip_index(), mine(i), 0)),
                pl.BlockSpec((N_LAND, tr, cols), lambda i: (0, i, 0))]
    args = [p, rcv]
    aliases = {}
    if acc is not None:
        in_specs.append(ANY)
        args.append(acc)
        aliases = {2: 0}
    return pl.pallas_call(
        body, name=name, grid=(nt,), in_specs=in_specs,
        out_specs=pl.BlockSpec((None, tr, cols), lambda i: (l, mine(i), 0)),
        out_shape=jax.ShapeDtypeStruct((DEPTH, r, cols), F32), input_output_aliases=aliases,
        compiler_params=_params(("parallel",)),
    )(*args)


def _share_halves(fs, l):
    n = len(fs)

    def body(*refs):
        o_refs = refs[n:2 * n]
        send_sems, recv_sems = refs[2 * n:]
        x, y, c, _, _ = _place()

        def halves(i):
            rh = fs[i].shape[1] // 2
            return o_refs[i].at[l, pl.ds(c * rh, rh), :], o_refs[i].at[l, pl.ds((1 - c) * rh, rh), :]

        def copy(i, rows):
            return pltpu.make_async_remote_copy(src_ref=rows, dst_ref=rows, send_sem=send_sems.at[i],
                                                recv_sem=recv_sems.at[i], device_id=(x, y, 1 - c),
                                                device_id_type=MESH_IDS)

        for i in range(n):
            copy(i, halves(i)[0]).start()
        for i in range(n):
            mine, theirs = halves(i)
            copy(i, mine).wait_send()
            copy(i, theirs).wait_recv()

    return pl.pallas_call(
        body, name="grad_share_halves", in_specs=[ANY] * n, out_specs=[ANY] * n,
        out_shape=[jax.ShapeDtypeStruct(t.shape, t.dtype) for t in fs],
        input_output_aliases={i: i for i in range(n)},
        scratch_shapes=[pltpu.SemaphoreType.DMA((n,)), pltpu.SemaphoreType.DMA((n,))],
        compiler_params=pltpu.CompilerParams(has_side_effects=True),
    )(*fs)


def _reduce_scatter_begin(slices, l, tag):
    names = list(slices)
    send_sems, recv_sems, ps, lands, token = _scatter_start([slices[k] for k in names],
                                                            "grad_scatter_start_%s%d" % (tag, l))
    return dict(names=names, ps=ps, lands=lands, sems=(send_sems, recv_sems), token=token, l=l, tag=tag)


def _reduce_scatter_end(st, stacked, after):
    l = st['l']
    ps, rs = _scatter_wait(st['ps'], st['lands'], *st['sems'], after, "grad_scatter_wait_%s%d" % (st['tag'], l))
    fs = [_reduce_own(p, r, stacked.get(k), l, "grad_reduce_own_" + k)
          for k, p, r in zip(st['names'], ps, rs)]
    stacked.update(zip(st['names'], _share_halves(fs, l)))


def _all_reduce_small(packed, after=()):
    rows = packed.shape[0]
    rh = rows // 2
    tr = _pick(rh, 512, 8)

    def body(x_ref, *rest):
        o_ref, sib, chip_sums, send_sems, recv_sems = rest[len(after):]
        x, y, c, me, chips = _place()
        sibling = (x, y, 1 - c)
        mine, theirs = pl.ds(c * rh, rh), pl.ds((1 - c) * rh, rh)

        def copy(src, dst, k, to):
            return pltpu.make_async_remote_copy(src_ref=src, dst_ref=dst, send_sem=send_sems.at[k],
                                                recv_sem=recv_sems.at[k], device_id=to, device_id_type=MESH_IDS)

        def tiles(fn):
            @pl.loop(0, rh // tr)
            def _(t):
                fn(pl.ds(pl.multiple_of(t * tr, 8), tr))

        to_sibling = copy(x_ref.at[theirs], sib, 0, sibling)
        to_sibling.start()
        to_sibling.wait()

        def chip_sum(sl):
            chip_sums[me, sl, :] = x_ref.at[mine][sl, :] + sib[sl, :]
        tiles(chip_sum)

        out = [copy(chip_sums.at[me], chip_sums.at[me], 1 + j, (cx, cy, c)) for j, (cx, cy) in enumerate(chips)]
        for cp in out:
            cp.start()
        for j, (cx, cy) in enumerate(chips):
            copy(chip_sums.at[2 * cx + cy], chip_sums.at[2 * cx + cy], 1 + j, (x, y, c)).wait_recv()
        for cp in out:
            cp.wait_send()

        def total(sl):
            acc = chip_sums[0, sl, :]
            for s in range(1, N_CHIPS):
                acc = acc + chip_sums[s, sl, :]
            o_ref.at[mine][sl, :] = acc
        tiles(total)

        share = copy(o_ref.at[mine], o_ref.at[mine], 4, sibling)
        share.start()
        share.wait_send()
        copy(o_ref.at[theirs], o_ref.at[theirs], 4, sibling).wait_recv()

    vm = pl.BlockSpec(memory_space=pltpu.VMEM)
    return pl.pallas_call(
        body, name="all_reduce_small", in_specs=[vm] + [ANY] * len(after), out_specs=vm,
        out_shape=jax.ShapeDtypeStruct(packed.shape, F32),
        scratch_shapes=[pltpu.VMEM((rh, LANES), F32), pltpu.VMEM((N_CHIPS, rh, LANES), F32),
                        pltpu.SemaphoreType.DMA((5,)), pltpu.SemaphoreType.DMA((5,))],
        compiler_params=pltpu.CompilerParams(vmem_limit_bytes=VMEM_LIMIT, has_side_effects=True),
    )(packed, *after)


_WEIGHTS = ('norm_mix_g', 'w_in', 'qkv_conv_w', 'a_log', 'dt_bias', 'gdn_norm_g', 'w_branch_a', 'sgu_ln_g',
            'sgu_ln_b', 'sgu_w', 'sgu_b', 'w_branch_b', 'w_out', 'norm_ffn_g', 'w_up', 'ffn_conv_w', 'ffn_conv_b',
            'w_down', 'final_norm_g')


def _local_step(x, target, weights_of, final_norm_g, bwd_after=None, mid_layer=None, last_layer=None):
    lws, saves = [], []
    for l in range(DEPTH):
        lw, late = weights_of(l, x)
        x, s, lw = _layer_fwd(x, lw, late)
        lws.append(lw)
        saves.append(s)
    loss, dx, dx_bf, d_final = _loss_head(x, final_norm_g.reshape(1, -1), target)
    grads = [None] * DEPTH
    for l in reversed(range(DEPTH)):
        after = bwd_after(l) if bwd_after is not None else ()
        mid = (lambda g, dh2, l=l: mid_layer(l, g, dh2)) if mid_layer is not None else None
        last = (lambda g, dp, l=l: last_layer(l, g, dp)) if last_layer is not None else None
        dx, dx_bf, grads[l] = _layer_bwd(dx, dx_bf, lws[l], saves[l], after, mid, last)
    return loss, dx, grads, d_final


def _pack_small(grads, d_final):
    parts = [grads[l][k].reshape(-1) for l in range(DEPTH) for k in _SMALL_GRADS] + [d_final.reshape(-1)]
    flat = jnp.concatenate(parts)
    rows = -(-flat.size // (16 * LANES)) * 16
    return jnp.pad(flat, (0, rows * LANES - flat.size)).reshape(rows, LANES), [p.size for p in parts]


def _unpack_small(packed, grads, d_final):
    flat = packed.reshape(-1)
    out, off = [], 0
    for l in range(DEPTH):
        d = {}
        for k in _SMALL_GRADS:
            t = grads[l][k]
            d[k] = flat[off:off + t.size].reshape(t.shape)
            off += t.size
        out.append(d)
    return out, flat[off:off + d_final.size].reshape(d_final.shape)


def kernel(x, norm_mix_g, w_in, qkv_conv_w, a_log, dt_bias, gdn_norm_g, w_branch_a, sgu_ln_g, sgu_ln_b, sgu_w, sgu_b, w_branch_b, w_out, norm_ffn_g, w_up, ffn_conv_w, ffn_conv_b, w_down, final_norm_g, loss_target, m_norm_mix_g, m_w_in, m_qkv_conv_w, m_a_log, m_dt_bias, m_gdn_norm_g, m_w_branch_a, m_sgu_ln_g, m_sgu_ln_b, m_sgu_w, m_sgu_b, m_w_branch_b, m_w_out, m_norm_ffn_g, m_w_up, m_ffn_conv_w, m_ffn_conv_b, m_w_down, m_final_norm_g, v_norm_mix_g, v_w_in, v_qkv_conv_w, v_a_log, v_dt_bias, v_gdn_norm_g, v_w_branch_a, v_sgu_ln_g, v_sgu_ln_b, v_sgu_w, v_sgu_b, v_w_branch_b, v_w_out, v_norm_ffn_g, v_w_up, v_ffn_conv_w, v_ffn_conv_b, v_w_down, v_final_norm_g):
    w = dict(norm_mix_g=norm_mix_g, w_in=w_in, qkv_conv_w=qkv_conv_w, a_log=a_log, dt_bias=dt_bias,
             gdn_norm_g=gdn_norm_g, w_branch_a=w_branch_a, sgu_ln_g=sgu_ln_g, sgu_ln_b=sgu_ln_b, sgu_w=sgu_w,
             sgu_b=sgu_b, w_branch_b=w_branch_b, w_out=w_out, norm_ffn_g=norm_ffn_g, w_up=w_up,
             ffn_conv_w=ffn_conv_w, ffn_conv_b=ffn_conv_b, w_down=w_down, final_norm_g=final_norm_g)
    m = dict(norm_mix_g=m_norm_mix_g, w_in=m_w_in, qkv_conv_w=m_qkv_conv_w, a_log=m_a_log, dt_bias=m_dt_bias,
             gdn_norm_g=m_gdn_norm_g, w_branch_a=m_w_branch_a, sgu_ln_g=m_sgu_ln_g, sgu_ln_b=m_sgu_ln_b,
             sgu_w=m_sgu_w, sgu_b=m_sgu_b, w_branch_b=m_w_branch_b, w_out=m_w_out, norm_ffn_g=m_norm_ffn_g,
             w_up=m_w_up, ffn_conv_w=m_ffn_conv_w, ffn_conv_b=m_ffn_conv_b, w_down=m_w_down,
             final_norm_g=m_final_norm_g)
    v = dict(norm_mix_g=v_norm_mix_g, w_in=v_w_in, qkv_conv_w=v_qkv_conv_w, a_log=v_a_log, dt_bias=v_dt_bias,
             gdn_norm_g=v_gdn_norm_g, w_branch_a=v_w_branch_a, sgu_ln_g=v_sgu_ln_g, sgu_ln_b=v_sgu_ln_b,
             sgu_w=v_sgu_w, sgu_b=v_sgu_b, w_branch_b=v_w_branch_b, w_out=v_w_out, norm_ffn_g=v_norm_ffn_g,
             w_up=v_w_up, ffn_conv_w=v_ffn_conv_w, ffn_conv_b=v_ffn_conv_b, w_down=v_w_down,
             final_norm_g=v_final_norm_g)
    chip = _chip_index()

    in_flight = []
    for l in range(DEPTH):
        for tag, names in (("first", _FIRST), ("rest", _REST)):
            earlier = (in_flight[-1][3],) if in_flight else ()
            placed = [_place_shard(w[k], l, BF16 if k in _BIG else F32, "place_" + k, earlier) for k in names]
            in_flight.append(_gather_start(placed, "gather_start_%s_%d" % (tag, l), earlier))

    def arrived(l, tag, names, after):
        send_sems, recv_sems, bufs, _ = in_flight[2 * l + (tag == "rest")]
        bufs = _gather_wait(bufs, send_sems, recv_sems, after, "gather_wait_%s_%d" % (tag, l))
        return dict(zip(names, _gather_forward(bufs, "gather_forward_" + tag)))

    def weights_of(l, x_in):
        first = arrived(l, "first", _FIRST, in_flight[-1][3] if l == 0 else x_in)
        return (_layer_weights_first(l, first, w),
                lambda p: _layer_weights_rest(arrived(l, "rest", _REST, p)))

    group_a, group_b = ('w_up', 'w_down'), ('w_in', 'w_branch_a', 'w_branch_b', 'w_out')
    big_g = {}
    pending = {'a': None, 'b': None}

    def finish(tag, after):
        if pending[tag] is not None:
            _reduce_scatter_end(pending[tag], big_g, after)
            pending[tag] = None

    def bwd_after(l):
        return () if pending['b'] is None else (pending['b']['token'],)

    def mid_layer(l, g, dh2):
        finish('b', dh2)
        pending['a'] = _reduce_scatter_begin(_big_grad_slices(g, group_a), l, 'a')
        return (pending['a']['token'],)

    def last_layer(l, g, dp):
        finish('a', dp)
        pending['b'] = _reduce_scatter_begin(_big_grad_slices(g, group_b), l, 'b')
        return (pending['b']['token'],)

    loss, grad_x, grads, d_final = _local_step(x[0], loss_target[0], weights_of, w['final_norm_g'], bwd_after,
                                               mid_layer, last_layer)
    loss = lax.psum(loss[0, 0], ("x", "y", "c"))
    packed, _ = _pack_small(grads, d_final)
    reduced_small = _all_reduce_small(packed, (pending['b']['token'],))
    small_g, d_final = _unpack_small(reduced_small, grads, d_final)
    deltas, new_m, new_v, g_out = {}, {}, {}, {}
    for k in group_a:
        deltas[k], new_m[k], new_v[k], g_out[k] = _adamw(w[k], big_g[k].reshape(w[k].shape), m[k], v[k],
                                                         "adamw_" + k, (pending['b']['token'],), pass_g=True)
    finish('b', (reduced_small,) + tuple(deltas[k] for k in group_a))

    def stack(k):
        return jnp.stack([small_g[l][k] for l in range(DEPTH)])

    nd = 2 * GDN_HEADS
    for k in ('norm_mix_g', 'gdn_norm_g', 'sgu_ln_g', 'sgu_ln_b', 'norm_ffn_g', 'ffn_conv_b'):
        g_out[k] = stack(k).reshape(w[k].shape)
    g_out['sgu_w'] = stack('sgu_w')
    g_out['a_log'] = stack('a_log')[:, 0, :nd].reshape(w['a_log'].shape)
    g_out['dt_bias'] = stack('dt_bias')[:, 0, :nd].reshape(w['dt_bias'].shape)
    g_out['sgu_b'] = jnp.swapaxes(stack('sgu_bt')[:, :, :SGU_GROUPS], 1, 2)
    for k in ('qkv_conv_w', 'ffn_conv_w'):
        full = stack(k)
        width = w[k].shape[-1]
        g_out[k] = lax.dynamic_slice_in_dim(full, chip * width, width, axis=2)
    g_out['final_norm_g'] = d_final.reshape(w['final_norm_g'].shape)

    for k in group_b:
        deltas[k], new_m[k], new_v[k], g_out[k] = _adamw(w[k], big_g[k].reshape(w[k].shape), m[k], v[k],
                                                         "adamw_" + k, pass_g=True)
    for k in _WEIGHTS:
        if k not in _BIG:
            deltas[k], new_m[k], new_v[k] = _adamw(w[k], g_out[k], m[k], v[k], "adamw_" + k)
    return (loss, grad_x[None], *[g_out[k] for k in _WEIGHTS], *[deltas[k] for k in _WEIGHTS],
            *[new_m[k] for k in _WEIGHTS], *[new_v[k] for k in _WEIGHTS])
```
